```python
import math
import jax
import jax.numpy as jnp
from jax import lax
import numpy as np


D_MODEL = 1024
BATCH = 2
SEQ = 16384
DEPTH = 2

GRID_W = 64
CTX_LEN = 256
BRANCH_W = D_MODEL // 2
N_BRANCH = 3
EPS = 1e-6
GLA_HEADS = 4
GLA_DV = BRANCH_W // GLA_HEADS
GLA_DK = GLA_DV // 2
GLA_KW = GLA_HEADS * GLA_DK
GLA_VW = GLA_HEADS * GLA_DV
GLA_RANK = 16
GLA_GATE_NORM = 16.0
GLA_CHUNK = 64
ATTN_HD = 64
ATTN_HEADS = BRANCH_W // ATTN_HD
ATTN_KV_HEADS = ATTN_HEADS // 4
ATTN_GROUP = ATTN_HEADS // ATTN_KV_HEADS
ATTN_QW = ATTN_HEADS * ATTN_HD
ATTN_KVW = ATTN_KV_HEADS * ATTN_HD
Q_BLOCK = 128
ROPE_THETA = 10000.0
HY_W = BRANCH_W
HY_ORDER = 2
HY_EMB = 33
HY_BANDS = (HY_EMB - 1) // 2
HY_FFN = 64
HY_SHORT = 3
HY_MOD_SHIFT = 0.05
HY_DECAY_SHORT_PCT = 0.3
HY_DECAY_LONG_PCT = 1.5
HY_DECAY_TARGET = 1e-2
N_FILT = 2 * HY_ORDER * HY_W
SPLITS = (GLA_KW, GLA_KW, GLA_VW, GLA_VW, GLA_RANK, GLA_RANK,
          ATTN_QW, ATTN_KVW, ATTN_KVW, ATTN_QW,
          (HY_ORDER + 1) * HY_W, HY_W,
          N_BRANCH * D_MODEL)
N_IN = sum(SPLITS)

kernel_name = "hybrid_gla_gqa_hyena_prefix_dit"


def rms_norm(a):
    af = a.astype(jnp.float32)
    return (af * lax.rsqrt(jnp.mean(af * af, axis=-1, keepdims=True) + EPS)).astype(a.dtype)


def heads(a, n):
    return a.reshape(a.shape[:-1] + (n, a.shape[-1] // n))


def rev(a):
    return a[:, ::-1]


def split_proj(p):
    idx = [int(i) for i in np.cumsum(SPLITS)[:-1]]
    return jnp.split(p, idx, axis=-1)


def axial_rope_angles(L):
    t = jnp.arange(L)
    row = (t // GRID_W).astype(jnp.float32)
    col = (t % GRID_W).astype(jnp.float32)
    half = ATTN_HD // 2
    inv = ROPE_THETA ** (-jnp.arange(0, half, 2, dtype=jnp.float32) / half)
    return jnp.concatenate([row[:, None] * inv, col[:, None] * inv], axis=-1)


def apply_rope(a, ang):
    af = a.astype(jnp.float32).reshape(a.shape[:-1] + (ATTN_HD // 2, 2))
    cos = jnp.cos(ang)[None, :, None, :]
    sin = jnp.sin(ang)[None, :, None, :]
    a0, a1 = af[..., 0], af[..., 1]
    out = jnp.stack([a0 * cos - a1 * sin, a0 * sin + a1 * cos], axis=-1)
    return out.reshape(a.shape).astype(a.dtype)


def gla_decays(af, ab, wa_f, ba_f, wa_b, ba_b):
    gf = jax.nn.log_sigmoid((af @ wa_f + ba_f).astype(jnp.float32)) / GLA_GATE_NORM
    gb = jax.nn.log_sigmoid((ab @ wa_b + ba_b).astype(jnp.float32)) / GLA_GATE_NORM
    return heads(gf, GLA_HEADS), heads(gb, GLA_HEADS)


def gla_scan(q, k, v, g, s0):
    b_, L, H, _ = q.shape
    dv = v.shape[-1]
    nc = L // GLA_CHUNK

    def chunks(a):
        return a.reshape(b_, nc, GLA_CHUNK, H, a.shape[-1]).transpose(1, 0, 3, 2, 4)

    mask = jnp.tril(jnp.ones((GLA_CHUNK, GLA_CHUNK), dtype=bool))[:, :, None]

    def step(s, inp):
        qc, kc, vc, gc = inp
        bcum = jnp.cumsum(gc, axis=2)
        blast = bcum[:, :, -1:, :]
        diff = bcum[:, :, :, None, :] - bcum[:, :, None, :, :]
        decay = jnp.exp(jnp.where(mask, diff, -jnp.inf))
        att = jnp.einsum('bhid,bhjd,bhijd->bhij', qc, kc, decay)
        o = (jnp.einsum('bhij,bhjv->bhiv', att, vc)
             + jnp.einsum('bhid,bhdv->bhiv', qc * jnp.exp(bcum), s))
        s_new = (jnp.exp(blast[:, :, 0, :])[..., None] * s
                 + jnp.einsum('bhjd,bhjv->bhdv', kc * jnp.exp(blast - bcum), vc))
        return s_new, o

    s_fin, o = lax.scan(step, s0, (chunks(q), chunks(k), chunks(v), chunks(g)))
    o = o.transpose(1, 0, 3, 2, 4).reshape(b_, L, H, dv)
    return o, s_fin


def gla_final_state(k, v, g):
    bcum = jnp.cumsum(g, axis=1)
    w = jnp.exp(bcum[:, -1:] - bcum)
    return jnp.einsum('blhd,blhv->bhdv', k * w, v)


def gla_out(o, norm_w, z):
    o = rms_norm(o) * norm_w.astype(jnp.float32)
    return o.reshape(o.shape[:2] + (GLA_VW,)).astype(z.dtype) * jax.nn.silu(z)


def attend(qg, keys, vals):
    s = jnp.einsum('bqgrd,bkgd->bgrqk', qg, keys).astype(jnp.float32) * (ATTN_HD ** -0.5)
    p = jax.nn.softmax(s, axis=-1).astype(vals.dtype)
    return jnp.einsum('bgrqk,bkgd->bqgrd', p, vals)


def attn_latent(q, k, v, k_ctx, v_ctx):
    b_, L = q.shape[0], q.shape[1]
    nb = L // Q_BLOCK
    keys = jnp.concatenate([k, k_ctx], axis=1)
    vals = jnp.concatenate([v, v_ctx], axis=1)
    qb = q.reshape(b_, nb, Q_BLOCK, ATTN_KV_HEADS, ATTN_GROUP, ATTN_HD).transpose(1, 0, 2, 3, 4, 5)
    o = lax.map(lambda qblk: attend(qblk, keys, vals), qb)
    return o.transpose(1, 0, 2, 3, 4, 5).reshape(b_, L, ATTN_QW)


def qk_norm(a, gain):
    return rms_norm(a) * gain


def short_conv(u, w, bias):
    L = u.shape[1]
    pad = HY_SHORT // 2
    up = jnp.pad(u, ((0, 0), (pad, pad), (0, 0)))
    out = bias
    for i in range(HY_SHORT):
        out = out + up[:, i:i + L] * w[i]
    return out


def hyena_filters(L, f1_w, f1_b, f1_freq, f2_w, f2_b, f2_freq, f3_w):
    f32 = jnp.float32
    t = jnp.linspace(0.0, 1.0, L, dtype=f32)[:, None]
    w = 2.0 * math.pi * jnp.arange(L, dtype=f32)[:, None] / L
    fr = jnp.linspace(1e-4, HY_BANDS - 1, HY_BANDS, dtype=f32)[None]
    emb = jnp.concatenate([t, jnp.cos(fr * w), -jnp.sin(fr * w)], axis=-1)
    h = jnp.sin(f1_freq.astype(f32) * (emb @ f1_w.astype(f32) + f1_b.astype(f32)))
    h = jnp.sin(f2_freq.astype(f32) * (h @ f2_w.astype(f32) + f2_b.astype(f32)))
    h = h @ f3_w.astype(f32)
    deltas = jnp.abs(jnp.linspace(math.log(HY_DECAY_TARGET) / HY_DECAY_SHORT_PCT,
                                  math.log(HY_DECAY_TARGET) / HY_DECAY_LONG_PCT, N_FILT, dtype=f32))
    h = h * (jnp.exp(-t * deltas) + HY_MOD_SHIFT)
    h = h / jnp.sum(jnp.abs(h), axis=0, keepdims=True)
    return h.reshape(L, HY_ORDER, 2, HY_W)


def bidir_long_conv(z, hf, hb):
    L = z.shape[1]
    filt = jnp.concatenate([hf, jnp.zeros_like(hf[:1]), hb[:0:-1]], axis=0)
    ff = jnp.fft.rfft(filt, axis=0)
    zf = jnp.fft.rfft(z, n=2 * L, axis=1)
    return jnp.fft.irfft(zf * ff[None], n=2 * L, axis=1)[:, :L]


def hyena_mix(u, conv_w, conv_b, f1_w, f1_b, f1_freq, f2_w, f2_b, f2_freq, f3_w, skip):
    L = u.shape[1]
    u = short_conv(u, conv_w, conv_b)
    v, x1, x2 = jnp.split(u, HY_ORDER + 1, axis=-1)
    h = hyena_filters(L, f1_w, f1_b, f1_freq, f2_w, f2_b, f2_freq, f3_w)
    z = v.astype(jnp.float32)
    for o, gt in enumerate((x1, x2)):
        z = gt.astype(jnp.float32) * (bidir_long_conv(z, h[:, o, 0], h[:, o, 1])
                                      + skip[o].astype(jnp.float32) * z)
    return z.astype(u.dtype)


def merge_branches(y_gla, y_attn, y_hy, mg, w_g, w_a, w_h, w_o):
    m = jax.nn.sigmoid(mg.astype(jnp.float32)).astype(y_gla.dtype)
    m_g, m_a, m_h = jnp.split(m, N_BRANCH, axis=-1)
    y = m_g * (y_gla @ w_g) + m_a * (y_attn @ w_a) + m_h * (y_hy @ w_h)
    return y @ w_o


def setup_inputs(seed: int = 0) -> dict:
    key = jax.random.key(seed)
    ks = jax.random.split(key, 32)
    f32 = jnp.float32
    D = D_MODEL

    def nrm(k, shape, scale):
        return scale * jax.random.normal(k, shape, f32)

    return {
        'x': nrm(ks[0], (BATCH, SEQ, D), 1.0),
        'c': nrm(ks[1], (BATCH, D), 1.0),
        'ctx': nrm(ks[2], (BATCH, CTX_LEN, D), 1.0),
        'c_ctx': nrm(ks[3], (D,), 1.0),
        'w_ada': nrm(ks[4], (DEPTH, D, 3 * D), 0.5 * D ** -0.5),
        'b_ada': nrm(ks[5], (DEPTH, 3 * D), 0.02),
        'w_in': nrm(ks[6], (DEPTH, D, N_IN), D ** -0.5),
        'gla_wa_f': nrm(ks[7], (DEPTH, GLA_RANK, GLA_KW), GLA_RANK ** -0.5),
        'gla_ba_f': nrm(ks[8], (DEPTH, GLA_KW), 0.1),
        'gla_wa_b': nrm(ks[9], (DEPTH, GLA_RANK, GLA_KW), GLA_RANK ** -0.5),
        'gla_ba_b': nrm(ks[10], (DEPTH, GLA_KW), 0.1),
        'gla_norm': 1.0 + nrm(ks[11], (DEPTH, GLA_DV), 0.02),
        'attn_qnorm': 1.0 + nrm(ks[12], (DEPTH, ATTN_HD), 0.02),
        'attn_knorm': 1.0 + nrm(ks[13], (DEPTH, ATTN_HD), 0.02),
        'hy_conv_w': nrm(ks[14], (DEPTH, HY_SHORT, (HY_ORDER + 1) * HY_W), HY_SHORT ** -0.5),
        'hy_conv_b': nrm(ks[15], (DEPTH, (HY_ORDER + 1) * HY_W), 0.02),
        'hy_f1_w': nrm(ks[16], (DEPTH, HY_EMB, HY_FFN), HY_EMB ** -0.5),
        'hy_f1_b': nrm(ks[17], (DEPTH, HY_FFN), 0.1),
        'hy_f1_freq': 1.0 + nrm(ks[18], (DEPTH, HY_FFN), 0.02),
        'hy_f2_w': nrm(ks[19], (DEPTH, HY_FFN, HY_FFN), HY_FFN ** -0.5),
        'hy_f2_b': nrm(ks[20], (DEPTH, HY_FFN), 0.1),
        'hy_f2_freq': 1.0 + nrm(ks[21], (DEPTH, HY_FFN), 0.02),
        'hy_f3_w': nrm(ks[22], (DEPTH, HY_FFN, N_FILT), HY_FFN ** -0.5),
        'hy_skip': nrm(ks[23], (DEPTH, HY_ORDER, HY_W), 0.5),
        'w_br_gla': nrm(ks[24], (DEPTH, BRANCH_W, D), BRANCH_W ** -0.5),
        'w_br_attn': nrm(ks[25], (DEPTH, BRANCH_W, D), BRANCH_W ** -0.5),
        'w_br_hy': nrm(ks[26], (DEPTH, BRANCH_W, D), BRANCH_W ** -0.5),
        'w_out': nrm(ks[27], (DEPTH, D, D), D ** -0.5),
        'final_norm': 1.0 + nrm(ks[28], (D,), 0.02),
    }


def reference(x, c, ctx, c_ctx, w_ada, b_ada, w_in, gla_wa_f, gla_ba_f, gla_wa_b, gla_ba_b,
              gla_norm, attn_qnorm, attn_knorm, hy_conv_w, hy_conv_b, hy_f1_w, hy_f1_b,
              hy_f1_freq, hy_f2_w, hy_f2_b, hy_f2_freq, hy_f3_w, hy_skip, w_br_gla, w_br_attn,
              w_br_hy, w_out, final_norm):
    b_ = x.shape[0]
    L = x.shape[1]
    Lc = ctx.shape[1]
    ang = axial_rope_angles(L)
    s_lat = jax.nn.silu(c)
    s_ctx = jax.nn.silu(c_ctx)
    zero_state = jnp.zeros((b_, GLA_HEADS, GLA_DK, GLA_DV), jnp.float32)
    gla_scale = GLA_DK ** -0.5

    for l in range(DEPTH):
        need_ctx = l < DEPTH - 1
        shift, scale, gate = jnp.split(s_lat @ w_ada[l] + b_ada[l], 3, axis=-1)
        shift_c, scale_c, gate_c = jnp.split(s_ctx @ w_ada[l] + b_ada[l], 3, axis=-1)
        h = rms_norm(x) * (1.0 + scale[:, None]) + shift[:, None]
        hc = rms_norm(ctx) * (1.0 + scale_c) + shift_c
        (g_q, g_k, g_v, g_z, g_af, g_ab, a_q, a_k, a_v, a_z, y_u, y_z, m_lat) = split_proj(h @ w_in[l])
        (gc_q, gc_k, gc_v, gc_z, gc_af, gc_ab, ac_q, ac_k, ac_v, ac_z, yc_u, yc_z, m_ctx) = split_proj(hc @ w_in[l])
        gla_p = (gla_wa_f[l], gla_ba_f[l], gla_wa_b[l], gla_ba_b[l])
        hy_p = (hy_conv_w[l], hy_conv_b[l], hy_f1_w[l], hy_f1_b[l], hy_f1_freq[l],
                hy_f2_w[l], hy_f2_b[l], hy_f2_freq[l], hy_f3_w[l], hy_skip[l])

        k, v = heads(g_k, GLA_HEADS), heads(g_v, GLA_HEADS)
        q = heads(g_q, GLA_HEADS) * gla_scale
        gf, gb = gla_decays(g_af, g_ab, *gla_p)
        kc, vc = heads(gc_k, GLA_HEADS), heads(gc_v, GLA_HEADS)
        gfc, gbc = gla_decays(gc_af, gc_ab, *gla_p)
        if need_ctx:
            qc = heads(gc_q, GLA_HEADS) * gla_scale
            oc_f, sc_f = gla_scan(qc, kc, vc, gfc, zero_state)
            oc_b, sc_b = gla_scan(rev(qc), rev(kc), rev(vc), rev(gbc), zero_state)
            yc_gla = gla_out(oc_f + rev(oc_b), gla_norm[l], gc_z)
        else:
            sc_f = gla_final_state(kc, vc, gfc)
            sc_b = gla_final_state(rev(kc), rev(vc), rev(gbc))
        o_f, _ = gla_scan(q, k, v, gf, sc_f)
        o_b, _ = gla_scan(rev(q), rev(k), rev(v), rev(gb), sc_b)
        y_gla = gla_out(o_f + rev(o_b), gla_norm[l], g_z)

        kc_a = qk_norm(heads(ac_k, ATTN_KV_HEADS), attn_knorm[l])
        vc_a = heads(ac_v, ATTN_KV_HEADS)
        q_a = apply_rope(qk_norm(heads(a_q, ATTN_HEADS), attn_qnorm[l]), ang)
        k_a = apply_rope(qk_norm(heads(a_k, ATTN_KV_HEADS), attn_knorm[l]), ang)
        y_attn = attn_latent(q_a, k_a, heads(a_v, ATTN_KV_HEADS), kc_a, vc_a) * jax.nn.silu(a_z)

        y_hy = hyena_mix(y_u, *hy_p) * jax.nn.silu(y_z)

        out = merge_branches(y_gla, y_attn, y_hy, m_lat, w_br_gla[l], w_br_attn[l], w_br_hy[l], w_out[l])
        x_new = x + gate[:, None] * out

        if need_ctx:
            qc_a = qk_norm(heads(ac_q, ATTN_HEADS), attn_qnorm[l])
            qc_g = qc_a.reshape(b_, Lc, ATTN_KV_HEADS, ATTN_GROUP, ATTN_HD)
            yc_attn = attend(qc_g, kc_a, vc_a).reshape(b_, Lc, ATTN_QW) * jax.nn.silu(ac_z)
            yc_hy = hyena_mix(yc_u, *hy_p) * jax.nn.silu(yc_z)
            out_c = merge_branches(yc_gla, yc_attn, yc_hy, m_ctx, w_br_gla[l], w_br_attn[l], w_br_hy[l], w_out[l])
            ctx = ctx + gate_c * out_c
        x = x_new

    return rms_norm(x) * final_norm
```

```python
import functools
import math

import numpy as np
import jax
import jax.numpy as jnp
from jax import lax
from jax.experimental import pallas as pl
from jax.experimental.pallas import tpu as pltpu

F32 = jnp.float32
BF16 = jnp.bfloat16

D_MODEL = 1024
GRID_W = 64
BRANCH_W = D_MODEL // 2
N_BRANCH = 3
EPS = 1e-6
GLA_HEADS = 4
GLA_DV = BRANCH_W // GLA_HEADS
GLA_DK = GLA_DV // 2
GLA_KW = GLA_HEADS * GLA_DK
GLA_VW = GLA_HEADS * GLA_DV
GLA_RANK = 16
GLA_GATE_NORM = 16.0
GLA_CHUNK = 64
ATTN_HD = 64
ATTN_HEADS = BRANCH_W // ATTN_HD
ATTN_KV_HEADS = ATTN_HEADS // 4
ATTN_GROUP = ATTN_HEADS // ATTN_KV_HEADS
ATTN_QW = ATTN_HEADS * ATTN_HD
ATTN_KVW = ATTN_KV_HEADS * ATTN_HD
ROPE_THETA = 10000.0
HY_W = BRANCH_W
HY_ORDER = 2
HY_EMB = 33
HY_BANDS = (HY_EMB - 1) // 2
HY_FFN = 64
HY_SHORT = 3
HY_MOD_SHIFT = 0.05
HY_DECAY_SHORT_PCT = 0.3
HY_DECAY_LONG_PCT = 1.5
HY_DECAY_TARGET = 1e-2
N_FILT = 2 * HY_ORDER * HY_W
SPLITS = (GLA_KW, GLA_KW, GLA_VW, GLA_VW, GLA_RANK, GLA_RANK,
          ATTN_QW, ATTN_KVW, ATTN_KVW, ATTN_QW,
          (HY_ORDER + 1) * HY_W, HY_W,
          N_BRANCH * D_MODEL)

LANES = 128
VMEM_LIMIT = 48 * 1024 * 1024

COLS = {
    'm': (0, 3072), 'y_u': (3072, 1536), 'g_v': (4608, 512), 'g_z': (5120, 512),
    'a_q': (5632, 512), 'a_z': (6144, 512), 'y_z': (6656, 512), 'g_q': (7168, 256),
    'g_k': (7424, 256), 'a_k': (7680, 128), 'a_v': (7808, 128), 'g_a': (7936, 128),
}
N_PROJ = 8064
PROJ_TN = 1152


def _cparams(*sem):
    return pltpu.CompilerParams(dimension_semantics=sem, vmem_limit_bytes=VMEM_LIMIT)


def _split(a):
    hi = a.astype(BF16)
    lo = (a - hi.astype(F32)).astype(BF16)
    return hi, lo


def _dot(a, b):
    return jnp.dot(a, b, preferred_element_type=F32)


def _dot_hp(a, b):
    ah, al = _split(a)
    bh, bl = _split(b)
    return _dot(ah, bh) + _dot(al, bh) + _dot(ah, bl)


def _dot_mat(mh, ml, x):
    xh, xl = _split(x)
    return _dot(mh, xh) + _dot(mh, xl) + _dot(ml, xh)


def _sigmoid(x):
    return 1.0 / (1.0 + jnp.exp(-x))


def _silu(x):
    return x * _sigmoid(x)


def _ada_kernel(c_ref, w_ref, b_ref, o_ref):
    o_ref[0] = _dot_hp(_silu(c_ref[...]), w_ref[0]) + b_ref[0]


def _ada_call(cmat, w_ada, b_ada):
    depth, d, n3 = w_ada.shape
    tn = 1024
    return pl.pallas_call(
        _ada_kernel,
        grid=(depth, n3 // tn),
        in_specs=[pl.BlockSpec((8, d), lambda l, j: (0, 0)),
                  pl.BlockSpec((1, d, tn), lambda l, j: (l, 0, j)),
                  pl.BlockSpec((1, 1, tn), lambda l, j: (l, 0, j))],
        out_specs=pl.BlockSpec((1, 8, tn), lambda l, j: (l, 0, j)),
        out_shape=jax.ShapeDtypeStruct((depth, 8, n3), F32),
        compiler_params=_cparams("arbitrary", "arbitrary"),
    )(cmat, w_ada, b_ada.reshape(depth, 1, n3))


def _proj_kernel(x_ref, sc_ref, sh_ref, w_ref, o_ref, h_scr):
    @pl.when(pl.program_id(2) == 0)
    def _():
        x = x_ref[0]
        ms = jnp.mean(x * x, axis=-1, keepdims=True)
        h = x * lax.rsqrt(ms + EPS) * (1.0 + sc_ref[0]) + sh_ref[0]
        h_scr[...] = h.astype(BF16)

    o_ref[0] = _dot(h_scr[...], w_ref[...])


def _proj_call(x, scale, shift, w):
    b_, L, d = x.shape
    n = w.shape[1]
    tm = min(512, L)
    tn = PROJ_TN
    return pl.pallas_call(
        _proj_kernel,
        grid=(b_, L // tm, n // tn),
        in_specs=[pl.BlockSpec((1, tm, d), lambda b, i, j: (b, i, 0)),
                  pl.BlockSpec((1, 1, d), lambda b, i, j: (b, 0, 0)),
                  pl.BlockSpec((1, 1, d), lambda b, i, j: (b, 0, 0)),
                  pl.BlockSpec((d, tn), lambda b, i, j: (0, j))],
        out_specs=pl.BlockSpec((1, tm, tn), lambda b, i, j: (b, i, j)),
        out_shape=jax.ShapeDtypeStruct((b_, L, n), F32),
        scratch_shapes=[pltpu.VMEM((tm, d), BF16)],
        compiler_params=_cparams("arbitrary", "arbitrary", "arbitrary"),
    )(x, scale, shift, w)


def _gla_kernel(*refs, reverse, finalize, nchunks):
    if finalize:
        (q_ref, k_ref, v_ref, ga_ref, wa_ref, ba_ref, s0_ref, tri_ref, op_ref, z_ref, nw_ref,
         o_ref, sfin_ref, s_scr) = refs
    else:
        (q_ref, k_ref, v_ref, ga_ref, wa_ref, ba_ref, s0_ref, tri_ref,
         o_ref, sfin_ref, s_scr) = refs
    C = GLA_CHUNK

    @pl.when(pl.program_id(1) == 0)
    def _():
        s_scr[...] = s0_ref[0]

    xg = _dot_hp(ga_ref[0], wa_ref[...]) + ba_ref[...]
    g_all = (jnp.minimum(xg, 0.0) - jnp.log(1.0 + jnp.exp(-jnp.abs(xg)))) * (1.0 / GLA_GATE_NORM)

    tri = tri_ref[...]
    ri = lax.broadcasted_iota(jnp.int32, (C, C), 0)
    ci = lax.broadcasted_iota(jnp.int32, (C, C), 1)
    mask = (ci >= ri) if reverse else (ci <= ri)
    lane = lax.broadcasted_iota(jnp.int32, (1, GLA_KW), 1)
    hmask = [(lane >= h * GLA_DK) & (lane < (h + 1) * GLA_DK) for h in range(GLA_HEADS)]
    tot_row = 0 if reverse else C - 1
    mid_row = C // 2
    scale = GLA_DK ** -0.5
    order = range(nchunks - 1, -1, -1) if reverse else range(nchunks)

    for c in order:
        rows = slice(c * C, (c + 1) * C)
        g = g_all[rows]
        g1 = g.astype(BF16)
        r1 = g - g1.astype(F32)
        g2 = r1.astype(BF16)
        g3 = (r1 - g2.astype(F32)).astype(BF16)
        b = _dot(tri, g1) + _dot(tri, g2) + _dot(tri, g3)
        bm = b[mid_row:mid_row + 1]
        bt = b[tot_row:tot_row + 1]
        qc = q_ref[0, rows, :] * scale
        kc = k_ref[0, rows, :]
        vc = v_ref[0, rows, :].astype(BF16)
        qa = qc * jnp.exp(b - bm)
        kb = (kc * jnp.exp(bm - b)).astype(BF16)
        qe = qc * jnp.exp(b)
        kd = (kc * jnp.exp(bt - b)).astype(BF16)
        dec = jnp.exp(bt)
        st = s_scr[...]
        st_b = st.astype(BF16)
        upd = jnp.zeros_like(st)
        outs = []
        for h in range(GLA_HEADS):
            vh = vc[:, h * GLA_DV:(h + 1) * GLA_DV]
            qa_h = jnp.where(hmask[h], qa, 0.0).astype(BF16)
            qe_h = jnp.where(hmask[h], qe, 0.0).astype(BF16)
            att = lax.dot_general(qa_h, kb, (((1,), (1,)), ((), ())), preferred_element_type=F32)
            att = jnp.where(mask, att, 0.0).astype(BF16)
            o_h = _dot(att, vh) + lax.dot_general(qe_h, st_b, (((1,), (1,)), ((), ())),
                                                  preferred_element_type=F32)
            outs.append(o_h)
            u_h = lax.dot_general(vh, kd, (((0,), (0,)), ((), ())), preferred_element_type=F32)
            upd = upd + jnp.where(hmask[h], u_h, 0.0)
        s_scr[...] = dec * st + upd
        o = jnp.concatenate(outs, axis=1)
        if finalize:
            o = o + op_ref[0, rows, :]
            parts = []
            for h in range(GLA_HEADS):
                oh = o[:, h * GLA_DV:(h + 1) * GLA_DV]
                ms = jnp.mean(oh * oh, axis=-1, keepdims=True)
                parts.append(oh * lax.rsqrt(ms + EPS) * nw_ref[...])
            y = jnp.concatenate(parts, axis=1) * _silu(z_ref[0, rows, :])
            o_ref[0, rows, :] = y.astype(o_ref.dtype)
        else:
            o_ref[0, rows, :] = o

    @pl.when(pl.program_id(1) == pl.num_programs(1) - 1)
    def _():
        sfin_ref[0] = s_scr[...]


def _gla_call(P, wa_pad, ba, s0, tri, reverse, fin=None):
    b_, L, _ = P.shape
    T = min(256, L)
    nt = L // T
    if reverse:
        tmap = lambda t: nt - 1 - t
    else:
        tmap = lambda t: t

    def col(name, width):
        blk = COLS[name][0] // width
        return pl.BlockSpec((1, T, width), lambda b, t: (b, tmap(t), blk))

    in_specs = [col('g_q', GLA_KW), col('g_k', GLA_KW), col('g_v', GLA_VW), col('g_a', LANES),
                pl.BlockSpec((LANES, GLA_KW), lambda b, t: (0, 0)),
                pl.BlockSpec((1, GLA_KW), lambda b, t: (0, 0)),
                pl.BlockSpec((1, GLA_DV, GLA_KW), lambda b, t: (b, 0, 0)),
                pl.BlockSpec((GLA_CHUNK, GLA_CHUNK), lambda b, t: (0, 0))]
    args = [P, P, P, P, wa_pad, ba, s0, tri]
    if fin is not None:
        o_prev, nw = fin
        in_specs += [pl.BlockSpec((1, T, GLA_VW), lambda b, t: (b, tmap(t), 0)),
                     col('g_z', GLA_VW),
                     pl.BlockSpec((1, GLA_DV), lambda b, t: (0, 0))]
        args += [o_prev, P, nw]
    out_dtype = BF16 if fin is not None else F32
    return pl.pallas_call(
        functools.partial(_gla_kernel, reverse=reverse, finalize=fin is not None, nchunks=T // GLA_CHUNK),
        grid=(b_, nt),
        in_specs=in_specs,
        out_specs=[pl.BlockSpec((1, T, GLA_VW), lambda b, t: (b, tmap(t), 0)),
                   pl.BlockSpec((1, GLA_DV, GLA_KW), lambda b, t: (b, 0, 0))],
        out_shape=[jax.ShapeDtypeStruct((b_, L, GLA_VW), out_dtype),
                   jax.ShapeDtypeStruct((b_, GLA_DV, GLA_KW), F32)],
        scratch_shapes=[pltpu.VMEM((GLA_DV, GLA_KW), F32)],
        compiler_params=_cparams("arbitrary", "arbitrary"),
    )(*args)


def _norm_rope(a, gain, bd, cos, sin):
    sq = a * a
    sh, sl = _split(sq)
    ss = _dot(sh, bd) + _dot(sl, bd)
    an = a * lax.rsqrt(ss * (1.0 / ATTN_HD) + EPS) * gain
    w = a.shape[-1]
    lane = lax.broadcasted_iota(jnp.int32, an.shape, 1)
    partner = jnp.where(lane % 2 == 0, pltpu.roll(an, w - 1, 1), pltpu.roll(an, 1, 1))
    return an * cos + partner * sin


def _qkv_prep_kernel(q_ref, k_ref, v_ref, cq_ref, sq_ref, ck_ref, sk_ref, gq_ref, gk_ref,
                     bdq_ref, bdk_ref, qo_ref, ko_ref, vo_ref):
    q = _norm_rope(q_ref[0], gq_ref[...], bdq_ref[...], cq_ref[...], sq_ref[...])
    qo_ref[0] = (q * (ATTN_HD ** -0.5)).astype(BF16)
    k = _norm_rope(k_ref[0], gk_ref[...], bdk_ref[...], ck_ref[...], sk_ref[...]).astype(BF16)
    v = v_ref[0].astype(BF16)
    for g in range(ATTN_KV_HEADS):
        ko_ref[0, g] = k[:, g * ATTN_HD:(g + 1) * ATTN_HD]
        vo_ref[0, g] = v[:, g * ATTN_HD:(g + 1) * ATTN_HD]


def _qkv_prep_call(P, cos_q, sin_q, cos_k, sin_k, gq, gk, bdq, bdk):
    b_, L, _ = P.shape
    T = min(512, L)

    def col(name, width):
        blk = COLS[name][0] // width
        return pl.BlockSpec((1, T, width), lambda b, t: (b, t, blk))

    tab = lambda w: pl.BlockSpec((T, w), lambda b, t: (t, 0))
    const = lambda r, w: pl.BlockSpec((r, w), lambda b, t: (0, 0))
    kv_spec = pl.BlockSpec((1, ATTN_KV_HEADS, T, ATTN_HD), lambda b, t: (b, 0, t, 0))
    return pl.pallas_call(
        _qkv_prep_kernel,
        grid=(b_, L // T),
        in_specs=[col('a_q', ATTN_QW), col('a_k', ATTN_KVW), col('a_v', ATTN_KVW),
                  tab(ATTN_QW), tab(ATTN_QW), tab(ATTN_KVW), tab(ATTN_KVW),
                  const(1, ATTN_QW), const(1, ATTN_KVW),
                  const(ATTN_QW, ATTN_QW), const(ATTN_KVW, ATTN_KVW)],
        out_specs=[pl.BlockSpec((1, T, ATTN_QW), lambda b, t: (b, t, 0)), kv_spec, kv_spec],
        out_shape=[jax.ShapeDtypeStruct((b_, L, ATTN_QW), BF16),
                   jax.ShapeDtypeStruct((b_, ATTN_KV_HEADS, L, ATTN_HD), BF16),
                   jax.ShapeDtypeStruct((b_, ATTN_KV_HEADS, L, ATTN_HD), BF16)],
        compiler_params=_cparams("arbitrary", "arbitrary"),
    )(P, P, P, cos_q, sin_q, cos_k, sin_k, gq, gk, bdq, bdk)


def _attn_kernel(q_ref, k_ref, v_ref, z_ref, o_ref, *, tk, nk):
    q = q_ref[0]
    tq = q.shape[0]
    qs = jnp.concatenate([q[:, r * ATTN_HD:(r + 1) * ATTN_HD] for r in range(ATTN_GROUP)], axis=0)

    def body(j, carry):
        m, l, acc = carry
        start = pl.multiple_of(j * tk, tk)
        kj = k_ref[0, 0, pl.ds(start, tk), :]
        vj = v_ref[0, 0, pl.ds(start, tk), :]
        s = lax.dot_general(qs, kj, (((1,), (1,)), ((), ())), preferred_element_type=F32)
        m_new = jnp.maximum(m, jnp.max(s, axis=-1, keepdims=True))
        p = jnp.exp(s - m_new)
        alpha = jnp.exp(m - m_new)
        l = alpha * l + jnp.sum(p, axis=-1, keepdims=True)
        acc = alpha * acc + _dot(p.astype(BF16), vj)
        return m_new, l, acc

    rows = ATTN_GROUP * tq
    init = (jnp.full((rows, 1), -jnp.inf, F32), jnp.zeros((rows, 1), F32),
            jnp.zeros((rows, ATTN_HD), F32))
    m, l, acc = lax.fori_loop(0, nk, body, init)
    o = acc / l
    o = jnp.concatenate([o[r * tq:(r + 1) * tq] for r in range(ATTN_GROUP)], axis=1)
    o_ref[0] = (o * _silu(z_ref[0])).astype(o_ref.dtype)


def _attn_call(q, k, v, P):
    b_, L, _ = q.shape
    Lk = k.shape[2]
    tq = min(128, L)
    tk = next(t for t in (512, 256, 128, 64) if Lk % t == 0)
    gw = ATTN_GROUP * ATTN_HD
    zblk = COLS['a_z'][0] // gw
    return pl.pallas_call(
        functools.partial(_attn_kernel, tk=tk, nk=Lk // tk),
        grid=(b_, ATTN_KV_HEADS, L // tq),
        in_specs=[pl.BlockSpec((1, tq, gw), lambda b, g, i: (b, i, g)),
                  pl.BlockSpec((1, 1, Lk, ATTN_HD), lambda b, g, i: (b, g, 0, 0)),
                  pl.BlockSpec((1, 1, Lk, ATTN_HD), lambda b, g, i: (b, g, 0, 0)),
                  pl.BlockSpec((1, tq, gw), lambda b, g, i: (b, i, zblk + g))],
        out_specs=pl.BlockSpec((1, tq, gw), lambda b, g, i: (b, i, g)),
        out_shape=jax.ShapeDtypeStruct((b_, L, ATTN_QW), BF16),
        compiler_params=_cparams("arbitrary", "arbitrary", "arbitrary"),
    )(q, k, v, P)


def _short_conv_kernel(u_ref, up_ref, un_ref, w_ref, b_ref, v_ref, x1_ref, x2_ref):
    t = pl.program_id(1)
    nt = pl.num_programs(1)
    u = u_ref[0]
    T = u.shape[0]
    prev_row = jnp.where(t > 0, up_ref[0, 7:8, :], 0.0)
    next_row = jnp.where(t < nt - 1, un_ref[0, 0:1, :], 0.0)
    row = lax.broadcasted_iota(jnp.int32, u.shape, 0)
    u_m1 = jnp.where(row == 0, prev_row, pltpu.roll(u, 1, 0))
    u_p1 = jnp.where(row == T - 1, next_row, pltpu.roll(u, T - 1, 0))
    w = w_ref[...]
    out = b_ref[...] + u_m1 * w[0:1] + u * w[1:2] + u_p1 * w[2:3]
    v_ref[0] = out[:, 0:HY_W]
    x1_ref[0] = out[:, HY_W:2 * HY_W]
    x2_ref[0] = out[:, 2 * HY_W:3 * HY_W]


def _short_conv_call(P, w, bias):
    b_, L, _ = P.shape
    T = min(512, L)
    cw = (HY_ORDER + 1) * HY_W
    blk = COLS['y_u'][0] // cw
    hb = T // 8
    nh = L // 8
    out_spec = pl.BlockSpec((1, T, HY_W), lambda b, t: (b, t, 0))
    shp = jax.ShapeDtypeStruct((b_, L, HY_W), F32)
    return pl.pallas_call(
        _short_conv_kernel,
        grid=(b_, L // T),
        in_specs=[pl.BlockSpec((1, T, cw), lambda b, t: (b, t, blk)),
                  pl.BlockSpec((1, 8, cw), lambda b, t: (b, jnp.maximum(t * hb - 1, 0), blk)),
                  pl.BlockSpec((1, 8, cw), lambda b, t: (b, jnp.minimum((t + 1) * hb, nh - 1), blk)),
                  pl.BlockSpec((HY_SHORT, cw), lambda b, t: (0, 0)),
                  pl.BlockSpec((1, cw), lambda b, t: (0, 0))],
        out_specs=[out_spec, out_spec, out_spec],
        out_shape=[shp, shp, shp],
        compiler_params=_cparams("arbitrary", "arbitrary"),
    )(P, P, P, w, bias)


def _filter_kernel(emb_ref, w1_ref, b1_ref, f1_ref, w2_ref, b2_ref, f2_ref, w3_ref, dl_ref,
                   h_ref, sum_ref):
    t = pl.program_id(0)
    emb = emb_ref[...]
    h = jnp.sin(f1_ref[...] * (_dot_hp(emb, w1_ref[...]) + b1_ref[...]))
    h = jnp.sin(f2_ref[...] * (_dot_hp(h, w2_ref[...]) + b2_ref[...]))
    h = _dot_hp(h, w3_ref[...])
    tt = emb[:, 0:1]
    h = h * (jnp.exp(-tt * dl_ref[...]) + HY_MOD_SHIFT)

    @pl.when(t == 0)
    def _():
        sum_ref[...] = jnp.zeros_like(sum_ref)

    sum_ref[...] += jnp.broadcast_to(jnp.sum(jnp.abs(h), axis=0, keepdims=True), sum_ref.shape)
    row = lax.broadcasted_iota(jnp.int32, h.shape, 0) + t * h.shape[0]
    colm = lax.broadcasted_iota(jnp.int32, h.shape, 1)
    bwd = (colm // HY_W) % 2 == 1
    h_ref[...] = jnp.where(bwd & (row == 0), 0.0, h)


def _filter_call(emb, w1, b1, f1, w2, b2, f2, w3, deltas):
    L = emb.shape[0]
    TL = min(256, L)
    const = lambda r, w: pl.BlockSpec((r, w), lambda t: (0, 0))
    return pl.pallas_call(
        _filter_kernel,
        grid=(L // TL,),
        in_specs=[pl.BlockSpec((TL, LANES), lambda t: (t, 0)),
                  const(LANES, HY_FFN), const(1, HY_FFN), const(1, HY_FFN),
                  const(HY_FFN, HY_FFN), const(1, HY_FFN), const(1, HY_FFN),
                  const(HY_FFN, N_FILT), const(1, N_FILT)],
        out_specs=[pl.BlockSpec((TL, N_FILT), lambda t: (t, 0)),
                   pl.BlockSpec((8, N_FILT), lambda t: (0, 0))],
        out_shape=[jax.ShapeDtypeStruct((L, N_FILT), F32),
                   jax.ShapeDtypeStruct((8, N_FILT), F32)],
        compiler_params=_cparams("arbitrary"),
    )(emb, w1, b1, f1, w2, b2, f2, w3, deltas)


def _fft_sizes(n):
    lg = int(round(math.log2(n)))
    assert 1 << lg == n
    n1 = 1 << ((lg + 1) // 2)
    return n1, n // n1


def _bf16_pair(a):
    a = np.asarray(a, np.float32)
    hi = jnp.asarray(a, F32).astype(BF16)
    lo = (jnp.asarray(a, F32) - hi.astype(F32)).astype(BF16)
    return hi, lo


def _fft_tables(n1, n2):
    n = n1 * n2
    h1 = n1 // 2
    k1 = np.arange(n1)[:, None]
    a = 2.0 * np.pi * ((k1 * np.arange(h1)[None, :]) % n1) / n1
    c, s = np.cos(a), np.sin(a)
    fa = np.block([[c, s], [-s, c]])
    fa_real = np.concatenate([c, -s], axis=0)
    fd = np.block([[c.T, -s.T], [s.T, c.T]])
    k2 = np.arange(n2)[:, None]
    b = 2.0 * np.pi * ((k2 * np.arange(n2)[None, :]) % n2) / n2
    cb, sb = np.cos(b), np.sin(b)
    fb = np.block([[cb, sb], [-sb, cb]])
    fc = np.block([[cb, -sb], [sb, cb]])
    kn = (jnp.arange(n1, dtype=jnp.int32)[:, None] * jnp.arange(n2, dtype=jnp.int32)[None, :]) % n
    tw = kn.astype(F32) * (2.0 * math.pi / n)
    twr = jnp.broadcast_to(jnp.cos(tw)[:, :, None], (n1, n2, LANES))
    twi = jnp.broadcast_to(-jnp.sin(tw)[:, :, None], (n1, n2, LANES))
    return dict(fa=_bf16_pair(fa), fa_real=_bf16_pair(fa_real), fd=_bf16_pair(fd),
                fb=_bf16_pair(fb), fc=_bf16_pair(fc), twr=twr, twi=twi)


def _fa_kernel(x_ref, mh_ref, ml_ref, o_ref, *, packed):
    if packed:
        x = x_ref[...]
        x = x.reshape(x.shape[0] * x.shape[1], x.shape[2])
    else:
        x = x_ref[...]
    o_ref[...] = _dot_mat(mh_ref[...], ml_ref[...], x)


def _fa_call(x, mats, n1, packed):
    mh, ml = mats
    wtot = x.shape[-1]
    W = min(2048, wtot)
    h1 = n1 // 2
    if packed:
        x_spec = pl.BlockSpec((2, h1, W), lambda j: (0, 0, j))
    else:
        x_spec = pl.BlockSpec((h1, W), lambda j: (0, j))
    return pl.pallas_call(
        functools.partial(_fa_kernel, packed=packed),
        grid=(wtot // W,),
        in_specs=[x_spec, pl.BlockSpec(mh.shape, lambda j: (0, 0)), pl.BlockSpec(ml.shape, lambda j: (0, 0))],
        out_specs=pl.BlockSpec((2 * n1, W), lambda j: (0, j)),
        out_shape=jax.ShapeDtypeStruct((2 * n1, wtot), F32),
        compiler_params=_cparams("arbitrary"),
    )(x, mh, ml)


def _lane_tile(a, width):
    return jnp.concatenate([a] * (width // a.shape[-1]), axis=-1)


def _fb_filter_kernel(t_ref, twr_ref, twi_ref, fh_ref, fl_ref, inv_ref, g_ref, *, n2, scale):
    w = t_ref.shape[-1]
    tr = t_ref[0, 0]
    ti = t_ref[1, 0]
    twr = _lane_tile(twr_ref[0], w)
    twi = _lane_tile(twi_ref[0], w)
    p = jnp.concatenate([tr * twr - ti * twi, tr * twi + ti * twr], axis=0)
    z = _dot_mat(fh_ref[...], fl_ref[...], p) * (inv_ref[0:1, :] * scale)
    zr, zi = z[:n2], z[n2:]
    for o in range(HY_ORDER):
        f0 = (2 * o) * HY_W
        b0 = (2 * o + 1) * HY_W
        g_ref[o, 0, 0] = zr[:, f0:f0 + HY_W] + zr[:, b0:b0 + HY_W]
        g_ref[o, 1, 0] = zi[:, f0:f0 + HY_W] - zi[:, b0:b0 + HY_W]


def _fb_filter_call(t, tabs, inv_sums, n1, n2):
    fh, fl = tabs['fb']
    t4 = t.reshape(2, n1, n2, N_FILT)
    return pl.pallas_call(
        functools.partial(_fb_filter_kernel, n2=n2, scale=1.0 / (n1 * n2)),
        grid=(n1,),
        in_specs=[pl.BlockSpec((2, 1, n2, N_FILT), lambda k: (0, k, 0, 0)),
                  pl.BlockSpec((1, n2, LANES), lambda k: (k, 0, 0)),
                  pl.BlockSpec((1, n2, LANES), lambda k: (k, 0, 0)),
                  pl.BlockSpec(fh.shape, lambda k: (0, 0)),
                  pl.BlockSpec(fl.shape, lambda k: (0, 0)),
                  pl.BlockSpec((8, N_FILT), lambda k: (0, 0))],
        out_specs=pl.BlockSpec((HY_ORDER, 2, 1, n2, HY_W), lambda k: (0, 0, k, 0, 0)),
        out_shape=jax.ShapeDtypeStruct((HY_ORDER, 2, n1, n2, HY_W), F32),
        compiler_params=_cparams("arbitrary"),
    )(t4, tabs['twr'], tabs['twi'], fh, fl, inv_sums)


def _fb_kernel(t_ref, twr_ref, twi_ref, fh_ref, fl_ref, ch_ref, cl_ref, g_ref, o_ref, *, n2):
    w = t_ref.shape[-1]
    tr = t_ref[0, 0]
    ti = t_ref[1, 0]
    twr = _lane_tile(twr_ref[0], w)
    twi = _lane_tile(twi_ref[0], w)
    p = jnp.concatenate([tr * twr - ti * twi, tr * twi + ti * twr], axis=0)
    z = _dot_mat(fh_ref[...], fl_ref[...], p)
    zr, zi = z[:n2], z[n2:]
    gr = g_ref[0, 0, 0]
    gi = g_ref[0, 1, 0]
    y = jnp.concatenate([zr * gr - zi * gi, zr * gi + zi * gr], axis=0)
    v = _dot_mat(ch_ref[...], cl_ref[...], y)
    vr, vi = v[:n2], v[n2:]
    o_ref[0, 0] = vr * twr + vi * twi
    o_ref[1, 0] = vi * twr - vr * twi


def _fb_call(t, g, order, tabs, n1, n2):
    fh, fl = tabs['fb']
    ch, cl = tabs['fc']
    t4 = t.reshape(2, n1, n2, HY_W)
    mat = lambda m: pl.BlockSpec(m.shape, lambda k: (0, 0))
    out = pl.pallas_call(
        functools.partial(_fb_kernel, n2=n2),
        grid=(n1,),
        in_specs=[pl.BlockSpec((2, 1, n2, HY_W), lambda k: (0, k, 0, 0)),
                  pl.BlockSpec((1, n2, LANES), lambda k: (k, 0, 0)),
                  pl.BlockSpec((1, n2, LANES), lambda k: (k, 0, 0)),
                  mat(fh), mat(fl), mat(ch), mat(cl),
                  pl.BlockSpec((1, 2, 1, n2, HY_W), lambda k: (order, 0, k, 0, 0))],
        out_specs=pl.BlockSpec((2, 1, n2, HY_W), lambda k: (0, k, 0, 0)),
        out_shape=jax.ShapeDtypeStruct((2, n1, n2, HY_W), F32),
        compiler_params=_cparams("arbitrary"),
    )(t4, tabs['twr'], tabs['twi'], fh, fl, ch, cl, g)
    return out.reshape(2 * n1, n2 * HY_W)


def _fd_kernel(u_ref, mh_ref, ml_ref, z_ref, x_ref, sk_ref, o_ref):
    y = _dot_mat(mh_ref[...], ml_ref[...], u_ref[...])
    y = y.reshape(o_ref.shape)
    o_ref[...] = x_ref[...] * (y + sk_ref[...] * z_ref[...])


def _fd_call(u, mats, z, xg, skip_row, n1):
    mh, ml = mats
    b_, h1, wtot = z.shape
    W = min(2048, wtot)
    dat = pl.BlockSpec((b_, h1, W), lambda j: (0, 0, j))
    return pl.pallas_call(
        _fd_kernel,
        grid=(wtot // W,),
        in_specs=[pl.BlockSpec((2 * n1, W), lambda j: (0, j)),
                  pl.BlockSpec(mh.shape, lambda j: (0, 0)), pl.BlockSpec(ml.shape, lambda j: (0, 0)),
                  dat, dat, pl.BlockSpec((1, W), lambda j: (0, 0))],
        out_specs=dat,
        out_shape=jax.ShapeDtypeStruct((b_, h1, wtot), F32),
        compiler_params=_cparams("arbitrary"),
    )(u, mh, ml, z, xg, skip_row)


def _hyena_filter_spectrum(L, tabs, n1, n2, fp):
    f32 = F32
    t = jnp.linspace(0.0, 1.0, L, dtype=f32)[:, None]
    w = 2.0 * math.pi * jnp.arange(L, dtype=f32)[:, None] / L
    fr = jnp.linspace(1e-4, HY_BANDS - 1, HY_BANDS, dtype=f32)[None]
    emb = jnp.concatenate([t, jnp.cos(fr * w), -jnp.sin(fr * w)], axis=-1)
    emb = jnp.pad(emb, ((0, 0), (0, LANES - HY_EMB)))
    deltas = jnp.abs(jnp.linspace(math.log(HY_DECAY_TARGET) / HY_DECAY_SHORT_PCT,
                                  math.log(HY_DECAY_TARGET) / HY_DECAY_LONG_PCT, N_FILT, dtype=f32))[None]
    f1_w, f1_b, f1_freq, f2_w, f2_b, f2_freq, f3_w = fp
    w1 = jnp.pad(f1_w, ((0, LANES - HY_EMB), (0, 0)))
    h, sums = _filter_call(emb, w1, f1_b[None], f1_freq[None], f2_w, f2_b[None], f2_freq[None], f3_w, deltas)
    inv_sums = 1.0 / sums
    hv = h.reshape(n1 // 2, n2 * N_FILT)
    th = _fa_call(hv, tabs['fa_real'], n1, packed=False)
    return _fb_filter_call(th, tabs, inv_sums, n1, n2)


def _hyena_conv(z, xg, g, order, skip, tabs, n1, n2):
    b_, L, wd = z.shape
    assert b_ == 2, "the two batch rows ride as the real and imaginary parts of one transform"
    zv = z.reshape(b_, n1 // 2, n2 * wd)
    xv = xg.reshape(b_, n1 // 2, n2 * wd)
    t = _fa_call(zv, tabs['fa'], n1, packed=True)
    u = _fb_call(t, g, order, tabs, n1, n2)
    wtile = min(2048, n2 * wd)
    skip_row = jnp.tile(skip[None, :], (1, wtile // wd))
    out = _fd_call(u, tabs['fd'], zv, xv, skip_row, n1)
    return out.reshape(b_, L, wd)


def _merge_kernel(yg_ref, ya_ref, yh_ref, yz_ref, mg_ref, ma_ref, mh_ref, x_ref, gate_ref,
                  wg_ref, wa_ref, wh_ref, wo_ref, fn_ref, o_ref, *, final):
    yh = (yh_ref[0] * _silu(yz_ref[0])).astype(BF16)
    y = (_sigmoid(mg_ref[0]) * _dot(yg_ref[0], wg_ref[...])
         + _sigmoid(ma_ref[0]) * _dot(ya_ref[0], wa_ref[...])
         + _sigmoid(mh_ref[0]) * _dot(yh, wh_ref[...]))
    out = _dot(y.astype(BF16), wo_ref[...])
    xn = x_ref[0] + gate_ref[0] * out
    if final:
        ms = jnp.mean(xn * xn, axis=-1, keepdims=True)
        xn = xn * lax.rsqrt(ms + EPS) * fn_ref[...]
    o_ref[0] = xn


def _merge_call(y_gla, y_attn, y_hy, P, x, gate, wg, wa, wh, wo, fnorm, final):
    b_, L, d = x.shape
    T = min(512, L)
    row = lambda w, blk=0: pl.BlockSpec((1, T, w), lambda b, t: (b, t, blk))
    const = lambda r, w: pl.BlockSpec((r, w), lambda b, t: (0, 0))
    mblk = COLS['m'][0] // d
    return pl.pallas_call(
        functools.partial(_merge_kernel, final=final),
        grid=(b_, L // T),
        in_specs=[row(BRANCH_W), row(BRANCH_W), row(BRANCH_W), row(HY_W, COLS['y_z'][0] // HY_W),
                  row(d, mblk), row(d, mblk + 1), row(d, mblk + 2), row(d),
                  pl.BlockSpec((1, 1, d), lambda b, t: (b, 0, 0)),
                  const(BRANCH_W, d), const(BRANCH_W, d), const(BRANCH_W, d), const(d, d), const(1, d)],
        out_specs=row(d),
        out_shape=jax.ShapeDtypeStruct((b_, L, d), F32),
        compiler_params=_cparams("arbitrary", "arbitrary"),
    )(y_gla, y_attn, y_hy, P, P, P, P, x, gate, wg, wa, wh, wo, fnorm)


def _pack_w_in(w):
    parts = dict(zip(('g_q', 'g_k', 'g_v', 'g_z', 'g_af', 'g_ab', 'a_q', 'a_k', 'a_v', 'a_z', 'y_u', 'y_z', 'm'),
                     jnp.split(w, [int(i) for i in np.cumsum(SPLITS)[:-1]], axis=-1)))
    parts['g_a'] = jnp.pad(jnp.concatenate([parts['g_af'], parts['g_ab']], axis=-1),
                           ((0, 0), (0, LANES - 2 * GLA_RANK)))
    order = sorted(COLS, key=lambda n: COLS[n][0])
    return jnp.concatenate([parts[n] for n in order], axis=-1).astype(BF16)


def _rope_tables(L, heads):
    t = jnp.arange(L)
    row = (t // GRID_W).astype(F32)
    colp = (t % GRID_W).astype(F32)
    half = ATTN_HD // 2
    inv = ROPE_THETA ** (-jnp.arange(0, half, 2, dtype=F32) / half)
    ang = jnp.concatenate([row[:, None] * inv, colp[:, None] * inv], axis=-1)
    cos = jnp.repeat(jnp.cos(ang), 2, axis=-1)
    sin = jnp.stack([-jnp.sin(ang), jnp.sin(ang)], axis=-1).reshape(L, ATTN_HD)
    return jnp.tile(cos, (1, heads)), jnp.tile(sin, (1, heads))


def _identity_rope(L, heads):
    return jnp.ones((L, heads * ATTN_HD), F32), jnp.zeros((L, heads * ATTN_HD), F32)


def _block_diag_ones(width):
    i = np.arange(width) // ATTN_HD
    return jnp.asarray((i[:, None] == i[None, :]).astype(np.float32)).astype(BF16)


def kernel(x, c, ctx, c_ctx, w_ada, b_ada, w_in, gla_wa_f, gla_ba_f, gla_wa_b, gla_ba_b, gla_norm,
           attn_qnorm, attn_knorm, hy_conv_w, hy_conv_b, hy_f1_w, hy_f1_b, hy_f1_freq, hy_f2_w,
           hy_f2_b, hy_f2_freq, hy_f3_w, hy_skip, w_br_gla, w_br_attn, w_br_hy, w_out, final_norm):
    b_, L, d = x.shape
    Lc = ctx.shape[1]
    depth = w_ada.shape[0]

    cos_q, sin_q = _rope_tables(L, ATTN_HEADS)
    cos_k, sin_k = cos_q[:, :ATTN_KVW], sin_q[:, :ATTN_KVW]
    cos_qc, sin_qc = _identity_rope(Lc, ATTN_HEADS)
    cos_kc, sin_kc = cos_qc[:, :ATTN_KVW], sin_qc[:, :ATTN_KVW]
    bdq, bdk = _block_diag_ones(ATTN_QW), _block_diag_ones(ATTN_KVW)
    n1, n2 = _fft_sizes(2 * L)
    n1c, n2c = _fft_sizes(2 * Lc)
    tabs = _fft_tables(n1, n2)
    tabs_c = _fft_tables(n1c, n2c)
    ii = np.arange(GLA_CHUNK)
    tri_f = jnp.asarray((ii[None, :] <= ii[:, None]).astype(np.float32)).astype(BF16)
    tri_b = jnp.asarray((ii[None, :] >= ii[:, None]).astype(np.float32)).astype(BF16)
    zero_state = jnp.zeros((b_, GLA_DV, GLA_KW), F32)

    cmat = jnp.concatenate([c, c_ctx[None], jnp.zeros((8 - b_ - 1, d), F32)], axis=0)
    mods = _ada_call(cmat, w_ada, b_ada)

    for l in range(depth):
        need_ctx = l < depth - 1
        shift, scale, gate = [m[:b_, None, :] for m in jnp.split(mods[l], 3, axis=-1)]
        shift_c, scale_c, gate_c = [jnp.broadcast_to(m[b_:b_ + 1, None, :], (b_, 1, d))
                                    for m in jnp.split(mods[l], 3, axis=-1)]
        w_packed = _pack_w_in(w_in[l])
        P = _proj_call(x, scale, shift, w_packed)
        Pc = _proj_call(ctx, scale_c, shift_c, w_packed)

        rk = GLA_RANK
        wa_f = jnp.zeros((LANES, GLA_KW), F32).at[0:rk].set(gla_wa_f[l])
        wa_b = jnp.zeros((LANES, GLA_KW), F32).at[rk:2 * rk].set(gla_wa_b[l])
        ba_f, ba_b = gla_ba_f[l][None], gla_ba_b[l][None]
        nw = gla_norm[l][None]
        oc_f, sc_f = _gla_call(Pc, wa_f, ba_f, zero_state, tri_f, reverse=False)
        yc_gla, sc_b = _gla_call(Pc, wa_b, ba_b, zero_state, tri_b, reverse=True, fin=(oc_f, nw))
        o_f, _ = _gla_call(P, wa_f, ba_f, sc_f, tri_f, reverse=False)
        y_gla, _ = _gla_call(P, wa_b, ba_b, sc_b, tri_b, reverse=True, fin=(o_f, nw))

        gq = jnp.tile(attn_qnorm[l], ATTN_HEADS)[None]
        gk = jnp.tile(attn_knorm[l], ATTN_KV_HEADS)[None]
        q_a, k_a, v_a = _qkv_prep_call(P, cos_q, sin_q, cos_k, sin_k, gq, gk, bdq, bdk)
        qc_a, kc_a, vc_a = _qkv_prep_call(Pc, cos_qc, sin_qc, cos_kc, sin_kc, gq, gk, bdq, bdk)
        y_attn = _attn_call(q_a, jnp.concatenate([k_a, kc_a], axis=2),
                            jnp.concatenate([v_a, vc_a], axis=2), P)

        fp = (hy_f1_w[l], hy_f1_b[l], hy_f1_freq[l], hy_f2_w[l], hy_f2_b[l], hy_f2_freq[l], hy_f3_w[l])
        g_spec = _hyena_filter_spectrum(L, tabs, n1, n2, fp)
        v0, x1, x2 = _short_conv_call(P, hy_conv_w[l], hy_conv_b[l][None])
        z1 = _hyena_conv(v0, x1, g_spec, 0, hy_skip[l, 0], tabs, n1, n2)
        y_hy = _hyena_conv(z1, x2, g_spec, 1, hy_skip[l, 1], tabs, n1, n2)

        wg, wa, wh, wo = (w_br_gla[l].astype(BF16), w_br_attn[l].astype(BF16),
                          w_br_hy[l].astype(BF16), w_out[l].astype(BF16))
        fn = final_norm[None]
        x_new = _merge_call(y_gla, y_attn, y_hy, P, x, gate, wg, wa, wh, wo, fn, final=not need_ctx)

        if need_ctx:
            yc_attn = _attn_call(qc_a, kc_a, vc_a, Pc)
            gc_spec = _hyena_filter_spectrum(Lc, tabs_c, n1c, n2c, fp)
            vc0, xc1, xc2 = _short_conv_call(Pc, hy_conv_w[l], hy_conv_b[l][None])
            zc1 = _hyena_conv(vc0, xc1, gc_spec, 0, hy_skip[l, 0], tabs_c, n1c, n2c)
            yc_hy = _hyena_conv(zc1, xc2, gc_spec, 1, hy_skip[l, 1], tabs_c, n1c, n2c)
            ctx = _merge_call(yc_gla, yc_attn, yc_hy, Pc, ctx, gate_c, wg, wa, wh, wo, fn, final=False)
        x = x_new

    return x
```

```python
import functools
import math

import numpy as np
import jax
import jax.numpy as jnp
from jax import lax
from jax.experimental import pallas as pl
from jax.experimental.pallas import tpu as pltpu

F32 = jnp.float32
BF16 = jnp.bfloat16

D_MODEL = 1024
GRID_W = 64
BRANCH_W = D_MODEL // 2
N_BRANCH = 3
EPS = 1e-6
GLA_HEADS = 4
GLA_DV = BRANCH_W // GLA_HEADS
GLA_DK = GLA_DV // 2
GLA_KW = GLA_HEADS * GLA_DK
GLA_VW = GLA_HEADS * GLA_DV
GLA_RANK = 16
GLA_GATE_NORM = 16.0
GLA_CHUNK = 64
ATTN_HD = 64
ATTN_HEADS = BRANCH_W // ATTN_HD
ATTN_KV_HEADS = ATTN_HEADS // 4
ATTN_GROUP = ATTN_HEADS // ATTN_KV_HEADS
ATTN_QW = ATTN_HEADS * ATTN_HD
ATTN_KVW = ATTN_KV_HEADS * ATTN_HD
ROPE_THETA = 10000.0
HY_W = BRANCH_W
HY_ORDER = 2
HY_EMB = 33
HY_BANDS = (HY_EMB - 1) // 2
HY_FFN = 64
HY_SHORT = 3
HY_MOD_SHIFT = 0.05
HY_DECAY_SHORT_PCT = 0.3
HY_DECAY_LONG_PCT = 1.5
HY_DECAY_TARGET = 1e-2
N_FILT = 2 * HY_ORDER * HY_W
SPLITS = (GLA_KW, GLA_KW, GLA_VW, GLA_VW, GLA_RANK, GLA_RANK,
          ATTN_QW, ATTN_KVW, ATTN_KVW, ATTN_QW,
          (HY_ORDER + 1) * HY_W, HY_W,
          N_BRANCH * D_MODEL)

LANES = 128
VMEM_LIMIT = 48 * 1024 * 1024

COLS = {
    'm': (0, 3072), 'y_u': (3072, 1536), 'g_v': (4608, 512), 'g_z': (5120, 512),
    'a_q': (5632, 512), 'a_z': (6144, 512), 'y_z': (6656, 512), 'g_q': (7168, 256),
    'g_k': (7424, 256), 'a_k': (7680, 128), 'a_v': (7808, 128), 'g_a': (7936, 128),
}
N_PROJ = 8064
PROJ_TN = 1152


def _cparams(*sem):
    return pltpu.CompilerParams(dimension_semantics=sem, vmem_limit_bytes=VMEM_LIMIT)


def _split(a):
    hi = a.astype(BF16)
    lo = (a - hi.astype(F32)).astype(BF16)
    return hi, lo


def _dot(a, b):
    return jnp.dot(a, b, preferred_element_type=F32)


def _dot_hp(a, b):
    ah, al = _split(a)
    bh, bl = _split(b)
    return _dot(ah, bh) + _dot(al, bh) + _dot(ah, bl)


def _dot_mat(mh, ml, x):
    xh, xl = _split(x)
    return _dot(mh, xh) + _dot(mh, xl) + _dot(ml, xh)


def _sigmoid(x):
    return 1.0 / (1.0 + jnp.exp(-x))


def _silu(x):
    return x * _sigmoid(x)


def _ada_kernel(c_ref, w_ref, b_ref, o_ref):
    o_ref[0] = _dot_hp(_silu(c_ref[...]), w_ref[0]) + b_ref[0]


def _ada_call(cmat, w_ada, b_ada):
    depth, d, n3 = w_ada.shape
    tn = 1024
    return pl.pallas_call(
        _ada_kernel,
        grid=(depth, n3 // tn),
        in_specs=[pl.BlockSpec((8, d), lambda l, j: (0, 0)),
                  pl.BlockSpec((1, d, tn), lambda l, j: (l, 0, j)),
                  pl.BlockSpec((1, 1, tn), lambda l, j: (l, 0, j))],
        out_specs=pl.BlockSpec((1, 8, tn), lambda l, j: (l, 0, j)),
        out_shape=jax.ShapeDtypeStruct((depth, 8, n3), F32),
        compiler_params=_cparams("arbitrary", "arbitrary"),
    )(cmat, w_ada, b_ada.reshape(depth, 1, n3))


def _proj_kernel(x_ref, sc_ref, sh_ref, w_ref, o_ref, h_scr):
    @pl.when(pl.program_id(2) == 0)
    def _():
        x = x_ref[0]
        ms = jnp.mean(x * x, axis=-1, keepdims=True)
        h = x * lax.rsqrt(ms + EPS) * (1.0 + sc_ref[0]) + sh_ref[0]
        h_scr[...] = h.astype(BF16)

    o_ref[0] = _dot(h_scr[...], w_ref[...])


def _proj_call(x, scale, shift, w):
    b_, L, d = x.shape
    n = w.shape[1]
    tm = min(512, L)
    tn = PROJ_TN
    return pl.pallas_call(
        _proj_kernel,
        grid=(b_, L // tm, n // tn),
        in_specs=[pl.BlockSpec((1, tm, d), lambda b, i, j: (b, i, 0)),
                  pl.BlockSpec((1, 1, d), lambda b, i, j: (b, 0, 0)),
                  pl.BlockSpec((1, 1, d), lambda b, i, j: (b, 0, 0)),
                  pl.BlockSpec((d, tn), lambda b, i, j: (0, j))],
        out_specs=pl.BlockSpec((1, tm, tn), lambda b, i, j: (b, i, j)),
        out_shape=jax.ShapeDtypeStruct((b_, L, n), F32),
        scratch_shapes=[pltpu.VMEM((tm, d), BF16)],
        compiler_params=_cparams("arbitrary", "arbitrary", "arbitrary"),
    )(x, scale, shift, w)


def _gla_kernel(*refs, reverse, finalize, nchunks):
    if finalize:
        (q_ref, k_ref, v_ref, ga_ref, wa_ref, ba_ref, s0_ref, tri_ref, op_ref, z_ref, nw_ref,
         o_ref, sfin_ref, s_scr) = refs
    else:
        (q_ref, k_ref, v_ref, ga_ref, wa_ref, ba_ref, s0_ref, tri_ref,
         o_ref, sfin_ref, s_scr) = refs
    C = GLA_CHUNK

    @pl.when(pl.program_id(1) == 0)
    def _():
        s_scr[...] = s0_ref[0]

    xg = _dot_hp(ga_ref[0], wa_ref[...]) + ba_ref[...]
    g_all = (jnp.minimum(xg, 0.0) - jnp.log(1.0 + jnp.exp(-jnp.abs(xg)))) * (1.0 / GLA_GATE_NORM)

    tri = tri_ref[...]
    ri = lax.broadcasted_iota(jnp.int32, (C, C), 0)
    ci = lax.broadcasted_iota(jnp.int32, (C, C), 1)
    mask = (ci >= ri) if reverse else (ci <= ri)
    lane = lax.broadcasted_iota(jnp.int32, (1, GLA_KW), 1)
    hmask = [(lane >= h * GLA_DK) & (lane < (h + 1) * GLA_DK) for h in range(GLA_HEADS)]
    tot_row = 0 if reverse else C - 1
    mid_row = C // 2
    scale = GLA_DK ** -0.5
    order = range(nchunks - 1, -1, -1) if reverse else range(nchunks)

    for c in order:
        rows = slice(c * C, (c + 1) * C)
        g = g_all[rows]
        g1 = g.astype(BF16)
        r1 = g - g1.astype(F32)
        g2 = r1.astype(BF16)
        g3 = (r1 - g2.astype(F32)).astype(BF16)
        b = _dot(tri, g1) + _dot(tri, g2) + _dot(tri, g3)
        bm = b[mid_row:mid_row + 1]
        bt = b[tot_row:tot_row + 1]
        qc = q_ref[0, rows, :] * scale
        kc = k_ref[0, rows, :]
        vc = v_ref[0, rows, :].astype(BF16)
        qa = qc * jnp.exp(b - bm)
        kb = (kc * jnp.exp(bm - b)).astype(BF16)
        qe = qc * jnp.exp(b)
        kd = (kc * jnp.exp(bt - b)).astype(BF16)
        dec = jnp.exp(bt)
        st = s_scr[...]
        st_b = st.astype(BF16)
        upd = jnp.zeros_like(st)
        outs = []
        for h in range(GLA_HEADS):
            vh = vc[:, h * GLA_DV:(h + 1) * GLA_DV]
            qa_h = jnp.where(hmask[h], qa, 0.0).astype(BF16)
            qe_h = jnp.where(hmask[h], qe, 0.0).astype(BF16)
            att = lax.dot_general(qa_h, kb, (((1,), (1,)), ((), ())), preferred_element_type=F32)
            att = jnp.where(mask, att, 0.0).astype(BF16)
            o_h = _dot(att, vh) + lax.dot_general(qe_h, st_b, (((1,), (1,)), ((), ())),
                                                  preferred_element_type=F32)
            outs.append(o_h)
            u_h = lax.dot_general(vh, kd, (((0,), (0,)), ((), ())), preferred_element_type=F32)
            upd = upd + jnp.where(hmask[h], u_h, 0.0)
        s_scr[...] = dec * st + upd
        o = jnp.concatenate(outs, axis=1)
        if finalize:
            o = o + op_ref[0, rows, :]
            parts = []
            for h in range(GLA_HEADS):
                oh = o[:, h * GLA_DV:(h + 1) * GLA_DV]
                ms = jnp.mean(oh * oh, axis=-1, keepdims=True)
                parts.append(oh * lax.rsqrt(ms + EPS) * nw_ref[...])
            y = jnp.concatenate(parts, axis=1) * _silu(z_ref[0, rows, :])
            o_ref[0, rows, :] = y.astype(o_ref.dtype)
        else:
            o_ref[0, rows, :] = o

    @pl.when(pl.program_id(1) == pl.num_programs(1) - 1)
    def _():
        sfin_ref[0] = s_scr[...]


def _gla_call(P, wa_pad, ba, s0, tri, reverse, fin=None):
    b_, L, _ = P.shape
    T = min(256, L)
    nt = L // T
    if reverse:
        tmap = lambda t: nt - 1 - t
    else:
        tmap = lambda t: t

    def col(name, width):
        blk = COLS[name][0] // width
        return pl.BlockSpec((1, T, width), lambda b, t: (b, tmap(t), blk))

    in_specs = [col('g_q', GLA_KW), col('g_k', GLA_KW), col('g_v', GLA_VW), col('g_a', LANES),
                pl.BlockSpec((LANES, GLA_KW), lambda b, t: (0, 0)),
                pl.BlockSpec((1, GLA_KW), lambda b, t: (0, 0)),
                pl.BlockSpec((1, GLA_DV, GLA_KW), lambda b, t: (b, 0, 0)),
                pl.BlockSpec((GLA_CHUNK, GLA_CHUNK), lambda b, t: (0, 0))]
    args = [P, P, P, P, wa_pad, ba, s0, tri]
    if fin is not None:
        o_prev, nw = fin
        in_specs += [pl.BlockSpec((1, T, GLA_VW), lambda b, t: (b, tmap(t), 0)),
                     col('g_z', GLA_VW),
                     pl.BlockSpec((1, GLA_DV), lambda b, t: (0, 0))]
        args += [o_prev, P, nw]
    out_dtype = BF16 if fin is not None else F32
    return pl.pallas_call(
        functools.partial(_gla_kernel, reverse=reverse, finalize=fin is not None, nchunks=T // GLA_CHUNK),
        grid=(b_, nt),
        in_specs=in_specs,
        out_specs=[pl.BlockSpec((1, T, GLA_VW), lambda b, t: (b, tmap(t), 0)),
                   pl.BlockSpec((1, GLA_DV, GLA_KW), lambda b, t: (b, 0, 0))],
        out_shape=[jax.ShapeDtypeStruct((b_, L, GLA_VW), out_dtype),
                   jax.ShapeDtypeStruct((b_, GLA_DV, GLA_KW), F32)],
        scratch_shapes=[pltpu.VMEM((GLA_DV, GLA_KW), F32)],
        compiler_params=_cparams("arbitrary", "arbitrary"),
    )(*args)


def _norm_rope(a, gain, bd, cos, sin):
    sq = a * a
    sh, sl = _split(sq)
    ss = _dot(sh, bd) + _dot(sl, bd)
    an = a * lax.rsqrt(ss * (1.0 / ATTN_HD) + EPS) * gain
    w = a.shape[-1]
    lane = lax.broadcasted_iota(jnp.int32, an.shape, 1)
    partner = jnp.where(lane % 2 == 0, pltpu.roll(an, w - 1, 1), pltpu.roll(an, 1, 1))
    return an * cos + partner * sin


Q_SCALE = (ATTN_HD ** -0.5) * math.log2(math.e)
ATTN_TK_MAX = 640


def _qkv_prep_kernel(q_ref, k_ref, v_ref, cq_ref, sq_ref, ck_ref, sk_ref, gq_ref, gk_ref,
                     bdq_ref, bdk_ref, eq_ref, ev_ref, qo_ref, ko_ref, vo_ref):
    nt = (((1,), (1,)), ((), ()))
    q = _norm_rope(q_ref[0], gq_ref[...], bdq_ref[...], cq_ref[...], sq_ref[...])
    qb = (q * Q_SCALE).astype(BF16)
    qo_ref[0] = lax.dot_general(eq_ref[...], qb, nt, preferred_element_type=F32).astype(BF16)
    k = _norm_rope(k_ref[0], gk_ref[...], bdk_ref[...], ck_ref[...], sk_ref[...]).astype(BF16)
    for g in range(ATTN_KV_HEADS):
        ko_ref[0, g] = k[:, g * ATTN_HD:(g + 1) * ATTN_HD]
    vb = v_ref[0].astype(BF16)
    vo_ref[0] = lax.dot_general(ev_ref[...], vb, nt, preferred_element_type=F32).astype(BF16)


def _qkv_prep_call(P, cos_q, sin_q, cos_k, sin_k, gq, gk, bdq, bdk, eye_q, eye_v):
    b_, L, _ = P.shape
    T = min(512, L)

    def col(name, width):
        blk = COLS[name][0] // width
        return pl.BlockSpec((1, T, width), lambda b, t: (b, t, blk))

    tab = lambda w: pl.BlockSpec((T, w), lambda b, t: (t, 0))
    const = lambda r, w: pl.BlockSpec((r, w), lambda b, t: (0, 0))
    return pl.pallas_call(
        _qkv_prep_kernel,
        grid=(b_, L // T),
        in_specs=[col('a_q', ATTN_QW), col('a_k', ATTN_KVW), col('a_v', ATTN_KVW),
                  tab(ATTN_QW), tab(ATTN_QW), tab(ATTN_KVW), tab(ATTN_KVW),
                  const(1, ATTN_QW), const(1, ATTN_KVW),
                  const(ATTN_QW, ATTN_QW), const(ATTN_KVW, ATTN_KVW),
                  const(ATTN_QW, ATTN_QW), const(ATTN_KVW, ATTN_KVW)],
        out_specs=[pl.BlockSpec((1, ATTN_QW, T), lambda b, t: (b, 0, t)),
                   pl.BlockSpec((1, ATTN_KV_HEADS, T, ATTN_HD), lambda b, t: (b, 0, t, 0)),
                   pl.BlockSpec((1, ATTN_KVW, T), lambda b, t: (b, 0, t))],
        out_shape=[jax.ShapeDtypeStruct((b_, ATTN_QW, L), BF16),
                   jax.ShapeDtypeStruct((b_, ATTN_KV_HEADS, L, ATTN_HD), BF16),
                   jax.ShapeDtypeStruct((b_, ATTN_KVW, L), BF16)],
        compiler_params=_cparams("arbitrary", "arbitrary"),
    )(P, P, P, cos_q, sin_q, cos_k, sin_k, gq, gk, bdq, bdk, eye_q, eye_v)


def _attn_kernel(qt_ref, k_ref, vt_ref, z_ref, e_ref, o_ref, m_scr, l_scr, a_scr, acc_scr, s_scr, p_scr,
                 *, tk, nk):
    m_scr[...] = jnp.full(m_scr.shape, -jnp.inf, F32)
    l_scr[...] = jnp.zeros(l_scr.shape, F32)
    acc_scr[...] = jnp.zeros(acc_scr.shape, F32)

    def body(j, carry):
        start = pl.multiple_of(j * tk, tk)
        kj = k_ref[0, 0, pl.ds(start, tk), :]
        vtj = vt_ref[0, :, pl.ds(start, tk)]
        for h in range(ATTN_GROUP):
            s_scr[h] = _dot(kj, qt_ref[0, h * ATTN_HD:(h + 1) * ATTN_HD, :])
        for h in range(ATTN_GROUP):
            s = s_scr[h]
            m_old = m_scr[h]
            m_new = jnp.maximum(m_old, jnp.max(s, axis=0, keepdims=True))
            p = jnp.exp2(s - m_new)
            alpha = jnp.exp2(m_old - m_new)
            a_scr[h] = alpha
            l_scr[h] = alpha * l_scr[h] + jnp.sum(p, axis=0, keepdims=True)
            m_scr[h] = m_new
            p_scr[h] = p.astype(BF16)
        for h in range(ATTN_GROUP):
            acc_scr[h] = a_scr[h] * acc_scr[h] + _dot(vtj, p_scr[h])
        return carry

    lax.fori_loop(0, nk, body, 0)
    tn = (((0,), (0,)), ((), ()))
    out = None
    for h in range(ATTN_GROUP):
        oh, ol = _split(acc_scr[h] / l_scr[h])
        part = (lax.dot_general(oh, e_ref[h], tn, preferred_element_type=F32)
                + lax.dot_general(ol, e_ref[h], tn, preferred_element_type=F32))
        out = part if out is None else out + part
    o_ref[0] = (out * _silu(z_ref[0])).astype(o_ref.dtype)


def _attn_call(qt, k, vt, P, e_heads):
    b_, _, L = qt.shape
    Lk = k.shape[2]
    tq = min(256, L)
    tk = max(t for t in range(LANES, ATTN_TK_MAX + 1, LANES) if Lk % t == 0)
    gw = ATTN_GROUP * ATTN_HD
    zblk = COLS['a_z'][0] // gw
    return pl.pallas_call(
        functools.partial(_attn_kernel, tk=tk, nk=Lk // tk),
        grid=(b_, ATTN_KV_HEADS, L // tq),
        in_specs=[pl.BlockSpec((1, gw, tq), lambda b, g, i: (b, g, i)),
                  pl.BlockSpec((1, 1, Lk, ATTN_HD), lambda b, g, i: (b, g, 0, 0)),
                  pl.BlockSpec((1, ATTN_HD, Lk), lambda b, g, i: (b, g, 0)),
                  pl.BlockSpec((1, tq, gw), lambda b, g, i: (b, i, zblk + g)),
                  pl.BlockSpec((ATTN_GROUP, ATTN_HD, gw), lambda b, g, i: (0, 0, 0))],
        out_specs=pl.BlockSpec((1, tq, gw), lambda b, g, i: (b, i, g)),
        out_shape=jax.ShapeDtypeStruct((b_, L, ATTN_QW), BF16),
        scratch_shapes=[pltpu.VMEM((ATTN_GROUP, 1, tq), F32), pltpu.VMEM((ATTN_GROUP, 1, tq), F32),
                        pltpu.VMEM((ATTN_GROUP, 1, tq), F32), pltpu.VMEM((ATTN_GROUP, ATTN_HD, tq), F32),
                        pltpu.VMEM((ATTN_GROUP, tk, tq), F32), pltpu.VMEM((ATTN_GROUP, tk, tq), BF16)],
        compiler_params=_cparams("arbitrary", "arbitrary", "arbitrary"),
    )(qt, k, vt, P, e_heads)


def _short_conv_kernel(u_ref, up_ref, un_ref, w_ref, b_ref, v_ref, x1_ref, x2_ref):
    t = pl.program_id(1)
    nt = pl.num_programs(1)
    u = u_ref[0]
    T = u.shape[0]
    prev_row = jnp.where(t > 0, up_ref[0, 7:8, :], 0.0)
    next_row = jnp.where(t < nt - 1, un_ref[0, 0:1, :], 0.0)
    row = lax.broadcasted_iota(jnp.int32, u.shape, 0)
    u_m1 = jnp.where(row == 0, prev_row, pltpu.roll(u, 1, 0))
    u_p1 = jnp.where(row == T - 1, next_row, pltpu.roll(u, T - 1, 0))
    w = w_ref[...]
    out = b_ref[...] + u_m1 * w[0:1] + u * w[1:2] + u_p1 * w[2:3]
    v_ref[0] = out[:, 0:HY_W]
    x1_ref[0] = out[:, HY_W:2 * HY_W]
    x2_ref[0] = out[:, 2 * HY_W:3 * HY_W]


def _short_conv_call(P, w, bias):
    b_, L, _ = P.shape
    T = min(512, L)
    cw = (HY_ORDER + 1) * HY_W
    blk = COLS['y_u'][0] // cw
    hb = T // 8
    nh = L // 8
    out_spec = pl.BlockSpec((1, T, HY_W), lambda b, t: (b, t, 0))
    shp = jax.ShapeDtypeStruct((b_, L, HY_W), F32)
    return pl.pallas_call(
        _short_conv_kernel,
        grid=(b_, L // T),
        in_specs=[pl.BlockSpec((1, T, cw), lambda b, t: (b, t, blk)),
                  pl.BlockSpec((1, 8, cw), lambda b, t: (b, jnp.maximum(t * hb - 1, 0), blk)),
                  pl.BlockSpec((1, 8, cw), lambda b, t: (b, jnp.minimum((t + 1) * hb, nh - 1), blk)),
                  pl.BlockSpec((HY_SHORT, cw), lambda b, t: (0, 0)),
                  pl.BlockSpec((1, cw), lambda b, t: (0, 0))],
        out_specs=[out_spec, out_spec, out_spec],
        out_shape=[shp, shp, shp],
        compiler_params=_cparams("arbitrary", "arbitrary"),
    )(P, P, P, w, bias)


def _filter_kernel(emb_ref, w1_ref, b1_ref, f1_ref, w2_ref, b2_ref, f2_ref, w3_ref, dl_ref,
                   h_ref, sum_ref):
    t = pl.program_id(0)
    emb = emb_ref[...]
    h = jnp.sin(f1_ref[...] * (_dot_hp(emb, w1_ref[...]) + b1_ref[...]))
    h = jnp.sin(f2_ref[...] * (_dot_hp(h, w2_ref[...]) + b2_ref[...]))
    h = _dot_hp(h, w3_ref[...])
    tt = emb[:, 0:1]
    h = h * (jnp.exp(-tt * dl_ref[...]) + HY_MOD_SHIFT)

    @pl.when(t == 0)
    def _():
        sum_ref[...] = jnp.zeros_like(sum_ref)

    sum_ref[...] += jnp.broadcast_to(jnp.sum(jnp.abs(h), axis=0, keepdims=True), sum_ref.shape)
    row = lax.broadcasted_iota(jnp.int32, h.shape, 0) + t * h.shape[0]
    colm = lax.broadcasted_iota(jnp.int32, h.shape, 1)
    bwd = (colm // HY_W) % 2 == 1
    h_ref[...] = jnp.where(bwd & (row == 0), 0.0, h)


def _filter_call(emb, w1, b1, f1, w2, b2, f2, w3, deltas):
    L = emb.shape[0]
    TL = min(256, L)
    const = lambda r, w: pl.BlockSpec((r, w), lambda t: (0, 0))
    return pl.pallas_call(
        _filter_kernel,
        grid=(L // TL,),
        in_specs=[pl.BlockSpec((TL, LANES), lambda t: (t, 0)),
                  const(LANES, HY_FFN), const(1, HY_FFN), const(1, HY_FFN),
                  const(HY_FFN, HY_FFN), const(1, HY_FFN), const(1, HY_FFN),
                  const(HY_FFN, N_FILT), const(1, N_FILT)],
        out_specs=[pl.BlockSpec((TL, N_FILT), lambda t: (t, 0)),
                   pl.BlockSpec((8, N_FILT), lambda t: (0, 0))],
        out_shape=[jax.ShapeDtypeStruct((L, N_FILT), F32),
                   jax.ShapeDtypeStruct((8, N_FILT), F32)],
        compiler_params=_cparams("arbitrary"),
    )(emb, w1, b1, f1, w2, b2, f2, w3, deltas)


def _fft_sizes(n):
    lg = int(round(math.log2(n)))
    assert 1 << lg == n
    n1 = 1 << ((lg + 1) // 2)
    return n1, n // n1


def _bf16_pair(a):
    a = np.asarray(a, np.float32)
    hi = jnp.asarray(a, F32).astype(BF16)
    lo = (jnp.asarray(a, F32) - hi.astype(F32)).astype(BF16)
    return hi, lo


def _fft_tables(n1, n2):
    n = n1 * n2
    h1 = n1 // 2
    k1 = np.arange(n1)[:, None]
    a = 2.0 * np.pi * ((k1 * np.arange(h1)[None, :]) % n1) / n1
    c, s = np.cos(a), np.sin(a)
    fa = np.block([[c, s], [-s, c]])
    fa_real = np.concatenate([c, -s], axis=0)
    fd = np.block([[c.T, -s.T], [s.T, c.T]])
    k2 = np.arange(n2)[:, None]
    b = 2.0 * np.pi * ((k2 * np.arange(n2)[None, :]) % n2) / n2
    cb, sb = np.cos(b), np.sin(b)
    fb = np.block([[cb, sb], [-sb, cb]])
    fc = np.block([[cb, -sb], [sb, cb]])
    kn = (jnp.arange(n1, dtype=jnp.int32)[:, None] * jnp.arange(n2, dtype=jnp.int32)[None, :]) % n
    tw = kn.astype(F32) * (2.0 * math.pi / n)
    twr = jnp.broadcast_to(jnp.cos(tw)[:, :, None], (n1, n2, LANES))
    twi = jnp.broadcast_to(-jnp.sin(tw)[:, :, None], (n1, n2, LANES))
    return dict(fa=_bf16_pair(fa), fa_real=_bf16_pair(fa_real), fd=_bf16_pair(fd),
                fb=_bf16_pair(fb), fc=_bf16_pair(fc), twr=twr, twi=twi)


def _fa_kernel(x_ref, mh_ref, ml_ref, o_ref, *, packed):
    if packed:
        x = x_ref[...]
        x = x.reshape(x.shape[0] * x.shape[1], x.shape[2])
    else:
        x = x_ref[...]
    o_ref[...] = _dot_mat(mh_ref[...], ml_ref[...], x)


def _fa_call(x, mats, n1, packed):
    mh, ml = mats
    wtot = x.shape[-1]
    W = min(2048, wtot)
    h1 = n1 // 2
    if packed:
        x_spec = pl.BlockSpec((2, h1, W), lambda j: (0, 0, j))
    else:
        x_spec = pl.BlockSpec((h1, W), lambda j: (0, j))
    return pl.pallas_call(
        functools.partial(_fa_kernel, packed=packed),
        grid=(wtot // W,),
        in_specs=[x_spec, pl.BlockSpec(mh.shape, lambda j: (0, 0)), pl.BlockSpec(ml.shape, lambda j: (0, 0))],
        out_specs=pl.BlockSpec((2 * n1, W), lambda j: (0, j)),
        out_shape=jax.ShapeDtypeStruct((2 * n1, wtot), F32),
        compiler_params=_cparams("arbitrary"),
    )(x, mh, ml)


def _lane_tile(a, width):
    return jnp.concatenate([a] * (width // a.shape[-1]), axis=-1)


def _fb_filter_kernel(t_ref, twr_ref, twi_ref, fh_ref, fl_ref, inv_ref, g_ref, *, n2, scale):
    w = t_ref.shape[-1]
    tr = t_ref[0, 0]
    ti = t_ref[1, 0]
    twr = _lane_tile(twr_ref[0], w)
    twi = _lane_tile(twi_ref[0], w)
    p = jnp.concatenate([tr * twr - ti * twi, tr * twi + ti * twr], axis=0)
    z = _dot_mat(fh_ref[...], fl_ref[...], p) * (inv_ref[0:1, :] * scale)
    zr, zi = z[:n2], z[n2:]
    for o in range(HY_ORDER):
        f0 = (2 * o) * HY_W
        b0 = (2 * o + 1) * HY_W
        g_ref[o, 0, 0] = zr[:, f0:f0 + HY_W] + zr[:, b0:b0 + HY_W]
        g_ref[o, 1, 0] = zi[:, f0:f0 + HY_W] - zi[:, b0:b0 + HY_W]


def _fb_filter_call(t, tabs, inv_sums, n1, n2):
    fh, fl = tabs['fb']
    t4 = t.reshape(2, n1, n2, N_FILT)
    return pl.pallas_call(
        functools.partial(_fb_filter_kernel, n2=n2, scale=1.0 / (n1 * n2)),
        grid=(n1,),
        in_specs=[pl.BlockSpec((2, 1, n2, N_FILT), lambda k: (0, k, 0, 0)),
                  pl.BlockSpec((1, n2, LANES), lambda k: (k, 0, 0)),
                  pl.BlockSpec((1, n2, LANES), lambda k: (k, 0, 0)),
                  pl.BlockSpec(fh.shape, lambda k: (0, 0)),
                  pl.BlockSpec(fl.shape, lambda k: (0, 0)),
                  pl.BlockSpec((8, N_FILT), lambda k: (0, 0))],
        out_specs=pl.BlockSpec((HY_ORDER, 2, 1, n2, HY_W), lambda k: (0, 0, k, 0, 0)),
        out_shape=jax.ShapeDtypeStruct((HY_ORDER, 2, n1, n2, HY_W), F32),
        compiler_params=_cparams("arbitrary"),
    )(t4, tabs['twr'], tabs['twi'], fh, fl, inv_sums)


def _fb_kernel(t_ref, twr_ref, twi_ref, fh_ref, fl_ref, ch_ref, cl_ref, g_ref, o_ref, *, n2):
    w = t_ref.shape[-1]
    tr = t_ref[0, 0]
    ti = t_ref[1, 0]
    twr = _lane_tile(twr_ref[0], w)
    twi = _lane_tile(twi_ref[0], w)
    p = jnp.concatenate([tr * twr - ti * twi, tr * twi + ti * twr], axis=0)
    z = _dot_mat(fh_ref[...], fl_ref[...], p)
    zr, zi = z[:n2], z[n2:]
    gr = g_ref[0, 0, 0]
    gi = g_ref[0, 1, 0]
    y = jnp.concatenate([zr * gr - zi * gi, zr * gi + zi * gr], axis=0)
    v = _dot_mat(ch_ref[...], cl_ref[...], y)
    vr, vi = v[:n2], v[n2:]
    o_ref[0, 0] = vr * twr + vi * twi
    o_ref[1, 0] = vi * twr - vr * twi


def _fb_call(t, g, order, tabs, n1, n2):
    fh, fl = tabs['fb']
    ch, cl = tabs['fc']
    t4 = t.reshape(2, n1, n2, HY_W)
    mat = lambda m: pl.BlockSpec(m.shape, lambda k: (0, 0))
    out = pl.pallas_call(
        functools.partial(_fb_kernel, n2=n2),
        grid=(n1,),
        in_specs=[pl.BlockSpec((2, 1, n2, HY_W), lambda k: (0, k, 0, 0)),
                  pl.BlockSpec((1, n2, LANES), lambda k: (k, 0, 0)),
                  pl.BlockSpec((1, n2, LANES), lambda k: (k, 0, 0)),
                  mat(fh), mat(fl), mat(ch), mat(cl),
                  pl.BlockSpec((1, 2, 1, n2, HY_W), lambda k: (order, 0, k, 0, 0))],
        out_specs=pl.BlockSpec((2, 1, n2, HY_W), lambda k: (0, k, 0, 0)),
        out_shape=jax.ShapeDtypeStruct((2, n1, n2, HY_W), F32),
        compiler_params=_cparams("arbitrary"),
    )(t4, tabs['twr'], tabs['twi'], fh, fl, ch, cl, g)
    return out.reshape(2 * n1, n2 * HY_W)


def _fd_kernel(u_ref, mh_ref, ml_ref, z_ref, x_ref, sk_ref, o_ref):
    y = _dot_mat(mh_ref[...], ml_ref[...], u_ref[...])
    y = y.reshape(o_ref.shape)
    o_ref[...] = x_ref[...] * (y + sk_ref[...] * z_ref[...])


def _fd_call(u, mats, z, xg, skip_row, n1):
    mh, ml = mats
    b_, h1, wtot = z.shape
    W = min(2048, wtot)
    dat = pl.BlockSpec((b_, h1, W), lambda j: (0, 0, j))
    return pl.pallas_call(
        _fd_kernel,
        grid=(wtot // W,),
        in_specs=[pl.BlockSpec((2 * n1, W), lambda j: (0, j)),
                  pl.BlockSpec(mh.shape, lambda j: (0, 0)), pl.BlockSpec(ml.shape, lambda j: (0, 0)),
                  dat, dat, pl.BlockSpec((1, W), lambda j: (0, 0))],
        out_specs=dat,
        out_shape=jax.ShapeDtypeStruct((b_, h1, wtot), F32),
        compiler_params=_cparams("arbitrary"),
    )(u, mh, ml, z, xg, skip_row)


def _hyena_filter_spectrum(L, tabs, n1, n2, fp):
    f32 = F32
    t = jnp.linspace(0.0, 1.0, L, dtype=f32)[:, None]
    w = 2.0 * math.pi * jnp.arange(L, dtype=f32)[:, None] / L
    fr = jnp.linspace(1e-4, HY_BANDS - 1, HY_BANDS, dtype=f32)[None]
    emb = jnp.concatenate([t, jnp.cos(fr * w), -jnp.sin(fr * w)], axis=-1)
    emb = jnp.pad(emb, ((0, 0), (0, LANES - HY_EMB)))
    deltas = jnp.abs(jnp.linspace(math.log(HY_DECAY_TARGET) / HY_DECAY_SHORT_PCT,
                                  math.log(HY_DECAY_TARGET) / HY_DECAY_LONG_PCT, N_FILT, dtype=f32))[None]
    f1_w, f1_b, f1_freq, f2_w, f2_b, f2_freq, f3_w = fp
    w1 = jnp.pad(f1_w, ((0, LANES - HY_EMB), (0, 0)))
    h, sums = _filter_call(emb, w1, f1_b[None], f1_freq[None], f2_w, f2_b[None], f2_freq[None], f3_w, deltas)
    inv_sums = 1.0 / sums
    hv = h.reshape(n1 // 2, n2 * N_FILT)
    th = _fa_call(hv, tabs['fa_real'], n1, packed=False)
    return _fb_filter_call(th, tabs, inv_sums, n1, n2)


def _hyena_conv(z, xg, g, order, skip, tabs, n1, n2):
    b_, L, wd = z.shape
    assert b_ == 2, "the two batch rows ride as the real and imaginary parts of one transform"
    zv = z.reshape(b_, n1 // 2, n2 * wd)
    xv = xg.reshape(b_, n1 // 2, n2 * wd)
    t = _fa_call(zv, tabs['fa'], n1, packed=True)
    u = _fb_call(t, g, order, tabs, n1, n2)
    wtile = min(2048, n2 * wd)
    skip_row = jnp.tile(skip[None, :], (1, wtile // wd))
    out = _fd_call(u, tabs['fd'], zv, xv, skip_row, n1)
    return out.reshape(b_, L, wd)


def _merge_kernel(yg_ref, ya_ref, yh_ref, yz_ref, mg_ref, ma_ref, mh_ref, x_ref, gate_ref,
                  wg_ref, wa_ref, wh_ref, wo_ref, fn_ref, o_ref, *, final):
    yh = (yh_ref[0] * _silu(yz_ref[0])).astype(BF16)
    y = (_sigmoid(mg_ref[0]) * _dot(yg_ref[0], wg_ref[...])
         + _sigmoid(ma_ref[0]) * _dot(ya_ref[0], wa_ref[...])
         + _sigmoid(mh_ref[0]) * _dot(yh, wh_ref[...]))
    out = _dot(y.astype(BF16), wo_ref[...])
    xn = x_ref[0] + gate_ref[0] * out
    if final:
        ms = jnp.mean(xn * xn, axis=-1, keepdims=True)
        xn = xn * lax.rsqrt(ms + EPS) * fn_ref[...]
    o_ref[0] = xn


def _merge_call(y_gla, y_attn, y_hy, P, x, gate, wg, wa, wh, wo, fnorm, final):
    b_, L, d = x.shape
    T = min(512, L)
    row = lambda w, blk=0: pl.BlockSpec((1, T, w), lambda b, t: (b, t, blk))
    const = lambda r, w: pl.BlockSpec((r, w), lambda b, t: (0, 0))
    mblk = COLS['m'][0] // d
    return pl.pallas_call(
        functools.partial(_merge_kernel, final=final),
        grid=(b_, L // T),
        in_specs=[row(BRANCH_W), row(BRANCH_W), row(BRANCH_W), row(HY_W, COLS['y_z'][0] // HY_W),
                  row(d, mblk), row(d, mblk + 1), row(d, mblk + 2), row(d),
                  pl.BlockSpec((1, 1, d), lambda b, t: (b, 0, 0)),
                  const(BRANCH_W, d), const(BRANCH_W, d), const(BRANCH_W, d), const(d, d), const(1, d)],
        out_specs=row(d),
        out_shape=jax.ShapeDtypeStruct((b_, L, d), F32),
        compiler_params=_cparams("arbitrary", "arbitrary"),
    )(y_gla, y_attn, y_hy, P, P, P, P, x, gate, wg, wa, wh, wo, fnorm)


def _pack_w_in(w):
    parts = dict(zip(('g_q', 'g_k', 'g_v', 'g_z', 'g_af', 'g_ab', 'a_q', 'a_k', 'a_v', 'a_z', 'y_u', 'y_z', 'm'),
                     jnp.split(w, [int(i) for i in np.cumsum(SPLITS)[:-1]], axis=-1)))
    parts['g_a'] = jnp.pad(jnp.concatenate([parts['g_af'], parts['g_ab']], axis=-1),
                           ((0, 0), (0, LANES - 2 * GLA_RANK)))
    order = sorted(COLS, key=lambda n: COLS[n][0])
    return jnp.concatenate([parts[n] for n in order], axis=-1).astype(BF16)


def _rope_tables(L, heads):
    t = jnp.arange(L)
    row = (t // GRID_W).astype(F32)
    colp = (t % GRID_W).astype(F32)
    half = ATTN_HD // 2
    inv = ROPE_THETA ** (-jnp.arange(0, half, 2, dtype=F32) / half)
    ang = jnp.concatenate([row[:, None] * inv, colp[:, None] * inv], axis=-1)
    cos = jnp.repeat(jnp.cos(ang), 2, axis=-1)
    sin = jnp.stack([-jnp.sin(ang), jnp.sin(ang)], axis=-1).reshape(L, ATTN_HD)
    return jnp.tile(cos, (1, heads)), jnp.tile(sin, (1, heads))


def _identity_rope(L, heads):
    return jnp.ones((L, heads * ATTN_HD), F32), jnp.zeros((L, heads * ATTN_HD), F32)


def _block_diag_ones(width):
    i = np.arange(width) // ATTN_HD
    return jnp.asarray((i[:, None] == i[None, :]).astype(np.float32)).astype(BF16)


def kernel(x, c, ctx, c_ctx, w_ada, b_ada, w_in, gla_wa_f, gla_ba_f, gla_wa_b, gla_ba_b, gla_norm,
           attn_qnorm, attn_knorm, hy_conv_w, hy_conv_b, hy_f1_w, hy_f1_b, hy_f1_freq, hy_f2_w,
           hy_f2_b, hy_f2_freq, hy_f3_w, hy_skip, w_br_gla, w_br_attn, w_br_hy, w_out, final_norm):
    b_, L, d = x.shape
    Lc = ctx.shape[1]
    depth = w_ada.shape[0]

    cos_q, sin_q = _rope_tables(L, ATTN_HEADS)
    cos_k, sin_k = cos_q[:, :ATTN_KVW], sin_q[:, :ATTN_KVW]
    cos_qc, sin_qc = _identity_rope(Lc, ATTN_HEADS)
    cos_kc, sin_kc = cos_qc[:, :ATTN_KVW], sin_qc[:, :ATTN_KVW]
    bdq, bdk = _block_diag_ones(ATTN_QW), _block_diag_ones(ATTN_KVW)
    eye_q, eye_v = jnp.eye(ATTN_QW, dtype=BF16), jnp.eye(ATTN_KVW, dtype=BF16)
    gw = ATTN_GROUP * ATTN_HD
    e_heads = jnp.stack([jnp.eye(ATTN_HD, gw, k=h * ATTN_HD, dtype=BF16) for h in range(ATTN_GROUP)])
    n1, n2 = _fft_sizes(2 * L)
    n1c, n2c = _fft_sizes(2 * Lc)
    tabs = _fft_tables(n1, n2)
    tabs_c = _fft_tables(n1c, n2c)
    ii = np.arange(GLA_CHUNK)
    tri_f = jnp.asarray((ii[None, :] <= ii[:, None]).astype(np.float32)).astype(BF16)
    tri_b = jnp.asarray((ii[None, :] >= ii[:, None]).astype(np.float32)).astype(BF16)
    zero_state = jnp.zeros((b_, GLA_DV, GLA_KW), F32)

    cmat = jnp.concatenate([c, c_ctx[None], jnp.zeros((8 - b_ - 1, d), F32)], axis=0)
    mods = _ada_call(cmat, w_ada, b_ada)

    for l in range(depth):
        need_ctx = l < depth - 1
        shift, scale, gate = [m[:b_, None, :] for m in jnp.split(mods[l], 3, axis=-1)]
        shift_c, scale_c, gate_c = [jnp.broadcast_to(m[b_:b_ + 1, None, :], (b_, 1, d))
                                    for m in jnp.split(mods[l], 3, axis=-1)]
        w_packed = _pack_w_in(w_in[l])
        P = _proj_call(x, scale, shift, w_packed)
        Pc = _proj_call(ctx, scale_c, shift_c, w_packed)

        rk = GLA_RANK
        wa_f = jnp.zeros((LANES, GLA_KW), F32).at[0:rk].set(gla_wa_f[l])
        wa_b = jnp.zeros((LANES, GLA_KW), F32).at[rk:2 * rk].set(gla_wa_b[l])
        ba_f, ba_b = gla_ba_f[l][None], gla_ba_b[l][None]
        nw = gla_norm[l][None]
        oc_f, sc_f = _gla_call(Pc, wa_f, ba_f, zero_state, tri_f, reverse=False)
        yc_gla, sc_b = _gla_call(Pc, wa_b, ba_b, zero_state, tri_b, reverse=True, fin=(oc_f, nw))
        o_f, _ = _gla_call(P, wa_f, ba_f, sc_f, tri_f, reverse=False)
        y_gla, _ = _gla_call(P, wa_b, ba_b, sc_b, tri_b, reverse=True, fin=(o_f, nw))

        gq = jnp.tile(attn_qnorm[l], ATTN_HEADS)[None]
        gk = jnp.tile(attn_knorm[l], ATTN_KV_HEADS)[None]
        q_a, k_a, v_a = _qkv_prep_call(P, cos_q, sin_q, cos_k, sin_k, gq, gk, bdq, bdk, eye_q, eye_v)
        qc_a, kc_a, vc_a = _qkv_prep_call(Pc, cos_qc, sin_qc, cos_kc, sin_kc, gq, gk, bdq, bdk,
                                          eye_q, eye_v)
        y_attn = _attn_call(q_a, jnp.concatenate([k_a, kc_a], axis=2),
                            jnp.concatenate([v_a, vc_a], axis=2), P, e_heads)

        fp = (hy_f1_w[l], hy_f1_b[l], hy_f1_freq[l], hy_f2_w[l], hy_f2_b[l], hy_f2_freq[l], hy_f3_w[l])
        g_spec = _hyena_filter_spectrum(L, tabs, n1, n2, fp)
        v0, x1, x2 = _short_conv_call(P, hy_conv_w[l], hy_conv_b[l][None])
        z1 = _hyena_conv(v0, x1, g_spec, 0, hy_skip[l, 0], tabs, n1, n2)
        y_hy = _hyena_conv(z1, x2, g_spec, 1, hy_skip[l, 1], tabs, n1, n2)

        wg, wa, wh, wo = (w_br_gla[l].astype(BF16), w_br_attn[l].astype(BF16),
                          w_br_hy[l].astype(BF16), w_out[l].astype(BF16))
        fn = final_norm[None]
        x_new = _merge_call(y_gla, y_attn, y_hy, P, x, gate, wg, wa, wh, wo, fn, final=not need_ctx)

        if need_ctx:
            yc_attn = _attn_call(qc_a, kc_a, vc_a, Pc, e_heads)
            gc_spec = _hyena_filter_spectrum(Lc, tabs_c, n1c, n2c, fp)
            vc0, xc1, xc2 = _short_conv_call(Pc, hy_conv_w[l], hy_conv_b[l][None])
            zc1 = _hyena_conv(vc0, xc1, gc_spec, 0, hy_skip[l, 0], tabs_c, n1c, n2c)
            yc_hy = _hyena_conv(zc1, xc2, gc_spec, 1, hy_skip[l, 1], tabs_c, n1c, n2c)
            ctx = _merge_call(yc_gla, yc_attn, yc_hy, Pc, ctx, gate_c, wg, wa, wh, wo, fn, final=False)
        x = x_new

    return x
```

```python
import functools
import math

import numpy as np
import jax
import jax.numpy as jnp
from jax import lax
from jax.experimental import pallas as pl
from jax.experimental.pallas import tpu as pltpu

F32 = jnp.float32
BF16 = jnp.bfloat16

D_MODEL = 1024
GRID_W = 64
BRANCH_W = D_MODEL // 2
N_BRANCH = 3
EPS = 1e-6
GLA_HEADS = 4
GLA_DV = BRANCH_W // GLA_HEADS
GLA_DK = GLA_DV // 2
GLA_KW = GLA_HEADS * GLA_DK
GLA_VW = GLA_HEADS * GLA_DV
GLA_RANK = 16
GLA_GATE_NORM = 16.0
GLA_CHUNK = 64
ATTN_HD = 64
ATTN_HEADS = BRANCH_W // ATTN_HD
ATTN_KV_HEADS = ATTN_HEADS // 4
ATTN_GROUP = ATTN_HEADS // ATTN_KV_HEADS
ATTN_QW = ATTN_HEADS * ATTN_HD
ATTN_KVW = ATTN_KV_HEADS * ATTN_HD
ROPE_THETA = 10000.0
HY_W = BRANCH_W
HY_ORDER = 2
HY_EMB = 33
HY_BANDS = (HY_EMB - 1) // 2
HY_FFN = 64
HY_SHORT = 3
HY_MOD_SHIFT = 0.05
HY_DECAY_SHORT_PCT = 0.3
HY_DECAY_LONG_PCT = 1.5
HY_DECAY_TARGET = 1e-2
N_FILT = 2 * HY_ORDER * HY_W
SPLITS = (GLA_KW, GLA_KW, GLA_VW, GLA_VW, GLA_RANK, GLA_RANK,
          ATTN_QW, ATTN_KVW, ATTN_KVW, ATTN_QW,
          (HY_ORDER + 1) * HY_W, HY_W,
          N_BRANCH * D_MODEL)

LANES = 128
VMEM_LIMIT = 48 * 1024 * 1024

COLS = {
    'm': (0, 3072), 'y_u': (3072, 1536), 'g_v': (4608, 512), 'g_z': (5120, 512),
    'a_q': (5632, 512), 'a_z': (6144, 512), 'y_z': (6656, 512), 'g_q': (7168, 256),
    'g_k': (7424, 256), 'a_k': (7680, 128), 'a_v': (7808, 128), 'g_a': (7936, 128),
}
N_PROJ = 8064
PROJ_TN = 2688


def _cparams(*sem):
    return pltpu.CompilerParams(dimension_semantics=sem, vmem_limit_bytes=VMEM_LIMIT)


def _split(a):
    hi = a.astype(BF16)
    lo = (a - hi.astype(F32)).astype(BF16)
    return hi, lo


def _dot(a, b):
    return jnp.dot(a, b, preferred_element_type=F32)


def _dot_hp(a, b):
    ah, al = _split(a)
    bh, bl = _split(b)
    return _dot(ah, bh) + _dot(al, bh) + _dot(ah, bl)


def _dot_mat(mh, ml, x):
    xh, xl = _split(x)
    return _dot(mh, xh) + _dot(mh, xl) + _dot(ml, xh)


def _sigmoid(x):
    return 1.0 / (1.0 + jnp.exp(-x))


def _silu(x):
    return x * _sigmoid(x)


def _ada_kernel(c_ref, w_ref, b_ref, o_ref):
    o_ref[0] = _dot_hp(_silu(c_ref[...]), w_ref[0]) + b_ref[0]


def _ada_call(cmat, w_ada, b_ada):
    depth, d, n3 = w_ada.shape
    tn = 1024
    return pl.pallas_call(
        _ada_kernel,
        grid=(depth, n3 // tn),
        in_specs=[pl.BlockSpec((8, d), lambda l, j: (0, 0)),
                  pl.BlockSpec((1, d, tn), lambda l, j: (l, 0, j)),
                  pl.BlockSpec((1, 1, tn), lambda l, j: (l, 0, j))],
        out_specs=pl.BlockSpec((1, 8, tn), lambda l, j: (l, 0, j)),
        out_shape=jax.ShapeDtypeStruct((depth, 8, n3), F32),
        compiler_params=_cparams("arbitrary", "arbitrary"),
        name="ada",
    )(cmat, w_ada, b_ada.reshape(depth, 1, n3))


def _proj_kernel(x_ref, sc_ref, sh_ref, w_ref, o_ref):
    x = x_ref[0]
    ms = jnp.mean(x * x, axis=-1, keepdims=True)
    h = x * lax.rsqrt(ms + EPS) * (1.0 + sc_ref[0]) + sh_ref[0]
    o_ref[0] = _dot(h.astype(BF16), w_ref[...]).astype(o_ref.dtype)


def _proj_call(x, scale, shift, w, out_dtype, tn):
    b_, L, d = x.shape
    n = w.shape[1]
    tm = min(512, L)
    return pl.pallas_call(
        _proj_kernel,
        grid=(n // tn, b_, L // tm),
        in_specs=[pl.BlockSpec((1, tm, d), lambda j, b, i: (b, i, 0)),
                  pl.BlockSpec((1, 1, d), lambda j, b, i: (b, 0, 0)),
                  pl.BlockSpec((1, 1, d), lambda j, b, i: (b, 0, 0)),
                  pl.BlockSpec((d, tn), lambda j, b, i: (0, j))],
        out_specs=pl.BlockSpec((1, tm, tn), lambda j, b, i: (b, i, j)),
        out_shape=jax.ShapeDtypeStruct((b_, L, n), out_dtype),
        compiler_params=_cparams("arbitrary", "arbitrary", "arbitrary"),
        name="proj",
    )(x, scale, shift, w)


def _gla_kernel(*refs, reverse, finalize, nchunks):
    if finalize:
        (q_ref, k_ref, v_ref, ga_ref, wa_ref, ba_ref, s0_ref, tri_ref, op_ref, z_ref, nw_ref,
         o_ref, sfin_ref, s_scr) = refs
    else:
        (q_ref, k_ref, v_ref, ga_ref, wa_ref, ba_ref, s0_ref, tri_ref,
         o_ref, sfin_ref, s_scr) = refs
    C = GLA_CHUNK

    @pl.when(pl.program_id(1) == 0)
    def _():
        s_scr[...] = s0_ref[0]

    xg = _dot_hp(ga_ref[0], wa_ref[...]) + ba_ref[...]
    g_all = (jnp.minimum(xg, 0.0) - jnp.log(1.0 + jnp.exp(-jnp.abs(xg)))) * (1.0 / GLA_GATE_NORM)

    tri = tri_ref[...]
    ri = lax.broadcasted_iota(jnp.int32, (C, C), 0)
    ci = lax.broadcasted_iota(jnp.int32, (C, C), 1)
    mask = (ci >= ri) if reverse else (ci <= ri)
    lane = lax.broadcasted_iota(jnp.int32, (1, GLA_KW), 1)
    hmask = [(lane >= h * GLA_DK) & (lane < (h + 1) * GLA_DK) for h in range(GLA_HEADS)]
    tot_row = 0 if reverse else C - 1
    mid_row = C // 2
    scale = GLA_DK ** -0.5
    order = range(nchunks - 1, -1, -1) if reverse else range(nchunks)

    for c in order:
        rows = slice(c * C, (c + 1) * C)
        g = g_all[rows]
        g1 = g.astype(BF16)
        r1 = g - g1.astype(F32)
        g2 = r1.astype(BF16)
        g3 = (r1 - g2.astype(F32)).astype(BF16)
        b = _dot(tri, g1) + _dot(tri, g2) + _dot(tri, g3)
        bm = b[mid_row:mid_row + 1]
        bt = b[tot_row:tot_row + 1]
        qc = q_ref[0, rows, :].astype(F32) * scale
        kc = k_ref[0, rows, :].astype(F32)
        vc = v_ref[0, rows, :].astype(BF16)
        qa = qc * jnp.exp(b - bm)
        kb = (kc * jnp.exp(bm - b)).astype(BF16)
        qe = qc * jnp.exp(b)
        kd = (kc * jnp.exp(bt - b)).astype(BF16)
        dec = jnp.exp(bt)
        st = s_scr[...]
        st_b = st.astype(BF16)
        upd = jnp.zeros_like(st)
        outs = []
        for h in range(GLA_HEADS):
            vh = vc[:, h * GLA_DV:(h + 1) * GLA_DV]
            qa_h = jnp.where(hmask[h], qa, 0.0).astype(BF16)
            qe_h = jnp.where(hmask[h], qe, 0.0).astype(BF16)
            att = lax.dot_general(qa_h, kb, (((1,), (1,)), ((), ())), preferred_element_type=F32)
            att = jnp.where(mask, att, 0.0).astype(BF16)
            o_h = _dot(att, vh) + lax.dot_general(qe_h, st_b, (((1,), (1,)), ((), ())),
                                                  preferred_element_type=F32)
            outs.append(o_h)
            u_h = lax.dot_general(vh, kd, (((0,), (0,)), ((), ())), preferred_element_type=F32)
            upd = upd + jnp.where(hmask[h], u_h, 0.0)
        s_scr[...] = dec * st + upd
        o = jnp.concatenate(outs, axis=1)
        if finalize:
            o = o + op_ref[0, rows, :]
            parts = []
            for h in range(GLA_HEADS):
                oh = o[:, h * GLA_DV:(h + 1) * GLA_DV]
                ms = jnp.mean(oh * oh, axis=-1, keepdims=True)
                parts.append(oh * lax.rsqrt(ms + EPS) * nw_ref[...])
            y = jnp.concatenate(parts, axis=1) * _silu(z_ref[0, rows, :].astype(F32))
            o_ref[0, rows, :] = y.astype(o_ref.dtype)
        else:
            o_ref[0, rows, :] = o

    @pl.when(pl.program_id(1) == pl.num_programs(1) - 1)
    def _():
        sfin_ref[0] = s_scr[...]


def _gla_call(P, GA, wa_pad, ba, s0, tri, reverse, fin=None):
    b_, L, _ = P.shape
    T = min(256, L)
    nt = L // T
    if reverse:
        tmap = lambda t: nt - 1 - t
    else:
        tmap = lambda t: t

    def col(name, width):
        blk = COLS[name][0] // width
        return pl.BlockSpec((1, T, width), lambda b, t: (b, tmap(t), blk))

    in_specs = [col('g_q', GLA_KW), col('g_k', GLA_KW), col('g_v', GLA_VW),
                pl.BlockSpec((1, T, LANES), lambda b, t: (b, tmap(t), 0)),
                pl.BlockSpec((LANES, GLA_KW), lambda b, t: (0, 0)),
                pl.BlockSpec((1, GLA_KW), lambda b, t: (0, 0)),
                pl.BlockSpec((1, GLA_DV, GLA_KW), lambda b, t: (b, 0, 0)),
                pl.BlockSpec((GLA_CHUNK, GLA_CHUNK), lambda b, t: (0, 0))]
    args = [P, P, P, GA, wa_pad, ba, s0, tri]
    if fin is not None:
        o_prev, nw = fin
        in_specs += [pl.BlockSpec((1, T, GLA_VW), lambda b, t: (b, tmap(t), 0)),
                     col('g_z', GLA_VW),
                     pl.BlockSpec((1, GLA_DV), lambda b, t: (0, 0))]
        args += [o_prev, P, nw]
    out_dtype = BF16 if fin is not None else F32
    return pl.pallas_call(
        functools.partial(_gla_kernel, reverse=reverse, finalize=fin is not None, nchunks=T // GLA_CHUNK),
        grid=(b_, nt),
        in_specs=in_specs,
        out_specs=[pl.BlockSpec((1, T, GLA_VW), lambda b, t: (b, tmap(t), 0)),
                   pl.BlockSpec((1, GLA_DV, GLA_KW), lambda b, t: (b, 0, 0))],
        out_shape=[jax.ShapeDtypeStruct((b_, L, GLA_VW), out_dtype),
                   jax.ShapeDtypeStruct((b_, GLA_DV, GLA_KW), F32)],
        scratch_shapes=[pltpu.VMEM((GLA_DV, GLA_KW), F32)],
        compiler_params=_cparams("arbitrary", "arbitrary"),
        name="gla",
    )(*args)


def _norm_rope(a, gain, bd, cos, sin):
    sq = a * a
    sh, sl = _split(sq)
    ss = _dot(sh, bd) + _dot(sl, bd)
    an = a * lax.rsqrt(ss * (1.0 / ATTN_HD) + EPS) * gain
    w = a.shape[-1]
    lane = lax.broadcasted_iota(jnp.int32, an.shape, 1)
    partner = jnp.where(lane % 2 == 0, pltpu.roll(an, w - 1, 1), pltpu.roll(an, 1, 1))
    return an * cos + partner * sin


Q_SCALE = (ATTN_HD ** -0.5) * math.log2(math.e)
ATTN_TK_MAX = 640


def _qkv_prep_kernel(q_ref, k_ref, v_ref, cq_ref, sq_ref, ck_ref, sk_ref, gq_ref, gk_ref,
                     bdq_ref, bdk_ref, eq_ref, ev_ref, qo_ref, ko_ref, vo_ref):
    nt = (((1,), (1,)), ((), ()))
    q = _norm_rope(q_ref[0].astype(F32), gq_ref[...], bdq_ref[...], cq_ref[...], sq_ref[...])
    qb = (q * Q_SCALE).astype(BF16)
    qo_ref[0] = lax.dot_general(eq_ref[...], qb, nt, preferred_element_type=F32).astype(BF16)
    k = _norm_rope(k_ref[0].astype(F32), gk_ref[...], bdk_ref[...], ck_ref[...], sk_ref[...]).astype(BF16)
    for g in range(ATTN_KV_HEADS):
        ko_ref[0, g] = k[:, g * ATTN_HD:(g + 1) * ATTN_HD]
    vb = v_ref[0].astype(BF16)
    vo_ref[0] = lax.dot_general(ev_ref[...], vb, nt, preferred_element_type=F32).astype(BF16)


def _qkv_prep_call(P, cos_q, sin_q, cos_k, sin_k, gq, gk, bdq, bdk, eye_q, eye_v):
    b_, L, _ = P.shape
    T = min(512, L)

    def col(name, width):
        blk = COLS[name][0] // width
        return pl.BlockSpec((1, T, width), lambda b, t: (b, t, blk))

    tab = lambda w: pl.BlockSpec((T, w), lambda b, t: (t, 0))
    const = lambda r, w: pl.BlockSpec((r, w), lambda b, t: (0, 0))
    return pl.pallas_call(
        _qkv_prep_kernel,
        grid=(b_, L // T),
        in_specs=[col('a_q', ATTN_QW), col('a_k', ATTN_KVW), col('a_v', ATTN_KVW),
                  tab(ATTN_QW), tab(ATTN_QW), tab(ATTN_KVW), tab(ATTN_KVW),
                  const(1, ATTN_QW), const(1, ATTN_KVW),
                  const(ATTN_QW, ATTN_QW), const(ATTN_KVW, ATTN_KVW),
                  const(ATTN_QW, ATTN_QW), const(ATTN_KVW, ATTN_KVW)],
        out_specs=[pl.BlockSpec((1, ATTN_QW, T), lambda b, t: (b, 0, t)),
                   pl.BlockSpec((1, ATTN_KV_HEADS, T, ATTN_HD), lambda b, t: (b, 0, t, 0)),
                   pl.BlockSpec((1, ATTN_KVW, T), lambda b, t: (b, 0, t))],
        out_shape=[jax.ShapeDtypeStruct((b_, ATTN_QW, L), BF16),
                   jax.ShapeDtypeStruct((b_, ATTN_KV_HEADS, L, ATTN_HD), BF16),
                   jax.ShapeDtypeStruct((b_, ATTN_KVW, L), BF16)],
        compiler_params=_cparams("arbitrary", "arbitrary"),
        name="qkv_prep",
    )(P, P, P, cos_q, sin_q, cos_k, sin_k, gq, gk, bdq, bdk, eye_q, eye_v)


ATTN_SUM_ROWS = 16


def _attn_kernel(qt_ref, k_ref, vt_ref, z_ref, e_ref, o_ref, m_scr, a_scr, acc_scr, s_scr, p_scr,
                 *, tk, nk):
    m_scr[...] = jnp.full(m_scr.shape, -jnp.inf, F32)
    acc_scr[...] = jnp.zeros(acc_scr.shape, F32)
    ones = jnp.ones((ATTN_SUM_ROWS, tk), BF16)

    def scores(j, slot):
        start = j * tk if isinstance(j, int) else pl.multiple_of(j * tk, tk)
        kj = k_ref[0, 0, pl.ds(start, tk), :]
        for h in range(ATTN_GROUP):
            s_scr[slot, h] = _dot(kj, qt_ref[0, h * ATTN_HD:(h + 1) * ATTN_HD, :])

    scores(0, 0)

    def step(j, slot):
        scores(min(j + 1, nk - 1) if isinstance(j, int) else jnp.minimum(j + 1, nk - 1), 1 - slot)
        start = j * tk if isinstance(j, int) else pl.multiple_of(j * tk, tk)
        vte = jnp.concatenate([vt_ref[0, :, pl.ds(start, tk)], ones], axis=0)
        for h in range(ATTN_GROUP):
            s = s_scr[slot, h]
            m_old = m_scr[h]
            m_new = jnp.maximum(m_old, jnp.max(s, axis=0, keepdims=True))
            a_scr[h] = jnp.exp2(m_old - m_new)
            m_scr[h] = m_new
            p_scr[h] = jnp.exp2(s - m_new).astype(BF16)
        for h in range(ATTN_GROUP):
            acc_scr[h] = a_scr[h] * acc_scr[h] + _dot(vte, p_scr[h])

    def pair(i, carry):
        step(2 * i, 0)
        step(2 * i + 1, 1)
        return carry

    lax.fori_loop(0, nk // 2, pair, 0)
    if nk % 2:
        step(nk - 1, 0)
    tn = (((0,), (0,)), ((), ()))
    out = None
    for h in range(ATTN_GROUP):
        acc = acc_scr[h]
        oh, ol = _split(acc[:ATTN_HD] / acc[ATTN_HD:ATTN_HD + 1])
        part = (lax.dot_general(oh, e_ref[h], tn, preferred_element_type=F32)
                + lax.dot_general(ol, e_ref[h], tn, preferred_element_type=F32))
        out = part if out is None else out + part
    o_ref[0] = (out * _silu(z_ref[0].astype(F32))).astype(o_ref.dtype)


def _attn_call(qt, k, vt, P, e_heads):
    b_, _, L = qt.shape
    Lk = k.shape[2]
    tq = min(256, L)
    tk = max(t for t in range(LANES, ATTN_TK_MAX + 1, LANES) if Lk % t == 0)
    gw = ATTN_GROUP * ATTN_HD
    zblk = COLS['a_z'][0] // gw
    return pl.pallas_call(
        functools.partial(_attn_kernel, tk=tk, nk=Lk // tk),
        grid=(b_, ATTN_KV_HEADS, L // tq),
        in_specs=[pl.BlockSpec((1, gw, tq), lambda b, g, i: (b, g, i)),
                  pl.BlockSpec((1, 1, Lk, ATTN_HD), lambda b, g, i: (b, g, 0, 0)),
                  pl.BlockSpec((1, ATTN_HD, Lk), lambda b, g, i: (b, g, 0)),
                  pl.BlockSpec((1, tq, gw), lambda b, g, i: (b, i, zblk + g)),
                  pl.BlockSpec((ATTN_GROUP, ATTN_HD, gw), lambda b, g, i: (0, 0, 0))],
        out_specs=pl.BlockSpec((1, tq, gw), lambda b, g, i: (b, i, g)),
        out_shape=jax.ShapeDtypeStruct((b_, L, ATTN_QW), BF16),
        scratch_shapes=[pltpu.VMEM((ATTN_GROUP, 1, tq), F32), pltpu.VMEM((ATTN_GROUP, 1, tq), F32),
                        pltpu.VMEM((ATTN_GROUP, ATTN_HD + ATTN_SUM_ROWS, tq), F32),
                        pltpu.VMEM((2, ATTN_GROUP, tk, tq), F32), pltpu.VMEM((ATTN_GROUP, tk, tq), BF16)],
        compiler_params=_cparams("arbitrary", "arbitrary", "arbitrary"),
        name="attn",
    )(qt, k, vt, P, e_heads)


def _short_conv_kernel(u_ref, up_ref, un_ref, w_ref, b_ref, v_ref, x1_ref, x2_ref):
    t = pl.program_id(1)
    nt = pl.num_programs(1)
    u = u_ref[0].astype(F32)
    T = u.shape[0]
    hr = up_ref.shape[1]
    prev_row = jnp.where(t > 0, up_ref[0, hr - 1:hr, :].astype(F32), 0.0)
    next_row = jnp.where(t < nt - 1, un_ref[0, 0:1, :].astype(F32), 0.0)
    row = lax.broadcasted_iota(jnp.int32, u.shape, 0)
    u_m1 = jnp.where(row == 0, prev_row, pltpu.roll(u, 1, 0))
    u_p1 = jnp.where(row == T - 1, next_row, pltpu.roll(u, T - 1, 0))
    w = w_ref[...]
    out = b_ref[...] + u_m1 * w[0:1] + u * w[1:2] + u_p1 * w[2:3]
    v_ref[0] = out[:, 0:HY_W]
    x1_ref[0] = out[:, HY_W:2 * HY_W]
    x2_ref[0] = out[:, 2 * HY_W:3 * HY_W]


def _short_conv_call(P, w, bias):
    b_, L, _ = P.shape
    T = min(512, L)
    cw = (HY_ORDER + 1) * HY_W
    blk = COLS['y_u'][0] // cw
    hr = 16
    hb = T // hr
    nh = L // hr
    out_spec = pl.BlockSpec((1, T, HY_W), lambda b, t: (b, t, 0))
    shp = jax.ShapeDtypeStruct((b_, L, HY_W), F32)
    return pl.pallas_call(
        _short_conv_kernel,
        grid=(b_, L // T),
        in_specs=[pl.BlockSpec((1, T, cw), lambda b, t: (b, t, blk)),
                  pl.BlockSpec((1, hr, cw), lambda b, t: (b, jnp.maximum(t * hb - 1, 0), blk)),
                  pl.BlockSpec((1, hr, cw), lambda b, t: (b, jnp.minimum((t + 1) * hb, nh - 1), blk)),
                  pl.BlockSpec((HY_SHORT, cw), lambda b, t: (0, 0)),
                  pl.BlockSpec((1, cw), lambda b, t: (0, 0))],
        out_specs=[out_spec, out_spec, out_spec],
        out_shape=[shp, shp, shp],
        compiler_params=_cparams("arbitrary", "arbitrary"),
        name="short_conv",
    )(P, P, P, w, bias)


def _filter_kernel(emb_ref, w1_ref, b1_ref, f1_ref, w2_ref, b2_ref, f2_ref, w3_ref, dl_ref,
                   h_ref, sum_ref):
    t = pl.program_id(0)
    emb = emb_ref[...]
    h = jnp.sin(f1_ref[...] * (_dot_hp(emb, w1_ref[...]) + b1_ref[...]))
    h = jnp.sin(f2_ref[...] * (_dot_hp(h, w2_ref[...]) + b2_ref[...]))
    h = _dot_hp(h, w3_ref[...])
    tt = emb[:, 0:1]
    h = h * (jnp.exp(-tt * dl_ref[...]) + HY_MOD_SHIFT)

    @pl.when(t == 0)
    def _():
        sum_ref[...] = jnp.zeros_like(sum_ref)

    sum_ref[...] += jnp.broadcast_to(jnp.sum(jnp.abs(h), axis=0, keepdims=True), sum_ref.shape)
    row = lax.broadcasted_iota(jnp.int32, h.shape, 0) + t * h.shape[0]
    colm = lax.broadcasted_iota(jnp.int32, h.shape, 1)
    bwd = (colm // HY_W) % 2 == 1
    h_ref[...] = jnp.where(bwd & (row == 0), 0.0, h)


def _filter_call(emb, w1, b1, f1, w2, b2, f2, w3, deltas):
    L = emb.shape[0]
    TL = min(256, L)
    const = lambda r, w: pl.BlockSpec((r, w), lambda t: (0, 0))
    return pl.pallas_call(
        _filter_kernel,
        grid=(L // TL,),
        in_specs=[pl.BlockSpec((TL, LANES), lambda t: (t, 0)),
                  const(LANES, HY_FFN), const(1, HY_FFN), const(1, HY_FFN),
                  const(HY_FFN, HY_FFN), const(1, HY_FFN), const(1, HY_FFN),
                  const(HY_FFN, N_FILT), const(1, N_FILT)],
        out_specs=[pl.BlockSpec((TL, N_FILT), lambda t: (t, 0)),
                   pl.BlockSpec((8, N_FILT), lambda t: (0, 0))],
        out_shape=[jax.ShapeDtypeStruct((L, N_FILT), F32),
                   jax.ShapeDtypeStruct((8, N_FILT), F32)],
        compiler_params=_cparams("arbitrary"),
        name="hy_filter",
    )(emb, w1, b1, f1, w2, b2, f2, w3, deltas)


def _fft_sizes(n):
    lg = int(round(math.log2(n)))
    assert 1 << lg == n
    n1 = 1 << ((lg + 1) // 2)
    return n1, n // n1


def _bf16_pair(a):
    a = np.asarray(a, np.float32)
    hi = jnp.asarray(a, F32).astype(BF16)
    lo = (jnp.asarray(a, F32) - hi.astype(F32)).astype(BF16)
    return hi, lo


def _fft_tables(n1, n2):
    n = n1 * n2
    h1 = n1 // 2
    k1 = np.arange(n1)[:, None]
    a = 2.0 * np.pi * ((k1 * np.arange(h1)[None, :]) % n1) / n1
    c, s = np.cos(a), np.sin(a)
    fa = np.block([[c, s], [-s, c]])
    fa_real = np.concatenate([c, -s], axis=0)
    fd = np.block([[c.T, -s.T], [s.T, c.T]])
    k2 = np.arange(n2)[:, None]
    b = 2.0 * np.pi * ((k2 * np.arange(n2)[None, :]) % n2) / n2
    cb, sb = np.cos(b), np.sin(b)
    fb = np.block([[cb, sb], [-sb, cb]])
    fc = np.block([[cb, -sb], [sb, cb]])
    kn = (jnp.arange(n1, dtype=jnp.int32)[:, None] * jnp.arange(n2, dtype=jnp.int32)[None, :]) % n
    tw = kn.astype(F32) * (2.0 * math.pi / n)
    twr = jnp.broadcast_to(jnp.cos(tw)[:, :, None], (n1, n2, LANES))
    twi = jnp.broadcast_to(-jnp.sin(tw)[:, :, None], (n1, n2, LANES))
    return dict(fa=_bf16_pair(fa), fa_real=_bf16_pair(fa_real), fd=_bf16_pair(fd),
                fb=_bf16_pair(fb), fc=_bf16_pair(fc), twr=twr, twi=twi)


def _fa_kernel(x_ref, mh_ref, ml_ref, o_ref, *, packed):
    if packed:
        x = x_ref[...]
        x = x.reshape(x.shape[0] * x.shape[1], x.shape[2])
    else:
        x = x_ref[...]
    o_ref[...] = _dot_mat(mh_ref[...], ml_ref[...], x)


def _fa_call(x, mats, n1, packed):
    mh, ml = mats
    wtot = x.shape[-1]
    W = min(2048, wtot)
    h1 = n1 // 2
    if packed:
        x_spec = pl.BlockSpec((2, h1, W), lambda j: (0, 0, j))
    else:
        x_spec = pl.BlockSpec((h1, W), lambda j: (0, j))
    return pl.pallas_call(
        functools.partial(_fa_kernel, packed=packed),
        grid=(wtot // W,),
        in_specs=[x_spec, pl.BlockSpec(mh.shape, lambda j: (0, 0)), pl.BlockSpec(ml.shape, lambda j: (0, 0))],
        out_specs=pl.BlockSpec((2 * n1, W), lambda j: (0, j)),
        out_shape=jax.ShapeDtypeStruct((2 * n1, wtot), F32),
        compiler_params=_cparams("arbitrary"),
        name="fft_a",
    )(x, mh, ml)


def _lane_tile(a, width):
    return jnp.concatenate([a] * (width // a.shape[-1]), axis=-1)


def _fb_filter_kernel(t_ref, twr_ref, twi_ref, fh_ref, fl_ref, inv_ref, g_ref, *, n2, scale):
    w = t_ref.shape[-1]
    tr = t_ref[0, 0]
    ti = t_ref[1, 0]
    twr = _lane_tile(twr_ref[0], w)
    twi = _lane_tile(twi_ref[0], w)
    p = jnp.concatenate([tr * twr - ti * twi, tr * twi + ti * twr], axis=0)
    z = _dot_mat(fh_ref[...], fl_ref[...], p) * (inv_ref[0:1, :] * scale)
    zr, zi = z[:n2], z[n2:]
    for o in range(HY_ORDER):
        f0 = (2 * o) * HY_W
        b0 = (2 * o + 1) * HY_W
        g_ref[o, 0, 0] = zr[:, f0:f0 + HY_W] + zr[:, b0:b0 + HY_W]
        g_ref[o, 1, 0] = zi[:, f0:f0 + HY_W] - zi[:, b0:b0 + HY_W]


def _fb_filter_call(t, tabs, inv_sums, n1, n2):
    fh, fl = tabs['fb']
    t4 = t.reshape(2, n1, n2, N_FILT)
    return pl.pallas_call(
        functools.partial(_fb_filter_kernel, n2=n2, scale=1.0 / (n1 * n2)),
        grid=(n1,),
        in_specs=[pl.BlockSpec((2, 1, n2, N_FILT), lambda k: (0, k, 0, 0)),
                  pl.BlockSpec((1, n2, LANES), lambda k: (k, 0, 0)),
                  pl.BlockSpec((1, n2, LANES), lambda k: (k, 0, 0)),
                  pl.BlockSpec(fh.shape, lambda k: (0, 0)),
                  pl.BlockSpec(fl.shape, lambda k: (0, 0)),
                  pl.BlockSpec((8, N_FILT), lambda k: (0, 0))],
        out_specs=pl.BlockSpec((HY_ORDER, 2, 1, n2, HY_W), lambda k: (0, 0, k, 0, 0)),
        out_shape=jax.ShapeDtypeStruct((HY_ORDER, 2, n1, n2, HY_W), F32),
        compiler_params=_cparams("arbitrary"),
        name="fft_b_filter",
    )(t4, tabs['twr'], tabs['twi'], fh, fl, inv_sums)


def _fb_kernel(t_ref, twr_ref, twi_ref, fh_ref, fl_ref, ch_ref, cl_ref, g_ref, o_ref, *, n2):
    w = t_ref.shape[-1]
    tr = t_ref[0, 0]
    ti = t_ref[1, 0]
    twr = _lane_tile(twr_ref[0], w)
    twi = _lane_tile(twi_ref[0], w)
    p = jnp.concatenate([tr * twr - ti * twi, tr * twi + ti * twr], axis=0)
    z = _dot_mat(fh_ref[...], fl_ref[...], p)
    zr, zi = z[:n2], z[n2:]
    gr = g_ref[0, 0, 0]
    gi = g_ref[0, 1, 0]
    y = jnp.concatenate([zr * gr - zi * gi, zr * gi + zi * gr], axis=0)
    v = _dot_mat(ch_ref[...], cl_ref[...], y)
    vr, vi = v[:n2], v[n2:]
    o_ref[0, 0] = vr * twr + vi * twi
    o_ref[1, 0] = vi * twr - vr * twi


def _fb_call(t, g, order, tabs, n1, n2):
    fh, fl = tabs['fb']
    ch, cl = tabs['fc']
    t4 = t.reshape(2, n1, n2, HY_W)
    mat = lambda m: pl.BlockSpec(m.shape, lambda k: (0, 0))
    out = pl.pallas_call(
        functools.partial(_fb_kernel, n2=n2),
        grid=(n1,),
        in_specs=[pl.BlockSpec((2, 1, n2, HY_W), lambda k: (0, k, 0, 0)),
                  pl.BlockSpec((1, n2, LANES), lambda k: (k, 0, 0)),
                  pl.BlockSpec((1, n2, LANES), lambda k: (k, 0, 0)),
                  mat(fh), mat(fl), mat(ch), mat(cl),
                  pl.BlockSpec((1, 2, 1, n2, HY_W), lambda k: (order, 0, k, 0, 0))],
        out_specs=pl.BlockSpec((2, 1, n2, HY_W), lambda k: (0, k, 0, 0)),
        out_shape=jax.ShapeDtypeStruct((2, n1, n2, HY_W), F32),
        compiler_params=_cparams("arbitrary"),
        name="fft_b",
    )(t4, tabs['twr'], tabs['twi'], fh, fl, ch, cl, g)
    return out.reshape(2 * n1, n2 * HY_W)


def _fd_kernel(u_ref, mh_ref, ml_ref, z_ref, x_ref, sk_ref, o_ref):
    y = _dot_mat(mh_ref[...], ml_ref[...], u_ref[...])
    y = y.reshape(o_ref.shape)
    o_ref[...] = x_ref[...] * (y + sk_ref[...] * z_ref[...])


def _fd_call(u, mats, z, xg, skip_row, n1):
    mh, ml = mats
    b_, h1, wtot = z.shape
    W = min(2048, wtot)
    dat = pl.BlockSpec((b_, h1, W), lambda j: (0, 0, j))
    return pl.pallas_call(
        _fd_kernel,
        grid=(wtot // W,),
        in_specs=[pl.BlockSpec((2 * n1, W), lambda j: (0, j)),
                  pl.BlockSpec(mh.shape, lambda j: (0, 0)), pl.BlockSpec(ml.shape, lambda j: (0, 0)),
                  dat, dat, pl.BlockSpec((1, W), lambda j: (0, 0))],
        out_specs=dat,
        out_shape=jax.ShapeDtypeStruct((b_, h1, wtot), F32),
        compiler_params=_cparams("arbitrary"),
        name="fft_d",
    )(u, mh, ml, z, xg, skip_row)


def _hyena_filter_spectrum(L, tabs, n1, n2, fp):
    f32 = F32
    t = jnp.linspace(0.0, 1.0, L, dtype=f32)[:, None]
    w = 2.0 * math.pi * jnp.arange(L, dtype=f32)[:, None] / L
    fr = jnp.linspace(1e-4, HY_BANDS - 1, HY_BANDS, dtype=f32)[None]
    emb = jnp.concatenate([t, jnp.cos(fr * w), -jnp.sin(fr * w)], axis=-1)
    emb = jnp.pad(emb, ((0, 0), (0, LANES - HY_EMB)))
    deltas = jnp.abs(jnp.linspace(math.log(HY_DECAY_TARGET) / HY_DECAY_SHORT_PCT,
                                  math.log(HY_DECAY_TARGET) / HY_DECAY_LONG_PCT, N_FILT, dtype=f32))[None]
    f1_w, f1_b, f1_freq, f2_w, f2_b, f2_freq, f3_w = fp
    w1 = jnp.pad(f1_w, ((0, LANES - HY_EMB), (0, 0)))
    h, sums = _filter_call(emb, w1, f1_b[None], f1_freq[None], f2_w, f2_b[None], f2_freq[None], f3_w, deltas)
    inv_sums = 1.0 / sums
    hv = h.reshape(n1 // 2, n2 * N_FILT)
    th = _fa_call(hv, tabs['fa_real'], n1, packed=False)
    return _fb_filter_call(th, tabs, inv_sums, n1, n2)


def _hyena_conv(z, xg, g, order, skip, tabs, n1, n2):
    b_, L, wd = z.shape
    assert b_ == 2, "the two batch rows ride as the real and imaginary parts of one transform"
    zv = z.reshape(b_, n1 // 2, n2 * wd)
    xv = xg.reshape(b_, n1 // 2, n2 * wd)
    t = _fa_call(zv, tabs['fa'], n1, packed=True)
    u = _fb_call(t, g, order, tabs, n1, n2)
    wtile = min(2048, n2 * wd)
    skip_row = jnp.tile(skip[None, :], (1, wtile // wd))
    out = _fd_call(u, tabs['fd'], zv, xv, skip_row, n1)
    return out.reshape(b_, L, wd)


def _merge_kernel(yg_ref, ya_ref, yh_ref, yz_ref, mg_ref, ma_ref, mh_ref, x_ref, gate_ref,
                  wg_ref, wa_ref, wh_ref, wo_ref, fn_ref, o_ref, *, final):
    yh = (yh_ref[0] * _silu(yz_ref[0].astype(F32))).astype(BF16)
    y = (_sigmoid(mg_ref[0].astype(F32)) * _dot(yg_ref[0], wg_ref[...])
         + _sigmoid(ma_ref[0].astype(F32)) * _dot(ya_ref[0], wa_ref[...])
         + _sigmoid(mh_ref[0].astype(F32)) * _dot(yh, wh_ref[...]))
    out = _dot(y.astype(BF16), wo_ref[...])
    xn = x_ref[0] + gate_ref[0] * out
    if final:
        ms = jnp.mean(xn * xn, axis=-1, keepdims=True)
        xn = xn * lax.rsqrt(ms + EPS) * fn_ref[...]
    o_ref[0] = xn


def _merge_call(y_gla, y_attn, y_hy, P, x, gate, wg, wa, wh, wo, fnorm, final):
    b_, L, d = x.shape
    T = min(512, L)
    row = lambda w, blk=0: pl.BlockSpec((1, T, w), lambda b, t: (b, t, blk))
    const = lambda r, w: pl.BlockSpec((r, w), lambda b, t: (0, 0))
    mblk = COLS['m'][0] // d
    return pl.pallas_call(
        functools.partial(_merge_kernel, final=final),
        grid=(b_, L // T),
        in_specs=[row(BRANCH_W), row(BRANCH_W), row(BRANCH_W), row(HY_W, COLS['y_z'][0] // HY_W),
                  row(d, mblk), row(d, mblk + 1), row(d, mblk + 2), row(d),
                  pl.BlockSpec((1, 1, d), lambda b, t: (b, 0, 0)),
                  const(BRANCH_W, d), const(BRANCH_W, d), const(BRANCH_W, d), const(d, d), const(1, d)],
        out_specs=row(d),
        out_shape=jax.ShapeDtypeStruct((b_, L, d), F32),
        compiler_params=_cparams("arbitrary", "arbitrary"),
        name="merge",
    )(y_gla, y_attn, y_hy, P, P, P, P, x, gate, wg, wa, wh, wo, fnorm)


def _pack_w_in(w):
    parts = dict(zip(('g_q', 'g_k', 'g_v', 'g_z', 'g_af', 'g_ab', 'a_q', 'a_k', 'a_v', 'a_z', 'y_u', 'y_z', 'm'),
                     jnp.split(w, [int(i) for i in np.cumsum(SPLITS)[:-1]], axis=-1)))
    parts['g_a'] = jnp.pad(jnp.concatenate([parts['g_af'], parts['g_ab']], axis=-1),
                           ((0, 0), (0, LANES - 2 * GLA_RANK)))
    order = sorted(COLS, key=lambda n: COLS[n][0])
    return jnp.concatenate([parts[n] for n in order], axis=-1).astype(BF16)


def _rope_tables(L, heads):
    t = jnp.arange(L)
    row = (t // GRID_W).astype(F32)
    colp = (t % GRID_W).astype(F32)
    half = ATTN_HD // 2
    inv = ROPE_THETA ** (-jnp.arange(0, half, 2, dtype=F32) / half)
    ang = jnp.concatenate([row[:, None] * inv, colp[:, None] * inv], axis=-1)
    cos = jnp.repeat(jnp.cos(ang), 2, axis=-1)
    sin = jnp.stack([-jnp.sin(ang), jnp.sin(ang)], axis=-1).reshape(L, ATTN_HD)
    return jnp.tile(cos, (1, heads)), jnp.tile(sin, (1, heads))


def _identity_rope(L, heads):
    return jnp.ones((L, heads * ATTN_HD), F32), jnp.zeros((L, heads * ATTN_HD), F32)


def _block_diag_ones(width):
    i = np.arange(width) // ATTN_HD
    return jnp.asarray((i[:, None] == i[None, :]).astype(np.float32)).astype(BF16)


def kernel(x, c, ctx, c_ctx, w_ada, b_ada, w_in, gla_wa_f, gla_ba_f, gla_wa_b, gla_ba_b, gla_norm,
           attn_qnorm, attn_knorm, hy_conv_w, hy_conv_b, hy_f1_w, hy_f1_b, hy_f1_freq, hy_f2_w,
           hy_f2_b, hy_f2_freq, hy_f3_w, hy_skip, w_br_gla, w_br_attn, w_br_hy, w_out, final_norm):
    b_, L, d = x.shape
    Lc = ctx.shape[1]
    depth = w_ada.shape[0]

    cos_q, sin_q = _rope_tables(L, ATTN_HEADS)
    cos_k, sin_k = cos_q[:, :ATTN_KVW], sin_q[:, :ATTN_KVW]
    cos_qc, sin_qc = _identity_rope(Lc, ATTN_HEADS)
    cos_kc, sin_kc = cos_qc[:, :ATTN_KVW], sin_qc[:, :ATTN_KVW]
    bdq, bdk = _block_diag_ones(ATTN_QW), _block_diag_ones(ATTN_KVW)
    eye_q, eye_v = jnp.eye(ATTN_QW, dtype=BF16), jnp.eye(ATTN_KVW, dtype=BF16)
    gw = ATTN_GROUP * ATTN_HD
    e_heads = jnp.stack([jnp.eye(ATTN_HD, gw, k=h * ATTN_HD, dtype=BF16) for h in range(ATTN_GROUP)])
    n1, n2 = _fft_sizes(2 * L)
    n1c, n2c = _fft_sizes(2 * Lc)
    tabs = _fft_tables(n1, n2)
    tabs_c = _fft_tables(n1c, n2c)
    ii = np.arange(GLA_CHUNK)
    tri_f = jnp.asarray((ii[None, :] <= ii[:, None]).astype(np.float32)).astype(BF16)
    tri_b = jnp.asarray((ii[None, :] >= ii[:, None]).astype(np.float32)).astype(BF16)
    zero_state = jnp.zeros((b_, GLA_DV, GLA_KW), F32)

    cmat = jnp.concatenate([c, c_ctx[None], jnp.zeros((8 - b_ - 1, d), F32)], axis=0)
    mods = _ada_call(cmat, w_ada, b_ada)

    for l in range(depth):
        need_ctx = l < depth - 1
        shift, scale, gate = [m[:b_, None, :] for m in jnp.split(mods[l], 3, axis=-1)]
        shift_c, scale_c, gate_c = [jnp.broadcast_to(m[b_:b_ + 1, None, :], (b_, 1, d))
                                    for m in jnp.split(mods[l], 3, axis=-1)]
        w_packed = _pack_w_in(w_in[l])
        P = _proj_call(x, scale, shift, w_packed, BF16, PROJ_TN)
        Pc = _proj_call(ctx, scale_c, shift_c, w_packed, BF16, PROJ_TN)
        ga0 = COLS['g_a'][0]
        w_ga = w_packed[:, ga0:ga0 + LANES]
        GA = _proj_call(x, scale, shift, w_ga, F32, LANES)
        GAc = _proj_call(ctx, scale_c, shift_c, w_ga, F32, LANES)

        rk = GLA_RANK
        wa_f = jnp.zeros((LANES, GLA_KW), F32).at[0:rk].set(gla_wa_f[l])
        wa_b = jnp.zeros((LANES, GLA_KW), F32).at[rk:2 * rk].set(gla_wa_b[l])
        ba_f, ba_b = gla_ba_f[l][None], gla_ba_b[l][None]
        nw = gla_norm[l][None]
        oc_f, sc_f = _gla_call(Pc, GAc, wa_f, ba_f, zero_state, tri_f, reverse=False)
        yc_gla, sc_b = _gla_call(Pc, GAc, wa_b, ba_b, zero_state, tri_b, reverse=True, fin=(oc_f, nw))
        o_f, _ = _gla_call(P, GA, wa_f, ba_f, sc_f, tri_f, reverse=False)
        y_gla, _ = _gla_call(P, GA, wa_b, ba_b, sc_b, tri_b, reverse=True, fin=(o_f, nw))

        gq = jnp.tile(attn_qnorm[l], ATTN_HEADS)[None]
        gk = jnp.tile(attn_knorm[l], ATTN_KV_HEADS)[None]
        q_a, k_a, v_a = _qkv_prep_call(P, cos_q, sin_q, cos_k, sin_k, gq, gk, bdq, bdk, eye_q, eye_v)
        qc_a, kc_a, vc_a = _qkv_prep_call(Pc, cos_qc, sin_qc, cos_kc, sin_kc, gq, gk, bdq, bdk,
                                          eye_q, eye_v)
        y_attn = _attn_call(q_a, jnp.concatenate([k_a, kc_a], axis=2),
                            jnp.concatenate([v_a, vc_a], axis=2), P, e_heads)

        fp = (hy_f1_w[l], hy_f1_b[l], hy_f1_freq[l], hy_f2_w[l], hy_f2_b[l], hy_f2_freq[l], hy_f3_w[l])
        g_spec = _hyena_filter_spectrum(L, tabs, n1, n2, fp)
        v0, x1, x2 = _short_conv_call(P, hy_conv_w[l], hy_conv_b[l][None])
        z1 = _hyena_conv(v0, x1, g_spec, 0, hy_skip[l, 0], tabs, n1, n2)
        y_hy = _hyena_conv(z1, x2, g_spec, 1, hy_skip[l, 1], tabs, n1, n2)

        wg, wa, wh, wo = (w_br_gla[l].astype(BF16), w_br_attn[l].astype(BF16),
                          w_br_hy[l].astype(BF16), w_out[l].astype(BF16))
        fn = final_norm[None]
        x_new = _merge_call(y_gla, y_attn, y_hy, P, x, gate, wg, wa, wh, wo, fn, final=not need_ctx)

        if need_ctx:
            yc_attn = _attn_call(qc_a, kc_a, vc_a, Pc, e_heads)
            gc_spec = _hyena_filter_spectrum(Lc, tabs_c, n1c, n2c, fp)
            vc0, xc1, xc2 = _short_conv_call(Pc, hy_conv_w[l], hy_conv_b[l][None])
            zc1 = _hyena_conv(vc0, xc1, gc_spec, 0, hy_skip[l, 0], tabs_c, n1c, n2c)
            yc_hy = _hyena_conv(zc1, xc2, gc_spec, 1, hy_skip[l, 1], tabs_c, n1c, n2c)
            ctx = _merge_call(yc_gla, yc_attn, yc_hy, Pc, ctx, gate_c, wg, wa, wh, wo, fn, final=False)
        x = x_new

    return x
```

```python
import functools
import math

import numpy as np
import jax
import jax.numpy as jnp
from jax import lax
from jax.experimental import pallas as pl
from jax.experimental.pallas import tpu as pltpu

F32 = jnp.float32
BF16 = jnp.bfloat16

D_MODEL = 1024
GRID_W = 64
BRANCH_W = D_MODEL // 2
N_BRANCH = 3
EPS = 1e-6
GLA_HEADS = 4
GLA_DV = BRANCH_W // GLA_HEADS
GLA_DK = GLA_DV // 2
GLA_KW = GLA_HEADS * GLA_DK
GLA_VW = GLA_HEADS * GLA_DV
GLA_RANK = 16
GLA_GATE_NORM = 16.0
GLA_CHUNK = 64
GLA_BLOCK = 256
ATTN_HD = 64
ATTN_HEADS = BRANCH_W // ATTN_HD
ATTN_KV_HEADS = ATTN_HEADS // 4
ATTN_GROUP = ATTN_HEADS // ATTN_KV_HEADS
ATTN_QW = ATTN_HEADS * ATTN_HD
ATTN_KVW = ATTN_KV_HEADS * ATTN_HD
ROPE_THETA = 10000.0
HY_W = BRANCH_W
HY_ORDER = 2
HY_EMB = 33
HY_BANDS = (HY_EMB - 1) // 2
HY_FFN = 64
HY_SHORT = 3
HY_MOD_SHIFT = 0.05
HY_DECAY_SHORT_PCT = 0.3
HY_DECAY_LONG_PCT = 1.5
HY_DECAY_TARGET = 1e-2
N_FILT = 2 * HY_ORDER * HY_W
SPLITS = (GLA_KW, GLA_KW, GLA_VW, GLA_VW, GLA_RANK, GLA_RANK,
          ATTN_QW, ATTN_KVW, ATTN_KVW, ATTN_QW,
          (HY_ORDER + 1) * HY_W, HY_W,
          N_BRANCH * D_MODEL)

LANES = 128
VMEM_LIMIT = 48 * 1024 * 1024

COLS = {
    'm': (0, 3072), 'y_u': (3072, 1536), 'g_v': (4608, 512), 'g_z': (5120, 512),
    'a_q': (5632, 512), 'a_z': (6144, 512), 'y_z': (6656, 512), 'g_q': (7168, 256),
    'g_k': (7424, 256), 'a_k': (7680, 128), 'a_v': (7808, 128), 'g_a': (7936, 128),
}
N_PROJ = 8064
PROJ_TN = 2688


def _cparams(*sem):
    return pltpu.CompilerParams(dimension_semantics=sem, vmem_limit_bytes=VMEM_LIMIT)


def _split(a):
    hi = a.astype(BF16)
    lo = (a - hi.astype(F32)).astype(BF16)
    return hi, lo


def _dot(a, b):
    return jnp.dot(a, b, preferred_element_type=F32)


def _dot_hp(a, b):
    ah, al = _split(a)
    bh, bl = _split(b)
    return _dot(ah, bh) + _dot(al, bh) + _dot(ah, bl)


def _dot_mat(mh, ml, x):
    xh, xl = _split(x)
    return _dot(mh, xh) + _dot(mh, xl) + _dot(ml, xh)


def _sigmoid(x):
    return 1.0 / (1.0 + jnp.exp(-x))


def _silu(x):
    return x * _sigmoid(x)


def _ada_kernel(c_ref, w_ref, b_ref, o_ref):
    o_ref[0] = _dot_hp(_silu(c_ref[...]), w_ref[0]) + b_ref[0]


def _ada_call(cmat, w_ada, b_ada):
    depth, d, n3 = w_ada.shape
    tn = 1024
    return pl.pallas_call(
        _ada_kernel,
        grid=(depth, n3 // tn),
        in_specs=[pl.BlockSpec((8, d), lambda l, j: (0, 0)),
                  pl.BlockSpec((1, d, tn), lambda l, j: (l, 0, j)),
                  pl.BlockSpec((1, 1, tn), lambda l, j: (l, 0, j))],
        out_specs=pl.BlockSpec((1, 8, tn), lambda l, j: (l, 0, j)),
        out_shape=jax.ShapeDtypeStruct((depth, 8, n3), F32),
        compiler_params=_cparams("arbitrary", "arbitrary"),
        name="ada",
    )(cmat, w_ada, b_ada.reshape(depth, 1, n3))


def _proj_kernel(x_ref, sc_ref, sh_ref, w_ref, o_ref):
    x = x_ref[0]
    ms = jnp.mean(x * x, axis=-1, keepdims=True)
    h = x * lax.rsqrt(ms + EPS) * (1.0 + sc_ref[0]) + sh_ref[0]
    o_ref[0] = _dot(h.astype(BF16), w_ref[...]).astype(o_ref.dtype)


def _proj_call(x, scale, shift, w, out_dtype, tn):
    b_, L, d = x.shape
    n = w.shape[1]
    tm = min(512, L)
    return pl.pallas_call(
        _proj_kernel,
        grid=(n // tn, b_, L // tm),
        in_specs=[pl.BlockSpec((1, tm, d), lambda j, b, i: (b, i, 0)),
                  pl.BlockSpec((1, 1, d), lambda j, b, i: (b, 0, 0)),
                  pl.BlockSpec((1, 1, d), lambda j, b, i: (b, 0, 0)),
                  pl.BlockSpec((d, tn), lambda j, b, i: (0, j))],
        out_specs=pl.BlockSpec((1, tm, tn), lambda j, b, i: (b, i, j)),
        out_shape=jax.ShapeDtypeStruct((b_, L, n), out_dtype),
        compiler_params=_cparams("arbitrary", "arbitrary", "arbitrary"),
        name="proj",
    )(x, scale, shift, w)


def _gla_kernel(*refs, reverse, finalize, nchunks):
    if finalize:
        (q_ref, k_ref, v_ref, ga_ref, wa_ref, ba_ref, s0_ref, tri_ref, op_ref, z_ref, nw_ref,
         o_ref, sfin_ref, s_scr) = refs
    else:
        (q_ref, k_ref, v_ref, ga_ref, wa_ref, ba_ref, s0_ref, tri_ref,
         o_ref, sfin_ref, s_scr) = refs
    C = GLA_CHUNK
    T = nchunks * C
    nt_dims = (((1,), (1,)), ((), ()))
    tn_dims = (((0,), (0,)), ((), ()))

    @pl.when(pl.program_id(1) == 0)
    def _():
        s_scr[...] = s0_ref[0]

    xg = _dot_hp(ga_ref[0], wa_ref[...]) + ba_ref[...]
    g = (jnp.minimum(xg, 0.0) - jnp.log(1.0 + jnp.exp(-jnp.abs(xg)))) * (1.0 / GLA_GATE_NORM)

    tri = tri_ref[...]
    g1 = g.astype(BF16)
    r1 = g - g1.astype(F32)
    g2 = r1.astype(BF16)
    g3 = (r1 - g2.astype(F32)).astype(BF16)
    b = _dot(tri, g1) + _dot(tri, g2) + _dot(tri, g3)

    def chunk_row(r):
        return jnp.concatenate([jnp.broadcast_to(b[c * C + r:c * C + r + 1], (C, GLA_KW))
                                for c in range(nchunks)], axis=0)

    tot_row = 0 if reverse else C - 1
    bm = chunk_row(C // 2)
    bt = chunk_row(tot_row)
    ri = lax.broadcasted_iota(jnp.int32, (T, T), 0)
    ci = lax.broadcasted_iota(jnp.int32, (T, T), 1)
    same_chunk = (ri // C) == (ci // C)
    mask = same_chunk & ((ci >= ri) if reverse else (ci <= ri))
    lane = lax.broadcasted_iota(jnp.int32, (1, GLA_KW), 1)
    hmask = [(lane >= h * GLA_DK) & (lane < (h + 1) * GLA_DK) for h in range(GLA_HEADS)]

    q = q_ref[0].astype(F32) * (GLA_DK ** -0.5)
    k = k_ref[0].astype(F32)
    v = v_ref[0].astype(BF16)
    qa = q * jnp.exp(b - bm)
    kb = (k * jnp.exp(bm - b)).astype(BF16)
    qe = q * jnp.exp(b)
    kd = (k * jnp.exp(bt - b)).astype(BF16)

    intra = []
    for h in range(GLA_HEADS):
        qa_h = jnp.where(hmask[h], qa, 0.0).astype(BF16)
        att = lax.dot_general(qa_h, kb, nt_dims, preferred_element_type=F32)
        att = jnp.where(mask, att, 0.0).astype(BF16)
        intra.append(_dot(att, v[:, h * GLA_DV:(h + 1) * GLA_DV]))

    inter = [None] * nchunks
    order = range(nchunks - 1, -1, -1) if reverse else range(nchunks)
    for c in order:
        rows = slice(c * C, (c + 1) * C)
        st = s_scr[...]
        st_b = st.astype(BF16)
        dec = jnp.exp(b[c * C + tot_row:c * C + tot_row + 1])
        upd = jnp.zeros_like(st)
        outs = []
        for h in range(GLA_HEADS):
            qe_h = jnp.where(hmask[h], qe[rows], 0.0).astype(BF16)
            outs.append(lax.dot_general(qe_h, st_b, nt_dims, preferred_element_type=F32))
            u_h = lax.dot_general(v[rows, h * GLA_DV:(h + 1) * GLA_DV], kd[rows], tn_dims,
                                  preferred_element_type=F32)
            upd = upd + jnp.where(hmask[h], u_h, 0.0)
        s_scr[...] = dec * st + upd
        inter[c] = jnp.concatenate(outs, axis=1)
    o = jnp.concatenate(intra, axis=1) + jnp.concatenate(inter, axis=0)

    if finalize:
        o = o + op_ref[0]
        parts = []
        for h in range(GLA_HEADS):
            oh = o[:, h * GLA_DV:(h + 1) * GLA_DV]
            ms = jnp.mean(oh * oh, axis=-1, keepdims=True)
            parts.append(oh * lax.rsqrt(ms + EPS) * nw_ref[...])
        y = jnp.concatenate(parts, axis=1) * _silu(z_ref[0].astype(F32))
        o_ref[0] = y.astype(o_ref.dtype)
    else:
        o_ref[0] = o

    @pl.when(pl.program_id(1) == pl.num_programs(1) - 1)
    def _():
        sfin_ref[0] = s_scr[...]


def _gla_call(P, GA, wa_pad, ba, s0, reverse, fin=None):
    b_, L, _ = P.shape
    T = min(GLA_BLOCK, L)
    nt = L // T
    if reverse:
        tmap = lambda t: nt - 1 - t
    else:
        tmap = lambda t: t
    ii = np.arange(T)
    same = (ii[:, None] // GLA_CHUNK) == (ii[None, :] // GLA_CHUNK)
    tri = same & ((ii[None, :] >= ii[:, None]) if reverse else (ii[None, :] <= ii[:, None]))
    tri = jnp.asarray(tri.astype(np.float32)).astype(BF16)

    def col(name, width):
        blk = COLS[name][0] // width
        return pl.BlockSpec((1, T, width), lambda b, t: (b, tmap(t), blk))

    in_specs = [col('g_q', GLA_KW), col('g_k', GLA_KW), col('g_v', GLA_VW),
                pl.BlockSpec((1, T, LANES), lambda b, t: (b, tmap(t), 0)),
                pl.BlockSpec((LANES, GLA_KW), lambda b, t: (0, 0)),
                pl.BlockSpec((1, GLA_KW), lambda b, t: (0, 0)),
                pl.BlockSpec((1, GLA_DV, GLA_KW), lambda b, t: (b, 0, 0)),
                pl.BlockSpec((T, T), lambda b, t: (0, 0))]
    args = [P, P, P, GA, wa_pad, ba, s0, tri]
    if fin is not None:
        o_prev, nw = fin
        in_specs += [pl.BlockSpec((1, T, GLA_VW), lambda b, t: (b, tmap(t), 0)),
                     col('g_z', GLA_VW),
                     pl.BlockSpec((1, GLA_DV), lambda b, t: (0, 0))]
        args += [o_prev, P, nw]
    out_dtype = BF16 if fin is not None else F32
    return pl.pallas_call(
        functools.partial(_gla_kernel, reverse=reverse, finalize=fin is not None, nchunks=T // GLA_CHUNK),
        grid=(b_, nt),
        in_specs=in_specs,
        out_specs=[pl.BlockSpec((1, T, GLA_VW), lambda b, t: (b, tmap(t), 0)),
                   pl.BlockSpec((1, GLA_DV, GLA_KW), lambda b, t: (b, 0, 0))],
        out_shape=[jax.ShapeDtypeStruct((b_, L, GLA_VW), out_dtype),
                   jax.ShapeDtypeStruct((b_, GLA_DV, GLA_KW), F32)],
        scratch_shapes=[pltpu.VMEM((GLA_DV, GLA_KW), F32)],
        compiler_params=_cparams("arbitrary", "arbitrary"),
        name="gla",
    )(*args)


def _norm_rope(a, gain, bd, cos, sin):
    sq = a * a
    sh, sl = _split(sq)
    ss = _dot(sh, bd) + _dot(sl, bd)
    an = a * lax.rsqrt(ss * (1.0 / ATTN_HD) + EPS) * gain
    w = a.shape[-1]
    lane = lax.broadcasted_iota(jnp.int32, an.shape, 1)
    partner = jnp.where(lane % 2 == 0, pltpu.roll(an, w - 1, 1), pltpu.roll(an, 1, 1))
    return an * cos + partner * sin


Q_SCALE = (ATTN_HD ** -0.5) * math.log2(math.e)
ATTN_TK_MAX = 640


def _qkv_prep_kernel(q_ref, k_ref, v_ref, cq_ref, sq_ref, ck_ref, sk_ref, gq_ref, gk_ref,
                     bdq_ref, bdk_ref, eq_ref, ev_ref, qo_ref, ko_ref, vo_ref):
    nt = (((1,), (1,)), ((), ()))
    q = _norm_rope(q_ref[0].astype(F32), gq_ref[...], bdq_ref[...], cq_ref[...], sq_ref[...])
    qb = (q * Q_SCALE).astype(BF16)
    qo_ref[0] = lax.dot_general(eq_ref[...], qb, nt, preferred_element_type=F32).astype(BF16)
    k = _norm_rope(k_ref[0].astype(F32), gk_ref[...], bdk_ref[...], ck_ref[...], sk_ref[...]).astype(BF16)
    for g in range(ATTN_KV_HEADS):
        ko_ref[0, g] = k[:, g * ATTN_HD:(g + 1) * ATTN_HD]
    vb = v_ref[0].astype(BF16)
    vo_ref[0] = lax.dot_general(ev_ref[...], vb, nt, preferred_element_type=F32).astype(BF16)


def _qkv_prep_call(P, cos_q, sin_q, cos_k, sin_k, gq, gk, bdq, bdk, eye_q, eye_v):
    b_, L, _ = P.shape
    T = min(512, L)

    def col(name, width):
        blk = COLS[name][0] // width
        return pl.BlockSpec((1, T, width), lambda b, t: (b, t, blk))

    tab = lambda w: pl.BlockSpec((T, w), lambda b, t: (t, 0))
    const = lambda r, w: pl.BlockSpec((r, w), lambda b, t: (0, 0))
    return pl.pallas_call(
        _qkv_prep_kernel,
        grid=(b_, L // T),
        in_specs=[col('a_q', ATTN_QW), col('a_k', ATTN_KVW), col('a_v', ATTN_KVW),
                  tab(ATTN_QW), tab(ATTN_QW), tab(ATTN_KVW), tab(ATTN_KVW),
                  const(1, ATTN_QW), const(1, ATTN_KVW),
                  const(ATTN_QW, ATTN_QW), const(ATTN_KVW, ATTN_KVW),
                  const(ATTN_QW, ATTN_QW), const(ATTN_KVW, ATTN_KVW)],
        out_specs=[pl.BlockSpec((1, ATTN_QW, T), lambda b, t: (b, 0, t)),
                   pl.BlockSpec((1, ATTN_KV_HEADS, T, ATTN_HD), lambda b, t: (b, 0, t, 0)),
                   pl.BlockSpec((1, ATTN_KVW, T), lambda b, t: (b, 0, t))],
        out_shape=[jax.ShapeDtypeStruct((b_, ATTN_QW, L), BF16),
                   jax.ShapeDtypeStruct((b_, ATTN_KV_HEADS, L, ATTN_HD), BF16),
                   jax.ShapeDtypeStruct((b_, ATTN_KVW, L), BF16)],
        compiler_params=_cparams("arbitrary", "arbitrary"),
        name="qkv_prep",
    )(P, P, P, cos_q, sin_q, cos_k, sin_k, gq, gk, bdq, bdk, eye_q, eye_v)


ATTN_SUM_ROWS = 16
ATTN_UNROLL = 2
ATTN_TQ = 512


def _attn_kernel(qt_ref, k_ref, vt_ref, z_ref, e_ref, o_ref, m_scr, a_scr, acc_scr, s_scr, p_scr,
                 *, tk, nk):
    m_scr[...] = jnp.full(m_scr.shape, -jnp.inf, F32)
    acc_scr[...] = jnp.zeros(acc_scr.shape, F32)
    ones = jnp.ones((ATTN_SUM_ROWS, tk), BF16)

    def scores(j, slot):
        start = j * tk if isinstance(j, int) else pl.multiple_of(j * tk, tk)
        kj = k_ref[0, 0, pl.ds(start, tk), :]
        for h in range(ATTN_GROUP):
            s_scr[slot, h] = _dot(kj, qt_ref[0, h * ATTN_HD:(h + 1) * ATTN_HD, :])

    scores(0, 0)

    def step(j, slot):
        scores(min(j + 1, nk - 1) if isinstance(j, int) else jnp.minimum(j + 1, nk - 1), 1 - slot)
        start = j * tk if isinstance(j, int) else pl.multiple_of(j * tk, tk)
        vte = jnp.concatenate([vt_ref[0, :, pl.ds(start, tk)], ones], axis=0)
        for h in range(ATTN_GROUP):
            s = s_scr[slot, h]
            m_old = m_scr[h]
            m_new = jnp.maximum(m_old, jnp.max(s, axis=0, keepdims=True))
            a_scr[h] = jnp.exp2(m_old - m_new)
            m_scr[h] = m_new
            p_scr[h] = jnp.exp2(s - m_new).astype(BF16)
        for h in range(ATTN_GROUP):
            acc_scr[h] = a_scr[h] * acc_scr[h] + _dot(vte, p_scr[h])

    def trip(i, carry):
        for u in range(ATTN_UNROLL):
            step(ATTN_UNROLL * i + u, u % 2)
        return carry

    ntrips = nk // ATTN_UNROLL
    lax.fori_loop(0, ntrips, trip, 0)
    for j in range(ntrips * ATTN_UNROLL, nk):
        step(j, j % 2)
    tn = (((0,), (0,)), ((), ()))
    out = None
    for h in range(ATTN_GROUP):
        acc = acc_scr[h]
        oh, ol = _split(acc[:ATTN_HD] / acc[ATTN_HD:ATTN_HD + 1])
        part = (lax.dot_general(oh, e_ref[h], tn, preferred_element_type=F32)
                + lax.dot_general(ol, e_ref[h], tn, preferred_element_type=F32))
        out = part if out is None else out + part
    o_ref[0] = (out * _silu(z_ref[0].astype(F32))).astype(o_ref.dtype)


def _attn_call(qt, k, vt, P, e_heads):
    b_, _, L = qt.shape
    Lk = k.shape[2]
    tq = min(ATTN_TQ, L)
    tk = max(t for t in range(LANES, ATTN_TK_MAX + 1, LANES) if Lk % t == 0)
    gw = ATTN_GROUP * ATTN_HD
    zblk = COLS['a_z'][0] // gw
    return pl.pallas_call(
        functools.partial(_attn_kernel, tk=tk, nk=Lk // tk),
        grid=(b_, ATTN_KV_HEADS, L // tq),
        in_specs=[pl.BlockSpec((1, gw, tq), lambda b, g, i: (b, g, i)),
                  pl.BlockSpec((1, 1, Lk, ATTN_HD), lambda b, g, i: (b, g, 0, 0)),
                  pl.BlockSpec((1, ATTN_HD, Lk), lambda b, g, i: (b, g, 0)),
                  pl.BlockSpec((1, tq, gw), lambda b, g, i: (b, i, zblk + g)),
                  pl.BlockSpec((ATTN_GROUP, ATTN_HD, gw), lambda b, g, i: (0, 0, 0))],
        out_specs=pl.BlockSpec((1, tq, gw), lambda b, g, i: (b, i, g)),
        out_shape=jax.ShapeDtypeStruct((b_, L, ATTN_QW), BF16),
        scratch_shapes=[pltpu.VMEM((ATTN_GROUP, 1, tq), F32), pltpu.VMEM((ATTN_GROUP, 1, tq), F32),
                        pltpu.VMEM((ATTN_GROUP, ATTN_HD + ATTN_SUM_ROWS, tq), F32),
                        pltpu.VMEM((2, ATTN_GROUP, tk, tq), F32), pltpu.VMEM((ATTN_GROUP, tk, tq), BF16)],
        compiler_params=_cparams("arbitrary", "arbitrary", "arbitrary"),
        name="attn",
    )(qt, k, vt, P, e_heads)


def _short_conv_kernel(u_ref, up_ref, un_ref, w_ref, b_ref, v_ref, x1_ref, x2_ref):
    t = pl.program_id(1)
    nt = pl.num_programs(1)
    u = u_ref[0].astype(F32)
    T = u.shape[0]
    hr = up_ref.shape[1]
    prev_row = jnp.where(t > 0, up_ref[0, hr - 1:hr, :].astype(F32), 0.0)
    next_row = jnp.where(t < nt - 1, un_ref[0, 0:1, :].astype(F32), 0.0)
    row = lax.broadcasted_iota(jnp.int32, u.shape, 0)
    u_m1 = jnp.where(row == 0, prev_row, pltpu.roll(u, 1, 0))
    u_p1 = jnp.where(row == T - 1, next_row, pltpu.roll(u, T - 1, 0))
    w = w_ref[...]
    out = b_ref[...] + u_m1 * w[0:1] + u * w[1:2] + u_p1 * w[2:3]
    v_ref[0] = out[:, 0:HY_W]
    x1_ref[0] = out[:, HY_W:2 * HY_W]
    x2_ref[0] = out[:, 2 * HY_W:3 * HY_W]


def _short_conv_call(P, w, bias):
    b_, L, _ = P.shape
    T = min(512, L)
    cw = (HY_ORDER + 1) * HY_W
    blk = COLS['y_u'][0] // cw
    hr = 16
    hb = T // hr
    nh = L // hr
    out_spec = pl.BlockSpec((1, T, HY_W), lambda b, t: (b, t, 0))
    shp = jax.ShapeDtypeStruct((b_, L, HY_W), F32)
    return pl.pallas_call(
        _short_conv_kernel,
        grid=(b_, L // T),
        in_specs=[pl.BlockSpec((1, T, cw), lambda b, t: (b, t, blk)),
                  pl.BlockSpec((1, hr, cw), lambda b, t: (b, jnp.maximum(t * hb - 1, 0), blk)),
                  pl.BlockSpec((1, hr, cw), lambda b, t: (b, jnp.minimum((t + 1) * hb, nh - 1), blk)),
                  pl.BlockSpec((HY_SHORT, cw), lambda b, t: (0, 0)),
                  pl.BlockSpec((1, cw), lambda b, t: (0, 0))],
        out_specs=[out_spec, out_spec, out_spec],
        out_shape=[shp, shp, shp],
        compiler_params=_cparams("arbitrary", "arbitrary"),
        name="short_conv",
    )(P, P, P, w, bias)


def _filter_kernel(emb_ref, w1_ref, b1_ref, f1_ref, w2_ref, b2_ref, f2_ref, w3_ref, dl_ref,
                   h_ref, sum_ref):
    t = pl.program_id(0)
    emb = emb_ref[...]
    h = jnp.sin(f1_ref[...] * (_dot_hp(emb, w1_ref[...]) + b1_ref[...]))
    h = jnp.sin(f2_ref[...] * (_dot_hp(h, w2_ref[...]) + b2_ref[...]))
    h = _dot_hp(h, w3_ref[...])
    tt = emb[:, 0:1]
    h = h * (jnp.exp(-tt * dl_ref[...]) + HY_MOD_SHIFT)

    @pl.when(t == 0)
    def _():
        sum_ref[...] = jnp.zeros_like(sum_ref)

    sum_ref[...] += jnp.broadcast_to(jnp.sum(jnp.abs(h), axis=0, keepdims=True), sum_ref.shape)
    row = lax.broadcasted_iota(jnp.int32, h.shape, 0) + t * h.shape[0]
    colm = lax.broadcasted_iota(jnp.int32, h.shape, 1)
    bwd = (colm // HY_W) % 2 == 1
    h_ref[...] = jnp.where(bwd & (row == 0), 0.0, h)


def _filter_call(emb, w1, b1, f1, w2, b2, f2, w3, deltas):
    L = emb.shape[0]
    TL = min(256, L)
    const = lambda r, w: pl.BlockSpec((r, w), lambda t: (0, 0))
    return pl.pallas_call(
        _filter_kernel,
        grid=(L // TL,),
        in_specs=[pl.BlockSpec((TL, LANES), lambda t: (t, 0)),
                  const(LANES, HY_FFN), const(1, HY_FFN), const(1, HY_FFN),
                  const(HY_FFN, HY_FFN), const(1, HY_FFN), const(1, HY_FFN),
                  const(HY_FFN, N_FILT), const(1, N_FILT)],
        out_specs=[pl.BlockSpec((TL, N_FILT), lambda t: (t, 0)),
                   pl.BlockSpec((8, N_FILT), lambda t: (0, 0))],
        out_shape=[jax.ShapeDtypeStruct((L, N_FILT), F32),
                   jax.ShapeDtypeStruct((8, N_FILT), F32)],
        compiler_params=_cparams("arbitrary"),
        name="hy_filter",
    )(emb, w1, b1, f1, w2, b2, f2, w3, deltas)


def _fft_sizes(n):
    lg = int(round(math.log2(n)))
    assert 1 << lg == n
    n1 = 1 << ((lg + 1) // 2)
    return n1, n // n1


def _bf16_pair(a):
    a = np.asarray(a, np.float32)
    hi = jnp.asarray(a, F32).astype(BF16)
    lo = (jnp.asarray(a, F32) - hi.astype(F32)).astype(BF16)
    return hi, lo


def _fft_tables(n1, n2):
    n = n1 * n2
    h1 = n1 // 2
    k1 = np.arange(n1)[:, None]
    a = 2.0 * np.pi * ((k1 * np.arange(h1)[None, :]) % n1) / n1
    c, s = np.cos(a), np.sin(a)
    fa = np.block([[c, s], [-s, c]])
    fa_real = np.concatenate([c, -s], axis=0)
    fd = np.block([[c.T, -s.T], [s.T, c.T]])
    k2 = np.arange(n2)[:, None]
    b = 2.0 * np.pi * ((k2 * np.arange(n2)[None, :]) % n2) / n2
    cb, sb = np.cos(b), np.sin(b)
    fb = np.block([[cb, sb], [-sb, cb]])
    fc = np.block([[cb, -sb], [sb, cb]])
    kn = (jnp.arange(n1, dtype=jnp.int32)[:, None] * jnp.arange(n2, dtype=jnp.int32)[None, :]) % n
    tw = kn.astype(F32) * (2.0 * math.pi / n)
    twr = jnp.broadcast_to(jnp.cos(tw)[:, :, None], (n1, n2, LANES))
    twi = jnp.broadcast_to(-jnp.sin(tw)[:, :, None], (n1, n2, LANES))
    return dict(fa=_bf16_pair(fa), fa_real=_bf16_pair(fa_real), fd=_bf16_pair(fd),
                fb=_bf16_pair(fb), fc=_bf16_pair(fc), twr=twr, twi=twi)


def _fa_kernel(x_ref, mh_ref, ml_ref, o_ref, *, packed):
    if packed:
        x = x_ref[...]
        x = x.reshape(x.shape[0] * x.shape[1], x.shape[2])
    else:
        x = x_ref[...]
    o_ref[...] = _dot_mat(mh_ref[...], ml_ref[...], x)


def _fa_call(x, mats, n1, packed):
    mh, ml = mats
    wtot = x.shape[-1]
    W = min(2048, wtot)
    h1 = n1 // 2
    if packed:
        x_spec = pl.BlockSpec((2, h1, W), lambda j: (0, 0, j))
    else:
        x_spec = pl.BlockSpec((h1, W), lambda j: (0, j))
    return pl.pallas_call(
        functools.partial(_fa_kernel, packed=packed),
        grid=(wtot // W,),
        in_specs=[x_spec, pl.BlockSpec(mh.shape, lambda j: (0, 0)), pl.BlockSpec(ml.shape, lambda j: (0, 0))],
        out_specs=pl.BlockSpec((2 * n1, W), lambda j: (0, j)),
        out_shape=jax.ShapeDtypeStruct((2 * n1, wtot), F32),
        compiler_params=_cparams("arbitrary"),
        name="fft_a",
    )(x, mh, ml)


def _lane_tile(a, width):
    return jnp.concatenate([a] * (width // a.shape[-1]), axis=-1)


def _fb_filter_kernel(t_ref, twr_ref, twi_ref, fh_ref, fl_ref, inv_ref, g_ref, *, n2, scale):
    w = t_ref.shape[-1]
    tr = t_ref[0, 0]
    ti = t_ref[1, 0]
    twr = _lane_tile(twr_ref[0], w)
    twi = _lane_tile(twi_ref[0], w)
    p = jnp.concatenate([tr * twr - ti * twi, tr * twi + ti * twr], axis=0)
    z = _dot_mat(fh_ref[...], fl_ref[...], p) * (inv_ref[0:1, :] * scale)
    zr, zi = z[:n2], z[n2:]
    for o in range(HY_ORDER):
        f0 = (2 * o) * HY_W
        b0 = (2 * o + 1) * HY_W
        g_ref[o, 0, 0] = zr[:, f0:f0 + HY_W] + zr[:, b0:b0 + HY_W]
        g_ref[o, 1, 0] = zi[:, f0:f0 + HY_W] - zi[:, b0:b0 + HY_W]


def _fb_filter_call(t, tabs, inv_sums, n1, n2):
    fh, fl = tabs['fb']
    t4 = t.reshape(2, n1, n2, N_FILT)
    return pl.pallas_call(
        functools.partial(_fb_filter_kernel, n2=n2, scale=1.0 / (n1 * n2)),
        grid=(n1,),
        in_specs=[pl.BlockSpec((2, 1, n2, N_FILT), lambda k: (0, k, 0, 0)),
                  pl.BlockSpec((1, n2, LANES), lambda k: (k, 0, 0)),
                  pl.BlockSpec((1, n2, LANES), lambda k: (k, 0, 0)),
                  pl.BlockSpec(fh.shape, lambda k: (0, 0)),
                  pl.BlockSpec(fl.shape, lambda k: (0, 0)),
                  pl.BlockSpec((8, N_FILT), lambda k: (0, 0))],
        out_specs=pl.BlockSpec((HY_ORDER, 2, 1, n2, HY_W), lambda k: (0, 0, k, 0, 0)),
        out_shape=jax.ShapeDtypeStruct((HY_ORDER, 2, n1, n2, HY_W), F32),
        compiler_params=_cparams("arbitrary"),
        name="fft_b_filter",
    )(t4, tabs['twr'], tabs['twi'], fh, fl, inv_sums)


def _fb_kernel(t_ref, twr_ref, twi_ref, fh_ref, fl_ref, ch_ref, cl_ref, g_ref, o_ref, *, n2):
    w = t_ref.shape[-1]
    tr = t_ref[0, 0]
    ti = t_ref[1, 0]
    twr = _lane_tile(twr_ref[0], w)
    twi = _lane_tile(twi_ref[0], w)
    p = jnp.concatenate([tr * twr - ti * twi, tr * twi + ti * twr], axis=0)
    z = _dot_mat(fh_ref[...], fl_ref[...], p)
    zr, zi = z[:n2], z[n2:]
    gr = g_ref[0, 0, 0]
    gi = g_ref[0, 1, 0]
    y = jnp.concatenate([zr * gr - zi * gi, zr * gi + zi * gr], axis=0)
    v = _dot_mat(ch_ref[...], cl_ref[...], y)
    vr, vi = v[:n2], v[n2:]
    o_ref[0, 0] = vr * twr + vi * twi
    o_ref[1, 0] = vi * twr - vr * twi


def _fb_call(t, g, order, tabs, n1, n2):
    fh, fl = tabs['fb']
    ch, cl = tabs['fc']
    t4 = t.reshape(2, n1, n2, HY_W)
    mat = lambda m: pl.BlockSpec(m.shape, lambda k: (0, 0))
    out = pl.pallas_call(
        functools.partial(_fb_kernel, n2=n2),
        grid=(n1,),
        in_specs=[pl.BlockSpec((2, 1, n2, HY_W), lambda k: (0, k, 0, 0)),
                  pl.BlockSpec((1, n2, LANES), lambda k: (k, 0, 0)),
                  pl.BlockSpec((1, n2, LANES), lambda k: (k, 0, 0)),
                  mat(fh), mat(fl), mat(ch), mat(cl),
                  pl.BlockSpec((1, 2, 1, n2, HY_W), lambda k: (order, 0, k, 0, 0))],
        out_specs=pl.BlockSpec((2, 1, n2, HY_W), lambda k: (0, k, 0, 0)),
        out_shape=jax.ShapeDtypeStruct((2, n1, n2, HY_W), F32),
        compiler_params=_cparams("arbitrary"),
        name="fft_b",
    )(t4, tabs['twr'], tabs['twi'], fh, fl, ch, cl, g)
    return out.reshape(2 * n1, n2 * HY_W)


def _fd_kernel(u_ref, mh_ref, ml_ref, z_ref, x_ref, sk_ref, o_ref):
    y = _dot_mat(mh_ref[...], ml_ref[...], u_ref[...])
    y = y.reshape(o_ref.shape)
    o_ref[...] = x_ref[...] * (y + sk_ref[...] * z_ref[...])


def _fd_call(u, mats, z, xg, skip_row, n1):
    mh, ml = mats
    b_, h1, wtot = z.shape
    W = min(2048, wtot)
    dat = pl.BlockSpec((b_, h1, W), lambda j: (0, 0, j))
    return pl.pallas_call(
        _fd_kernel,
        grid=(wtot // W,),
        in_specs=[pl.BlockSpec((2 * n1, W), lambda j: (0, j)),
                  pl.BlockSpec(mh.shape, lambda j: (0, 0)), pl.BlockSpec(ml.shape, lambda j: (0, 0)),
                  dat, dat, pl.BlockSpec((1, W), lambda j: (0, 0))],
        out_specs=dat,
        out_shape=jax.ShapeDtypeStruct((b_, h1, wtot), F32),
        compiler_params=_cparams("arbitrary"),
        name="fft_d",
    )(u, mh, ml, z, xg, skip_row)


def _hyena_filter_spectrum(L, tabs, n1, n2, fp):
    f32 = F32
    t = jnp.linspace(0.0, 1.0, L, dtype=f32)[:, None]
    w = 2.0 * math.pi * jnp.arange(L, dtype=f32)[:, None] / L
    fr = jnp.linspace(1e-4, HY_BANDS - 1, HY_BANDS, dtype=f32)[None]
    emb = jnp.concatenate([t, jnp.cos(fr * w), -jnp.sin(fr * w)], axis=-1)
    emb = jnp.pad(emb, ((0, 0), (0, LANES - HY_EMB)))
    deltas = jnp.abs(jnp.linspace(math.log(HY_DECAY_TARGET) / HY_DECAY_SHORT_PCT,
                                  math.log(HY_DECAY_TARGET) / HY_DECAY_LONG_PCT, N_FILT, dtype=f32))[None]
    f1_w, f1_b, f1_freq, f2_w, f2_b, f2_freq, f3_w = fp
    w1 = jnp.pad(f1_w, ((0, LANES - HY_EMB), (0, 0)))
    h, sums = _filter_call(emb, w1, f1_b[None], f1_freq[None], f2_w, f2_b[None], f2_freq[None], f3_w, deltas)
    inv_sums = 1.0 / sums
    hv = h.reshape(n1 // 2, n2 * N_FILT)
    th = _fa_call(hv, tabs['fa_real'], n1, packed=False)
    return _fb_filter_call(th, tabs, inv_sums, n1, n2)


def _hyena_conv(z, xg, g, order, skip, tabs, n1, n2):
    b_, L, wd = z.shape
    assert b_ == 2, "the two batch rows ride as the real and imaginary parts of one transform"
    zv = z.reshape(b_, n1 // 2, n2 * wd)
    xv = xg.reshape(b_, n1 // 2, n2 * wd)
    t = _fa_call(zv, tabs['fa'], n1, packed=True)
    u = _fb_call(t, g, order, tabs, n1, n2)
    wtile = min(2048, n2 * wd)
    skip_row = jnp.tile(skip[None, :], (1, wtile // wd))
    out = _fd_call(u, tabs['fd'], zv, xv, skip_row, n1)
    return out.reshape(b_, L, wd)


def _merge_kernel(yg_ref, ya_ref, yh_ref, yz_ref, mg_ref, ma_ref, mh_ref, x_ref, gate_ref,
                  wg_ref, wa_ref, wh_ref, wo_ref, fn_ref, o_ref, *, final):
    yh = (yh_ref[0] * _silu(yz_ref[0].astype(F32))).astype(BF16)
    y = (_sigmoid(mg_ref[0].astype(F32)) * _dot(yg_ref[0], wg_ref[...])
         + _sigmoid(ma_ref[0].astype(F32)) * _dot(ya_ref[0], wa_ref[...])
         + _sigmoid(mh_ref[0].astype(F32)) * _dot(yh, wh_ref[...]))
    out = _dot(y.astype(BF16), wo_ref[...])
    xn = x_ref[0] + gate_ref[0] * out
    if final:
        ms = jnp.mean(xn * xn, axis=-1, keepdims=True)
        xn = xn * lax.rsqrt(ms + EPS) * fn_ref[...]
    o_ref[0] = xn


def _merge_call(y_gla, y_attn, y_hy, P, x, gate, wg, wa, wh, wo, fnorm, final):
    b_, L, d = x.shape
    T = min(512, L)
    row = lambda w, blk=0: pl.BlockSpec((1, T, w), lambda b, t: (b, t, blk))
    const = lambda r, w: pl.BlockSpec((r, w), lambda b, t: (0, 0))
    mblk = COLS['m'][0] // d
    return pl.pallas_call(
        functools.partial(_merge_kernel, final=final),
        grid=(b_, L // T),
        in_specs=[row(BRANCH_W), row(BRANCH_W), row(BRANCH_W), row(HY_W, COLS['y_z'][0] // HY_W),
                  row(d, mblk), row(d, mblk + 1), row(d, mblk + 2), row(d),
                  pl.BlockSpec((1, 1, d), lambda b, t: (b, 0, 0)),
                  const(BRANCH_W, d), const(BRANCH_W, d), const(BRANCH_W, d), const(d, d), const(1, d)],
        out_specs=row(d),
        out_shape=jax.ShapeDtypeStruct((b_, L, d), F32),
        compiler_params=_cparams("arbitrary", "arbitrary"),
        name="merge",
    )(y_gla, y_attn, y_hy, P, P, P, P, x, gate, wg, wa, wh, wo, fnorm)


def _pack_w_in(w):
    parts = dict(zip(('g_q', 'g_k', 'g_v', 'g_z', 'g_af', 'g_ab', 'a_q', 'a_k', 'a_v', 'a_z', 'y_u', 'y_z', 'm'),
                     jnp.split(w, [int(i) for i in np.cumsum(SPLITS)[:-1]], axis=-1)))
    parts['g_a'] = jnp.pad(jnp.concatenate([parts['g_af'], parts['g_ab']], axis=-1),
                           ((0, 0), (0, LANES - 2 * GLA_RANK)))
    order = sorted(COLS, key=lambda n: COLS[n][0])
    return jnp.concatenate([parts[n] for n in order], axis=-1).astype(BF16)


def _rope_tables(L, heads):
    t = jnp.arange(L)
    row = (t // GRID_W).astype(F32)
    colp = (t % GRID_W).astype(F32)
    half = ATTN_HD // 2
    inv = ROPE_THETA ** (-jnp.arange(0, half, 2, dtype=F32) / half)
    ang = jnp.concatenate([row[:, None] * inv, colp[:, None] * inv], axis=-1)
    cos = jnp.repeat(jnp.cos(ang), 2, axis=-1)
    sin = jnp.stack([-jnp.sin(ang), jnp.sin(ang)], axis=-1).reshape(L, ATTN_HD)
    return jnp.tile(cos, (1, heads)), jnp.tile(sin, (1, heads))


def _identity_rope(L, heads):
    return jnp.ones((L, heads * ATTN_HD), F32), jnp.zeros((L, heads * ATTN_HD), F32)


def _block_diag_ones(width):
    i = np.arange(width) // ATTN_HD
    return jnp.asarray((i[:, None] == i[None, :]).astype(np.float32)).astype(BF16)


def kernel(x, c, ctx, c_ctx, w_ada, b_ada, w_in, gla_wa_f, gla_ba_f, gla_wa_b, gla_ba_b, gla_norm,
           attn_qnorm, attn_knorm, hy_conv_w, hy_conv_b, hy_f1_w, hy_f1_b, hy_f1_freq, hy_f2_w,
           hy_f2_b, hy_f2_freq, hy_f3_w, hy_skip, w_br_gla, w_br_attn, w_br_hy, w_out, final_norm):
    b_, L, d = x.shape
    Lc = ctx.shape[1]
    depth = w_ada.shape[0]

    cos_q, sin_q = _rope_tables(L, ATTN_HEADS)
    cos_k, sin_k = cos_q[:, :ATTN_KVW], sin_q[:, :ATTN_KVW]
    cos_qc, sin_qc = _identity_rope(Lc, ATTN_HEADS)
    cos_kc, sin_kc = cos_qc[:, :ATTN_KVW], sin_qc[:, :ATTN_KVW]
    bdq, bdk = _block_diag_ones(ATTN_QW), _block_diag_ones(ATTN_KVW)
    eye_q, eye_v = jnp.eye(ATTN_QW, dtype=BF16), jnp.eye(ATTN_KVW, dtype=BF16)
    gw = ATTN_GROUP * ATTN_HD
    e_heads = jnp.stack([jnp.eye(ATTN_HD, gw, k=h * ATTN_HD, dtype=BF16) for h in range(ATTN_GROUP)])
    n1, n2 = _fft_sizes(2 * L)
    n1c, n2c = _fft_sizes(2 * Lc)
    tabs = _fft_tables(n1, n2)
    tabs_c = _fft_tables(n1c, n2c)
    zero_state = jnp.zeros((b_, GLA_DV, GLA_KW), F32)

    cmat = jnp.concatenate([c, c_ctx[None], jnp.zeros((8 - b_ - 1, d), F32)], axis=0)
    mods = _ada_call(cmat, w_ada, b_ada)

    for l in range(depth):
        need_ctx = l < depth - 1
        shift, scale, gate = [m[:b_, None, :] for m in jnp.split(mods[l], 3, axis=-1)]
        shift_c, scale_c, gate_c = [jnp.broadcast_to(m[b_:b_ + 1, None, :], (b_, 1, d))
                                    for m in jnp.split(mods[l], 3, axis=-1)]
        w_packed = _pack_w_in(w_in[l])
        P = _proj_call(x, scale, shift, w_packed, BF16, PROJ_TN)
        Pc = _proj_call(ctx, scale_c, shift_c, w_packed, BF16, PROJ_TN)
        ga0 = COLS['g_a'][0]
        w_ga = w_packed[:, ga0:ga0 + LANES]
        GA = _proj_call(x, scale, shift, w_ga, F32, LANES)
        GAc = _proj_call(ctx, scale_c, shift_c, w_ga, F32, LANES)

        rk = GLA_RANK
        wa_f = jnp.zeros((LANES, GLA_KW), F32).at[0:rk].set(gla_wa_f[l])
        wa_b = jnp.zeros((LANES, GLA_KW), F32).at[rk:2 * rk].set(gla_wa_b[l])
        ba_f, ba_b = gla_ba_f[l][None], gla_ba_b[l][None]
        nw = gla_norm[l][None]
        oc_f, sc_f = _gla_call(Pc, GAc, wa_f, ba_f, zero_state, reverse=False)
        yc_gla, sc_b = _gla_call(Pc, GAc, wa_b, ba_b, zero_state, reverse=True, fin=(oc_f, nw))
        o_f, _ = _gla_call(P, GA, wa_f, ba_f, sc_f, reverse=False)
        y_gla, _ = _gla_call(P, GA, wa_b, ba_b, sc_b, reverse=True, fin=(o_f, nw))

        gq = jnp.tile(attn_qnorm[l], ATTN_HEADS)[None]
        gk = jnp.tile(attn_knorm[l], ATTN_KV_HEADS)[None]
        q_a, k_a, v_a = _qkv_prep_call(P, cos_q, sin_q, cos_k, sin_k, gq, gk, bdq, bdk, eye_q, eye_v)
        qc_a, kc_a, vc_a = _qkv_prep_call(Pc, cos_qc, sin_qc, cos_kc, sin_kc, gq, gk, bdq, bdk,
                                          eye_q, eye_v)
        y_attn = _attn_call(q_a, jnp.concatenate([k_a, kc_a], axis=2),
                            jnp.concatenate([v_a, vc_a], axis=2), P, e_heads)

        fp = (hy_f1_w[l], hy_f1_b[l], hy_f1_freq[l], hy_f2_w[l], hy_f2_b[l], hy_f2_freq[l], hy_f3_w[l])
        g_spec = _hyena_filter_spectrum(L, tabs, n1, n2, fp)
        v0, x1, x2 = _short_conv_call(P, hy_conv_w[l], hy_conv_b[l][None])
        z1 = _hyena_conv(v0, x1, g_spec, 0, hy_skip[l, 0], tabs, n1, n2)
        y_hy = _hyena_conv(z1, x2, g_spec, 1, hy_skip[l, 1], tabs, n1, n2)

        wg, wa, wh, wo = (w_br_gla[l].astype(BF16), w_br_attn[l].astype(BF16),
                          w_br_hy[l].astype(BF16), w_out[l].astype(BF16))
        fn = final_norm[None]
        x_new = _merge_call(y_gla, y_attn, y_hy, P, x, gate, wg, wa, wh, wo, fn, final=not need_ctx)

        if need_ctx:
            yc_attn = _attn_call(qc_a, kc_a, vc_a, Pc, e_heads)
            gc_spec = _hyena_filter_spectrum(Lc, tabs_c, n1c, n2c, fp)
            vc0, xc1, xc2 = _short_conv_call(Pc, hy_conv_w[l], hy_conv_b[l][None])
            zc1 = _hyena_conv(vc0, xc1, gc_spec, 0, hy_skip[l, 0], tabs_c, n1c, n2c)
            yc_hy = _hyena_conv(zc1, xc2, gc_spec, 1, hy_skip[l, 1], tabs_c, n1c, n2c)
            ctx = _merge_call(yc_gla, yc_attn, yc_hy, Pc, ctx, gate_c, wg, wa, wh, wo, fn, final=False)
        x = x_new

    return x
```

```python
import functools
import math

import numpy as np
import jax
import jax.numpy as jnp
from jax import lax
from jax.experimental import pallas as pl
from jax.experimental.pallas import tpu as pltpu

F32 = jnp.float32
BF16 = jnp.bfloat16

D_MODEL = 1024
GRID_W = 64
BRANCH_W = D_MODEL // 2
N_BRANCH = 3
EPS = 1e-6
GLA_HEADS = 4
GLA_DV = BRANCH_W // GLA_HEADS
GLA_DK = GLA_DV // 2
GLA_KW = GLA_HEADS * GLA_DK
GLA_VW = GLA_HEADS * GLA_DV
GLA_RANK = 16
GLA_GATE_NORM = 16.0
GLA_CHUNK = 64
GLA_BLOCK = 256
ATTN_HD = 64
ATTN_HEADS = BRANCH_W // ATTN_HD
ATTN_KV_HEADS = ATTN_HEADS // 4
ATTN_GROUP = ATTN_HEADS // ATTN_KV_HEADS
ATTN_QW = ATTN_HEADS * ATTN_HD
ATTN_KVW = ATTN_KV_HEADS * ATTN_HD
ROPE_THETA = 10000.0
HY_W = BRANCH_W
HY_ORDER = 2
HY_EMB = 33
HY_BANDS = (HY_EMB - 1) // 2
HY_FFN = 64
HY_SHORT = 3
HY_MOD_SHIFT = 0.05
HY_DECAY_SHORT_PCT = 0.3
HY_DECAY_LONG_PCT = 1.5
HY_DECAY_TARGET = 1e-2
N_FILT = 2 * HY_ORDER * HY_W
SPLITS = (GLA_KW, GLA_KW, GLA_VW, GLA_VW, GLA_RANK, GLA_RANK,
          ATTN_QW, ATTN_KVW, ATTN_KVW, ATTN_QW,
          (HY_ORDER + 1) * HY_W, HY_W,
          N_BRANCH * D_MODEL)

LANES = 128
VMEM_LIMIT = 48 * 1024 * 1024
FFT_D_VMEM_LIMIT = 56 * 1024 * 1024

COLS = {
    'm': (0, 3072), 'y_u': (3072, 1536), 'g_v': (4608, 512), 'g_z': (5120, 512),
    'a_q': (5632, 512), 'a_z': (6144, 512), 'y_z': (6656, 512), 'g_q': (7168, 256),
    'g_k': (7424, 256), 'a_k': (7680, 128), 'a_v': (7808, 128), 'g_a': (7936, 128),
}
N_PROJ = 8064
PROJ_TN = 2688


def _cparams(*sem):
    return pltpu.CompilerParams(dimension_semantics=sem, vmem_limit_bytes=VMEM_LIMIT)


def _split(a):
    hi = a.astype(BF16)
    lo = (a - hi.astype(F32)).astype(BF16)
    return hi, lo


def _dot(a, b):
    return jnp.dot(a, b, preferred_element_type=F32)


def _dot_hp(a, b):
    ah, al = _split(a)
    bh, bl = _split(b)
    return _dot(ah, bh) + _dot(al, bh) + _dot(ah, bl)


def _dot_mat(mh, ml, x):
    xh, xl = _split(x)
    return _dot(mh, xh) + _dot(mh, xl) + _dot(ml, xh)


def _sigmoid(x):
    return 1.0 / (1.0 + jnp.exp(-x))


def _silu(x):
    return x * _sigmoid(x)


def _ada_kernel(c_ref, w_ref, b_ref, o_ref):
    o_ref[0] = _dot_hp(_silu(c_ref[...]), w_ref[0]) + b_ref[0]


def _ada_call(cmat, w_ada, b_ada):
    depth, d, n3 = w_ada.shape
    tn = 1024
    return pl.pallas_call(
        _ada_kernel,
        grid=(depth, n3 // tn),
        in_specs=[pl.BlockSpec((8, d), lambda l, j: (0, 0)),
                  pl.BlockSpec((1, d, tn), lambda l, j: (l, 0, j)),
                  pl.BlockSpec((1, 1, tn), lambda l, j: (l, 0, j))],
        out_specs=pl.BlockSpec((1, 8, tn), lambda l, j: (l, 0, j)),
        out_shape=jax.ShapeDtypeStruct((depth, 8, n3), F32),
        compiler_params=_cparams("arbitrary", "arbitrary"),
        name="ada",
    )(cmat, w_ada, b_ada.reshape(depth, 1, n3))


def _proj_kernel(x_ref, sc_ref, sh_ref, w_ref, o_ref):
    x = x_ref[0]
    ms = jnp.mean(x * x, axis=-1, keepdims=True)
    h = x * lax.rsqrt(ms + EPS) * (1.0 + sc_ref[0]) + sh_ref[0]
    o_ref[0] = _dot(h.astype(BF16), w_ref[...]).astype(o_ref.dtype)


def _proj_call(x, scale, shift, w, out_dtype, tn):
    b_, L, d = x.shape
    n = w.shape[1]
    tm = min(512, L)
    return pl.pallas_call(
        _proj_kernel,
        grid=(n // tn, b_, L // tm),
        in_specs=[pl.BlockSpec((1, tm, d), lambda j, b, i: (b, i, 0)),
                  pl.BlockSpec((1, 1, d), lambda j, b, i: (b, 0, 0)),
                  pl.BlockSpec((1, 1, d), lambda j, b, i: (b, 0, 0)),
                  pl.BlockSpec((d, tn), lambda j, b, i: (0, j))],
        out_specs=pl.BlockSpec((1, tm, tn), lambda j, b, i: (b, i, j)),
        out_shape=jax.ShapeDtypeStruct((b_, L, n), out_dtype),
        compiler_params=_cparams("arbitrary", "arbitrary", "arbitrary"),
        name="proj",
    )(x, scale, shift, w)


def _gla_kernel(*refs, reverse, finalize, nchunks):
    if finalize:
        (q_ref, k_ref, v_ref, ga_ref, wa_ref, ba_ref, s0_ref, tri_ref, op_ref, z_ref, nw_ref,
         o_ref, sfin_ref, s_scr) = refs
    else:
        (q_ref, k_ref, v_ref, ga_ref, wa_ref, ba_ref, s0_ref, tri_ref,
         o_ref, sfin_ref, s_scr) = refs
    C = GLA_CHUNK
    T = nchunks * C
    nt_dims = (((1,), (1,)), ((), ()))
    tn_dims = (((0,), (0,)), ((), ()))

    @pl.when(pl.program_id(1) == 0)
    def _():
        s_scr[...] = s0_ref[0]

    xg = _dot_hp(ga_ref[0], wa_ref[...]) + ba_ref[...]
    g = (jnp.minimum(xg, 0.0) - jnp.log(1.0 + jnp.exp(-jnp.abs(xg)))) * (1.0 / GLA_GATE_NORM)

    tri = tri_ref[...]
    g1 = g.astype(BF16)
    r1 = g - g1.astype(F32)
    g2 = r1.astype(BF16)
    g3 = (r1 - g2.astype(F32)).astype(BF16)
    b = _dot(tri, g1) + _dot(tri, g2) + _dot(tri, g3)

    def chunk_row(r):
        return jnp.concatenate([jnp.broadcast_to(b[c * C + r:c * C + r + 1], (C, GLA_KW))
                                for c in range(nchunks)], axis=0)

    tot_row = 0 if reverse else C - 1
    bm = chunk_row(C // 2)
    bt = chunk_row(tot_row)
    ri = lax.broadcasted_iota(jnp.int32, (T, T), 0)
    ci = lax.broadcasted_iota(jnp.int32, (T, T), 1)
    same_chunk = (ri // C) == (ci // C)
    mask = same_chunk & ((ci >= ri) if reverse else (ci <= ri))
    lane = lax.broadcasted_iota(jnp.int32, (1, GLA_KW), 1)
    hmask = [(lane >= h * GLA_DK) & (lane < (h + 1) * GLA_DK) for h in range(GLA_HEADS)]

    q = q_ref[0].astype(F32) * (GLA_DK ** -0.5)
    k = k_ref[0].astype(F32)
    v = v_ref[0].astype(BF16)
    qa = q * jnp.exp(b - bm)
    kb = (k * jnp.exp(bm - b)).astype(BF16)
    qe = q * jnp.exp(b)
    kd = (k * jnp.exp(bt - b)).astype(BF16)

    intra = []
    for h in range(GLA_HEADS):
        qa_h = jnp.where(hmask[h], qa, 0.0).astype(BF16)
        att = lax.dot_general(qa_h, kb, nt_dims, preferred_element_type=F32)
        att = jnp.where(mask, att, 0.0).astype(BF16)
        intra.append(_dot(att, v[:, h * GLA_DV:(h + 1) * GLA_DV]))

    inter = [None] * nchunks
    order = range(nchunks - 1, -1, -1) if reverse else range(nchunks)
    for c in order:
        rows = slice(c * C, (c + 1) * C)
        st = s_scr[...]
        st_b = st.astype(BF16)
        dec = jnp.exp(b[c * C + tot_row:c * C + tot_row + 1])
        upd = jnp.zeros_like(st)
        outs = []
        for h in range(GLA_HEADS):
            qe_h = jnp.where(hmask[h], qe[rows], 0.0).astype(BF16)
            outs.append(lax.dot_general(qe_h, st_b, nt_dims, preferred_element_type=F32))
            u_h = lax.dot_general(v[rows, h * GLA_DV:(h + 1) * GLA_DV], kd[rows], tn_dims,
                                  preferred_element_type=F32)
            upd = upd + jnp.where(hmask[h], u_h, 0.0)
        s_scr[...] = dec * st + upd
        inter[c] = jnp.concatenate(outs, axis=1)
    o = jnp.concatenate(intra, axis=1) + jnp.concatenate(inter, axis=0)

    if finalize:
        o = o + op_ref[0]
        parts = []
        for h in range(GLA_HEADS):
            oh = o[:, h * GLA_DV:(h + 1) * GLA_DV]
            ms = jnp.mean(oh * oh, axis=-1, keepdims=True)
            parts.append(oh * lax.rsqrt(ms + EPS) * nw_ref[...])
        y = jnp.concatenate(parts, axis=1) * _silu(z_ref[0].astype(F32))
        o_ref[0] = y.astype(o_ref.dtype)
    else:
        o_ref[0] = o

    @pl.when(pl.program_id(1) == pl.num_programs(1) - 1)
    def _():
        sfin_ref[0] = s_scr[...]


def _gla_call(P, GA, wa_pad, ba, s0, reverse, fin=None):
    b_, L, _ = P.shape
    T = min(GLA_BLOCK, L)
    nt = L // T
    if reverse:
        tmap = lambda t: nt - 1 - t
    else:
        tmap = lambda t: t
    ii = np.arange(T)
    same = (ii[:, None] // GLA_CHUNK) == (ii[None, :] // GLA_CHUNK)
    tri = same & ((ii[None, :] >= ii[:, None]) if reverse else (ii[None, :] <= ii[:, None]))
    tri = jnp.asarray(tri.astype(np.float32)).astype(BF16)

    def col(name, width):
        blk = COLS[name][0] // width
        return pl.BlockSpec((1, T, width), lambda b, t: (b, tmap(t), blk))

    in_specs = [col('g_q', GLA_KW), col('g_k', GLA_KW), col('g_v', GLA_VW),
                pl.BlockSpec((1, T, LANES), lambda b, t: (b, tmap(t), 0)),
                pl.BlockSpec((LANES, GLA_KW), lambda b, t: (0, 0)),
                pl.BlockSpec((1, GLA_KW), lambda b, t: (0, 0)),
                pl.BlockSpec((1, GLA_DV, GLA_KW), lambda b, t: (b, 0, 0)),
                pl.BlockSpec((T, T), lambda b, t: (0, 0))]
    args = [P, P, P, GA, wa_pad, ba, s0, tri]
    if fin is not None:
        o_prev, nw = fin
        in_specs += [pl.BlockSpec((1, T, GLA_VW), lambda b, t: (b, tmap(t), 0)),
                     col('g_z', GLA_VW),
                     pl.BlockSpec((1, GLA_DV), lambda b, t: (0, 0))]
        args += [o_prev, P, nw]
    out_dtype = BF16 if fin is not None else F32
    return pl.pallas_call(
        functools.partial(_gla_kernel, reverse=reverse, finalize=fin is not None, nchunks=T // GLA_CHUNK),
        grid=(b_, nt),
        in_specs=in_specs,
        out_specs=[pl.BlockSpec((1, T, GLA_VW), lambda b, t: (b, tmap(t), 0)),
                   pl.BlockSpec((1, GLA_DV, GLA_KW), lambda b, t: (b, 0, 0))],
        out_shape=[jax.ShapeDtypeStruct((b_, L, GLA_VW), out_dtype),
                   jax.ShapeDtypeStruct((b_, GLA_DV, GLA_KW), F32)],
        scratch_shapes=[pltpu.VMEM((GLA_DV, GLA_KW), F32)],
        compiler_params=_cparams("arbitrary", "arbitrary"),
        name="gla",
    )(*args)


def _norm_rope(a, gain, bd, cos, sin):
    sq = a * a
    sh, sl = _split(sq)
    ss = _dot(sh, bd) + _dot(sl, bd)
    an = a * lax.rsqrt(ss * (1.0 / ATTN_HD) + EPS) * gain
    w = a.shape[-1]
    lane = lax.broadcasted_iota(jnp.int32, an.shape, 1)
    partner = jnp.where(lane % 2 == 0, pltpu.roll(an, w - 1, 1), pltpu.roll(an, 1, 1))
    return an * cos + partner * sin


Q_SCALE = (ATTN_HD ** -0.5) * math.log2(math.e)
ATTN_TK_MAX = 640


def _qkv_prep_kernel(q_ref, k_ref, v_ref, cq_ref, sq_ref, ck_ref, sk_ref, gq_ref, gk_ref,
                     bdq_ref, bdk_ref, eq_ref, ev_ref, qo_ref, ko_ref, vo_ref):
    nt = (((1,), (1,)), ((), ()))
    q = _norm_rope(q_ref[0].astype(F32), gq_ref[...], bdq_ref[...], cq_ref[...], sq_ref[...])
    qb = (q * Q_SCALE).astype(BF16)
    qo_ref[0] = lax.dot_general(eq_ref[...], qb, nt, preferred_element_type=F32).astype(BF16)
    k = _norm_rope(k_ref[0].astype(F32), gk_ref[...], bdk_ref[...], ck_ref[...], sk_ref[...]).astype(BF16)
    for g in range(ATTN_KV_HEADS):
        ko_ref[0, g] = k[:, g * ATTN_HD:(g + 1) * ATTN_HD]
    vb = v_ref[0].astype(BF16)
    vo_ref[0] = lax.dot_general(ev_ref[...], vb, nt, preferred_element_type=F32).astype(BF16)


def _qkv_prep_call(P, cos_q, sin_q, cos_k, sin_k, gq, gk, bdq, bdk, eye_q, eye_v):
    b_, L, _ = P.shape
    T = min(512, L)

    def col(name, width):
        blk = COLS[name][0] // width
        return pl.BlockSpec((1, T, width), lambda b, t: (b, t, blk))

    tab = lambda w: pl.BlockSpec((T, w), lambda b, t: (t, 0))
    const = lambda r, w: pl.BlockSpec((r, w), lambda b, t: (0, 0))
    return pl.pallas_call(
        _qkv_prep_kernel,
        grid=(b_, L // T),
        in_specs=[col('a_q', ATTN_QW), col('a_k', ATTN_KVW), col('a_v', ATTN_KVW),
                  tab(ATTN_QW), tab(ATTN_QW), tab(ATTN_KVW), tab(ATTN_KVW),
                  const(1, ATTN_QW), const(1, ATTN_KVW),
                  const(ATTN_QW, ATTN_QW), const(ATTN_KVW, ATTN_KVW),
                  const(ATTN_QW, ATTN_QW), const(ATTN_KVW, ATTN_KVW)],
        out_specs=[pl.BlockSpec((1, ATTN_QW, T), lambda b, t: (b, 0, t)),
                   pl.BlockSpec((1, ATTN_KV_HEADS, T, ATTN_HD), lambda b, t: (b, 0, t, 0)),
                   pl.BlockSpec((1, ATTN_KVW, T), lambda b, t: (b, 0, t))],
        out_shape=[jax.ShapeDtypeStruct((b_, ATTN_QW, L), BF16),
                   jax.ShapeDtypeStruct((b_, ATTN_KV_HEADS, L, ATTN_HD), BF16),
                   jax.ShapeDtypeStruct((b_, ATTN_KVW, L), BF16)],
        compiler_params=_cparams("arbitrary", "arbitrary"),
        name="qkv_prep",
    )(P, P, P, cos_q, sin_q, cos_k, sin_k, gq, gk, bdq, bdk, eye_q, eye_v)


ATTN_SUM_ROWS = 16
ATTN_UNROLL = 2
ATTN_TQ = 512


def _attn_kernel(qt_ref, k_ref, vt_ref, z_ref, e_ref, o_ref, m_scr, a_scr, acc_scr, s_scr, p_scr,
                 *, tk, nk):
    m_scr[...] = jnp.full(m_scr.shape, -jnp.inf, F32)
    acc_scr[...] = jnp.zeros(acc_scr.shape, F32)
    ones = jnp.ones((ATTN_SUM_ROWS, tk), BF16)

    def scores(j, slot):
        start = j * tk if isinstance(j, int) else pl.multiple_of(j * tk, tk)
        kj = k_ref[0, 0, pl.ds(start, tk), :]
        for h in range(ATTN_GROUP):
            s_scr[slot, h] = _dot(kj, qt_ref[0, h * ATTN_HD:(h + 1) * ATTN_HD, :])

    scores(0, 0)

    def step(j, slot):
        scores(min(j + 1, nk - 1) if isinstance(j, int) else jnp.minimum(j + 1, nk - 1), 1 - slot)
        start = j * tk if isinstance(j, int) else pl.multiple_of(j * tk, tk)
        vte = jnp.concatenate([vt_ref[0, :, pl.ds(start, tk)], ones], axis=0)
        for h in range(ATTN_GROUP):
            s = s_scr[slot, h]
            m_old = m_scr[h]
            m_new = jnp.maximum(m_old, jnp.max(s, axis=0, keepdims=True))
            a_scr[h] = jnp.exp2(m_old - m_new)
            m_scr[h] = m_new
            p_scr[h] = jnp.exp2(s - m_new).astype(BF16)
        for h in range(ATTN_GROUP):
            acc_scr[h] = a_scr[h] * acc_scr[h] + _dot(vte, p_scr[h])

    def trip(i, carry):
        for u in range(ATTN_UNROLL):
            step(ATTN_UNROLL * i + u, u % 2)
        return carry

    ntrips = nk // ATTN_UNROLL
    lax.fori_loop(0, ntrips, trip, 0)
    for j in range(ntrips * ATTN_UNROLL, nk):
        step(j, j % 2)
    tn = (((0,), (0,)), ((), ()))
    out = None
    for h in range(ATTN_GROUP):
        acc = acc_scr[h]
        oh, ol = _split(acc[:ATTN_HD] / acc[ATTN_HD:ATTN_HD + 1])
        part = (lax.dot_general(oh, e_ref[h], tn, preferred_element_type=F32)
                + lax.dot_general(ol, e_ref[h], tn, preferred_element_type=F32))
        out = part if out is None else out + part
    o_ref[0] = (out * _silu(z_ref[0].astype(F32))).astype(o_ref.dtype)


def _attn_call(qt, k, vt, P, e_heads):
    b_, _, L = qt.shape
    Lk = k.shape[2]
    tq = min(ATTN_TQ, L)
    tk = max(t for t in range(LANES, ATTN_TK_MAX + 1, LANES) if Lk % t == 0)
    gw = ATTN_GROUP * ATTN_HD
    zblk = COLS['a_z'][0] // gw
    return pl.pallas_call(
        functools.partial(_attn_kernel, tk=tk, nk=Lk // tk),
        grid=(b_, ATTN_KV_HEADS, L // tq),
        in_specs=[pl.BlockSpec((1, gw, tq), lambda b, g, i: (b, g, i)),
                  pl.BlockSpec((1, 1, Lk, ATTN_HD), lambda b, g, i: (b, g, 0, 0)),
                  pl.BlockSpec((1, ATTN_HD, Lk), lambda b, g, i: (b, g, 0)),
                  pl.BlockSpec((1, tq, gw), lambda b, g, i: (b, i, zblk + g)),
                  pl.BlockSpec((ATTN_GROUP, ATTN_HD, gw), lambda b, g, i: (0, 0, 0))],
        out_specs=pl.BlockSpec((1, tq, gw), lambda b, g, i: (b, i, g)),
        out_shape=jax.ShapeDtypeStruct((b_, L, ATTN_QW), BF16),
        scratch_shapes=[pltpu.VMEM((ATTN_GROUP, 1, tq), F32), pltpu.VMEM((ATTN_GROUP, 1, tq), F32),
                        pltpu.VMEM((ATTN_GROUP, ATTN_HD + ATTN_SUM_ROWS, tq), F32),
                        pltpu.VMEM((2, ATTN_GROUP, tk, tq), F32), pltpu.VMEM((ATTN_GROUP, tk, tq), BF16)],
        compiler_params=_cparams("arbitrary", "arbitrary", "arbitrary"),
        name="attn",
    )(qt, k, vt, P, e_heads)


def _short_conv_kernel(u_ref, up_ref, un_ref, w_ref, b_ref, v_ref, x1_ref, x2_ref):
    t = pl.program_id(1)
    nt = pl.num_programs(1)
    u = u_ref[0].astype(F32)
    T = u.shape[0]
    hr = up_ref.shape[1]
    prev_row = jnp.where(t > 0, up_ref[0, hr - 1:hr, :].astype(F32), 0.0)
    next_row = jnp.where(t < nt - 1, un_ref[0, 0:1, :].astype(F32), 0.0)
    row = lax.broadcasted_iota(jnp.int32, u.shape, 0)
    u_m1 = jnp.where(row == 0, prev_row, pltpu.roll(u, 1, 0))
    u_p1 = jnp.where(row == T - 1, next_row, pltpu.roll(u, T - 1, 0))
    w = w_ref[...]
    out = b_ref[...] + u_m1 * w[0:1] + u * w[1:2] + u_p1 * w[2:3]
    v_ref[0] = out[:, 0:HY_W]
    x1_ref[0] = out[:, HY_W:2 * HY_W]
    x2_ref[0] = out[:, 2 * HY_W:3 * HY_W]


def _short_conv_call(P, w, bias):
    b_, L, _ = P.shape
    T = min(512, L)
    cw = (HY_ORDER + 1) * HY_W
    blk = COLS['y_u'][0] // cw
    hr = 16
    hb = T // hr
    nh = L // hr
    out_spec = pl.BlockSpec((1, T, HY_W), lambda b, t: (b, t, 0))
    shp = jax.ShapeDtypeStruct((b_, L, HY_W), F32)
    return pl.pallas_call(
        _short_conv_kernel,
        grid=(b_, L // T),
        in_specs=[pl.BlockSpec((1, T, cw), lambda b, t: (b, t, blk)),
                  pl.BlockSpec((1, hr, cw), lambda b, t: (b, jnp.maximum(t * hb - 1, 0), blk)),
                  pl.BlockSpec((1, hr, cw), lambda b, t: (b, jnp.minimum((t + 1) * hb, nh - 1), blk)),
                  pl.BlockSpec((HY_SHORT, cw), lambda b, t: (0, 0)),
                  pl.BlockSpec((1, cw), lambda b, t: (0, 0))],
        out_specs=[out_spec, out_spec, out_spec],
        out_shape=[shp, shp, shp],
        compiler_params=_cparams("arbitrary", "arbitrary"),
        name="short_conv",
    )(P, P, P, w, bias)


FFT_ROWS = 8


def _filter_kernel(emb_ref, w1_ref, b1_ref, f1_ref, w2_ref, b2_ref, f2_ref, w3_ref, dl_ref,
                   hf_ref, hb_ref, sum_ref):
    t = pl.program_id(0)
    emb = emb_ref[...]
    h = jnp.sin(f1_ref[...] * (_dot_hp(emb, w1_ref[...]) + b1_ref[...]))
    h = jnp.sin(f2_ref[...] * (_dot_hp(h, w2_ref[...]) + b2_ref[...]))
    hh, hl = _split(h)
    tt = emb[:, 0:1]

    @pl.when(t == 0)
    def _():
        sum_ref[...] = jnp.zeros_like(sum_ref)

    lag0 = (lax.broadcasted_iota(jnp.int32, (h.shape[0], HY_W), 0) + t * h.shape[0]) == 0
    for o in range(HY_ORDER):
        for d, out_ref in enumerate((hf_ref, hb_ref)):
            cols = slice((2 * o + d) * HY_W, (2 * o + d + 1) * HY_W)
            wh, wl = _split(w3_ref[:, cols])
            f = _dot(hh, wh) + _dot(hl, wh) + _dot(hh, wl)
            f = f * (jnp.exp(-tt * dl_ref[:, cols]) + HY_MOD_SHIFT)
            sum_ref[:, cols] += jnp.broadcast_to(jnp.sum(jnp.abs(f), axis=0, keepdims=True), (8, HY_W))
            if d == 1:
                f = jnp.where(lag0, 0.0, f)
            out_ref[o] = pltpu.einshape("(an)c->a(nc)", f, a=FFT_ROWS)


def _filter_call(emb, w1, b1, f1, w2, b2, f2, w3, deltas, n1, n2):
    L = emb.shape[0]
    TL = FFT_ROWS * n2
    const = lambda r, w: pl.BlockSpec((r, w), lambda t: (0, 0))
    hspec = pl.BlockSpec((HY_ORDER, FFT_ROWS, n2 * HY_W), lambda t: (0, t, 0))
    hshape = jax.ShapeDtypeStruct((HY_ORDER, n1 // 2, n2 * HY_W), F32)
    return pl.pallas_call(
        _filter_kernel,
        grid=(L // TL,),
        in_specs=[pl.BlockSpec((TL, LANES), lambda t: (t, 0)),
                  const(LANES, HY_FFN), const(1, HY_FFN), const(1, HY_FFN),
                  const(HY_FFN, HY_FFN), const(1, HY_FFN), const(1, HY_FFN),
                  const(HY_FFN, N_FILT), const(1, N_FILT)],
        out_specs=[hspec, hspec, pl.BlockSpec((8, N_FILT), lambda t: (0, 0))],
        out_shape=[hshape, hshape, jax.ShapeDtypeStruct((8, N_FILT), F32)],
        compiler_params=_cparams("arbitrary"),
        name="hy_filter",
    )(emb, w1, b1, f1, w2, b2, f2, w3, deltas)


def _fft_sizes(n):
    lg = int(round(math.log2(n)))
    assert 1 << lg == n
    n1 = 1 << ((lg + 1) // 2)
    return n1, n // n1


def _bf16_pair(a):
    a = np.asarray(a, np.float32)
    hi = jnp.asarray(a, F32).astype(BF16)
    lo = (jnp.asarray(a, F32) - hi.astype(F32)).astype(BF16)
    return hi, lo


def _fft_tables(n1, n2):
    n = n1 * n2
    h1 = n1 // 2
    k1 = np.arange(n1)[:, None]
    a = 2.0 * np.pi * ((k1 * np.arange(h1)[None, :]) % n1) / n1
    c, s = np.cos(a), np.sin(a)
    fa = np.block([[c, s], [-s, c]])
    fa_real = np.concatenate([c, -s], axis=0)
    fd = np.block([[c.T, -s.T], [s.T, c.T]])
    k2 = np.arange(n2)[:, None]
    b = 2.0 * np.pi * ((k2 * np.arange(n2)[None, :]) % n2) / n2
    cb, sb = np.cos(b), np.sin(b)
    fb = np.block([[cb, sb], [-sb, cb]])
    fc = np.block([[cb, -sb], [sb, cb]])
    kn = (jnp.arange(n1, dtype=jnp.int32)[:, None] * jnp.arange(n2, dtype=jnp.int32)[None, :]) % n
    tw = kn.astype(F32) * (2.0 * math.pi / n)
    twr = jnp.broadcast_to(jnp.cos(tw)[:, :, None], (n1, n2, LANES))
    twi = jnp.broadcast_to(-jnp.sin(tw)[:, :, None], (n1, n2, LANES))
    return dict(fa=_bf16_pair(fa), fa_real=_bf16_pair(fa_real), fd=_bf16_pair(fd),
                fb=_bf16_pair(fb), fc=_bf16_pair(fc), twr=twr, twi=twi)


def _fa_kernel(x_ref, mh_ref, ml_ref, o_ref, xh_scr, xl_scr, *, packed, part_axis):
    @pl.when(pl.program_id(part_axis) == 0)
    def _():
        if packed:
            x = x_ref[...]
            x = x.reshape(x.shape[0] * x.shape[1], x.shape[2], x.shape[3])
            x = pltpu.einshape("rnc->r(nc)", x)
        else:
            x = x_ref[0]
        xh, xl = _split(x)
        xh_scr[...] = xh
        xl_scr[...] = xl

    mh = mh_ref[...]
    xh = xh_scr[...]
    o_ref[0] = _dot(mh, xh) + _dot(mh, xl_scr[...]) + _dot(ml_ref[...], xh)


def _fa_call(x4, mats, n1):
    mh, ml = mats
    _, h1, n2, wd = x4.shape
    wt = FFT_ROWS * wd
    return pl.pallas_call(
        functools.partial(_fa_kernel, packed=True, part_axis=1),
        grid=(n2 // FFT_ROWS, 2),
        in_specs=[pl.BlockSpec((2, h1, FFT_ROWS, wd), lambda j, p: (0, 0, j, 0)),
                  pl.BlockSpec((n1, n1), lambda j, p: (p, 0)),
                  pl.BlockSpec((n1, n1), lambda j, p: (p, 0))],
        out_specs=pl.BlockSpec((1, n1, wt), lambda j, p: (p, 0, j)),
        out_shape=jax.ShapeDtypeStruct((2, n1, n2 * wd), F32),
        scratch_shapes=[pltpu.VMEM((n1, wt), BF16), pltpu.VMEM((n1, wt), BF16)],
        compiler_params=_cparams("arbitrary", "arbitrary"),
        name="fft_a",
    )(x4, mh, ml)


def _fa_real_call(x, mats, n1):
    mh, ml = mats
    ng, h1, lanes = x.shape
    wt = min(FFT_ROWS * HY_W, lanes)
    out = pl.pallas_call(
        functools.partial(_fa_kernel, packed=False, part_axis=2),
        grid=(ng, lanes // wt, 2),
        in_specs=[pl.BlockSpec((1, h1, wt), lambda g, j, p: (g, 0, j)),
                  pl.BlockSpec((n1, h1), lambda g, j, p: (p, 0)),
                  pl.BlockSpec((n1, h1), lambda g, j, p: (p, 0))],
        out_specs=pl.BlockSpec((1, n1, wt), lambda g, j, p: (2 * g + p, 0, j)),
        out_shape=jax.ShapeDtypeStruct((2 * ng, n1, lanes), F32),
        scratch_shapes=[pltpu.VMEM((h1, wt), BF16), pltpu.VMEM((h1, wt), BF16)],
        compiler_params=_cparams("arbitrary", "arbitrary", "arbitrary"),
        name="fft_a_real",
    )(x, mh, ml)
    return out.reshape(ng, 2, n1, lanes)


def _lane_tile(a, width):
    return jnp.concatenate([a] * (width // a.shape[-1]), axis=-1)


def _fb_filter_kernel(tf_ref, tb_ref, twr_ref, twi_ref, fh_ref, fl_ref, sf_ref, sb_ref, g_ref,
                      tf4_scr, tb4_scr, *, n2, scale):
    w = g_ref.shape[-1]
    tf4_scr[...] = pltpu.einshape("pk(nc)->pknc", tf_ref[0], n=n2)
    tb4_scr[...] = pltpu.einshape("pk(nc)->pknc", tb_ref[0], n=n2)
    inv = jnp.concatenate([scale / sf_ref[0:1, :], scale / sb_ref[0:1, :]], axis=1)

    def body(kk, carry):
        twr = _lane_tile(twr_ref[kk], 2 * w)
        twi = _lane_tile(twi_ref[kk], 2 * w)
        tr = jnp.concatenate([tf4_scr[0, kk], tb4_scr[0, kk]], axis=1)
        ti = jnp.concatenate([tf4_scr[1, kk], tb4_scr[1, kk]], axis=1)
        p = jnp.concatenate([tr * twr - ti * twi, tr * twi + ti * twr], axis=0)
        z = _dot_mat(fh_ref[...], fl_ref[...], p) * inv
        g_ref[0, 0, kk] = z[:n2, :w] + z[:n2, w:]
        g_ref[0, 1, kk] = z[n2:, :w] - z[n2:, w:]
        return carry

    lax.fori_loop(0, FFT_ROWS, body, 0, unroll=2)


def _fb_filter_call(thf, thb, sums, tabs, n1, n2):
    fh, fl = tabs['fb']
    lanes = n2 * HY_W
    tspec = pl.BlockSpec((1, 2, FFT_ROWS, lanes), lambda g, i: (g, 0, i, 0))
    twspec = pl.BlockSpec((FFT_ROWS, n2, LANES), lambda g, i: (i, 0, 0))
    sspec = lambda d: pl.BlockSpec((8, HY_W), lambda g, i: (0, 2 * g + d))
    return pl.pallas_call(
        functools.partial(_fb_filter_kernel, n2=n2, scale=1.0 / (n1 * n2)),
        grid=(HY_ORDER, n1 // FFT_ROWS),
        in_specs=[tspec, tspec, twspec, twspec,
                  pl.BlockSpec(fh.shape, lambda g, i: (0, 0)), pl.BlockSpec(fl.shape, lambda g, i: (0, 0)),
                  sspec(0), sspec(1)],
        out_specs=pl.BlockSpec((1, 2, FFT_ROWS, n2, HY_W), lambda g, i: (g, 0, i, 0, 0)),
        out_shape=jax.ShapeDtypeStruct((HY_ORDER, 2, n1, n2, HY_W), F32),
        scratch_shapes=[pltpu.VMEM((2, FFT_ROWS, n2, HY_W), F32), pltpu.VMEM((2, FFT_ROWS, n2, HY_W), F32)],
        compiler_params=_cparams("arbitrary", "arbitrary"),
        name="fft_b_filter",
    )(thf, thb, tabs['twr'], tabs['twi'], fh, fl, sums, sums)


def _fb_kernel(t_ref, twr_ref, twi_ref, fh_ref, fl_ref, ch_ref, cl_ref, g_ref, o_ref, t4_scr, *, n2):
    w = g_ref.shape[-1]
    t4_scr[...] = pltpu.einshape("pk(nc)->pknc", t_ref[...], n=n2)

    def body(kk, carry):
        tr = t4_scr[0, kk]
        ti = t4_scr[1, kk]
        twr = _lane_tile(twr_ref[kk], w)
        twi = _lane_tile(twi_ref[kk], w)
        p = jnp.concatenate([tr * twr - ti * twi, tr * twi + ti * twr], axis=0)
        z = _dot_mat(fh_ref[...], fl_ref[...], p)
        zr, zi = z[:n2], z[n2:]
        gr = g_ref[0, 0, kk]
        gi = g_ref[0, 1, kk]
        y = jnp.concatenate([zr * gr - zi * gi, zr * gi + zi * gr], axis=0)
        v = _dot_mat(ch_ref[...], cl_ref[...], y)
        vr, vi = v[:n2], v[n2:]
        t4_scr[0, kk] = vr * twr + vi * twi
        t4_scr[1, kk] = vi * twr - vr * twi
        return carry

    lax.fori_loop(0, FFT_ROWS, body, 0, unroll=2)
    o_ref[...] = pltpu.einshape("pknc->pk(nc)", t4_scr[...])


def _fb_call(t, g, order, tabs, n1, n2):
    fh, fl = tabs['fb']
    ch, cl = tabs['fc']
    lanes = n2 * HY_W
    mat = lambda m: pl.BlockSpec(m.shape, lambda i: (0, 0))
    dat = pl.BlockSpec((2, FFT_ROWS, lanes), lambda i: (0, i, 0))
    twspec = pl.BlockSpec((FFT_ROWS, n2, LANES), lambda i: (i, 0, 0))
    return pl.pallas_call(
        functools.partial(_fb_kernel, n2=n2),
        grid=(n1 // FFT_ROWS,),
        in_specs=[dat, twspec, twspec, mat(fh), mat(fl), mat(ch), mat(cl),
                  pl.BlockSpec((1, 2, FFT_ROWS, n2, HY_W), lambda i: (order, 0, i, 0, 0))],
        out_specs=dat,
        out_shape=jax.ShapeDtypeStruct((2, n1, lanes), F32),
        scratch_shapes=[pltpu.VMEM((2, FFT_ROWS, n2, HY_W), F32)],
        compiler_params=_cparams("arbitrary"),
        name="fft_b",
    )(t, tabs['twr'], tabs['twi'], fh, fl, ch, cl, g)


def _fd_kernel(u_ref, mh_ref, ml_ref, z_ref, x_ref, sk_ref, o_ref):
    u = u_ref[...]
    u = u.reshape(u.shape[0] * u.shape[1], u.shape[2])
    y = _dot_mat(mh_ref[...], ml_ref[...], u)
    y = pltpu.einshape("r(nc)->rnc", y, n=FFT_ROWS)
    o_ref[0] = x_ref[0] * (y + sk_ref[...] * z_ref[0])


def _fd_call(u, mats, z4, x4, skip_row, n1):
    mh, ml = mats
    b_, h1, n2, wd = z4.shape
    wt = FFT_ROWS * wd
    dat = pl.BlockSpec((1, h1, FFT_ROWS, wd), lambda j, p: (p, 0, j, 0))
    return pl.pallas_call(
        _fd_kernel,
        grid=(n2 // FFT_ROWS, b_),
        in_specs=[pl.BlockSpec((2, n1, wt), lambda j, p: (0, 0, j)),
                  pl.BlockSpec((h1, 2 * n1), lambda j, p: (p, 0)),
                  pl.BlockSpec((h1, 2 * n1), lambda j, p: (p, 0)),
                  dat, dat, pl.BlockSpec((1, wd), lambda j, p: (0, 0))],
        out_specs=dat,
        out_shape=jax.ShapeDtypeStruct(z4.shape, F32),
        compiler_params=pltpu.CompilerParams(dimension_semantics=("arbitrary", "arbitrary"),
                                             vmem_limit_bytes=FFT_D_VMEM_LIMIT),
        name="fft_d",
    )(u, mh, ml, z4, x4, skip_row)


def _hyena_filter_spectrum(L, tabs, n1, n2, fp):
    f32 = F32
    t = jnp.linspace(0.0, 1.0, L, dtype=f32)[:, None]
    w = 2.0 * math.pi * jnp.arange(L, dtype=f32)[:, None] / L
    fr = jnp.linspace(1e-4, HY_BANDS - 1, HY_BANDS, dtype=f32)[None]
    emb = jnp.concatenate([t, jnp.cos(fr * w), -jnp.sin(fr * w)], axis=-1)
    emb = jnp.pad(emb, ((0, 0), (0, LANES - HY_EMB)))
    deltas = jnp.abs(jnp.linspace(math.log(HY_DECAY_TARGET) / HY_DECAY_SHORT_PCT,
                                  math.log(HY_DECAY_TARGET) / HY_DECAY_LONG_PCT, N_FILT, dtype=f32))[None]
    f1_w, f1_b, f1_freq, f2_w, f2_b, f2_freq, f3_w = fp
    w1 = jnp.pad(f1_w, ((0, LANES - HY_EMB), (0, 0)))
    hf, hb, sums = _filter_call(emb, w1, f1_b[None], f1_freq[None], f2_w, f2_b[None],
                                f2_freq[None], f3_w, deltas, n1, n2)
    thf = _fa_real_call(hf, tabs['fa_real'], n1)
    thb = _fa_real_call(hb, tabs['fa_real'], n1)
    return _fb_filter_call(thf, thb, sums, tabs, n1, n2)


def _hyena_conv(z, xg, g, order, skip, tabs, n1, n2):
    b_, L, wd = z.shape
    assert b_ == 2, "the two batch rows ride as the real and imaginary parts of one transform"
    z4 = z.reshape(b_, n1 // 2, n2, wd)
    x4 = xg.reshape(b_, n1 // 2, n2, wd)
    t = _fa_call(z4, tabs['fa'], n1)
    u = _fb_call(t, g, order, tabs, n1, n2)
    out = _fd_call(u, tabs['fd'], z4, x4, skip[None, :], n1)
    return out.reshape(b_, L, wd)


def _merge_kernel(yg_ref, ya_ref, yh_ref, yz_ref, mg_ref, ma_ref, mh_ref, x_ref, gate_ref,
                  wg_ref, wa_ref, wh_ref, wo_ref, fn_ref, o_ref, *, final):
    yh = (yh_ref[0] * _silu(yz_ref[0].astype(F32))).astype(BF16)
    y = (_sigmoid(mg_ref[0].astype(F32)) * _dot(yg_ref[0], wg_ref[...])
         + _sigmoid(ma_ref[0].astype(F32)) * _dot(ya_ref[0], wa_ref[...])
         + _sigmoid(mh_ref[0].astype(F32)) * _dot(yh, wh_ref[...]))
    out = _dot(y.astype(BF16), wo_ref[...])
    xn = x_ref[0] + gate_ref[0] * out
    if final:
        ms = jnp.mean(xn * xn, axis=-1, keepdims=True)
        xn = xn * lax.rsqrt(ms + EPS) * fn_ref[...]
    o_ref[0] = xn


def _merge_call(y_gla, y_attn, y_hy, P, x, gate, wg, wa, wh, wo, fnorm, final):
    b_, L, d = x.shape
    T = min(512, L)
    row = lambda w, blk=0: pl.BlockSpec((1, T, w), lambda b, t: (b, t, blk))
    const = lambda r, w: pl.BlockSpec((r, w), lambda b, t: (0, 0))
    mblk = COLS['m'][0] // d
    return pl.pallas_call(
        functools.partial(_merge_kernel, final=final),
        grid=(b_, L // T),
        in_specs=[row(BRANCH_W), row(BRANCH_W), row(BRANCH_W), row(HY_W, COLS['y_z'][0] // HY_W),
                  row(d, mblk), row(d, mblk + 1), row(d, mblk + 2), row(d),
                  pl.BlockSpec((1, 1, d), lambda b, t: (b, 0, 0)),
                  const(BRANCH_W, d), const(BRANCH_W, d), const(BRANCH_W, d), const(d, d), const(1, d)],
        out_specs=row(d),
        out_shape=jax.ShapeDtypeStruct((b_, L, d), F32),
        compiler_params=_cparams("arbitrary", "arbitrary"),
        name="merge",
    )(y_gla, y_attn, y_hy, P, P, P, P, x, gate, wg, wa, wh, wo, fnorm)


def _pack_w_in(w):
    parts = dict(zip(('g_q', 'g_k', 'g_v', 'g_z', 'g_af', 'g_ab', 'a_q', 'a_k', 'a_v', 'a_z', 'y_u', 'y_z', 'm'),
                     jnp.split(w, [int(i) for i in np.cumsum(SPLITS)[:-1]], axis=-1)))
    parts['g_a'] = jnp.pad(jnp.concatenate([parts['g_af'], parts['g_ab']], axis=-1),
                           ((0, 0), (0, LANES - 2 * GLA_RANK)))
    order = sorted(COLS, key=lambda n: COLS[n][0])
    return jnp.concatenate([parts[n] for n in order], axis=-1).astype(BF16)


def _rope_tables(L, heads):
    t = jnp.arange(L)
    row = (t // GRID_W).astype(F32)
    colp = (t % GRID_W).astype(F32)
    half = ATTN_HD // 2
    inv = ROPE_THETA ** (-jnp.arange(0, half, 2, dtype=F32) / half)
    ang = jnp.concatenate([row[:, None] * inv, colp[:, None] * inv], axis=-1)
    cos = jnp.repeat(jnp.cos(ang), 2, axis=-1)
    sin = jnp.stack([-jnp.sin(ang), jnp.sin(ang)], axis=-1).reshape(L, ATTN_HD)
    return jnp.tile(cos, (1, heads)), jnp.tile(sin, (1, heads))


def _identity_rope(L, heads):
    return jnp.ones((L, heads * ATTN_HD), F32), jnp.zeros((L, heads * ATTN_HD), F32)


def _block_diag_ones(width):
    i = np.arange(width) // ATTN_HD
    return jnp.asarray((i[:, None] == i[None, :]).astype(np.float32)).astype(BF16)


def kernel(x, c, ctx, c_ctx, w_ada, b_ada, w_in, gla_wa_f, gla_ba_f, gla_wa_b, gla_ba_b, gla_norm,
           attn_qnorm, attn_knorm, hy_conv_w, hy_conv_b, hy_f1_w, hy_f1_b, hy_f1_freq, hy_f2_w,
           hy_f2_b, hy_f2_freq, hy_f3_w, hy_skip, w_br_gla, w_br_attn, w_br_hy, w_out, final_norm):
    b_, L, d = x.shape
    Lc = ctx.shape[1]
    depth = w_ada.shape[0]

    cos_q, sin_q = _rope_tables(L, ATTN_HEADS)
    cos_k, sin_k = cos_q[:, :ATTN_KVW], sin_q[:, :ATTN_KVW]
    cos_qc, sin_qc = _identity_rope(Lc, ATTN_HEADS)
    cos_kc, sin_kc = cos_qc[:, :ATTN_KVW], sin_qc[:, :ATTN_KVW]
    bdq, bdk = _block_diag_ones(ATTN_QW), _block_diag_ones(ATTN_KVW)
    eye_q, eye_v = jnp.eye(ATTN_QW, dtype=BF16), jnp.eye(ATTN_KVW, dtype=BF16)
    gw = ATTN_GROUP * ATTN_HD
    e_heads = jnp.stack([jnp.eye(ATTN_HD, gw, k=h * ATTN_HD, dtype=BF16) for h in range(ATTN_GROUP)])
    n1, n2 = _fft_sizes(2 * L)
    n1c, n2c = _fft_sizes(2 * Lc)
    tabs = _fft_tables(n1, n2)
    tabs_c = _fft_tables(n1c, n2c)
    zero_state = jnp.zeros((b_, GLA_DV, GLA_KW), F32)

    cmat = jnp.concatenate([c, c_ctx[None], jnp.zeros((8 - b_ - 1, d), F32)], axis=0)
    mods = _ada_call(cmat, w_ada, b_ada)

    for l in range(depth):
        need_ctx = l < depth - 1
        shift, scale, gate = [m[:b_, None, :] for m in jnp.split(mods[l], 3, axis=-1)]
        shift_c, scale_c, gate_c = [jnp.broadcast_to(m[b_:b_ + 1, None, :], (b_, 1, d))
                                    for m in jnp.split(mods[l], 3, axis=-1)]
        w_packed = _pack_w_in(w_in[l])
        P = _proj_call(x, scale, shift, w_packed, BF16, PROJ_TN)
        Pc = _proj_call(ctx, scale_c, shift_c, w_packed, BF16, PROJ_TN)
        ga0 = COLS['g_a'][0]
        w_ga = w_packed[:, ga0:ga0 + LANES]
        GA = _proj_call(x, scale, shift, w_ga, F32, LANES)
        GAc = _proj_call(ctx, scale_c, shift_c, w_ga, F32, LANES)

        rk = GLA_RANK
        wa_f = jnp.zeros((LANES, GLA_KW), F32).at[0:rk].set(gla_wa_f[l])
        wa_b = jnp.zeros((LANES, GLA_KW), F32).at[rk:2 * rk].set(gla_wa_b[l])
        ba_f, ba_b = gla_ba_f[l][None], gla_ba_b[l][None]
        nw = gla_norm[l][None]
        oc_f, sc_f = _gla_call(Pc, GAc, wa_f, ba_f, zero_state, reverse=False)
        yc_gla, sc_b = _gla_call(Pc, GAc, wa_b, ba_b, zero_state, reverse=True, fin=(oc_f, nw))
        o_f, _ = _gla_call(P, GA, wa_f, ba_f, sc_f, reverse=False)
        y_gla, _ = _gla_call(P, GA, wa_b, ba_b, sc_b, reverse=True, fin=(o_f, nw))

        gq = jnp.tile(attn_qnorm[l], ATTN_HEADS)[None]
        gk = jnp.tile(attn_knorm[l], ATTN_KV_HEADS)[None]
        q_a, k_a, v_a = _qkv_prep_call(P, cos_q, sin_q, cos_k, sin_k, gq, gk, bdq, bdk, eye_q, eye_v)
        qc_a, kc_a, vc_a = _qkv_prep_call(Pc, cos_qc, sin_qc, cos_kc, sin_kc, gq, gk, bdq, bdk,
                                          eye_q, eye_v)
        y_attn = _attn_call(q_a, jnp.concatenate([k_a, kc_a], axis=2),
                            jnp.concatenate([v_a, vc_a], axis=2), P, e_heads)

        fp = (hy_f1_w[l], hy_f1_b[l], hy_f1_freq[l], hy_f2_w[l], hy_f2_b[l], hy_f2_freq[l], hy_f3_w[l])
        g_spec = _hyena_filter_spectrum(L, tabs, n1, n2, fp)
        v0, x1, x2 = _short_conv_call(P, hy_conv_w[l], hy_conv_b[l][None])
        z1 = _hyena_conv(v0, x1, g_spec, 0, hy_skip[l, 0], tabs, n1, n2)
        y_hy = _hyena_conv(z1, x2, g_spec, 1, hy_skip[l, 1], tabs, n1, n2)

        wg, wa, wh, wo = (w_br_gla[l].astype(BF16), w_br_attn[l].astype(BF16),
                          w_br_hy[l].astype(BF16), w_out[l].astype(BF16))
        fn = final_norm[None]
        x_new = _merge_call(y_gla, y_attn, y_hy, P, x, gate, wg, wa, wh, wo, fn, final=not need_ctx)

        if need_ctx:
            yc_attn = _attn_call(qc_a, kc_a, vc_a, Pc, e_heads)
            gc_spec = _hyena_filter_spectrum(Lc, tabs_c, n1c, n2c, fp)
            vc0, xc1, xc2 = _short_conv_call(Pc, hy_conv_w[l], hy_conv_b[l][None])
            zc1 = _hyena_conv(vc0, xc1, gc_spec, 0, hy_skip[l, 0], tabs_c, n1c, n2c)
            yc_hy = _hyena_conv(zc1, xc2, gc_spec, 1, hy_skip[l, 1], tabs_c, n1c, n2c)
            ctx = _merge_call(yc_gla, yc_attn, yc_hy, Pc, ctx, gate_c, wg, wa, wh, wo, fn, final=False)
        x = x_new

    return x
```

```python
import functools
import math

import numpy as np
import jax
import jax.numpy as jnp
from jax import lax
from jax.experimental import pallas as pl
from jax.experimental.pallas import tpu as pltpu

F32 = jnp.float32
BF16 = jnp.bfloat16

D_MODEL = 1024
GRID_W = 64
BRANCH_W = D_MODEL // 2
N_BRANCH = 3
EPS = 1e-6
GLA_HEADS = 4
GLA_DV = BRANCH_W // GLA_HEADS
GLA_DK = GLA_DV // 2
GLA_KW = GLA_HEADS * GLA_DK
GLA_VW = GLA_HEADS * GLA_DV
GLA_RANK = 16
GLA_GATE_NORM = 16.0
GLA_CHUNK = 64
GLA_BLOCK = 256
ATTN_HD = 64
ATTN_HEADS = BRANCH_W // ATTN_HD
ATTN_KV_HEADS = ATTN_HEADS // 4
ATTN_GROUP = ATTN_HEADS // ATTN_KV_HEADS
ATTN_QW = ATTN_HEADS * ATTN_HD
ATTN_KVW = ATTN_KV_HEADS * ATTN_HD
ROPE_THETA = 10000.0
HY_W = BRANCH_W
HY_ORDER = 2
HY_EMB = 33
HY_BANDS = (HY_EMB - 1) // 2
HY_FFN = 64
HY_SHORT = 3
HY_MOD_SHIFT = 0.05
HY_DECAY_SHORT_PCT = 0.3
HY_DECAY_LONG_PCT = 1.5
HY_DECAY_TARGET = 1e-2
N_FILT = 2 * HY_ORDER * HY_W
SPLITS = (GLA_KW, GLA_KW, GLA_VW, GLA_VW, GLA_RANK, GLA_RANK,
          ATTN_QW, ATTN_KVW, ATTN_KVW, ATTN_QW,
          (HY_ORDER + 1) * HY_W, HY_W,
          N_BRANCH * D_MODEL)

LANES = 128
VMEM_LIMIT = 48 * 1024 * 1024
FFT_D_VMEM_LIMIT = 56 * 1024 * 1024

COLS = {
    'm': (0, 3072), 'y_u': (3072, 1536), 'g_v': (4608, 512), 'g_z': (5120, 512),
    'a_q': (5632, 512), 'a_z': (6144, 512), 'y_z': (6656, 512), 'g_q': (7168, 256),
    'g_k': (7424, 256), 'a_k': (7680, 128), 'a_v': (7808, 128), 'g_a': (7936, 128),
}
N_PROJ = 8064
PROJ_TN = 2688


def _cparams(*sem):
    return pltpu.CompilerParams(dimension_semantics=sem, vmem_limit_bytes=VMEM_LIMIT)


def _split(a):
    hi = a.astype(BF16)
    lo = (a - hi.astype(F32)).astype(BF16)
    return hi, lo


def _dot(a, b):
    return jnp.dot(a, b, preferred_element_type=F32)


def _dot_hp(a, b):
    ah, al = _split(a)
    bh, bl = _split(b)
    return _dot(ah, bh) + _dot(al, bh) + _dot(ah, bl)


def _dot_mat(mh, ml, x):
    xh = x.astype(BF16)
    return _dot(mh, xh) + _dot(ml, xh)


def _sigmoid(x):
    return 1.0 / (1.0 + jnp.exp(-x))


def _silu(x):
    return x * _sigmoid(x)


def _ada_kernel(c_ref, w_ref, b_ref, o_ref):
    o_ref[0] = _dot_hp(_silu(c_ref[...]), w_ref[0]) + b_ref[0]


def _ada_call(cmat, w_ada, b_ada):
    depth, d, n3 = w_ada.shape
    tn = 1024
    return pl.pallas_call(
        _ada_kernel,
        grid=(depth, n3 // tn),
        in_specs=[pl.BlockSpec((8, d), lambda l, j: (0, 0)),
                  pl.BlockSpec((1, d, tn), lambda l, j: (l, 0, j)),
                  pl.BlockSpec((1, 1, tn), lambda l, j: (l, 0, j))],
        out_specs=pl.BlockSpec((1, 8, tn), lambda l, j: (l, 0, j)),
        out_shape=jax.ShapeDtypeStruct((depth, 8, n3), F32),
        compiler_params=_cparams("arbitrary", "arbitrary"),
        name="ada",
    )(cmat, w_ada, b_ada.reshape(depth, 1, n3))


def _proj_kernel(x_ref, sc_ref, sh_ref, w_ref, o_ref):
    x = x_ref[0]
    ms = jnp.mean(x * x, axis=-1, keepdims=True)
    h = x * lax.rsqrt(ms + EPS) * (1.0 + sc_ref[0]) + sh_ref[0]
    o_ref[0] = _dot(h.astype(BF16), w_ref[...]).astype(o_ref.dtype)


def _proj_call(x, scale, shift, w, out_dtype, tn):
    b_, L, d = x.shape
    n = w.shape[1]
    tm = min(512, L)
    return pl.pallas_call(
        _proj_kernel,
        grid=(n // tn, b_, L // tm),
        in_specs=[pl.BlockSpec((1, tm, d), lambda j, b, i: (b, i, 0)),
                  pl.BlockSpec((1, 1, d), lambda j, b, i: (b, 0, 0)),
                  pl.BlockSpec((1, 1, d), lambda j, b, i: (b, 0, 0)),
                  pl.BlockSpec((d, tn), lambda j, b, i: (0, j))],
        out_specs=pl.BlockSpec((1, tm, tn), lambda j, b, i: (b, i, j)),
        out_shape=jax.ShapeDtypeStruct((b_, L, n), out_dtype),
        compiler_params=_cparams("arbitrary", "arbitrary", "arbitrary"),
        name="proj",
    )(x, scale, shift, w)


def _gla_kernel(*refs, reverse, finalize, nchunks):
    if finalize:
        (q_ref, k_ref, v_ref, ga_ref, wa_ref, ba_ref, s0_ref, tri_ref, op_ref, z_ref, nw_ref,
         o_ref, sfin_ref, s_scr) = refs
    else:
        (q_ref, k_ref, v_ref, ga_ref, wa_ref, ba_ref, s0_ref, tri_ref,
         o_ref, sfin_ref, s_scr) = refs
    C = GLA_CHUNK
    T = nchunks * C
    nt_dims = (((1,), (1,)), ((), ()))
    tn_dims = (((0,), (0,)), ((), ()))

    @pl.when(pl.program_id(1) == 0)
    def _():
        s_scr[...] = s0_ref[0]

    xg = _dot_hp(ga_ref[0], wa_ref[...]) + ba_ref[...]
    g = (jnp.minimum(xg, 0.0) - jnp.log(1.0 + jnp.exp(-jnp.abs(xg)))) * (1.0 / GLA_GATE_NORM)

    tri = tri_ref[...]
    g1 = g.astype(BF16)
    r1 = g - g1.astype(F32)
    g2 = r1.astype(BF16)
    g3 = (r1 - g2.astype(F32)).astype(BF16)
    b = _dot(tri, g1) + _dot(tri, g2) + _dot(tri, g3)

    def chunk_row(r):
        return jnp.concatenate([jnp.broadcast_to(b[c * C + r:c * C + r + 1], (C, GLA_KW))
                                for c in range(nchunks)], axis=0)

    tot_row = 0 if reverse else C - 1
    bm = chunk_row(C // 2)
    bt = chunk_row(tot_row)
    ri = lax.broadcasted_iota(jnp.int32, (T, T), 0)
    ci = lax.broadcasted_iota(jnp.int32, (T, T), 1)
    same_chunk = (ri // C) == (ci // C)
    mask = same_chunk & ((ci >= ri) if reverse else (ci <= ri))
    lane = lax.broadcasted_iota(jnp.int32, (1, GLA_KW), 1)
    hmask = [(lane >= h * GLA_DK) & (lane < (h + 1) * GLA_DK) for h in range(GLA_HEADS)]

    q = q_ref[0].astype(F32) * (GLA_DK ** -0.5)
    k = k_ref[0].astype(F32)
    v = v_ref[0].astype(BF16)
    qa = q * jnp.exp(b - bm)
    kb = (k * jnp.exp(bm - b)).astype(BF16)
    qe = q * jnp.exp(b)
    kd = (k * jnp.exp(bt - b)).astype(BF16)

    intra = []
    for h in range(GLA_HEADS):
        qa_h = jnp.where(hmask[h], qa, 0.0).astype(BF16)
        att = lax.dot_general(qa_h, kb, nt_dims, preferred_element_type=F32)
        att = jnp.where(mask, att, 0.0).astype(BF16)
        intra.append(_dot(att, v[:, h * GLA_DV:(h + 1) * GLA_DV]))

    inter = [None] * nchunks
    order = range(nchunks - 1, -1, -1) if reverse else range(nchunks)
    for c in order:
        rows = slice(c * C, (c + 1) * C)
        st = s_scr[...]
        st_b = st.astype(BF16)
        dec = jnp.exp(b[c * C + tot_row:c * C + tot_row + 1])
        upd = jnp.zeros_like(st)
        outs = []
        for h in range(GLA_HEADS):
            qe_h = jnp.where(hmask[h], qe[rows], 0.0).astype(BF16)
            outs.append(lax.dot_general(qe_h, st_b, nt_dims, preferred_element_type=F32))
            u_h = lax.dot_general(v[rows, h * GLA_DV:(h + 1) * GLA_DV], kd[rows], tn_dims,
                                  preferred_element_type=F32)
            upd = upd + jnp.where(hmask[h], u_h, 0.0)
        s_scr[...] = dec * st + upd
        inter[c] = jnp.concatenate(outs, axis=1)
    o = jnp.concatenate(intra, axis=1) + jnp.concatenate(inter, axis=0)

    if finalize:
        o = o + op_ref[0]
        parts = []
        for h in range(GLA_HEADS):
            oh = o[:, h * GLA_DV:(h + 1) * GLA_DV]
            ms = jnp.mean(oh * oh, axis=-1, keepdims=True)
            parts.append(oh * lax.rsqrt(ms + EPS) * nw_ref[...])
        y = jnp.concatenate(parts, axis=1) * _silu(z_ref[0].astype(F32))
        o_ref[0] = y.astype(o_ref.dtype)
    else:
        o_ref[0] = o

    @pl.when(pl.program_id(1) == pl.num_programs(1) - 1)
    def _():
        sfin_ref[0] = s_scr[...]


def _gla_call(P, GA, wa_pad, ba, s0, reverse, fin=None):
    b_, L, _ = P.shape
    T = min(GLA_BLOCK, L)
    nt = L // T
    if reverse:
        tmap = lambda t: nt - 1 - t
    else:
        tmap = lambda t: t
    ii = np.arange(T)
    same = (ii[:, None] // GLA_CHUNK) == (ii[None, :] // GLA_CHUNK)
    tri = same & ((ii[None, :] >= ii[:, None]) if reverse else (ii[None, :] <= ii[:, None]))
    tri = jnp.asarray(tri.astype(np.float32)).astype(BF16)

    def col(name, width):
        blk = COLS[name][0] // width
        return pl.BlockSpec((1, T, width), lambda b, t: (b, tmap(t), blk))

    in_specs = [col('g_q', GLA_KW), col('g_k', GLA_KW), col('g_v', GLA_VW),
                pl.BlockSpec((1, T, LANES), lambda b, t: (b, tmap(t), 0)),
                pl.BlockSpec((LANES, GLA_KW), lambda b, t: (0, 0)),
                pl.BlockSpec((1, GLA_KW), lambda b, t: (0, 0)),
                pl.BlockSpec((1, GLA_DV, GLA_KW), lambda b, t: (b, 0, 0)),
                pl.BlockSpec((T, T), lambda b, t: (0, 0))]
    args = [P, P, P, GA, wa_pad, ba, s0, tri]
    if fin is not None:
        o_prev, nw = fin
        in_specs += [pl.BlockSpec((1, T, GLA_VW), lambda b, t: (b, tmap(t), 0)),
                     col('g_z', GLA_VW),
                     pl.BlockSpec((1, GLA_DV), lambda b, t: (0, 0))]
        args += [o_prev, P, nw]
    out_dtype = BF16 if fin is not None else F32
    return pl.pallas_call(
        functools.partial(_gla_kernel, reverse=reverse, finalize=fin is not None, nchunks=T // GLA_CHUNK),
        grid=(b_, nt),
        in_specs=in_specs,
        out_specs=[pl.BlockSpec((1, T, GLA_VW), lambda b, t: (b, tmap(t), 0)),
                   pl.BlockSpec((1, GLA_DV, GLA_KW), lambda b, t: (b, 0, 0))],
        out_shape=[jax.ShapeDtypeStruct((b_, L, GLA_VW), out_dtype),
                   jax.ShapeDtypeStruct((b_, GLA_DV, GLA_KW), F32)],
        scratch_shapes=[pltpu.VMEM((GLA_DV, GLA_KW), F32)],
        compiler_params=_cparams("arbitrary", "arbitrary"),
        name="gla",
    )(*args)


def _norm_rope(a, gain, bd, cos, sin):
    sq = a * a
    sh, sl = _split(sq)
    ss = _dot(sh, bd) + _dot(sl, bd)
    an = a * lax.rsqrt(ss * (1.0 / ATTN_HD) + EPS) * gain
    w = a.shape[-1]
    lane = lax.broadcasted_iota(jnp.int32, an.shape, 1)
    partner = jnp.where(lane % 2 == 0, pltpu.roll(an, w - 1, 1), pltpu.roll(an, 1, 1))
    return an * cos + partner * sin


Q_SCALE = (ATTN_HD ** -0.5) * math.log2(math.e)
ATTN_TK_MAX = 640


def _qkv_prep_kernel(q_ref, k_ref, v_ref, cq_ref, sq_ref, ck_ref, sk_ref, gq_ref, gk_ref,
                     bdq_ref, bdk_ref, eq_ref, ev_ref, qo_ref, ko_ref, vo_ref):
    nt = (((1,), (1,)), ((), ()))
    q = _norm_rope(q_ref[0].astype(F32), gq_ref[...], bdq_ref[...], cq_ref[...], sq_ref[...])
    qb = (q * Q_SCALE).astype(BF16)
    qo_ref[0] = lax.dot_general(eq_ref[...], qb, nt, preferred_element_type=F32).astype(BF16)
    k = _norm_rope(k_ref[0].astype(F32), gk_ref[...], bdk_ref[...], ck_ref[...], sk_ref[...]).astype(BF16)
    for g in range(ATTN_KV_HEADS):
        ko_ref[0, g] = k[:, g * ATTN_HD:(g + 1) * ATTN_HD]
    vb = v_ref[0].astype(BF16)
    vo_ref[0] = lax.dot_general(ev_ref[...], vb, nt, preferred_element_type=F32).astype(BF16)


def _qkv_prep_call(P, cos_q, sin_q, cos_k, sin_k, gq, gk, bdq, bdk, eye_q, eye_v):
    b_, L, _ = P.shape
    T = min(512, L)

    def col(name, width):
        blk = COLS[name][0] // width
        return pl.BlockSpec((1, T, width), lambda b, t: (b, t, blk))

    tab = lambda w: pl.BlockSpec((T, w), lambda b, t: (t, 0))
    const = lambda r, w: pl.BlockSpec((r, w), lambda b, t: (0, 0))
    return pl.pallas_call(
        _qkv_prep_kernel,
        grid=(b_, L // T),
        in_specs=[col('a_q', ATTN_QW), col('a_k', ATTN_KVW), col('a_v', ATTN_KVW),
                  tab(ATTN_QW), tab(ATTN_QW), tab(ATTN_KVW), tab(ATTN_KVW),
                  const(1, ATTN_QW), const(1, ATTN_KVW),
                  const(ATTN_QW, ATTN_QW), const(ATTN_KVW, ATTN_KVW),
                  const(ATTN_QW, ATTN_QW), const(ATTN_KVW, ATTN_KVW)],
        out_specs=[pl.BlockSpec((1, ATTN_QW, T), lambda b, t: (b, 0, t)),
                   pl.BlockSpec((1, ATTN_KV_HEADS, T, ATTN_HD), lambda b, t: (b, 0, t, 0)),
                   pl.BlockSpec((1, ATTN_KVW, T), lambda b, t: (b, 0, t))],
        out_shape=[jax.ShapeDtypeStruct((b_, ATTN_QW, L), BF16),
                   jax.ShapeDtypeStruct((b_, ATTN_KV_HEADS, L, ATTN_HD), BF16),
                   jax.ShapeDtypeStruct((b_, ATTN_KVW, L), BF16)],
        compiler_params=_cparams("arbitrary", "arbitrary"),
        name="qkv_prep",
    )(P, P, P, cos_q, sin_q, cos_k, sin_k, gq, gk, bdq, bdk, eye_q, eye_v)


ATTN_SUM_ROWS = 16
ATTN_UNROLL = 2
ATTN_ROW_CHUNK = 128
ATTN_TQ = 512


def _attn_kernel(qt_ref, k_ref, vt_ref, z_ref, e_ref, o_ref, m_scr, a_scr, acc_scr, s_scr, p_scr,
                 *, tk, nk):
    m_scr[...] = jnp.full(m_scr.shape, -jnp.inf, F32)
    acc_scr[...] = jnp.zeros(acc_scr.shape, F32)
    p_scr[1] = jnp.zeros(p_scr.shape[1:], BF16)
    a_scr[1] = jnp.ones(a_scr.shape[1:], F32)
    ones = jnp.ones((ATTN_SUM_ROWS, tk), BF16)

    def scores(j, slot):
        start = j * tk if isinstance(j, int) else pl.multiple_of(j * tk, tk)
        kj = k_ref[0, 0, pl.ds(start, tk), :]
        for h in range(ATTN_GROUP):
            s_scr[slot, h] = _dot(kj, qt_ref[0, h * ATTN_HD:(h + 1) * ATTN_HD, :])

    def pv(j, slot):
        start = j * tk if isinstance(j, int) else pl.multiple_of(j * tk, tk)
        vte = jnp.concatenate([vt_ref[0, :, pl.ds(start, tk)], ones], axis=0)
        for h in range(ATTN_GROUP):
            acc_scr[h] = a_scr[slot, h] * acc_scr[h] + _dot(vte, p_scr[slot, h])

    scores(0, 0)

    def step(j, slot):
        clamp = (lambda v: min(max(v, 0), nk - 1)) if isinstance(j, int) else (lambda v: jnp.clip(v, 0, nk - 1))
        scores(clamp(j + 1), 1 - slot)
        pv(clamp(j - 1), 1 - slot)
        for h in range(ATTN_GROUP):
            m_old = m_scr[h]
            m_new = m_old
            for r in range(0, tk, ATTN_ROW_CHUNK):
                m_new = jnp.maximum(m_new, jnp.max(s_scr[slot, h, r:r + ATTN_ROW_CHUNK, :], axis=0, keepdims=True))
            a_scr[slot, h] = jnp.exp2(m_old - m_new)
            m_scr[h] = m_new
            for r in range(0, tk, ATTN_ROW_CHUNK):
                p_scr[slot, h, r:r + ATTN_ROW_CHUNK, :] = jnp.exp2(
                    s_scr[slot, h, r:r + ATTN_ROW_CHUNK, :] - m_new).astype(BF16)

    def trip(i, carry):
        for u in range(ATTN_UNROLL):
            step(ATTN_UNROLL * i + u, u % 2)
        return carry

    ntrips = nk // ATTN_UNROLL
    lax.fori_loop(0, ntrips, trip, 0)
    for j in range(ntrips * ATTN_UNROLL, nk):
        step(j, j % 2)
    pv(nk - 1, (nk - 1) % 2)
    tn = (((0,), (0,)), ((), ()))
    out = None
    for h in range(ATTN_GROUP):
        acc = acc_scr[h]
        oh, ol = _split(acc[:ATTN_HD] / acc[ATTN_HD:ATTN_HD + 1])
        part = (lax.dot_general(oh, e_ref[h], tn, preferred_element_type=F32)
                + lax.dot_general(ol, e_ref[h], tn, preferred_element_type=F32))
        out = part if out is None else out + part
    o_ref[0] = (out * _silu(z_ref[0].astype(F32))).astype(o_ref.dtype)


def _attn_call(qt, k, vt, P, e_heads):
    b_, _, L = qt.shape
    Lk = k.shape[2]
    tq = min(ATTN_TQ, L)
    tk = max(t for t in range(LANES, ATTN_TK_MAX + 1, LANES) if Lk % t == 0)
    gw = ATTN_GROUP * ATTN_HD
    zblk = COLS['a_z'][0] // gw
    return pl.pallas_call(
        functools.partial(_attn_kernel, tk=tk, nk=Lk // tk),
        grid=(b_, ATTN_KV_HEADS, L // tq),
        in_specs=[pl.BlockSpec((1, gw, tq), lambda b, g, i: (b, g, i)),
                  pl.BlockSpec((1, 1, Lk, ATTN_HD), lambda b, g, i: (b, g, 0, 0)),
                  pl.BlockSpec((1, ATTN_HD, Lk), lambda b, g, i: (b, g, 0)),
                  pl.BlockSpec((1, tq, gw), lambda b, g, i: (b, i, zblk + g)),
                  pl.BlockSpec((ATTN_GROUP, ATTN_HD, gw), lambda b, g, i: (0, 0, 0))],
        out_specs=pl.BlockSpec((1, tq, gw), lambda b, g, i: (b, i, g)),
        out_shape=jax.ShapeDtypeStruct((b_, L, ATTN_QW), BF16),
        scratch_shapes=[pltpu.VMEM((ATTN_GROUP, 1, tq), F32), pltpu.VMEM((2, ATTN_GROUP, 1, tq), F32),
                        pltpu.VMEM((ATTN_GROUP, ATTN_HD + ATTN_SUM_ROWS, tq), F32),
                        pltpu.VMEM((2, ATTN_GROUP, tk, tq), F32), pltpu.VMEM((2, ATTN_GROUP, tk, tq), BF16)],
        compiler_params=_cparams("arbitrary", "arbitrary", "arbitrary"),
        name="attn",
    )(qt, k, vt, P, e_heads)


def _short_conv_kernel(u_ref, up_ref, un_ref, w_ref, b_ref, v_ref, x1_ref, x2_ref):
    t = pl.program_id(1)
    nt = pl.num_programs(1)
    u = u_ref[0].astype(F32)
    T = u.shape[0]
    hr = up_ref.shape[1]
    prev_row = jnp.where(t > 0, up_ref[0, hr - 1:hr, :].astype(F32), 0.0)
    next_row = jnp.where(t < nt - 1, un_ref[0, 0:1, :].astype(F32), 0.0)
    row = lax.broadcasted_iota(jnp.int32, u.shape, 0)
    u_m1 = jnp.where(row == 0, prev_row, pltpu.roll(u, 1, 0))
    u_p1 = jnp.where(row == T - 1, next_row, pltpu.roll(u, T - 1, 0))
    w = w_ref[...]
    out = b_ref[...] + u_m1 * w[0:1] + u * w[1:2] + u_p1 * w[2:3]
    v_ref[0] = out[:, 0:HY_W]
    x1_ref[0] = out[:, HY_W:2 * HY_W]
    x2_ref[0] = out[:, 2 * HY_W:3 * HY_W]


def _short_conv_call(P, w, bias):
    b_, L, _ = P.shape
    T = min(512, L)
    cw = (HY_ORDER + 1) * HY_W
    blk = COLS['y_u'][0] // cw
    hr = 16
    hb = T // hr
    nh = L // hr
    out_spec = pl.BlockSpec((1, T, HY_W), lambda b, t: (b, t, 0))
    shp = jax.ShapeDtypeStruct((b_, L, HY_W), F32)
    return pl.pallas_call(
        _short_conv_kernel,
        grid=(b_, L // T),
        in_specs=[pl.BlockSpec((1, T, cw), lambda b, t: (b, t, blk)),
                  pl.BlockSpec((1, hr, cw), lambda b, t: (b, jnp.maximum(t * hb - 1, 0), blk)),
                  pl.BlockSpec((1, hr, cw), lambda b, t: (b, jnp.minimum((t + 1) * hb, nh - 1), blk)),
                  pl.BlockSpec((HY_SHORT, cw), lambda b, t: (0, 0)),
                  pl.BlockSpec((1, cw), lambda b, t: (0, 0))],
        out_specs=[out_spec, out_spec, out_spec],
        out_shape=[shp, shp, shp],
        compiler_params=_cparams("arbitrary", "arbitrary"),
        name="short_conv",
    )(P, P, P, w, bias)


FFT_ROWS = 8


def _filter_kernel(emb_ref, w1_ref, b1_ref, f1_ref, w2_ref, b2_ref, f2_ref, w3_ref, dl_ref,
                   hf_ref, hb_ref, sum_ref):
    t = pl.program_id(0)
    emb = emb_ref[...]
    h = jnp.sin(f1_ref[...] * (_dot_hp(emb, w1_ref[...]) + b1_ref[...]))
    h = jnp.sin(f2_ref[...] * (_dot_hp(h, w2_ref[...]) + b2_ref[...]))
    hh, hl = _split(h)
    tt = emb[:, 0:1]

    @pl.when(t == 0)
    def _():
        sum_ref[...] = jnp.zeros_like(sum_ref)

    lag0 = (lax.broadcasted_iota(jnp.int32, (h.shape[0], HY_W), 0) + t * h.shape[0]) == 0
    for o in range(HY_ORDER):
        for d, out_ref in enumerate((hf_ref, hb_ref)):
            cols = slice((2 * o + d) * HY_W, (2 * o + d + 1) * HY_W)
            wh, wl = _split(w3_ref[:, cols])
            f = _dot(hh, wh) + _dot(hl, wh) + _dot(hh, wl)
            f = f * (jnp.exp(-tt * dl_ref[:, cols]) + HY_MOD_SHIFT)
            sum_ref[:, cols] += jnp.broadcast_to(jnp.sum(jnp.abs(f), axis=0, keepdims=True), (8, HY_W))
            if d == 1:
                f = jnp.where(lag0, 0.0, f)
            out_ref[o] = pltpu.einshape("(an)c->a(nc)", f, a=FFT_ROWS)


def _filter_call(emb, w1, b1, f1, w2, b2, f2, w3, deltas, n1, n2):
    L = emb.shape[0]
    TL = FFT_ROWS * n2
    const = lambda r, w: pl.BlockSpec((r, w), lambda t: (0, 0))
    hspec = pl.BlockSpec((HY_ORDER, FFT_ROWS, n2 * HY_W), lambda t: (0, t, 0))
    hshape = jax.ShapeDtypeStruct((HY_ORDER, n1 // 2, n2 * HY_W), F32)
    return pl.pallas_call(
        _filter_kernel,
        grid=(L // TL,),
        in_specs=[pl.BlockSpec((TL, LANES), lambda t: (t, 0)),
                  const(LANES, HY_FFN), const(1, HY_FFN), const(1, HY_FFN),
                  const(HY_FFN, HY_FFN), const(1, HY_FFN), const(1, HY_FFN),
                  const(HY_FFN, N_FILT), const(1, N_FILT)],
        out_specs=[hspec, hspec, pl.BlockSpec((8, N_FILT), lambda t: (0, 0))],
        out_shape=[hshape, hshape, jax.ShapeDtypeStruct((8, N_FILT), F32)],
        compiler_params=_cparams("arbitrary"),
        name="hy_filter",
    )(emb, w1, b1, f1, w2, b2, f2, w3, deltas)


def _fft_sizes(n):
    lg = int(round(math.log2(n)))
    assert 1 << lg == n
    n1 = 1 << ((lg + 1) // 2)
    return n1, n // n1


def _bf16_pair(a):
    a = np.asarray(a, np.float32)
    hi = jnp.asarray(a, F32).astype(BF16)
    lo = (jnp.asarray(a, F32) - hi.astype(F32)).astype(BF16)
    return hi, lo


def _fft_tables(n1, n2):
    n = n1 * n2
    h1 = n1 // 2
    k1 = np.arange(n1)[:, None]
    a = 2.0 * np.pi * ((k1 * np.arange(h1)[None, :]) % n1) / n1
    c, s = np.cos(a), np.sin(a)
    fa = np.block([[c, s], [-s, c]])
    fa_real = np.concatenate([c, -s], axis=0)
    fd = np.block([[c.T, -s.T], [s.T, c.T]])
    k2 = np.arange(n2)[:, None]
    b = 2.0 * np.pi * ((k2 * np.arange(n2)[None, :]) % n2) / n2
    cb, sb = np.cos(b), np.sin(b)
    fb = np.block([[cb, sb], [-sb, cb]])
    fc = np.block([[cb, -sb], [sb, cb]])
    kn = (jnp.arange(n1, dtype=jnp.int32)[:, None] * jnp.arange(n2, dtype=jnp.int32)[None, :]) % n
    tw = kn.astype(F32) * (2.0 * math.pi / n)
    twr = jnp.broadcast_to(jnp.cos(tw)[:, :, None], (n1, n2, LANES))
    twi = jnp.broadcast_to(-jnp.sin(tw)[:, :, None], (n1, n2, LANES))
    return dict(fa=_bf16_pair(fa), fa_real=_bf16_pair(fa_real), fd=_bf16_pair(fd),
                fb=_bf16_pair(fb), fc=_bf16_pair(fc), twr=twr, twi=twi)


def _fa_kernel(x_ref, mh_ref, ml_ref, o_ref, xh_scr, *, packed, part_axis):
    @pl.when(pl.program_id(part_axis) == 0)
    def _():
        if packed:
            x = x_ref[...]
            x = x.reshape(x.shape[0] * x.shape[1], x.shape[2], x.shape[3])
            x = pltpu.einshape("rnc->r(nc)", x)
        else:
            x = x_ref[0]
        xh_scr[...] = x.astype(BF16)

    xh = xh_scr[...]
    o_ref[0] = _dot(mh_ref[...], xh) + _dot(ml_ref[...], xh)


def _fa_call(x4, mats, n1):
    mh, ml = mats
    _, h1, n2, wd = x4.shape
    wt = FFT_ROWS * wd
    return pl.pallas_call(
        functools.partial(_fa_kernel, packed=True, part_axis=1),
        grid=(n2 // FFT_ROWS, 2),
        in_specs=[pl.BlockSpec((2, h1, FFT_ROWS, wd), lambda j, p: (0, 0, j, 0)),
                  pl.BlockSpec((n1, n1), lambda j, p: (p, 0)),
                  pl.BlockSpec((n1, n1), lambda j, p: (p, 0))],
        out_specs=pl.BlockSpec((1, n1, wt), lambda j, p: (p, 0, j)),
        out_shape=jax.ShapeDtypeStruct((2, n1, n2 * wd), F32),
        scratch_shapes=[pltpu.VMEM((n1, wt), BF16)],
        compiler_params=_cparams("arbitrary", "arbitrary"),
        name="fft_a",
    )(x4, mh, ml)


def _fa_real_call(x, mats, n1):
    mh, ml = mats
    ng, h1, lanes = x.shape
    wt = min(FFT_ROWS * HY_W, lanes)
    out = pl.pallas_call(
        functools.partial(_fa_kernel, packed=False, part_axis=2),
        grid=(ng, lanes // wt, 2),
        in_specs=[pl.BlockSpec((1, h1, wt), lambda g, j, p: (g, 0, j)),
                  pl.BlockSpec((n1, h1), lambda g, j, p: (p, 0)),
                  pl.BlockSpec((n1, h1), lambda g, j, p: (p, 0))],
        out_specs=pl.BlockSpec((1, n1, wt), lambda g, j, p: (2 * g + p, 0, j)),
        out_shape=jax.ShapeDtypeStruct((2 * ng, n1, lanes), F32),
        scratch_shapes=[pltpu.VMEM((h1, wt), BF16)],
        compiler_params=_cparams("arbitrary", "arbitrary", "arbitrary"),
        name="fft_a_real",
    )(x, mh, ml)
    return out.reshape(ng, 2, n1, lanes)


def _lane_tile(a, width):
    return jnp.concatenate([a] * (width // a.shape[-1]), axis=-1)


def _fb_filter_kernel(tf_ref, tb_ref, twr_ref, twi_ref, fh_ref, fl_ref, sf_ref, sb_ref, g_ref,
                      tf4_scr, tb4_scr, *, n2, scale):
    w = g_ref.shape[-1]
    tf4_scr[...] = pltpu.einshape("pk(nc)->pknc", tf_ref[0], n=n2)
    tb4_scr[...] = pltpu.einshape("pk(nc)->pknc", tb_ref[0], n=n2)
    inv = jnp.concatenate([scale / sf_ref[0:1, :], scale / sb_ref[0:1, :]], axis=1)

    def body(kk, carry):
        twr = _lane_tile(twr_ref[kk], 2 * w)
        twi = _lane_tile(twi_ref[kk], 2 * w)
        tr = jnp.concatenate([tf4_scr[0, kk], tb4_scr[0, kk]], axis=1)
        ti = jnp.concatenate([tf4_scr[1, kk], tb4_scr[1, kk]], axis=1)
        p = jnp.concatenate([tr * twr - ti * twi, tr * twi + ti * twr], axis=0)
        z = _dot_mat(fh_ref[...], fl_ref[...], p) * inv
        g_ref[0, 0, kk] = z[:n2, :w] + z[:n2, w:]
        g_ref[0, 1, kk] = z[n2:, :w] - z[n2:, w:]
        return carry

    lax.fori_loop(0, FFT_ROWS, body, 0, unroll=2)


def _fb_filter_call(thf, thb, sums, tabs, n1, n2):
    fh, fl = tabs['fb']
    lanes = n2 * HY_W
    tspec = pl.BlockSpec((1, 2, FFT_ROWS, lanes), lambda g, i: (g, 0, i, 0))
    twspec = pl.BlockSpec((FFT_ROWS, n2, LANES), lambda g, i: (i, 0, 0))
    sspec = lambda d: pl.BlockSpec((8, HY_W), lambda g, i: (0, 2 * g + d))
    return pl.pallas_call(
        functools.partial(_fb_filter_kernel, n2=n2, scale=1.0 / (n1 * n2)),
        grid=(HY_ORDER, n1 // FFT_ROWS),
        in_specs=[tspec, tspec, twspec, twspec,
                  pl.BlockSpec(fh.shape, lambda g, i: (0, 0)), pl.BlockSpec(fl.shape, lambda g, i: (0, 0)),
                  sspec(0), sspec(1)],
        out_specs=pl.BlockSpec((1, 2, FFT_ROWS, n2, HY_W), lambda g, i: (g, 0, i, 0, 0)),
        out_shape=jax.ShapeDtypeStruct((HY_ORDER, 2, n1, n2, HY_W), F32),
        scratch_shapes=[pltpu.VMEM((2, FFT_ROWS, n2, HY_W), F32), pltpu.VMEM((2, FFT_ROWS, n2, HY_W), F32)],
        compiler_params=_cparams("arbitrary", "arbitrary"),
        name="fft_b_filter",
    )(thf, thb, tabs['twr'], tabs['twi'], fh, fl, sums, sums)


def _fb_kernel(t_ref, twr_ref, twi_ref, fh_ref, fl_ref, ch_ref, cl_ref, g_ref, o_ref, t4_scr, *, n2):
    w = g_ref.shape[-1]
    t4_scr[...] = pltpu.einshape("pk(nc)->pknc", t_ref[...], n=n2)

    def body(kk, carry):
        tr = t4_scr[0, kk]
        ti = t4_scr[1, kk]
        twr = _lane_tile(twr_ref[kk], w)
        twi = _lane_tile(twi_ref[kk], w)
        p = jnp.concatenate([tr * twr - ti * twi, tr * twi + ti * twr], axis=0)
        z = _dot_mat(fh_ref[...], fl_ref[...], p)
        zr, zi = z[:n2], z[n2:]
        gr = g_ref[0, 0, kk]
        gi = g_ref[0, 1, kk]
        y = jnp.concatenate([zr * gr - zi * gi, zr * gi + zi * gr], axis=0)
        v = _dot_mat(ch_ref[...], cl_ref[...], y)
        vr, vi = v[:n2], v[n2:]
        t4_scr[0, kk] = vr * twr + vi * twi
        t4_scr[1, kk] = vi * twr - vr * twi
        return carry

    lax.fori_loop(0, FFT_ROWS, body, 0, unroll=2)
    o_ref[...] = pltpu.einshape("pknc->pk(nc)", t4_scr[...])


def _fb_call(t, g, order, tabs, n1, n2):
    fh, fl = tabs['fb']
    ch, cl = tabs['fc']
    lanes = n2 * HY_W
    mat = lambda m: pl.BlockSpec(m.shape, lambda i: (0, 0))
    dat = pl.BlockSpec((2, FFT_ROWS, lanes), lambda i: (0, i, 0))
    twspec = pl.BlockSpec((FFT_ROWS, n2, LANES), lambda i: (i, 0, 0))
    return pl.pallas_call(
        functools.partial(_fb_kernel, n2=n2),
        grid=(n1 // FFT_ROWS,),
        in_specs=[dat, twspec, twspec, mat(fh), mat(fl), mat(ch), mat(cl),
                  pl.BlockSpec((1, 2, FFT_ROWS, n2, HY_W), lambda i: (order, 0, i, 0, 0))],
        out_specs=dat,
        out_shape=jax.ShapeDtypeStruct((2, n1, lanes), F32),
        scratch_shapes=[pltpu.VMEM((2, FFT_ROWS, n2, HY_W), F32)],
        compiler_params=_cparams("arbitrary"),
        name="fft_b",
    )(t, tabs['twr'], tabs['twi'], fh, fl, ch, cl, g)


def _fd_kernel(u_ref, mh_ref, ml_ref, z_ref, x_ref, sk_ref, o_ref):
    u = u_ref[...]
    u = u.reshape(u.shape[0] * u.shape[1], u.shape[2])
    y = _dot_mat(mh_ref[...], ml_ref[...], u)
    y = pltpu.einshape("r(nc)->rnc", y, n=FFT_ROWS)
    o_ref[0] = x_ref[0] * (y + sk_ref[...] * z_ref[0])


def _fd_call(u, mats, z4, x4, skip_row, n1):
    mh, ml = mats
    b_, h1, n2, wd = z4.shape
    wt = FFT_ROWS * wd
    dat = pl.BlockSpec((1, h1, FFT_ROWS, wd), lambda j, p: (p, 0, j, 0))
    return pl.pallas_call(
        _fd_kernel,
        grid=(n2 // FFT_ROWS, b_),
        in_specs=[pl.BlockSpec((2, n1, wt), lambda j, p: (0, 0, j)),
                  pl.BlockSpec((h1, 2 * n1), lambda j, p: (p, 0)),
                  pl.BlockSpec((h1, 2 * n1), lambda j, p: (p, 0)),
                  dat, dat, pl.BlockSpec((1, wd), lambda j, p: (0, 0))],
        out_specs=dat,
        out_shape=jax.ShapeDtypeStruct(z4.shape, F32),
        compiler_params=pltpu.CompilerParams(dimension_semantics=("arbitrary", "arbitrary"),
                                             vmem_limit_bytes=FFT_D_VMEM_LIMIT),
        name="fft_d",
    )(u, mh, ml, z4, x4, skip_row)


def _hyena_filter_spectrum(L, tabs, n1, n2, fp):
    f32 = F32
    t = jnp.linspace(0.0, 1.0, L, dtype=f32)[:, None]
    w = 2.0 * math.pi * jnp.arange(L, dtype=f32)[:, None] / L
    fr = jnp.linspace(1e-4, HY_BANDS - 1, HY_BANDS, dtype=f32)[None]
    emb = jnp.concatenate([t, jnp.cos(fr * w), -jnp.sin(fr * w)], axis=-1)
    emb = jnp.pad(emb, ((0, 0), (0, LANES - HY_EMB)))
    deltas = jnp.abs(jnp.linspace(math.log(HY_DECAY_TARGET) / HY_DECAY_SHORT_PCT,
                                  math.log(HY_DECAY_TARGET) / HY_DECAY_LONG_PCT, N_FILT, dtype=f32))[None]
    f1_w, f1_b, f1_freq, f2_w, f2_b, f2_freq, f3_w = fp
    w1 = jnp.pad(f1_w, ((0, LANES - HY_EMB), (0, 0)))
    hf, hb, sums = _filter_call(emb, w1, f1_b[None], f1_freq[None], f2_w, f2_b[None],
                                f2_freq[None], f3_w, deltas, n1, n2)
    thf = _fa_real_call(hf, tabs['fa_real'], n1)
    thb = _fa_real_call(hb, tabs['fa_real'], n1)
    return _fb_filter_call(thf, thb, sums, tabs, n1, n2)


def _hyena_conv(z, xg, g, order, skip, tabs, n1, n2):
    b_, L, wd = z.shape
    assert b_ == 2, "the two batch rows ride as the real and imaginary parts of one transform"
    z4 = z.reshape(b_, n1 // 2, n2, wd)
    x4 = xg.reshape(b_, n1 // 2, n2, wd)
    t = _fa_call(z4, tabs['fa'], n1)
    u = _fb_call(t, g, order, tabs, n1, n2)
    out = _fd_call(u, tabs['fd'], z4, x4, skip[None, :], n1)
    return out.reshape(b_, L, wd)


def _merge_kernel(yg_ref, ya_ref, yh_ref, yz_ref, mg_ref, ma_ref, mh_ref, x_ref, gate_ref,
                  wg_ref, wa_ref, wh_ref, wo_ref, fn_ref, o_ref, *, final):
    yh = (yh_ref[0] * _silu(yz_ref[0].astype(F32))).astype(BF16)
    y = (_sigmoid(mg_ref[0].astype(F32)) * _dot(yg_ref[0], wg_ref[...])
         + _sigmoid(ma_ref[0].astype(F32)) * _dot(ya_ref[0], wa_ref[...])
         + _sigmoid(mh_ref[0].astype(F32)) * _dot(yh, wh_ref[...]))
    out = _dot(y.astype(BF16), wo_ref[...])
    xn = x_ref[0] + gate_ref[0] * out
    if final:
        ms = jnp.mean(xn * xn, axis=-1, keepdims=True)
        xn = xn * lax.rsqrt(ms + EPS) * fn_ref[...]
    o_ref[0] = xn


def _merge_call(y_gla, y_attn, y_hy, P, x, gate, wg, wa, wh, wo, fnorm, final):
    b_, L, d = x.shape
    T = min(512, L)
    row = lambda w, blk=0: pl.BlockSpec((1, T, w), lambda b, t: (b, t, blk))
    const = lambda r, w: pl.BlockSpec((r, w), lambda b, t: (0, 0))
    mblk = COLS['m'][0] // d
    return pl.pallas_call(
        functools.partial(_merge_kernel, final=final),
        grid=(b_, L // T),
        in_specs=[row(BRANCH_W), row(BRANCH_W), row(BRANCH_W), row(HY_W, COLS['y_z'][0] // HY_W),
                  row(d, mblk), row(d, mblk + 1), row(d, mblk + 2), row(d),
                  pl.BlockSpec((1, 1, d), lambda b, t: (b, 0, 0)),
                  const(BRANCH_W, d), const(BRANCH_W, d), const(BRANCH_W, d), const(d, d), const(1, d)],
        out_specs=row(d),
        out_shape=jax.ShapeDtypeStruct((b_, L, d), F32),
        compiler_params=_cparams("arbitrary", "arbitrary"),
        name="merge",
    )(y_gla, y_attn, y_hy, P, P, P, P, x, gate, wg, wa, wh, wo, fnorm)


def _pack_w_in(w):
    parts = dict(zip(('g_q', 'g_k', 'g_v', 'g_z', 'g_af', 'g_ab', 'a_q', 'a_k', 'a_v', 'a_z', 'y_u', 'y_z', 'm'),
                     jnp.split(w, [int(i) for i in np.cumsum(SPLITS)[:-1]], axis=-1)))
    parts['g_a'] = jnp.pad(jnp.concatenate([parts['g_af'], parts['g_ab']], axis=-1),
                           ((0, 0), (0, LANES - 2 * GLA_RANK)))
    order = sorted(COLS, key=lambda n: COLS[n][0])
    return jnp.concatenate([parts[n] for n in order], axis=-1).astype(BF16)


def _rope_tables(L, heads):
    t = jnp.arange(L)
    row = (t // GRID_W).astype(F32)
    colp = (t % GRID_W).astype(F32)
    half = ATTN_HD // 2
    inv = ROPE_THETA ** (-jnp.arange(0, half, 2, dtype=F32) / half)
    ang = jnp.concatenate([row[:, None] * inv, colp[:, None] * inv], axis=-1)
    cos = jnp.repeat(jnp.cos(ang), 2, axis=-1)
    sin = jnp.stack([-jnp.sin(ang), jnp.sin(ang)], axis=-1).reshape(L, ATTN_HD)
    return jnp.tile(cos, (1, heads)), jnp.tile(sin, (1, heads))


def _identity_rope(L, heads):
    return jnp.ones((L, heads * ATTN_HD), F32), jnp.zeros((L, heads * ATTN_HD), F32)


def _block_diag_ones(width):
    i = np.arange(width) // ATTN_HD
    return jnp.asarray((i[:, None] == i[None, :]).astype(np.float32)).astype(BF16)


def kernel(x, c, ctx, c_ctx, w_ada, b_ada, w_in, gla_wa_f, gla_ba_f, gla_wa_b, gla_ba_b, gla_norm,
           attn_qnorm, attn_knorm, hy_conv_w, hy_conv_b, hy_f1_w, hy_f1_b, hy_f1_freq, hy_f2_w,
           hy_f2_b, hy_f2_freq, hy_f3_w, hy_skip, w_br_gla, w_br_attn, w_br_hy, w_out, final_norm):
    b_, L, d = x.shape
    Lc = ctx.shape[1]
    depth = w_ada.shape[0]

    cos_q, sin_q = _rope_tables(L, ATTN_HEADS)
    cos_k, sin_k = cos_q[:, :ATTN_KVW], sin_q[:, :ATTN_KVW]
    cos_qc, sin_qc = _identity_rope(Lc, ATTN_HEADS)
    cos_kc, sin_kc = cos_qc[:, :ATTN_KVW], sin_qc[:, :ATTN_KVW]
    bdq, bdk = _block_diag_ones(ATTN_QW), _block_diag_ones(ATTN_KVW)
    eye_q, eye_v = jnp.eye(ATTN_QW, dtype=BF16), jnp.eye(ATTN_KVW, dtype=BF16)
    gw = ATTN_GROUP * ATTN_HD
    e_heads = jnp.stack([jnp.eye(ATTN_HD, gw, k=h * ATTN_HD, dtype=BF16) for h in range(ATTN_GROUP)])
    n1, n2 = _fft_sizes(2 * L)
    n1c, n2c = _fft_sizes(2 * Lc)
    tabs = _fft_tables(n1, n2)
    tabs_c = _fft_tables(n1c, n2c)
    zero_state = jnp.zeros((b_, GLA_DV, GLA_KW), F32)

    cmat = jnp.concatenate([c, c_ctx[None], jnp.zeros((8 - b_ - 1, d), F32)], axis=0)
    mods = _ada_call(cmat, w_ada, b_ada)

    for l in range(depth):
        need_ctx = l < depth - 1
        shift, scale, gate = [m[:b_, None, :] for m in jnp.split(mods[l], 3, axis=-1)]
        shift_c, scale_c, gate_c = [jnp.broadcast_to(m[b_:b_ + 1, None, :], (b_, 1, d))
                                    for m in jnp.split(mods[l], 3, axis=-1)]
        w_packed = _pack_w_in(w_in[l])
        P = _proj_call(x, scale, shift, w_packed, BF16, PROJ_TN)
        Pc = _proj_call(ctx, scale_c, shift_c, w_packed, BF16, PROJ_TN)
        ga0 = COLS['g_a'][0]
        w_ga = w_packed[:, ga0:ga0 + LANES]
        GA = _proj_call(x, scale, shift, w_ga, F32, LANES)
        GAc = _proj_call(ctx, scale_c, shift_c, w_ga, F32, LANES)

        rk = GLA_RANK
        wa_f = jnp.zeros((LANES, GLA_KW), F32).at[0:rk].set(gla_wa_f[l])
        wa_b = jnp.zeros((LANES, GLA_KW), F32).at[rk:2 * rk].set(gla_wa_b[l])
        ba_f, ba_b = gla_ba_f[l][None], gla_ba_b[l][None]
        nw = gla_norm[l][None]
        oc_f, sc_f = _gla_call(Pc, GAc, wa_f, ba_f, zero_state, reverse=False)
        yc_gla, sc_b = _gla_call(Pc, GAc, wa_b, ba_b, zero_state, reverse=True, fin=(oc_f, nw))
        o_f, _ = _gla_call(P, GA, wa_f, ba_f, sc_f, reverse=False)
        y_gla, _ = _gla_call(P, GA, wa_b, ba_b, sc_b, reverse=True, fin=(o_f, nw))

        gq = jnp.tile(attn_qnorm[l], ATTN_HEADS)[None]
        gk = jnp.tile(attn_knorm[l], ATTN_KV_HEADS)[None]
        q_a, k_a, v_a = _qkv_prep_call(P, cos_q, sin_q, cos_k, sin_k, gq, gk, bdq, bdk, eye_q, eye_v)
        qc_a, kc_a, vc_a = _qkv_prep_call(Pc, cos_qc, sin_qc, cos_kc, sin_kc, gq, gk, bdq, bdk,
                                          eye_q, eye_v)
        y_attn = _attn_call(q_a, jnp.concatenate([k_a, kc_a], axis=2),
                            jnp.concatenate([v_a, vc_a], axis=2), P, e_heads)

        fp = (hy_f1_w[l], hy_f1_b[l], hy_f1_freq[l], hy_f2_w[l], hy_f2_b[l], hy_f2_freq[l], hy_f3_w[l])
        g_spec = _hyena_filter_spectrum(L, tabs, n1, n2, fp)
        v0, x1, x2 = _short_conv_call(P, hy_conv_w[l], hy_conv_b[l][None])
        z1 = _hyena_conv(v0, x1, g_spec, 0, hy_skip[l, 0], tabs, n1, n2)
        y_hy = _hyena_conv(z1, x2, g_spec, 1, hy_skip[l, 1], tabs, n1, n2)

        wg, wa, wh, wo = (w_br_gla[l].astype(BF16), w_br_attn[l].astype(BF16),
                          w_br_hy[l].astype(BF16), w_out[l].astype(BF16))
        fn = final_norm[None]
        x_new = _merge_call(y_gla, y_attn, y_hy, P, x, gate, wg, wa, wh, wo, fn, final=not need_ctx)

        if need_ctx:
            yc_attn = _attn_call(qc_a, kc_a, vc_a, Pc, e_heads)
            gc_spec = _hyena_filter_spectrum(Lc, tabs_c, n1c, n2c, fp)
            vc0, xc1, xc2 = _short_conv_call(Pc, hy_conv_w[l], hy_conv_b[l][None])
            zc1 = _hyena_conv(vc0, xc1, gc_spec, 0, hy_skip[l, 0], tabs_c, n1c, n2c)
            yc_hy = _hyena_conv(zc1, xc2, gc_spec, 1, hy_skip[l, 1], tabs_c, n1c, n2c)
            ctx = _merge_call(yc_gla, yc_attn, yc_hy, Pc, ctx, gate_c, wg, wa, wh, wo, fn, final=False)
        x = x_new

    return x
```

```python
import functools
import math

import numpy as np
import jax
import jax.numpy as jnp
from jax import lax
from jax.experimental import pallas as pl
from jax.experimental.pallas import tpu as pltpu

F32 = jnp.float32
BF16 = jnp.bfloat16

D_MODEL = 1024
GRID_W = 64
BRANCH_W = D_MODEL // 2
N_BRANCH = 3
EPS = 1e-6
GLA_HEADS = 4
GLA_DV = BRANCH_W // GLA_HEADS
GLA_DK = GLA_DV // 2
GLA_KW = GLA_HEADS * GLA_DK
GLA_VW = GLA_HEADS * GLA_DV
GLA_RANK = 16
GLA_GATE_NORM = 16.0
GLA_CHUNK = 64
GLA_BLOCK = 256
ATTN_HD = 64
ATTN_HEADS = BRANCH_W // ATTN_HD
ATTN_KV_HEADS = ATTN_HEADS // 4
ATTN_GROUP = ATTN_HEADS // ATTN_KV_HEADS
ATTN_QW = ATTN_HEADS * ATTN_HD
ATTN_KVW = ATTN_KV_HEADS * ATTN_HD
ROPE_THETA = 10000.0
HY_W = BRANCH_W
HY_ORDER = 2
HY_EMB = 33
HY_BANDS = (HY_EMB - 1) // 2
HY_FFN = 64
HY_SHORT = 3
HY_MOD_SHIFT = 0.05
HY_DECAY_SHORT_PCT = 0.3
HY_DECAY_LONG_PCT = 1.5
HY_DECAY_TARGET = 1e-2
N_FILT = 2 * HY_ORDER * HY_W
SPLITS = (GLA_KW, GLA_KW, GLA_VW, GLA_VW, GLA_RANK, GLA_RANK,
          ATTN_QW, ATTN_KVW, ATTN_KVW, ATTN_QW,
          (HY_ORDER + 1) * HY_W, HY_W,
          N_BRANCH * D_MODEL)

LANES = 128
VMEM_LIMIT = 48 * 1024 * 1024
FFT_D_VMEM_LIMIT = 56 * 1024 * 1024

COLS = {
    'm': (0, 3072), 'y_u': (3072, 1536), 'g_v': (4608, 512), 'g_z': (5120, 512),
    'a_q': (5632, 512), 'a_z': (6144, 512), 'y_z': (6656, 512), 'g_q': (7168, 256),
    'g_k': (7424, 256), 'a_k': (7680, 128), 'a_v': (7808, 128), 'g_a': (7936, 128),
    'g_a_lo': (8064, 128),
}
N_PROJ = 8192
PROJ_TN = 2048


def _cparams(*sem):
    return pltpu.CompilerParams(dimension_semantics=sem, vmem_limit_bytes=VMEM_LIMIT)


def _split(a):
    hi = a.astype(BF16)
    lo = (a - hi.astype(F32)).astype(BF16)
    return hi, lo


def _dot(a, b):
    return jnp.dot(a, b, preferred_element_type=F32)


def _dot_hp(a, b):
    ah, al = _split(a)
    bh, bl = _split(b)
    return _dot(ah, bh) + _dot(al, bh) + _dot(ah, bl)


def _dot_mat(mh, ml, x):
    xh = x.astype(BF16)
    return _dot(mh, xh) + _dot(ml, xh)


def _sigmoid(x):
    return 1.0 / (1.0 + jnp.exp(-x))


def _silu(x):
    return x * _sigmoid(x)


def _ada_kernel(c_ref, w_ref, b_ref, o_ref):
    o_ref[0] = _dot_hp(_silu(c_ref[...]), w_ref[0]) + b_ref[0]


def _ada_call(cmat, w_ada, b_ada):
    depth, d, n3 = w_ada.shape
    tn = 1024
    return pl.pallas_call(
        _ada_kernel,
        grid=(depth, n3 // tn),
        in_specs=[pl.BlockSpec((8, d), lambda l, j: (0, 0)),
                  pl.BlockSpec((1, d, tn), lambda l, j: (l, 0, j)),
                  pl.BlockSpec((1, 1, tn), lambda l, j: (l, 0, j))],
        out_specs=pl.BlockSpec((1, 8, tn), lambda l, j: (l, 0, j)),
        out_shape=jax.ShapeDtypeStruct((depth, 8, n3), F32),
        compiler_params=_cparams("arbitrary", "arbitrary"),
        name="ada",
    )(cmat, w_ada, b_ada.reshape(depth, 1, n3))


def _proj_kernel(x_ref, sc_ref, sh_ref, w_ref, o_ref, *, res_tile, hi_off, lo_off):
    x = x_ref[0]
    ms = jnp.mean(x * x, axis=-1, keepdims=True)
    h = x * lax.rsqrt(ms + EPS) * (1.0 + sc_ref[0]) + sh_ref[0]
    res = _dot(h.astype(BF16), w_ref[...])
    out = res.astype(o_ref.dtype)
    o_ref[0] = out

    @pl.when(pl.program_id(0) == res_tile)
    def _():
        hi = slice(hi_off, hi_off + LANES)
        o_ref[0, :, lo_off:lo_off + LANES] = (res[:, hi] - out[:, hi].astype(F32)).astype(o_ref.dtype)


def _proj_call(x, scale, shift, w):
    b_, L, d = x.shape
    n = w.shape[1]
    tm = min(512, L)
    tn = PROJ_TN
    hi0, lo0 = COLS['g_a'][0], COLS['g_a_lo'][0]
    assert hi0 // tn == lo0 // tn
    return pl.pallas_call(
        functools.partial(_proj_kernel, res_tile=hi0 // tn, hi_off=hi0 % tn, lo_off=lo0 % tn),
        grid=(n // tn, b_, L // tm),
        in_specs=[pl.BlockSpec((1, tm, d), lambda j, b, i: (b, i, 0)),
                  pl.BlockSpec((1, 1, d), lambda j, b, i: (b, 0, 0)),
                  pl.BlockSpec((1, 1, d), lambda j, b, i: (b, 0, 0)),
                  pl.BlockSpec((d, tn), lambda j, b, i: (0, j))],
        out_specs=pl.BlockSpec((1, tm, tn), lambda j, b, i: (b, i, j)),
        out_shape=jax.ShapeDtypeStruct((b_, L, n), BF16),
        compiler_params=_cparams("arbitrary", "arbitrary", "arbitrary"),
        name="proj",
    )(x, scale, shift, w)


def _gla_kernel(*refs, reverse, finalize, nchunks):
    if finalize:
        (q_ref, k_ref, v_ref, ga_ref, gal_ref, wa_ref, ba_ref, s0_ref, tri_ref, op_ref, z_ref, nw_ref,
         o_ref, sfin_ref, s_scr) = refs
    else:
        (q_ref, k_ref, v_ref, ga_ref, gal_ref, wa_ref, ba_ref, s0_ref, tri_ref,
         o_ref, sfin_ref, s_scr) = refs
    C = GLA_CHUNK
    T = nchunks * C
    nt_dims = (((1,), (1,)), ((), ()))
    tn_dims = (((0,), (0,)), ((), ()))

    @pl.when(pl.program_id(1) == 0)
    def _():
        s_scr[...] = s0_ref[0]

    ga = ga_ref[0].astype(F32) + gal_ref[0].astype(F32)
    xg = _dot_hp(ga, wa_ref[...]) + ba_ref[...]
    g = (jnp.minimum(xg, 0.0) - jnp.log(1.0 + jnp.exp(-jnp.abs(xg)))) * (1.0 / GLA_GATE_NORM)

    tri = tri_ref[...]
    g1 = g.astype(BF16)
    r1 = g - g1.astype(F32)
    g2 = r1.astype(BF16)
    g3 = (r1 - g2.astype(F32)).astype(BF16)
    b = _dot(tri, g1) + _dot(tri, g2) + _dot(tri, g3)

    def chunk_row(r):
        return jnp.concatenate([jnp.broadcast_to(b[c * C + r:c * C + r + 1], (C, GLA_KW))
                                for c in range(nchunks)], axis=0)

    tot_row = 0 if reverse else C - 1
    bm = chunk_row(C // 2)
    bt = chunk_row(tot_row)
    ri = lax.broadcasted_iota(jnp.int32, (T, T), 0)
    ci = lax.broadcasted_iota(jnp.int32, (T, T), 1)
    same_chunk = (ri // C) == (ci // C)
    mask = same_chunk & ((ci >= ri) if reverse else (ci <= ri))
    lane = lax.broadcasted_iota(jnp.int32, (1, GLA_KW), 1)
    hmask = [(lane >= h * GLA_DK) & (lane < (h + 1) * GLA_DK) for h in range(GLA_HEADS)]

    q = q_ref[0].astype(F32) * (GLA_DK ** -0.5)
    k = k_ref[0].astype(F32)
    v = v_ref[0].astype(BF16)
    qa = q * jnp.exp(b - bm)
    kb = (k * jnp.exp(bm - b)).astype(BF16)
    qe = q * jnp.exp(b)
    kd = (k * jnp.exp(bt - b)).astype(BF16)

    intra = []
    for h in range(GLA_HEADS):
        qa_h = jnp.where(hmask[h], qa, 0.0).astype(BF16)
        att = lax.dot_general(qa_h, kb, nt_dims, preferred_element_type=F32)
        att = jnp.where(mask, att, 0.0).astype(BF16)
        intra.append(_dot(att, v[:, h * GLA_DV:(h + 1) * GLA_DV]))

    inter = [None] * nchunks
    order = range(nchunks - 1, -1, -1) if reverse else range(nchunks)
    for c in order:
        rows = slice(c * C, (c + 1) * C)
        st = s_scr[...]
        st_b = st.astype(BF16)
        dec = jnp.exp(b[c * C + tot_row:c * C + tot_row + 1])
        upd = jnp.zeros_like(st)
        outs = []
        for h in range(GLA_HEADS):
            qe_h = jnp.where(hmask[h], qe[rows], 0.0).astype(BF16)
            outs.append(lax.dot_general(qe_h, st_b, nt_dims, preferred_element_type=F32))
            u_h = lax.dot_general(v[rows, h * GLA_DV:(h + 1) * GLA_DV], kd[rows], tn_dims,
                                  preferred_element_type=F32)
            upd = upd + jnp.where(hmask[h], u_h, 0.0)
        s_scr[...] = dec * st + upd
        inter[c] = jnp.concatenate(outs, axis=1)
    o = jnp.concatenate(intra, axis=1) + jnp.concatenate(inter, axis=0)

    if finalize:
        o = o + op_ref[0]
        parts = []
        for h in range(GLA_HEADS):
            oh = o[:, h * GLA_DV:(h + 1) * GLA_DV]
            ms = jnp.mean(oh * oh, axis=-1, keepdims=True)
            parts.append(oh * lax.rsqrt(ms + EPS) * nw_ref[...])
        y = jnp.concatenate(parts, axis=1) * _silu(z_ref[0].astype(F32))
        o_ref[0] = y.astype(o_ref.dtype)
    else:
        o_ref[0] = o

    @pl.when(pl.program_id(1) == pl.num_programs(1) - 1)
    def _():
        sfin_ref[0] = s_scr[...]


def _gla_call(P, wa_pad, ba, s0, reverse, fin=None):
    b_, L, _ = P.shape
    T = min(GLA_BLOCK, L)
    nt = L // T
    if reverse:
        tmap = lambda t: nt - 1 - t
    else:
        tmap = lambda t: t
    ii = np.arange(T)
    same = (ii[:, None] // GLA_CHUNK) == (ii[None, :] // GLA_CHUNK)
    tri = same & ((ii[None, :] >= ii[:, None]) if reverse else (ii[None, :] <= ii[:, None]))
    tri = jnp.asarray(tri.astype(np.float32)).astype(BF16)

    def col(name, width):
        blk = COLS[name][0] // width
        return pl.BlockSpec((1, T, width), lambda b, t: (b, tmap(t), blk))

    in_specs = [col('g_q', GLA_KW), col('g_k', GLA_KW), col('g_v', GLA_VW),
                col('g_a', LANES), col('g_a_lo', LANES),
                pl.BlockSpec((LANES, GLA_KW), lambda b, t: (0, 0)),
                pl.BlockSpec((1, GLA_KW), lambda b, t: (0, 0)),
                pl.BlockSpec((1, GLA_DV, GLA_KW), lambda b, t: (b, 0, 0)),
                pl.BlockSpec((T, T), lambda b, t: (0, 0))]
    args = [P, P, P, P, P, wa_pad, ba, s0, tri]
    if fin is not None:
        o_prev, nw = fin
        in_specs += [pl.BlockSpec((1, T, GLA_VW), lambda b, t: (b, tmap(t), 0)),
                     col('g_z', GLA_VW),
                     pl.BlockSpec((1, GLA_DV), lambda b, t: (0, 0))]
        args += [o_prev, P, nw]
    out_dtype = BF16 if fin is not None else F32
    return pl.pallas_call(
        functools.partial(_gla_kernel, reverse=reverse, finalize=fin is not None, nchunks=T // GLA_CHUNK),
        grid=(b_, nt),
        in_specs=in_specs,
        out_specs=[pl.BlockSpec((1, T, GLA_VW), lambda b, t: (b, tmap(t), 0)),
                   pl.BlockSpec((1, GLA_DV, GLA_KW), lambda b, t: (b, 0, 0))],
        out_shape=[jax.ShapeDtypeStruct((b_, L, GLA_VW), out_dtype),
                   jax.ShapeDtypeStruct((b_, GLA_DV, GLA_KW), F32)],
        scratch_shapes=[pltpu.VMEM((GLA_DV, GLA_KW), F32)],
        compiler_params=_cparams("arbitrary", "arbitrary"),
        name="gla",
    )(*args)


def _norm_rope(a, gain, bd, cos, sin):
    sq = a * a
    sh, sl = _split(sq)
    ss = _dot(sh, bd) + _dot(sl, bd)
    an = a * lax.rsqrt(ss * (1.0 / ATTN_HD) + EPS) * gain
    w = a.shape[-1]
    lane = lax.broadcasted_iota(jnp.int32, an.shape, 1)
    partner = jnp.where(lane % 2 == 0, pltpu.roll(an, w - 1, 1), pltpu.roll(an, 1, 1))
    return an * cos + partner * sin


Q_SCALE = (ATTN_HD ** -0.5) * math.log2(math.e)
ATTN_TK_MAX = 640


def _qkv_prep_kernel(q_ref, k_ref, v_ref, cq_ref, sq_ref, ck_ref, sk_ref, gq_ref, gk_ref,
                     bdq_ref, bdk_ref, eq_ref, ev_ref, qo_ref, ko_ref, vo_ref):
    nt = (((1,), (1,)), ((), ()))
    q = _norm_rope(q_ref[0].astype(F32), gq_ref[...], bdq_ref[...], cq_ref[...], sq_ref[...])
    qb = (q * Q_SCALE).astype(BF16)
    qo_ref[0] = lax.dot_general(eq_ref[...], qb, nt, preferred_element_type=F32).astype(BF16)
    k = _norm_rope(k_ref[0].astype(F32), gk_ref[...], bdk_ref[...], ck_ref[...], sk_ref[...]).astype(BF16)
    for g in range(ATTN_KV_HEADS):
        ko_ref[0, g] = k[:, g * ATTN_HD:(g + 1) * ATTN_HD]
    vb = v_ref[0].astype(BF16)
    vo_ref[0] = lax.dot_general(ev_ref[...], vb, nt, preferred_element_type=F32).astype(BF16)


def _qkv_prep_call(P, cos_q, sin_q, cos_k, sin_k, gq, gk, bdq, bdk, eye_q, eye_v):
    b_, L, _ = P.shape
    T = min(512, L)

    def col(name, width):
        blk = COLS[name][0] // width
        return pl.BlockSpec((1, T, width), lambda b, t: (b, t, blk))

    tab = lambda w: pl.BlockSpec((T, w), lambda b, t: (t, 0))
    const = lambda r, w: pl.BlockSpec((r, w), lambda b, t: (0, 0))
    return pl.pallas_call(
        _qkv_prep_kernel,
        grid=(b_, L // T),
        in_specs=[col('a_q', ATTN_QW), col('a_k', ATTN_KVW), col('a_v', ATTN_KVW),
                  tab(ATTN_QW), tab(ATTN_QW), tab(ATTN_KVW), tab(ATTN_KVW),
                  const(1, ATTN_QW), const(1, ATTN_KVW),
                  const(ATTN_QW, ATTN_QW), const(ATTN_KVW, ATTN_KVW),
                  const(ATTN_QW, ATTN_QW), const(ATTN_KVW, ATTN_KVW)],
        out_specs=[pl.BlockSpec((1, ATTN_QW, T), lambda b, t: (b, 0, t)),
                   pl.BlockSpec((1, ATTN_KV_HEADS, T, ATTN_HD), lambda b, t: (b, 0, t, 0)),
                   pl.BlockSpec((1, ATTN_KVW, T), lambda b, t: (b, 0, t))],
        out_shape=[jax.ShapeDtypeStruct((b_, ATTN_QW, L), BF16),
                   jax.ShapeDtypeStruct((b_, ATTN_KV_HEADS, L, ATTN_HD), BF16),
                   jax.ShapeDtypeStruct((b_, ATTN_KVW, L), BF16)],
        compiler_params=_cparams("arbitrary", "arbitrary"),
        name="qkv_prep",
    )(P, P, P, cos_q, sin_q, cos_k, sin_k, gq, gk, bdq, bdk, eye_q, eye_v)


ATTN_SUM_ROWS = 16
ATTN_UNROLL = 2
ATTN_TQ = 512


def _attn_kernel(qt_ref, k_ref, vt_ref, z_ref, e_ref, o_ref, m_scr, a_scr, acc_scr, s_scr, p_scr,
                 *, tk, nk):
    m_scr[...] = jnp.full(m_scr.shape, -jnp.inf, F32)
    acc_scr[...] = jnp.zeros(acc_scr.shape, F32)
    ones = jnp.ones((ATTN_SUM_ROWS, tk), BF16)

    def scores(j, slot):
        start = j * tk if isinstance(j, int) else pl.multiple_of(j * tk, tk)
        kj = k_ref[0, 0, pl.ds(start, tk), :]
        for h in range(ATTN_GROUP):
            s_scr[slot, h] = _dot(kj, qt_ref[0, h * ATTN_HD:(h + 1) * ATTN_HD, :])

    scores(0, 0)

    def step(j, slot):
        scores(min(j + 1, nk - 1) if isinstance(j, int) else jnp.minimum(j + 1, nk - 1), 1 - slot)
        start = j * tk if isinstance(j, int) else pl.multiple_of(j * tk, tk)
        vte = jnp.concatenate([vt_ref[0, :, pl.ds(start, tk)], ones], axis=0)
        for h in range(ATTN_GROUP):
            s = s_scr[slot, h]
            m_old = m_scr[h]
            m_new = jnp.maximum(m_old, jnp.max(s, axis=0, keepdims=True))
            a_scr[h] = jnp.exp2(m_old - m_new)
            m_scr[h] = m_new
            p_scr[h] = jnp.exp2(s - m_new).astype(BF16)
        for h in range(ATTN_GROUP):
            acc_scr[h] = a_scr[h] * acc_scr[h] + _dot(vte, p_scr[h])

    def trip(i, carry):
        for u in range(ATTN_UNROLL):
            step(ATTN_UNROLL * i + u, u % 2)
        return carry

    ntrips = nk // ATTN_UNROLL
    lax.fori_loop(0, ntrips, trip, 0)
    for j in range(ntrips * ATTN_UNROLL, nk):
        step(j, j % 2)
    tn = (((0,), (0,)), ((), ()))
    out = None
    for h in range(ATTN_GROUP):
        acc = acc_scr[h]
        oh, ol = _split(acc[:ATTN_HD] / acc[ATTN_HD:ATTN_HD + 1])
        part = (lax.dot_general(oh, e_ref[h], tn, preferred_element_type=F32)
                + lax.dot_general(ol, e_ref[h], tn, preferred_element_type=F32))
        out = part if out is None else out + part
    o_ref[0] = (out * _silu(z_ref[0].astype(F32))).astype(o_ref.dtype)


def _attn_call(qt, k, vt, P, e_heads):
    b_, _, L = qt.shape
    Lk = k.shape[2]
    tq = min(ATTN_TQ, L)
    tk = max(t for t in range(LANES, ATTN_TK_MAX + 1, LANES) if Lk % t == 0)
    gw = ATTN_GROUP * ATTN_HD
    zblk = COLS['a_z'][0] // gw
    return pl.pallas_call(
        functools.partial(_attn_kernel, tk=tk, nk=Lk // tk),
        grid=(b_, ATTN_KV_HEADS, L // tq),
        in_specs=[pl.BlockSpec((1, gw, tq), lambda b, g, i: (b, g, i)),
                  pl.BlockSpec((1, 1, Lk, ATTN_HD), lambda b, g, i: (b, g, 0, 0)),
                  pl.BlockSpec((1, ATTN_HD, Lk), lambda b, g, i: (b, g, 0)),
                  pl.BlockSpec((1, tq, gw), lambda b, g, i: (b, i, zblk + g)),
                  pl.BlockSpec((ATTN_GROUP, ATTN_HD, gw), lambda b, g, i: (0, 0, 0))],
        out_specs=pl.BlockSpec((1, tq, gw), lambda b, g, i: (b, i, g)),
        out_shape=jax.ShapeDtypeStruct((b_, L, ATTN_QW), BF16),
        scratch_shapes=[pltpu.VMEM((ATTN_GROUP, 1, tq), F32), pltpu.VMEM((ATTN_GROUP, 1, tq), F32),
                        pltpu.VMEM((ATTN_GROUP, ATTN_HD + ATTN_SUM_ROWS, tq), F32),
                        pltpu.VMEM((2, ATTN_GROUP, tk, tq), F32), pltpu.VMEM((ATTN_GROUP, tk, tq), BF16)],
        compiler_params=_cparams("arbitrary", "arbitrary", "arbitrary"),
        name="attn",
    )(qt, k, vt, P, e_heads)


def _short_conv_kernel(u_ref, up_ref, un_ref, w_ref, b_ref, v_ref, x1_ref, x2_ref):
    t = pl.program_id(1)
    nt = pl.num_programs(1)
    u = u_ref[0].astype(F32)
    T = u.shape[0]
    hr = up_ref.shape[1]
    prev_row = jnp.where(t > 0, up_ref[0, hr - 1:hr, :].astype(F32), 0.0)
    next_row = jnp.where(t < nt - 1, un_ref[0, 0:1, :].astype(F32), 0.0)
    row = lax.broadcasted_iota(jnp.int32, u.shape, 0)
    u_m1 = jnp.where(row == 0, prev_row, pltpu.roll(u, 1, 0))
    u_p1 = jnp.where(row == T - 1, next_row, pltpu.roll(u, T - 1, 0))
    w = w_ref[...]
    out = b_ref[...] + u_m1 * w[0:1] + u * w[1:2] + u_p1 * w[2:3]
    v_ref[0] = out[:, 0:HY_W]
    x1_ref[0] = out[:, HY_W:2 * HY_W]
    x2_ref[0] = out[:, 2 * HY_W:3 * HY_W]


def _short_conv_call(P, w, bias):
    b_, L, _ = P.shape
    T = min(512, L)
    cw = (HY_ORDER + 1) * HY_W
    blk = COLS['y_u'][0] // cw
    hr = 16
    hb = T // hr
    nh = L // hr
    out_spec = pl.BlockSpec((1, T, HY_W), lambda b, t: (b, t, 0))
    shp = jax.ShapeDtypeStruct((b_, L, HY_W), F32)
    return pl.pallas_call(
        _short_conv_kernel,
        grid=(b_, L // T),
        in_specs=[pl.BlockSpec((1, T, cw), lambda b, t: (b, t, blk)),
                  pl.BlockSpec((1, hr, cw), lambda b, t: (b, jnp.maximum(t * hb - 1, 0), blk)),
                  pl.BlockSpec((1, hr, cw), lambda b, t: (b, jnp.minimum((t + 1) * hb, nh - 1), blk)),
                  pl.BlockSpec((HY_SHORT, cw), lambda b, t: (0, 0)),
                  pl.BlockSpec((1, cw), lambda b, t: (0, 0))],
        out_specs=[out_spec, out_spec, out_spec],
        out_shape=[shp, shp, shp],
        compiler_params=_cparams("arbitrary", "arbitrary"),
        name="short_conv",
    )(P, P, P, w, bias)


FFT_ROWS = 8


def _filter_kernel(emb_ref, w1_ref, b1_ref, f1_ref, w2_ref, b2_ref, f2_ref, w3_ref, dl_ref,
                   hf_ref, hb_ref, sum_ref):
    t = pl.program_id(0)
    emb = emb_ref[...]
    h = jnp.sin(f1_ref[...] * (_dot_hp(emb, w1_ref[...]) + b1_ref[...]))
    h = jnp.sin(f2_ref[...] * (_dot_hp(h, w2_ref[...]) + b2_ref[...]))
    hh, hl = _split(h)
    tt = emb[:, 0:1]

    @pl.when(t == 0)
    def _():
        sum_ref[...] = jnp.zeros_like(sum_ref)

    lag0 = (lax.broadcasted_iota(jnp.int32, (h.shape[0], HY_W), 0) + t * h.shape[0]) == 0
    for o in range(HY_ORDER):
        for d, out_ref in enumerate((hf_ref, hb_ref)):
            cols = slice((2 * o + d) * HY_W, (2 * o + d + 1) * HY_W)
            wh, wl = _split(w3_ref[:, cols])
            f = _dot(hh, wh) + _dot(hl, wh) + _dot(hh, wl)
            f = f * (jnp.exp(-tt * dl_ref[:, cols]) + HY_MOD_SHIFT)
            sum_ref[:, cols] += jnp.broadcast_to(jnp.sum(jnp.abs(f), axis=0, keepdims=True), (8, HY_W))
            if d == 1:
                f = jnp.where(lag0, 0.0, f)
            out_ref[o] = pltpu.einshape("(an)c->a(nc)", f, a=FFT_ROWS)


def _filter_call(emb, w1, b1, f1, w2, b2, f2, w3, deltas, n1, n2):
    L = emb.shape[0]
    TL = FFT_ROWS * n2
    const = lambda r, w: pl.BlockSpec((r, w), lambda t: (0, 0))
    hspec = pl.BlockSpec((HY_ORDER, FFT_ROWS, n2 * HY_W), lambda t: (0, t, 0))
    hshape = jax.ShapeDtypeStruct((HY_ORDER, n1 // 2, n2 * HY_W), F32)
    return pl.pallas_call(
        _filter_kernel,
        grid=(L // TL,),
        in_specs=[pl.BlockSpec((TL, LANES), lambda t: (t, 0)),
                  const(LANES, HY_FFN), const(1, HY_FFN), const(1, HY_FFN),
                  const(HY_FFN, HY_FFN), const(1, HY_FFN), const(1, HY_FFN),
                  const(HY_FFN, N_FILT), const(1, N_FILT)],
        out_specs=[hspec, hspec, pl.BlockSpec((8, N_FILT), lambda t: (0, 0))],
        out_shape=[hshape, hshape, jax.ShapeDtypeStruct((8, N_FILT), F32)],
        compiler_params=_cparams("arbitrary"),
        name="hy_filter",
    )(emb, w1, b1, f1, w2, b2, f2, w3, deltas)


def _fft_sizes(n):
    lg = int(round(math.log2(n)))
    assert 1 << lg == n
    n1 = 1 << ((lg + 1) // 2)
    return n1, n // n1


def _bf16_pair(a):
    a = np.asarray(a, np.float32)
    hi = jnp.asarray(a, F32).astype(BF16)
    lo = (jnp.asarray(a, F32) - hi.astype(F32)).astype(BF16)
    return hi, lo


def _fft_tables(n1, n2):
    n = n1 * n2
    h1 = n1 // 2
    k1 = np.arange(n1)[:, None]
    a = 2.0 * np.pi * ((k1 * np.arange(h1)[None, :]) % n1) / n1
    c, s = np.cos(a), np.sin(a)
    fa = np.block([[c, s], [-s, c]])
    fa_real = np.concatenate([c, -s], axis=0)
    fd = np.block([[c.T, -s.T], [s.T, c.T]])
    k2 = np.arange(n2)[:, None]
    b = 2.0 * np.pi * ((k2 * np.arange(n2)[None, :]) % n2) / n2
    cb, sb = np.cos(b), np.sin(b)
    fb = np.block([[cb, sb], [-sb, cb]])
    fc = np.block([[cb, -sb], [sb, cb]])
    kn = (jnp.arange(n1, dtype=jnp.int32)[:, None] * jnp.arange(n2, dtype=jnp.int32)[None, :]) % n
    tw = kn.astype(F32) * (2.0 * math.pi / n)
    twr = jnp.broadcast_to(jnp.cos(tw)[:, :, None], (n1, n2, LANES))
    twi = jnp.broadcast_to(-jnp.sin(tw)[:, :, None], (n1, n2, LANES))
    return dict(fa=_bf16_pair(fa), fa_real=_bf16_pair(fa_real), fd=_bf16_pair(fd),
                fb=_bf16_pair(fb), fc=_bf16_pair(fc), twr=twr, twi=twi)


def _fa_kernel(x_ref, mh_ref, ml_ref, o_ref, xh_scr, *, packed, part_axis):
    @pl.when(pl.program_id(part_axis) == 0)
    def _():
        if packed:
            x = x_ref[...]
            x = x.reshape(x.shape[0] * x.shape[1], x.shape[2], x.shape[3])
            x = pltpu.einshape("rnc->r(nc)", x)
        else:
            x = x_ref[0]
        xh_scr[...] = x.astype(BF16)

    xh = xh_scr[...]
    o_ref[0] = _dot(mh_ref[...], xh) + _dot(ml_ref[...], xh)


def _fa_call(x4, mats, n1):
    mh, ml = mats
    _, h1, n2, wd = x4.shape
    wt = FFT_ROWS * wd
    return pl.pallas_call(
        functools.partial(_fa_kernel, packed=True, part_axis=1),
        grid=(n2 // FFT_ROWS, 2),
        in_specs=[pl.BlockSpec((2, h1, FFT_ROWS, wd), lambda j, p: (0, 0, j, 0)),
                  pl.BlockSpec((n1, n1), lambda j, p: (p, 0)),
                  pl.BlockSpec((n1, n1), lambda j, p: (p, 0))],
        out_specs=pl.BlockSpec((1, n1, wt), lambda j, p: (p, 0, j)),
        out_shape=jax.ShapeDtypeStruct((2, n1, n2 * wd), F32),
        scratch_shapes=[pltpu.VMEM((n1, wt), BF16)],
        compiler_params=_cparams("arbitrary", "arbitrary"),
        name="fft_a",
    )(x4, mh, ml)


def _fa_real_call(x, mats, n1):
    mh, ml = mats
    ng, h1, lanes = x.shape
    wt = min(FFT_ROWS * HY_W, lanes)
    out = pl.pallas_call(
        functools.partial(_fa_kernel, packed=False, part_axis=2),
        grid=(ng, lanes // wt, 2),
        in_specs=[pl.BlockSpec((1, h1, wt), lambda g, j, p: (g, 0, j)),
                  pl.BlockSpec((n1, h1), lambda g, j, p: (p, 0)),
                  pl.BlockSpec((n1, h1), lambda g, j, p: (p, 0))],
        out_specs=pl.BlockSpec((1, n1, wt), lambda g, j, p: (2 * g + p, 0, j)),
        out_shape=jax.ShapeDtypeStruct((2 * ng, n1, lanes), F32),
        scratch_shapes=[pltpu.VMEM((h1, wt), BF16)],
        compiler_params=_cparams("arbitrary", "arbitrary", "arbitrary"),
        name="fft_a_real",
    )(x, mh, ml)
    return out.reshape(ng, 2, n1, lanes)


def _lane_tile(a, width):
    return jnp.concatenate([a] * (width // a.shape[-1]), axis=-1)


def _fb_filter_kernel(tf_ref, tb_ref, twr_ref, twi_ref, fh_ref, fl_ref, sf_ref, sb_ref, g_ref,
                      tf4_scr, tb4_scr, *, n2, scale):
    w = g_ref.shape[-1]
    tf4_scr[...] = pltpu.einshape("pk(nc)->pknc", tf_ref[0], n=n2)
    tb4_scr[...] = pltpu.einshape("pk(nc)->pknc", tb_ref[0], n=n2)
    inv = jnp.concatenate([scale / sf_ref[0:1, :], scale / sb_ref[0:1, :]], axis=1)

    def body(kk, carry):
        twr = _lane_tile(twr_ref[kk], 2 * w)
        twi = _lane_tile(twi_ref[kk], 2 * w)
        tr = jnp.concatenate([tf4_scr[0, kk], tb4_scr[0, kk]], axis=1)
        ti = jnp.concatenate([tf4_scr[1, kk], tb4_scr[1, kk]], axis=1)
        p = jnp.concatenate([tr * twr - ti * twi, tr * twi + ti * twr], axis=0)
        z = _dot_mat(fh_ref[...], fl_ref[...], p) * inv
        g_ref[0, 0, kk] = z[:n2, :w] + z[:n2, w:]
        g_ref[0, 1, kk] = z[n2:, :w] - z[n2:, w:]
        return carry

    lax.fori_loop(0, FFT_ROWS, body, 0, unroll=2)


def _fb_filter_call(thf, thb, sums, tabs, n1, n2):
    fh, fl = tabs['fb']
    lanes = n2 * HY_W
    tspec = pl.BlockSpec((1, 2, FFT_ROWS, lanes), lambda g, i: (g, 0, i, 0))
    twspec = pl.BlockSpec((FFT_ROWS, n2, LANES), lambda g, i: (i, 0, 0))
    sspec = lambda d: pl.BlockSpec((8, HY_W), lambda g, i: (0, 2 * g + d))
    return pl.pallas_call(
        functools.partial(_fb_filter_kernel, n2=n2, scale=1.0 / (n1 * n2)),
        grid=(HY_ORDER, n1 // FFT_ROWS),
        in_specs=[tspec, tspec, twspec, twspec,
                  pl.BlockSpec(fh.shape, lambda g, i: (0, 0)), pl.BlockSpec(fl.shape, lambda g, i: (0, 0)),
                  sspec(0), sspec(1)],
        out_specs=pl.BlockSpec((1, 2, FFT_ROWS, n2, HY_W), lambda g, i: (g, 0, i, 0, 0)),
        out_shape=jax.ShapeDtypeStruct((HY_ORDER, 2, n1, n2, HY_W), F32),
        scratch_shapes=[pltpu.VMEM((2, FFT_ROWS, n2, HY_W), F32), pltpu.VMEM((2, FFT_ROWS, n2, HY_W), F32)],
        compiler_params=_cparams("arbitrary", "arbitrary"),
        name="fft_b_filter",
    )(thf, thb, tabs['twr'], tabs['twi'], fh, fl, sums, sums)


def _fb_kernel(t_ref, twr_ref, twi_ref, fh_ref, fl_ref, ch_ref, cl_ref, g_ref, o_ref, t4_scr, *, n2):
    w = g_ref.shape[-1]
    t4_scr[...] = pltpu.einshape("pk(nc)->pknc", t_ref[...], n=n2)

    def body(kk, carry):
        tr = t4_scr[0, kk]
        ti = t4_scr[1, kk]
        twr = _lane_tile(twr_ref[kk], w)
        twi = _lane_tile(twi_ref[kk], w)
        p = jnp.concatenate([tr * twr - ti * twi, tr * twi + ti * twr], axis=0)
        z = _dot_mat(fh_ref[...], fl_ref[...], p)
        zr, zi = z[:n2], z[n2:]
        gr = g_ref[0, 0, kk]
        gi = g_ref[0, 1, kk]
        y = jnp.concatenate([zr * gr - zi * gi, zr * gi + zi * gr], axis=0)
        v = _dot_mat(ch_ref[...], cl_ref[...], y)
        vr, vi = v[:n2], v[n2:]
        t4_scr[0, kk] = vr * twr + vi * twi
        t4_scr[1, kk] = vi * twr - vr * twi
        return carry

    lax.fori_loop(0, FFT_ROWS, body, 0, unroll=2)
    o_ref[...] = pltpu.einshape("pknc->pk(nc)", t4_scr[...])


def _fb_call(t, g, order, tabs, n1, n2):
    fh, fl = tabs['fb']
    ch, cl = tabs['fc']
    lanes = n2 * HY_W
    mat = lambda m: pl.BlockSpec(m.shape, lambda i: (0, 0))
    dat = pl.BlockSpec((2, FFT_ROWS, lanes), lambda i: (0, i, 0))
    twspec = pl.BlockSpec((FFT_ROWS, n2, LANES), lambda i: (i, 0, 0))
    return pl.pallas_call(
        functools.partial(_fb_kernel, n2=n2),
        grid=(n1 // FFT_ROWS,),
        in_specs=[dat, twspec, twspec, mat(fh), mat(fl), mat(ch), mat(cl),
                  pl.BlockSpec((1, 2, FFT_ROWS, n2, HY_W), lambda i: (order, 0, i, 0, 0))],
        out_specs=dat,
        out_shape=jax.ShapeDtypeStruct((2, n1, lanes), F32),
        scratch_shapes=[pltpu.VMEM((2, FFT_ROWS, n2, HY_W), F32)],
        compiler_params=_cparams("arbitrary"),
        name="fft_b",
    )(t, tabs['twr'], tabs['twi'], fh, fl, ch, cl, g)


def _fd_kernel(u_ref, mh_ref, ml_ref, z_ref, x_ref, sk_ref, o_ref):
    u = u_ref[...]
    u = u.reshape(u.shape[0] * u.shape[1], u.shape[2])
    y = _dot_mat(mh_ref[...], ml_ref[...], u)
    y = pltpu.einshape("r(nc)->rnc", y, n=FFT_ROWS)
    o_ref[0] = x_ref[0] * (y + sk_ref[...] * z_ref[0])


def _fd_call(u, mats, z4, x4, skip_row, n1):
    mh, ml = mats
    b_, h1, n2, wd = z4.shape
    wt = FFT_ROWS * wd
    dat = pl.BlockSpec((1, h1, FFT_ROWS, wd), lambda j, p: (p, 0, j, 0))
    return pl.pallas_call(
        _fd_kernel,
        grid=(n2 // FFT_ROWS, b_),
        in_specs=[pl.BlockSpec((2, n1, wt), lambda j, p: (0, 0, j)),
                  pl.BlockSpec((h1, 2 * n1), lambda j, p: (p, 0)),
                  pl.BlockSpec((h1, 2 * n1), lambda j, p: (p, 0)),
                  dat, dat, pl.BlockSpec((1, wd), lambda j, p: (0, 0))],
        out_specs=dat,
        out_shape=jax.ShapeDtypeStruct(z4.shape, F32),
        compiler_params=pltpu.CompilerParams(dimension_semantics=("arbitrary", "arbitrary"),
                                             vmem_limit_bytes=FFT_D_VMEM_LIMIT),
        name="fft_d",
    )(u, mh, ml, z4, x4, skip_row)


def _hyena_filter_spectrum(L, tabs, n1, n2, fp):
    f32 = F32
    t = jnp.linspace(0.0, 1.0, L, dtype=f32)[:, None]
    w = 2.0 * math.pi * jnp.arange(L, dtype=f32)[:, None] / L
    fr = jnp.linspace(1e-4, HY_BANDS - 1, HY_BANDS, dtype=f32)[None]
    emb = jnp.concatenate([t, jnp.cos(fr * w), -jnp.sin(fr * w)], axis=-1)
    emb = jnp.pad(emb, ((0, 0), (0, LANES - HY_EMB)))
    deltas = jnp.abs(jnp.linspace(math.log(HY_DECAY_TARGET) / HY_DECAY_SHORT_PCT,
                                  math.log(HY_DECAY_TARGET) / HY_DECAY_LONG_PCT, N_FILT, dtype=f32))[None]
    f1_w, f1_b, f1_freq, f2_w, f2_b, f2_freq, f3_w = fp
    w1 = jnp.pad(f1_w, ((0, LANES - HY_EMB), (0, 0)))
    hf, hb, sums = _filter_call(emb, w1, f1_b[None], f1_freq[None], f2_w, f2_b[None],
                                f2_freq[None], f3_w, deltas, n1, n2)
    thf = _fa_real_call(hf, tabs['fa_real'], n1)
    thb = _fa_real_call(hb, tabs['fa_real'], n1)
    return _fb_filter_call(thf, thb, sums, tabs, n1, n2)


def _hyena_conv(z, xg, g, order, skip, tabs, n1, n2):
    b_, L, wd = z.shape
    assert b_ == 2, "the two batch rows ride as the real and imaginary parts of one transform"
    z4 = z.reshape(b_, n1 // 2, n2, wd)
    x4 = xg.reshape(b_, n1 // 2, n2, wd)
    t = _fa_call(z4, tabs['fa'], n1)
    u = _fb_call(t, g, order, tabs, n1, n2)
    out = _fd_call(u, tabs['fd'], z4, x4, skip[None, :], n1)
    return out.reshape(b_, L, wd)


def _merge_kernel(yg_ref, ya_ref, yh_ref, yz_ref, mg_ref, ma_ref, mh_ref, x_ref, gate_ref,
                  wg_ref, wa_ref, wh_ref, wo_ref, fn_ref, o_ref, *, final):
    yh = (yh_ref[0] * _silu(yz_ref[0].astype(F32))).astype(BF16)
    y = (_sigmoid(mg_ref[0].astype(F32)) * _dot(yg_ref[0], wg_ref[...])
         + _sigmoid(ma_ref[0].astype(F32)) * _dot(ya_ref[0], wa_ref[...])
         + _sigmoid(mh_ref[0].astype(F32)) * _dot(yh, wh_ref[...]))
    out = _dot(y.astype(BF16), wo_ref[...])
    xn = x_ref[0] + gate_ref[0] * out
    if final:
        ms = jnp.mean(xn * xn, axis=-1, keepdims=True)
        xn = xn * lax.rsqrt(ms + EPS) * fn_ref[...]
    o_ref[0] = xn


def _merge_call(y_gla, y_attn, y_hy, P, x, gate, wg, wa, wh, wo, fnorm, final):
    b_, L, d = x.shape
    T = min(512, L)
    row = lambda w, blk=0: pl.BlockSpec((1, T, w), lambda b, t: (b, t, blk))
    const = lambda r, w: pl.BlockSpec((r, w), lambda b, t: (0, 0))
    mblk = COLS['m'][0] // d
    return pl.pallas_call(
        functools.partial(_merge_kernel, final=final),
        grid=(b_, L // T),
        in_specs=[row(BRANCH_W), row(BRANCH_W), row(BRANCH_W), row(HY_W, COLS['y_z'][0] // HY_W),
                  row(d, mblk), row(d, mblk + 1), row(d, mblk + 2), row(d),
                  pl.BlockSpec((1, 1, d), lambda b, t: (b, 0, 0)),
                  const(BRANCH_W, d), const(BRANCH_W, d), const(BRANCH_W, d), const(d, d), const(1, d)],
        out_specs=row(d),
        out_shape=jax.ShapeDtypeStruct((b_, L, d), F32),
        compiler_params=_cparams("arbitrary", "arbitrary"),
        name="merge",
    )(y_gla, y_attn, y_hy, P, P, P, P, x, gate, wg, wa, wh, wo, fnorm)


def _pack_w_in(w):
    parts = dict(zip(('g_q', 'g_k', 'g_v', 'g_z', 'g_af', 'g_ab', 'a_q', 'a_k', 'a_v', 'a_z', 'y_u', 'y_z', 'm'),
                     jnp.split(w, [int(i) for i in np.cumsum(SPLITS)[:-1]], axis=-1)))
    parts['g_a'] = jnp.pad(jnp.concatenate([parts['g_af'], parts['g_ab']], axis=-1),
                           ((0, 0), (0, LANES - 2 * GLA_RANK)))
    parts['g_a_lo'] = jnp.zeros_like(parts['g_a'])
    order = sorted(COLS, key=lambda n: COLS[n][0])
    return jnp.concatenate([parts[n] for n in order], axis=-1).astype(BF16)


def _rope_tables(L, heads):
    t = jnp.arange(L)
    row = (t // GRID_W).astype(F32)
    colp = (t % GRID_W).astype(F32)
    half = ATTN_HD // 2
    inv = ROPE_THETA ** (-jnp.arange(0, half, 2, dtype=F32) / half)
    ang = jnp.concatenate([row[:, None] * inv, colp[:, None] * inv], axis=-1)
    cos = jnp.repeat(jnp.cos(ang), 2, axis=-1)
    sin = jnp.stack([-jnp.sin(ang), jnp.sin(ang)], axis=-1).reshape(L, ATTN_HD)
    return jnp.tile(cos, (1, heads)), jnp.tile(sin, (1, heads))


def _identity_rope(L, heads):
    return jnp.ones((L, heads * ATTN_HD), F32), jnp.zeros((L, heads * ATTN_HD), F32)


def _block_diag_ones(width):
    i = np.arange(width) // ATTN_HD
    return jnp.asarray((i[:, None] == i[None, :]).astype(np.float32)).astype(BF16)


def kernel(x, c, ctx, c_ctx, w_ada, b_ada, w_in, gla_wa_f, gla_ba_f, gla_wa_b, gla_ba_b, gla_norm,
           attn_qnorm, attn_knorm, hy_conv_w, hy_conv_b, hy_f1_w, hy_f1_b, hy_f1_freq, hy_f2_w,
           hy_f2_b, hy_f2_freq, hy_f3_w, hy_skip, w_br_gla, w_br_attn, w_br_hy, w_out, final_norm):
    b_, L, d = x.shape
    Lc = ctx.shape[1]
    depth = w_ada.shape[0]

    cos_q, sin_q = _rope_tables(L, ATTN_HEADS)
    cos_k, sin_k = cos_q[:, :ATTN_KVW], sin_q[:, :ATTN_KVW]
    cos_qc, sin_qc = _identity_rope(Lc, ATTN_HEADS)
    cos_kc, sin_kc = cos_qc[:, :ATTN_KVW], sin_qc[:, :ATTN_KVW]
    bdq, bdk = _block_diag_ones(ATTN_QW), _block_diag_ones(ATTN_KVW)
    eye_q, eye_v = jnp.eye(ATTN_QW, dtype=BF16), jnp.eye(ATTN_KVW, dtype=BF16)
    gw = ATTN_GROUP * ATTN_HD
    e_heads = jnp.stack([jnp.eye(ATTN_HD, gw, k=h * ATTN_HD, dtype=BF16) for h in range(ATTN_GROUP)])
    n1, n2 = _fft_sizes(2 * L)
    n1c, n2c = _fft_sizes(2 * Lc)
    tabs = _fft_tables(n1, n2)
    tabs_c = _fft_tables(n1c, n2c)
    zero_state = jnp.zeros((b_, GLA_DV, GLA_KW), F32)

    cmat = jnp.concatenate([c, c_ctx[None], jnp.zeros((8 - b_ - 1, d), F32)], axis=0)
    mods = _ada_call(cmat, w_ada, b_ada)

    for l in range(depth):
        need_ctx = l < depth - 1
        shift, scale, gate = [m[:b_, None, :] for m in jnp.split(mods[l], 3, axis=-1)]
        shift_c, scale_c, gate_c = [jnp.broadcast_to(m[b_:b_ + 1, None, :], (b_, 1, d))
                                    for m in jnp.split(mods[l], 3, axis=-1)]
        w_packed = _pack_w_in(w_in[l])
        P = _proj_call(x, scale, shift, w_packed)
        Pc = _proj_call(ctx, scale_c, shift_c, w_packed)

        rk = GLA_RANK
        wa_f = jnp.zeros((LANES, GLA_KW), F32).at[0:rk].set(gla_wa_f[l])
        wa_b = jnp.zeros((LANES, GLA_KW), F32).at[rk:2 * rk].set(gla_wa_b[l])
        ba_f, ba_b = gla_ba_f[l][None], gla_ba_b[l][None]
        nw = gla_norm[l][None]
        oc_f, sc_f = _gla_call(Pc, wa_f, ba_f, zero_state, reverse=False)
        yc_gla, sc_b = _gla_call(Pc, wa_b, ba_b, zero_state, reverse=True, fin=(oc_f, nw))
        o_f, _ = _gla_call(P, wa_f, ba_f, sc_f, reverse=False)
        y_gla, _ = _gla_call(P, wa_b, ba_b, sc_b, reverse=True, fin=(o_f, nw))

        gq = jnp.tile(attn_qnorm[l], ATTN_HEADS)[None]
        gk = jnp.tile(attn_knorm[l], ATTN_KV_HEADS)[None]
        q_a, k_a, v_a = _qkv_prep_call(P, cos_q, sin_q, cos_k, sin_k, gq, gk, bdq, bdk, eye_q, eye_v)
        qc_a, kc_a, vc_a = _qkv_prep_call(Pc, cos_qc, sin_qc, cos_kc, sin_kc, gq, gk, bdq, bdk,
                                          eye_q, eye_v)
        y_attn = _attn_call(q_a, jnp.concatenate([k_a, kc_a], axis=2),
                            jnp.concatenate([v_a, vc_a], axis=2), P, e_heads)

        fp = (hy_f1_w[l], hy_f1_b[l], hy_f1_freq[l], hy_f2_w[l], hy_f2_b[l], hy_f2_freq[l], hy_f3_w[l])
        g_spec = _hyena_filter_spectrum(L, tabs, n1, n2, fp)
        v0, x1, x2 = _short_conv_call(P, hy_conv_w[l], hy_conv_b[l][None])
        z1 = _hyena_conv(v0, x1, g_spec, 0, hy_skip[l, 0], tabs, n1, n2)
        y_hy = _hyena_conv(z1, x2, g_spec, 1, hy_skip[l, 1], tabs, n1, n2)

        wg, wa, wh, wo = (w_br_gla[l].astype(BF16), w_br_attn[l].astype(BF16),
                          w_br_hy[l].astype(BF16), w_out[l].astype(BF16))
        fn = final_norm[None]
        x_new = _merge_call(y_gla, y_attn, y_hy, P, x, gate, wg, wa, wh, wo, fn, final=not need_ctx)

        if need_ctx:
            yc_attn = _attn_call(qc_a, kc_a, vc_a, Pc, e_heads)
            gc_spec = _hyena_filter_spectrum(Lc, tabs_c, n1c, n2c, fp)
            vc0, xc1, xc2 = _short_conv_call(Pc, hy_conv_w[l], hy_conv_b[l][None])
            zc1 = _hyena_conv(vc0, xc1, gc_spec, 0, hy_skip[l, 0], tabs_c, n1c, n2c)
            yc_hy = _hyena_conv(zc1, xc2, gc_spec, 1, hy_skip[l, 1], tabs_c, n1c, n2c)
            ctx = _merge_call(yc_gla, yc_attn, yc_hy, Pc, ctx, gate_c, wg, wa, wh, wo, fn, final=False)
        x = x_new

    return x
```

```python
import functools
import math

import numpy as np
import jax
import jax.numpy as jnp
from jax import lax
from jax.experimental import pallas as pl
from jax.experimental.pallas import tpu as pltpu

F32 = jnp.float32
BF16 = jnp.bfloat16
F8 = jnp.float8_e4m3fn

D_MODEL = 1024
GRID_W = 64
BRANCH_W = D_MODEL // 2
N_BRANCH = 3
EPS = 1e-6
GLA_HEADS = 4
GLA_DV = BRANCH_W // GLA_HEADS
GLA_DK = GLA_DV // 2
GLA_KW = GLA_HEADS * GLA_DK
GLA_VW = GLA_HEADS * GLA_DV
GLA_RANK = 16
GLA_GATE_NORM = 16.0
GLA_CHUNK = 64
GLA_BLOCK = 256
ATTN_HD = 64
ATTN_HEADS = BRANCH_W // ATTN_HD
ATTN_KV_HEADS = ATTN_HEADS // 4
ATTN_GROUP = ATTN_HEADS // ATTN_KV_HEADS
ATTN_QW = ATTN_HEADS * ATTN_HD
ATTN_KVW = ATTN_KV_HEADS * ATTN_HD
ROPE_THETA = 10000.0
HY_W = BRANCH_W
HY_ORDER = 2
HY_EMB = 33
HY_BANDS = (HY_EMB - 1) // 2
HY_FFN = 64
HY_SHORT = 3
HY_MOD_SHIFT = 0.05
HY_DECAY_SHORT_PCT = 0.3
HY_DECAY_LONG_PCT = 1.5
HY_DECAY_TARGET = 1e-2
N_FILT = 2 * HY_ORDER * HY_W
SPLITS = (GLA_KW, GLA_KW, GLA_VW, GLA_VW, GLA_RANK, GLA_RANK,
          ATTN_QW, ATTN_KVW, ATTN_KVW, ATTN_QW,
          (HY_ORDER + 1) * HY_W, HY_W,
          N_BRANCH * D_MODEL)

LANES = 128
VMEM_LIMIT = 48 * 1024 * 1024
FFT_D_VMEM_LIMIT = 56 * 1024 * 1024

COLS = {
    'm': (0, 3072), 'y_u': (3072, 1536), 'g_v': (4608, 512), 'g_z': (5120, 512),
    'a_q': (5632, 512), 'a_z': (6144, 512), 'y_z': (6656, 512), 'g_q': (7168, 256),
    'g_k': (7424, 256), 'a_k': (7680, 128), 'a_v': (7808, 128), 'g_a': (7936, 128),
    'g_a_lo': (8064, 128),
}
N_PROJ = 8192
PROJ_TN = 2048


def _cparams(*sem):
    return pltpu.CompilerParams(dimension_semantics=sem, vmem_limit_bytes=VMEM_LIMIT)


def _split(a):
    hi = a.astype(BF16)
    lo = (a - hi.astype(F32)).astype(BF16)
    return hi, lo


def _dot(a, b):
    return jnp.dot(a, b, preferred_element_type=F32)


def _dot_hp(a, b):
    ah, al = _split(a)
    bh, bl = _split(b)
    return _dot(ah, bh) + _dot(al, bh) + _dot(ah, bl)


def _dot_mat(mh, ml, x):
    xh = x.astype(BF16)
    return _dot(mh, xh) + _dot(ml, xh)


def _sigmoid(x):
    return 1.0 / (1.0 + jnp.exp(-x))


def _silu(x):
    return x * _sigmoid(x)


def _ada_kernel(c_ref, w_ref, b_ref, o_ref):
    o_ref[0] = _dot_hp(_silu(c_ref[...]), w_ref[0]) + b_ref[0]


def _ada_call(cmat, w_ada, b_ada):
    depth, d, n3 = w_ada.shape
    tn = 1024
    return pl.pallas_call(
        _ada_kernel,
        grid=(depth, n3 // tn),
        in_specs=[pl.BlockSpec((8, d), lambda l, j: (0, 0)),
                  pl.BlockSpec((1, d, tn), lambda l, j: (l, 0, j)),
                  pl.BlockSpec((1, 1, tn), lambda l, j: (l, 0, j))],
        out_specs=pl.BlockSpec((1, 8, tn), lambda l, j: (l, 0, j)),
        out_shape=jax.ShapeDtypeStruct((depth, 8, n3), F32),
        compiler_params=_cparams("arbitrary", "arbitrary"),
        name="ada",
    )(cmat, w_ada, b_ada.reshape(depth, 1, n3))


def _proj_kernel(x_ref, sc_ref, sh_ref, w_ref, o_ref, *, res_tile, hi_off, lo_off):
    x = x_ref[0]
    ms = jnp.mean(x * x, axis=-1, keepdims=True)
    h = x * lax.rsqrt(ms + EPS) * (1.0 + sc_ref[0]) + sh_ref[0]
    res = _dot(h.astype(BF16), w_ref[...])
    out = res.astype(o_ref.dtype)
    o_ref[0] = out

    @pl.when(pl.program_id(0) == res_tile)
    def _():
        hi = slice(hi_off, hi_off + LANES)
        o_ref[0, :, lo_off:lo_off + LANES] = (res[:, hi] - out[:, hi].astype(F32)).astype(o_ref.dtype)


def _proj_call(x, scale, shift, w):
    b_, L, d = x.shape
    n = w.shape[1]
    tm = min(512, L)
    tn = PROJ_TN
    hi0, lo0 = COLS['g_a'][0], COLS['g_a_lo'][0]
    assert hi0 // tn == lo0 // tn
    return pl.pallas_call(
        functools.partial(_proj_kernel, res_tile=hi0 // tn, hi_off=hi0 % tn, lo_off=lo0 % tn),
        grid=(n // tn, b_, L // tm),
        in_specs=[pl.BlockSpec((1, tm, d), lambda j, b, i: (b, i, 0)),
                  pl.BlockSpec((1, 1, d), lambda j, b, i: (b, 0, 0)),
                  pl.BlockSpec((1, 1, d), lambda j, b, i: (b, 0, 0)),
                  pl.BlockSpec((d, tn), lambda j, b, i: (0, j))],
        out_specs=pl.BlockSpec((1, tm, tn), lambda j, b, i: (b, i, j)),
        out_shape=jax.ShapeDtypeStruct((b_, L, n), BF16),
        compiler_params=_cparams("arbitrary", "arbitrary", "arbitrary"),
        name="proj",
    )(x, scale, shift, w)


def _gla_kernel(*refs, reverse, finalize, nchunks):
    if finalize:
        (q_ref, k_ref, v_ref, ga_ref, gal_ref, wa_ref, ba_ref, s0_ref, tri_ref, op_ref, z_ref, nw_ref,
         o_ref, sfin_ref, s_scr) = refs
    else:
        (q_ref, k_ref, v_ref, ga_ref, gal_ref, wa_ref, ba_ref, s0_ref, tri_ref,
         o_ref, sfin_ref, s_scr) = refs
    C = GLA_CHUNK
    T = nchunks * C
    nt_dims = (((1,), (1,)), ((), ()))
    tn_dims = (((0,), (0,)), ((), ()))

    @pl.when(pl.program_id(1) == 0)
    def _():
        s_scr[...] = s0_ref[0]

    ga = ga_ref[0].astype(F32) + gal_ref[0].astype(F32)
    xg = _dot_hp(ga, wa_ref[...]) + ba_ref[...]
    g = (jnp.minimum(xg, 0.0) - jnp.log(1.0 + jnp.exp(-jnp.abs(xg)))) * (1.0 / GLA_GATE_NORM)

    tri = tri_ref[...]
    g1 = g.astype(BF16)
    r1 = g - g1.astype(F32)
    g2 = r1.astype(BF16)
    g3 = (r1 - g2.astype(F32)).astype(BF16)
    b = _dot(tri, g1) + _dot(tri, g2) + _dot(tri, g3)

    def chunk_row(r):
        return jnp.concatenate([jnp.broadcast_to(b[c * C + r:c * C + r + 1], (C, GLA_KW))
                                for c in range(nchunks)], axis=0)

    tot_row = 0 if reverse else C - 1
    bm = chunk_row(C // 2)
    bt = chunk_row(tot_row)
    ri = lax.broadcasted_iota(jnp.int32, (T, T), 0)
    ci = lax.broadcasted_iota(jnp.int32, (T, T), 1)
    same_chunk = (ri // C) == (ci // C)
    mask = same_chunk & ((ci >= ri) if reverse else (ci <= ri))
    lane = lax.broadcasted_iota(jnp.int32, (1, GLA_KW), 1)
    hmask = [(lane >= h * GLA_DK) & (lane < (h + 1) * GLA_DK) for h in range(GLA_HEADS)]

    q = q_ref[0].astype(F32) * (GLA_DK ** -0.5)
    k = k_ref[0].astype(F32)
    v = v_ref[0].astype(BF16)
    qa = q * jnp.exp(b - bm)
    kb = (k * jnp.exp(bm - b)).astype(BF16)
    qe = q * jnp.exp(b)
    kd = (k * jnp.exp(bt - b)).astype(BF16)

    intra = []
    for h in range(GLA_HEADS):
        qa_h = jnp.where(hmask[h], qa, 0.0).astype(BF16)
        att = lax.dot_general(qa_h, kb, nt_dims, preferred_element_type=F32)
        att = jnp.where(mask, att, 0.0).astype(BF16)
        intra.append(_dot(att, v[:, h * GLA_DV:(h + 1) * GLA_DV]))

    inter = [None] * nchunks
    order = range(nchunks - 1, -1, -1) if reverse else range(nchunks)
    for c in order:
        rows = slice(c * C, (c + 1) * C)
        st = s_scr[...]
        st_b = st.astype(BF16)
        dec = jnp.exp(b[c * C + tot_row:c * C + tot_row + 1])
        upd = jnp.zeros_like(st)
        outs = []
        for h in range(GLA_HEADS):
            qe_h = jnp.where(hmask[h], qe[rows], 0.0).astype(BF16)
            outs.append(lax.dot_general(qe_h, st_b, nt_dims, preferred_element_type=F32))
            u_h = lax.dot_general(v[rows, h * GLA_DV:(h + 1) * GLA_DV], kd[rows], tn_dims,
                                  preferred_element_type=F32)
            upd = upd + jnp.where(hmask[h], u_h, 0.0)
        s_scr[...] = dec * st + upd
        inter[c] = jnp.concatenate(outs, axis=1)
    o = jnp.concatenate(intra, axis=1) + jnp.concatenate(inter, axis=0)

    if finalize:
        o = o + op_ref[0]
        parts = []
        for h in range(GLA_HEADS):
            oh = o[:, h * GLA_DV:(h + 1) * GLA_DV]
            ms = jnp.mean(oh * oh, axis=-1, keepdims=True)
            parts.append(oh * lax.rsqrt(ms + EPS) * nw_ref[...])
        y = jnp.concatenate(parts, axis=1) * _silu(z_ref[0].astype(F32))
        o_ref[0] = y.astype(o_ref.dtype)
    else:
        o_ref[0] = o

    @pl.when(pl.program_id(1) == pl.num_programs(1) - 1)
    def _():
        sfin_ref[0] = s_scr[...]


def _gla_call(P, wa_pad, ba, s0, reverse, fin=None):
    b_, L, _ = P.shape
    T = min(GLA_BLOCK, L)
    nt = L // T
    if reverse:
        tmap = lambda t: nt - 1 - t
    else:
        tmap = lambda t: t
    ii = np.arange(T)
    same = (ii[:, None] // GLA_CHUNK) == (ii[None, :] // GLA_CHUNK)
    tri = same & ((ii[None, :] >= ii[:, None]) if reverse else (ii[None, :] <= ii[:, None]))
    tri = jnp.asarray(tri.astype(np.float32)).astype(BF16)

    def col(name, width):
        blk = COLS[name][0] // width
        return pl.BlockSpec((1, T, width), lambda b, t: (b, tmap(t), blk))

    in_specs = [col('g_q', GLA_KW), col('g_k', GLA_KW), col('g_v', GLA_VW),
                col('g_a', LANES), col('g_a_lo', LANES),
                pl.BlockSpec((LANES, GLA_KW), lambda b, t: (0, 0)),
                pl.BlockSpec((1, GLA_KW), lambda b, t: (0, 0)),
                pl.BlockSpec((1, GLA_DV, GLA_KW), lambda b, t: (b, 0, 0)),
                pl.BlockSpec((T, T), lambda b, t: (0, 0))]
    args = [P, P, P, P, P, wa_pad, ba, s0, tri]
    if fin is not None:
        o_prev, nw = fin
        in_specs += [pl.BlockSpec((1, T, GLA_VW), lambda b, t: (b, tmap(t), 0)),
                     col('g_z', GLA_VW),
                     pl.BlockSpec((1, GLA_DV), lambda b, t: (0, 0))]
        args += [o_prev, P, nw]
    out_dtype = BF16 if fin is not None else F32
    return pl.pallas_call(
        functools.partial(_gla_kernel, reverse=reverse, finalize=fin is not None, nchunks=T // GLA_CHUNK),
        grid=(b_, nt),
        in_specs=in_specs,
        out_specs=[pl.BlockSpec((1, T, GLA_VW), lambda b, t: (b, tmap(t), 0)),
                   pl.BlockSpec((1, GLA_DV, GLA_KW), lambda b, t: (b, 0, 0))],
        out_shape=[jax.ShapeDtypeStruct((b_, L, GLA_VW), out_dtype),
                   jax.ShapeDtypeStruct((b_, GLA_DV, GLA_KW), F32)],
        scratch_shapes=[pltpu.VMEM((GLA_DV, GLA_KW), F32)],
        compiler_params=_cparams("arbitrary", "arbitrary"),
        name="gla",
    )(*args)


def _norm_rope(a, gain, bd, cos, sin):
    sq = a * a
    sh, sl = _split(sq)
    ss = _dot(sh, bd) + _dot(sl, bd)
    an = a * lax.rsqrt(ss * (1.0 / ATTN_HD) + EPS) * gain
    w = a.shape[-1]
    lane = lax.broadcasted_iota(jnp.int32, an.shape, 1)
    partner = jnp.where(lane % 2 == 0, pltpu.roll(an, w - 1, 1), pltpu.roll(an, 1, 1))
    return an * cos + partner * sin


Q_SCALE = (ATTN_HD ** -0.5) * math.log2(math.e)
ATTN_QK_DEPTH = 4 * ATTN_HD
ATTN_TK_MAX = 640


def _split8(a):
    hi = a.astype(F8).astype(F32)
    lo = (a - hi).astype(F8)
    return hi.astype(BF16), lo.astype(BF16)


def _qkv_prep_kernel(q_ref, k_ref, v_ref, cq_ref, sq_ref, ck_ref, sk_ref, gq_ref, gk_ref,
                     bdq_ref, bdk_ref, eq_ref, ev_ref, qo_ref, ko_ref, vo_ref):
    nt = (((1,), (1,)), ((), ()))
    hd = ATTN_HD
    q = _norm_rope(q_ref[0].astype(F32), gq_ref[...], bdq_ref[...], cq_ref[...], sq_ref[...]) * Q_SCALE
    qh, ql = _split8(q)
    qt = lax.dot_general(eq_ref[...], jnp.concatenate([qh, ql], axis=1), nt, preferred_element_type=F32)
    for h in range(ATTN_HEADS):
        hi = qt[2 * hd * h:2 * hd * h + hd]
        lo = qt[2 * hd * h + hd:2 * hd * (h + 1)]
        qo_ref[0, ATTN_QK_DEPTH * h:ATTN_QK_DEPTH * (h + 1), :] = jnp.concatenate([hi, hi, lo, lo], axis=0).astype(F8)
    k = _norm_rope(k_ref[0].astype(F32), gk_ref[...], bdk_ref[...], ck_ref[...], sk_ref[...])
    kh, kl = _split8(k)
    for g in range(ATTN_KV_HEADS):
        cols = slice(g * hd, (g + 1) * hd)
        ko_ref[0, g] = jnp.concatenate([kh[:, cols], kl[:, cols], kh[:, cols], kl[:, cols]], axis=1).astype(F8)
    vb = v_ref[0].astype(BF16)
    vo_ref[0] = lax.dot_general(ev_ref[...], vb, nt, preferred_element_type=F32).astype(BF16)


def _qkv_prep_call(P, cos_q, sin_q, cos_k, sin_k, gq, gk, bdq, bdk, eye_q, eye_v):
    b_, L, _ = P.shape
    T = min(512, L)

    def col(name, width):
        blk = COLS[name][0] // width
        return pl.BlockSpec((1, T, width), lambda b, t: (b, t, blk))

    tab = lambda w: pl.BlockSpec((T, w), lambda b, t: (t, 0))
    const = lambda r, w: pl.BlockSpec((r, w), lambda b, t: (0, 0))
    return pl.pallas_call(
        _qkv_prep_kernel,
        grid=(b_, L // T),
        in_specs=[col('a_q', ATTN_QW), col('a_k', ATTN_KVW), col('a_v', ATTN_KVW),
                  tab(ATTN_QW), tab(ATTN_QW), tab(ATTN_KVW), tab(ATTN_KVW),
                  const(1, ATTN_QW), const(1, ATTN_KVW),
                  const(ATTN_QW, ATTN_QW), const(ATTN_KVW, ATTN_KVW),
                  const(2 * ATTN_QW, 2 * ATTN_QW), const(ATTN_KVW, ATTN_KVW)],
        out_specs=[pl.BlockSpec((1, ATTN_HEADS * ATTN_QK_DEPTH, T), lambda b, t: (b, 0, t)),
                   pl.BlockSpec((1, ATTN_KV_HEADS, T, ATTN_QK_DEPTH), lambda b, t: (b, 0, t, 0)),
                   pl.BlockSpec((1, ATTN_KVW, T), lambda b, t: (b, 0, t))],
        out_shape=[jax.ShapeDtypeStruct((b_, ATTN_HEADS * ATTN_QK_DEPTH, L), F8),
                   jax.ShapeDtypeStruct((b_, ATTN_KV_HEADS, L, ATTN_QK_DEPTH), F8),
                   jax.ShapeDtypeStruct((b_, ATTN_KVW, L), BF16)],
        compiler_params=_cparams("arbitrary", "arbitrary"),
        name="qkv_prep",
    )(P, P, P, cos_q, sin_q, cos_k, sin_k, gq, gk, bdq, bdk, eye_q, eye_v)


ATTN_SUM_ROWS = 16
ATTN_UNROLL = 2
ATTN_TQ = 512


def _attn_kernel(qt_ref, k_ref, vt_ref, z_ref, e_ref, o_ref, m_scr, a_scr, acc_scr, s_scr, p_scr,
                 *, tk, nk):
    m_scr[...] = jnp.full(m_scr.shape, -jnp.inf, F32)
    acc_scr[...] = jnp.zeros(acc_scr.shape, F32)
    ones = jnp.ones((ATTN_SUM_ROWS, tk), BF16)

    def scores(j, slot):
        start = j * tk if isinstance(j, int) else pl.multiple_of(j * tk, tk)
        kj = k_ref[0, 0, pl.ds(start, tk), :]
        for h in range(ATTN_GROUP):
            s_scr[slot, h] = _dot(kj, qt_ref[0, h * ATTN_QK_DEPTH:(h + 1) * ATTN_QK_DEPTH, :])

    scores(0, 0)

    def step(j, slot):
        scores(min(j + 1, nk - 1) if isinstance(j, int) else jnp.minimum(j + 1, nk - 1), 1 - slot)
        start = j * tk if isinstance(j, int) else pl.multiple_of(j * tk, tk)
        vte = jnp.concatenate([vt_ref[0, :, pl.ds(start, tk)], ones], axis=0)
        for h in range(ATTN_GROUP):
            s = s_scr[slot, h]
            m_old = m_scr[h]
            m_new = jnp.maximum(m_old, jnp.max(s, axis=0, keepdims=True))
            a_scr[h] = jnp.exp2(m_old - m_new)
            m_scr[h] = m_new
            p_scr[h] = jnp.exp2(s - m_new).astype(BF16)
        for h in range(ATTN_GROUP):
            acc_scr[h] = a_scr[h] * acc_scr[h] + _dot(vte, p_scr[h])

    def trip(i, carry):
        for u in range(ATTN_UNROLL):
            step(ATTN_UNROLL * i + u, u % 2)
        return carry

    ntrips = nk // ATTN_UNROLL
    lax.fori_loop(0, ntrips, trip, 0)
    for j in range(ntrips * ATTN_UNROLL, nk):
        step(j, j % 2)
    tn = (((0,), (0,)), ((), ()))
    out = None
    for h in range(ATTN_GROUP):
        acc = acc_scr[h]
        oh, ol = _split(acc[:ATTN_HD] / acc[ATTN_HD:ATTN_HD + 1])
        part = (lax.dot_general(oh, e_ref[h], tn, preferred_element_type=F32)
                + lax.dot_general(ol, e_ref[h], tn, preferred_element_type=F32))
        out = part if out is None else out + part
    o_ref[0] = (out * _silu(z_ref[0].astype(F32))).astype(o_ref.dtype)


def _attn_call(qt, k, vt, P, e_heads):
    b_, _, L = qt.shape
    Lk = k.shape[2]
    tq = min(ATTN_TQ, L)
    tk = max(t for t in range(LANES, ATTN_TK_MAX + 1, LANES) if Lk % t == 0)
    gw = ATTN_GROUP * ATTN_HD
    zblk = COLS['a_z'][0] // gw
    return pl.pallas_call(
        functools.partial(_attn_kernel, tk=tk, nk=Lk // tk),
        grid=(b_, ATTN_KV_HEADS, L // tq),
        in_specs=[pl.BlockSpec((1, ATTN_GROUP * ATTN_QK_DEPTH, tq), lambda b, g, i: (b, g, i)),
                  pl.BlockSpec((1, 1, Lk, ATTN_QK_DEPTH), lambda b, g, i: (b, g, 0, 0)),
                  pl.BlockSpec((1, ATTN_HD, Lk), lambda b, g, i: (b, g, 0)),
                  pl.BlockSpec((1, tq, gw), lambda b, g, i: (b, i, zblk + g)),
                  pl.BlockSpec((ATTN_GROUP, ATTN_HD, gw), lambda b, g, i: (0, 0, 0))],
        out_specs=pl.BlockSpec((1, tq, gw), lambda b, g, i: (b, i, g)),
        out_shape=jax.ShapeDtypeStruct((b_, L, ATTN_QW), BF16),
        scratch_shapes=[pltpu.VMEM((ATTN_GROUP, 1, tq), F32), pltpu.VMEM((ATTN_GROUP, 1, tq), F32),
                        pltpu.VMEM((ATTN_GROUP, ATTN_HD + ATTN_SUM_ROWS, tq), F32),
                        pltpu.VMEM((2, ATTN_GROUP, tk, tq), F32), pltpu.VMEM((ATTN_GROUP, tk, tq), BF16)],
        compiler_params=_cparams("arbitrary", "arbitrary", "arbitrary"),
        name="attn",
    )(qt, k, vt, P, e_heads)


def _short_conv_kernel(u_ref, up_ref, un_ref, w_ref, b_ref, v_ref, x1_ref, x2_ref):
    t = pl.program_id(1)
    nt = pl.num_programs(1)
    u = u_ref[0].astype(F32)
    T = u.shape[0]
    hr = up_ref.shape[1]
    prev_row = jnp.where(t > 0, up_ref[0, hr - 1:hr, :].astype(F32), 0.0)
    next_row = jnp.where(t < nt - 1, un_ref[0, 0:1, :].astype(F32), 0.0)
    row = lax.broadcasted_iota(jnp.int32, u.shape, 0)
    u_m1 = jnp.where(row == 0, prev_row, pltpu.roll(u, 1, 0))
    u_p1 = jnp.where(row == T - 1, next_row, pltpu.roll(u, T - 1, 0))
    w = w_ref[...]
    out = b_ref[...] + u_m1 * w[0:1] + u * w[1:2] + u_p1 * w[2:3]
    v_ref[0] = out[:, 0:HY_W]
    x1_ref[0] = out[:, HY_W:2 * HY_W]
    x2_ref[0] = out[:, 2 * HY_W:3 * HY_W]


def _short_conv_call(P, w, bias):
    b_, L, _ = P.shape
    T = min(512, L)
    cw = (HY_ORDER + 1) * HY_W
    blk = COLS['y_u'][0] // cw
    hr = 16
    hb = T // hr
    nh = L // hr
    out_spec = pl.BlockSpec((1, T, HY_W), lambda b, t: (b, t, 0))
    shp = jax.ShapeDtypeStruct((b_, L, HY_W), F32)
    return pl.pallas_call(
        _short_conv_kernel,
        grid=(b_, L // T),
        in_specs=[pl.BlockSpec((1, T, cw), lambda b, t: (b, t, blk)),
                  pl.BlockSpec((1, hr, cw), lambda b, t: (b, jnp.maximum(t * hb - 1, 0), blk)),
                  pl.BlockSpec((1, hr, cw), lambda b, t: (b, jnp.minimum((t + 1) * hb, nh - 1), blk)),
                  pl.BlockSpec((HY_SHORT, cw), lambda b, t: (0, 0)),
                  pl.BlockSpec((1, cw), lambda b, t: (0, 0))],
        out_specs=[out_spec, out_spec, out_spec],
        out_shape=[shp, shp, shp],
        compiler_params=_cparams("arbitrary", "arbitrary"),
        name="short_conv",
    )(P, P, P, w, bias)


FFT_ROWS = 8


def _filter_kernel(emb_ref, w1_ref, b1_ref, f1_ref, w2_ref, b2_ref, f2_ref, w3_ref, dl_ref,
                   hf_ref, hb_ref, sum_ref):
    t = pl.program_id(0)
    emb = emb_ref[...]
    h = jnp.sin(f1_ref[...] * (_dot_hp(emb, w1_ref[...]) + b1_ref[...]))
    h = jnp.sin(f2_ref[...] * (_dot_hp(h, w2_ref[...]) + b2_ref[...]))
    hh, hl = _split(h)
    tt = emb[:, 0:1]

    @pl.when(t == 0)
    def _():
        sum_ref[...] = jnp.zeros_like(sum_ref)

    lag0 = (lax.broadcasted_iota(jnp.int32, (h.shape[0], HY_W), 0) + t * h.shape[0]) == 0
    for o in range(HY_ORDER):
        for d, out_ref in enumerate((hf_ref, hb_ref)):
            cols = slice((2 * o + d) * HY_W, (2 * o + d + 1) * HY_W)
            wh, wl = _split(w3_ref[:, cols])
            f = _dot(hh, wh) + _dot(hl, wh) + _dot(hh, wl)
            f = f * (jnp.exp(-tt * dl_ref[:, cols]) + HY_MOD_SHIFT)
            sum_ref[:, cols] += jnp.broadcast_to(jnp.sum(jnp.abs(f), axis=0, keepdims=True), (8, HY_W))
            if d == 1:
                f = jnp.where(lag0, 0.0, f)
            out_ref[o] = pltpu.einshape("(an)c->a(nc)", f, a=FFT_ROWS)


def _filter_call(emb, w1, b1, f1, w2, b2, f2, w3, deltas, n1, n2):
    L = emb.shape[0]
    TL = FFT_ROWS * n2
    const = lambda r, w: pl.BlockSpec((r, w), lambda t: (0, 0))
    hspec = pl.BlockSpec((HY_ORDER, FFT_ROWS, n2 * HY_W), lambda t: (0, t, 0))
    hshape = jax.ShapeDtypeStruct((HY_ORDER, n1 // 2, n2 * HY_W), F32)
    return pl.pallas_call(
        _filter_kernel,
        grid=(L // TL,),
        in_specs=[pl.BlockSpec((TL, LANES), lambda t: (t, 0)),
                  const(LANES, HY_FFN), const(1, HY_FFN), const(1, HY_FFN),
                  const(HY_FFN, HY_FFN), const(1, HY_FFN), const(1, HY_FFN),
                  const(HY_FFN, N_FILT), const(1, N_FILT)],
        out_specs=[hspec, hspec, pl.BlockSpec((8, N_FILT), lambda t: (0, 0))],
        out_shape=[hshape, hshape, jax.ShapeDtypeStruct((8, N_FILT), F32)],
        compiler_params=_cparams("arbitrary"),
        name="hy_filter",
    )(emb, w1, b1, f1, w2, b2, f2, w3, deltas)


def _fft_sizes(n):
    lg = int(round(math.log2(n)))
    assert 1 << lg == n
    n1 = 1 << ((lg + 1) // 2)
    return n1, n // n1


def _bf16_pair(a):
    a = np.asarray(a, np.float32)
    hi = jnp.asarray(a, F32).astype(BF16)
    lo = (jnp.asarray(a, F32) - hi.astype(F32)).astype(BF16)
    return hi, lo


def _fft_tables(n1, n2):
    n = n1 * n2
    h1 = n1 // 2
    k1 = np.arange(n1)[:, None]
    a = 2.0 * np.pi * ((k1 * np.arange(h1)[None, :]) % n1) / n1
    c, s = np.cos(a), np.sin(a)
    fa = np.block([[c, s], [-s, c]])
    fa_real = np.concatenate([c, -s], axis=0)
    fd = np.block([[c.T, -s.T], [s.T, c.T]])
    k2 = np.arange(n2)[:, None]
    b = 2.0 * np.pi * ((k2 * np.arange(n2)[None, :]) % n2) / n2
    cb, sb = np.cos(b), np.sin(b)
    fb = np.block([[cb, sb], [-sb, cb]])
    fc = np.block([[cb, -sb], [sb, cb]])
    kn = (jnp.arange(n1, dtype=jnp.int32)[:, None] * jnp.arange(n2, dtype=jnp.int32)[None, :]) % n
    tw = kn.astype(F32) * (2.0 * math.pi / n)
    twr = jnp.broadcast_to(jnp.cos(tw)[:, :, None], (n1, n2, LANES))
    twi = jnp.broadcast_to(-jnp.sin(tw)[:, :, None], (n1, n2, LANES))
    return dict(fa=_bf16_pair(fa), fa_real=_bf16_pair(fa_real), fd=_bf16_pair(fd),
                fb=_bf16_pair(fb), fc=_bf16_pair(fc), twr=twr, twi=twi)


def _fa_kernel(x_ref, mh_ref, ml_ref, o_ref, xh_scr, *, packed, part_axis):
    @pl.when(pl.program_id(part_axis) == 0)
    def _():
        if packed:
            x = x_ref[...]
            x = x.reshape(x.shape[0] * x.shape[1], x.shape[2], x.shape[3])
            x = pltpu.einshape("rnc->r(nc)", x)
        else:
            x = x_ref[0]
        xh_scr[...] = x.astype(BF16)

    xh = xh_scr[...]
    o_ref[0] = _dot(mh_ref[...], xh) + _dot(ml_ref[...], xh)


def _fa_call(x4, mats, n1):
    mh, ml = mats
    _, h1, n2, wd = x4.shape
    wt = FFT_ROWS * wd
    return pl.pallas_call(
        functools.partial(_fa_kernel, packed=True, part_axis=1),
        grid=(n2 // FFT_ROWS, 2),
        in_specs=[pl.BlockSpec((2, h1, FFT_ROWS, wd), lambda j, p: (0, 0, j, 0)),
                  pl.BlockSpec((n1, n1), lambda j, p: (p, 0)),
                  pl.BlockSpec((n1, n1), lambda j, p: (p, 0))],
        out_specs=pl.BlockSpec((1, n1, wt), lambda j, p: (p, 0, j)),
        out_shape=jax.ShapeDtypeStruct((2, n1, n2 * wd), F32),
        scratch_shapes=[pltpu.VMEM((n1, wt), BF16)],
        compiler_params=_cparams("arbitrary", "arbitrary"),
        name="fft_a",
    )(x4, mh, ml)


def _fa_real_call(x, mats, n1):
    mh, ml = mats
    ng, h1, lanes = x.shape
    wt = min(FFT_ROWS * HY_W, lanes)
    out = pl.pallas_call(
        functools.partial(_fa_kernel, packed=False, part_axis=2),
        grid=(ng, lanes // wt, 2),
        in_specs=[pl.BlockSpec((1, h1, wt), lambda g, j, p: (g, 0, j)),
                  pl.BlockSpec((n1, h1), lambda g, j, p: (p, 0)),
                  pl.BlockSpec((n1, h1), lambda g, j, p: (p, 0))],
        out_specs=pl.BlockSpec((1, n1, wt), lambda g, j, p: (2 * g + p, 0, j)),
        out_shape=jax.ShapeDtypeStruct((2 * ng, n1, lanes), F32),
        scratch_shapes=[pltpu.VMEM((h1, wt), BF16)],
        compiler_params=_cparams("arbitrary", "arbitrary", "arbitrary"),
        name="fft_a_real",
    )(x, mh, ml)
    return out.reshape(ng, 2, n1, lanes)


def _lane_tile(a, width):
    return jnp.concatenate([a] * (width // a.shape[-1]), axis=-1)


def _fb_filter_kernel(tf_ref, tb_ref, twr_ref, twi_ref, fh_ref, fl_ref, sf_ref, sb_ref, g_ref,
                      tf4_scr, tb4_scr, *, n2, scale):
    w = g_ref.shape[-1]
    tf4_scr[...] = pltpu.einshape("pk(nc)->pknc", tf_ref[0], n=n2)
    tb4_scr[...] = pltpu.einshape("pk(nc)->pknc", tb_ref[0], n=n2)
    inv = jnp.concatenate([scale / sf_ref[0:1, :], scale / sb_ref[0:1, :]], axis=1)

    def body(kk, carry):
        twr = _lane_tile(twr_ref[kk], 2 * w)
        twi = _lane_tile(twi_ref[kk], 2 * w)
        tr = jnp.concatenate([tf4_scr[0, kk], tb4_scr[0, kk]], axis=1)
        ti = jnp.concatenate([tf4_scr[1, kk], tb4_scr[1, kk]], axis=1)
        p = jnp.concatenate([tr * twr - ti * twi, tr * twi + ti * twr], axis=0)
        z = _dot_mat(fh_ref[...], fl_ref[...], p) * inv
        g_ref[0, 0, kk] = z[:n2, :w] + z[:n2, w:]
        g_ref[0, 1, kk] = z[n2:, :w] - z[n2:, w:]
        return carry

    lax.fori_loop(0, FFT_ROWS, body, 0, unroll=2)


def _fb_filter_call(thf, thb, sums, tabs, n1, n2):
    fh, fl = tabs['fb']
    lanes = n2 * HY_W
    tspec = pl.BlockSpec((1, 2, FFT_ROWS, lanes), lambda g, i: (g, 0, i, 0))
    twspec = pl.BlockSpec((FFT_ROWS, n2, LANES), lambda g, i: (i, 0, 0))
    sspec = lambda d: pl.BlockSpec((8, HY_W), lambda g, i: (0, 2 * g + d))
    return pl.pallas_call(
        functools.partial(_fb_filter_kernel, n2=n2, scale=1.0 / (n1 * n2)),
        grid=(HY_ORDER, n1 // FFT_ROWS),
        in_specs=[tspec, tspec, twspec, twspec,
                  pl.BlockSpec(fh.shape, lambda g, i: (0, 0)), pl.BlockSpec(fl.shape, lambda g, i: (0, 0)),
                  sspec(0), sspec(1)],
        out_specs=pl.BlockSpec((1, 2, FFT_ROWS, n2, HY_W), lambda g, i: (g, 0, i, 0, 0)),
        out_shape=jax.ShapeDtypeStruct((HY_ORDER, 2, n1, n2, HY_W), F32),
        scratch_shapes=[pltpu.VMEM((2, FFT_ROWS, n2, HY_W), F32), pltpu.VMEM((2, FFT_ROWS, n2, HY_W), F32)],
        compiler_params=_cparams("arbitrary", "arbitrary"),
        name="fft_b_filter",
    )(thf, thb, tabs['twr'], tabs['twi'], fh, fl, sums, sums)


def _fb_kernel(t_ref, twr_ref, twi_ref, fh_ref, fl_ref, ch_ref, cl_ref, g_ref, o_ref, t4_scr, *, n2):
    w = g_ref.shape[-1]
    t4_scr[...] = pltpu.einshape("pk(nc)->pknc", t_ref[...], n=n2)

    def body(kk, carry):
        tr = t4_scr[0, kk]
        ti = t4_scr[1, kk]
        twr = _lane_tile(twr_ref[kk], w)
        twi = _lane_tile(twi_ref[kk], w)
        p = jnp.concatenate([tr * twr - ti * twi, tr * twi + ti * twr], axis=0)
        z = _dot_mat(fh_ref[...], fl_ref[...], p)
        zr, zi = z[:n2], z[n2:]
        gr = g_ref[0, 0, kk]
        gi = g_ref[0, 1, kk]
        y = jnp.concatenate([zr * gr - zi * gi, zr * gi + zi * gr], axis=0)
        v = _dot_mat(ch_ref[...], cl_ref[...], y)
        vr, vi = v[:n2], v[n2:]
        t4_scr[0, kk] = vr * twr + vi * twi
        t4_scr[1, kk] = vi * twr - vr * twi
        return carry

    lax.fori_loop(0, FFT_ROWS, body, 0, unroll=2)
    o_ref[...] = pltpu.einshape("pknc->pk(nc)", t4_scr[...])


def _fb_call(t, g, order, tabs, n1, n2):
    fh, fl = tabs['fb']
    ch, cl = tabs['fc']
    lanes = n2 * HY_W
    mat = lambda m: pl.BlockSpec(m.shape, lambda i: (0, 0))
    dat = pl.BlockSpec((2, FFT_ROWS, lanes), lambda i: (0, i, 0))
    twspec = pl.BlockSpec((FFT_ROWS, n2, LANES), lambda i: (i, 0, 0))
    return pl.pallas_call(
        functools.partial(_fb_kernel, n2=n2),
        grid=(n1 // FFT_ROWS,),
        in_specs=[dat, twspec, twspec, mat(fh), mat(fl), mat(ch), mat(cl),
                  pl.BlockSpec((1, 2, FFT_ROWS, n2, HY_W), lambda i: (order, 0, i, 0, 0))],
        out_specs=dat,
        out_shape=jax.ShapeDtypeStruct((2, n1, lanes), F32),
        scratch_shapes=[pltpu.VMEM((2, FFT_ROWS, n2, HY_W), F32)],
        compiler_params=_cparams("arbitrary"),
        name="fft_b",
    )(t, tabs['twr'], tabs['twi'], fh, fl, ch, cl, g)


def _fd_kernel(u_ref, mh_ref, ml_ref, z_ref, x_ref, sk_ref, o_ref):
    u = u_ref[...]
    u = u.reshape(u.shape[0] * u.shape[1], u.shape[2])
    y = _dot_mat(mh_ref[...], ml_ref[...], u)
    y = pltpu.einshape("r(nc)->rnc", y, n=FFT_ROWS)
    o_ref[0] = x_ref[0] * (y + sk_ref[...] * z_ref[0])


def _fd_call(u, mats, z4, x4, skip_row, n1):
    mh, ml = mats
    b_, h1, n2, wd = z4.shape
    wt = FFT_ROWS * wd
    dat = pl.BlockSpec((1, h1, FFT_ROWS, wd), lambda j, p: (p, 0, j, 0))
    return pl.pallas_call(
        _fd_kernel,
        grid=(n2 // FFT_ROWS, b_),
        in_specs=[pl.BlockSpec((2, n1, wt), lambda j, p: (0, 0, j)),
                  pl.BlockSpec((h1, 2 * n1), lambda j, p: (p, 0)),
                  pl.BlockSpec((h1, 2 * n1), lambda j, p: (p, 0)),
                  dat, dat, pl.BlockSpec((1, wd), lambda j, p: (0, 0))],
        out_specs=dat,
        out_shape=jax.ShapeDtypeStruct(z4.shape, F32),
        compiler_params=pltpu.CompilerParams(dimension_semantics=("arbitrary", "arbitrary"),
                                             vmem_limit_bytes=FFT_D_VMEM_LIMIT),
        name="fft_d",
    )(u, mh, ml, z4, x4, skip_row)


def _hyena_filter_spectrum(L, tabs, n1, n2, fp):
    f32 = F32
    t = jnp.linspace(0.0, 1.0, L, dtype=f32)[:, None]
    w = 2.0 * math.pi * jnp.arange(L, dtype=f32)[:, None] / L
    fr = jnp.linspace(1e-4, HY_BANDS - 1, HY_BANDS, dtype=f32)[None]
    emb = jnp.concatenate([t, jnp.cos(fr * w), -jnp.sin(fr * w)], axis=-1)
    emb = jnp.pad(emb, ((0, 0), (0, LANES - HY_EMB)))
    deltas = jnp.abs(jnp.linspace(math.log(HY_DECAY_TARGET) / HY_DECAY_SHORT_PCT,
                                  math.log(HY_DECAY_TARGET) / HY_DECAY_LONG_PCT, N_FILT, dtype=f32))[None]
    f1_w, f1_b, f1_freq, f2_w, f2_b, f2_freq, f3_w = fp
    w1 = jnp.pad(f1_w, ((0, LANES - HY_EMB), (0, 0)))
    hf, hb, sums = _filter_call(emb, w1, f1_b[None], f1_freq[None], f2_w, f2_b[None],
                                f2_freq[None], f3_w, deltas, n1, n2)
    thf = _fa_real_call(hf, tabs['fa_real'], n1)
    thb = _fa_real_call(hb, tabs['fa_real'], n1)
    return _fb_filter_call(thf, thb, sums, tabs, n1, n2)


def _hyena_conv(z, xg, g, order, skip, tabs, n1, n2):
    b_, L, wd = z.shape
    assert b_ == 2, "the two batch rows ride as the real and imaginary parts of one transform"
    z4 = z.reshape(b_, n1 // 2, n2, wd)
    x4 = xg.reshape(b_, n1 // 2, n2, wd)
    t = _fa_call(z4, tabs['fa'], n1)
    u = _fb_call(t, g, order, tabs, n1, n2)
    out = _fd_call(u, tabs['fd'], z4, x4, skip[None, :], n1)
    return out.reshape(b_, L, wd)


def _merge_kernel(yg_ref, ya_ref, yh_ref, yz_ref, mg_ref, ma_ref, mh_ref, x_ref, gate_ref,
                  wg_ref, wa_ref, wh_ref, wo_ref, fn_ref, o_ref, *, final):
    yh = (yh_ref[0] * _silu(yz_ref[0].astype(F32))).astype(BF16)
    y = (_sigmoid(mg_ref[0].astype(F32)) * _dot(yg_ref[0], wg_ref[...])
         + _sigmoid(ma_ref[0].astype(F32)) * _dot(ya_ref[0], wa_ref[...])
         + _sigmoid(mh_ref[0].astype(F32)) * _dot(yh, wh_ref[...]))
    out = _dot(y.astype(BF16), wo_ref[...])
    xn = x_ref[0] + gate_ref[0] * out
    if final:
        ms = jnp.mean(xn * xn, axis=-1, keepdims=True)
        xn = xn * lax.rsqrt(ms + EPS) * fn_ref[...]
    o_ref[0] = xn


def _merge_call(y_gla, y_attn, y_hy, P, x, gate, wg, wa, wh, wo, fnorm, final):
    b_, L, d = x.shape
    T = min(512, L)
    row = lambda w, blk=0: pl.BlockSpec((1, T, w), lambda b, t: (b, t, blk))
    const = lambda r, w: pl.BlockSpec((r, w), lambda b, t: (0, 0))
    mblk = COLS['m'][0] // d
    return pl.pallas_call(
        functools.partial(_merge_kernel, final=final),
        grid=(b_, L // T),
        in_specs=[row(BRANCH_W), row(BRANCH_W), row(BRANCH_W), row(HY_W, COLS['y_z'][0] // HY_W),
                  row(d, mblk), row(d, mblk + 1), row(d, mblk + 2), row(d),
                  pl.BlockSpec((1, 1, d), lambda b, t: (b, 0, 0)),
                  const(BRANCH_W, d), const(BRANCH_W, d), const(BRANCH_W, d), const(d, d), const(1, d)],
        out_specs=row(d),
        out_shape=jax.ShapeDtypeStruct((b_, L, d), F32),
        compiler_params=_cparams("arbitrary", "arbitrary"),
        name="merge",
    )(y_gla, y_attn, y_hy, P, P, P, P, x, gate, wg, wa, wh, wo, fnorm)


def _pack_w_in(w):
    parts = dict(zip(('g_q', 'g_k', 'g_v', 'g_z', 'g_af', 'g_ab', 'a_q', 'a_k', 'a_v', 'a_z', 'y_u', 'y_z', 'm'),
                     jnp.split(w, [int(i) for i in np.cumsum(SPLITS)[:-1]], axis=-1)))
    parts['g_a'] = jnp.pad(jnp.concatenate([parts['g_af'], parts['g_ab']], axis=-1),
                           ((0, 0), (0, LANES - 2 * GLA_RANK)))
    parts['g_a_lo'] = jnp.zeros_like(parts['g_a'])
    order = sorted(COLS, key=lambda n: COLS[n][0])
    return jnp.concatenate([parts[n] for n in order], axis=-1).astype(BF16)


def _rope_tables(L, heads):
    t = jnp.arange(L)
    row = (t // GRID_W).astype(F32)
    colp = (t % GRID_W).astype(F32)
    half = ATTN_HD // 2
    inv = ROPE_THETA ** (-jnp.arange(0, half, 2, dtype=F32) / half)
    ang = jnp.concatenate([row[:, None] * inv, colp[:, None] * inv], axis=-1)
    cos = jnp.repeat(jnp.cos(ang), 2, axis=-1)
    sin = jnp.stack([-jnp.sin(ang), jnp.sin(ang)], axis=-1).reshape(L, ATTN_HD)
    return jnp.tile(cos, (1, heads)), jnp.tile(sin, (1, heads))


def _identity_rope(L, heads):
    return jnp.ones((L, heads * ATTN_HD), F32), jnp.zeros((L, heads * ATTN_HD), F32)


def _block_diag_ones(width):
    i = np.arange(width) // ATTN_HD
    return jnp.asarray((i[:, None] == i[None, :]).astype(np.float32)).astype(BF16)


def kernel(x, c, ctx, c_ctx, w_ada, b_ada, w_in, gla_wa_f, gla_ba_f, gla_wa_b, gla_ba_b, gla_norm,
           attn_qnorm, attn_knorm, hy_conv_w, hy_conv_b, hy_f1_w, hy_f1_b, hy_f1_freq, hy_f2_w,
           hy_f2_b, hy_f2_freq, hy_f3_w, hy_skip, w_br_gla, w_br_attn, w_br_hy, w_out, final_norm):
    b_, L, d = x.shape
    Lc = ctx.shape[1]
    depth = w_ada.shape[0]

    cos_q, sin_q = _rope_tables(L, ATTN_HEADS)
    cos_k, sin_k = cos_q[:, :ATTN_KVW], sin_q[:, :ATTN_KVW]
    cos_qc, sin_qc = _identity_rope(Lc, ATTN_HEADS)
    cos_kc, sin_kc = cos_qc[:, :ATTN_KVW], sin_qc[:, :ATTN_KVW]
    bdq, bdk = _block_diag_ones(ATTN_QW), _block_diag_ones(ATTN_KVW)
    eye_v = jnp.eye(ATTN_KVW, dtype=BF16)
    rr = np.arange(2 * ATTN_QW)
    src = (rr // ATTN_HD % 2) * ATTN_QW + (rr // (2 * ATTN_HD)) * ATTN_HD + rr % ATTN_HD
    eye_q = jnp.asarray((src[:, None] == np.arange(2 * ATTN_QW)[None, :]).astype(np.float32)).astype(BF16)
    gw = ATTN_GROUP * ATTN_HD
    e_heads = jnp.stack([jnp.eye(ATTN_HD, gw, k=h * ATTN_HD, dtype=BF16) for h in range(ATTN_GROUP)])
    n1, n2 = _fft_sizes(2 * L)
    n1c, n2c = _fft_sizes(2 * Lc)
    tabs = _fft_tables(n1, n2)
    tabs_c = _fft_tables(n1c, n2c)
    zero_state = jnp.zeros((b_, GLA_DV, GLA_KW), F32)

    cmat = jnp.concatenate([c, c_ctx[None], jnp.zeros((8 - b_ - 1, d), F32)], axis=0)
    mods = _ada_call(cmat, w_ada, b_ada)

    for l in range(depth):
        need_ctx = l < depth - 1
        shift, scale, gate = [m[:b_, None, :] for m in jnp.split(mods[l], 3, axis=-1)]
        shift_c, scale_c, gate_c = [jnp.broadcast_to(m[b_:b_ + 1, None, :], (b_, 1, d))
                                    for m in jnp.split(mods[l], 3, axis=-1)]
        w_packed = _pack_w_in(w_in[l])
        P = _proj_call(x, scale, shift, w_packed)
        Pc = _proj_call(ctx, scale_c, shift_c, w_packed)

        rk = GLA_RANK
        wa_f = jnp.zeros((LANES, GLA_KW), F32).at[0:rk].set(gla_wa_f[l])
        wa_b = jnp.zeros((LANES, GLA_KW), F32).at[rk:2 * rk].set(gla_wa_b[l])
        ba_f, ba_b = gla_ba_f[l][None], gla_ba_b[l][None]
        nw = gla_norm[l][None]
        oc_f, sc_f = _gla_call(Pc, wa_f, ba_f, zero_state, reverse=False)
        yc_gla, sc_b = _gla_call(Pc, wa_b, ba_b, zero_state, reverse=True, fin=(oc_f, nw))
        o_f, _ = _gla_call(P, wa_f, ba_f, sc_f, reverse=False)
        y_gla, _ = _gla_call(P, wa_b, ba_b, sc_b, reverse=True, fin=(o_f, nw))

        gq = jnp.tile(attn_qnorm[l], ATTN_HEADS)[None]
        gk = jnp.tile(attn_knorm[l], ATTN_KV_HEADS)[None]
        q_a, k_a, v_a = _qkv_prep_call(P, cos_q, sin_q, cos_k, sin_k, gq, gk, bdq, bdk, eye_q, eye_v)
        qc_a, kc_a, vc_a = _qkv_prep_call(Pc, cos_qc, sin_qc, cos_kc, sin_kc, gq, gk, bdq, bdk,
                                          eye_q, eye_v)
        y_attn = _attn_call(q_a, jnp.concatenate([k_a, kc_a], axis=2),
                            jnp.concatenate([v_a, vc_a], axis=2), P, e_heads)

        fp = (hy_f1_w[l], hy_f1_b[l], hy_f1_freq[l], hy_f2_w[l], hy_f2_b[l], hy_f2_freq[l], hy_f3_w[l])
        g_spec = _hyena_filter_spectrum(L, tabs, n1, n2, fp)
        v0, x1, x2 = _short_conv_call(P, hy_conv_w[l], hy_conv_b[l][None])
        z1 = _hyena_conv(v0, x1, g_spec, 0, hy_skip[l, 0], tabs, n1, n2)
        y_hy = _hyena_conv(z1, x2, g_spec, 1, hy_skip[l, 1], tabs, n1, n2)

        wg, wa, wh, wo = (w_br_gla[l].astype(BF16), w_br_attn[l].astype(BF16),
                          w_br_hy[l].astype(BF16), w_out[l].astype(BF16))
        fn = final_norm[None]
        x_new = _merge_call(y_gla, y_attn, y_hy, P, x, gate, wg, wa, wh, wo, fn, final=not need_ctx)

        if need_ctx:
            yc_attn = _attn_call(qc_a, kc_a, vc_a, Pc, e_heads)
            gc_spec = _hyena_filter_spectrum(Lc, tabs_c, n1c, n2c, fp)
            vc0, xc1, xc2 = _short_conv_call(Pc, hy_conv_w[l], hy_conv_b[l][None])
            zc1 = _hyena_conv(vc0, xc1, gc_spec, 0, hy_skip[l, 0], tabs_c, n1c, n2c)
            yc_hy = _hyena_conv(zc1, xc2, gc_spec, 1, hy_skip[l, 1], tabs_c, n1c, n2c)
            ctx = _merge_call(yc_gla, yc_attn, yc_hy, Pc, ctx, gate_c, wg, wa, wh, wo, fn, final=False)
        x = x_new

    return x
```

```python
import functools
import math

import numpy as np
import jax
import jax.numpy as jnp
from jax import lax
from jax.experimental import pallas as pl
from jax.experimental.pallas import tpu as pltpu

F32 = jnp.float32
BF16 = jnp.bfloat16
F8 = jnp.float8_e4m3fn

D_MODEL = 1024
GRID_W = 64
BRANCH_W = D_MODEL // 2
N_BRANCH = 3
EPS = 1e-6
GLA_HEADS = 4
GLA_DV = BRANCH_W // GLA_HEADS
GLA_DK = GLA_DV // 2
GLA_KW = GLA_HEADS * GLA_DK
GLA_VW = GLA_HEADS * GLA_DV
GLA_RANK = 16
GLA_GATE_NORM = 16.0
GLA_CHUNK = 64
GLA_BLOCK = 256
ATTN_HD = 64
ATTN_HEADS = BRANCH_W // ATTN_HD
ATTN_KV_HEADS = ATTN_HEADS // 4
ATTN_GROUP = ATTN_HEADS // ATTN_KV_HEADS
ATTN_QW = ATTN_HEADS * ATTN_HD
ATTN_KVW = ATTN_KV_HEADS * ATTN_HD
ROPE_THETA = 10000.0
HY_W = BRANCH_W
HY_ORDER = 2
HY_EMB = 33
HY_BANDS = (HY_EMB - 1) // 2
HY_FFN = 64
HY_SHORT = 3
HY_MOD_SHIFT = 0.05
HY_DECAY_SHORT_PCT = 0.3
HY_DECAY_LONG_PCT = 1.5
HY_DECAY_TARGET = 1e-2
N_FILT = 2 * HY_ORDER * HY_W
SPLITS = (GLA_KW, GLA_KW, GLA_VW, GLA_VW, GLA_RANK, GLA_RANK,
          ATTN_QW, ATTN_KVW, ATTN_KVW, ATTN_QW,
          (HY_ORDER + 1) * HY_W, HY_W,
          N_BRANCH * D_MODEL)

LANES = 128
VMEM_LIMIT = 48 * 1024 * 1024
FFT_D_VMEM_LIMIT = 56 * 1024 * 1024

COLS = {
    'm': (0, 3072), 'y_u': (3072, 1536), 'g_v': (4608, 512), 'g_z': (5120, 512),
    'a_q': (5632, 512), 'a_z': (6144, 512), 'y_z': (6656, 512), 'g_q': (7168, 256),
    'g_k': (7424, 256), 'a_k': (7680, 128), 'a_v': (7808, 128), 'g_a': (7936, 128),
    'g_a_lo': (8064, 128),
}
N_PROJ = 8192
PROJ_TN = 2048


def _cparams(*sem):
    return pltpu.CompilerParams(dimension_semantics=sem, vmem_limit_bytes=VMEM_LIMIT)


def _split(a):
    hi = a.astype(BF16)
    lo = (a - hi.astype(F32)).astype(BF16)
    return hi, lo


def _dot(a, b):
    return jnp.dot(a, b, preferred_element_type=F32)


def _dot_hp(a, b):
    ah, al = _split(a)
    bh, bl = _split(b)
    return _dot(ah, bh) + _dot(al, bh) + _dot(ah, bl)


def _dot_mat(mh, ml, x):
    xh = x.astype(BF16)
    return _dot(mh, xh) + _dot(ml, xh)


def _sigmoid(x):
    return 1.0 / (1.0 + jnp.exp(-x))


def _silu(x):
    return x * _sigmoid(x)


def _ada_kernel(c_ref, w_ref, b_ref, o_ref):
    o_ref[0] = _dot_hp(_silu(c_ref[...]), w_ref[0]) + b_ref[0]


def _ada_call(cmat, w_ada, b_ada):
    depth, d, n3 = w_ada.shape
    tn = 1024
    return pl.pallas_call(
        _ada_kernel,
        grid=(depth, n3 // tn),
        in_specs=[pl.BlockSpec((8, d), lambda l, j: (0, 0)),
                  pl.BlockSpec((1, d, tn), lambda l, j: (l, 0, j)),
                  pl.BlockSpec((1, 1, tn), lambda l, j: (l, 0, j))],
        out_specs=pl.BlockSpec((1, 8, tn), lambda l, j: (l, 0, j)),
        out_shape=jax.ShapeDtypeStruct((depth, 8, n3), F32),
        compiler_params=_cparams("arbitrary", "arbitrary"),
        name="ada",
    )(cmat, w_ada, b_ada.reshape(depth, 1, n3))


def _proj_kernel(x_ref, sc_ref, sh_ref, w_ref, o_ref, *, res_tile, hi_off, lo_off):
    x = x_ref[0]
    ms = jnp.mean(x * x, axis=-1, keepdims=True)
    h = x * lax.rsqrt(ms + EPS) * (1.0 + sc_ref[0]) + sh_ref[0]
    res = _dot(h.astype(BF16), w_ref[...])
    out = res.astype(o_ref.dtype)
    o_ref[0] = out

    @pl.when(pl.program_id(0) == res_tile)
    def _():
        hi = slice(hi_off, hi_off + LANES)
        o_ref[0, :, lo_off:lo_off + LANES] = (res[:, hi] - out[:, hi].astype(F32)).astype(o_ref.dtype)


def _proj_call(x, scale, shift, w):
    b_, L, d = x.shape
    n = w.shape[1]
    tm = min(1024, L)
    tn = PROJ_TN
    hi0, lo0 = COLS['g_a'][0], COLS['g_a_lo'][0]
    assert hi0 // tn == lo0 // tn
    return pl.pallas_call(
        functools.partial(_proj_kernel, res_tile=hi0 // tn, hi_off=hi0 % tn, lo_off=lo0 % tn),
        grid=(n // tn, b_, L // tm),
        in_specs=[pl.BlockSpec((1, tm, d), lambda j, b, i: (b, i, 0)),
                  pl.BlockSpec((1, 1, d), lambda j, b, i: (b, 0, 0)),
                  pl.BlockSpec((1, 1, d), lambda j, b, i: (b, 0, 0)),
                  pl.BlockSpec((d, tn), lambda j, b, i: (0, j))],
        out_specs=pl.BlockSpec((1, tm, tn), lambda j, b, i: (b, i, j)),
        out_shape=jax.ShapeDtypeStruct((b_, L, n), BF16),
        compiler_params=_cparams("arbitrary", "arbitrary", "arbitrary"),
        name="proj",
    )(x, scale, shift, w)


def _gla_kernel(*refs, reverse, finalize, nchunks):
    if finalize:
        (q_ref, k_ref, v_ref, ga_ref, gal_ref, wa_ref, ba_ref, s0_ref, tri_ref, op_ref, z_ref, nw_ref,
         o_ref, sfin_ref, s_scr) = refs
    else:
        (q_ref, k_ref, v_ref, ga_ref, gal_ref, wa_ref, ba_ref, s0_ref, tri_ref,
         o_ref, sfin_ref, s_scr) = refs
    C = GLA_CHUNK
    T = nchunks * C
    nt_dims = (((1,), (1,)), ((), ()))
    tn_dims = (((0,), (0,)), ((), ()))

    @pl.when(pl.program_id(1) == 0)
    def _():
        s_scr[...] = s0_ref[0]

    ga = ga_ref[0].astype(F32) + gal_ref[0].astype(F32)
    xg = _dot_hp(ga, wa_ref[...]) + ba_ref[...]
    g = (jnp.minimum(xg, 0.0) - jnp.log(1.0 + jnp.exp(-jnp.abs(xg)))) * (1.0 / GLA_GATE_NORM)

    tri = tri_ref[...]
    g1 = g.astype(BF16)
    r1 = g - g1.astype(F32)
    g2 = r1.astype(BF16)
    g3 = (r1 - g2.astype(F32)).astype(BF16)
    b = _dot(tri, g1) + _dot(tri, g2) + _dot(tri, g3)

    def chunk_row(r):
        return jnp.concatenate([jnp.broadcast_to(b[c * C + r:c * C + r + 1], (C, GLA_KW))
                                for c in range(nchunks)], axis=0)

    tot_row = 0 if reverse else C - 1
    bm = chunk_row(C // 2)
    bt = chunk_row(tot_row)
    ri = lax.broadcasted_iota(jnp.int32, (T, T), 0)
    ci = lax.broadcasted_iota(jnp.int32, (T, T), 1)
    same_chunk = (ri // C) == (ci // C)
    mask = same_chunk & ((ci >= ri) if reverse else (ci <= ri))
    lane = lax.broadcasted_iota(jnp.int32, (1, GLA_KW), 1)
    hmask = [(lane >= h * GLA_DK) & (lane < (h + 1) * GLA_DK) for h in range(GLA_HEADS)]

    q = q_ref[0].astype(F32) * (GLA_DK ** -0.5)
    k = k_ref[0].astype(F32)
    v = v_ref[0].astype(BF16)
    qa = q * jnp.exp(b - bm)
    kb = (k * jnp.exp(bm - b)).astype(BF16)
    qe = q * jnp.exp(b)
    kd = (k * jnp.exp(bt - b)).astype(BF16)

    intra = []
    for h in range(GLA_HEADS):
        qa_h = jnp.where(hmask[h], qa, 0.0).astype(BF16)
        att = lax.dot_general(qa_h, kb, nt_dims, preferred_element_type=F32)
        att = jnp.where(mask, att, 0.0).astype(BF16)
        intra.append(_dot(att, v[:, h * GLA_DV:(h + 1) * GLA_DV]))

    inter = [None] * nchunks
    order = range(nchunks - 1, -1, -1) if reverse else range(nchunks)
    for c in order:
        rows = slice(c * C, (c + 1) * C)
        st = s_scr[...]
        st_b = st.astype(BF16)
        dec = jnp.exp(b[c * C + tot_row:c * C + tot_row + 1])
        upd = jnp.zeros_like(st)
        outs = []
        for h in range(GLA_HEADS):
            qe_h = jnp.where(hmask[h], qe[rows], 0.0).astype(BF16)
            outs.append(lax.dot_general(qe_h, st_b, nt_dims, preferred_element_type=F32))
            u_h = lax.dot_general(v[rows, h * GLA_DV:(h + 1) * GLA_DV], kd[rows], tn_dims,
                                  preferred_element_type=F32)
            upd = upd + jnp.where(hmask[h], u_h, 0.0)
        s_scr[...] = dec * st + upd
        inter[c] = jnp.concatenate(outs, axis=1)
    o = jnp.concatenate(intra, axis=1) + jnp.concatenate(inter, axis=0)

    if finalize:
        o = o + op_ref[0]
        parts = []
        for h in range(GLA_HEADS):
            oh = o[:, h * GLA_DV:(h + 1) * GLA_DV]
            ms = jnp.mean(oh * oh, axis=-1, keepdims=True)
            parts.append(oh * lax.rsqrt(ms + EPS) * nw_ref[...])
        y = jnp.concatenate(parts, axis=1) * _silu(z_ref[0].astype(F32))
        o_ref[0] = y.astype(o_ref.dtype)
    else:
        o_ref[0] = o

    @pl.when(pl.program_id(1) == pl.num_programs(1) - 1)
    def _():
        sfin_ref[0] = s_scr[...]


def _gla_call(P, wa_pad, ba, s0, reverse, fin=None):
    b_, L, _ = P.shape
    T = min(GLA_BLOCK, L)
    nt = L // T
    if reverse:
        tmap = lambda t: nt - 1 - t
    else:
        tmap = lambda t: t
    ii = np.arange(T)
    same = (ii[:, None] // GLA_CHUNK) == (ii[None, :] // GLA_CHUNK)
    tri = same & ((ii[None, :] >= ii[:, None]) if reverse else (ii[None, :] <= ii[:, None]))
    tri = jnp.asarray(tri.astype(np.float32)).astype(BF16)

    def col(name, width):
        blk = COLS[name][0] // width
        return pl.BlockSpec((1, T, width), lambda b, t: (b, tmap(t), blk))

    in_specs = [col('g_q', GLA_KW), col('g_k', GLA_KW), col('g_v', GLA_VW),
                col('g_a', LANES), col('g_a_lo', LANES),
                pl.BlockSpec((LANES, GLA_KW), lambda b, t: (0, 0)),
                pl.BlockSpec((1, GLA_KW), lambda b, t: (0, 0)),
                pl.BlockSpec((1, GLA_DV, GLA_KW), lambda b, t: (b, 0, 0)),
                pl.BlockSpec((T, T), lambda b, t: (0, 0))]
    args = [P, P, P, P, P, wa_pad, ba, s0, tri]
    if fin is not None:
        o_prev, nw = fin
        in_specs += [pl.BlockSpec((1, T, GLA_VW), lambda b, t: (b, tmap(t), 0)),
                     col('g_z', GLA_VW),
                     pl.BlockSpec((1, GLA_DV), lambda b, t: (0, 0))]
        args += [o_prev, P, nw]
    out_dtype = BF16 if fin is not None else F32
    return pl.pallas_call(
        functools.partial(_gla_kernel, reverse=reverse, finalize=fin is not None, nchunks=T // GLA_CHUNK),
        grid=(b_, nt),
        in_specs=in_specs,
        out_specs=[pl.BlockSpec((1, T, GLA_VW), lambda b, t: (b, tmap(t), 0)),
                   pl.BlockSpec((1, GLA_DV, GLA_KW), lambda b, t: (b, 0, 0))],
        out_shape=[jax.ShapeDtypeStruct((b_, L, GLA_VW), out_dtype),
                   jax.ShapeDtypeStruct((b_, GLA_DV, GLA_KW), F32)],
        scratch_shapes=[pltpu.VMEM((GLA_DV, GLA_KW), F32)],
        compiler_params=_cparams("arbitrary", "arbitrary"),
        name="gla",
    )(*args)


def _norm_rope(a, gain, bd, cos, sin):
    sq = a * a
    sh, sl = _split(sq)
    ss = _dot(sh, bd) + _dot(sl, bd)
    an = a * lax.rsqrt(ss * (1.0 / ATTN_HD) + EPS) * gain
    w = a.shape[-1]
    lane = lax.broadcasted_iota(jnp.int32, an.shape, 1)
    partner = jnp.where(lane % 2 == 0, pltpu.roll(an, w - 1, 1), pltpu.roll(an, 1, 1))
    return an * cos + partner * sin


Q_SCALE = (ATTN_HD ** -0.5) * math.log2(math.e)
ATTN_QK_DEPTH = 4 * ATTN_HD
ATTN_TK_MAX = 640


def _split8(a):
    hi = a.astype(F8).astype(F32)
    lo = (a - hi).astype(F8)
    return hi.astype(BF16), lo.astype(BF16)


def _qkv_prep_kernel(q_ref, k_ref, v_ref, cq_ref, sq_ref, ck_ref, sk_ref, gq_ref, gk_ref,
                     bdq_ref, bdk_ref, eq_ref, ev_ref, qo_ref, ko_ref, vo_ref):
    nt = (((1,), (1,)), ((), ()))
    hd = ATTN_HD
    q = _norm_rope(q_ref[0].astype(F32), gq_ref[...], bdq_ref[...], cq_ref[...], sq_ref[...]) * Q_SCALE
    qh, ql = _split8(q)
    qt = lax.dot_general(eq_ref[...], jnp.concatenate([qh, ql], axis=1), nt, preferred_element_type=F32)
    for h in range(ATTN_HEADS):
        hi = qt[2 * hd * h:2 * hd * h + hd]
        lo = qt[2 * hd * h + hd:2 * hd * (h + 1)]
        qo_ref[0, ATTN_QK_DEPTH * h:ATTN_QK_DEPTH * (h + 1), :] = jnp.concatenate([hi, hi, lo, lo], axis=0).astype(F8)
    k = _norm_rope(k_ref[0].astype(F32), gk_ref[...], bdk_ref[...], ck_ref[...], sk_ref[...])
    kh, kl = _split8(k)
    for g in range(ATTN_KV_HEADS):
        cols = slice(g * hd, (g + 1) * hd)
        ko_ref[0, g] = jnp.concatenate([kh[:, cols], kl[:, cols], kh[:, cols], kl[:, cols]], axis=1).astype(F8)
    vb = v_ref[0].astype(BF16)
    vo_ref[0] = lax.dot_general(ev_ref[...], vb, nt, preferred_element_type=F32).astype(BF16)


def _qkv_prep_call(P, cos_q, sin_q, cos_k, sin_k, gq, gk, bdq, bdk, eye_q, eye_v):
    b_, L, _ = P.shape
    T = min(1024, L)

    def col(name, width):
        blk = COLS[name][0] // width
        return pl.BlockSpec((1, T, width), lambda b, t: (b, t, blk))

    tab = lambda w: pl.BlockSpec((T, w), lambda b, t: (t, 0))
    const = lambda r, w: pl.BlockSpec((r, w), lambda b, t: (0, 0))
    return pl.pallas_call(
        _qkv_prep_kernel,
        grid=(b_, L // T),
        in_specs=[col('a_q', ATTN_QW), col('a_k', ATTN_KVW), col('a_v', ATTN_KVW),
                  tab(ATTN_QW), tab(ATTN_QW), tab(ATTN_KVW), tab(ATTN_KVW),
                  const(1, ATTN_QW), const(1, ATTN_KVW),
                  const(ATTN_QW, ATTN_QW), const(ATTN_KVW, ATTN_KVW),
                  const(2 * ATTN_QW, 2 * ATTN_QW), const(ATTN_KVW, ATTN_KVW)],
        out_specs=[pl.BlockSpec((1, ATTN_HEADS * ATTN_QK_DEPTH, T), lambda b, t: (b, 0, t)),
                   pl.BlockSpec((1, ATTN_KV_HEADS, T, ATTN_QK_DEPTH), lambda b, t: (b, 0, t, 0)),
                   pl.BlockSpec((1, ATTN_KVW, T), lambda b, t: (b, 0, t))],
        out_shape=[jax.ShapeDtypeStruct((b_, ATTN_HEADS * ATTN_QK_DEPTH, L), F8),
                   jax.ShapeDtypeStruct((b_, ATTN_KV_HEADS, L, ATTN_QK_DEPTH), F8),
                   jax.ShapeDtypeStruct((b_, ATTN_KVW, L), BF16)],
        compiler_params=_cparams("arbitrary", "arbitrary"),
        name="qkv_prep",
    )(P, P, P, cos_q, sin_q, cos_k, sin_k, gq, gk, bdq, bdk, eye_q, eye_v)


ATTN_SUM_ROWS = 16
ATTN_UNROLL = 2
ATTN_TQ = 512


def _attn_kernel(qt_ref, k_ref, vt_ref, z_ref, e_ref, o_ref, m_scr, a_scr, acc_scr, s_scr, p_scr,
                 *, tk, nk):
    m_scr[...] = jnp.full(m_scr.shape, -jnp.inf, F32)
    acc_scr[...] = jnp.zeros(acc_scr.shape, F32)
    ones = jnp.ones((ATTN_SUM_ROWS, tk), BF16)

    def scores(j, slot):
        start = j * tk if isinstance(j, int) else pl.multiple_of(j * tk, tk)
        kj = k_ref[0, 0, pl.ds(start, tk), :]
        for h in range(ATTN_GROUP):
            s_scr[slot, h] = _dot(kj, qt_ref[0, h * ATTN_QK_DEPTH:(h + 1) * ATTN_QK_DEPTH, :])

    scores(0, 0)

    def step(j, slot):
        scores(min(j + 1, nk - 1) if isinstance(j, int) else jnp.minimum(j + 1, nk - 1), 1 - slot)
        start = j * tk if isinstance(j, int) else pl.multiple_of(j * tk, tk)
        vte = jnp.concatenate([vt_ref[0, :, pl.ds(start, tk)], ones], axis=0)
        for h in range(ATTN_GROUP):
            s = s_scr[slot, h]
            m_old = m_scr[h]
            m_new = jnp.maximum(m_old, jnp.max(s, axis=0, keepdims=True))
            a_scr[h] = jnp.exp2(m_old - m_new)
            m_scr[h] = m_new
            p_scr[h] = jnp.exp2(s - m_new).astype(BF16)
        for h in range(ATTN_GROUP):
            acc_scr[h] = a_scr[h] * acc_scr[h] + _dot(vte, p_scr[h])

    def trip(i, carry):
        for u in range(ATTN_UNROLL):
            step(ATTN_UNROLL * i + u, u % 2)
        return carry

    ntrips = nk // ATTN_UNROLL
    lax.fori_loop(0, ntrips, trip, 0)
    for j in range(ntrips * ATTN_UNROLL, nk):
        step(j, j % 2)
    tn = (((0,), (0,)), ((), ()))
    out = None
    for h in range(ATTN_GROUP):
        acc = acc_scr[h]
        oh, ol = _split(acc[:ATTN_HD] / acc[ATTN_HD:ATTN_HD + 1])
        part = (lax.dot_general(oh, e_ref[h], tn, preferred_element_type=F32)
                + lax.dot_general(ol, e_ref[h], tn, preferred_element_type=F32))
        out = part if out is None else out + part
    o_ref[0] = (out * _silu(z_ref[0].astype(F32))).astype(o_ref.dtype)


def _attn_call(qt, k, vt, P, e_heads):
    b_, _, L = qt.shape
    Lk = k.shape[2]
    tq = min(ATTN_TQ, L)
    tk = max(t for t in range(LANES, ATTN_TK_MAX + 1, LANES) if Lk % t == 0)
    gw = ATTN_GROUP * ATTN_HD
    zblk = COLS['a_z'][0] // gw
    return pl.pallas_call(
        functools.partial(_attn_kernel, tk=tk, nk=Lk // tk),
        grid=(b_, ATTN_KV_HEADS, L // tq),
        in_specs=[pl.BlockSpec((1, ATTN_GROUP * ATTN_QK_DEPTH, tq), lambda b, g, i: (b, g, i)),
                  pl.BlockSpec((1, 1, Lk, ATTN_QK_DEPTH), lambda b, g, i: (b, g, 0, 0)),
                  pl.BlockSpec((1, ATTN_HD, Lk), lambda b, g, i: (b, g, 0)),
                  pl.BlockSpec((1, tq, gw), lambda b, g, i: (b, i, zblk + g)),
                  pl.BlockSpec((ATTN_GROUP, ATTN_HD, gw), lambda b, g, i: (0, 0, 0))],
        out_specs=pl.BlockSpec((1, tq, gw), lambda b, g, i: (b, i, g)),
        out_shape=jax.ShapeDtypeStruct((b_, L, ATTN_QW), BF16),
        scratch_shapes=[pltpu.VMEM((ATTN_GROUP, 1, tq), F32), pltpu.VMEM((ATTN_GROUP, 1, tq), F32),
                        pltpu.VMEM((ATTN_GROUP, ATTN_HD + ATTN_SUM_ROWS, tq), F32),
                        pltpu.VMEM((2, ATTN_GROUP, tk, tq), F32), pltpu.VMEM((ATTN_GROUP, tk, tq), BF16)],
        compiler_params=_cparams("arbitrary", "arbitrary", "arbitrary"),
        name="attn",
    )(qt, k, vt, P, e_heads)


def _short_conv_kernel(u_ref, up_ref, un_ref, w_ref, b_ref, v_ref, x1_ref, x2_ref):
    t = pl.program_id(1)
    nt = pl.num_programs(1)
    u = u_ref[0].astype(F32)
    T = u.shape[0]
    hr = up_ref.shape[1]
    prev_row = jnp.where(t > 0, up_ref[0, hr - 1:hr, :].astype(F32), 0.0)
    next_row = jnp.where(t < nt - 1, un_ref[0, 0:1, :].astype(F32), 0.0)
    row = lax.broadcasted_iota(jnp.int32, u.shape, 0)
    u_m1 = jnp.where(row == 0, prev_row, pltpu.roll(u, 1, 0))
    u_p1 = jnp.where(row == T - 1, next_row, pltpu.roll(u, T - 1, 0))
    w = w_ref[...]
    out = b_ref[...] + u_m1 * w[0:1] + u * w[1:2] + u_p1 * w[2:3]
    v_ref[0] = out[:, 0:HY_W]
    x1_ref[0] = out[:, HY_W:2 * HY_W]
    x2_ref[0] = out[:, 2 * HY_W:3 * HY_W]


def _short_conv_call(P, w, bias):
    b_, L, _ = P.shape
    T = min(1024, L)
    cw = (HY_ORDER + 1) * HY_W
    blk = COLS['y_u'][0] // cw
    hr = 16
    hb = T // hr
    nh = L // hr
    out_spec = pl.BlockSpec((1, T, HY_W), lambda b, t: (b, t, 0))
    shp = jax.ShapeDtypeStruct((b_, L, HY_W), F32)
    return pl.pallas_call(
        _short_conv_kernel,
        grid=(b_, L // T),
        in_specs=[pl.BlockSpec((1, T, cw), lambda b, t: (b, t, blk)),
                  pl.BlockSpec((1, hr, cw), lambda b, t: (b, jnp.maximum(t * hb - 1, 0), blk)),
                  pl.BlockSpec((1, hr, cw), lambda b, t: (b, jnp.minimum((t + 1) * hb, nh - 1), blk)),
                  pl.BlockSpec((HY_SHORT, cw), lambda b, t: (0, 0)),
                  pl.BlockSpec((1, cw), lambda b, t: (0, 0))],
        out_specs=[out_spec, out_spec, out_spec],
        out_shape=[shp, shp, shp],
        compiler_params=_cparams("arbitrary", "arbitrary"),
        name="short_conv",
    )(P, P, P, w, bias)


FFT_ROWS = 8


def _filter_kernel(emb_ref, w1_ref, b1_ref, f1_ref, w2_ref, b2_ref, f2_ref, w3_ref, dl_ref,
                   hf_ref, hb_ref, sum_ref):
    t = pl.program_id(0)
    emb = emb_ref[...]
    h = jnp.sin(f1_ref[...] * (_dot_hp(emb, w1_ref[...]) + b1_ref[...]))
    h = jnp.sin(f2_ref[...] * (_dot_hp(h, w2_ref[...]) + b2_ref[...]))
    hh, hl = _split(h)
    tt = emb[:, 0:1]

    @pl.when(t == 0)
    def _():
        sum_ref[...] = jnp.zeros_like(sum_ref)

    lag0 = (lax.broadcasted_iota(jnp.int32, (h.shape[0], HY_W), 0) + t * h.shape[0]) == 0
    for o in range(HY_ORDER):
        for d, out_ref in enumerate((hf_ref, hb_ref)):
            cols = slice((2 * o + d) * HY_W, (2 * o + d + 1) * HY_W)
            wh, wl = _split(w3_ref[:, cols])
            f = _dot(hh, wh) + _dot(hl, wh) + _dot(hh, wl)
            f = f * (jnp.exp(-tt * dl_ref[:, cols]) + HY_MOD_SHIFT)
            sum_ref[:, cols] += jnp.broadcast_to(jnp.sum(jnp.abs(f), axis=0, keepdims=True), (8, HY_W))
            if d == 1:
                f = jnp.where(lag0, 0.0, f)
            out_ref[o] = pltpu.einshape("(an)c->a(nc)", f, a=FFT_ROWS)


def _filter_call(emb, w1, b1, f1, w2, b2, f2, w3, deltas, n1, n2):
    L = emb.shape[0]
    TL = FFT_ROWS * n2
    const = lambda r, w: pl.BlockSpec((r, w), lambda t: (0, 0))
    hspec = pl.BlockSpec((HY_ORDER, FFT_ROWS, n2 * HY_W), lambda t: (0, t, 0))
    hshape = jax.ShapeDtypeStruct((HY_ORDER, n1 // 2, n2 * HY_W), F32)
    return pl.pallas_call(
        _filter_kernel,
        grid=(L // TL,),
        in_specs=[pl.BlockSpec((TL, LANES), lambda t: (t, 0)),
                  const(LANES, HY_FFN), const(1, HY_FFN), const(1, HY_FFN),
                  const(HY_FFN, HY_FFN), const(1, HY_FFN), const(1, HY_FFN),
                  const(HY_FFN, N_FILT), const(1, N_FILT)],
        out_specs=[hspec, hspec, pl.BlockSpec((8, N_FILT), lambda t: (0, 0))],
        out_shape=[hshape, hshape, jax.ShapeDtypeStruct((8, N_FILT), F32)],
        compiler_params=_cparams("arbitrary"),
        name="hy_filter",
    )(emb, w1, b1, f1, w2, b2, f2, w3, deltas)


def _fft_sizes(n):
    lg = int(round(math.log2(n)))
    assert 1 << lg == n
    n1 = 1 << ((lg + 1) // 2)
    return n1, n // n1


def _bf16_pair(a):
    a = np.asarray(a, np.float32)
    hi = jnp.asarray(a, F32).astype(BF16)
    lo = (jnp.asarray(a, F32) - hi.astype(F32)).astype(BF16)
    return hi, lo


def _fft_tables(n1, n2):
    n = n1 * n2
    h1 = n1 // 2
    k1 = np.arange(n1)[:, None]
    a = 2.0 * np.pi * ((k1 * np.arange(h1)[None, :]) % n1) / n1
    c, s = np.cos(a), np.sin(a)
    fa = np.block([[c, s], [-s, c]])
    fa_real = np.concatenate([c, -s], axis=0)
    fd = np.block([[c.T, -s.T], [s.T, c.T]])
    k2 = np.arange(n2)[:, None]
    b = 2.0 * np.pi * ((k2 * np.arange(n2)[None, :]) % n2) / n2
    cb, sb = np.cos(b), np.sin(b)
    fb = np.block([[cb, sb], [-sb, cb]])
    fc = np.block([[cb, -sb], [sb, cb]])
    kn = (jnp.arange(n1, dtype=jnp.int32)[:, None] * jnp.arange(n2, dtype=jnp.int32)[None, :]) % n
    tw = kn.astype(F32) * (2.0 * math.pi / n)
    twr = jnp.broadcast_to(jnp.cos(tw)[:, :, None], (n1, n2, LANES))
    twi = jnp.broadcast_to(-jnp.sin(tw)[:, :, None], (n1, n2, LANES))
    return dict(fa=_bf16_pair(fa), fa_real=_bf16_pair(fa_real), fd=_bf16_pair(fd),
                fb=_bf16_pair(fb), fc=_bf16_pair(fc), twr=twr, twi=twi)


def _fa_kernel(x_ref, mh_ref, ml_ref, o_ref, xh_scr, *, packed, part_axis):
    @pl.when(pl.program_id(part_axis) == 0)
    def _():
        if packed:
            x = x_ref[...]
            x = x.reshape(x.shape[0] * x.shape[1], x.shape[2], x.shape[3])
            x = pltpu.einshape("rnc->r(nc)", x)
        else:
            x = x_ref[0]
        xh_scr[...] = x.astype(BF16)

    xh = xh_scr[...]
    o_ref[0] = _dot(mh_ref[...], xh) + _dot(ml_ref[...], xh)


def _fa_call(x4, mats, n1):
    mh, ml = mats
    _, h1, n2, wd = x4.shape
    wt = FFT_ROWS * wd
    return pl.pallas_call(
        functools.partial(_fa_kernel, packed=True, part_axis=1),
        grid=(n2 // FFT_ROWS, 2),
        in_specs=[pl.BlockSpec((2, h1, FFT_ROWS, wd), lambda j, p: (0, 0, j, 0)),
                  pl.BlockSpec((n1, n1), lambda j, p: (p, 0)),
                  pl.BlockSpec((n1, n1), lambda j, p: (p, 0))],
        out_specs=pl.BlockSpec((1, n1, wt), lambda j, p: (p, 0, j)),
        out_shape=jax.ShapeDtypeStruct((2, n1, n2 * wd), F32),
        scratch_shapes=[pltpu.VMEM((n1, wt), BF16)],
        compiler_params=_cparams("arbitrary", "arbitrary"),
        name="fft_a",
    )(x4, mh, ml)


def _fa_real_call(x, mats, n1):
    mh, ml = mats
    ng, h1, lanes = x.shape
    wt = min(FFT_ROWS * HY_W, lanes)
    out = pl.pallas_call(
        functools.partial(_fa_kernel, packed=False, part_axis=2),
        grid=(ng, lanes // wt, 2),
        in_specs=[pl.BlockSpec((1, h1, wt), lambda g, j, p: (g, 0, j)),
                  pl.BlockSpec((n1, h1), lambda g, j, p: (p, 0)),
                  pl.BlockSpec((n1, h1), lambda g, j, p: (p, 0))],
        out_specs=pl.BlockSpec((1, n1, wt), lambda g, j, p: (2 * g + p, 0, j)),
        out_shape=jax.ShapeDtypeStruct((2 * ng, n1, lanes), F32),
        scratch_shapes=[pltpu.VMEM((h1, wt), BF16)],
        compiler_params=_cparams("arbitrary", "arbitrary", "arbitrary"),
        name="fft_a_real",
    )(x, mh, ml)
    return out.reshape(ng, 2, n1, lanes)


def _lane_tile(a, width):
    return jnp.concatenate([a] * (width // a.shape[-1]), axis=-1)


def _fb_filter_kernel(tf_ref, tb_ref, twr_ref, twi_ref, fh_ref, fl_ref, sf_ref, sb_ref, g_ref,
                      tf4_scr, tb4_scr, *, n2, scale):
    w = g_ref.shape[-1]
    tf4_scr[...] = pltpu.einshape("pk(nc)->pknc", tf_ref[0], n=n2)
    tb4_scr[...] = pltpu.einshape("pk(nc)->pknc", tb_ref[0], n=n2)
    inv = jnp.concatenate([scale / sf_ref[0:1, :], scale / sb_ref[0:1, :]], axis=1)

    def body(kk, carry):
        twr = _lane_tile(twr_ref[kk], 2 * w)
        twi = _lane_tile(twi_ref[kk], 2 * w)
        tr = jnp.concatenate([tf4_scr[0, kk], tb4_scr[0, kk]], axis=1)
        ti = jnp.concatenate([tf4_scr[1, kk], tb4_scr[1, kk]], axis=1)
        p = jnp.concatenate([tr * twr - ti * twi, tr * twi + ti * twr], axis=0)
        z = _dot_mat(fh_ref[...], fl_ref[...], p) * inv
        g_ref[0, 0, kk] = z[:n2, :w] + z[:n2, w:]
        g_ref[0, 1, kk] = z[n2:, :w] - z[n2:, w:]
        return carry

    lax.fori_loop(0, FFT_ROWS, body, 0, unroll=2)


def _fb_filter_call(thf, thb, sums, tabs, n1, n2):
    fh, fl = tabs['fb']
    lanes = n2 * HY_W
    tspec = pl.BlockSpec((1, 2, FFT_ROWS, lanes), lambda g, i: (g, 0, i, 0))
    twspec = pl.BlockSpec((FFT_ROWS, n2, LANES), lambda g, i: (i, 0, 0))
    sspec = lambda d: pl.BlockSpec((8, HY_W), lambda g, i: (0, 2 * g + d))
    return pl.pallas_call(
        functools.partial(_fb_filter_kernel, n2=n2, scale=1.0 / (n1 * n2)),
        grid=(HY_ORDER, n1 // FFT_ROWS),
        in_specs=[tspec, tspec, twspec, twspec,
                  pl.BlockSpec(fh.shape, lambda g, i: (0, 0)), pl.BlockSpec(fl.shape, lambda g, i: (0, 0)),
                  sspec(0), sspec(1)],
        out_specs=pl.BlockSpec((1, 2, FFT_ROWS, n2, HY_W), lambda g, i: (g, 0, i, 0, 0)),
        out_shape=jax.ShapeDtypeStruct((HY_ORDER, 2, n1, n2, HY_W), F32),
        scratch_shapes=[pltpu.VMEM((2, FFT_ROWS, n2, HY_W), F32), pltpu.VMEM((2, FFT_ROWS, n2, HY_W), F32)],
        compiler_params=_cparams("arbitrary", "arbitrary"),
        name="fft_b_filter",
    )(thf, thb, tabs['twr'], tabs['twi'], fh, fl, sums, sums)


def _fb_kernel(t_ref, twr_ref, twi_ref, fh_ref, fl_ref, ch_ref, cl_ref, g_ref, o_ref, t4_scr, *, n2):
    w = g_ref.shape[-1]
    t4_scr[...] = pltpu.einshape("pk(nc)->pknc", t_ref[...], n=n2)

    def body(kk, carry):
        tr = t4_scr[0, kk]
        ti = t4_scr[1, kk]
        twr = _lane_tile(twr_ref[kk], w)
        twi = _lane_tile(twi_ref[kk], w)
        p = jnp.concatenate([tr * twr - ti * twi, tr * twi + ti * twr], axis=0)
        z = _dot_mat(fh_ref[...], fl_ref[...], p)
        zr, zi = z[:n2], z[n2:]
        gr = g_ref[0, 0, kk]
        gi = g_ref[0, 1, kk]
        y = jnp.concatenate([zr * gr - zi * gi, zr * gi + zi * gr], axis=0)
        v = _dot_mat(ch_ref[...], cl_ref[...], y)
        vr, vi = v[:n2], v[n2:]
        t4_scr[0, kk] = vr * twr + vi * twi
        t4_scr[1, kk] = vi * twr - vr * twi
        return carry

    lax.fori_loop(0, FFT_ROWS, body, 0, unroll=4)
    o_ref[...] = pltpu.einshape("pknc->pk(nc)", t4_scr[...])


def _fb_call(t, g, order, tabs, n1, n2):
    fh, fl = tabs['fb']
    ch, cl = tabs['fc']
    lanes = n2 * HY_W
    mat = lambda m: pl.BlockSpec(m.shape, lambda i: (0, 0))
    dat = pl.BlockSpec((2, FFT_ROWS, lanes), lambda i: (0, i, 0))
    twspec = pl.BlockSpec((FFT_ROWS, n2, LANES), lambda i: (i, 0, 0))
    return pl.pallas_call(
        functools.partial(_fb_kernel, n2=n2),
        grid=(n1 // FFT_ROWS,),
        in_specs=[dat, twspec, twspec, mat(fh), mat(fl), mat(ch), mat(cl),
                  pl.BlockSpec((1, 2, FFT_ROWS, n2, HY_W), lambda i: (order, 0, i, 0, 0))],
        out_specs=dat,
        out_shape=jax.ShapeDtypeStruct((2, n1, lanes), F32),
        scratch_shapes=[pltpu.VMEM((2, FFT_ROWS, n2, HY_W), F32)],
        compiler_params=_cparams("arbitrary"),
        name="fft_b",
    )(t, tabs['twr'], tabs['twi'], fh, fl, ch, cl, g)


def _fd_kernel(u_ref, mh_ref, ml_ref, z_ref, x_ref, sk_ref, o_ref):
    u = u_ref[...]
    u = u.reshape(u.shape[0] * u.shape[1], u.shape[2])
    y = _dot_mat(mh_ref[...], ml_ref[...], u)
    y = pltpu.einshape("r(nc)->rnc", y, n=FFT_ROWS)
    o_ref[0] = x_ref[0] * (y + sk_ref[...] * z_ref[0])


def _fd_call(u, mats, z4, x4, skip_row, n1):
    mh, ml = mats
    b_, h1, n2, wd = z4.shape
    wt = FFT_ROWS * wd
    dat = pl.BlockSpec((1, h1, FFT_ROWS, wd), lambda j, p: (p, 0, j, 0))
    return pl.pallas_call(
        _fd_kernel,
        grid=(n2 // FFT_ROWS, b_),
        in_specs=[pl.BlockSpec((2, n1, wt), lambda j, p: (0, 0, j)),
                  pl.BlockSpec((h1, 2 * n1), lambda j, p: (p, 0)),
                  pl.BlockSpec((h1, 2 * n1), lambda j, p: (p, 0)),
                  dat, dat, pl.BlockSpec((1, wd), lambda j, p: (0, 0))],
        out_specs=dat,
        out_shape=jax.ShapeDtypeStruct(z4.shape, F32),
        compiler_params=pltpu.CompilerParams(dimension_semantics=("arbitrary", "arbitrary"),
                                             vmem_limit_bytes=FFT_D_VMEM_LIMIT),
        name="fft_d",
    )(u, mh, ml, z4, x4, skip_row)


def _hyena_filter_spectrum(L, tabs, n1, n2, fp):
    f32 = F32
    t = jnp.linspace(0.0, 1.0, L, dtype=f32)[:, None]
    w = 2.0 * math.pi * jnp.arange(L, dtype=f32)[:, None] / L
    fr = jnp.linspace(1e-4, HY_BANDS - 1, HY_BANDS, dtype=f32)[None]
    emb = jnp.concatenate([t, jnp.cos(fr * w), -jnp.sin(fr * w)], axis=-1)
    emb = jnp.pad(emb, ((0, 0), (0, LANES - HY_EMB)))
    deltas = jnp.abs(jnp.linspace(math.log(HY_DECAY_TARGET) / HY_DECAY_SHORT_PCT,
                                  math.log(HY_DECAY_TARGET) / HY_DECAY_LONG_PCT, N_FILT, dtype=f32))[None]
    f1_w, f1_b, f1_freq, f2_w, f2_b, f2_freq, f3_w = fp
    w1 = jnp.pad(f1_w, ((0, LANES - HY_EMB), (0, 0)))
    hf, hb, sums = _filter_call(emb, w1, f1_b[None], f1_freq[None], f2_w, f2_b[None],
                                f2_freq[None], f3_w, deltas, n1, n2)
    thf = _fa_real_call(hf, tabs['fa_real'], n1)
    thb = _fa_real_call(hb, tabs['fa_real'], n1)
    return _fb_filter_call(thf, thb, sums, tabs, n1, n2)


def _hyena_conv(z, xg, g, order, skip, tabs, n1, n2):
    b_, L, wd = z.shape
    assert b_ == 2, "the two batch rows ride as the real and imaginary parts of one transform"
    z4 = z.reshape(b_, n1 // 2, n2, wd)
    x4 = xg.reshape(b_, n1 // 2, n2, wd)
    t = _fa_call(z4, tabs['fa'], n1)
    u = _fb_call(t, g, order, tabs, n1, n2)
    out = _fd_call(u, tabs['fd'], z4, x4, skip[None, :], n1)
    return out.reshape(b_, L, wd)


def _merge_kernel(yg_ref, ya_ref, yh_ref, yz_ref, mg_ref, ma_ref, mh_ref, x_ref, gate_ref,
                  wg_ref, wa_ref, wh_ref, wo_ref, fn_ref, o_ref, *, final):
    yh = (yh_ref[0] * _silu(yz_ref[0].astype(F32))).astype(BF16)
    y = (_sigmoid(mg_ref[0].astype(F32)) * _dot(yg_ref[0], wg_ref[...])
         + _sigmoid(ma_ref[0].astype(F32)) * _dot(ya_ref[0], wa_ref[...])
         + _sigmoid(mh_ref[0].astype(F32)) * _dot(yh, wh_ref[...]))
    out = _dot(y.astype(BF16), wo_ref[...])
    xn = x_ref[0] + gate_ref[0] * out
    if final:
        ms = jnp.mean(xn * xn, axis=-1, keepdims=True)
        xn = xn * lax.rsqrt(ms + EPS) * fn_ref[...]
    o_ref[0] = xn


def _merge_call(y_gla, y_attn, y_hy, P, x, gate, wg, wa, wh, wo, fnorm, final):
    b_, L, d = x.shape
    T = min(1024, L)
    row = lambda w, blk=0: pl.BlockSpec((1, T, w), lambda b, t: (b, t, blk))
    const = lambda r, w: pl.BlockSpec((r, w), lambda b, t: (0, 0))
    mblk = COLS['m'][0] // d
    return pl.pallas_call(
        functools.partial(_merge_kernel, final=final),
        grid=(b_, L // T),
        in_specs=[row(BRANCH_W), row(BRANCH_W), row(BRANCH_W), row(HY_W, COLS['y_z'][0] // HY_W),
                  row(d, mblk), row(d, mblk + 1), row(d, mblk + 2), row(d),
                  pl.BlockSpec((1, 1, d), lambda b, t: (b, 0, 0)),
                  const(BRANCH_W, d), const(BRANCH_W, d), const(BRANCH_W, d), const(d, d), const(1, d)],
        out_specs=row(d),
        out_shape=jax.ShapeDtypeStruct((b_, L, d), F32),
        compiler_params=_cparams("arbitrary", "arbitrary"),
        name="merge",
    )(y_gla, y_attn, y_hy, P, P, P, P, x, gate, wg, wa, wh, wo, fnorm)


def _pack_w_in(w):
    parts = dict(zip(('g_q', 'g_k', 'g_v', 'g_z', 'g_af', 'g_ab', 'a_q', 'a_k', 'a_v', 'a_z', 'y_u', 'y_z', 'm'),
                     jnp.split(w, [int(i) for i in np.cumsum(SPLITS)[:-1]], axis=-1)))
    parts['g_a'] = jnp.pad(jnp.concatenate([parts['g_af'], parts['g_ab']], axis=-1),
                           ((0, 0), (0, LANES - 2 * GLA_RANK)))
    parts['g_a_lo'] = jnp.zeros_like(parts['g_a'])
    order = sorted(COLS, key=lambda n: COLS[n][0])
    return jnp.concatenate([parts[n] for n in order], axis=-1).astype(BF16)


def _rope_tables(L, heads):
    t = jnp.arange(L)
    row = (t // GRID_W).astype(F32)
    colp = (t % GRID_W).astype(F32)
    half = ATTN_HD // 2
    inv = ROPE_THETA ** (-jnp.arange(0, half, 2, dtype=F32) / half)
    ang = jnp.concatenate([row[:, None] * inv, colp[:, None] * inv], axis=-1)
    cos = jnp.repeat(jnp.cos(ang), 2, axis=-1)
    sin = jnp.stack([-jnp.sin(ang), jnp.sin(ang)], axis=-1).reshape(L, ATTN_HD)
    return jnp.tile(cos, (1, heads)), jnp.tile(sin, (1, heads))


def _identity_rope(L, heads):
    return jnp.ones((L, heads * ATTN_HD), F32), jnp.zeros((L, heads * ATTN_HD), F32)


def _block_diag_ones(width):
    i = np.arange(width) // ATTN_HD
    return jnp.asarray((i[:, None] == i[None, :]).astype(np.float32)).astype(BF16)


def kernel(x, c, ctx, c_ctx, w_ada, b_ada, w_in, gla_wa_f, gla_ba_f, gla_wa_b, gla_ba_b, gla_norm,
           attn_qnorm, attn_knorm, hy_conv_w, hy_conv_b, hy_f1_w, hy_f1_b, hy_f1_freq, hy_f2_w,
           hy_f2_b, hy_f2_freq, hy_f3_w, hy_skip, w_br_gla, w_br_attn, w_br_hy, w_out, final_norm):
    b_, L, d = x.shape
    Lc = ctx.shape[1]
    depth = w_ada.shape[0]

    cos_q, sin_q = _rope_tables(L, ATTN_HEADS)
    cos_k, sin_k = cos_q[:, :ATTN_KVW], sin_q[:, :ATTN_KVW]
    cos_qc, sin_qc = _identity_rope(Lc, ATTN_HEADS)
    cos_kc, sin_kc = cos_qc[:, :ATTN_KVW], sin_qc[:, :ATTN_KVW]
    bdq, bdk = _block_diag_ones(ATTN_QW), _block_diag_ones(ATTN_KVW)
    eye_v = jnp.eye(ATTN_KVW, dtype=BF16)
    rr = np.arange(2 * ATTN_QW)
    src = (rr // ATTN_HD % 2) * ATTN_QW + (rr // (2 * ATTN_HD)) * ATTN_HD + rr % ATTN_HD
    eye_q = jnp.asarray((src[:, None] == np.arange(2 * ATTN_QW)[None, :]).astype(np.float32)).astype(BF16)
    gw = ATTN_GROUP * ATTN_HD
    e_heads = jnp.stack([jnp.eye(ATTN_HD, gw, k=h * ATTN_HD, dtype=BF16) for h in range(ATTN_GROUP)])
    n1, n2 = _fft_sizes(2 * L)
    n1c, n2c = _fft_sizes(2 * Lc)
    tabs = _fft_tables(n1, n2)
    tabs_c = _fft_tables(n1c, n2c)
    zero_state = jnp.zeros((b_, GLA_DV, GLA_KW), F32)

    cmat = jnp.concatenate([c, c_ctx[None], jnp.zeros((8 - b_ - 1, d), F32)], axis=0)
    mods = _ada_call(cmat, w_ada, b_ada)

    for l in range(depth):
        need_ctx = l < depth - 1
        shift, scale, gate = [m[:b_, None, :] for m in jnp.split(mods[l], 3, axis=-1)]
        shift_c, scale_c, gate_c = [jnp.broadcast_to(m[b_:b_ + 1, None, :], (b_, 1, d))
                                    for m in jnp.split(mods[l], 3, axis=-1)]
        w_packed = _pack_w_in(w_in[l])
        P = _proj_call(x, scale, shift, w_packed)
        Pc = _proj_call(ctx, scale_c, shift_c, w_packed)

        rk = GLA_RANK
        wa_f = jnp.zeros((LANES, GLA_KW), F32).at[0:rk].set(gla_wa_f[l])
        wa_b = jnp.zeros((LANES, GLA_KW), F32).at[rk:2 * rk].set(gla_wa_b[l])
        ba_f, ba_b = gla_ba_f[l][None], gla_ba_b[l][None]
        nw = gla_norm[l][None]
        oc_f, sc_f = _gla_call(Pc, wa_f, ba_f, zero_state, reverse=False)
        yc_gla, sc_b = _gla_call(Pc, wa_b, ba_b, zero_state, reverse=True, fin=(oc_f, nw))
        o_f, _ = _gla_call(P, wa_f, ba_f, sc_f, reverse=False)
        y_gla, _ = _gla_call(P, wa_b, ba_b, sc_b, reverse=True, fin=(o_f, nw))

        gq = jnp.tile(attn_qnorm[l], ATTN_HEADS)[None]
        gk = jnp.tile(attn_knorm[l], ATTN_KV_HEADS)[None]
        q_a, k_a, v_a = _qkv_prep_call(P, cos_q, sin_q, cos_k, sin_k, gq, gk, bdq, bdk, eye_q, eye_v)
        qc_a, kc_a, vc_a = _qkv_prep_call(Pc, cos_qc, sin_qc, cos_kc, sin_kc, gq, gk, bdq, bdk,
                                          eye_q, eye_v)
        y_attn = _attn_call(q_a, jnp.concatenate([k_a, kc_a], axis=2),
                            jnp.concatenate([v_a, vc_a], axis=2), P, e_heads)

        fp = (hy_f1_w[l], hy_f1_b[l], hy_f1_freq[l], hy_f2_w[l], hy_f2_b[l], hy_f2_freq[l], hy_f3_w[l])
        g_spec = _hyena_filter_spectrum(L, tabs, n1, n2, fp)
        v0, x1, x2 = _short_conv_call(P, hy_conv_w[l], hy_conv_b[l][None])
        z1 = _hyena_conv(v0, x1, g_spec, 0, hy_skip[l, 0], tabs, n1, n2)
        y_hy = _hyena_conv(z1, x2, g_spec, 1, hy_skip[l, 1], tabs, n1, n2)

        wg, wa, wh, wo = (w_br_gla[l].astype(BF16), w_br_attn[l].astype(BF16),
                          w_br_hy[l].astype(BF16), w_out[l].astype(BF16))
        fn = final_norm[None]
        x_new = _merge_call(y_gla, y_attn, y_hy, P, x, gate, wg, wa, wh, wo, fn, final=not need_ctx)

        if need_ctx:
            yc_attn = _attn_call(qc_a, kc_a, vc_a, Pc, e_heads)
            gc_spec = _hyena_filter_spectrum(Lc, tabs_c, n1c, n2c, fp)
            vc0, xc1, xc2 = _short_conv_call(Pc, hy_conv_w[l], hy_conv_b[l][None])
            zc1 = _hyena_conv(vc0, xc1, gc_spec, 0, hy_skip[l, 0], tabs_c, n1c, n2c)
            yc_hy = _hyena_conv(zc1, xc2, gc_spec, 1, hy_skip[l, 1], tabs_c, n1c, n2c)
            ctx = _merge_call(yc_gla, yc_attn, yc_hy, Pc, ctx, gate_c, wg, wa, wh, wo, fn, final=False)
        x = x_new

    return x
```

```python
import functools
import math

import numpy as np
import jax
import jax.numpy as jnp
from jax import lax
from jax.experimental import pallas as pl
from jax.experimental.pallas import tpu as pltpu

F32 = jnp.float32
BF16 = jnp.bfloat16
F8 = jnp.float8_e4m3fn

D_MODEL = 1024
GRID_W = 64
BRANCH_W = D_MODEL // 2
N_BRANCH = 3
EPS = 1e-6
GLA_HEADS = 4
GLA_DV = BRANCH_W // GLA_HEADS
GLA_DK = GLA_DV // 2
GLA_KW = GLA_HEADS * GLA_DK
GLA_VW = GLA_HEADS * GLA_DV
GLA_RANK = 16
GLA_GATE_NORM = 16.0
GLA_CHUNK = 64
GLA_BLOCK = 256
ATTN_HD = 64
ATTN_HEADS = BRANCH_W // ATTN_HD
ATTN_KV_HEADS = ATTN_HEADS // 4
ATTN_GROUP = ATTN_HEADS // ATTN_KV_HEADS
ATTN_QW = ATTN_HEADS * ATTN_HD
ATTN_KVW = ATTN_KV_HEADS * ATTN_HD
ROPE_THETA = 10000.0
HY_W = BRANCH_W
HY_ORDER = 2
HY_EMB = 33
HY_BANDS = (HY_EMB - 1) // 2
HY_FFN = 64
HY_SHORT = 3
HY_MOD_SHIFT = 0.05
HY_DECAY_SHORT_PCT = 0.3
HY_DECAY_LONG_PCT = 1.5
HY_DECAY_TARGET = 1e-2
N_FILT = 2 * HY_ORDER * HY_W
SPLITS = (GLA_KW, GLA_KW, GLA_VW, GLA_VW, GLA_RANK, GLA_RANK,
          ATTN_QW, ATTN_KVW, ATTN_KVW, ATTN_QW,
          (HY_ORDER + 1) * HY_W, HY_W,
          N_BRANCH * D_MODEL)

LANES = 128
VMEM_LIMIT = 48 * 1024 * 1024
FFT_D_VMEM_LIMIT = 56 * 1024 * 1024

COLS = {
    'm': (0, 3072), 'y_u': (3072, 1536), 'g_v': (4608, 512), 'g_z': (5120, 512),
    'a_q': (5632, 512), 'a_z': (6144, 512), 'y_z': (6656, 512), 'g_q': (7168, 256),
    'g_k': (7424, 256), 'a_k': (7680, 128), 'a_v': (7808, 128), 'g_a': (7936, 128),
    'g_a_lo': (8064, 128),
}
N_PROJ = 8192
PROJ_TN = 2048


def _cparams(*sem):
    return pltpu.CompilerParams(dimension_semantics=sem, vmem_limit_bytes=VMEM_LIMIT)


def _split(a):
    hi = a.astype(BF16)
    lo = (a - hi.astype(F32)).astype(BF16)
    return hi, lo


def _dot(a, b):
    return jnp.dot(a, b, preferred_element_type=F32)


def _dot_hp(a, b):
    ah, al = _split(a)
    bh, bl = _split(b)
    return _dot(ah, bh) + _dot(al, bh) + _dot(ah, bl)


def _dot_mat(mh, ml, x):
    xh = x.astype(BF16)
    return _dot(mh, xh) + _dot(ml, xh)


def _sigmoid(x):
    return 1.0 / (1.0 + jnp.exp(-x))


def _silu(x):
    return x * _sigmoid(x)


def _ada_kernel(c_ref, w_ref, b_ref, o_ref):
    o_ref[0] = _dot_hp(_silu(c_ref[...]), w_ref[0]) + b_ref[0]


def _ada_call(cmat, w_ada, b_ada):
    depth, d, n3 = w_ada.shape
    tn = 1024
    return pl.pallas_call(
        _ada_kernel,
        grid=(depth, n3 // tn),
        in_specs=[pl.BlockSpec((8, d), lambda l, j: (0, 0)),
                  pl.BlockSpec((1, d, tn), lambda l, j: (l, 0, j)),
                  pl.BlockSpec((1, 1, tn), lambda l, j: (l, 0, j))],
        out_specs=pl.BlockSpec((1, 8, tn), lambda l, j: (l, 0, j)),
        out_shape=jax.ShapeDtypeStruct((depth, 8, n3), F32),
        compiler_params=_cparams("arbitrary", "arbitrary"),
        name="ada",
    )(cmat, w_ada, b_ada.reshape(depth, 1, n3))


def _proj_kernel(x_ref, sc_ref, sh_ref, w_ref, o_ref, *, res_tile, hi_off, lo_off):
    x = x_ref[0]
    ms = jnp.mean(x * x, axis=-1, keepdims=True)
    h = x * lax.rsqrt(ms + EPS) * (1.0 + sc_ref[0]) + sh_ref[0]
    res = _dot(h.astype(BF16), w_ref[...])
    out = res.astype(o_ref.dtype)
    o_ref[0] = out

    @pl.when(pl.program_id(0) == res_tile)
    def _():
        hi = slice(hi_off, hi_off + LANES)
        o_ref[0, :, lo_off:lo_off + LANES] = (res[:, hi] - out[:, hi].astype(F32)).astype(o_ref.dtype)


def _proj_call(x, scale, shift, w):
    b_, L, d = x.shape
    n = w.shape[1]
    tm = min(1024, L)
    tn = PROJ_TN
    hi0, lo0 = COLS['g_a'][0], COLS['g_a_lo'][0]
    assert hi0 // tn == lo0 // tn
    return pl.pallas_call(
        functools.partial(_proj_kernel, res_tile=hi0 // tn, hi_off=hi0 % tn, lo_off=lo0 % tn),
        grid=(n // tn, b_, L // tm),
        in_specs=[pl.BlockSpec((1, tm, d), lambda j, b, i: (b, i, 0)),
                  pl.BlockSpec((1, 1, d), lambda j, b, i: (b, 0, 0)),
                  pl.BlockSpec((1, 1, d), lambda j, b, i: (b, 0, 0)),
                  pl.BlockSpec((d, tn), lambda j, b, i: (0, j))],
        out_specs=pl.BlockSpec((1, tm, tn), lambda j, b, i: (b, i, j)),
        out_shape=jax.ShapeDtypeStruct((b_, L, n), BF16),
        compiler_params=_cparams("arbitrary", "arbitrary", "arbitrary"),
        name="proj",
    )(x, scale, shift, w)


def _gla_kernel(*refs, reverse, finalize, nchunks):
    if finalize:
        (q_ref, k_ref, v_ref, ga_ref, gal_ref, wa_ref, ba_ref, s0_ref, tri_ref, op_ref, z_ref, nw_ref,
         o_ref, sfin_ref, s_scr) = refs
    else:
        (q_ref, k_ref, v_ref, ga_ref, gal_ref, wa_ref, ba_ref, s0_ref, tri_ref,
         o_ref, sfin_ref, s_scr) = refs
    C = GLA_CHUNK
    T = nchunks * C
    nt_dims = (((1,), (1,)), ((), ()))
    tn_dims = (((0,), (0,)), ((), ()))

    @pl.when(pl.program_id(1) == 0)
    def _():
        s_scr[...] = s0_ref[0]

    ga = ga_ref[0].astype(F32) + gal_ref[0].astype(F32)
    xg = _dot_hp(ga, wa_ref[...]) + ba_ref[...]
    g = (jnp.minimum(xg, 0.0) - jnp.log(1.0 + jnp.exp(-jnp.abs(xg)))) * (1.0 / GLA_GATE_NORM)

    tri = tri_ref[...]
    g1 = g.astype(BF16)
    r1 = g - g1.astype(F32)
    g2 = r1.astype(BF16)
    g3 = (r1 - g2.astype(F32)).astype(BF16)
    b = _dot(tri, g1) + _dot(tri, g2) + _dot(tri, g3)

    def chunk_row(r):
        return jnp.concatenate([jnp.broadcast_to(b[c * C + r:c * C + r + 1], (C, GLA_KW))
                                for c in range(nchunks)], axis=0)

    tot_row = 0 if reverse else C - 1
    bm = chunk_row(C // 2)
    bt = chunk_row(tot_row)
    ri = lax.broadcasted_iota(jnp.int32, (T, T), 0)
    ci = lax.broadcasted_iota(jnp.int32, (T, T), 1)
    same_chunk = (ri // C) == (ci // C)
    mask = same_chunk & ((ci >= ri) if reverse else (ci <= ri))
    lane = lax.broadcasted_iota(jnp.int32, (1, GLA_KW), 1)
    hmask = [(lane >= h * GLA_DK) & (lane < (h + 1) * GLA_DK) for h in range(GLA_HEADS)]

    q = q_ref[0].astype(F32) * (GLA_DK ** -0.5)
    k = k_ref[0].astype(F32)
    v = v_ref[0].astype(BF16)
    qa = q * jnp.exp(b - bm)
    kb = (k * jnp.exp(bm - b)).astype(BF16)
    qe = q * jnp.exp(b)
    kd = (k * jnp.exp(bt - b)).astype(BF16)

    intra = []
    for h in range(GLA_HEADS):
        qa_h = jnp.where(hmask[h], qa, 0.0).astype(BF16)
        att = lax.dot_general(qa_h, kb, nt_dims, preferred_element_type=F32)
        att = jnp.where(mask, att, 0.0).astype(BF16)
        intra.append(_dot(att, v[:, h * GLA_DV:(h + 1) * GLA_DV]))

    inter = [None] * nchunks
    order = range(nchunks - 1, -1, -1) if reverse else range(nchunks)
    for c in order:
        rows = slice(c * C, (c + 1) * C)
        st = s_scr[...]
        st_b = st.astype(BF16)
        dec = jnp.exp(b[c * C + tot_row:c * C + tot_row + 1])
        upd = jnp.zeros_like(st)
        outs = []
        for h in range(GLA_HEADS):
            qe_h = jnp.where(hmask[h], qe[rows], 0.0).astype(BF16)
            outs.append(lax.dot_general(qe_h, st_b, nt_dims, preferred_element_type=F32))
            u_h = lax.dot_general(v[rows, h * GLA_DV:(h + 1) * GLA_DV], kd[rows], tn_dims,
                                  preferred_element_type=F32)
            upd = upd + jnp.where(hmask[h], u_h, 0.0)
        s_scr[...] = dec * st + upd
        inter[c] = jnp.concatenate(outs, axis=1)
    o = jnp.concatenate(intra, axis=1) + jnp.concatenate(inter, axis=0)

    if finalize:
        o = o + op_ref[0]
        parts = []
        for h in range(GLA_HEADS):
            oh = o[:, h * GLA_DV:(h + 1) * GLA_DV]
            ms = jnp.mean(oh * oh, axis=-1, keepdims=True)
            parts.append(oh * lax.rsqrt(ms + EPS) * nw_ref[...])
        y = jnp.concatenate(parts, axis=1) * _silu(z_ref[0].astype(F32))
        o_ref[0] = y.astype(o_ref.dtype)
    else:
        o_ref[0] = o

    @pl.when(pl.program_id(1) == pl.num_programs(1) - 1)
    def _():
        sfin_ref[0] = s_scr[...]


def _gla_call(P, wa_pad, ba, s0, reverse, fin=None):
    b_, L, _ = P.shape
    T = min(GLA_BLOCK, L)
    nt = L // T
    if reverse:
        tmap = lambda t: nt - 1 - t
    else:
        tmap = lambda t: t
    ii = np.arange(T)
    same = (ii[:, None] // GLA_CHUNK) == (ii[None, :] // GLA_CHUNK)
    tri = same & ((ii[None, :] >= ii[:, None]) if reverse else (ii[None, :] <= ii[:, None]))
    tri = jnp.asarray(tri.astype(np.float32)).astype(BF16)

    def col(name, width):
        blk = COLS[name][0] // width
        return pl.BlockSpec((1, T, width), lambda b, t: (b, tmap(t), blk))

    in_specs = [col('g_q', GLA_KW), col('g_k', GLA_KW), col('g_v', GLA_VW),
                col('g_a', LANES), col('g_a_lo', LANES),
                pl.BlockSpec((LANES, GLA_KW), lambda b, t: (0, 0)),
                pl.BlockSpec((1, GLA_KW), lambda b, t: (0, 0)),
                pl.BlockSpec((1, GLA_DV, GLA_KW), lambda b, t: (b, 0, 0)),
                pl.BlockSpec((T, T), lambda b, t: (0, 0))]
    args = [P, P, P, P, P, wa_pad, ba, s0, tri]
    if fin is not None:
        o_prev, nw = fin
        in_specs += [pl.BlockSpec((1, T, GLA_VW), lambda b, t: (b, tmap(t), 0)),
                     col('g_z', GLA_VW),
                     pl.BlockSpec((1, GLA_DV), lambda b, t: (0, 0))]
        args += [o_prev, P, nw]
    out_dtype = BF16 if fin is not None else F32
    return pl.pallas_call(
        functools.partial(_gla_kernel, reverse=reverse, finalize=fin is not None, nchunks=T // GLA_CHUNK),
        grid=(b_, nt),
        in_specs=in_specs,
        out_specs=[pl.BlockSpec((1, T, GLA_VW), lambda b, t: (b, tmap(t), 0)),
                   pl.BlockSpec((1, GLA_DV, GLA_KW), lambda b, t: (b, 0, 0))],
        out_shape=[jax.ShapeDtypeStruct((b_, L, GLA_VW), out_dtype),
                   jax.ShapeDtypeStruct((b_, GLA_DV, GLA_KW), F32)],
        scratch_shapes=[pltpu.VMEM((GLA_DV, GLA_KW), F32)],
        compiler_params=_cparams("arbitrary", "arbitrary"),
        name="gla",
    )(*args)


def _norm_rope(a, gain, bd, cos, sin):
    sq = a * a
    sh, sl = _split(sq)
    ss = _dot(sh, bd) + _dot(sl, bd)
    an = a * lax.rsqrt(ss * (1.0 / ATTN_HD) + EPS) * gain
    w = a.shape[-1]
    lane = lax.broadcasted_iota(jnp.int32, an.shape, 1)
    partner = jnp.where(lane % 2 == 0, pltpu.roll(an, w - 1, 1), pltpu.roll(an, 1, 1))
    return an * cos + partner * sin


Q_SCALE = (ATTN_HD ** -0.5) * math.log2(math.e)
ATTN_QK_DEPTH = 4 * ATTN_HD
ATTN_TK_MAX = 640


def _split8(a):
    hi = a.astype(F8).astype(F32)
    lo = (a - hi).astype(F8)
    return hi.astype(BF16), lo.astype(BF16)


def _qkv_prep_kernel(q_ref, k_ref, v_ref, cq_ref, sq_ref, ck_ref, sk_ref, gq_ref, gk_ref,
                     bdq_ref, bdk_ref, eq_ref, ev_ref, qo_ref, ko_ref, vo_ref):
    nt = (((1,), (1,)), ((), ()))
    hd = ATTN_HD
    q = _norm_rope(q_ref[0].astype(F32), gq_ref[...], bdq_ref[...], cq_ref[...], sq_ref[...]) * Q_SCALE
    qh, ql = _split8(q)
    qt = lax.dot_general(eq_ref[...], jnp.concatenate([qh, ql], axis=1), nt, preferred_element_type=F32)
    for h in range(ATTN_HEADS):
        hi = qt[2 * hd * h:2 * hd * h + hd]
        lo = qt[2 * hd * h + hd:2 * hd * (h + 1)]
        qo_ref[0, ATTN_QK_DEPTH * h:ATTN_QK_DEPTH * (h + 1), :] = jnp.concatenate([hi, hi, lo, lo], axis=0).astype(F8)
    k = _norm_rope(k_ref[0].astype(F32), gk_ref[...], bdk_ref[...], ck_ref[...], sk_ref[...])
    kh, kl = _split8(k)
    for g in range(ATTN_KV_HEADS):
        cols = slice(g * hd, (g + 1) * hd)
        ko_ref[0, g] = jnp.concatenate([kh[:, cols], kl[:, cols], kh[:, cols], kl[:, cols]], axis=1).astype(F8)
    vb = v_ref[0].astype(BF16)
    vo_ref[0] = lax.dot_general(ev_ref[...], vb, nt, preferred_element_type=F32).astype(BF16)


def _qkv_prep_call(P, cos_q, sin_q, cos_k, sin_k, gq, gk, bdq, bdk, eye_q, eye_v):
    b_, L, _ = P.shape
    T = min(1024, L)

    def col(name, width):
        blk = COLS[name][0] // width
        return pl.BlockSpec((1, T, width), lambda b, t: (b, t, blk))

    tab = lambda w: pl.BlockSpec((T, w), lambda b, t: (t, 0))
    const = lambda r, w: pl.BlockSpec((r, w), lambda b, t: (0, 0))
    return pl.pallas_call(
        _qkv_prep_kernel,
        grid=(b_, L // T),
        in_specs=[col('a_q', ATTN_QW), col('a_k', ATTN_KVW), col('a_v', ATTN_KVW),
                  tab(ATTN_QW), tab(ATTN_QW), tab(ATTN_KVW), tab(ATTN_KVW),
                  const(1, ATTN_QW), const(1, ATTN_KVW),
                  const(ATTN_QW, ATTN_QW), const(ATTN_KVW, ATTN_KVW),
                  const(2 * ATTN_QW, 2 * ATTN_QW), const(ATTN_KVW, ATTN_KVW)],
        out_specs=[pl.BlockSpec((1, ATTN_HEADS * ATTN_QK_DEPTH, T), lambda b, t: (b, 0, t)),
                   pl.BlockSpec((1, ATTN_KV_HEADS, T, ATTN_QK_DEPTH), lambda b, t: (b, 0, t, 0)),
                   pl.BlockSpec((1, ATTN_KVW, T), lambda b, t: (b, 0, t))],
        out_shape=[jax.ShapeDtypeStruct((b_, ATTN_HEADS * ATTN_QK_DEPTH, L), F8),
                   jax.ShapeDtypeStruct((b_, ATTN_KV_HEADS, L, ATTN_QK_DEPTH), F8),
                   jax.ShapeDtypeStruct((b_, ATTN_KVW, L), BF16)],
        compiler_params=_cparams("arbitrary", "arbitrary"),
        name="qkv_prep",
    )(P, P, P, cos_q, sin_q, cos_k, sin_k, gq, gk, bdq, bdk, eye_q, eye_v)


ATTN_SUM_ROWS = 16
ATTN_UNROLL = 2
ATTN_TQ = 512


def _attn_kernel(qt_ref, k_ref, vt_ref, z_ref, e_ref, o_ref, m_scr, a_scr, acc_scr, s_scr, p_scr,
                 *, tk, nk):
    m_scr[...] = jnp.full(m_scr.shape, -jnp.inf, F32)
    acc_scr[...] = jnp.zeros(acc_scr.shape, F32)
    ones = jnp.ones((ATTN_SUM_ROWS, tk), BF16)

    def scores(j, slot):
        start = j * tk if isinstance(j, int) else pl.multiple_of(j * tk, tk)
        kj = k_ref[0, 0, pl.ds(start, tk), :]
        for h in range(ATTN_GROUP):
            s_scr[slot, h] = _dot(kj, qt_ref[0, h * ATTN_QK_DEPTH:(h + 1) * ATTN_QK_DEPTH, :])

    scores(0, 0)

    def step(j, slot):
        scores(min(j + 1, nk - 1) if isinstance(j, int) else jnp.minimum(j + 1, nk - 1), 1 - slot)
        start = j * tk if isinstance(j, int) else pl.multiple_of(j * tk, tk)
        vte = jnp.concatenate([vt_ref[0, :, pl.ds(start, tk)], ones], axis=0)
        for h in range(ATTN_GROUP):
            s = s_scr[slot, h]
            m_old = m_scr[h]
            m_new = jnp.maximum(m_old, jnp.max(s, axis=0, keepdims=True))
            a_scr[h] = jnp.exp2(m_old - m_new)
            m_scr[h] = m_new
            p_scr[h] = jnp.exp2(s - m_new).astype(BF16)
        for h in range(ATTN_GROUP):
            acc_scr[h] = a_scr[h] * acc_scr[h] + _dot(vte, p_scr[h])

    def trip(i, carry):
        for u in range(ATTN_UNROLL):
            step(ATTN_UNROLL * i + u, u % 2)
        return carry

    ntrips = nk // ATTN_UNROLL
    lax.fori_loop(0, ntrips, trip, 0)
    for j in range(ntrips * ATTN_UNROLL, nk):
        step(j, j % 2)
    tn = (((0,), (0,)), ((), ()))
    out = None
    for h in range(ATTN_GROUP):
        acc = acc_scr[h]
        oh, ol = _split(acc[:ATTN_HD] / acc[ATTN_HD:ATTN_HD + 1])
        part = (lax.dot_general(oh, e_ref[h], tn, preferred_element_type=F32)
                + lax.dot_general(ol, e_ref[h], tn, preferred_element_type=F32))
        out = part if out is None else out + part
    o_ref[0] = (out * _silu(z_ref[0].astype(F32))).astype(o_ref.dtype)


def _attn_call(qt, k, vt, P, e_heads):
    b_, _, L = qt.shape
    Lk = k.shape[2]
    tq = min(ATTN_TQ, L)
    tk = max(t for t in range(LANES, ATTN_TK_MAX + 1, LANES) if Lk % t == 0)
    gw = ATTN_GROUP * ATTN_HD
    zblk = COLS['a_z'][0] // gw
    return pl.pallas_call(
        functools.partial(_attn_kernel, tk=tk, nk=Lk // tk),
        grid=(b_, ATTN_KV_HEADS, L // tq),
        in_specs=[pl.BlockSpec((1, ATTN_GROUP * ATTN_QK_DEPTH, tq), lambda b, g, i: (b, g, i)),
                  pl.BlockSpec((1, 1, Lk, ATTN_QK_DEPTH), lambda b, g, i: (b, g, 0, 0)),
                  pl.BlockSpec((1, ATTN_HD, Lk), lambda b, g, i: (b, g, 0)),
                  pl.BlockSpec((1, tq, gw), lambda b, g, i: (b, i, zblk + g)),
                  pl.BlockSpec((ATTN_GROUP, ATTN_HD, gw), lambda b, g, i: (0, 0, 0))],
        out_specs=pl.BlockSpec((1, tq, gw), lambda b, g, i: (b, i, g)),
        out_shape=jax.ShapeDtypeStruct((b_, L, ATTN_QW), BF16),
        scratch_shapes=[pltpu.VMEM((ATTN_GROUP, 1, tq), F32), pltpu.VMEM((ATTN_GROUP, 1, tq), F32),
                        pltpu.VMEM((ATTN_GROUP, ATTN_HD + ATTN_SUM_ROWS, tq), F32),
                        pltpu.VMEM((2, ATTN_GROUP, tk, tq), F32), pltpu.VMEM((ATTN_GROUP, tk, tq), BF16)],
        compiler_params=_cparams("arbitrary", "arbitrary", "arbitrary"),
        name="attn",
    )(qt, k, vt, P, e_heads)


def _short_conv_kernel(u_ref, up_ref, un_ref, w_ref, b_ref, v_ref, x1_ref, x2_ref):
    t = pl.program_id(1)
    nt = pl.num_programs(1)
    u = u_ref[0].astype(F32)
    T = u.shape[0]
    hr = up_ref.shape[1]
    prev_row = jnp.where(t > 0, up_ref[0, hr - 1:hr, :].astype(F32), 0.0)
    next_row = jnp.where(t < nt - 1, un_ref[0, 0:1, :].astype(F32), 0.0)
    row = lax.broadcasted_iota(jnp.int32, u.shape, 0)
    u_m1 = jnp.where(row == 0, prev_row, pltpu.roll(u, 1, 0))
    u_p1 = jnp.where(row == T - 1, next_row, pltpu.roll(u, T - 1, 0))
    w = w_ref[...]
    out = b_ref[...] + u_m1 * w[0:1] + u * w[1:2] + u_p1 * w[2:3]
    v_ref[0] = out[:, 0:HY_W]
    x1_ref[0] = out[:, HY_W:2 * HY_W]
    x2_ref[0] = out[:, 2 * HY_W:3 * HY_W]


def _short_conv_call(P, w, bias):
    b_, L, _ = P.shape
    T = min(1024, L)
    cw = (HY_ORDER + 1) * HY_W
    blk = COLS['y_u'][0] // cw
    hr = 16
    hb = T // hr
    nh = L // hr
    out_spec = pl.BlockSpec((1, T, HY_W), lambda b, t: (b, t, 0))
    shp = jax.ShapeDtypeStruct((b_, L, HY_W), F32)
    return pl.pallas_call(
        _short_conv_kernel,
        grid=(b_, L // T),
        in_specs=[pl.BlockSpec((1, T, cw), lambda b, t: (b, t, blk)),
                  pl.BlockSpec((1, hr, cw), lambda b, t: (b, jnp.maximum(t * hb - 1, 0), blk)),
                  pl.BlockSpec((1, hr, cw), lambda b, t: (b, jnp.minimum((t + 1) * hb, nh - 1), blk)),
                  pl.BlockSpec((HY_SHORT, cw), lambda b, t: (0, 0)),
                  pl.BlockSpec((1, cw), lambda b, t: (0, 0))],
        out_specs=[out_spec, out_spec, out_spec],
        out_shape=[shp, shp, shp],
        compiler_params=_cparams("arbitrary", "arbitrary"),
        name="short_conv",
    )(P, P, P, w, bias)


FFT_ROWS = 8


def _filter_kernel(emb_ref, w1_ref, b1_ref, f1_ref, w2_ref, b2_ref, f2_ref, w3_ref, dl_ref,
                   hf_ref, hb_ref, sum_ref):
    t = pl.program_id(0)
    emb = emb_ref[...]
    h = jnp.sin(f1_ref[...] * (_dot_hp(emb, w1_ref[...]) + b1_ref[...]))
    h = jnp.sin(f2_ref[...] * (_dot_hp(h, w2_ref[...]) + b2_ref[...]))
    hh, hl = _split(h)
    tt = emb[:, 0:1]

    @pl.when(t == 0)
    def _():
        sum_ref[...] = jnp.zeros_like(sum_ref)

    lag0 = (lax.broadcasted_iota(jnp.int32, (h.shape[0], HY_W), 0) + t * h.shape[0]) == 0
    for o in range(HY_ORDER):
        for d, out_ref in enumerate((hf_ref, hb_ref)):
            cols = slice((2 * o + d) * HY_W, (2 * o + d + 1) * HY_W)
            wh, wl = _split(w3_ref[:, cols])
            f = _dot(hh, wh) + _dot(hl, wh) + _dot(hh, wl)
            f = f * (jnp.exp(-tt * dl_ref[:, cols]) + HY_MOD_SHIFT)
            sum_ref[:, cols] += jnp.broadcast_to(jnp.sum(jnp.abs(f), axis=0, keepdims=True), (8, HY_W))
            if d == 1:
                f = jnp.where(lag0, 0.0, f)
            out_ref[o] = pltpu.einshape("(an)c->a(nc)", f, a=FFT_ROWS)


def _filter_call(emb, w1, b1, f1, w2, b2, f2, w3, deltas, n1, n2):
    L = emb.shape[0]
    TL = FFT_ROWS * n2
    const = lambda r, w: pl.BlockSpec((r, w), lambda t: (0, 0))
    hspec = pl.BlockSpec((HY_ORDER, FFT_ROWS, n2 * HY_W), lambda t: (0, t, 0))
    hshape = jax.ShapeDtypeStruct((HY_ORDER, n1 // 2, n2 * HY_W), F32)
    return pl.pallas_call(
        _filter_kernel,
        grid=(L // TL,),
        in_specs=[pl.BlockSpec((TL, LANES), lambda t: (t, 0)),
                  const(LANES, HY_FFN), const(1, HY_FFN), const(1, HY_FFN),
                  const(HY_FFN, HY_FFN), const(1, HY_FFN), const(1, HY_FFN),
                  const(HY_FFN, N_FILT), const(1, N_FILT)],
        out_specs=[hspec, hspec, pl.BlockSpec((8, N_FILT), lambda t: (0, 0))],
        out_shape=[hshape, hshape, jax.ShapeDtypeStruct((8, N_FILT), F32)],
        compiler_params=_cparams("arbitrary"),
        name="hy_filter",
    )(emb, w1, b1, f1, w2, b2, f2, w3, deltas)


def _fft_sizes(n):
    lg = int(round(math.log2(n)))
    assert 1 << lg == n
    n1 = 1 << ((lg + 1) // 2)
    return n1, n // n1


def _bf16_pair(a):
    a = np.asarray(a, np.float32)
    hi = jnp.asarray(a, F32).astype(BF16)
    lo = (jnp.asarray(a, F32) - hi.astype(F32)).astype(BF16)
    return hi, lo


def _fft_tables(n1, n2):
    n = n1 * n2
    h1 = n1 // 2
    k1 = np.arange(n1)[:, None]
    a = 2.0 * np.pi * ((k1 * np.arange(h1)[None, :]) % n1) / n1
    c, s = np.cos(a), np.sin(a)
    fa = np.block([[c, s], [-s, c]])
    fa_real = np.concatenate([c, -s], axis=0)
    fd = np.block([[c.T, -s.T], [s.T, c.T]])
    k2 = np.arange(n2)[:, None]
    b = 2.0 * np.pi * ((k2 * np.arange(n2)[None, :]) % n2) / n2
    cb, sb = np.cos(b), np.sin(b)
    fb = np.block([[cb, sb], [-sb, cb]])
    fc = np.block([[cb, -sb], [sb, cb]])
    kn = (jnp.arange(n1, dtype=jnp.int32)[:, None] * jnp.arange(n2, dtype=jnp.int32)[None, :]) % n
    tw = kn.astype(F32) * (2.0 * math.pi / n)
    twr = jnp.broadcast_to(jnp.cos(tw)[:, :, None], (n1, n2, LANES))
    twi = jnp.broadcast_to(-jnp.sin(tw)[:, :, None], (n1, n2, LANES))
    return dict(fa=_bf16_pair(fa), fa_real=_bf16_pair(fa_real), fd=_bf16_pair(fd),
                fb=_bf16_pair(fb), fc=_bf16_pair(fc), twr=twr, twi=twi)


def _fa_kernel(x_ref, mh_ref, ml_ref, o_ref, xh_scr, *, packed, part_axis):
    @pl.when(pl.program_id(part_axis) == 0)
    def _():
        if packed:
            x = x_ref[...]
            x = x.reshape(x.shape[0] * x.shape[1], x.shape[2], x.shape[3])
            x = pltpu.einshape("rnc->r(nc)", x)
        else:
            x = x_ref[0]
        xh_scr[...] = x.astype(BF16)

    xh = xh_scr[...]
    r = _dot(mh_ref[...], xh) + _dot(ml_ref[...], xh)
    o_ref[0] = pltpu.einshape("r(nc)->rnc", r, n=FFT_ROWS)


def _fa_call(x4, mats, n1):
    mh, ml = mats
    _, h1, n2, wd = x4.shape
    wt = FFT_ROWS * wd
    return pl.pallas_call(
        functools.partial(_fa_kernel, packed=True, part_axis=1),
        grid=(n2 // FFT_ROWS, 2),
        in_specs=[pl.BlockSpec((2, h1, FFT_ROWS, wd), lambda j, p: (0, 0, j, 0)),
                  pl.BlockSpec((n1, n1), lambda j, p: (p, 0)),
                  pl.BlockSpec((n1, n1), lambda j, p: (p, 0))],
        out_specs=pl.BlockSpec((1, n1, FFT_ROWS, wd), lambda j, p: (p, 0, j, 0)),
        out_shape=jax.ShapeDtypeStruct((2, n1, n2, wd), F32),
        scratch_shapes=[pltpu.VMEM((n1, wt), BF16)],
        compiler_params=_cparams("arbitrary", "arbitrary"),
        name="fft_a",
    )(x4, mh, ml)


def _fa_real_call(x, mats, n1):
    mh, ml = mats
    ng, h1, lanes = x.shape
    wt = FFT_ROWS * HY_W
    n2 = lanes // HY_W
    out = pl.pallas_call(
        functools.partial(_fa_kernel, packed=False, part_axis=2),
        grid=(ng, lanes // wt, 2),
        in_specs=[pl.BlockSpec((1, h1, wt), lambda g, j, p: (g, 0, j)),
                  pl.BlockSpec((n1, h1), lambda g, j, p: (p, 0)),
                  pl.BlockSpec((n1, h1), lambda g, j, p: (p, 0))],
        out_specs=pl.BlockSpec((1, n1, FFT_ROWS, HY_W), lambda g, j, p: (2 * g + p, 0, j, 0)),
        out_shape=jax.ShapeDtypeStruct((2 * ng, n1, n2, HY_W), F32),
        scratch_shapes=[pltpu.VMEM((h1, wt), BF16)],
        compiler_params=_cparams("arbitrary", "arbitrary", "arbitrary"),
        name="fft_a_real",
    )(x, mh, ml)
    return out.reshape(ng, 2, n1, n2, HY_W)


def _lane_tile(a, width):
    return jnp.concatenate([a] * (width // a.shape[-1]), axis=-1)


def _fb_filter_kernel(tf_ref, tb_ref, twr_ref, twi_ref, fh_ref, fl_ref, sf_ref, sb_ref, g_ref, *, n2, scale):
    w = g_ref.shape[-1]
    inv = jnp.concatenate([scale / sf_ref[0:1, :], scale / sb_ref[0:1, :]], axis=1)

    def body(kk, carry):
        twr = _lane_tile(twr_ref[kk], 2 * w)
        twi = _lane_tile(twi_ref[kk], 2 * w)
        tr = jnp.concatenate([tf_ref[0, 0, kk], tb_ref[0, 0, kk]], axis=1)
        ti = jnp.concatenate([tf_ref[0, 1, kk], tb_ref[0, 1, kk]], axis=1)
        p = jnp.concatenate([tr * twr - ti * twi, tr * twi + ti * twr], axis=0)
        z = _dot_mat(fh_ref[...], fl_ref[...], p) * inv
        g_ref[0, 0, kk] = z[:n2, :w] + z[:n2, w:]
        g_ref[0, 1, kk] = z[n2:, :w] - z[n2:, w:]
        return carry

    lax.fori_loop(0, FFT_ROWS, body, 0, unroll=2)


def _fb_filter_call(thf, thb, sums, tabs, n1, n2):
    fh, fl = tabs['fb']
    tspec = pl.BlockSpec((1, 2, FFT_ROWS, n2, HY_W), lambda g, i: (g, 0, i, 0, 0))
    twspec = pl.BlockSpec((FFT_ROWS, n2, LANES), lambda g, i: (i, 0, 0))
    sspec = lambda d: pl.BlockSpec((8, HY_W), lambda g, i: (0, 2 * g + d))
    return pl.pallas_call(
        functools.partial(_fb_filter_kernel, n2=n2, scale=1.0 / (n1 * n2)),
        grid=(HY_ORDER, n1 // FFT_ROWS),
        in_specs=[tspec, tspec, twspec, twspec,
                  pl.BlockSpec(fh.shape, lambda g, i: (0, 0)), pl.BlockSpec(fl.shape, lambda g, i: (0, 0)),
                  sspec(0), sspec(1)],
        out_specs=pl.BlockSpec((1, 2, FFT_ROWS, n2, HY_W), lambda g, i: (g, 0, i, 0, 0)),
        out_shape=jax.ShapeDtypeStruct((HY_ORDER, 2, n1, n2, HY_W), F32),
        compiler_params=_cparams("arbitrary", "arbitrary"),
        name="fft_b_filter",
    )(thf, thb, tabs['twr'], tabs['twi'], fh, fl, sums, sums)


def _fb_kernel(t_ref, twr_ref, twi_ref, fh_ref, fl_ref, ch_ref, cl_ref, g_ref, o_ref, t4_scr, *, n2):
    w = g_ref.shape[-1]

    def body(kk, carry):
        tr = t_ref[0, kk]
        ti = t_ref[1, kk]
        twr = _lane_tile(twr_ref[kk], w)
        twi = _lane_tile(twi_ref[kk], w)
        p = jnp.concatenate([tr * twr - ti * twi, tr * twi + ti * twr], axis=0)
        z = _dot_mat(fh_ref[...], fl_ref[...], p)
        zr, zi = z[:n2], z[n2:]
        gr = g_ref[0, 0, kk]
        gi = g_ref[0, 1, kk]
        y = jnp.concatenate([zr * gr - zi * gi, zr * gi + zi * gr], axis=0)
        v = _dot_mat(ch_ref[...], cl_ref[...], y)
        vr, vi = v[:n2], v[n2:]
        t4_scr[0, kk] = vr * twr + vi * twi
        t4_scr[1, kk] = vi * twr - vr * twi
        return carry

    lax.fori_loop(0, FFT_ROWS, body, 0, unroll=4)
    o_ref[...] = pltpu.einshape("pknc->pk(nc)", t4_scr[...])


def _fb_call(t, g, order, tabs, n1, n2):
    fh, fl = tabs['fb']
    ch, cl = tabs['fc']
    lanes = n2 * HY_W
    mat = lambda m: pl.BlockSpec(m.shape, lambda i: (0, 0))
    dat = pl.BlockSpec((2, FFT_ROWS, lanes), lambda i: (0, i, 0))
    twspec = pl.BlockSpec((FFT_ROWS, n2, LANES), lambda i: (i, 0, 0))
    return pl.pallas_call(
        functools.partial(_fb_kernel, n2=n2),
        grid=(n1 // FFT_ROWS,),
        in_specs=[pl.BlockSpec((2, FFT_ROWS, n2, HY_W), lambda i: (0, i, 0, 0)),
                  twspec, twspec, mat(fh), mat(fl), mat(ch), mat(cl),
                  pl.BlockSpec((1, 2, FFT_ROWS, n2, HY_W), lambda i: (order, 0, i, 0, 0))],
        out_specs=dat,
        out_shape=jax.ShapeDtypeStruct((2, n1, lanes), F32),
        scratch_shapes=[pltpu.VMEM((2, FFT_ROWS, n2, HY_W), F32)],
        compiler_params=_cparams("arbitrary"),
        name="fft_b",
    )(t, tabs['twr'], tabs['twi'], fh, fl, ch, cl, g)


def _fd_kernel(u_ref, mh_ref, ml_ref, z_ref, x_ref, sk_ref, o_ref):
    u = u_ref[...]
    u = u.reshape(u.shape[0] * u.shape[1], u.shape[2])
    y = _dot_mat(mh_ref[...], ml_ref[...], u)
    y = pltpu.einshape("r(nc)->rnc", y, n=FFT_ROWS)
    o_ref[0] = x_ref[0] * (y + sk_ref[...] * z_ref[0])


def _fd_call(u, mats, z4, x4, skip_row, n1):
    mh, ml = mats
    b_, h1, n2, wd = z4.shape
    wt = FFT_ROWS * wd
    dat = pl.BlockSpec((1, h1, FFT_ROWS, wd), lambda j, p: (p, 0, j, 0))
    return pl.pallas_call(
        _fd_kernel,
        grid=(n2 // FFT_ROWS, b_),
        in_specs=[pl.BlockSpec((2, n1, wt), lambda j, p: (0, 0, j)),
                  pl.BlockSpec((h1, 2 * n1), lambda j, p: (p, 0)),
                  pl.BlockSpec((h1, 2 * n1), lambda j, p: (p, 0)),
                  dat, dat, pl.BlockSpec((1, wd), lambda j, p: (0, 0))],
        out_specs=dat,
        out_shape=jax.ShapeDtypeStruct(z4.shape, F32),
        compiler_params=pltpu.CompilerParams(dimension_semantics=("arbitrary", "arbitrary"),
                                             vmem_limit_bytes=FFT_D_VMEM_LIMIT),
        name="fft_d",
    )(u, mh, ml, z4, x4, skip_row)


def _hyena_filter_spectrum(L, tabs, n1, n2, fp):
    f32 = F32
    t = jnp.linspace(0.0, 1.0, L, dtype=f32)[:, None]
    w = 2.0 * math.pi * jnp.arange(L, dtype=f32)[:, None] / L
    fr = jnp.linspace(1e-4, HY_BANDS - 1, HY_BANDS, dtype=f32)[None]
    emb = jnp.concatenate([t, jnp.cos(fr * w), -jnp.sin(fr * w)], axis=-1)
    emb = jnp.pad(emb, ((0, 0), (0, LANES - HY_EMB)))
    deltas = jnp.abs(jnp.linspace(math.log(HY_DECAY_TARGET) / HY_DECAY_SHORT_PCT,
                                  math.log(HY_DECAY_TARGET) / HY_DECAY_LONG_PCT, N_FILT, dtype=f32))[None]
    f1_w, f1_b, f1_freq, f2_w, f2_b, f2_freq, f3_w = fp
    w1 = jnp.pad(f1_w, ((0, LANES - HY_EMB), (0, 0)))
    hf, hb, sums = _filter_call(emb, w1, f1_b[None], f1_freq[None], f2_w, f2_b[None],
                                f2_freq[None], f3_w, deltas, n1, n2)
    thf = _fa_real_call(hf, tabs['fa_real'], n1)
    thb = _fa_real_call(hb, tabs['fa_real'], n1)
    return _fb_filter_call(thf, thb, sums, tabs, n1, n2)


def _hyena_conv(z, xg, g, order, skip, tabs, n1, n2):
    b_, L, wd = z.shape
    assert b_ == 2, "the two batch rows ride as the real and imaginary parts of one transform"
    z4 = z.reshape(b_, n1 // 2, n2, wd)
    x4 = xg.reshape(b_, n1 // 2, n2, wd)
    t = _fa_call(z4, tabs['fa'], n1)
    u = _fb_call(t, g, order, tabs, n1, n2)
    out = _fd_call(u, tabs['fd'], z4, x4, skip[None, :], n1)
    return out.reshape(b_, L, wd)


def _merge_kernel(yg_ref, ya_ref, yh_ref, yz_ref, mg_ref, ma_ref, mh_ref, x_ref, gate_ref,
                  wg_ref, wa_ref, wh_ref, wo_ref, fn_ref, o_ref, *, final):
    yh = (yh_ref[0] * _silu(yz_ref[0].astype(F32))).astype(BF16)
    y = (_sigmoid(mg_ref[0].astype(F32)) * _dot(yg_ref[0], wg_ref[...])
         + _sigmoid(ma_ref[0].astype(F32)) * _dot(ya_ref[0], wa_ref[...])
         + _sigmoid(mh_ref[0].astype(F32)) * _dot(yh, wh_ref[...]))
    out = _dot(y.astype(BF16), wo_ref[...])
    xn = x_ref[0] + gate_ref[0] * out
    if final:
        ms = jnp.mean(xn * xn, axis=-1, keepdims=True)
        xn = xn * lax.rsqrt(ms + EPS) * fn_ref[...]
    o_ref[0] = xn


def _merge_call(y_gla, y_attn, y_hy, P, x, gate, wg, wa, wh, wo, fnorm, final):
    b_, L, d = x.shape
    T = min(1024, L)
    row = lambda w, blk=0: pl.BlockSpec((1, T, w), lambda b, t: (b, t, blk))
    const = lambda r, w: pl.BlockSpec((r, w), lambda b, t: (0, 0))
    mblk = COLS['m'][0] // d
    return pl.pallas_call(
        functools.partial(_merge_kernel, final=final),
        grid=(b_, L // T),
        in_specs=[row(BRANCH_W), row(BRANCH_W), row(BRANCH_W), row(HY_W, COLS['y_z'][0] // HY_W),
                  row(d, mblk), row(d, mblk + 1), row(d, mblk + 2), row(d),
                  pl.BlockSpec((1, 1, d), lambda b, t: (b, 0, 0)),
                  const(BRANCH_W, d), const(BRANCH_W, d), const(BRANCH_W, d), const(d, d), const(1, d)],
        out_specs=row(d),
        out_shape=jax.ShapeDtypeStruct((b_, L, d), F32),
        compiler_params=_cparams("arbitrary", "arbitrary"),
        name="merge",
    )(y_gla, y_attn, y_hy, P, P, P, P, x, gate, wg, wa, wh, wo, fnorm)


def _pack_w_in(w):
    parts = dict(zip(('g_q', 'g_k', 'g_v', 'g_z', 'g_af', 'g_ab', 'a_q', 'a_k', 'a_v', 'a_z', 'y_u', 'y_z', 'm'),
                     jnp.split(w, [int(i) for i in np.cumsum(SPLITS)[:-1]], axis=-1)))
    parts['g_a'] = jnp.pad(jnp.concatenate([parts['g_af'], parts['g_ab']], axis=-1),
                           ((0, 0), (0, LANES - 2 * GLA_RANK)))
    parts['g_a_lo'] = jnp.zeros_like(parts['g_a'])
    order = sorted(COLS, key=lambda n: COLS[n][0])
    return jnp.concatenate([parts[n] for n in order], axis=-1).astype(BF16)


def _rope_tables(L, heads):
    t = jnp.arange(L)
    row = (t // GRID_W).astype(F32)
    colp = (t % GRID_W).astype(F32)
    half = ATTN_HD // 2
    inv = ROPE_THETA ** (-jnp.arange(0, half, 2, dtype=F32) / half)
    ang = jnp.concatenate([row[:, None] * inv, colp[:, None] * inv], axis=-1)
    cos = jnp.repeat(jnp.cos(ang), 2, axis=-1)
    sin = jnp.stack([-jnp.sin(ang), jnp.sin(ang)], axis=-1).reshape(L, ATTN_HD)
    return jnp.tile(cos, (1, heads)), jnp.tile(sin, (1, heads))


def _identity_rope(L, heads):
    return jnp.ones((L, heads * ATTN_HD), F32), jnp.zeros((L, heads * ATTN_HD), F32)


def _block_diag_ones(width):
    i = np.arange(width) // ATTN_HD
    return jnp.asarray((i[:, None] == i[None, :]).astype(np.float32)).astype(BF16)


def kernel(x, c, ctx, c_ctx, w_ada, b_ada, w_in, gla_wa_f, gla_ba_f, gla_wa_b, gla_ba_b, gla_norm,
           attn_qnorm, attn_knorm, hy_conv_w, hy_conv_b, hy_f1_w, hy_f1_b, hy_f1_freq, hy_f2_w,
           hy_f2_b, hy_f2_freq, hy_f3_w, hy_skip, w_br_gla, w_br_attn, w_br_hy, w_out, final_norm):
    b_, L, d = x.shape
    Lc = ctx.shape[1]
    depth = w_ada.shape[0]

    cos_q, sin_q = _rope_tables(L, ATTN_HEADS)
    cos_k, sin_k = cos_q[:, :ATTN_KVW], sin_q[:, :ATTN_KVW]
    cos_qc, sin_qc = _identity_rope(Lc, ATTN_HEADS)
    cos_kc, sin_kc = cos_qc[:, :ATTN_KVW], sin_qc[:, :ATTN_KVW]
    bdq, bdk = _block_diag_ones(ATTN_QW), _block_diag_ones(ATTN_KVW)
    eye_v = jnp.eye(ATTN_KVW, dtype=BF16)
    rr = np.arange(2 * ATTN_QW)
    src = (rr // ATTN_HD % 2) * ATTN_QW + (rr // (2 * ATTN_HD)) * ATTN_HD + rr % ATTN_HD
    eye_q = jnp.asarray((src[:, None] == np.arange(2 * ATTN_QW)[None, :]).astype(np.float32)).astype(BF16)
    gw = ATTN_GROUP * ATTN_HD
    e_heads = jnp.stack([jnp.eye(ATTN_HD, gw, k=h * ATTN_HD, dtype=BF16) for h in range(ATTN_GROUP)])
    n1, n2 = _fft_sizes(2 * L)
    n1c, n2c = _fft_sizes(2 * Lc)
    tabs = _fft_tables(n1, n2)
    tabs_c = _fft_tables(n1c, n2c)
    zero_state = jnp.zeros((b_, GLA_DV, GLA_KW), F32)

    cmat = jnp.concatenate([c, c_ctx[None], jnp.zeros((8 - b_ - 1, d), F32)], axis=0)
    mods = _ada_call(cmat, w_ada, b_ada)

    for l in range(depth):
        need_ctx = l < depth - 1
        shift, scale, gate = [m[:b_, None, :] for m in jnp.split(mods[l], 3, axis=-1)]
        shift_c, scale_c, gate_c = [jnp.broadcast_to(m[b_:b_ + 1, None, :], (b_, 1, d))
                                    for m in jnp.split(mods[l], 3, axis=-1)]
        w_packed = _pack_w_in(w_in[l])
        P = _proj_call(x, scale, shift, w_packed)
        Pc = _proj_call(ctx, scale_c, shift_c, w_packed)

        rk = GLA_RANK
        wa_f = jnp.zeros((LANES, GLA_KW), F32).at[0:rk].set(gla_wa_f[l])
        wa_b = jnp.zeros((LANES, GLA_KW), F32).at[rk:2 * rk].set(gla_wa_b[l])
        ba_f, ba_b = gla_ba_f[l][None], gla_ba_b[l][None]
        nw = gla_norm[l][None]
        oc_f, sc_f = _gla_call(Pc, wa_f, ba_f, zero_state, reverse=False)
        yc_gla, sc_b = _gla_call(Pc, wa_b, ba_b, zero_state, reverse=True, fin=(oc_f, nw))
        o_f, _ = _gla_call(P, wa_f, ba_f, sc_f, reverse=False)
        y_gla, _ = _gla_call(P, wa_b, ba_b, sc_b, reverse=True, fin=(o_f, nw))

        gq = jnp.tile(attn_qnorm[l], ATTN_HEADS)[None]
        gk = jnp.tile(attn_knorm[l], ATTN_KV_HEADS)[None]
        q_a, k_a, v_a = _qkv_prep_call(P, cos_q, sin_q, cos_k, sin_k, gq, gk, bdq, bdk, eye_q, eye_v)
        qc_a, kc_a, vc_a = _qkv_prep_call(Pc, cos_qc, sin_qc, cos_kc, sin_kc, gq, gk, bdq, bdk,
                                          eye_q, eye_v)
        y_attn = _attn_call(q_a, jnp.concatenate([k_a, kc_a], axis=2),
                            jnp.concatenate([v_a, vc_a], axis=2), P, e_heads)

        fp = (hy_f1_w[l], hy_f1_b[l], hy_f1_freq[l], hy_f2_w[l], hy_f2_b[l], hy_f2_freq[l], hy_f3_w[l])
        g_spec = _hyena_filter_spectrum(L, tabs, n1, n2, fp)
        v0, x1, x2 = _short_conv_call(P, hy_conv_w[l], hy_conv_b[l][None])
        z1 = _hyena_conv(v0, x1, g_spec, 0, hy_skip[l, 0], tabs, n1, n2)
        y_hy = _hyena_conv(z1, x2, g_spec, 1, hy_skip[l, 1], tabs, n1, n2)

        wg, wa, wh, wo = (w_br_gla[l].astype(BF16), w_br_attn[l].astype(BF16),
                          w_br_hy[l].astype(BF16), w_out[l].astype(BF16))
        fn = final_norm[None]
        x_new = _merge_call(y_gla, y_attn, y_hy, P, x, gate, wg, wa, wh, wo, fn, final=not need_ctx)

        if need_ctx:
            yc_attn = _attn_call(qc_a, kc_a, vc_a, Pc, e_heads)
            gc_spec = _hyena_filter_spectrum(Lc, tabs_c, n1c, n2c, fp)
            vc0, xc1, xc2 = _short_conv_call(Pc, hy_conv_w[l], hy_conv_b[l][None])
            zc1 = _hyena_conv(vc0, xc1, gc_spec, 0, hy_skip[l, 0], tabs_c, n1c, n2c)
            yc_hy = _hyena_conv(zc1, xc2, gc_spec, 1, hy_skip[l, 1], tabs_c, n1c, n2c)
            ctx = _merge_call(yc_gla, yc_attn, yc_hy, Pc, ctx, gate_c, wg, wa, wh, wo, fn, final=False)
        x = x_new

    return x
```

```python
import functools
import math

import numpy as np
import jax
import jax.numpy as jnp
from jax import lax
from jax.experimental import pallas as pl
from jax.experimental.pallas import tpu as pltpu

F32 = jnp.float32
BF16 = jnp.bfloat16
F8 = jnp.float8_e4m3fn

D_MODEL = 1024
GRID_W = 64
BRANCH_W = D_MODEL // 2
N_BRANCH = 3
EPS = 1e-6
GLA_HEADS = 4
GLA_DV = BRANCH_W // GLA_HEADS
GLA_DK = GLA_DV // 2
GLA_KW = GLA_HEADS * GLA_DK
GLA_VW = GLA_HEADS * GLA_DV
GLA_RANK = 16
GLA_GATE_NORM = 16.0
GLA_CHUNK = 64
GLA_BLOCK = 256
ATTN_HD = 64
ATTN_HEADS = BRANCH_W // ATTN_HD
ATTN_KV_HEADS = ATTN_HEADS // 4
ATTN_GROUP = ATTN_HEADS // ATTN_KV_HEADS
ATTN_QW = ATTN_HEADS * ATTN_HD
ATTN_KVW = ATTN_KV_HEADS * ATTN_HD
ROPE_THETA = 10000.0
HY_W = BRANCH_W
HY_ORDER = 2
HY_EMB = 33
HY_BANDS = (HY_EMB - 1) // 2
HY_FFN = 64
HY_SHORT = 3
HY_MOD_SHIFT = 0.05
HY_DECAY_SHORT_PCT = 0.3
HY_DECAY_LONG_PCT = 1.5
HY_DECAY_TARGET = 1e-2
N_FILT = 2 * HY_ORDER * HY_W
SPLITS = (GLA_KW, GLA_KW, GLA_VW, GLA_VW, GLA_RANK, GLA_RANK,
          ATTN_QW, ATTN_KVW, ATTN_KVW, ATTN_QW,
          (HY_ORDER + 1) * HY_W, HY_W,
          N_BRANCH * D_MODEL)

LANES = 128
VMEM_LIMIT = 48 * 1024 * 1024
FFT_D_VMEM_LIMIT = 56 * 1024 * 1024

COLS = {
    'm': (0, 3072), 'y_u': (3072, 1536), 'g_v': (4608, 512), 'g_z': (5120, 512),
    'a_q': (5632, 512), 'a_z': (6144, 512), 'y_z': (6656, 512), 'g_q': (7168, 256),
    'g_k': (7424, 256), 'a_k': (7680, 128), 'a_v': (7808, 128), 'g_a': (7936, 128),
    'g_a_lo': (8064, 128),
}
N_PROJ = 8192
PROJ_TN = 2048


def _cparams(*sem):
    return pltpu.CompilerParams(dimension_semantics=sem, vmem_limit_bytes=VMEM_LIMIT)


def _split(a):
    hi = a.astype(BF16)
    lo = (a - hi.astype(F32)).astype(BF16)
    return hi, lo


def _dot(a, b):
    return jnp.dot(a, b, preferred_element_type=F32)


def _dot_hp(a, b):
    ah, al = _split(a)
    bh, bl = _split(b)
    return _dot(ah, bh) + _dot(al, bh) + _dot(ah, bl)


def _dot_mat(mh, ml, x):
    xh = x.astype(BF16)
    return _dot(mh, xh) + _dot(ml, xh)


def _sigmoid(x):
    return 1.0 / (1.0 + jnp.exp(-x))


def _silu(x):
    return x * _sigmoid(x)


def _ada_kernel(c_ref, w_ref, b_ref, o_ref):
    o_ref[0] = _dot_hp(_silu(c_ref[...]), w_ref[0]) + b_ref[0]


def _ada_call(cmat, w_ada, b_ada):
    depth, d, n3 = w_ada.shape
    tn = 1024
    return pl.pallas_call(
        _ada_kernel,
        grid=(depth, n3 // tn),
        in_specs=[pl.BlockSpec((8, d), lambda l, j: (0, 0)),
                  pl.BlockSpec((1, d, tn), lambda l, j: (l, 0, j)),
                  pl.BlockSpec((1, 1, tn), lambda l, j: (l, 0, j))],
        out_specs=pl.BlockSpec((1, 8, tn), lambda l, j: (l, 0, j)),
        out_shape=jax.ShapeDtypeStruct((depth, 8, n3), F32),
        compiler_params=_cparams("arbitrary", "arbitrary"),
        name="ada",
    )(cmat, w_ada, b_ada.reshape(depth, 1, n3))


def _proj_kernel(x_ref, sc_ref, sh_ref, w_ref, o_ref, *, res_tile, hi_off, lo_off):
    x = x_ref[0]
    ms = jnp.mean(x * x, axis=-1, keepdims=True)
    h = x * lax.rsqrt(ms + EPS) * (1.0 + sc_ref[0]) + sh_ref[0]
    res = _dot(h.astype(BF16), w_ref[...])
    out = res.astype(o_ref.dtype)
    o_ref[0] = out

    @pl.when(pl.program_id(0) == res_tile)
    def _():
        hi = slice(hi_off, hi_off + LANES)
        o_ref[0, :, lo_off:lo_off + LANES] = (res[:, hi] - out[:, hi].astype(F32)).astype(o_ref.dtype)


def _proj_call(x, scale, shift, w):
    b_, L, d = x.shape
    n = w.shape[1]
    tm = min(1024, L)
    tn = PROJ_TN
    hi0, lo0 = COLS['g_a'][0], COLS['g_a_lo'][0]
    assert hi0 // tn == lo0 // tn
    return pl.pallas_call(
        functools.partial(_proj_kernel, res_tile=hi0 // tn, hi_off=hi0 % tn, lo_off=lo0 % tn),
        grid=(n // tn, b_, L // tm),
        in_specs=[pl.BlockSpec((1, tm, d), lambda j, b, i: (b, i, 0)),
                  pl.BlockSpec((1, 1, d), lambda j, b, i: (b, 0, 0)),
                  pl.BlockSpec((1, 1, d), lambda j, b, i: (b, 0, 0)),
                  pl.BlockSpec((d, tn), lambda j, b, i: (0, j))],
        out_specs=pl.BlockSpec((1, tm, tn), lambda j, b, i: (b, i, j)),
        out_shape=jax.ShapeDtypeStruct((b_, L, n), BF16),
        compiler_params=_cparams("arbitrary", "arbitrary", "arbitrary"),
        name="proj",
    )(x, scale, shift, w)


def _gla_kernel(*refs, reverse, finalize, nchunks):
    if finalize:
        (q_ref, k_ref, v_ref, ga_ref, gal_ref, wa_ref, ba_ref, s0_ref, tri_ref, op_ref, z_ref, nw_ref,
         o_ref, sfin_ref, s_scr) = refs
    else:
        (q_ref, k_ref, v_ref, ga_ref, gal_ref, wa_ref, ba_ref, s0_ref, tri_ref,
         o_ref, sfin_ref, s_scr) = refs
    C = GLA_CHUNK
    T = nchunks * C
    nt_dims = (((1,), (1,)), ((), ()))
    tn_dims = (((0,), (0,)), ((), ()))

    @pl.when(pl.program_id(1) == 0)
    def _():
        s_scr[...] = s0_ref[0]

    ga = ga_ref[0].astype(F32) + gal_ref[0].astype(F32)
    xg = _dot_hp(ga, wa_ref[...]) + ba_ref[...]
    g = (jnp.minimum(xg, 0.0) - jnp.log(1.0 + jnp.exp(-jnp.abs(xg)))) * (1.0 / GLA_GATE_NORM)

    tri = tri_ref[...]
    g1 = g.astype(BF16)
    r1 = g - g1.astype(F32)
    g2 = r1.astype(BF16)
    g3 = (r1 - g2.astype(F32)).astype(BF16)
    b = _dot(tri, g1) + _dot(tri, g2) + _dot(tri, g3)

    def chunk_row(r):
        return jnp.concatenate([jnp.broadcast_to(b[c * C + r:c * C + r + 1], (C, GLA_KW))
                                for c in range(nchunks)], axis=0)

    tot_row = 0 if reverse else C - 1
    bm = chunk_row(C // 2)
    bt = chunk_row(tot_row)
    ri = lax.broadcasted_iota(jnp.int32, (T, T), 0)
    ci = lax.broadcasted_iota(jnp.int32, (T, T), 1)
    same_chunk = (ri // C) == (ci // C)
    mask = same_chunk & ((ci >= ri) if reverse else (ci <= ri))
    lane = lax.broadcasted_iota(jnp.int32, (1, GLA_KW), 1)
    hmask = [(lane >= h * GLA_DK) & (lane < (h + 1) * GLA_DK) for h in range(GLA_HEADS)]

    q = q_ref[0].astype(F32) * (GLA_DK ** -0.5)
    k = k_ref[0].astype(F32)
    v = v_ref[0].astype(BF16)
    qa = q * jnp.exp(b - bm)
    kb = (k * jnp.exp(bm - b)).astype(BF16)
    qe = q * jnp.exp(b)
    kd = (k * jnp.exp(bt - b)).astype(BF16)

    intra = []
    for h in range(GLA_HEADS):
        qa_h = jnp.where(hmask[h], qa, 0.0).astype(BF16)
        att = lax.dot_general(qa_h, kb, nt_dims, preferred_element_type=F32)
        att = jnp.where(mask, att, 0.0).astype(BF16)
        intra.append(_dot(att, v[:, h * GLA_DV:(h + 1) * GLA_DV]))

    inter = [None] * nchunks
    order = range(nchunks - 1, -1, -1) if reverse else range(nchunks)
    for c in order:
        rows = slice(c * C, (c + 1) * C)
        st = s_scr[...]
        st_b = st.astype(BF16)
        dec = jnp.exp(b[c * C + tot_row:c * C + tot_row + 1])
        upd = jnp.zeros_like(st)
        outs = []
        for h in range(GLA_HEADS):
            qe_h = jnp.where(hmask[h], qe[rows], 0.0).astype(BF16)
            outs.append(lax.dot_general(qe_h, st_b, nt_dims, preferred_element_type=F32))
            u_h = lax.dot_general(v[rows, h * GLA_DV:(h + 1) * GLA_DV], kd[rows], tn_dims,
                                  preferred_element_type=F32)
            upd = upd + jnp.where(hmask[h], u_h, 0.0)
        s_scr[...] = dec * st + upd
        inter[c] = jnp.concatenate(outs, axis=1)
    o = jnp.concatenate(intra, axis=1) + jnp.concatenate(inter, axis=0)

    if finalize:
        o = o + op_ref[0]
        parts = []
        for h in range(GLA_HEADS):
            oh = o[:, h * GLA_DV:(h + 1) * GLA_DV]
            ms = jnp.mean(oh * oh, axis=-1, keepdims=True)
            parts.append(oh * lax.rsqrt(ms + EPS) * nw_ref[...])
        y = jnp.concatenate(parts, axis=1) * _silu(z_ref[0].astype(F32))
        o_ref[0] = y.astype(o_ref.dtype)
    else:
        o_ref[0] = o

    @pl.when(pl.program_id(1) == pl.num_programs(1) - 1)
    def _():
        sfin_ref[0] = s_scr[...]


def _gla_call(P, wa_pad, ba, s0, reverse, fin=None):
    b_, L, _ = P.shape
    T = min(GLA_BLOCK, L)
    nt = L // T
    if reverse:
        tmap = lambda t: nt - 1 - t
    else:
        tmap = lambda t: t
    ii = np.arange(T)
    same = (ii[:, None] // GLA_CHUNK) == (ii[None, :] // GLA_CHUNK)
    tri = same & ((ii[None, :] >= ii[:, None]) if reverse else (ii[None, :] <= ii[:, None]))
    tri = jnp.asarray(tri.astype(np.float32)).astype(BF16)

    def col(name, width):
        blk = COLS[name][0] // width
        return pl.BlockSpec((1, T, width), lambda b, t: (b, tmap(t), blk))

    in_specs = [col('g_q', GLA_KW), col('g_k', GLA_KW), col('g_v', GLA_VW),
                col('g_a', LANES), col('g_a_lo', LANES),
                pl.BlockSpec((LANES, GLA_KW), lambda b, t: (0, 0)),
                pl.BlockSpec((1, GLA_KW), lambda b, t: (0, 0)),
                pl.BlockSpec((1, GLA_DV, GLA_KW), lambda b, t: (b, 0, 0)),
                pl.BlockSpec((T, T), lambda b, t: (0, 0))]
    args = [P, P, P, P, P, wa_pad, ba, s0, tri]
    if fin is not None:
        o_prev, nw = fin
        in_specs += [pl.BlockSpec((1, T, GLA_VW), lambda b, t: (b, tmap(t), 0)),
                     col('g_z', GLA_VW),
                     pl.BlockSpec((1, GLA_DV), lambda b, t: (0, 0))]
        args += [o_prev, P, nw]
    out_dtype = BF16 if fin is not None else F32
    return pl.pallas_call(
        functools.partial(_gla_kernel, reverse=reverse, finalize=fin is not None, nchunks=T // GLA_CHUNK),
        grid=(b_, nt),
        in_specs=in_specs,
        out_specs=[pl.BlockSpec((1, T, GLA_VW), lambda b, t: (b, tmap(t), 0)),
                   pl.BlockSpec((1, GLA_DV, GLA_KW), lambda b, t: (b, 0, 0))],
        out_shape=[jax.ShapeDtypeStruct((b_, L, GLA_VW), out_dtype),
                   jax.ShapeDtypeStruct((b_, GLA_DV, GLA_KW), F32)],
        scratch_shapes=[pltpu.VMEM((GLA_DV, GLA_KW), F32)],
        compiler_params=_cparams("arbitrary", "arbitrary"),
        name="gla",
    )(*args)


def _norm_rope(a, gain, bd, cos, sin):
    sq = a * a
    sh, sl = _split(sq)
    ss = _dot(sh, bd) + _dot(sl, bd)
    an = a * lax.rsqrt(ss * (1.0 / ATTN_HD) + EPS) * gain
    w = a.shape[-1]
    lane = lax.broadcasted_iota(jnp.int32, an.shape, 1)
    partner = jnp.where(lane % 2 == 0, pltpu.roll(an, w - 1, 1), pltpu.roll(an, 1, 1))
    return an * cos + partner * sin


Q_SCALE = (ATTN_HD ** -0.5) * math.log2(math.e)
ATTN_QK_DEPTH = 4 * ATTN_HD
ATTN_TK_MAX = 640


def _split8(a):
    hi = a.astype(F8).astype(F32)
    lo = (a - hi).astype(F8)
    return hi.astype(BF16), lo.astype(BF16)


def _qkv_prep_kernel(q_ref, k_ref, v_ref, cq_ref, sq_ref, ck_ref, sk_ref, gq_ref, gk_ref,
                     bdq_ref, bdk_ref, eq_ref, ev_ref, qo_ref, ko_ref, vo_ref):
    nt = (((1,), (1,)), ((), ()))
    hd = ATTN_HD
    q = _norm_rope(q_ref[0].astype(F32), gq_ref[...], bdq_ref[...], cq_ref[...], sq_ref[...]) * Q_SCALE
    qh, ql = _split8(q)
    qt = lax.dot_general(eq_ref[...], jnp.concatenate([qh, ql], axis=1), nt, preferred_element_type=F32)
    for h in range(ATTN_HEADS):
        hi = qt[2 * hd * h:2 * hd * h + hd]
        lo = qt[2 * hd * h + hd:2 * hd * (h + 1)]
        qo_ref[0, ATTN_QK_DEPTH * h:ATTN_QK_DEPTH * (h + 1), :] = jnp.concatenate([hi, hi, lo, lo], axis=0).astype(F8)
    k = _norm_rope(k_ref[0].astype(F32), gk_ref[...], bdk_ref[...], ck_ref[...], sk_ref[...])
    kh, kl = _split8(k)
    for g in range(ATTN_KV_HEADS):
        cols = slice(g * hd, (g + 1) * hd)
        ko_ref[0, g] = jnp.concatenate([kh[:, cols], kl[:, cols], kh[:, cols], kl[:, cols]], axis=1).astype(F8)
    vb = v_ref[0].astype(BF16)
    vo_ref[0] = lax.dot_general(ev_ref[...], vb, nt, preferred_element_type=F32).astype(BF16)


def _qkv_prep_call(P, cos_q, sin_q, cos_k, sin_k, gq, gk, bdq, bdk, eye_q, eye_v):
    b_, L, _ = P.shape
    T = min(1024, L)

    def col(name, width):
        blk = COLS[name][0] // width
        return pl.BlockSpec((1, T, width), lambda b, t: (b, t, blk))

    tab = lambda w: pl.BlockSpec((T, w), lambda b, t: (t, 0))
    const = lambda r, w: pl.BlockSpec((r, w), lambda b, t: (0, 0))
    return pl.pallas_call(
        _qkv_prep_kernel,
        grid=(b_, L // T),
        in_specs=[col('a_q', ATTN_QW), col('a_k', ATTN_KVW), col('a_v', ATTN_KVW),
                  tab(ATTN_QW), tab(ATTN_QW), tab(ATTN_KVW), tab(ATTN_KVW),
                  const(1, ATTN_QW), const(1, ATTN_KVW),
                  const(ATTN_QW, ATTN_QW), const(ATTN_KVW, ATTN_KVW),
                  const(2 * ATTN_QW, 2 * ATTN_QW), const(ATTN_KVW, ATTN_KVW)],
        out_specs=[pl.BlockSpec((1, ATTN_HEADS * ATTN_QK_DEPTH, T), lambda b, t: (b, 0, t)),
                   pl.BlockSpec((1, ATTN_KV_HEADS, T, ATTN_QK_DEPTH), lambda b, t: (b, 0, t, 0)),
                   pl.BlockSpec((1, ATTN_KVW, T), lambda b, t: (b, 0, t))],
        out_shape=[jax.ShapeDtypeStruct((b_, ATTN_HEADS * ATTN_QK_DEPTH, L), F8),
                   jax.ShapeDtypeStruct((b_, ATTN_KV_HEADS, L, ATTN_QK_DEPTH), F8),
                   jax.ShapeDtypeStruct((b_, ATTN_KVW, L), BF16)],
        compiler_params=_cparams("arbitrary", "arbitrary"),
        name="qkv_prep",
    )(P, P, P, cos_q, sin_q, cos_k, sin_k, gq, gk, bdq, bdk, eye_q, eye_v)


ATTN_SUM_ROWS = 16
ATTN_UNROLL = 2
ATTN_TQ = 512


def _attn_kernel(qt_ref, k_ref, vt_ref, z_ref, e_ref, o_ref, m_scr, a_scr, t_scr, acc_scr, s_scr, p_scr,
                 *, tk, nk):
    m_scr[...] = jnp.full(m_scr.shape, -jnp.inf, F32)
    acc_scr[...] = jnp.zeros(acc_scr.shape, F32)
    ones = jnp.ones((ATTN_SUM_ROWS, tk), BF16)

    def scores(j, slot):
        start = j * tk if isinstance(j, int) else pl.multiple_of(j * tk, tk)
        kj = k_ref[0, 0, pl.ds(start, tk), :]
        for h in range(ATTN_GROUP):
            s = _dot(kj, qt_ref[0, h * ATTN_QK_DEPTH:(h + 1) * ATTN_QK_DEPTH, :])
            s_scr[slot, h] = s
            t_scr[slot, h] = jnp.max(s, axis=0, keepdims=True)

    scores(0, 0)

    def step(j, slot):
        scores(min(j + 1, nk - 1) if isinstance(j, int) else jnp.minimum(j + 1, nk - 1), 1 - slot)
        start = j * tk if isinstance(j, int) else pl.multiple_of(j * tk, tk)
        vte = jnp.concatenate([vt_ref[0, :, pl.ds(start, tk)], ones], axis=0)
        for h in range(ATTN_GROUP):
            m_old = m_scr[h]
            m_new = jnp.maximum(m_old, t_scr[slot, h])
            a_scr[h] = jnp.exp2(m_old - m_new)
            m_scr[h] = m_new
            p_scr[h] = jnp.exp2(s_scr[slot, h] - m_new).astype(BF16)
        for h in range(ATTN_GROUP):
            acc_scr[h] = a_scr[h] * acc_scr[h] + _dot(vte, p_scr[h])

    def trip(i, carry):
        for u in range(ATTN_UNROLL):
            step(ATTN_UNROLL * i + u, u % 2)
        return carry

    ntrips = nk // ATTN_UNROLL
    lax.fori_loop(0, ntrips, trip, 0)
    for j in range(ntrips * ATTN_UNROLL, nk):
        step(j, j % 2)
    tn = (((0,), (0,)), ((), ()))
    out = None
    for h in range(ATTN_GROUP):
        acc = acc_scr[h]
        oh, ol = _split(acc[:ATTN_HD] / acc[ATTN_HD:ATTN_HD + 1])
        part = (lax.dot_general(oh, e_ref[h], tn, preferred_element_type=F32)
                + lax.dot_general(ol, e_ref[h], tn, preferred_element_type=F32))
        out = part if out is None else out + part
    o_ref[0] = (out * _silu(z_ref[0].astype(F32))).astype(o_ref.dtype)


def _attn_call(qt, k, vt, P, e_heads):
    b_, _, L = qt.shape
    Lk = k.shape[2]
    tq = min(ATTN_TQ, L)
    tk = max(t for t in range(LANES, ATTN_TK_MAX + 1, LANES) if Lk % t == 0)
    gw = ATTN_GROUP * ATTN_HD
    zblk = COLS['a_z'][0] // gw
    return pl.pallas_call(
        functools.partial(_attn_kernel, tk=tk, nk=Lk // tk),
        grid=(b_, ATTN_KV_HEADS, L // tq),
        in_specs=[pl.BlockSpec((1, ATTN_GROUP * ATTN_QK_DEPTH, tq), lambda b, g, i: (b, g, i)),
                  pl.BlockSpec((1, 1, Lk, ATTN_QK_DEPTH), lambda b, g, i: (b, g, 0, 0)),
                  pl.BlockSpec((1, ATTN_HD, Lk), lambda b, g, i: (b, g, 0)),
                  pl.BlockSpec((1, tq, gw), lambda b, g, i: (b, i, zblk + g)),
                  pl.BlockSpec((ATTN_GROUP, ATTN_HD, gw), lambda b, g, i: (0, 0, 0))],
        out_specs=pl.BlockSpec((1, tq, gw), lambda b, g, i: (b, i, g)),
        out_shape=jax.ShapeDtypeStruct((b_, L, ATTN_QW), BF16),
        scratch_shapes=[pltpu.VMEM((ATTN_GROUP, 1, tq), F32), pltpu.VMEM((ATTN_GROUP, 1, tq), F32),
                        pltpu.VMEM((2, ATTN_GROUP, 1, tq), F32),
                        pltpu.VMEM((ATTN_GROUP, ATTN_HD + ATTN_SUM_ROWS, tq), F32),
                        pltpu.VMEM((2, ATTN_GROUP, tk, tq), F32), pltpu.VMEM((ATTN_GROUP, tk, tq), BF16)],
        compiler_params=_cparams("arbitrary", "arbitrary", "arbitrary"),
        name="attn",
    )(qt, k, vt, P, e_heads)


def _short_conv_kernel(u_ref, up_ref, un_ref, w_ref, b_ref, v_ref, x1_ref, x2_ref):
    t = pl.program_id(1)
    nt = pl.num_programs(1)
    u = u_ref[0].astype(F32)
    T = u.shape[0]
    hr = up_ref.shape[1]
    prev_row = jnp.where(t > 0, up_ref[0, hr - 1:hr, :].astype(F32), 0.0)
    next_row = jnp.where(t < nt - 1, un_ref[0, 0:1, :].astype(F32), 0.0)
    row = lax.broadcasted_iota(jnp.int32, u.shape, 0)
    u_m1 = jnp.where(row == 0, prev_row, pltpu.roll(u, 1, 0))
    u_p1 = jnp.where(row == T - 1, next_row, pltpu.roll(u, T - 1, 0))
    w = w_ref[...]
    out = b_ref[...] + u_m1 * w[0:1] + u * w[1:2] + u_p1 * w[2:3]
    v_ref[0] = out[:, 0:HY_W]
    x1_ref[0] = out[:, HY_W:2 * HY_W]
    x2_ref[0] = out[:, 2 * HY_W:3 * HY_W]


def _short_conv_call(P, w, bias):
    b_, L, _ = P.shape
    T = min(1024, L)
    cw = (HY_ORDER + 1) * HY_W
    blk = COLS['y_u'][0] // cw
    hr = 16
    hb = T // hr
    nh = L // hr
    out_spec = pl.BlockSpec((1, T, HY_W), lambda b, t: (b, t, 0))
    shp = jax.ShapeDtypeStruct((b_, L, HY_W), F32)
    return pl.pallas_call(
        _short_conv_kernel,
        grid=(b_, L // T),
        in_specs=[pl.BlockSpec((1, T, cw), lambda b, t: (b, t, blk)),
                  pl.BlockSpec((1, hr, cw), lambda b, t: (b, jnp.maximum(t * hb - 1, 0), blk)),
                  pl.BlockSpec((1, hr, cw), lambda b, t: (b, jnp.minimum((t + 1) * hb, nh - 1), blk)),
                  pl.BlockSpec((HY_SHORT, cw), lambda b, t: (0, 0)),
                  pl.BlockSpec((1, cw), lambda b, t: (0, 0))],
        out_specs=[out_spec, out_spec, out_spec],
        out_shape=[shp, shp, shp],
        compiler_params=_cparams("arbitrary", "arbitrary"),
        name="short_conv",
    )(P, P, P, w, bias)


FFT_ROWS = 8


def _filter_kernel(emb_ref, w1_ref, b1_ref, f1_ref, w2_ref, b2_ref, f2_ref, w3_ref, dl_ref,
                   hf_ref, hb_ref, sum_ref):
    t = pl.program_id(0)
    emb = emb_ref[...]
    h = jnp.sin(f1_ref[...] * (_dot_hp(emb, w1_ref[...]) + b1_ref[...]))
    h = jnp.sin(f2_ref[...] * (_dot_hp(h, w2_ref[...]) + b2_ref[...]))
    hh, hl = _split(h)
    tt = emb[:, 0:1]

    @pl.when(t == 0)
    def _():
        sum_ref[...] = jnp.zeros_like(sum_ref)

    lag0 = (lax.broadcasted_iota(jnp.int32, (h.shape[0], HY_W), 0) + t * h.shape[0]) == 0
    for o in range(HY_ORDER):
        for d, out_ref in enumerate((hf_ref, hb_ref)):
            cols = slice((2 * o + d) * HY_W, (2 * o + d + 1) * HY_W)
            wh, wl = _split(w3_ref[:, cols])
            f = _dot(hh, wh) + _dot(hl, wh) + _dot(hh, wl)
            f = f * (jnp.exp(-tt * dl_ref[:, cols]) + HY_MOD_SHIFT)
            sum_ref[:, cols] += jnp.broadcast_to(jnp.sum(jnp.abs(f), axis=0, keepdims=True), (8, HY_W))
            if d == 1:
                f = jnp.where(lag0, 0.0, f)
            out_ref[o] = pltpu.einshape("(an)c->a(nc)", f, a=FFT_ROWS)


def _filter_call(emb, w1, b1, f1, w2, b2, f2, w3, deltas, n1, n2):
    L = emb.shape[0]
    TL = FFT_ROWS * n2
    const = lambda r, w: pl.BlockSpec((r, w), lambda t: (0, 0))
    hspec = pl.BlockSpec((HY_ORDER, FFT_ROWS, n2 * HY_W), lambda t: (0, t, 0))
    hshape = jax.ShapeDtypeStruct((HY_ORDER, n1 // 2, n2 * HY_W), F32)
    return pl.pallas_call(
        _filter_kernel,
        grid=(L // TL,),
        in_specs=[pl.BlockSpec((TL, LANES), lambda t: (t, 0)),
                  const(LANES, HY_FFN), const(1, HY_FFN), const(1, HY_FFN),
                  const(HY_FFN, HY_FFN), const(1, HY_FFN), const(1, HY_FFN),
                  const(HY_FFN, N_FILT), const(1, N_FILT)],
        out_specs=[hspec, hspec, pl.BlockSpec((8, N_FILT), lambda t: (0, 0))],
        out_shape=[hshape, hshape, jax.ShapeDtypeStruct((8, N_FILT), F32)],
        compiler_params=_cparams("arbitrary"),
        name="hy_filter",
    )(emb, w1, b1, f1, w2, b2, f2, w3, deltas)


def _fft_sizes(n):
    lg = int(round(math.log2(n)))
    assert 1 << lg == n
    n1 = 1 << ((lg + 1) // 2)
    return n1, n // n1


def _bf16_pair(a):
    a = np.asarray(a, np.float32)
    hi = jnp.asarray(a, F32).astype(BF16)
    lo = (jnp.asarray(a, F32) - hi.astype(F32)).astype(BF16)
    return hi, lo


def _fft_tables(n1, n2):
    n = n1 * n2
    h1 = n1 // 2
    k1 = np.arange(n1)[:, None]
    a = 2.0 * np.pi * ((k1 * np.arange(h1)[None, :]) % n1) / n1
    c, s = np.cos(a), np.sin(a)
    fa = np.block([[c, s], [-s, c]])
    fa_real = np.concatenate([c, -s], axis=0)
    fd = np.block([[c.T, -s.T], [s.T, c.T]])
    k2 = np.arange(n2)[:, None]
    b = 2.0 * np.pi * ((k2 * np.arange(n2)[None, :]) % n2) / n2
    cb, sb = np.cos(b), np.sin(b)
    fb = np.block([[cb, sb], [-sb, cb]])
    fc = np.block([[cb, -sb], [sb, cb]])
    kn = (jnp.arange(n1, dtype=jnp.int32)[:, None] * jnp.arange(n2, dtype=jnp.int32)[None, :]) % n
    tw = kn.astype(F32) * (2.0 * math.pi / n)
    twr = jnp.broadcast_to(jnp.cos(tw)[:, :, None], (n1, n2, LANES))
    twi = jnp.broadcast_to(-jnp.sin(tw)[:, :, None], (n1, n2, LANES))
    return dict(fa=_bf16_pair(fa), fa_real=_bf16_pair(fa_real), fd=_bf16_pair(fd),
                fb=_bf16_pair(fb), fc=_bf16_pair(fc), twr=twr, twi=twi)


def _fa_kernel(x_ref, mh_ref, ml_ref, o_ref, xh_scr, *, packed, part_axis):
    @pl.when(pl.program_id(part_axis) == 0)
    def _():
        if packed:
            x = x_ref[...]
            x = x.reshape(x.shape[0] * x.shape[1], x.shape[2], x.shape[3])
            x = pltpu.einshape("rnc->r(nc)", x)
        else:
            x = x_ref[0]
        xh_scr[...] = x.astype(BF16)

    xh = xh_scr[...]
    r = _dot(mh_ref[...], xh) + _dot(ml_ref[...], xh)
    o_ref[0] = pltpu.einshape("r(nc)->rnc", r, n=FFT_ROWS)


def _fa_call(x4, mats, n1):
    mh, ml = mats
    _, h1, n2, wd = x4.shape
    wt = FFT_ROWS * wd
    return pl.pallas_call(
        functools.partial(_fa_kernel, packed=True, part_axis=1),
        grid=(n2 // FFT_ROWS, 2),
        in_specs=[pl.BlockSpec((2, h1, FFT_ROWS, wd), lambda j, p: (0, 0, j, 0)),
                  pl.BlockSpec((n1, n1), lambda j, p: (p, 0)),
                  pl.BlockSpec((n1, n1), lambda j, p: (p, 0))],
        out_specs=pl.BlockSpec((1, n1, FFT_ROWS, wd), lambda j, p: (p, 0, j, 0)),
        out_shape=jax.ShapeDtypeStruct((2, n1, n2, wd), F32),
        scratch_shapes=[pltpu.VMEM((n1, wt), BF16)],
        compiler_params=_cparams("arbitrary", "arbitrary"),
        name="fft_a",
    )(x4, mh, ml)


def _fa_real_call(x, mats, n1):
    mh, ml = mats
    ng, h1, lanes = x.shape
    wt = FFT_ROWS * HY_W
    n2 = lanes // HY_W
    out = pl.pallas_call(
        functools.partial(_fa_kernel, packed=False, part_axis=2),
        grid=(ng, lanes // wt, 2),
        in_specs=[pl.BlockSpec((1, h1, wt), lambda g, j, p: (g, 0, j)),
                  pl.BlockSpec((n1, h1), lambda g, j, p: (p, 0)),
                  pl.BlockSpec((n1, h1), lambda g, j, p: (p, 0))],
        out_specs=pl.BlockSpec((1, n1, FFT_ROWS, HY_W), lambda g, j, p: (2 * g + p, 0, j, 0)),
        out_shape=jax.ShapeDtypeStruct((2 * ng, n1, n2, HY_W), F32),
        scratch_shapes=[pltpu.VMEM((h1, wt), BF16)],
        compiler_params=_cparams("arbitrary", "arbitrary", "arbitrary"),
        name="fft_a_real",
    )(x, mh, ml)
    return out.reshape(ng, 2, n1, n2, HY_W)


def _lane_tile(a, width):
    return jnp.concatenate([a] * (width // a.shape[-1]), axis=-1)


def _fb_filter_kernel(tf_ref, tb_ref, twr_ref, twi_ref, fh_ref, fl_ref, sf_ref, sb_ref, g_ref, *, n2, scale):
    w = g_ref.shape[-1]
    inv = jnp.concatenate([scale / sf_ref[0:1, :], scale / sb_ref[0:1, :]], axis=1)

    def body(kk, carry):
        twr = _lane_tile(twr_ref[kk], 2 * w)
        twi = _lane_tile(twi_ref[kk], 2 * w)
        tr = jnp.concatenate([tf_ref[0, 0, kk], tb_ref[0, 0, kk]], axis=1)
        ti = jnp.concatenate([tf_ref[0, 1, kk], tb_ref[0, 1, kk]], axis=1)
        p = jnp.concatenate([tr * twr - ti * twi, tr * twi + ti * twr], axis=0)
        z = _dot_mat(fh_ref[...], fl_ref[...], p) * inv
        g_ref[0, 0, kk] = z[:n2, :w] + z[:n2, w:]
        g_ref[0, 1, kk] = z[n2:, :w] - z[n2:, w:]
        return carry

    lax.fori_loop(0, FFT_ROWS, body, 0, unroll=2)


def _fb_filter_call(thf, thb, sums, tabs, n1, n2):
    fh, fl = tabs['fb']
    tspec = pl.BlockSpec((1, 2, FFT_ROWS, n2, HY_W), lambda g, i: (g, 0, i, 0, 0))
    twspec = pl.BlockSpec((FFT_ROWS, n2, LANES), lambda g, i: (i, 0, 0))
    sspec = lambda d: pl.BlockSpec((8, HY_W), lambda g, i: (0, 2 * g + d))
    return pl.pallas_call(
        functools.partial(_fb_filter_kernel, n2=n2, scale=1.0 / (n1 * n2)),
        grid=(HY_ORDER, n1 // FFT_ROWS),
        in_specs=[tspec, tspec, twspec, twspec,
                  pl.BlockSpec(fh.shape, lambda g, i: (0, 0)), pl.BlockSpec(fl.shape, lambda g, i: (0, 0)),
                  sspec(0), sspec(1)],
        out_specs=pl.BlockSpec((1, 2, FFT_ROWS, n2, HY_W), lambda g, i: (g, 0, i, 0, 0)),
        out_shape=jax.ShapeDtypeStruct((HY_ORDER, 2, n1, n2, HY_W), F32),
        compiler_params=_cparams("arbitrary", "arbitrary"),
        name="fft_b_filter",
    )(thf, thb, tabs['twr'], tabs['twi'], fh, fl, sums, sums)


def _fb_kernel(t_ref, twr_ref, twi_ref, fh_ref, fl_ref, ch_ref, cl_ref, g_ref, o_ref, t4_scr, *, n2):
    w = g_ref.shape[-1]

    def body(kk, carry):
        tr = t_ref[0, kk]
        ti = t_ref[1, kk]
        twr = _lane_tile(twr_ref[kk], w)
        twi = _lane_tile(twi_ref[kk], w)
        p = jnp.concatenate([tr * twr - ti * twi, tr * twi + ti * twr], axis=0)
        z = _dot_mat(fh_ref[...], fl_ref[...], p)
        zr, zi = z[:n2], z[n2:]
        gr = g_ref[0, 0, kk]
        gi = g_ref[0, 1, kk]
        y = jnp.concatenate([zr * gr - zi * gi, zr * gi + zi * gr], axis=0)
        v = _dot_mat(ch_ref[...], cl_ref[...], y)
        vr, vi = v[:n2], v[n2:]
        t4_scr[0, kk] = vr * twr + vi * twi
        t4_scr[1, kk] = vi * twr - vr * twi
        return carry

    lax.fori_loop(0, FFT_ROWS, body, 0, unroll=4)
    o_ref[...] = pltpu.einshape("pknc->pk(nc)", t4_scr[...])


def _fb_call(t, g, order, tabs, n1, n2):
    fh, fl = tabs['fb']
    ch, cl = tabs['fc']
    lanes = n2 * HY_W
    mat = lambda m: pl.BlockSpec(m.shape, lambda i: (0, 0))
    dat = pl.BlockSpec((2, FFT_ROWS, lanes), lambda i: (0, i, 0))
    twspec = pl.BlockSpec((FFT_ROWS, n2, LANES), lambda i: (i, 0, 0))
    return pl.pallas_call(
        functools.partial(_fb_kernel, n2=n2),
        grid=(n1 // FFT_ROWS,),
        in_specs=[pl.BlockSpec((2, FFT_ROWS, n2, HY_W), lambda i: (0, i, 0, 0)),
                  twspec, twspec, mat(fh), mat(fl), mat(ch), mat(cl),
                  pl.BlockSpec((1, 2, FFT_ROWS, n2, HY_W), lambda i: (order, 0, i, 0, 0))],
        out_specs=dat,
        out_shape=jax.ShapeDtypeStruct((2, n1, lanes), F32),
        scratch_shapes=[pltpu.VMEM((2, FFT_ROWS, n2, HY_W), F32)],
        compiler_params=_cparams("arbitrary"),
        name="fft_b",
    )(t, tabs['twr'], tabs['twi'], fh, fl, ch, cl, g)


def _fd_kernel(u_ref, mh_ref, ml_ref, z_ref, x_ref, sk_ref, o_ref):
    u = u_ref[...]
    u = u.reshape(u.shape[0] * u.shape[1], u.shape[2])
    y = _dot_mat(mh_ref[...], ml_ref[...], u)
    y = pltpu.einshape("r(nc)->rnc", y, n=FFT_ROWS)
    o_ref[0] = x_ref[0] * (y + sk_ref[...] * z_ref[0])


def _fd_call(u, mats, z4, x4, skip_row, n1):
    mh, ml = mats
    b_, h1, n2, wd = z4.shape
    wt = FFT_ROWS * wd
    dat = pl.BlockSpec((1, h1, FFT_ROWS, wd), lambda j, p: (p, 0, j, 0))
    return pl.pallas_call(
        _fd_kernel,
        grid=(n2 // FFT_ROWS, b_),
        in_specs=[pl.BlockSpec((2, n1, wt), lambda j, p: (0, 0, j)),
                  pl.BlockSpec((h1, 2 * n1), lambda j, p: (p, 0)),
                  pl.BlockSpec((h1, 2 * n1), lambda j, p: (p, 0)),
                  dat, dat, pl.BlockSpec((1, wd), lambda j, p: (0, 0))],
        out_specs=dat,
        out_shape=jax.ShapeDtypeStruct(z4.shape, F32),
        compiler_params=pltpu.CompilerParams(dimension_semantics=("arbitrary", "arbitrary"),
                                             vmem_limit_bytes=FFT_D_VMEM_LIMIT),
        name="fft_d",
    )(u, mh, ml, z4, x4, skip_row)


def _hyena_filter_spectrum(L, tabs, n1, n2, fp):
    f32 = F32
    t = jnp.linspace(0.0, 1.0, L, dtype=f32)[:, None]
    w = 2.0 * math.pi * jnp.arange(L, dtype=f32)[:, None] / L
    fr = jnp.linspace(1e-4, HY_BANDS - 1, HY_BANDS, dtype=f32)[None]
    emb = jnp.concatenate([t, jnp.cos(fr * w), -jnp.sin(fr * w)], axis=-1)
    emb = jnp.pad(emb, ((0, 0), (0, LANES - HY_EMB)))
    deltas = jnp.abs(jnp.linspace(math.log(HY_DECAY_TARGET) / HY_DECAY_SHORT_PCT,
                                  math.log(HY_DECAY_TARGET) / HY_DECAY_LONG_PCT, N_FILT, dtype=f32))[None]
    f1_w, f1_b, f1_freq, f2_w, f2_b, f2_freq, f3_w = fp
    w1 = jnp.pad(f1_w, ((0, LANES - HY_EMB), (0, 0)))
    hf, hb, sums = _filter_call(emb, w1, f1_b[None], f1_freq[None], f2_w, f2_b[None],
                                f2_freq[None], f3_w, deltas, n1, n2)
    thf = _fa_real_call(hf, tabs['fa_real'], n1)
    thb = _fa_real_call(hb, tabs['fa_real'], n1)
    return _fb_filter_call(thf, thb, sums, tabs, n1, n2)


def _hyena_conv(z, xg, g, order, skip, tabs, n1, n2):
    b_, L, wd = z.shape
    assert b_ == 2, "the two batch rows ride as the real and imaginary parts of one transform"
    z4 = z.reshape(b_, n1 // 2, n2, wd)
    x4 = xg.reshape(b_, n1 // 2, n2, wd)
    t = _fa_call(z4, tabs['fa'], n1)
    u = _fb_call(t, g, order, tabs, n1, n2)
    out = _fd_call(u, tabs['fd'], z4, x4, skip[None, :], n1)
    return out.reshape(b_, L, wd)


def _merge_kernel(yg_ref, ya_ref, yh_ref, yz_ref, mg_ref, ma_ref, mh_ref, x_ref, gate_ref,
                  wg_ref, wa_ref, wh_ref, wo_ref, fn_ref, o_ref, *, final):
    yh = (yh_ref[0] * _silu(yz_ref[0].astype(F32))).astype(BF16)
    y = (_sigmoid(mg_ref[0].astype(F32)) * _dot(yg_ref[0], wg_ref[...])
         + _sigmoid(ma_ref[0].astype(F32)) * _dot(ya_ref[0], wa_ref[...])
         + _sigmoid(mh_ref[0].astype(F32)) * _dot(yh, wh_ref[...]))
    out = _dot(y.astype(BF16), wo_ref[...])
    xn = x_ref[0] + gate_ref[0] * out
    if final:
        ms = jnp.mean(xn * xn, axis=-1, keepdims=True)
        xn = xn * lax.rsqrt(ms + EPS) * fn_ref[...]
    o_ref[0] = xn


def _merge_call(y_gla, y_attn, y_hy, P, x, gate, wg, wa, wh, wo, fnorm, final):
    b_, L, d = x.shape
    T = min(1024, L)
    row = lambda w, blk=0: pl.BlockSpec((1, T, w), lambda b, t: (b, t, blk))
    const = lambda r, w: pl.BlockSpec((r, w), lambda b, t: (0, 0))
    mblk = COLS['m'][0] // d
    return pl.pallas_call(
        functools.partial(_merge_kernel, final=final),
        grid=(b_, L // T),
        in_specs=[row(BRANCH_W), row(BRANCH_W), row(BRANCH_W), row(HY_W, COLS['y_z'][0] // HY_W),
                  row(d, mblk), row(d, mblk + 1), row(d, mblk + 2), row(d),
                  pl.BlockSpec((1, 1, d), lambda b, t: (b, 0, 0)),
                  const(BRANCH_W, d), const(BRANCH_W, d), const(BRANCH_W, d), const(d, d), const(1, d)],
        out_specs=row(d),
        out_shape=jax.ShapeDtypeStruct((b_, L, d), F32),
        compiler_params=_cparams("arbitrary", "arbitrary"),
        name="merge",
    )(y_gla, y_attn, y_hy, P, P, P, P, x, gate, wg, wa, wh, wo, fnorm)


def _pack_w_in(w):
    parts = dict(zip(('g_q', 'g_k', 'g_v', 'g_z', 'g_af', 'g_ab', 'a_q', 'a_k', 'a_v', 'a_z', 'y_u', 'y_z', 'm'),
                     jnp.split(w, [int(i) for i in np.cumsum(SPLITS)[:-1]], axis=-1)))
    parts['g_a'] = jnp.pad(jnp.concatenate([parts['g_af'], parts['g_ab']], axis=-1),
                           ((0, 0), (0, LANES - 2 * GLA_RANK)))
    parts['g_a_lo'] = jnp.zeros_like(parts['g_a'])
    order = sorted(COLS, key=lambda n: COLS[n][0])
    return jnp.concatenate([parts[n] for n in order], axis=-1).astype(BF16)


def _rope_tables(L, heads):
    t = jnp.arange(L)
    row = (t // GRID_W).astype(F32)
    colp = (t % GRID_W).astype(F32)
    half = ATTN_HD // 2
    inv = ROPE_THETA ** (-jnp.arange(0, half, 2, dtype=F32) / half)
    ang = jnp.concatenate([row[:, None] * inv, colp[:, None] * inv], axis=-1)
    cos = jnp.repeat(jnp.cos(ang), 2, axis=-1)
    sin = jnp.stack([-jnp.sin(ang), jnp.sin(ang)], axis=-1).reshape(L, ATTN_HD)
    return jnp.tile(cos, (1, heads)), jnp.tile(sin, (1, heads))


def _identity_rope(L, heads):
    return jnp.ones((L, heads * ATTN_HD), F32), jnp.zeros((L, heads * ATTN_HD), F32)


def _block_diag_ones(width):
    i = np.arange(width) // ATTN_HD
    return jnp.asarray((i[:, None] == i[None, :]).astype(np.float32)).astype(BF16)


def kernel(x, c, ctx, c_ctx, w_ada, b_ada, w_in, gla_wa_f, gla_ba_f, gla_wa_b, gla_ba_b, gla_norm,
           attn_qnorm, attn_knorm, hy_conv_w, hy_conv_b, hy_f1_w, hy_f1_b, hy_f1_freq, hy_f2_w,
           hy_f2_b, hy_f2_freq, hy_f3_w, hy_skip, w_br_gla, w_br_attn, w_br_hy, w_out, final_norm):
    b_, L, d = x.shape
    Lc = ctx.shape[1]
    depth = w_ada.shape[0]

    cos_q, sin_q = _rope_tables(L, ATTN_HEADS)
    cos_k, sin_k = cos_q[:, :ATTN_KVW], sin_q[:, :ATTN_KVW]
    cos_qc, sin_qc = _identity_rope(Lc, ATTN_HEADS)
    cos_kc, sin_kc = cos_qc[:, :ATTN_KVW], sin_qc[:, :ATTN_KVW]
    bdq, bdk = _block_diag_ones(ATTN_QW), _block_diag_ones(ATTN_KVW)
    eye_v = jnp.eye(ATTN_KVW, dtype=BF16)
    rr = np.arange(2 * ATTN_QW)
    src = (rr // ATTN_HD % 2) * ATTN_QW + (rr // (2 * ATTN_HD)) * ATTN_HD + rr % ATTN_HD
    eye_q = jnp.asarray((src[:, None] == np.arange(2 * ATTN_QW)[None, :]).astype(np.float32)).astype(BF16)
    gw = ATTN_GROUP * ATTN_HD
    e_heads = jnp.stack([jnp.eye(ATTN_HD, gw, k=h * ATTN_HD, dtype=BF16) for h in range(ATTN_GROUP)])
    n1, n2 = _fft_sizes(2 * L)
    n1c, n2c = _fft_sizes(2 * Lc)
    tabs = _fft_tables(n1, n2)
    tabs_c = _fft_tables(n1c, n2c)
    zero_state = jnp.zeros((b_, GLA_DV, GLA_KW), F32)

    cmat = jnp.concatenate([c, c_ctx[None], jnp.zeros((8 - b_ - 1, d), F32)], axis=0)
    mods = _ada_call(cmat, w_ada, b_ada)

    for l in range(depth):
        need_ctx = l < depth - 1
        shift, scale, gate = [m[:b_, None, :] for m in jnp.split(mods[l], 3, axis=-1)]
        shift_c, scale_c, gate_c = [jnp.broadcast_to(m[b_:b_ + 1, None, :], (b_, 1, d))
                                    for m in jnp.split(mods[l], 3, axis=-1)]
        w_packed = _pack_w_in(w_in[l])
        P = _proj_call(x, scale, shift, w_packed)
        Pc = _proj_call(ctx, scale_c, shift_c, w_packed)

        rk = GLA_RANK
        wa_f = jnp.zeros((LANES, GLA_KW), F32).at[0:rk].set(gla_wa_f[l])
        wa_b = jnp.zeros((LANES, GLA_KW), F32).at[rk:2 * rk].set(gla_wa_b[l])
        ba_f, ba_b = gla_ba_f[l][None], gla_ba_b[l][None]
        nw = gla_norm[l][None]
        oc_f, sc_f = _gla_call(Pc, wa_f, ba_f, zero_state, reverse=False)
        yc_gla, sc_b = _gla_call(Pc, wa_b, ba_b, zero_state, reverse=True, fin=(oc_f, nw))
        o_f, _ = _gla_call(P, wa_f, ba_f, sc_f, reverse=False)
        y_gla, _ = _gla_call(P, wa_b, ba_b, sc_b, reverse=True, fin=(o_f, nw))

        gq = jnp.tile(attn_qnorm[l], ATTN_HEADS)[None]
        gk = jnp.tile(attn_knorm[l], ATTN_KV_HEADS)[None]
        q_a, k_a, v_a = _qkv_prep_call(P, cos_q, sin_q, cos_k, sin_k, gq, gk, bdq, bdk, eye_q, eye_v)
        qc_a, kc_a, vc_a = _qkv_prep_call(Pc, cos_qc, sin_qc, cos_kc, sin_kc, gq, gk, bdq, bdk,
                                          eye_q, eye_v)
        y_attn = _attn_call(q_a, jnp.concatenate([k_a, kc_a], axis=2),
                            jnp.concatenate([v_a, vc_a], axis=2), P, e_heads)

        fp = (hy_f1_w[l], hy_f1_b[l], hy_f1_freq[l], hy_f2_w[l], hy_f2_b[l], hy_f2_freq[l], hy_f3_w[l])
        g_spec = _hyena_filter_spectrum(L, tabs, n1, n2, fp)
        v0, x1, x2 = _short_conv_call(P, hy_conv_w[l], hy_conv_b[l][None])
        z1 = _hyena_conv(v0, x1, g_spec, 0, hy_skip[l, 0], tabs, n1, n2)
        y_hy = _hyena_conv(z1, x2, g_spec, 1, hy_skip[l, 1], tabs, n1, n2)

        wg, wa, wh, wo = (w_br_gla[l].astype(BF16), w_br_attn[l].astype(BF16),
                          w_br_hy[l].astype(BF16), w_out[l].astype(BF16))
        fn = final_norm[None]
        x_new = _merge_call(y_gla, y_attn, y_hy, P, x, gate, wg, wa, wh, wo, fn, final=not need_ctx)

        if need_ctx:
            yc_attn = _attn_call(qc_a, kc_a, vc_a, Pc, e_heads)
            gc_spec = _hyena_filter_spectrum(Lc, tabs_c, n1c, n2c, fp)
            vc0, xc1, xc2 = _short_conv_call(Pc, hy_conv_w[l], hy_conv_b[l][None])
            zc1 = _hyena_conv(vc0, xc1, gc_spec, 0, hy_skip[l, 0], tabs_c, n1c, n2c)
            yc_hy = _hyena_conv(zc1, xc2, gc_spec, 1, hy_skip[l, 1], tabs_c, n1c, n2c)
            ctx = _merge_call(yc_gla, yc_attn, yc_hy, Pc, ctx, gate_c, wg, wa, wh, wo, fn, final=False)
        x = x_new

    return x
```

```python
import functools
import math

import numpy as np
import jax
import jax.numpy as jnp
from jax import lax
from jax.experimental import pallas as pl
from jax.experimental.pallas import tpu as pltpu

F32 = jnp.float32
BF16 = jnp.bfloat16
F8 = jnp.float8_e4m3fn

D_MODEL = 1024
GRID_W = 64
BRANCH_W = D_MODEL // 2
N_BRANCH = 3
EPS = 1e-6
GLA_HEADS = 4
GLA_DV = BRANCH_W // GLA_HEADS
GLA_DK = GLA_DV // 2
GLA_KW = GLA_HEADS * GLA_DK
GLA_VW = GLA_HEADS * GLA_DV
GLA_RANK = 16
GLA_GATE_NORM = 16.0
GLA_CHUNK = 64
GLA_BLOCK = 256
ATTN_HD = 64
ATTN_HEADS = BRANCH_W // ATTN_HD
ATTN_KV_HEADS = ATTN_HEADS // 4
ATTN_GROUP = ATTN_HEADS // ATTN_KV_HEADS
ATTN_QW = ATTN_HEADS * ATTN_HD
ATTN_KVW = ATTN_KV_HEADS * ATTN_HD
ROPE_THETA = 10000.0
HY_W = BRANCH_W
HY_ORDER = 2
HY_EMB = 33
HY_BANDS = (HY_EMB - 1) // 2
HY_FFN = 64
HY_SHORT = 3
HY_MOD_SHIFT = 0.05
HY_DECAY_SHORT_PCT = 0.3
HY_DECAY_LONG_PCT = 1.5
HY_DECAY_TARGET = 1e-2
N_FILT = 2 * HY_ORDER * HY_W
SPLITS = (GLA_KW, GLA_KW, GLA_VW, GLA_VW, GLA_RANK, GLA_RANK,
          ATTN_QW, ATTN_KVW, ATTN_KVW, ATTN_QW,
          (HY_ORDER + 1) * HY_W, HY_W,
          N_BRANCH * D_MODEL)

LANES = 128
VMEM_LIMIT = 48 * 1024 * 1024
FFT_D_VMEM_LIMIT = 56 * 1024 * 1024

COLS = {
    'm': (0, 3072), 'y_u': (3072, 1536), 'g_v': (4608, 512), 'g_z': (5120, 512),
    'a_q': (5632, 512), 'a_z': (6144, 512), 'y_z': (6656, 512), 'g_q': (7168, 256),
    'g_k': (7424, 256), 'a_k': (7680, 128), 'a_v': (7808, 128), 'g_a': (7936, 128),
    'g_a_lo': (8064, 128),
}
N_PROJ = 8192
PROJ_TN = 2048


def _cparams(*sem):
    return pltpu.CompilerParams(dimension_semantics=sem, vmem_limit_bytes=VMEM_LIMIT)


def _split(a):
    hi = a.astype(BF16)
    lo = (a - hi.astype(F32)).astype(BF16)
    return hi, lo


def _dot(a, b):
    return jnp.dot(a, b, preferred_element_type=F32)


def _dot_hp(a, b):
    ah, al = _split(a)
    bh, bl = _split(b)
    return _dot(ah, bh) + _dot(al, bh) + _dot(ah, bl)


def _dot_mat(mh, ml, x):
    xh = x.astype(BF16)
    return _dot(mh, xh) + _dot(ml, xh)


def _sigmoid(x):
    return 1.0 / (1.0 + jnp.exp(-x))


def _silu(x):
    return x * _sigmoid(x)


def _ada_kernel(c_ref, w_ref, b_ref, o_ref):
    o_ref[0] = _dot_hp(_silu(c_ref[...]), w_ref[0]) + b_ref[0]


def _ada_call(cmat, w_ada, b_ada):
    depth, d, n3 = w_ada.shape
    tn = 1024
    return pl.pallas_call(
        _ada_kernel,
        grid=(depth, n3 // tn),
        in_specs=[pl.BlockSpec((8, d), lambda l, j: (0, 0)),
                  pl.BlockSpec((1, d, tn), lambda l, j: (l, 0, j)),
                  pl.BlockSpec((1, 1, tn), lambda l, j: (l, 0, j))],
        out_specs=pl.BlockSpec((1, 8, tn), lambda l, j: (l, 0, j)),
        out_shape=jax.ShapeDtypeStruct((depth, 8, n3), F32),
        compiler_params=_cparams("arbitrary", "arbitrary"),
        name="ada",
    )(cmat, w_ada, b_ada.reshape(depth, 1, n3))


def _proj_kernel(x_ref, sc_ref, sh_ref, w_ref, o_ref, *, res_tile, hi_off, lo_off):
    x = x_ref[0]
    ms = jnp.mean(x * x, axis=-1, keepdims=True)
    h = x * lax.rsqrt(ms + EPS) * (1.0 + sc_ref[0]) + sh_ref[0]
    res = _dot(h.astype(BF16), w_ref[...])
    out = res.astype(o_ref.dtype)
    o_ref[0] = out

    @pl.when(pl.program_id(0) == res_tile)
    def _():
        hi = slice(hi_off, hi_off + LANES)
        o_ref[0, :, lo_off:lo_off + LANES] = (res[:, hi] - out[:, hi].astype(F32)).astype(o_ref.dtype)


def _proj_call(x, scale, shift, w):
    b_, L, d = x.shape
    n = w.shape[1]
    tm = min(1024, L)
    tn = PROJ_TN
    hi0, lo0 = COLS['g_a'][0], COLS['g_a_lo'][0]
    assert hi0 // tn == lo0 // tn
    return pl.pallas_call(
        functools.partial(_proj_kernel, res_tile=hi0 // tn, hi_off=hi0 % tn, lo_off=lo0 % tn),
        grid=(n // tn, b_, L // tm),
        in_specs=[pl.BlockSpec((1, tm, d), lambda j, b, i: (b, i, 0)),
                  pl.BlockSpec((1, 1, d), lambda j, b, i: (b, 0, 0)),
                  pl.BlockSpec((1, 1, d), lambda j, b, i: (b, 0, 0)),
                  pl.BlockSpec((d, tn), lambda j, b, i: (0, j))],
        out_specs=pl.BlockSpec((1, tm, tn), lambda j, b, i: (b, i, j)),
        out_shape=jax.ShapeDtypeStruct((b_, L, n), BF16),
        compiler_params=_cparams("arbitrary", "arbitrary", "arbitrary"),
        name="proj",
    )(x, scale, shift, w)


def _gla_kernel(*refs, reverse, finalize, nchunks):
    if finalize:
        (q_ref, k_ref, v_ref, ga_ref, gal_ref, wa_ref, ba_ref, s0_ref, tri_ref, op_ref, z_ref, nw_ref,
         o_ref, sfin_ref, s_scr) = refs
    else:
        (q_ref, k_ref, v_ref, ga_ref, gal_ref, wa_ref, ba_ref, s0_ref, tri_ref,
         o_ref, sfin_ref, s_scr) = refs
    C = GLA_CHUNK
    T = nchunks * C
    nt_dims = (((1,), (1,)), ((), ()))
    tn_dims = (((0,), (0,)), ((), ()))

    @pl.when(pl.program_id(1) == 0)
    def _():
        s_scr[...] = s0_ref[0]

    ga = ga_ref[0].astype(F32) + gal_ref[0].astype(F32)
    xg = _dot_hp(ga, wa_ref[...]) + ba_ref[...]
    g = (jnp.minimum(xg, 0.0) - jnp.log(1.0 + jnp.exp(-jnp.abs(xg)))) * (1.0 / GLA_GATE_NORM)

    tri = tri_ref[...]
    g1 = g.astype(BF16)
    r1 = g - g1.astype(F32)
    g2 = r1.astype(BF16)
    g3 = (r1 - g2.astype(F32)).astype(BF16)
    b = _dot(tri, g1) + _dot(tri, g2) + _dot(tri, g3)

    def chunk_row(r):
        return jnp.concatenate([jnp.broadcast_to(b[c * C + r:c * C + r + 1], (C, GLA_KW))
                                for c in range(nchunks)], axis=0)

    tot_row = 0 if reverse else C - 1
    bm = chunk_row(C // 2)
    bt = chunk_row(tot_row)
    ri = lax.broadcasted_iota(jnp.int32, (T, T), 0)
    ci = lax.broadcasted_iota(jnp.int32, (T, T), 1)
    same_chunk = (ri // C) == (ci // C)
    mask = same_chunk & ((ci >= ri) if reverse else (ci <= ri))
    lane = lax.broadcasted_iota(jnp.int32, (1, GLA_KW), 1)
    hmask = [(lane >= h * GLA_DK) & (lane < (h + 1) * GLA_DK) for h in range(GLA_HEADS)]

    q = q_ref[0].astype(F32) * (GLA_DK ** -0.5)
    k = k_ref[0].astype(F32)
    v = v_ref[0].astype(BF16)
    qa = q * jnp.exp(b - bm)
    kb = (k * jnp.exp(bm - b)).astype(BF16)
    qe = q * jnp.exp(b)
    kd = (k * jnp.exp(bt - b)).astype(BF16)

    intra = []
    for h in range(GLA_HEADS):
        qa_h = jnp.where(hmask[h], qa, 0.0).astype(BF16)
        att = lax.dot_general(qa_h, kb, nt_dims, preferred_element_type=F32)
        att = jnp.where(mask, att, 0.0).astype(BF16)
        intra.append(_dot(att, v[:, h * GLA_DV:(h + 1) * GLA_DV]))

    inter = [None] * nchunks
    order = range(nchunks - 1, -1, -1) if reverse else range(nchunks)
    for c in order:
        rows = slice(c * C, (c + 1) * C)
        st = s_scr[...]
        st_b = st.astype(BF16)
        dec = jnp.exp(b[c * C + tot_row:c * C + tot_row + 1])
        upd = jnp.zeros_like(st)
        outs = []
        for h in range(GLA_HEADS):
            qe_h = jnp.where(hmask[h], qe[rows], 0.0).astype(BF16)
            outs.append(lax.dot_general(qe_h, st_b, nt_dims, preferred_element_type=F32))
            u_h = lax.dot_general(v[rows, h * GLA_DV:(h + 1) * GLA_DV], kd[rows], tn_dims,
                                  preferred_element_type=F32)
            upd = upd + jnp.where(hmask[h], u_h, 0.0)
        s_scr[...] = dec * st + upd
        inter[c] = jnp.concatenate(outs, axis=1)
    o = jnp.concatenate(intra, axis=1) + jnp.concatenate(inter, axis=0)

    if finalize:
        o = o + op_ref[0]
        parts = []
        for h in range(GLA_HEADS):
            oh = o[:, h * GLA_DV:(h + 1) * GLA_DV]
            ms = jnp.mean(oh * oh, axis=-1, keepdims=True)
            parts.append(oh * lax.rsqrt(ms + EPS) * nw_ref[...])
        y = jnp.concatenate(parts, axis=1) * _silu(z_ref[0].astype(F32))
        o_ref[0] = y.astype(o_ref.dtype)
    else:
        o_ref[0] = o

    @pl.when(pl.program_id(1) == pl.num_programs(1) - 1)
    def _():
        sfin_ref[0] = s_scr[...]


def _gla_call(P, wa_pad, ba, s0, reverse, fin=None):
    b_, L, _ = P.shape
    T = min(GLA_BLOCK, L)
    nt = L // T
    if reverse:
        tmap = lambda t: nt - 1 - t
    else:
        tmap = lambda t: t
    ii = np.arange(T)
    same = (ii[:, None] // GLA_CHUNK) == (ii[None, :] // GLA_CHUNK)
    tri = same & ((ii[None, :] >= ii[:, None]) if reverse else (ii[None, :] <= ii[:, None]))
    tri = jnp.asarray(tri.astype(np.float32)).astype(BF16)

    def col(name, width):
        blk = COLS[name][0] // width
        return pl.BlockSpec((1, T, width), lambda b, t: (b, tmap(t), blk))

    in_specs = [col('g_q', GLA_KW), col('g_k', GLA_KW), col('g_v', GLA_VW),
                col('g_a', LANES), col('g_a_lo', LANES),
                pl.BlockSpec((LANES, GLA_KW), lambda b, t: (0, 0)),
                pl.BlockSpec((1, GLA_KW), lambda b, t: (0, 0)),
                pl.BlockSpec((1, GLA_DV, GLA_KW), lambda b, t: (b, 0, 0)),
                pl.BlockSpec((T, T), lambda b, t: (0, 0))]
    args = [P, P, P, P, P, wa_pad, ba, s0, tri]
    if fin is not None:
        o_prev, nw = fin
        in_specs += [pl.BlockSpec((1, T, GLA_VW), lambda b, t: (b, tmap(t), 0)),
                     col('g_z', GLA_VW),
                     pl.BlockSpec((1, GLA_DV), lambda b, t: (0, 0))]
        args += [o_prev, P, nw]
    out_dtype = BF16 if fin is not None else F32
    return pl.pallas_call(
        functools.partial(_gla_kernel, reverse=reverse, finalize=fin is not None, nchunks=T // GLA_CHUNK),
        grid=(b_, nt),
        in_specs=in_specs,
        out_specs=[pl.BlockSpec((1, T, GLA_VW), lambda b, t: (b, tmap(t), 0)),
                   pl.BlockSpec((1, GLA_DV, GLA_KW), lambda b, t: (b, 0, 0))],
        out_shape=[jax.ShapeDtypeStruct((b_, L, GLA_VW), out_dtype),
                   jax.ShapeDtypeStruct((b_, GLA_DV, GLA_KW), F32)],
        scratch_shapes=[pltpu.VMEM((GLA_DV, GLA_KW), F32)],
        compiler_params=_cparams("arbitrary", "arbitrary"),
        name="gla",
    )(*args)


def _norm_rope(a, gain, bd, cos, sin):
    sq = a * a
    sh, sl = _split(sq)
    ss = _dot(sh, bd) + _dot(sl, bd)
    an = a * lax.rsqrt(ss * (1.0 / ATTN_HD) + EPS) * gain
    w = a.shape[-1]
    lane = lax.broadcasted_iota(jnp.int32, an.shape, 1)
    partner = jnp.where(lane % 2 == 0, pltpu.roll(an, w - 1, 1), pltpu.roll(an, 1, 1))
    return an * cos + partner * sin


Q_SCALE = (ATTN_HD ** -0.5) * math.log2(math.e)
ATTN_QK_DEPTH = 4 * ATTN_HD
ATTN_TK_MAX = 640


def _split8(a):
    hi = a.astype(F8).astype(F32)
    lo = (a - hi).astype(F8)
    return hi.astype(BF16), lo.astype(BF16)


def _qkv_prep_kernel(q_ref, k_ref, v_ref, cq_ref, sq_ref, ck_ref, sk_ref, gq_ref, gk_ref,
                     bdq_ref, bdk_ref, qo_ref, ko_ref, vo_ref):
    hd = ATTN_HD
    q = _norm_rope(q_ref[0].astype(F32), gq_ref[...], bdq_ref[...], cq_ref[...], sq_ref[...]) * Q_SCALE
    qh, ql = _split8(q)
    qt = jnp.concatenate([qh, ql], axis=1).T
    for h in range(ATTN_HEADS):
        hi = qt[hd * h:hd * (h + 1)]
        lo = qt[ATTN_QW + hd * h:ATTN_QW + hd * (h + 1)]
        qo_ref[0, ATTN_QK_DEPTH * h:ATTN_QK_DEPTH * (h + 1), :] = jnp.concatenate([hi, hi, lo, lo], axis=0).astype(F8)
    k = _norm_rope(k_ref[0].astype(F32), gk_ref[...], bdk_ref[...], ck_ref[...], sk_ref[...])
    kh, kl = _split8(k)
    for g in range(ATTN_KV_HEADS):
        cols = slice(g * hd, (g + 1) * hd)
        ko_ref[0, g] = jnp.concatenate([kh[:, cols], kl[:, cols], kh[:, cols], kl[:, cols]], axis=1).astype(F8)
    vo_ref[0] = v_ref[0].astype(BF16).T


def _qkv_prep_call(P, cos_q, sin_q, cos_k, sin_k, gq, gk, bdq, bdk):
    b_, L, _ = P.shape
    T = min(1024, L)

    def col(name, width):
        blk = COLS[name][0] // width
        return pl.BlockSpec((1, T, width), lambda b, t: (b, t, blk))

    tab = lambda w: pl.BlockSpec((T, w), lambda b, t: (t, 0))
    const = lambda r, w: pl.BlockSpec((r, w), lambda b, t: (0, 0))
    return pl.pallas_call(
        _qkv_prep_kernel,
        grid=(b_, L // T),
        in_specs=[col('a_q', ATTN_QW), col('a_k', ATTN_KVW), col('a_v', ATTN_KVW),
                  tab(ATTN_QW), tab(ATTN_QW), tab(ATTN_KVW), tab(ATTN_KVW),
                  const(1, ATTN_QW), const(1, ATTN_KVW),
                  const(ATTN_QW, ATTN_QW), const(ATTN_KVW, ATTN_KVW)],
        out_specs=[pl.BlockSpec((1, ATTN_HEADS * ATTN_QK_DEPTH, T), lambda b, t: (b, 0, t)),
                   pl.BlockSpec((1, ATTN_KV_HEADS, T, ATTN_QK_DEPTH), lambda b, t: (b, 0, t, 0)),
                   pl.BlockSpec((1, ATTN_KVW, T), lambda b, t: (b, 0, t))],
        out_shape=[jax.ShapeDtypeStruct((b_, ATTN_HEADS * ATTN_QK_DEPTH, L), F8),
                   jax.ShapeDtypeStruct((b_, ATTN_KV_HEADS, L, ATTN_QK_DEPTH), F8),
                   jax.ShapeDtypeStruct((b_, ATTN_KVW, L), BF16)],
        compiler_params=_cparams("arbitrary", "arbitrary"),
        name="qkv_prep",
    )(P, P, P, cos_q, sin_q, cos_k, sin_k, gq, gk, bdq, bdk)


ATTN_SUM_ROWS = 16
ATTN_UNROLL = 2
ATTN_TQ = 512


def _attn_kernel(qt_ref, k_ref, vt_ref, z_ref, e_ref, o_ref, m_scr, a_scr, t_scr, acc_scr, s_scr, p_scr,
                 *, tk, nk):
    m_scr[...] = jnp.full(m_scr.shape, -jnp.inf, F32)
    acc_scr[...] = jnp.zeros(acc_scr.shape, F32)
    ones = jnp.ones((ATTN_SUM_ROWS, tk), BF16)

    def scores(j, slot):
        start = j * tk if isinstance(j, int) else pl.multiple_of(j * tk, tk)
        kj = k_ref[0, 0, pl.ds(start, tk), :]
        for h in range(ATTN_GROUP):
            s = _dot(kj, qt_ref[0, h * ATTN_QK_DEPTH:(h + 1) * ATTN_QK_DEPTH, :])
            s_scr[slot, h] = s
            t_scr[slot, h] = jnp.max(s, axis=0, keepdims=True)

    scores(0, 0)

    def step(j, slot):
        scores(min(j + 1, nk - 1) if isinstance(j, int) else jnp.minimum(j + 1, nk - 1), 1 - slot)
        start = j * tk if isinstance(j, int) else pl.multiple_of(j * tk, tk)
        vte = jnp.concatenate([vt_ref[0, :, pl.ds(start, tk)], ones], axis=0)
        for h in range(ATTN_GROUP):
            m_old = m_scr[h]
            m_new = jnp.maximum(m_old, t_scr[slot, h])
            a_scr[h] = jnp.exp2(m_old - m_new)
            m_scr[h] = m_new
            p_scr[h] = jnp.exp2(s_scr[slot, h] - m_new).astype(BF16)
        for h in range(ATTN_GROUP):
            acc_scr[h] = a_scr[h] * acc_scr[h] + _dot(vte, p_scr[h])

    def trip(i, carry):
        for u in range(ATTN_UNROLL):
            step(ATTN_UNROLL * i + u, u % 2)
        return carry

    ntrips = nk // ATTN_UNROLL
    lax.fori_loop(0, ntrips, trip, 0)
    for j in range(ntrips * ATTN_UNROLL, nk):
        step(j, j % 2)
    tn = (((0,), (0,)), ((), ()))
    out = None
    for h in range(ATTN_GROUP):
        acc = acc_scr[h]
        oh, ol = _split(acc[:ATTN_HD] / acc[ATTN_HD:ATTN_HD + 1])
        part = (lax.dot_general(oh, e_ref[h], tn, preferred_element_type=F32)
                + lax.dot_general(ol, e_ref[h], tn, preferred_element_type=F32))
        out = part if out is None else out + part
    o_ref[0] = (out * _silu(z_ref[0].astype(F32))).astype(o_ref.dtype)


def _attn_call(qt, k, vt, P, e_heads):
    b_, _, L = qt.shape
    Lk = k.shape[2]
    tq = min(ATTN_TQ, L)
    tk = max(t for t in range(LANES, ATTN_TK_MAX + 1, LANES) if Lk % t == 0)
    gw = ATTN_GROUP * ATTN_HD
    zblk = COLS['a_z'][0] // gw
    return pl.pallas_call(
        functools.partial(_attn_kernel, tk=tk, nk=Lk // tk),
        grid=(b_, ATTN_KV_HEADS, L // tq),
        in_specs=[pl.BlockSpec((1, ATTN_GROUP * ATTN_QK_DEPTH, tq), lambda b, g, i: (b, g, i)),
                  pl.BlockSpec((1, 1, Lk, ATTN_QK_DEPTH), lambda b, g, i: (b, g, 0, 0)),
                  pl.BlockSpec((1, ATTN_HD, Lk), lambda b, g, i: (b, g, 0)),
                  pl.BlockSpec((1, tq, gw), lambda b, g, i: (b, i, zblk + g)),
                  pl.BlockSpec((ATTN_GROUP, ATTN_HD, gw), lambda b, g, i: (0, 0, 0))],
        out_specs=pl.BlockSpec((1, tq, gw), lambda b, g, i: (b, i, g)),
        out_shape=jax.ShapeDtypeStruct((b_, L, ATTN_QW), BF16),
        scratch_shapes=[pltpu.VMEM((ATTN_GROUP, 1, tq), F32), pltpu.VMEM((ATTN_GROUP, 1, tq), F32),
                        pltpu.VMEM((2, ATTN_GROUP, 1, tq), F32),
                        pltpu.VMEM((ATTN_GROUP, ATTN_HD + ATTN_SUM_ROWS, tq), F32),
                        pltpu.VMEM((2, ATTN_GROUP, tk, tq), F32), pltpu.VMEM((ATTN_GROUP, tk, tq), BF16)],
        compiler_params=_cparams("arbitrary", "arbitrary", "arbitrary"),
        name="attn",
    )(qt, k, vt, P, e_heads)


def _short_conv_kernel(u_ref, up_ref, un_ref, w_ref, b_ref, v_ref, x1_ref, x2_ref):
    t = pl.program_id(1)
    nt = pl.num_programs(1)
    u = u_ref[0].astype(F32)
    T = u.shape[0]
    hr = up_ref.shape[1]
    prev_row = jnp.where(t > 0, up_ref[0, hr - 1:hr, :].astype(F32), 0.0)
    next_row = jnp.where(t < nt - 1, un_ref[0, 0:1, :].astype(F32), 0.0)
    row = lax.broadcasted_iota(jnp.int32, u.shape, 0)
    u_m1 = jnp.where(row == 0, prev_row, pltpu.roll(u, 1, 0))
    u_p1 = jnp.where(row == T - 1, next_row, pltpu.roll(u, T - 1, 0))
    w = w_ref[...]
    out = b_ref[...] + u_m1 * w[0:1] + u * w[1:2] + u_p1 * w[2:3]
    v_ref[0] = out[:, 0:HY_W]
    x1_ref[0] = out[:, HY_W:2 * HY_W]
    x2_ref[0] = out[:, 2 * HY_W:3 * HY_W]


def _short_conv_call(P, w, bias):
    b_, L, _ = P.shape
    T = min(1024, L)
    cw = (HY_ORDER + 1) * HY_W
    blk = COLS['y_u'][0] // cw
    hr = 16
    hb = T // hr
    nh = L // hr
    out_spec = pl.BlockSpec((1, T, HY_W), lambda b, t: (b, t, 0))
    shp = jax.ShapeDtypeStruct((b_, L, HY_W), F32)
    return pl.pallas_call(
        _short_conv_kernel,
        grid=(b_, L // T),
        in_specs=[pl.BlockSpec((1, T, cw), lambda b, t: (b, t, blk)),
                  pl.BlockSpec((1, hr, cw), lambda b, t: (b, jnp.maximum(t * hb - 1, 0), blk)),
                  pl.BlockSpec((1, hr, cw), lambda b, t: (b, jnp.minimum((t + 1) * hb, nh - 1), blk)),
                  pl.BlockSpec((HY_SHORT, cw), lambda b, t: (0, 0)),
                  pl.BlockSpec((1, cw), lambda b, t: (0, 0))],
        out_specs=[out_spec, out_spec, out_spec],
        out_shape=[shp, shp, shp],
        compiler_params=_cparams("arbitrary", "arbitrary"),
        name="short_conv",
    )(P, P, P, w, bias)


FFT_ROWS = 8


def _filter_kernel(emb_ref, w1_ref, b1_ref, f1_ref, w2_ref, b2_ref, f2_ref, w3_ref, dl_ref,
                   hf_ref, hb_ref, sum_ref):
    t = pl.program_id(0)
    emb = emb_ref[...]
    h = jnp.sin(f1_ref[...] * (_dot_hp(emb, w1_ref[...]) + b1_ref[...]))
    h = jnp.sin(f2_ref[...] * (_dot_hp(h, w2_ref[...]) + b2_ref[...]))
    hh, hl = _split(h)
    tt = emb[:, 0:1]

    @pl.when(t == 0)
    def _():
        sum_ref[...] = jnp.zeros_like(sum_ref)

    lag0 = (lax.broadcasted_iota(jnp.int32, (h.shape[0], HY_W), 0) + t * h.shape[0]) == 0
    for o in range(HY_ORDER):
        for d, out_ref in enumerate((hf_ref, hb_ref)):
            cols = slice((2 * o + d) * HY_W, (2 * o + d + 1) * HY_W)
            wh, wl = _split(w3_ref[:, cols])
            f = _dot(hh, wh) + _dot(hl, wh) + _dot(hh, wl)
            f = f * (jnp.exp(-tt * dl_ref[:, cols]) + HY_MOD_SHIFT)
            sum_ref[:, cols] += jnp.broadcast_to(jnp.sum(jnp.abs(f), axis=0, keepdims=True), (8, HY_W))
            if d == 1:
                f = jnp.where(lag0, 0.0, f)
            out_ref[o] = pltpu.einshape("(an)c->a(nc)", f, a=FFT_ROWS)


def _filter_call(emb, w1, b1, f1, w2, b2, f2, w3, deltas, n1, n2):
    L = emb.shape[0]
    TL = FFT_ROWS * n2
    const = lambda r, w: pl.BlockSpec((r, w), lambda t: (0, 0))
    hspec = pl.BlockSpec((HY_ORDER, FFT_ROWS, n2 * HY_W), lambda t: (0, t, 0))
    hshape = jax.ShapeDtypeStruct((HY_ORDER, n1 // 2, n2 * HY_W), F32)
    return pl.pallas_call(
        _filter_kernel,
        grid=(L // TL,),
        in_specs=[pl.BlockSpec((TL, LANES), lambda t: (t, 0)),
                  const(LANES, HY_FFN), const(1, HY_FFN), const(1, HY_FFN),
                  const(HY_FFN, HY_FFN), const(1, HY_FFN), const(1, HY_FFN),
                  const(HY_FFN, N_FILT), const(1, N_FILT)],
        out_specs=[hspec, hspec, pl.BlockSpec((8, N_FILT), lambda t: (0, 0))],
        out_shape=[hshape, hshape, jax.ShapeDtypeStruct((8, N_FILT), F32)],
        compiler_params=_cparams("arbitrary"),
        name="hy_filter",
    )(emb, w1, b1, f1, w2, b2, f2, w3, deltas)


def _fft_sizes(n):
    lg = int(round(math.log2(n)))
    assert 1 << lg == n
    n1 = 1 << ((lg + 1) // 2)
    return n1, n // n1


def _bf16_pair(a):
    a = np.asarray(a, np.float32)
    hi = jnp.asarray(a, F32).astype(BF16)
    lo = (jnp.asarray(a, F32) - hi.astype(F32)).astype(BF16)
    return hi, lo


def _fft_tables(n1, n2):
    n = n1 * n2
    h1 = n1 // 2
    k1 = np.arange(n1)[:, None]
    a = 2.0 * np.pi * ((k1 * np.arange(h1)[None, :]) % n1) / n1
    c, s = np.cos(a), np.sin(a)
    fa = np.block([[c, s], [-s, c]])
    fa_real = np.concatenate([c, -s], axis=0)
    fd = np.block([[c.T, -s.T], [s.T, c.T]])
    k2 = np.arange(n2)[:, None]
    b = 2.0 * np.pi * ((k2 * np.arange(n2)[None, :]) % n2) / n2
    cb, sb = np.cos(b), np.sin(b)
    fb = np.block([[cb, sb], [-sb, cb]])
    fc = np.block([[cb, -sb], [sb, cb]])
    kn = (jnp.arange(n1, dtype=jnp.int32)[:, None] * jnp.arange(n2, dtype=jnp.int32)[None, :]) % n
    tw = kn.astype(F32) * (2.0 * math.pi / n)
    twr = jnp.broadcast_to(jnp.cos(tw)[:, :, None], (n1, n2, LANES))
    twi = jnp.broadcast_to(-jnp.sin(tw)[:, :, None], (n1, n2, LANES))
    return dict(fa=_bf16_pair(fa), fa_real=_bf16_pair(fa_real), fd=_bf16_pair(fd),
                fb=_bf16_pair(fb), fc=_bf16_pair(fc), twr=twr, twi=twi)


def _fa_kernel(x_ref, mh_ref, ml_ref, o_ref, xh_scr, *, packed, part_axis):
    @pl.when(pl.program_id(part_axis) == 0)
    def _():
        if packed:
            x = x_ref[...]
            x = x.reshape(x.shape[0] * x.shape[1], x.shape[2], x.shape[3])
            x = pltpu.einshape("rnc->r(nc)", x)
        else:
            x = x_ref[0]
        xh_scr[...] = x.astype(BF16)

    xh = xh_scr[...]
    r = _dot(mh_ref[...], xh) + _dot(ml_ref[...], xh)
    o_ref[0] = pltpu.einshape("r(nc)->rnc", r, n=FFT_ROWS)


def _fa_call(x4, mats, n1):
    mh, ml = mats
    _, h1, n2, wd = x4.shape
    wt = FFT_ROWS * wd
    return pl.pallas_call(
        functools.partial(_fa_kernel, packed=True, part_axis=1),
        grid=(n2 // FFT_ROWS, 2),
        in_specs=[pl.BlockSpec((2, h1, FFT_ROWS, wd), lambda j, p: (0, 0, j, 0)),
                  pl.BlockSpec((n1, n1), lambda j, p: (p, 0)),
                  pl.BlockSpec((n1, n1), lambda j, p: (p, 0))],
        out_specs=pl.BlockSpec((1, n1, FFT_ROWS, wd), lambda j, p: (p, 0, j, 0)),
        out_shape=jax.ShapeDtypeStruct((2, n1, n2, wd), F32),
        scratch_shapes=[pltpu.VMEM((n1, wt), BF16)],
        compiler_params=_cparams("arbitrary", "arbitrary"),
        name="fft_a",
    )(x4, mh, ml)


def _fa_real_call(x, mats, n1):
    mh, ml = mats
    ng, h1, lanes = x.shape
    wt = FFT_ROWS * HY_W
    n2 = lanes // HY_W
    out = pl.pallas_call(
        functools.partial(_fa_kernel, packed=False, part_axis=2),
        grid=(ng, lanes // wt, 2),
        in_specs=[pl.BlockSpec((1, h1, wt), lambda g, j, p: (g, 0, j)),
                  pl.BlockSpec((n1, h1), lambda g, j, p: (p, 0)),
                  pl.BlockSpec((n1, h1), lambda g, j, p: (p, 0))],
        out_specs=pl.BlockSpec((1, n1, FFT_ROWS, HY_W), lambda g, j, p: (2 * g + p, 0, j, 0)),
        out_shape=jax.ShapeDtypeStruct((2 * ng, n1, n2, HY_W), F32),
        scratch_shapes=[pltpu.VMEM((h1, wt), BF16)],
        compiler_params=_cparams("arbitrary", "arbitrary", "arbitrary"),
        name="fft_a_real",
    )(x, mh, ml)
    return out.reshape(ng, 2, n1, n2, HY_W)


def _lane_tile(a, width):
    return jnp.concatenate([a] * (width // a.shape[-1]), axis=-1)


def _fb_filter_kernel(tf_ref, tb_ref, twr_ref, twi_ref, fh_ref, fl_ref, sf_ref, sb_ref, g_ref, *, n2, scale):
    w = g_ref.shape[-1]
    inv = jnp.concatenate([scale / sf_ref[0:1, :], scale / sb_ref[0:1, :]], axis=1)

    def body(kk, carry):
        twr = _lane_tile(twr_ref[kk], 2 * w)
        twi = _lane_tile(twi_ref[kk], 2 * w)
        tr = jnp.concatenate([tf_ref[0, 0, kk], tb_ref[0, 0, kk]], axis=1)
        ti = jnp.concatenate([tf_ref[0, 1, kk], tb_ref[0, 1, kk]], axis=1)
        p = jnp.concatenate([tr * twr - ti * twi, tr * twi + ti * twr], axis=0)
        z = _dot_mat(fh_ref[...], fl_ref[...], p) * inv
        g_ref[0, 0, kk] = z[:n2, :w] + z[:n2, w:]
        g_ref[0, 1, kk] = z[n2:, :w] - z[n2:, w:]
        return carry

    lax.fori_loop(0, FFT_ROWS, body, 0, unroll=2)


def _fb_filter_call(thf, thb, sums, tabs, n1, n2):
    fh, fl = tabs['fb']
    tspec = pl.BlockSpec((1, 2, FFT_ROWS, n2, HY_W), lambda g, i: (g, 0, i, 0, 0))
    twspec = pl.BlockSpec((FFT_ROWS, n2, LANES), lambda g, i: (i, 0, 0))
    sspec = lambda d: pl.BlockSpec((8, HY_W), lambda g, i: (0, 2 * g + d))
    return pl.pallas_call(
        functools.partial(_fb_filter_kernel, n2=n2, scale=1.0 / (n1 * n2)),
        grid=(HY_ORDER, n1 // FFT_ROWS),
        in_specs=[tspec, tspec, twspec, twspec,
                  pl.BlockSpec(fh.shape, lambda g, i: (0, 0)), pl.BlockSpec(fl.shape, lambda g, i: (0, 0)),
                  sspec(0), sspec(1)],
        out_specs=pl.BlockSpec((1, 2, FFT_ROWS, n2, HY_W), lambda g, i: (g, 0, i, 0, 0)),
        out_shape=jax.ShapeDtypeStruct((HY_ORDER, 2, n1, n2, HY_W), F32),
        compiler_params=_cparams("arbitrary", "arbitrary"),
        name="fft_b_filter",
    )(thf, thb, tabs['twr'], tabs['twi'], fh, fl, sums, sums)


def _fb_kernel(t_ref, twr_ref, twi_ref, fh_ref, fl_ref, ch_ref, cl_ref, g_ref, o_ref, t4_scr, *, n2):
    w = g_ref.shape[-1]

    def body(kk, carry):
        tr = t_ref[0, kk]
        ti = t_ref[1, kk]
        twr = _lane_tile(twr_ref[kk], w)
        twi = _lane_tile(twi_ref[kk], w)
        p = jnp.concatenate([tr * twr - ti * twi, tr * twi + ti * twr], axis=0)
        z = _dot_mat(fh_ref[...], fl_ref[...], p)
        zr, zi = z[:n2], z[n2:]
        gr = g_ref[0, 0, kk]
        gi = g_ref[0, 1, kk]
        y = jnp.concatenate([zr * gr - zi * gi, zr * gi + zi * gr], axis=0)
        v = _dot_mat(ch_ref[...], cl_ref[...], y)
        vr, vi = v[:n2], v[n2:]
        t4_scr[0, kk] = vr * twr + vi * twi
        t4_scr[1, kk] = vi * twr - vr * twi
        return carry

    lax.fori_loop(0, FFT_ROWS, body, 0, unroll=4)
    o_ref[...] = pltpu.einshape("pknc->pk(nc)", t4_scr[...])


def _fb_call(t, g, order, tabs, n1, n2):
    fh, fl = tabs['fb']
    ch, cl = tabs['fc']
    lanes = n2 * HY_W
    mat = lambda m: pl.BlockSpec(m.shape, lambda i: (0, 0))
    dat = pl.BlockSpec((2, FFT_ROWS, lanes), lambda i: (0, i, 0))
    twspec = pl.BlockSpec((FFT_ROWS, n2, LANES), lambda i: (i, 0, 0))
    return pl.pallas_call(
        functools.partial(_fb_kernel, n2=n2),
        grid=(n1 // FFT_ROWS,),
        in_specs=[pl.BlockSpec((2, FFT_ROWS, n2, HY_W), lambda i: (0, i, 0, 0)),
                  twspec, twspec, mat(fh), mat(fl), mat(ch), mat(cl),
                  pl.BlockSpec((1, 2, FFT_ROWS, n2, HY_W), lambda i: (order, 0, i, 0, 0))],
        out_specs=dat,
        out_shape=jax.ShapeDtypeStruct((2, n1, lanes), F32),
        scratch_shapes=[pltpu.VMEM((2, FFT_ROWS, n2, HY_W), F32)],
        compiler_params=_cparams("arbitrary"),
        name="fft_b",
    )(t, tabs['twr'], tabs['twi'], fh, fl, ch, cl, g)


def _fd_kernel(u_ref, mh_ref, ml_ref, z_ref, x_ref, sk_ref, o_ref):
    u = u_ref[...]
    u = u.reshape(u.shape[0] * u.shape[1], u.shape[2])
    y = _dot_mat(mh_ref[...], ml_ref[...], u)
    y = pltpu.einshape("r(nc)->rnc", y, n=FFT_ROWS)
    o_ref[0] = x_ref[0] * (y + sk_ref[...] * z_ref[0])


def _fd_call(u, mats, z4, x4, skip_row, n1):
    mh, ml = mats
    b_, h1, n2, wd = z4.shape
    wt = FFT_ROWS * wd
    dat = pl.BlockSpec((1, h1, FFT_ROWS, wd), lambda j, p: (p, 0, j, 0))
    return pl.pallas_call(
        _fd_kernel,
        grid=(n2 // FFT_ROWS, b_),
        in_specs=[pl.BlockSpec((2, n1, wt), lambda j, p: (0, 0, j)),
                  pl.BlockSpec((h1, 2 * n1), lambda j, p: (p, 0)),
                  pl.BlockSpec((h1, 2 * n1), lambda j, p: (p, 0)),
                  dat, dat, pl.BlockSpec((1, wd), lambda j, p: (0, 0))],
        out_specs=dat,
        out_shape=jax.ShapeDtypeStruct(z4.shape, F32),
        compiler_params=pltpu.CompilerParams(dimension_semantics=("arbitrary", "arbitrary"),
                                             vmem_limit_bytes=FFT_D_VMEM_LIMIT),
        name="fft_d",
    )(u, mh, ml, z4, x4, skip_row)


def _hyena_filter_spectrum(L, tabs, n1, n2, fp):
    f32 = F32
    t = jnp.linspace(0.0, 1.0, L, dtype=f32)[:, None]
    w = 2.0 * math.pi * jnp.arange(L, dtype=f32)[:, None] / L
    fr = jnp.linspace(1e-4, HY_BANDS - 1, HY_BANDS, dtype=f32)[None]
    emb = jnp.concatenate([t, jnp.cos(fr * w), -jnp.sin(fr * w)], axis=-1)
    emb = jnp.pad(emb, ((0, 0), (0, LANES - HY_EMB)))
    deltas = jnp.abs(jnp.linspace(math.log(HY_DECAY_TARGET) / HY_DECAY_SHORT_PCT,
                                  math.log(HY_DECAY_TARGET) / HY_DECAY_LONG_PCT, N_FILT, dtype=f32))[None]
    f1_w, f1_b, f1_freq, f2_w, f2_b, f2_freq, f3_w = fp
    w1 = jnp.pad(f1_w, ((0, LANES - HY_EMB), (0, 0)))
    hf, hb, sums = _filter_call(emb, w1, f1_b[None], f1_freq[None], f2_w, f2_b[None],
                                f2_freq[None], f3_w, deltas, n1, n2)
    thf = _fa_real_call(hf, tabs['fa_real'], n1)
    thb = _fa_real_call(hb, tabs['fa_real'], n1)
    return _fb_filter_call(thf, thb, sums, tabs, n1, n2)


def _hyena_conv(z, xg, g, order, skip, tabs, n1, n2):
    b_, L, wd = z.shape
    assert b_ == 2, "the two batch rows ride as the real and imaginary parts of one transform"
    z4 = z.reshape(b_, n1 // 2, n2, wd)
    x4 = xg.reshape(b_, n1 // 2, n2, wd)
    t = _fa_call(z4, tabs['fa'], n1)
    u = _fb_call(t, g, order, tabs, n1, n2)
    out = _fd_call(u, tabs['fd'], z4, x4, skip[None, :], n1)
    return out.reshape(b_, L, wd)


def _merge_kernel(yg_ref, ya_ref, yh_ref, yz_ref, mg_ref, ma_ref, mh_ref, x_ref, gate_ref,
                  wg_ref, wa_ref, wh_ref, wo_ref, fn_ref, o_ref, *, final):
    yh = (yh_ref[0] * _silu(yz_ref[0].astype(F32))).astype(BF16)
    y = (_sigmoid(mg_ref[0].astype(F32)) * _dot(yg_ref[0], wg_ref[...])
         + _sigmoid(ma_ref[0].astype(F32)) * _dot(ya_ref[0], wa_ref[...])
         + _sigmoid(mh_ref[0].astype(F32)) * _dot(yh, wh_ref[...]))
    out = _dot(y.astype(BF16), wo_ref[...])
    xn = x_ref[0] + gate_ref[0] * out
    if final:
        ms = jnp.mean(xn * xn, axis=-1, keepdims=True)
        xn = xn * lax.rsqrt(ms + EPS) * fn_ref[...]
    o_ref[0] = xn


def _merge_call(y_gla, y_attn, y_hy, P, x, gate, wg, wa, wh, wo, fnorm, final):
    b_, L, d = x.shape
    T = min(1024, L)
    row = lambda w, blk=0: pl.BlockSpec((1, T, w), lambda b, t: (b, t, blk))
    const = lambda r, w: pl.BlockSpec((r, w), lambda b, t: (0, 0))
    mblk = COLS['m'][0] // d
    return pl.pallas_call(
        functools.partial(_merge_kernel, final=final),
        grid=(b_, L // T),
        in_specs=[row(BRANCH_W), row(BRANCH_W), row(BRANCH_W), row(HY_W, COLS['y_z'][0] // HY_W),
                  row(d, mblk), row(d, mblk + 1), row(d, mblk + 2), row(d),
                  pl.BlockSpec((1, 1, d), lambda b, t: (b, 0, 0)),
                  const(BRANCH_W, d), const(BRANCH_W, d), const(BRANCH_W, d), const(d, d), const(1, d)],
        out_specs=row(d),
        out_shape=jax.ShapeDtypeStruct((b_, L, d), F32),
        compiler_params=_cparams("arbitrary", "arbitrary"),
        name="merge",
    )(y_gla, y_attn, y_hy, P, P, P, P, x, gate, wg, wa, wh, wo, fnorm)


def _pack_w_in(w):
    parts = dict(zip(('g_q', 'g_k', 'g_v', 'g_z', 'g_af', 'g_ab', 'a_q', 'a_k', 'a_v', 'a_z', 'y_u', 'y_z', 'm'),
                     jnp.split(w, [int(i) for i in np.cumsum(SPLITS)[:-1]], axis=-1)))
    parts['g_a'] = jnp.pad(jnp.concatenate([parts['g_af'], parts['g_ab']], axis=-1),
                           ((0, 0), (0, LANES - 2 * GLA_RANK)))
    parts['g_a_lo'] = jnp.zeros_like(parts['g_a'])
    order = sorted(COLS, key=lambda n: COLS[n][0])
    return jnp.concatenate([parts[n] for n in order], axis=-1).astype(BF16)


def _rope_tables(L, heads):
    t = jnp.arange(L)
    row = (t // GRID_W).astype(F32)
    colp = (t % GRID_W).astype(F32)
    half = ATTN_HD // 2
    inv = ROPE_THETA ** (-jnp.arange(0, half, 2, dtype=F32) / half)
    ang = jnp.concatenate([row[:, None] * inv, colp[:, None] * inv], axis=-1)
    cos = jnp.repeat(jnp.cos(ang), 2, axis=-1)
    sin = jnp.stack([-jnp.sin(ang), jnp.sin(ang)], axis=-1).reshape(L, ATTN_HD)
    return jnp.tile(cos, (1, heads)), jnp.tile(sin, (1, heads))


def _identity_rope(L, heads):
    return jnp.ones((L, heads * ATTN_HD), F32), jnp.zeros((L, heads * ATTN_HD), F32)


def _block_diag_ones(width):
    i = np.arange(width) // ATTN_HD
    return jnp.asarray((i[:, None] == i[None, :]).astype(np.float32)).astype(BF16)


def kernel(x, c, ctx, c_ctx, w_ada, b_ada, w_in, gla_wa_f, gla_ba_f, gla_wa_b, gla_ba_b, gla_norm,
           attn_qnorm, attn_knorm, hy_conv_w, hy_conv_b, hy_f1_w, hy_f1_b, hy_f1_freq, hy_f2_w,
           hy_f2_b, hy_f2_freq, hy_f3_w, hy_skip, w_br_gla, w_br_attn, w_br_hy, w_out, final_norm):
    b_, L, d = x.shape
    Lc = ctx.shape[1]
    depth = w_ada.shape[0]

    cos_q, sin_q = _rope_tables(L, ATTN_HEADS)
    cos_k, sin_k = cos_q[:, :ATTN_KVW], sin_q[:, :ATTN_KVW]
    cos_qc, sin_qc = _identity_rope(Lc, ATTN_HEADS)
    cos_kc, sin_kc = cos_qc[:, :ATTN_KVW], sin_qc[:, :ATTN_KVW]
    bdq, bdk = _block_diag_ones(ATTN_QW), _block_diag_ones(ATTN_KVW)
    gw = ATTN_GROUP * ATTN_HD
    e_heads = jnp.stack([jnp.eye(ATTN_HD, gw, k=h * ATTN_HD, dtype=BF16) for h in range(ATTN_GROUP)])
    n1, n2 = _fft_sizes(2 * L)
    n1c, n2c = _fft_sizes(2 * Lc)
    tabs = _fft_tables(n1, n2)
    tabs_c = _fft_tables(n1c, n2c)
    zero_state = jnp.zeros((b_, GLA_DV, GLA_KW), F32)

    cmat = jnp.concatenate([c, c_ctx[None], jnp.zeros((8 - b_ - 1, d), F32)], axis=0)
    mods = _ada_call(cmat, w_ada, b_ada)

    for l in range(depth):
        need_ctx = l < depth - 1
        shift, scale, gate = [m[:b_, None, :] for m in jnp.split(mods[l], 3, axis=-1)]
        shift_c, scale_c, gate_c = [jnp.broadcast_to(m[b_:b_ + 1, None, :], (b_, 1, d))
                                    for m in jnp.split(mods[l], 3, axis=-1)]
        w_packed = _pack_w_in(w_in[l])
        P = _proj_call(x, scale, shift, w_packed)
        Pc = _proj_call(ctx, scale_c, shift_c, w_packed)

        rk = GLA_RANK
        wa_f = jnp.zeros((LANES, GLA_KW), F32).at[0:rk].set(gla_wa_f[l])
        wa_b = jnp.zeros((LANES, GLA_KW), F32).at[rk:2 * rk].set(gla_wa_b[l])
        ba_f, ba_b = gla_ba_f[l][None], gla_ba_b[l][None]
        nw = gla_norm[l][None]
        oc_f, sc_f = _gla_call(Pc, wa_f, ba_f, zero_state, reverse=False)
        yc_gla, sc_b = _gla_call(Pc, wa_b, ba_b, zero_state, reverse=True, fin=(oc_f, nw))
        o_f, _ = _gla_call(P, wa_f, ba_f, sc_f, reverse=False)
        y_gla, _ = _gla_call(P, wa_b, ba_b, sc_b, reverse=True, fin=(o_f, nw))

        gq = jnp.tile(attn_qnorm[l], ATTN_HEADS)[None]
        gk = jnp.tile(attn_knorm[l], ATTN_KV_HEADS)[None]
        q_a, k_a, v_a = _qkv_prep_call(P, cos_q, sin_q, cos_k, sin_k, gq, gk, bdq, bdk)
        qc_a, kc_a, vc_a = _qkv_prep_call(Pc, cos_qc, sin_qc, cos_kc, sin_kc, gq, gk, bdq, bdk)
        y_attn = _attn_call(q_a, jnp.concatenate([k_a, kc_a], axis=2),
                            jnp.concatenate([v_a, vc_a], axis=2), P, e_heads)

        fp = (hy_f1_w[l], hy_f1_b[l], hy_f1_freq[l], hy_f2_w[l], hy_f2_b[l], hy_f2_freq[l], hy_f3_w[l])
        g_spec = _hyena_filter_spectrum(L, tabs, n1, n2, fp)
        v0, x1, x2 = _short_conv_call(P, hy_conv_w[l], hy_conv_b[l][None])
        z1 = _hyena_conv(v0, x1, g_spec, 0, hy_skip[l, 0], tabs, n1, n2)
        y_hy = _hyena_conv(z1, x2, g_spec, 1, hy_skip[l, 1], tabs, n1, n2)

        wg, wa, wh, wo = (w_br_gla[l].astype(BF16), w_br_attn[l].astype(BF16),
                          w_br_hy[l].astype(BF16), w_out[l].astype(BF16))
        fn = final_norm[None]
        x_new = _merge_call(y_gla, y_attn, y_hy, P, x, gate, wg, wa, wh, wo, fn, final=not need_ctx)

        if need_ctx:
            yc_attn = _attn_call(qc_a, kc_a, vc_a, Pc, e_heads)
            gc_spec = _hyena_filter_spectrum(Lc, tabs_c, n1c, n2c, fp)
            vc0, xc1, xc2 = _short_conv_call(Pc, hy_conv_w[l], hy_conv_b[l][None])
            zc1 = _hyena_conv(vc0, xc1, gc_spec, 0, hy_skip[l, 0], tabs_c, n1c, n2c)
            yc_hy = _hyena_conv(zc1, xc2, gc_spec, 1, hy_skip[l, 1], tabs_c, n1c, n2c)
            ctx = _merge_call(yc_gla, yc_attn, yc_hy, Pc, ctx, gate_c, wg, wa, wh, wo, fn, final=False)
        x = x_new

    return x
```

```python
import functools
import math

import numpy as np
import jax
import jax.numpy as jnp
from jax import lax
from jax.experimental import pallas as pl
from jax.experimental.pallas import tpu as pltpu

F32 = jnp.float32
BF16 = jnp.bfloat16
F8 = jnp.float8_e4m3fn

D_MODEL = 1024
GRID_W = 64
BRANCH_W = D_MODEL // 2
N_BRANCH = 3
EPS = 1e-6
GLA_HEADS = 4
GLA_DV = BRANCH_W // GLA_HEADS
GLA_DK = GLA_DV // 2
GLA_KW = GLA_HEADS * GLA_DK
GLA_VW = GLA_HEADS * GLA_DV
GLA_RANK = 16
GLA_GATE_NORM = 16.0
GLA_CHUNK = 64
GLA_BLOCK = 256
ATTN_HD = 64
ATTN_HEADS = BRANCH_W // ATTN_HD
ATTN_KV_HEADS = ATTN_HEADS // 4
ATTN_GROUP = ATTN_HEADS // ATTN_KV_HEADS
ATTN_QW = ATTN_HEADS * ATTN_HD
ATTN_KVW = ATTN_KV_HEADS * ATTN_HD
ROPE_THETA = 10000.0
HY_W = BRANCH_W
HY_ORDER = 2
HY_EMB = 33
HY_BANDS = (HY_EMB - 1) // 2
HY_FFN = 64
HY_SHORT = 3
HY_MOD_SHIFT = 0.05
HY_DECAY_SHORT_PCT = 0.3
HY_DECAY_LONG_PCT = 1.5
HY_DECAY_TARGET = 1e-2
N_FILT = 2 * HY_ORDER * HY_W
SPLITS = (GLA_KW, GLA_KW, GLA_VW, GLA_VW, GLA_RANK, GLA_RANK,
          ATTN_QW, ATTN_KVW, ATTN_KVW, ATTN_QW,
          (HY_ORDER + 1) * HY_W, HY_W,
          N_BRANCH * D_MODEL)

LANES = 128
VMEM_LIMIT = 48 * 1024 * 1024
FFT_D_VMEM_LIMIT = 56 * 1024 * 1024

COLS = {
    'm': (0, 3072), 'y_u': (3072, 1536), 'g_v': (4608, 512), 'g_z': (5120, 512),
    'a_q': (5632, 512), 'a_z': (6144, 512), 'y_z': (6656, 512), 'g_q': (7168, 256),
    'g_k': (7424, 256), 'a_k': (7680, 128), 'a_v': (7808, 128), 'g_a': (7936, 128),
    'g_a_lo': (8064, 128),
}
N_PROJ = 8192
PROJ_TN = 2048


def _cparams(*sem):
    return pltpu.CompilerParams(dimension_semantics=sem, vmem_limit_bytes=VMEM_LIMIT)


def _split(a):
    hi = a.astype(BF16)
    lo = (a - hi.astype(F32)).astype(BF16)
    return hi, lo


def _dot(a, b):
    return jnp.dot(a, b, preferred_element_type=F32)


def _dot_hp(a, b):
    ah, al = _split(a)
    bh, bl = _split(b)
    return _dot(ah, bh) + _dot(al, bh) + _dot(ah, bl)


def _dot_mat(mh, ml, x):
    xh = x.astype(BF16)
    return _dot(mh, xh) + _dot(ml, xh)


def _sigmoid(x):
    return 1.0 / (1.0 + jnp.exp(-x))


def _silu(x):
    return x * _sigmoid(x)


def _ada_kernel(c_ref, w_ref, b_ref, o_ref):
    o_ref[0] = _dot_hp(_silu(c_ref[...]), w_ref[0]) + b_ref[0]


def _ada_call(cmat, w_ada, b_ada):
    depth, d, n3 = w_ada.shape
    tn = 1024
    return pl.pallas_call(
        _ada_kernel,
        grid=(depth, n3 // tn),
        in_specs=[pl.BlockSpec((8, d), lambda l, j: (0, 0)),
                  pl.BlockSpec((1, d, tn), lambda l, j: (l, 0, j)),
                  pl.BlockSpec((1, 1, tn), lambda l, j: (l, 0, j))],
        out_specs=pl.BlockSpec((1, 8, tn), lambda l, j: (l, 0, j)),
        out_shape=jax.ShapeDtypeStruct((depth, 8, n3), F32),
        compiler_params=_cparams("arbitrary", "arbitrary"),
        name="ada",
    )(cmat, w_ada, b_ada.reshape(depth, 1, n3))


def _proj_kernel(x_ref, sc_ref, sh_ref, w_ref, o_ref, *, res_tile, hi_off, lo_off):
    x = x_ref[0]
    ms = jnp.mean(x * x, axis=-1, keepdims=True)
    h = x * lax.rsqrt(ms + EPS) * (1.0 + sc_ref[0]) + sh_ref[0]
    res = _dot(h.astype(BF16), w_ref[...])
    out = res.astype(o_ref.dtype)
    o_ref[0] = out

    @pl.when(pl.program_id(0) == res_tile)
    def _():
        hi = slice(hi_off, hi_off + LANES)
        o_ref[0, :, lo_off:lo_off + LANES] = (res[:, hi] - out[:, hi].astype(F32)).astype(o_ref.dtype)


def _proj_call(x, scale, shift, w):
    b_, L, d = x.shape
    n = w.shape[1]
    tm = min(1024, L)
    tn = PROJ_TN
    hi0, lo0 = COLS['g_a'][0], COLS['g_a_lo'][0]
    assert hi0 // tn == lo0 // tn
    return pl.pallas_call(
        functools.partial(_proj_kernel, res_tile=hi0 // tn, hi_off=hi0 % tn, lo_off=lo0 % tn),
        grid=(n // tn, b_, L // tm),
        in_specs=[pl.BlockSpec((1, tm, d), lambda j, b, i: (b, i, 0)),
                  pl.BlockSpec((1, 1, d), lambda j, b, i: (b, 0, 0)),
                  pl.BlockSpec((1, 1, d), lambda j, b, i: (b, 0, 0)),
                  pl.BlockSpec((d, tn), lambda j, b, i: (0, j))],
        out_specs=pl.BlockSpec((1, tm, tn), lambda j, b, i: (b, i, j)),
        out_shape=jax.ShapeDtypeStruct((b_, L, n), BF16),
        compiler_params=_cparams("arbitrary", "arbitrary", "arbitrary"),
        name="proj",
    )(x, scale, shift, w)


def _gla_kernel(*refs, reverse, finalize, nchunks):
    if finalize:
        (q_ref, k_ref, v_ref, ga_ref, gal_ref, wa_ref, ba_ref, s0_ref, tri_ref, op_ref, z_ref, nw_ref,
         o_ref, sfin_ref, s_scr) = refs
    else:
        (q_ref, k_ref, v_ref, ga_ref, gal_ref, wa_ref, ba_ref, s0_ref, tri_ref,
         o_ref, sfin_ref, s_scr) = refs
    C = GLA_CHUNK
    T = nchunks * C
    nt_dims = (((1,), (1,)), ((), ()))
    tn_dims = (((0,), (0,)), ((), ()))

    @pl.when(pl.program_id(1) == 0)
    def _():
        s_scr[...] = s0_ref[0]

    ga = ga_ref[0].astype(F32) + gal_ref[0].astype(F32)
    xg = _dot_hp(ga, wa_ref[...]) + ba_ref[...]
    g = (jnp.minimum(xg, 0.0) - jnp.log(1.0 + jnp.exp(-jnp.abs(xg)))) * (1.0 / GLA_GATE_NORM)

    tri = tri_ref[...]
    g1 = g.astype(BF16)
    r1 = g - g1.astype(F32)
    g2 = r1.astype(BF16)
    g3 = (r1 - g2.astype(F32)).astype(BF16)
    b = _dot(tri, g1) + _dot(tri, g2) + _dot(tri, g3)

    def chunk_row(r):
        return jnp.concatenate([jnp.broadcast_to(b[c * C + r:c * C + r + 1], (C, GLA_KW))
                                for c in range(nchunks)], axis=0)

    tot_row = 0 if reverse else C - 1
    bm = chunk_row(C // 2)
    bt = chunk_row(tot_row)
    ri = lax.broadcasted_iota(jnp.int32, (T, T), 0)
    ci = lax.broadcasted_iota(jnp.int32, (T, T), 1)
    same_chunk = (ri // C) == (ci // C)
    mask = same_chunk & ((ci >= ri) if reverse else (ci <= ri))
    lane = lax.broadcasted_iota(jnp.int32, (1, GLA_KW), 1)
    hmask = [(lane >= h * GLA_DK) & (lane < (h + 1) * GLA_DK) for h in range(GLA_HEADS)]

    q = q_ref[0].astype(F32) * (GLA_DK ** -0.5)
    k = k_ref[0].astype(F32)
    v = v_ref[0].astype(BF16)
    qa = q * jnp.exp(b - bm)
    kb = (k * jnp.exp(bm - b)).astype(BF16)
    qe = q * jnp.exp(b)
    kd = (k * jnp.exp(bt - b)).astype(BF16)

    intra = []
    for h in range(GLA_HEADS):
        qa_h = jnp.where(hmask[h], qa, 0.0).astype(BF16)
        att = lax.dot_general(qa_h, kb, nt_dims, preferred_element_type=F32)
        att = jnp.where(mask, att, 0.0).astype(BF16)
        intra.append(_dot(att, v[:, h * GLA_DV:(h + 1) * GLA_DV]))

    inter = [None] * nchunks
    order = range(nchunks - 1, -1, -1) if reverse else range(nchunks)
    for c in order:
        rows = slice(c * C, (c + 1) * C)
        st = s_scr[...]
        st_b = st.astype(BF16)
        dec = jnp.exp(b[c * C + tot_row:c * C + tot_row + 1])
        upd = jnp.zeros_like(st)
        outs = []
        for h in range(GLA_HEADS):
            qe_h = jnp.where(hmask[h], qe[rows], 0.0).astype(BF16)
            outs.append(lax.dot_general(qe_h, st_b, nt_dims, preferred_element_type=F32))
            u_h = lax.dot_general(v[rows, h * GLA_DV:(h + 1) * GLA_DV], kd[rows], tn_dims,
                                  preferred_element_type=F32)
            upd = upd + jnp.where(hmask[h], u_h, 0.0)
        s_scr[...] = dec * st + upd
        inter[c] = jnp.concatenate(outs, axis=1)
    o = jnp.concatenate(intra, axis=1) + jnp.concatenate(inter, axis=0)

    if finalize:
        o = o + op_ref[0]
        parts = []
        for h in range(GLA_HEADS):
            oh = o[:, h * GLA_DV:(h + 1) * GLA_DV]
            ms = jnp.mean(oh * oh, axis=-1, keepdims=True)
            parts.append(oh * lax.rsqrt(ms + EPS) * nw_ref[...])
        y = jnp.concatenate(parts, axis=1) * _silu(z_ref[0].astype(F32))
        o_ref[0] = y.astype(o_ref.dtype)
    else:
        o_ref[0] = o

    @pl.when(pl.program_id(1) == pl.num_programs(1) - 1)
    def _():
        sfin_ref[0] = s_scr[...]


def _gla_call(P, wa_pad, ba, s0, reverse, fin=None):
    b_, L, _ = P.shape
    T = min(GLA_BLOCK, L)
    nt = L // T
    if reverse:
        tmap = lambda t: nt - 1 - t
    else:
        tmap = lambda t: t
    ii = np.arange(T)
    same = (ii[:, None] // GLA_CHUNK) == (ii[None, :] // GLA_CHUNK)
    tri = same & ((ii[None, :] >= ii[:, None]) if reverse else (ii[None, :] <= ii[:, None]))
    tri = jnp.asarray(tri.astype(np.float32)).astype(BF16)

    def col(name, width):
        blk = COLS[name][0] // width
        return pl.BlockSpec((1, T, width), lambda b, t: (b, tmap(t), blk))

    in_specs = [col('g_q', GLA_KW), col('g_k', GLA_KW), col('g_v', GLA_VW),
                col('g_a', LANES), col('g_a_lo', LANES),
                pl.BlockSpec((LANES, GLA_KW), lambda b, t: (0, 0)),
                pl.BlockSpec((1, GLA_KW), lambda b, t: (0, 0)),
                pl.BlockSpec((1, GLA_DV, GLA_KW), lambda b, t: (b, 0, 0)),
                pl.BlockSpec((T, T), lambda b, t: (0, 0))]
    args = [P, P, P, P, P, wa_pad, ba, s0, tri]
    if fin is not None:
        o_prev, nw = fin
        in_specs += [pl.BlockSpec((1, T, GLA_VW), lambda b, t: (b, tmap(t), 0)),
                     col('g_z', GLA_VW),
                     pl.BlockSpec((1, GLA_DV), lambda b, t: (0, 0))]
        args += [o_prev, P, nw]
    out_dtype = BF16 if fin is not None else F32
    return pl.pallas_call(
        functools.partial(_gla_kernel, reverse=reverse, finalize=fin is not None, nchunks=T // GLA_CHUNK),
        grid=(b_, nt),
        in_specs=in_specs,
        out_specs=[pl.BlockSpec((1, T, GLA_VW), lambda b, t: (b, tmap(t), 0)),
                   pl.BlockSpec((1, GLA_DV, GLA_KW), lambda b, t: (b, 0, 0))],
        out_shape=[jax.ShapeDtypeStruct((b_, L, GLA_VW), out_dtype),
                   jax.ShapeDtypeStruct((b_, GLA_DV, GLA_KW), F32)],
        scratch_shapes=[pltpu.VMEM((GLA_DV, GLA_KW), F32)],
        compiler_params=_cparams("arbitrary", "arbitrary"),
        name="gla",
    )(*args)


def _norm_rope(a, gain, bd, cos, sin):
    sq = a * a
    sh, sl = _split(sq)
    ss = _dot(sh, bd) + _dot(sl, bd)
    an = a * lax.rsqrt(ss * (1.0 / ATTN_HD) + EPS) * gain
    w = a.shape[-1]
    lane = lax.broadcasted_iota(jnp.int32, an.shape, 1)
    partner = jnp.where(lane % 2 == 0, pltpu.roll(an, w - 1, 1), pltpu.roll(an, 1, 1))
    return an * cos + partner * sin


Q_SCALE = (ATTN_HD ** -0.5) * math.log2(math.e)
ATTN_QK_DEPTH = 4 * ATTN_HD
ATTN_TK_MAX = 640


def _split8(a):
    hi = a.astype(F8).astype(F32)
    lo = (a - hi).astype(F8)
    return hi.astype(BF16), lo.astype(BF16)


def _qkv_prep_kernel(q_ref, k_ref, v_ref, cq_ref, sq_ref, ck_ref, sk_ref, gq_ref, gk_ref,
                     bdq_ref, bdk_ref, qo_ref, ko_ref, vo_ref):
    hd = ATTN_HD
    q = _norm_rope(q_ref[0].astype(F32), gq_ref[...], bdq_ref[...], cq_ref[...], sq_ref[...]) * Q_SCALE
    qh, ql = _split8(q)
    qt = jnp.concatenate([qh, ql], axis=1).T
    for h in range(ATTN_HEADS):
        hi = qt[hd * h:hd * (h + 1)]
        lo = qt[ATTN_QW + hd * h:ATTN_QW + hd * (h + 1)]
        qo_ref[0, ATTN_QK_DEPTH * h:ATTN_QK_DEPTH * (h + 1), :] = jnp.concatenate([hi, hi, lo, lo], axis=0).astype(F8)
    k = _norm_rope(k_ref[0].astype(F32), gk_ref[...], bdk_ref[...], ck_ref[...], sk_ref[...])
    kh, kl = _split8(k)
    for g in range(ATTN_KV_HEADS):
        cols = slice(g * hd, (g + 1) * hd)
        ko_ref[0, g] = jnp.concatenate([kh[:, cols], kl[:, cols], kh[:, cols], kl[:, cols]], axis=1).astype(F8)
    vo_ref[0] = v_ref[0].astype(BF16).T


def _qkv_prep_call(P, cos_q, sin_q, cos_k, sin_k, gq, gk, bdq, bdk):
    b_, L, _ = P.shape
    T = min(1024, L)

    def col(name, width):
        blk = COLS[name][0] // width
        return pl.BlockSpec((1, T, width), lambda b, t: (b, t, blk))

    tab = lambda w: pl.BlockSpec((T, w), lambda b, t: (t, 0))
    const = lambda r, w: pl.BlockSpec((r, w), lambda b, t: (0, 0))
    return pl.pallas_call(
        _qkv_prep_kernel,
        grid=(b_, L // T),
        in_specs=[col('a_q', ATTN_QW), col('a_k', ATTN_KVW), col('a_v', ATTN_KVW),
                  tab(ATTN_QW), tab(ATTN_QW), tab(ATTN_KVW), tab(ATTN_KVW),
                  const(1, ATTN_QW), const(1, ATTN_KVW),
                  const(ATTN_QW, ATTN_QW), const(ATTN_KVW, ATTN_KVW)],
        out_specs=[pl.BlockSpec((1, ATTN_HEADS * ATTN_QK_DEPTH, T), lambda b, t: (b, 0, t)),
                   pl.BlockSpec((1, ATTN_KV_HEADS, T, ATTN_QK_DEPTH), lambda b, t: (b, 0, t, 0)),
                   pl.BlockSpec((1, ATTN_KVW, T), lambda b, t: (b, 0, t))],
        out_shape=[jax.ShapeDtypeStruct((b_, ATTN_HEADS * ATTN_QK_DEPTH, L), F8),
                   jax.ShapeDtypeStruct((b_, ATTN_KV_HEADS, L, ATTN_QK_DEPTH), F8),
                   jax.ShapeDtypeStruct((b_, ATTN_KVW, L), BF16)],
        compiler_params=_cparams("arbitrary", "arbitrary"),
        name="qkv_prep",
    )(P, P, P, cos_q, sin_q, cos_k, sin_k, gq, gk, bdq, bdk)


ATTN_SUM_ROWS = 16
ATTN_UNROLL = 2
ATTN_TQ = 512


def _attn_kernel(qt_ref, k_ref, vt_ref, z_ref, e_ref, o_ref, m_scr, a_scr, t_scr, acc_scr, s_scr, *, tk, nk):
    m_scr[...] = jnp.full(m_scr.shape, -jnp.inf, F32)
    acc_scr[...] = jnp.zeros(acc_scr.shape, F32)
    ones = jnp.ones((ATTN_SUM_ROWS, tk), BF16)

    def score_head(kj, slot, h):
        s = _dot(kj, qt_ref[0, h * ATTN_QK_DEPTH:(h + 1) * ATTN_QK_DEPTH, :])
        s_scr[slot, h] = s
        t_scr[slot, h] = jnp.max(s, axis=0, keepdims=True)

    def keys(j):
        start = j * tk if isinstance(j, int) else pl.multiple_of(j * tk, tk)
        return k_ref[0, 0, pl.ds(start, tk), :]

    k0 = keys(0)
    for h in range(ATTN_GROUP):
        score_head(k0, 0, h)

    def step(j, slot):
        kj = keys(min(j + 1, nk - 1) if isinstance(j, int) else jnp.minimum(j + 1, nk - 1))
        start = j * tk if isinstance(j, int) else pl.multiple_of(j * tk, tk)
        vte = jnp.concatenate([vt_ref[0, :, pl.ds(start, tk)], ones], axis=0)
        for h in range(ATTN_GROUP):
            score_head(kj, 1 - slot, h)
            m_old = m_scr[h]
            m_new = jnp.maximum(m_old, t_scr[slot, h])
            a_scr[h] = jnp.exp2(m_old - m_new)
            m_scr[h] = m_new
            p = jnp.exp2(s_scr[slot, h] - m_new).astype(BF16)
            acc_scr[h] = a_scr[h] * acc_scr[h] + _dot(vte, p)

    def trip(i, carry):
        for u in range(ATTN_UNROLL):
            step(ATTN_UNROLL * i + u, u % 2)
        return carry

    ntrips = nk // ATTN_UNROLL
    lax.fori_loop(0, ntrips, trip, 0)
    for j in range(ntrips * ATTN_UNROLL, nk):
        step(j, j % 2)
    tn = (((0,), (0,)), ((), ()))
    out = None
    for h in range(ATTN_GROUP):
        acc = acc_scr[h]
        oh, ol = _split(acc[:ATTN_HD] / acc[ATTN_HD:ATTN_HD + 1])
        part = (lax.dot_general(oh, e_ref[h], tn, preferred_element_type=F32)
                + lax.dot_general(ol, e_ref[h], tn, preferred_element_type=F32))
        out = part if out is None else out + part
    o_ref[0] = (out * _silu(z_ref[0].astype(F32))).astype(o_ref.dtype)


def _attn_call(qt, k, vt, P, e_heads):
    b_, _, L = qt.shape
    Lk = k.shape[2]
    tq = min(ATTN_TQ, L)
    tk = max(t for t in range(LANES, ATTN_TK_MAX + 1, LANES) if Lk % t == 0)
    gw = ATTN_GROUP * ATTN_HD
    zblk = COLS['a_z'][0] // gw
    return pl.pallas_call(
        functools.partial(_attn_kernel, tk=tk, nk=Lk // tk),
        grid=(b_, ATTN_KV_HEADS, L // tq),
        in_specs=[pl.BlockSpec((1, ATTN_GROUP * ATTN_QK_DEPTH, tq), lambda b, g, i: (b, g, i)),
                  pl.BlockSpec((1, 1, Lk, ATTN_QK_DEPTH), lambda b, g, i: (b, g, 0, 0)),
                  pl.BlockSpec((1, ATTN_HD, Lk), lambda b, g, i: (b, g, 0)),
                  pl.BlockSpec((1, tq, gw), lambda b, g, i: (b, i, zblk + g)),
                  pl.BlockSpec((ATTN_GROUP, ATTN_HD, gw), lambda b, g, i: (0, 0, 0))],
        out_specs=pl.BlockSpec((1, tq, gw), lambda b, g, i: (b, i, g)),
        out_shape=jax.ShapeDtypeStruct((b_, L, ATTN_QW), BF16),
        scratch_shapes=[pltpu.VMEM((ATTN_GROUP, 1, tq), F32), pltpu.VMEM((ATTN_GROUP, 1, tq), F32),
                        pltpu.VMEM((2, ATTN_GROUP, 1, tq), F32),
                        pltpu.VMEM((ATTN_GROUP, ATTN_HD + ATTN_SUM_ROWS, tq), F32),
                        pltpu.VMEM((2, ATTN_GROUP, tk, tq), F32)],
        compiler_params=_cparams("arbitrary", "arbitrary", "arbitrary"),
        name="attn",
    )(qt, k, vt, P, e_heads)


def _short_conv_kernel(u_ref, up_ref, un_ref, w_ref, b_ref, v_ref, x1_ref, x2_ref):
    t = pl.program_id(1)
    nt = pl.num_programs(1)
    u = u_ref[0].astype(F32)
    T = u.shape[0]
    hr = up_ref.shape[1]
    prev_row = jnp.where(t > 0, up_ref[0, hr - 1:hr, :].astype(F32), 0.0)
    next_row = jnp.where(t < nt - 1, un_ref[0, 0:1, :].astype(F32), 0.0)
    row = lax.broadcasted_iota(jnp.int32, u.shape, 0)
    u_m1 = jnp.where(row == 0, prev_row, pltpu.roll(u, 1, 0))
    u_p1 = jnp.where(row == T - 1, next_row, pltpu.roll(u, T - 1, 0))
    w = w_ref[...]
    out = b_ref[...] + u_m1 * w[0:1] + u * w[1:2] + u_p1 * w[2:3]
    v_ref[0] = out[:, 0:HY_W]
    x1_ref[0] = out[:, HY_W:2 * HY_W]
    x2_ref[0] = out[:, 2 * HY_W:3 * HY_W]


def _short_conv_call(P, w, bias):
    b_, L, _ = P.shape
    T = min(1024, L)
    cw = (HY_ORDER + 1) * HY_W
    blk = COLS['y_u'][0] // cw
    hr = 16
    hb = T // hr
    nh = L // hr
    out_spec = pl.BlockSpec((1, T, HY_W), lambda b, t: (b, t, 0))
    shp = jax.ShapeDtypeStruct((b_, L, HY_W), F32)
    return pl.pallas_call(
        _short_conv_kernel,
        grid=(b_, L // T),
        in_specs=[pl.BlockSpec((1, T, cw), lambda b, t: (b, t, blk)),
                  pl.BlockSpec((1, hr, cw), lambda b, t: (b, jnp.maximum(t * hb - 1, 0), blk)),
                  pl.BlockSpec((1, hr, cw), lambda b, t: (b, jnp.minimum((t + 1) * hb, nh - 1), blk)),
                  pl.BlockSpec((HY_SHORT, cw), lambda b, t: (0, 0)),
                  pl.BlockSpec((1, cw), lambda b, t: (0, 0))],
        out_specs=[out_spec, out_spec, out_spec],
        out_shape=[shp, shp, shp],
        compiler_params=_cparams("arbitrary", "arbitrary"),
        name="short_conv",
    )(P, P, P, w, bias)


FFT_ROWS = 8


def _filter_kernel(emb_ref, w1_ref, b1_ref, f1_ref, w2_ref, b2_ref, f2_ref, w3_ref, dl_ref,
                   hf_ref, hb_ref, sum_ref):
    t = pl.program_id(0)
    emb = emb_ref[...]
    h = jnp.sin(f1_ref[...] * (_dot_hp(emb, w1_ref[...]) + b1_ref[...]))
    h = jnp.sin(f2_ref[...] * (_dot_hp(h, w2_ref[...]) + b2_ref[...]))
    hh, hl = _split(h)
    tt = emb[:, 0:1]

    @pl.when(t == 0)
    def _():
        sum_ref[...] = jnp.zeros_like(sum_ref)

    lag0 = (lax.broadcasted_iota(jnp.int32, (h.shape[0], HY_W), 0) + t * h.shape[0]) == 0
    for o in range(HY_ORDER):
        for d, out_ref in enumerate((hf_ref, hb_ref)):
            cols = slice((2 * o + d) * HY_W, (2 * o + d + 1) * HY_W)
            wh, wl = _split(w3_ref[:, cols])
            f = _dot(hh, wh) + _dot(hl, wh) + _dot(hh, wl)
            f = f * (jnp.exp(-tt * dl_ref[:, cols]) + HY_MOD_SHIFT)
            sum_ref[:, cols] += jnp.broadcast_to(jnp.sum(jnp.abs(f), axis=0, keepdims=True), (8, HY_W))
            if d == 1:
                f = jnp.where(lag0, 0.0, f)
            out_ref[o] = pltpu.einshape("(an)c->a(nc)", f, a=FFT_ROWS)


def _filter_call(emb, w1, b1, f1, w2, b2, f2, w3, deltas, n1, n2):
    L = emb.shape[0]
    TL = FFT_ROWS * n2
    const = lambda r, w: pl.BlockSpec((r, w), lambda t: (0, 0))
    hspec = pl.BlockSpec((HY_ORDER, FFT_ROWS, n2 * HY_W), lambda t: (0, t, 0))
    hshape = jax.ShapeDtypeStruct((HY_ORDER, n1 // 2, n2 * HY_W), F32)
    return pl.pallas_call(
        _filter_kernel,
        grid=(L // TL,),
        in_specs=[pl.BlockSpec((TL, LANES), lambda t: (t, 0)),
                  const(LANES, HY_FFN), const(1, HY_FFN), const(1, HY_FFN),
                  const(HY_FFN, HY_FFN), const(1, HY_FFN), const(1, HY_FFN),
                  const(HY_FFN, N_FILT), const(1, N_FILT)],
        out_specs=[hspec, hspec, pl.BlockSpec((8, N_FILT), lambda t: (0, 0))],
        out_shape=[hshape, hshape, jax.ShapeDtypeStruct((8, N_FILT), F32)],
        compiler_params=_cparams("arbitrary"),
        name="hy_filter",
    )(emb, w1, b1, f1, w2, b2, f2, w3, deltas)


def _fft_sizes(n):
    lg = int(round(math.log2(n)))
    assert 1 << lg == n
    n1 = 1 << ((lg + 1) // 2)
    return n1, n // n1


def _bf16_pair(a):
    a = np.asarray(a, np.float32)
    hi = jnp.asarray(a, F32).astype(BF16)
    lo = (jnp.asarray(a, F32) - hi.astype(F32)).astype(BF16)
    return hi, lo


def _fft_tables(n1, n2):
    n = n1 * n2
    h1 = n1 // 2
    k1 = np.arange(n1)[:, None]
    a = 2.0 * np.pi * ((k1 * np.arange(h1)[None, :]) % n1) / n1
    c, s = np.cos(a), np.sin(a)
    fa = np.block([[c, s], [-s, c]])
    fa_real = np.concatenate([c, -s], axis=0)
    fd = np.block([[c.T, -s.T], [s.T, c.T]])
    k2 = np.arange(n2)[:, None]
    b = 2.0 * np.pi * ((k2 * np.arange(n2)[None, :]) % n2) / n2
    cb, sb = np.cos(b), np.sin(b)
    fb = np.block([[cb, sb], [-sb, cb]])
    fc = np.block([[cb, -sb], [sb, cb]])
    kn = (jnp.arange(n1, dtype=jnp.int32)[:, None] * jnp.arange(n2, dtype=jnp.int32)[None, :]) % n
    tw = kn.astype(F32) * (2.0 * math.pi / n)
    twr = jnp.broadcast_to(jnp.cos(tw)[:, :, None], (n1, n2, LANES))
    twi = jnp.broadcast_to(-jnp.sin(tw)[:, :, None], (n1, n2, LANES))
    return dict(fa=_bf16_pair(fa), fa_real=_bf16_pair(fa_real), fd=_bf16_pair(fd),
                fb=_bf16_pair(fb), fc=_bf16_pair(fc), twr=twr, twi=twi)


def _fa_kernel(x_ref, mh_ref, ml_ref, o_ref, xh_scr, *, packed, part_axis):
    @pl.when(pl.program_id(part_axis) == 0)
    def _():
        if packed:
            x = x_ref[...]
            x = x.reshape(x.shape[0] * x.shape[1], x.shape[2], x.shape[3])
            x = pltpu.einshape("rnc->r(nc)", x)
        else:
            x = x_ref[0]
        xh_scr[...] = x.astype(BF16)

    xh = xh_scr[...]
    r = _dot(mh_ref[...], xh) + _dot(ml_ref[...], xh)
    o_ref[0] = pltpu.einshape("r(nc)->rnc", r, n=FFT_ROWS)


def _fa_call(x4, mats, n1):
    mh, ml = mats
    _, h1, n2, wd = x4.shape
    wt = FFT_ROWS * wd
    return pl.pallas_call(
        functools.partial(_fa_kernel, packed=True, part_axis=1),
        grid=(n2 // FFT_ROWS, 2),
        in_specs=[pl.BlockSpec((2, h1, FFT_ROWS, wd), lambda j, p: (0, 0, j, 0)),
                  pl.BlockSpec((n1, n1), lambda j, p: (p, 0)),
                  pl.BlockSpec((n1, n1), lambda j, p: (p, 0))],
        out_specs=pl.BlockSpec((1, n1, FFT_ROWS, wd), lambda j, p: (p, 0, j, 0)),
        out_shape=jax.ShapeDtypeStruct((2, n1, n2, wd), F32),
        scratch_shapes=[pltpu.VMEM((n1, wt), BF16)],
        compiler_params=_cparams("arbitrary", "arbitrary"),
        name="fft_a",
    )(x4, mh, ml)


def _fa_real_call(x, mats, n1):
    mh, ml = mats
    ng, h1, lanes = x.shape
    wt = FFT_ROWS * HY_W
    n2 = lanes // HY_W
    out = pl.pallas_call(
        functools.partial(_fa_kernel, packed=False, part_axis=2),
        grid=(ng, lanes // wt, 2),
        in_specs=[pl.BlockSpec((1, h1, wt), lambda g, j, p: (g, 0, j)),
                  pl.BlockSpec((n1, h1), lambda g, j, p: (p, 0)),
                  pl.BlockSpec((n1, h1), lambda g, j, p: (p, 0))],
        out_specs=pl.BlockSpec((1, n1, FFT_ROWS, HY_W), lambda g, j, p: (2 * g + p, 0, j, 0)),
        out_shape=jax.ShapeDtypeStruct((2 * ng, n1, n2, HY_W), F32),
        scratch_shapes=[pltpu.VMEM((h1, wt), BF16)],
        compiler_params=_cparams("arbitrary", "arbitrary", "arbitrary"),
        name="fft_a_real",
    )(x, mh, ml)
    return out.reshape(ng, 2, n1, n2, HY_W)


def _lane_tile(a, width):
    return jnp.concatenate([a] * (width // a.shape[-1]), axis=-1)


def _fb_filter_kernel(tf_ref, tb_ref, twr_ref, twi_ref, fh_ref, fl_ref, sf_ref, sb_ref, g_ref, *, n2, scale):
    w = g_ref.shape[-1]
    inv = jnp.concatenate([scale / sf_ref[0:1, :], scale / sb_ref[0:1, :]], axis=1)

    def body(kk, carry):
        twr = _lane_tile(twr_ref[kk], 2 * w)
        twi = _lane_tile(twi_ref[kk], 2 * w)
        tr = jnp.concatenate([tf_ref[0, 0, kk], tb_ref[0, 0, kk]], axis=1)
        ti = jnp.concatenate([tf_ref[0, 1, kk], tb_ref[0, 1, kk]], axis=1)
        p = jnp.concatenate([tr * twr - ti * twi, tr * twi + ti * twr], axis=0)
        z = _dot_mat(fh_ref[...], fl_ref[...], p) * inv
        g_ref[0, 0, kk] = z[:n2, :w] + z[:n2, w:]
        g_ref[0, 1, kk] = z[n2:, :w] - z[n2:, w:]
        return carry

    lax.fori_loop(0, FFT_ROWS, body, 0, unroll=2)


def _fb_filter_call(thf, thb, sums, tabs, n1, n2):
    fh, fl = tabs['fb']
    tspec = pl.BlockSpec((1, 2, FFT_ROWS, n2, HY_W), lambda g, i: (g, 0, i, 0, 0))
    twspec = pl.BlockSpec((FFT_ROWS, n2, LANES), lambda g, i: (i, 0, 0))
    sspec = lambda d: pl.BlockSpec((8, HY_W), lambda g, i: (0, 2 * g + d))
    return pl.pallas_call(
        functools.partial(_fb_filter_kernel, n2=n2, scale=1.0 / (n1 * n2)),
        grid=(HY_ORDER, n1 // FFT_ROWS),
        in_specs=[tspec, tspec, twspec, twspec,
                  pl.BlockSpec(fh.shape, lambda g, i: (0, 0)), pl.BlockSpec(fl.shape, lambda g, i: (0, 0)),
                  sspec(0), sspec(1)],
        out_specs=pl.BlockSpec((1, 2, FFT_ROWS, n2, HY_W), lambda g, i: (g, 0, i, 0, 0)),
        out_shape=jax.ShapeDtypeStruct((HY_ORDER, 2, n1, n2, HY_W), F32),
        compiler_params=_cparams("arbitrary", "arbitrary"),
        name="fft_b_filter",
    )(thf, thb, tabs['twr'], tabs['twi'], fh, fl, sums, sums)


def _fb_kernel(t_ref, twr_ref, twi_ref, fh_ref, fl_ref, ch_ref, cl_ref, g_ref, o_ref, t4_scr, *, n2):
    w = g_ref.shape[-1]

    def body(kk, carry):
        tr = t_ref[0, kk]
        ti = t_ref[1, kk]
        twr = _lane_tile(twr_ref[kk], w)
        twi = _lane_tile(twi_ref[kk], w)
        p = jnp.concatenate([tr * twr - ti * twi, tr * twi + ti * twr], axis=0)
        z = _dot_mat(fh_ref[...], fl_ref[...], p)
        zr, zi = z[:n2], z[n2:]
        gr = g_ref[0, 0, kk]
        gi = g_ref[0, 1, kk]
        y = jnp.concatenate([zr * gr - zi * gi, zr * gi + zi * gr], axis=0)
        v = _dot_mat(ch_ref[...], cl_ref[...], y)
        vr, vi = v[:n2], v[n2:]
        t4_scr[0, kk] = vr * twr + vi * twi
        t4_scr[1, kk] = vi * twr - vr * twi
        return carry

    lax.fori_loop(0, FFT_ROWS, body, 0, unroll=4)
    o_ref[...] = pltpu.einshape("pknc->pk(nc)", t4_scr[...])


def _fb_call(t, g, order, tabs, n1, n2):
    fh, fl = tabs['fb']
    ch, cl = tabs['fc']
    lanes = n2 * HY_W
    mat = lambda m: pl.BlockSpec(m.shape, lambda i: (0, 0))
    dat = pl.BlockSpec((2, FFT_ROWS, lanes), lambda i: (0, i, 0))
    twspec = pl.BlockSpec((FFT_ROWS, n2, LANES), lambda i: (i, 0, 0))
    return pl.pallas_call(
        functools.partial(_fb_kernel, n2=n2),
        grid=(n1 // FFT_ROWS,),
        in_specs=[pl.BlockSpec((2, FFT_ROWS, n2, HY_W), lambda i: (0, i, 0, 0)),
                  twspec, twspec, mat(fh), mat(fl), mat(ch), mat(cl),
                  pl.BlockSpec((1, 2, FFT_ROWS, n2, HY_W), lambda i: (order, 0, i, 0, 0))],
        out_specs=dat,
        out_shape=jax.ShapeDtypeStruct((2, n1, lanes), F32),
        scratch_shapes=[pltpu.VMEM((2, FFT_ROWS, n2, HY_W), F32)],
        compiler_params=_cparams("arbitrary"),
        name="fft_b",
    )(t, tabs['twr'], tabs['twi'], fh, fl, ch, cl, g)


def _fd_kernel(u_ref, mh_ref, ml_ref, z_ref, x_ref, sk_ref, o_ref):
    u = u_ref[...]
    u = u.reshape(u.shape[0] * u.shape[1], u.shape[2])
    y = _dot_mat(mh_ref[...], ml_ref[...], u)
    y = pltpu.einshape("r(nc)->rnc", y, n=FFT_ROWS)
    o_ref[0] = x_ref[0] * (y + sk_ref[...] * z_ref[0])


def _fd_call(u, mats, z4, x4, skip_row, n1):
    mh, ml = mats
    b_, h1, n2, wd = z4.shape
    wt = FFT_ROWS * wd
    dat = pl.BlockSpec((1, h1, FFT_ROWS, wd), lambda j, p: (p, 0, j, 0))
    return pl.pallas_call(
        _fd_kernel,
        grid=(n2 // FFT_ROWS, b_),
        in_specs=[pl.BlockSpec((2, n1, wt), lambda j, p: (0, 0, j)),
                  pl.BlockSpec((h1, 2 * n1), lambda j, p: (p, 0)),
                  pl.BlockSpec((h1, 2 * n1), lambda j, p: (p, 0)),
                  dat, dat, pl.BlockSpec((1, wd), lambda j, p: (0, 0))],
        out_specs=dat,
        out_shape=jax.ShapeDtypeStruct(z4.shape, F32),
        compiler_params=pltpu.CompilerParams(dimension_semantics=("arbitrary", "arbitrary"),
                                             vmem_limit_bytes=FFT_D_VMEM_LIMIT),
        name="fft_d",
    )(u, mh, ml, z4, x4, skip_row)


def _hyena_filter_spectrum(L, tabs, n1, n2, fp):
    f32 = F32
    t = jnp.linspace(0.0, 1.0, L, dtype=f32)[:, None]
    w = 2.0 * math.pi * jnp.arange(L, dtype=f32)[:, None] / L
    fr = jnp.linspace(1e-4, HY_BANDS - 1, HY_BANDS, dtype=f32)[None]
    emb = jnp.concatenate([t, jnp.cos(fr * w), -jnp.sin(fr * w)], axis=-1)
    emb = jnp.pad(emb, ((0, 0), (0, LANES - HY_EMB)))
    deltas = jnp.abs(jnp.linspace(math.log(HY_DECAY_TARGET) / HY_DECAY_SHORT_PCT,
                                  math.log(HY_DECAY_TARGET) / HY_DECAY_LONG_PCT, N_FILT, dtype=f32))[None]
    f1_w, f1_b, f1_freq, f2_w, f2_b, f2_freq, f3_w = fp
    w1 = jnp.pad(f1_w, ((0, LANES - HY_EMB), (0, 0)))
    hf, hb, sums = _filter_call(emb, w1, f1_b[None], f1_freq[None], f2_w, f2_b[None],
                                f2_freq[None], f3_w, deltas, n1, n2)
    thf = _fa_real_call(hf, tabs['fa_real'], n1)
    thb = _fa_real_call(hb, tabs['fa_real'], n1)
    return _fb_filter_call(thf, thb, sums, tabs, n1, n2)


def _hyena_conv(z, xg, g, order, skip, tabs, n1, n2):
    b_, L, wd = z.shape
    assert b_ == 2, "the two batch rows ride as the real and imaginary parts of one transform"
    z4 = z.reshape(b_, n1 // 2, n2, wd)
    x4 = xg.reshape(b_, n1 // 2, n2, wd)
    t = _fa_call(z4, tabs['fa'], n1)
    u = _fb_call(t, g, order, tabs, n1, n2)
    out = _fd_call(u, tabs['fd'], z4, x4, skip[None, :], n1)
    return out.reshape(b_, L, wd)


def _merge_kernel(yg_ref, ya_ref, yh_ref, yz_ref, mg_ref, ma_ref, mh_ref, x_ref, gate_ref,
                  wg_ref, wa_ref, wh_ref, wo_ref, fn_ref, o_ref, *, final):
    yh = (yh_ref[0] * _silu(yz_ref[0].astype(F32))).astype(BF16)
    y = (_sigmoid(mg_ref[0].astype(F32)) * _dot(yg_ref[0], wg_ref[...])
         + _sigmoid(ma_ref[0].astype(F32)) * _dot(ya_ref[0], wa_ref[...])
         + _sigmoid(mh_ref[0].astype(F32)) * _dot(yh, wh_ref[...]))
    out = _dot(y.astype(BF16), wo_ref[...])
    xn = x_ref[0] + gate_ref[0] * out
    if final:
        ms = jnp.mean(xn * xn, axis=-1, keepdims=True)
        xn = xn * lax.rsqrt(ms + EPS) * fn_ref[...]
    o_ref[0] = xn


def _merge_call(y_gla, y_attn, y_hy, P, x, gate, wg, wa, wh, wo, fnorm, final):
    b_, L, d = x.shape
    T = min(1024, L)
    row = lambda w, blk=0: pl.BlockSpec((1, T, w), lambda b, t: (b, t, blk))
    const = lambda r, w: pl.BlockSpec((r, w), lambda b, t: (0, 0))
    mblk = COLS['m'][0] // d
    return pl.pallas_call(
        functools.partial(_merge_kernel, final=final),
        grid=(b_, L // T),
        in_specs=[row(BRANCH_W), row(BRANCH_W), row(BRANCH_W), row(HY_W, COLS['y_z'][0] // HY_W),
                  row(d, mblk), row(d, mblk + 1), row(d, mblk + 2), row(d),
                  pl.BlockSpec((1, 1, d), lambda b, t: (b, 0, 0)),
                  const(BRANCH_W, d), const(BRANCH_W, d), const(BRANCH_W, d), const(d, d), const(1, d)],
        out_specs=row(d),
        out_shape=jax.ShapeDtypeStruct((b_, L, d), F32),
        compiler_params=_cparams("arbitrary", "arbitrary"),
        name="merge",
    )(y_gla, y_attn, y_hy, P, P, P, P, x, gate, wg, wa, wh, wo, fnorm)


def _pack_w_in(w):
    parts = dict(zip(('g_q', 'g_k', 'g_v', 'g_z', 'g_af', 'g_ab', 'a_q', 'a_k', 'a_v', 'a_z', 'y_u', 'y_z', 'm'),
                     jnp.split(w, [int(i) for i in np.cumsum(SPLITS)[:-1]], axis=-1)))
    parts['g_a'] = jnp.pad(jnp.concatenate([parts['g_af'], parts['g_ab']], axis=-1),
                           ((0, 0), (0, LANES - 2 * GLA_RANK)))
    parts['g_a_lo'] = jnp.zeros_like(parts['g_a'])
    order = sorted(COLS, key=lambda n: COLS[n][0])
    return jnp.concatenate([parts[n] for n in order], axis=-1).astype(BF16)


def _rope_tables(L, heads):
    t = jnp.arange(L)
    row = (t // GRID_W).astype(F32)
    colp = (t % GRID_W).astype(F32)
    half = ATTN_HD // 2
    inv = ROPE_THETA ** (-jnp.arange(0, half, 2, dtype=F32) / half)
    ang = jnp.concatenate([row[:, None] * inv, colp[:, None] * inv], axis=-1)
    cos = jnp.repeat(jnp.cos(ang), 2, axis=-1)
    sin = jnp.stack([-jnp.sin(ang), jnp.sin(ang)], axis=-1).reshape(L, ATTN_HD)
    return jnp.tile(cos, (1, heads)), jnp.tile(sin, (1, heads))


def _identity_rope(L, heads):
    return jnp.ones((L, heads * ATTN_HD), F32), jnp.zeros((L, heads * ATTN_HD), F32)


def _block_diag_ones(width):
    i = np.arange(width) // ATTN_HD
    return jnp.asarray((i[:, None] == i[None, :]).astype(np.float32)).astype(BF16)


def kernel(x, c, ctx, c_ctx, w_ada, b_ada, w_in, gla_wa_f, gla_ba_f, gla_wa_b, gla_ba_b, gla_norm,
           attn_qnorm, attn_knorm, hy_conv_w, hy_conv_b, hy_f1_w, hy_f1_b, hy_f1_freq, hy_f2_w,
           hy_f2_b, hy_f2_freq, hy_f3_w, hy_skip, w_br_gla, w_br_attn, w_br_hy, w_out, final_norm):
    b_, L, d = x.shape
    Lc = ctx.shape[1]
    depth = w_ada.shape[0]

    cos_q, sin_q = _rope_tables(L, ATTN_HEADS)
    cos_k, sin_k = cos_q[:, :ATTN_KVW], sin_q[:, :ATTN_KVW]
    cos_qc, sin_qc = _identity_rope(Lc, ATTN_HEADS)
    cos_kc, sin_kc = cos_qc[:, :ATTN_KVW], sin_qc[:, :ATTN_KVW]
    bdq, bdk = _block_diag_ones(ATTN_QW), _block_diag_ones(ATTN_KVW)
    gw = ATTN_GROUP * ATTN_HD
    e_heads = jnp.stack([jnp.eye(ATTN_HD, gw, k=h * ATTN_HD, dtype=BF16) for h in range(ATTN_GROUP)])
    n1, n2 = _fft_sizes(2 * L)
    n1c, n2c = _fft_sizes(2 * Lc)
    tabs = _fft_tables(n1, n2)
    tabs_c = _fft_tables(n1c, n2c)
    zero_state = jnp.zeros((b_, GLA_DV, GLA_KW), F32)

    cmat = jnp.concatenate([c, c_ctx[None], jnp.zeros((8 - b_ - 1, d), F32)], axis=0)
    mods = _ada_call(cmat, w_ada, b_ada)

    for l in range(depth):
        need_ctx = l < depth - 1
        shift, scale, gate = [m[:b_, None, :] for m in jnp.split(mods[l], 3, axis=-1)]
        shift_c, scale_c, gate_c = [jnp.broadcast_to(m[b_:b_ + 1, None, :], (b_, 1, d))
                                    for m in jnp.split(mods[l], 3, axis=-1)]
        w_packed = _pack_w_in(w_in[l])
        P = _proj_call(x, scale, shift, w_packed)
        Pc = _proj_call(ctx, scale_c, shift_c, w_packed)

        rk = GLA_RANK
        wa_f = jnp.zeros((LANES, GLA_KW), F32).at[0:rk].set(gla_wa_f[l])
        wa_b = jnp.zeros((LANES, GLA_KW), F32).at[rk:2 * rk].set(gla_wa_b[l])
        ba_f, ba_b = gla_ba_f[l][None], gla_ba_b[l][None]
        nw = gla_norm[l][None]
        oc_f, sc_f = _gla_call(Pc, wa_f, ba_f, zero_state, reverse=False)
        yc_gla, sc_b = _gla_call(Pc, wa_b, ba_b, zero_state, reverse=True, fin=(oc_f, nw))
        o_f, _ = _gla_call(P, wa_f, ba_f, sc_f, reverse=False)
        y_gla, _ = _gla_call(P, wa_b, ba_b, sc_b, reverse=True, fin=(o_f, nw))

        gq = jnp.tile(attn_qnorm[l], ATTN_HEADS)[None]
        gk = jnp.tile(attn_knorm[l], ATTN_KV_HEADS)[None]
        q_a, k_a, v_a = _qkv_prep_call(P, cos_q, sin_q, cos_k, sin_k, gq, gk, bdq, bdk)
        qc_a, kc_a, vc_a = _qkv_prep_call(Pc, cos_qc, sin_qc, cos_kc, sin_kc, gq, gk, bdq, bdk)
        y_attn = _attn_call(q_a, jnp.concatenate([k_a, kc_a], axis=2),
                            jnp.concatenate([v_a, vc_a], axis=2), P, e_heads)

        fp = (hy_f1_w[l], hy_f1_b[l], hy_f1_freq[l], hy_f2_w[l], hy_f2_b[l], hy_f2_freq[l], hy_f3_w[l])
        g_spec = _hyena_filter_spectrum(L, tabs, n1, n2, fp)
        v0, x1, x2 = _short_conv_call(P, hy_conv_w[l], hy_conv_b[l][None])
        z1 = _hyena_conv(v0, x1, g_spec, 0, hy_skip[l, 0], tabs, n1, n2)
        y_hy = _hyena_conv(z1, x2, g_spec, 1, hy_skip[l, 1], tabs, n1, n2)

        wg, wa, wh, wo = (w_br_gla[l].astype(BF16), w_br_attn[l].astype(BF16),
                          w_br_hy[l].astype(BF16), w_out[l].astype(BF16))
        fn = final_norm[None]
        x_new = _merge_call(y_gla, y_attn, y_hy, P, x, gate, wg, wa, wh, wo, fn, final=not need_ctx)

        if need_ctx:
            yc_attn = _attn_call(qc_a, kc_a, vc_a, Pc, e_heads)
            gc_spec = _hyena_filter_spectrum(Lc, tabs_c, n1c, n2c, fp)
            vc0, xc1, xc2 = _short_conv_call(Pc, hy_conv_w[l], hy_conv_b[l][None])
            zc1 = _hyena_conv(vc0, xc1, gc_spec, 0, hy_skip[l, 0], tabs_c, n1c, n2c)
            yc_hy = _hyena_conv(zc1, xc2, gc_spec, 1, hy_skip[l, 1], tabs_c, n1c, n2c)
            ctx = _merge_call(yc_gla, yc_attn, yc_hy, Pc, ctx, gate_c, wg, wa, wh, wo, fn, final=False)
        x = x_new

    return x
```

```python
import functools
import math

import numpy as np
import jax
import jax.numpy as jnp
from jax import lax
from jax.experimental import pallas as pl
from jax.experimental.pallas import tpu as pltpu

F32 = jnp.float32
BF16 = jnp.bfloat16
F8 = jnp.float8_e4m3fn

D_MODEL = 1024
GRID_W = 64
BRANCH_W = D_MODEL // 2
N_BRANCH = 3
EPS = 1e-6
GLA_HEADS = 4
GLA_DV = BRANCH_W // GLA_HEADS
GLA_DK = GLA_DV // 2
GLA_KW = GLA_HEADS * GLA_DK
GLA_VW = GLA_HEADS * GLA_DV
GLA_RANK = 16
GLA_GATE_NORM = 16.0
GLA_CHUNK = 64
GLA_BLOCK = 256
ATTN_HD = 64
ATTN_HEADS = BRANCH_W // ATTN_HD
ATTN_KV_HEADS = ATTN_HEADS // 4
ATTN_GROUP = ATTN_HEADS // ATTN_KV_HEADS
ATTN_QW = ATTN_HEADS * ATTN_HD
ATTN_KVW = ATTN_KV_HEADS * ATTN_HD
ROPE_THETA = 10000.0
HY_W = BRANCH_W
HY_ORDER = 2
HY_EMB = 33
HY_BANDS = (HY_EMB - 1) // 2
HY_FFN = 64
HY_SHORT = 3
HY_MOD_SHIFT = 0.05
HY_DECAY_SHORT_PCT = 0.3
HY_DECAY_LONG_PCT = 1.5
HY_DECAY_TARGET = 1e-2
N_FILT = 2 * HY_ORDER * HY_W
SPLITS = (GLA_KW, GLA_KW, GLA_VW, GLA_VW, GLA_RANK, GLA_RANK,
          ATTN_QW, ATTN_KVW, ATTN_KVW, ATTN_QW,
          (HY_ORDER + 1) * HY_W, HY_W,
          N_BRANCH * D_MODEL)

LANES = 128
VMEM_LIMIT = 48 * 1024 * 1024
FFT_D_VMEM_LIMIT = 56 * 1024 * 1024

COLS = {
    'm': (0, 3072), 'y_u': (3072, 1536), 'g_v': (4608, 512), 'g_z': (5120, 512),
    'a_q': (5632, 512), 'a_z': (6144, 512), 'y_z': (6656, 512), 'g_q': (7168, 256),
    'g_k': (7424, 256), 'a_k': (7680, 128), 'a_v': (7808, 128), 'g_a': (7936, 128),
    'g_a_lo': (8064, 128),
}
N_PROJ = 8192
PROJ_TN = 2048


def _cparams(*sem):
    return pltpu.CompilerParams(dimension_semantics=sem, vmem_limit_bytes=VMEM_LIMIT)


def _split(a):
    hi = a.astype(BF16)
    lo = (a - hi.astype(F32)).astype(BF16)
    return hi, lo


def _dot(a, b):
    return jnp.dot(a, b, preferred_element_type=F32)


def _dot_hp(a, b):
    ah, al = _split(a)
    bh, bl = _split(b)
    return _dot(ah, bh) + _dot(al, bh) + _dot(ah, bl)


def _dot_mat(mh, ml, x):
    xh = x.astype(BF16)
    return _dot(mh, xh) + _dot(ml, xh)


def _sigmoid(x):
    return 1.0 / (1.0 + jnp.exp(-x))


def _silu(x):
    return x * _sigmoid(x)


def _ada_kernel(c_ref, w_ref, b_ref, o_ref):
    o_ref[0] = _dot_hp(_silu(c_ref[...]), w_ref[0]) + b_ref[0]


def _ada_call(cmat, w_ada, b_ada):
    depth, d, n3 = w_ada.shape
    tn = 1024
    return pl.pallas_call(
        _ada_kernel,
        grid=(depth, n3 // tn),
        in_specs=[pl.BlockSpec((8, d), lambda l, j: (0, 0)),
                  pl.BlockSpec((1, d, tn), lambda l, j: (l, 0, j)),
                  pl.BlockSpec((1, 1, tn), lambda l, j: (l, 0, j))],
        out_specs=pl.BlockSpec((1, 8, tn), lambda l, j: (l, 0, j)),
        out_shape=jax.ShapeDtypeStruct((depth, 8, n3), F32),
        compiler_params=_cparams("arbitrary", "arbitrary"),
        name="ada",
    )(cmat, w_ada, b_ada.reshape(depth, 1, n3))


def _proj_kernel(x_ref, sc_ref, sh_ref, w_ref, o_ref, *, res_tile, hi_off, lo_off):
    x = x_ref[0]
    ms = jnp.mean(x * x, axis=-1, keepdims=True)
    h = x * lax.rsqrt(ms + EPS) * (1.0 + sc_ref[0]) + sh_ref[0]
    res = _dot(h.astype(BF16), w_ref[...])
    out = res.astype(o_ref.dtype)
    o_ref[0] = out

    @pl.when(pl.program_id(0) == res_tile)
    def _():
        hi = slice(hi_off, hi_off + LANES)
        o_ref[0, :, lo_off:lo_off + LANES] = (res[:, hi] - out[:, hi].astype(F32)).astype(o_ref.dtype)


def _proj_call(x, scale, shift, w):
    b_, L, d = x.shape
    n = w.shape[1]
    tm = min(1024, L)
    tn = PROJ_TN
    hi0, lo0 = COLS['g_a'][0], COLS['g_a_lo'][0]
    assert hi0 // tn == lo0 // tn
    return pl.pallas_call(
        functools.partial(_proj_kernel, res_tile=hi0 // tn, hi_off=hi0 % tn, lo_off=lo0 % tn),
        grid=(n // tn, b_, L // tm),
        in_specs=[pl.BlockSpec((1, tm, d), lambda j, b, i: (b, i, 0)),
                  pl.BlockSpec((1, 1, d), lambda j, b, i: (b, 0, 0)),
                  pl.BlockSpec((1, 1, d), lambda j, b, i: (b, 0, 0)),
                  pl.BlockSpec((d, tn), lambda j, b, i: (0, j))],
        out_specs=pl.BlockSpec((1, tm, tn), lambda j, b, i: (b, i, j)),
        out_shape=jax.ShapeDtypeStruct((b_, L, n), BF16),
        compiler_params=_cparams("arbitrary", "arbitrary", "arbitrary"),
        name="proj",
    )(x, scale, shift, w)


def _gla_kernel(*refs, reverse, finalize, nchunks):
    if finalize:
        (q_ref, k_ref, v_ref, ga_ref, gal_ref, wa_ref, ba_ref, s0_ref, tri_ref, op_ref, z_ref, nw_ref,
         o_ref, sfin_ref, s_scr) = refs
    else:
        (q_ref, k_ref, v_ref, ga_ref, gal_ref, wa_ref, ba_ref, s0_ref, tri_ref,
         o_ref, sfin_ref, s_scr) = refs
    C = GLA_CHUNK
    T = nchunks * C
    nt_dims = (((1,), (1,)), ((), ()))
    tn_dims = (((0,), (0,)), ((), ()))

    @pl.when(pl.program_id(1) == 0)
    def _():
        s_scr[...] = s0_ref[0]

    ga = ga_ref[0].astype(F32) + gal_ref[0].astype(F32)
    xg = _dot_hp(ga, wa_ref[...]) + ba_ref[...]
    g = (jnp.minimum(xg, 0.0) - jnp.log(1.0 + jnp.exp(-jnp.abs(xg)))) * (1.0 / GLA_GATE_NORM)

    tri = tri_ref[...]
    g1 = g.astype(BF16)
    r1 = g - g1.astype(F32)
    g2 = r1.astype(BF16)
    g3 = (r1 - g2.astype(F32)).astype(BF16)
    b = _dot(tri, g1) + _dot(tri, g2) + _dot(tri, g3)

    def chunk_row(r):
        return jnp.concatenate([jnp.broadcast_to(b[c * C + r:c * C + r + 1], (C, GLA_KW))
                                for c in range(nchunks)], axis=0)

    tot_row = 0 if reverse else C - 1
    bm = chunk_row(C // 2)
    bt = chunk_row(tot_row)
    ri = lax.broadcasted_iota(jnp.int32, (T, T), 0)
    ci = lax.broadcasted_iota(jnp.int32, (T, T), 1)
    same_chunk = (ri // C) == (ci // C)
    mask = same_chunk & ((ci >= ri) if reverse else (ci <= ri))
    lane = lax.broadcasted_iota(jnp.int32, (1, GLA_KW), 1)
    hmask = [(lane >= h * GLA_DK) & (lane < (h + 1) * GLA_DK) for h in range(GLA_HEADS)]

    q = q_ref[0].astype(F32) * (GLA_DK ** -0.5)
    k = k_ref[0].astype(F32)
    v = v_ref[0].astype(BF16)
    qa = q * jnp.exp(b - bm)
    kb = (k * jnp.exp(bm - b)).astype(BF16)
    qe = q * jnp.exp(b)
    kd = (k * jnp.exp(bt - b)).astype(BF16)

    intra = []
    for h in range(GLA_HEADS):
        qa_h = jnp.where(hmask[h], qa, 0.0).astype(BF16)
        att = lax.dot_general(qa_h, kb, nt_dims, preferred_element_type=F32)
        att = jnp.where(mask, att, 0.0).astype(BF16)
        intra.append(_dot(att, v[:, h * GLA_DV:(h + 1) * GLA_DV]))

    inter = [None] * nchunks
    order = range(nchunks - 1, -1, -1) if reverse else range(nchunks)
    for c in order:
        rows = slice(c * C, (c + 1) * C)
        st = s_scr[...]
        st_b = st.astype(BF16)
        dec = jnp.exp(b[c * C + tot_row:c * C + tot_row + 1])
        upd = jnp.zeros_like(st)
        outs = []
        for h in range(GLA_HEADS):
            qe_h = jnp.where(hmask[h], qe[rows], 0.0).astype(BF16)
            outs.append(lax.dot_general(qe_h, st_b, nt_dims, preferred_element_type=F32))
            u_h = lax.dot_general(v[rows, h * GLA_DV:(h + 1) * GLA_DV], kd[rows], tn_dims,
                                  preferred_element_type=F32)
            upd = upd + jnp.where(hmask[h], u_h, 0.0)
        s_scr[...] = dec * st + upd
        inter[c] = jnp.concatenate(outs, axis=1)
    o = jnp.concatenate(intra, axis=1) + jnp.concatenate(inter, axis=0)

    if finalize:
        o = o + op_ref[0]
        parts = []
        for h in range(GLA_HEADS):
            oh = o[:, h * GLA_DV:(h + 1) * GLA_DV]
            ms = jnp.mean(oh * oh, axis=-1, keepdims=True)
            parts.append(oh * lax.rsqrt(ms + EPS) * nw_ref[...])
        y = jnp.concatenate(parts, axis=1) * _silu(z_ref[0].astype(F32))
        o_ref[0] = y.astype(o_ref.dtype)
    else:
        o_ref[0] = o

    @pl.when(pl.program_id(1) == pl.num_programs(1) - 1)
    def _():
        sfin_ref[0] = s_scr[...]


def _gla_call(P, wa_pad, ba, s0, reverse, fin=None):
    b_, L, _ = P.shape
    T = min(GLA_BLOCK, L)
    nt = L // T
    if reverse:
        tmap = lambda t: nt - 1 - t
    else:
        tmap = lambda t: t
    ii = np.arange(T)
    same = (ii[:, None] // GLA_CHUNK) == (ii[None, :] // GLA_CHUNK)
    tri = same & ((ii[None, :] >= ii[:, None]) if reverse else (ii[None, :] <= ii[:, None]))
    tri = jnp.asarray(tri.astype(np.float32)).astype(BF16)

    def col(name, width):
        blk = COLS[name][0] // width
        return pl.BlockSpec((1, T, width), lambda b, t: (b, tmap(t), blk))

    in_specs = [col('g_q', GLA_KW), col('g_k', GLA_KW), col('g_v', GLA_VW),
                col('g_a', LANES), col('g_a_lo', LANES),
                pl.BlockSpec((LANES, GLA_KW), lambda b, t: (0, 0)),
                pl.BlockSpec((1, GLA_KW), lambda b, t: (0, 0)),
                pl.BlockSpec((1, GLA_DV, GLA_KW), lambda b, t: (b, 0, 0)),
                pl.BlockSpec((T, T), lambda b, t: (0, 0))]
    args = [P, P, P, P, P, wa_pad, ba, s0, tri]
    if fin is not None:
        o_prev, nw = fin
        in_specs += [pl.BlockSpec((1, T, GLA_VW), lambda b, t: (b, tmap(t), 0)),
                     col('g_z', GLA_VW),
                     pl.BlockSpec((1, GLA_DV), lambda b, t: (0, 0))]
        args += [o_prev, P, nw]
    out_dtype = BF16 if fin is not None else F32
    return pl.pallas_call(
        functools.partial(_gla_kernel, reverse=reverse, finalize=fin is not None, nchunks=T // GLA_CHUNK),
        grid=(b_, nt),
        in_specs=in_specs,
        out_specs=[pl.BlockSpec((1, T, GLA_VW), lambda b, t: (b, tmap(t), 0)),
                   pl.BlockSpec((1, GLA_DV, GLA_KW), lambda b, t: (b, 0, 0))],
        out_shape=[jax.ShapeDtypeStruct((b_, L, GLA_VW), out_dtype),
                   jax.ShapeDtypeStruct((b_, GLA_DV, GLA_KW), F32)],
        scratch_shapes=[pltpu.VMEM((GLA_DV, GLA_KW), F32)],
        compiler_params=_cparams("arbitrary", "arbitrary"),
        name="gla",
    )(*args)


def _norm_rope(a, gain, bd, cos, sin):
    sq = a * a
    sh, sl = _split(sq)
    ss = _dot(sh, bd) + _dot(sl, bd)
    an = a * lax.rsqrt(ss * (1.0 / ATTN_HD) + EPS) * gain
    w = a.shape[-1]
    lane = lax.broadcasted_iota(jnp.int32, an.shape, 1)
    partner = jnp.where(lane % 2 == 0, pltpu.roll(an, w - 1, 1), pltpu.roll(an, 1, 1))
    return an * cos + partner * sin


Q_SCALE = (ATTN_HD ** -0.5) * math.log2(math.e)
ATTN_QK_DEPTH = 4 * ATTN_HD
ATTN_TK_MAX = 640


def _split8(a):
    hi = a.astype(F8).astype(F32)
    lo = (a - hi).astype(F8)
    return hi.astype(BF16), lo.astype(BF16)


def _qkv_prep_kernel(q_ref, k_ref, v_ref, cq_ref, sq_ref, ck_ref, sk_ref, gq_ref, gk_ref,
                     bdq_ref, bdk_ref, qo_ref, ko_ref, vo_ref):
    hd = ATTN_HD
    q = _norm_rope(q_ref[0].astype(F32), gq_ref[...], bdq_ref[...], cq_ref[...], sq_ref[...]) * Q_SCALE
    qh, ql = _split8(q)
    qt = jnp.concatenate([qh, ql], axis=1).T
    for h in range(ATTN_HEADS):
        hi = qt[hd * h:hd * (h + 1)]
        lo = qt[ATTN_QW + hd * h:ATTN_QW + hd * (h + 1)]
        qo_ref[0, ATTN_QK_DEPTH * h:ATTN_QK_DEPTH * (h + 1), :] = jnp.concatenate([hi, hi, lo, lo], axis=0).astype(F8)
    k = _norm_rope(k_ref[0].astype(F32), gk_ref[...], bdk_ref[...], ck_ref[...], sk_ref[...])
    kh, kl = _split8(k)
    for g in range(ATTN_KV_HEADS):
        cols = slice(g * hd, (g + 1) * hd)
        ko_ref[0, g] = jnp.concatenate([kh[:, cols], kl[:, cols], kh[:, cols], kl[:, cols]], axis=1).astype(F8)
    vo_ref[0] = v_ref[0].astype(BF16).T


def _qkv_prep_call(P, cos_q, sin_q, cos_k, sin_k, gq, gk, bdq, bdk):
    b_, L, _ = P.shape
    T = min(1024, L)

    def col(name, width):
        blk = COLS[name][0] // width
        return pl.BlockSpec((1, T, width), lambda b, t: (b, t, blk))

    tab = lambda w: pl.BlockSpec((T, w), lambda b, t: (t, 0))
    const = lambda r, w: pl.BlockSpec((r, w), lambda b, t: (0, 0))
    return pl.pallas_call(
        _qkv_prep_kernel,
        grid=(b_, L // T),
        in_specs=[col('a_q', ATTN_QW), col('a_k', ATTN_KVW), col('a_v', ATTN_KVW),
                  tab(ATTN_QW), tab(ATTN_QW), tab(ATTN_KVW), tab(ATTN_KVW),
                  const(1, ATTN_QW), const(1, ATTN_KVW),
                  const(ATTN_QW, ATTN_QW), const(ATTN_KVW, ATTN_KVW)],
        out_specs=[pl.BlockSpec((1, ATTN_HEADS * ATTN_QK_DEPTH, T), lambda b, t: (b, 0, t)),
                   pl.BlockSpec((1, ATTN_KV_HEADS, T, ATTN_QK_DEPTH), lambda b, t: (b, 0, t, 0)),
                   pl.BlockSpec((1, ATTN_KVW, T), lambda b, t: (b, 0, t))],
        out_shape=[jax.ShapeDtypeStruct((b_, ATTN_HEADS * ATTN_QK_DEPTH, L), F8),
                   jax.ShapeDtypeStruct((b_, ATTN_KV_HEADS, L, ATTN_QK_DEPTH), F8),
                   jax.ShapeDtypeStruct((b_, ATTN_KVW, L), BF16)],
        compiler_params=_cparams("arbitrary", "arbitrary"),
        name="qkv_prep",
    )(P, P, P, cos_q, sin_q, cos_k, sin_k, gq, gk, bdq, bdk)


ATTN_SUM_ROWS = 16
ATTN_UNROLL = 2
ATTN_TQ = 512
ATTN_COLS = 256


def _attn_kernel(qt_ref, k_ref, vt_ref, z_ref, e_ref, o_ref, m_scr, t_scr, acc_scr, s_scr, *, tk, nk):
    m_scr[...] = jnp.full(m_scr.shape, -jnp.inf, F32)
    acc_scr[...] = jnp.zeros(acc_scr.shape, F32)
    ones = jnp.ones((ATTN_SUM_ROWS, tk), BF16)

    tq = qt_ref.shape[2]
    width = ATTN_COLS if tq % ATTN_COLS == 0 else tq
    parts = [(h, slice(c, c + width)) for h in range(ATTN_GROUP) for c in range(0, tq, width)]

    def score_part(kj, slot, h, c):
        s = _dot(kj, qt_ref[0, h * ATTN_QK_DEPTH:(h + 1) * ATTN_QK_DEPTH, c])
        s_scr[slot, h, :, c] = s
        t_scr[slot, h, :, c] = jnp.max(s, axis=0, keepdims=True)

    def keys(j):
        start = j * tk if isinstance(j, int) else pl.multiple_of(j * tk, tk)
        return k_ref[0, 0, pl.ds(start, tk), :]

    k0 = keys(0)
    for h, c in parts:
        score_part(k0, 0, h, c)

    def step(j, slot):
        kj = keys(min(j + 1, nk - 1) if isinstance(j, int) else jnp.minimum(j + 1, nk - 1))
        start = j * tk if isinstance(j, int) else pl.multiple_of(j * tk, tk)
        vte = jnp.concatenate([vt_ref[0, :, pl.ds(start, tk)], ones], axis=0)
        for h, c in parts:
            score_part(kj, 1 - slot, h, c)
            m_old = m_scr[h, :, c]
            m_new = jnp.maximum(m_old, t_scr[slot, h, :, c])
            alpha = jnp.exp2(m_old - m_new)
            m_scr[h, :, c] = m_new
            p = jnp.exp2(s_scr[slot, h, :, c] - m_new).astype(BF16)
            acc_scr[h, :, c] = alpha * acc_scr[h, :, c] + _dot(vte, p)

    def trip(i, carry):
        for u in range(ATTN_UNROLL):
            step(ATTN_UNROLL * i + u, u % 2)
        return carry

    ntrips = nk // ATTN_UNROLL
    lax.fori_loop(0, ntrips, trip, 0)
    for j in range(ntrips * ATTN_UNROLL, nk):
        step(j, j % 2)
    tn = (((0,), (0,)), ((), ()))
    out = None
    for h in range(ATTN_GROUP):
        acc = acc_scr[h]
        oh, ol = _split(acc[:ATTN_HD] / acc[ATTN_HD:ATTN_HD + 1])
        part = (lax.dot_general(oh, e_ref[h], tn, preferred_element_type=F32)
                + lax.dot_general(ol, e_ref[h], tn, preferred_element_type=F32))
        out = part if out is None else out + part
    o_ref[0] = (out * _silu(z_ref[0].astype(F32))).astype(o_ref.dtype)


def _attn_call(qt, k, vt, P, e_heads):
    b_, _, L = qt.shape
    Lk = k.shape[2]
    tq = min(ATTN_TQ, L)
    tk = max(t for t in range(LANES, ATTN_TK_MAX + 1, LANES) if Lk % t == 0)
    gw = ATTN_GROUP * ATTN_HD
    zblk = COLS['a_z'][0] // gw
    return pl.pallas_call(
        functools.partial(_attn_kernel, tk=tk, nk=Lk // tk),
        grid=(b_, ATTN_KV_HEADS, L // tq),
        in_specs=[pl.BlockSpec((1, ATTN_GROUP * ATTN_QK_DEPTH, tq), lambda b, g, i: (b, g, i)),
                  pl.BlockSpec((1, 1, Lk, ATTN_QK_DEPTH), lambda b, g, i: (b, g, 0, 0)),
                  pl.BlockSpec((1, ATTN_HD, Lk), lambda b, g, i: (b, g, 0)),
                  pl.BlockSpec((1, tq, gw), lambda b, g, i: (b, i, zblk + g)),
                  pl.BlockSpec((ATTN_GROUP, ATTN_HD, gw), lambda b, g, i: (0, 0, 0))],
        out_specs=pl.BlockSpec((1, tq, gw), lambda b, g, i: (b, i, g)),
        out_shape=jax.ShapeDtypeStruct((b_, L, ATTN_QW), BF16),
        scratch_shapes=[pltpu.VMEM((ATTN_GROUP, 1, tq), F32),
                        pltpu.VMEM((2, ATTN_GROUP, 1, tq), F32),
                        pltpu.VMEM((ATTN_GROUP, ATTN_HD + ATTN_SUM_ROWS, tq), F32),
                        pltpu.VMEM((2, ATTN_GROUP, tk, tq), F32)],
        compiler_params=_cparams("arbitrary", "arbitrary", "arbitrary"),
        name="attn",
    )(qt, k, vt, P, e_heads)


def _short_conv_kernel(u_ref, up_ref, un_ref, w_ref, b_ref, v_ref, x1_ref, x2_ref):
    t = pl.program_id(1)
    nt = pl.num_programs(1)
    u = u_ref[0].astype(F32)
    T = u.shape[0]
    hr = up_ref.shape[1]
    prev_row = jnp.where(t > 0, up_ref[0, hr - 1:hr, :].astype(F32), 0.0)
    next_row = jnp.where(t < nt - 1, un_ref[0, 0:1, :].astype(F32), 0.0)
    row = lax.broadcasted_iota(jnp.int32, u.shape, 0)
    u_m1 = jnp.where(row == 0, prev_row, pltpu.roll(u, 1, 0))
    u_p1 = jnp.where(row == T - 1, next_row, pltpu.roll(u, T - 1, 0))
    w = w_ref[...]
    out = b_ref[...] + u_m1 * w[0:1] + u * w[1:2] + u_p1 * w[2:3]
    v_ref[0] = out[:, 0:HY_W]
    x1_ref[0] = out[:, HY_W:2 * HY_W]
    x2_ref[0] = out[:, 2 * HY_W:3 * HY_W]


def _short_conv_call(P, w, bias):
    b_, L, _ = P.shape
    T = min(1024, L)
    cw = (HY_ORDER + 1) * HY_W
    blk = COLS['y_u'][0] // cw
    hr = 16
    hb = T // hr
    nh = L // hr
    out_spec = pl.BlockSpec((1, T, HY_W), lambda b, t: (b, t, 0))
    shp = jax.ShapeDtypeStruct((b_, L, HY_W), F32)
    return pl.pallas_call(
        _short_conv_kernel,
        grid=(b_, L // T),
        in_specs=[pl.BlockSpec((1, T, cw), lambda b, t: (b, t, blk)),
                  pl.BlockSpec((1, hr, cw), lambda b, t: (b, jnp.maximum(t * hb - 1, 0), blk)),
                  pl.BlockSpec((1, hr, cw), lambda b, t: (b, jnp.minimum((t + 1) * hb, nh - 1), blk)),
                  pl.BlockSpec((HY_SHORT, cw), lambda b, t: (0, 0)),
                  pl.BlockSpec((1, cw), lambda b, t: (0, 0))],
        out_specs=[out_spec, out_spec, out_spec],
        out_shape=[shp, shp, shp],
        compiler_params=_cparams("arbitrary", "arbitrary"),
        name="short_conv",
    )(P, P, P, w, bias)


FFT_ROWS = 8


def _filter_kernel(emb_ref, w1_ref, b1_ref, f1_ref, w2_ref, b2_ref, f2_ref, w3_ref, dl_ref,
                   hf_ref, hb_ref, sum_ref):
    t = pl.program_id(0)
    emb = emb_ref[...]
    h = jnp.sin(f1_ref[...] * (_dot_hp(emb, w1_ref[...]) + b1_ref[...]))
    h = jnp.sin(f2_ref[...] * (_dot_hp(h, w2_ref[...]) + b2_ref[...]))
    hh, hl = _split(h)
    tt = emb[:, 0:1]

    @pl.when(t == 0)
    def _():
        sum_ref[...] = jnp.zeros_like(sum_ref)

    lag0 = (lax.broadcasted_iota(jnp.int32, (h.shape[0], HY_W), 0) + t * h.shape[0]) == 0
    for o in range(HY_ORDER):
        for d, out_ref in enumerate((hf_ref, hb_ref)):
            cols = slice((2 * o + d) * HY_W, (2 * o + d + 1) * HY_W)
            wh, wl = _split(w3_ref[:, cols])
            f = _dot(hh, wh) + _dot(hl, wh) + _dot(hh, wl)
            f = f * (jnp.exp(-tt * dl_ref[:, cols]) + HY_MOD_SHIFT)
            sum_ref[:, cols] += jnp.broadcast_to(jnp.sum(jnp.abs(f), axis=0, keepdims=True), (8, HY_W))
            if d == 1:
                f = jnp.where(lag0, 0.0, f)
            out_ref[o] = pltpu.einshape("(an)c->a(nc)", f, a=FFT_ROWS)


def _filter_call(emb, w1, b1, f1, w2, b2, f2, w3, deltas, n1, n2):
    L = emb.shape[0]
    TL = FFT_ROWS * n2
    const = lambda r, w: pl.BlockSpec((r, w), lambda t: (0, 0))
    hspec = pl.BlockSpec((HY_ORDER, FFT_ROWS, n2 * HY_W), lambda t: (0, t, 0))
    hshape = jax.ShapeDtypeStruct((HY_ORDER, n1 // 2, n2 * HY_W), F32)
    return pl.pallas_call(
        _filter_kernel,
        grid=(L // TL,),
        in_specs=[pl.BlockSpec((TL, LANES), lambda t: (t, 0)),
                  const(LANES, HY_FFN), const(1, HY_FFN), const(1, HY_FFN),
                  const(HY_FFN, HY_FFN), const(1, HY_FFN), const(1, HY_FFN),
                  const(HY_FFN, N_FILT), const(1, N_FILT)],
        out_specs=[hspec, hspec, pl.BlockSpec((8, N_FILT), lambda t: (0, 0))],
        out_shape=[hshape, hshape, jax.ShapeDtypeStruct((8, N_FILT), F32)],
        compiler_params=_cparams("arbitrary"),
        name="hy_filter",
    )(emb, w1, b1, f1, w2, b2, f2, w3, deltas)


def _fft_sizes(n):
    lg = int(round(math.log2(n)))
    assert 1 << lg == n
    n1 = 1 << ((lg + 1) // 2)
    return n1, n // n1


def _bf16_pair(a):
    a = np.asarray(a, np.float32)
    hi = jnp.asarray(a, F32).astype(BF16)
    lo = (jnp.asarray(a, F32) - hi.astype(F32)).astype(BF16)
    return hi, lo


def _fft_tables(n1, n2):
    n = n1 * n2
    h1 = n1 // 2
    k1 = np.arange(n1)[:, None]
    a = 2.0 * np.pi * ((k1 * np.arange(h1)[None, :]) % n1) / n1
    c, s = np.cos(a), np.sin(a)
    fa = np.block([[c, s], [-s, c]])
    fa_real = np.concatenate([c, -s], axis=0)
    fd = np.block([[c.T, -s.T], [s.T, c.T]])
    k2 = np.arange(n2)[:, None]
    b = 2.0 * np.pi * ((k2 * np.arange(n2)[None, :]) % n2) / n2
    cb, sb = np.cos(b), np.sin(b)
    fb = np.block([[cb, sb], [-sb, cb]])
    fc = np.block([[cb, -sb], [sb, cb]])
    kn = (jnp.arange(n1, dtype=jnp.int32)[:, None] * jnp.arange(n2, dtype=jnp.int32)[None, :]) % n
    tw = kn.astype(F32) * (2.0 * math.pi / n)
    twr = jnp.broadcast_to(jnp.cos(tw)[:, :, None], (n1, n2, LANES))
    twi = jnp.broadcast_to(-jnp.sin(tw)[:, :, None], (n1, n2, LANES))
    return dict(fa=_bf16_pair(fa), fa_real=_bf16_pair(fa_real), fd=_bf16_pair(fd),
                fb=_bf16_pair(fb), fc=_bf16_pair(fc), twr=twr, twi=twi)


def _fa_kernel(x_ref, mh_ref, ml_ref, o_ref, xh_scr, *, packed, part_axis):
    @pl.when(pl.program_id(part_axis) == 0)
    def _():
        if packed:
            x = x_ref[...]
            x = x.reshape(x.shape[0] * x.shape[1], x.shape[2], x.shape[3])
            x = pltpu.einshape("rnc->r(nc)", x)
        else:
            x = x_ref[0]
        xh_scr[...] = x.astype(BF16)

    xh = xh_scr[...]
    r = _dot(mh_ref[...], xh) + _dot(ml_ref[...], xh)
    o_ref[0] = pltpu.einshape("r(nc)->rnc", r, n=FFT_ROWS)


def _fa_call(x4, mats, n1):
    mh, ml = mats
    _, h1, n2, wd = x4.shape
    wt = FFT_ROWS * wd
    return pl.pallas_call(
        functools.partial(_fa_kernel, packed=True, part_axis=1),
        grid=(n2 // FFT_ROWS, 2),
        in_specs=[pl.BlockSpec((2, h1, FFT_ROWS, wd), lambda j, p: (0, 0, j, 0)),
                  pl.BlockSpec((n1, n1), lambda j, p: (p, 0)),
                  pl.BlockSpec((n1, n1), lambda j, p: (p, 0))],
        out_specs=pl.BlockSpec((1, n1, FFT_ROWS, wd), lambda j, p: (p, 0, j, 0)),
        out_shape=jax.ShapeDtypeStruct((2, n1, n2, wd), F32),
        scratch_shapes=[pltpu.VMEM((n1, wt), BF16)],
        compiler_params=_cparams("arbitrary", "arbitrary"),
        name="fft_a",
    )(x4, mh, ml)


def _fa_real_call(x, mats, n1):
    mh, ml = mats
    ng, h1, lanes = x.shape
    wt = FFT_ROWS * HY_W
    n2 = lanes // HY_W
    out = pl.pallas_call(
        functools.partial(_fa_kernel, packed=False, part_axis=2),
        grid=(ng, lanes // wt, 2),
        in_specs=[pl.BlockSpec((1, h1, wt), lambda g, j, p: (g, 0, j)),
                  pl.BlockSpec((n1, h1), lambda g, j, p: (p, 0)),
                  pl.BlockSpec((n1, h1), lambda g, j, p: (p, 0))],
        out_specs=pl.BlockSpec((1, n1, FFT_ROWS, HY_W), lambda g, j, p: (2 * g + p, 0, j, 0)),
        out_shape=jax.ShapeDtypeStruct((2 * ng, n1, n2, HY_W), F32),
        scratch_shapes=[pltpu.VMEM((h1, wt), BF16)],
        compiler_params=_cparams("arbitrary", "arbitrary", "arbitrary"),
        name="fft_a_real",
    )(x, mh, ml)
    return out.reshape(ng, 2, n1, n2, HY_W)


def _lane_tile(a, width):
    return jnp.concatenate([a] * (width // a.shape[-1]), axis=-1)


def _fb_filter_kernel(tf_ref, tb_ref, twr_ref, twi_ref, fh_ref, fl_ref, sf_ref, sb_ref, g_ref, *, n2, scale):
    w = g_ref.shape[-1]
    inv = jnp.concatenate([scale / sf_ref[0:1, :], scale / sb_ref[0:1, :]], axis=1)

    def body(kk, carry):
        twr = _lane_tile(twr_ref[kk], 2 * w)
        twi = _lane_tile(twi_ref[kk], 2 * w)
        tr = jnp.concatenate([tf_ref[0, 0, kk], tb_ref[0, 0, kk]], axis=1)
        ti = jnp.concatenate([tf_ref[0, 1, kk], tb_ref[0, 1, kk]], axis=1)
        p = jnp.concatenate([tr * twr - ti * twi, tr * twi + ti * twr], axis=0)
        z = _dot_mat(fh_ref[...], fl_ref[...], p) * inv
        g_ref[0, 0, kk] = z[:n2, :w] + z[:n2, w:]
        g_ref[0, 1, kk] = z[n2:, :w] - z[n2:, w:]
        return carry

    lax.fori_loop(0, FFT_ROWS, body, 0, unroll=2)


def _fb_filter_call(thf, thb, sums, tabs, n1, n2):
    fh, fl = tabs['fb']
    tspec = pl.BlockSpec((1, 2, FFT_ROWS, n2, HY_W), lambda g, i: (g, 0, i, 0, 0))
    twspec = pl.BlockSpec((FFT_ROWS, n2, LANES), lambda g, i: (i, 0, 0))
    sspec = lambda d: pl.BlockSpec((8, HY_W), lambda g, i: (0, 2 * g + d))
    return pl.pallas_call(
        functools.partial(_fb_filter_kernel, n2=n2, scale=1.0 / (n1 * n2)),
        grid=(HY_ORDER, n1 // FFT_ROWS),
        in_specs=[tspec, tspec, twspec, twspec,
                  pl.BlockSpec(fh.shape, lambda g, i: (0, 0)), pl.BlockSpec(fl.shape, lambda g, i: (0, 0)),
                  sspec(0), sspec(1)],
        out_specs=pl.BlockSpec((1, 2, FFT_ROWS, n2, HY_W), lambda g, i: (g, 0, i, 0, 0)),
        out_shape=jax.ShapeDtypeStruct((HY_ORDER, 2, n1, n2, HY_W), F32),
        compiler_params=_cparams("arbitrary", "arbitrary"),
        name="fft_b_filter",
    )(thf, thb, tabs['twr'], tabs['twi'], fh, fl, sums, sums)


def _fb_kernel(t_ref, twr_ref, twi_ref, fh_ref, fl_ref, ch_ref, cl_ref, g_ref, o_ref, t4_scr, *, n2):
    w = g_ref.shape[-1]

    def body(kk, carry):
        tr = t_ref[0, kk]
        ti = t_ref[1, kk]
        twr = _lane_tile(twr_ref[kk], w)
        twi = _lane_tile(twi_ref[kk], w)
        p = jnp.concatenate([tr * twr - ti * twi, tr * twi + ti * twr], axis=0)
        z = _dot_mat(fh_ref[...], fl_ref[...], p)
        zr, zi = z[:n2], z[n2:]
        gr = g_ref[0, 0, kk]
        gi = g_ref[0, 1, kk]
        y = jnp.concatenate([zr * gr - zi * gi, zr * gi + zi * gr], axis=0)
        v = _dot_mat(ch_ref[...], cl_ref[...], y)
        vr, vi = v[:n2], v[n2:]
        t4_scr[0, kk] = vr * twr + vi * twi
        t4_scr[1, kk] = vi * twr - vr * twi
        return carry

    lax.fori_loop(0, FFT_ROWS, body, 0, unroll=4)
    o_ref[...] = pltpu.einshape("pknc->pk(nc)", t4_scr[...])


def _fb_call(t, g, order, tabs, n1, n2):
    fh, fl = tabs['fb']
    ch, cl = tabs['fc']
    lanes = n2 * HY_W
    mat = lambda m: pl.BlockSpec(m.shape, lambda i: (0, 0))
    dat = pl.BlockSpec((2, FFT_ROWS, lanes), lambda i: (0, i, 0))
    twspec = pl.BlockSpec((FFT_ROWS, n2, LANES), lambda i: (i, 0, 0))
    return pl.pallas_call(
        functools.partial(_fb_kernel, n2=n2),
        grid=(n1 // FFT_ROWS,),
        in_specs=[pl.BlockSpec((2, FFT_ROWS, n2, HY_W), lambda i: (0, i, 0, 0)),
                  twspec, twspec, mat(fh), mat(fl), mat(ch), mat(cl),
                  pl.BlockSpec((1, 2, FFT_ROWS, n2, HY_W), lambda i: (order, 0, i, 0, 0))],
        out_specs=dat,
        out_shape=jax.ShapeDtypeStruct((2, n1, lanes), F32),
        scratch_shapes=[pltpu.VMEM((2, FFT_ROWS, n2, HY_W), F32)],
        compiler_params=_cparams("arbitrary"),
        name="fft_b",
    )(t, tabs['twr'], tabs['twi'], fh, fl, ch, cl, g)


def _fd_kernel(u_ref, mh_ref, ml_ref, z_ref, x_ref, sk_ref, o_ref):
    u = u_ref[...]
    u = u.reshape(u.shape[0] * u.shape[1], u.shape[2])
    y = _dot_mat(mh_ref[...], ml_ref[...], u)
    y = pltpu.einshape("r(nc)->rnc", y, n=FFT_ROWS)
    o_ref[0] = x_ref[0] * (y + sk_ref[...] * z_ref[0])


def _fd_call(u, mats, z4, x4, skip_row, n1):
    mh, ml = mats
    b_, h1, n2, wd = z4.shape
    wt = FFT_ROWS * wd
    dat = pl.BlockSpec((1, h1, FFT_ROWS, wd), lambda j, p: (p, 0, j, 0))
    return pl.pallas_call(
        _fd_kernel,
        grid=(n2 // FFT_ROWS, b_),
        in_specs=[pl.BlockSpec((2, n1, wt), lambda j, p: (0, 0, j)),
                  pl.BlockSpec((h1, 2 * n1), lambda j, p: (p, 0)),
                  pl.BlockSpec((h1, 2 * n1), lambda j, p: (p, 0)),
                  dat, dat, pl.BlockSpec((1, wd), lambda j, p: (0, 0))],
        out_specs=dat,
        out_shape=jax.ShapeDtypeStruct(z4.shape, F32),
        compiler_params=pltpu.CompilerParams(dimension_semantics=("arbitrary", "arbitrary"),
                                             vmem_limit_bytes=FFT_D_VMEM_LIMIT),
        name="fft_d",
    )(u, mh, ml, z4, x4, skip_row)


def _hyena_filter_spectrum(L, tabs, n1, n2, fp):
    f32 = F32
    t = jnp.linspace(0.0, 1.0, L, dtype=f32)[:, None]
    w = 2.0 * math.pi * jnp.arange(L, dtype=f32)[:, None] / L
    fr = jnp.linspace(1e-4, HY_BANDS - 1, HY_BANDS, dtype=f32)[None]
    emb = jnp.concatenate([t, jnp.cos(fr * w), -jnp.sin(fr * w)], axis=-1)
    emb = jnp.pad(emb, ((0, 0), (0, LANES - HY_EMB)))
    deltas = jnp.abs(jnp.linspace(math.log(HY_DECAY_TARGET) / HY_DECAY_SHORT_PCT,
                                  math.log(HY_DECAY_TARGET) / HY_DECAY_LONG_PCT, N_FILT, dtype=f32))[None]
    f1_w, f1_b, f1_freq, f2_w, f2_b, f2_freq, f3_w = fp
    w1 = jnp.pad(f1_w, ((0, LANES - HY_EMB), (0, 0)))
    hf, hb, sums = _filter_call(emb, w1, f1_b[None], f1_freq[None], f2_w, f2_b[None],
                                f2_freq[None], f3_w, deltas, n1, n2)
    thf = _fa_real_call(hf, tabs['fa_real'], n1)
    thb = _fa_real_call(hb, tabs['fa_real'], n1)
    return _fb_filter_call(thf, thb, sums, tabs, n1, n2)


def _hyena_conv(z, xg, g, order, skip, tabs, n1, n2):
    b_, L, wd = z.shape
    assert b_ == 2, "the two batch rows ride as the real and imaginary parts of one transform"
    z4 = z.reshape(b_, n1 // 2, n2, wd)
    x4 = xg.reshape(b_, n1 // 2, n2, wd)
    t = _fa_call(z4, tabs['fa'], n1)
    u = _fb_call(t, g, order, tabs, n1, n2)
    out = _fd_call(u, tabs['fd'], z4, x4, skip[None, :], n1)
    return out.reshape(b_, L, wd)


def _merge_kernel(yg_ref, ya_ref, yh_ref, yz_ref, mg_ref, ma_ref, mh_ref, x_ref, gate_ref,
                  wg_ref, wa_ref, wh_ref, wo_ref, fn_ref, o_ref, *, final):
    yh = (yh_ref[0] * _silu(yz_ref[0].astype(F32))).astype(BF16)
    y = (_sigmoid(mg_ref[0].astype(F32)) * _dot(yg_ref[0], wg_ref[...])
         + _sigmoid(ma_ref[0].astype(F32)) * _dot(ya_ref[0], wa_ref[...])
         + _sigmoid(mh_ref[0].astype(F32)) * _dot(yh, wh_ref[...]))
    out = _dot(y.astype(BF16), wo_ref[...])
    xn = x_ref[0] + gate_ref[0] * out
    if final:
        ms = jnp.mean(xn * xn, axis=-1, keepdims=True)
        xn = xn * lax.rsqrt(ms + EPS) * fn_ref[...]
    o_ref[0] = xn


def _merge_call(y_gla, y_attn, y_hy, P, x, gate, wg, wa, wh, wo, fnorm, final):
    b_, L, d = x.shape
    T = min(1024, L)
    row = lambda w, blk=0: pl.BlockSpec((1, T, w), lambda b, t: (b, t, blk))
    const = lambda r, w: pl.BlockSpec((r, w), lambda b, t: (0, 0))
    mblk = COLS['m'][0] // d
    return pl.pallas_call(
        functools.partial(_merge_kernel, final=final),
        grid=(b_, L // T),
        in_specs=[row(BRANCH_W), row(BRANCH_W), row(BRANCH_W), row(HY_W, COLS['y_z'][0] // HY_W),
                  row(d, mblk), row(d, mblk + 1), row(d, mblk + 2), row(d),
                  pl.BlockSpec((1, 1, d), lambda b, t: (b, 0, 0)),
                  const(BRANCH_W, d), const(BRANCH_W, d), const(BRANCH_W, d), const(d, d), const(1, d)],
        out_specs=row(d),
        out_shape=jax.ShapeDtypeStruct((b_, L, d), F32),
        compiler_params=_cparams("arbitrary", "arbitrary"),
        name="merge",
    )(y_gla, y_attn, y_hy, P, P, P, P, x, gate, wg, wa, wh, wo, fnorm)


def _pack_w_in(w):
    parts = dict(zip(('g_q', 'g_k', 'g_v', 'g_z', 'g_af', 'g_ab', 'a_q', 'a_k', 'a_v', 'a_z', 'y_u', 'y_z', 'm'),
                     jnp.split(w, [int(i) for i in np.cumsum(SPLITS)[:-1]], axis=-1)))
    parts['g_a'] = jnp.pad(jnp.concatenate([parts['g_af'], parts['g_ab']], axis=-1),
                           ((0, 0), (0, LANES - 2 * GLA_RANK)))
    parts['g_a_lo'] = jnp.zeros_like(parts['g_a'])
    order = sorted(COLS, key=lambda n: COLS[n][0])
    return jnp.concatenate([parts[n] for n in order], axis=-1).astype(BF16)


def _rope_tables(L, heads):
    t = jnp.arange(L)
    row = (t // GRID_W).astype(F32)
    colp = (t % GRID_W).astype(F32)
    half = ATTN_HD // 2
    inv = ROPE_THETA ** (-jnp.arange(0, half, 2, dtype=F32) / half)
    ang = jnp.concatenate([row[:, None] * inv, colp[:, None] * inv], axis=-1)
    cos = jnp.repeat(jnp.cos(ang), 2, axis=-1)
    sin = jnp.stack([-jnp.sin(ang), jnp.sin(ang)], axis=-1).reshape(L, ATTN_HD)
    return jnp.tile(cos, (1, heads)), jnp.tile(sin, (1, heads))


def _identity_rope(L, heads):
    return jnp.ones((L, heads * ATTN_HD), F32), jnp.zeros((L, heads * ATTN_HD), F32)


def _block_diag_ones(width):
    i = np.arange(width) // ATTN_HD
    return jnp.asarray((i[:, None] == i[None, :]).astype(np.float32)).astype(BF16)


def kernel(x, c, ctx, c_ctx, w_ada, b_ada, w_in, gla_wa_f, gla_ba_f, gla_wa_b, gla_ba_b, gla_norm,
           attn_qnorm, attn_knorm, hy_conv_w, hy_conv_b, hy_f1_w, hy_f1_b, hy_f1_freq, hy_f2_w,
           hy_f2_b, hy_f2_freq, hy_f3_w, hy_skip, w_br_gla, w_br_attn, w_br_hy, w_out, final_norm):
    b_, L, d = x.shape
    Lc = ctx.shape[1]
    depth = w_ada.shape[0]

    cos_q, sin_q = _rope_tables(L, ATTN_HEADS)
    cos_k, sin_k = cos_q[:, :ATTN_KVW], sin_q[:, :ATTN_KVW]
    cos_qc, sin_qc = _identity_rope(Lc, ATTN_HEADS)
    cos_kc, sin_kc = cos_qc[:, :ATTN_KVW], sin_qc[:, :ATTN_KVW]
    bdq, bdk = _block_diag_ones(ATTN_QW), _block_diag_ones(ATTN_KVW)
    gw = ATTN_GROUP * ATTN_HD
    e_heads = jnp.stack([jnp.eye(ATTN_HD, gw, k=h * ATTN_HD, dtype=BF16) for h in range(ATTN_GROUP)])
    n1, n2 = _fft_sizes(2 * L)
    n1c, n2c = _fft_sizes(2 * Lc)
    tabs = _fft_tables(n1, n2)
    tabs_c = _fft_tables(n1c, n2c)
    zero_state = jnp.zeros((b_, GLA_DV, GLA_KW), F32)

    cmat = jnp.concatenate([c, c_ctx[None], jnp.zeros((8 - b_ - 1, d), F32)], axis=0)
    mods = _ada_call(cmat, w_ada, b_ada)

    for l in range(depth):
        need_ctx = l < depth - 1
        shift, scale, gate = [m[:b_, None, :] for m in jnp.split(mods[l], 3, axis=-1)]
        shift_c, scale_c, gate_c = [jnp.broadcast_to(m[b_:b_ + 1, None, :], (b_, 1, d))
                                    for m in jnp.split(mods[l], 3, axis=-1)]
        w_packed = _pack_w_in(w_in[l])
        P = _proj_call(x, scale, shift, w_packed)
        Pc = _proj_call(ctx, scale_c, shift_c, w_packed)

        rk = GLA_RANK
        wa_f = jnp.zeros((LANES, GLA_KW), F32).at[0:rk].set(gla_wa_f[l])
        wa_b = jnp.zeros((LANES, GLA_KW), F32).at[rk:2 * rk].set(gla_wa_b[l])
        ba_f, ba_b = gla_ba_f[l][None], gla_ba_b[l][None]
        nw = gla_norm[l][None]
        oc_f, sc_f = _gla_call(Pc, wa_f, ba_f, zero_state, reverse=False)
        yc_gla, sc_b = _gla_call(Pc, wa_b, ba_b, zero_state, reverse=True, fin=(oc_f, nw))
        o_f, _ = _gla_call(P, wa_f, ba_f, sc_f, reverse=False)
        y_gla, _ = _gla_call(P, wa_b, ba_b, sc_b, reverse=True, fin=(o_f, nw))

        gq = jnp.tile(attn_qnorm[l], ATTN_HEADS)[None]
        gk = jnp.tile(attn_knorm[l], ATTN_KV_HEADS)[None]
        q_a, k_a, v_a = _qkv_prep_call(P, cos_q, sin_q, cos_k, sin_k, gq, gk, bdq, bdk)
        qc_a, kc_a, vc_a = _qkv_prep_call(Pc, cos_qc, sin_qc, cos_kc, sin_kc, gq, gk, bdq, bdk)
        y_attn = _attn_call(q_a, jnp.concatenate([k_a, kc_a], axis=2),
                            jnp.concatenate([v_a, vc_a], axis=2), P, e_heads)

        fp = (hy_f1_w[l], hy_f1_b[l], hy_f1_freq[l], hy_f2_w[l], hy_f2_b[l], hy_f2_freq[l], hy_f3_w[l])
        g_spec = _hyena_filter_spectrum(L, tabs, n1, n2, fp)
        v0, x1, x2 = _short_conv_call(P, hy_conv_w[l], hy_conv_b[l][None])
        z1 = _hyena_conv(v0, x1, g_spec, 0, hy_skip[l, 0], tabs, n1, n2)
        y_hy = _hyena_conv(z1, x2, g_spec, 1, hy_skip[l, 1], tabs, n1, n2)

        wg, wa, wh, wo = (w_br_gla[l].astype(BF16), w_br_attn[l].astype(BF16),
                          w_br_hy[l].astype(BF16), w_out[l].astype(BF16))
        fn = final_norm[None]
        x_new = _merge_call(y_gla, y_attn, y_hy, P, x, gate, wg, wa, wh, wo, fn, final=not need_ctx)

        if need_ctx:
            yc_attn = _attn_call(qc_a, kc_a, vc_a, Pc, e_heads)
            gc_spec = _hyena_filter_spectrum(Lc, tabs_c, n1c, n2c, fp)
            vc0, xc1, xc2 = _short_conv_call(Pc, hy_conv_w[l], hy_conv_b[l][None])
            zc1 = _hyena_conv(vc0, xc1, gc_spec, 0, hy_skip[l, 0], tabs_c, n1c, n2c)
            yc_hy = _hyena_conv(zc1, xc2, gc_spec, 1, hy_skip[l, 1], tabs_c, n1c, n2c)
            ctx = _merge_call(yc_gla, yc_attn, yc_hy, Pc, ctx, gate_c, wg, wa, wh, wo, fn, final=False)
        x = x_new

    return x
```

```python
import functools
import math

import numpy as np
import jax
import jax.numpy as jnp
from jax import lax
from jax.experimental import pallas as pl
from jax.experimental.pallas import tpu as pltpu

F32 = jnp.float32
BF16 = jnp.bfloat16
F8 = jnp.float8_e4m3fn

D_MODEL = 1024
GRID_W = 64
BRANCH_W = D_MODEL // 2
N_BRANCH = 3
EPS = 1e-6
GLA_HEADS = 4
GLA_DV = BRANCH_W // GLA_HEADS
GLA_DK = GLA_DV // 2
GLA_KW = GLA_HEADS * GLA_DK
GLA_VW = GLA_HEADS * GLA_DV
GLA_RANK = 16
GLA_GATE_NORM = 16.0
GLA_CHUNK = 64
GLA_BLOCK = 256
ATTN_HD = 64
ATTN_HEADS = BRANCH_W // ATTN_HD
ATTN_KV_HEADS = ATTN_HEADS // 4
ATTN_GROUP = ATTN_HEADS // ATTN_KV_HEADS
ATTN_QW = ATTN_HEADS * ATTN_HD
ATTN_KVW = ATTN_KV_HEADS * ATTN_HD
ROPE_THETA = 10000.0
HY_W = BRANCH_W
HY_ORDER = 2
HY_EMB = 33
HY_BANDS = (HY_EMB - 1) // 2
HY_FFN = 64
HY_SHORT = 3
HY_MOD_SHIFT = 0.05
HY_DECAY_SHORT_PCT = 0.3
HY_DECAY_LONG_PCT = 1.5
HY_DECAY_TARGET = 1e-2
N_FILT = 2 * HY_ORDER * HY_W
SPLITS = (GLA_KW, GLA_KW, GLA_VW, GLA_VW, GLA_RANK, GLA_RANK,
          ATTN_QW, ATTN_KVW, ATTN_KVW, ATTN_QW,
          (HY_ORDER + 1) * HY_W, HY_W,
          N_BRANCH * D_MODEL)

LANES = 128
VMEM_LIMIT = 48 * 1024 * 1024
FFT_D_VMEM_LIMIT = 56 * 1024 * 1024

COLS = {
    'm': (0, 3072), 'y_u': (3072, 1536), 'g_v': (4608, 512), 'g_z': (5120, 512),
    'a_q': (5632, 512), 'a_z': (6144, 512), 'y_z': (6656, 512), 'g_q': (7168, 256),
    'g_k': (7424, 256), 'a_k': (7680, 128), 'a_v': (7808, 128), 'g_a': (7936, 128),
    'g_a_lo': (8064, 128),
}
N_PROJ = 8192
PROJ_TN = 2048


def _cparams(*sem):
    return pltpu.CompilerParams(dimension_semantics=sem, vmem_limit_bytes=VMEM_LIMIT)


def _split(a):
    hi = a.astype(BF16)
    lo = (a - hi.astype(F32)).astype(BF16)
    return hi, lo


def _dot(a, b):
    return jnp.dot(a, b, preferred_element_type=F32)


def _dot_hp(a, b):
    ah, al = _split(a)
    bh, bl = _split(b)
    return _dot(ah, bh) + _dot(al, bh) + _dot(ah, bl)


def _dot_mat(mh, ml, x):
    xh = x.astype(BF16)
    return _dot(mh, xh) + _dot(ml, xh)


def _sigmoid(x):
    return 1.0 / (1.0 + jnp.exp(-x))


def _silu(x):
    return x * _sigmoid(x)


def _ada_kernel(c_ref, w_ref, b_ref, o_ref):
    o_ref[0] = _dot_hp(_silu(c_ref[...]), w_ref[0]) + b_ref[0]


def _ada_call(cmat, w_ada, b_ada):
    depth, d, n3 = w_ada.shape
    tn = 1024
    return pl.pallas_call(
        _ada_kernel,
        grid=(depth, n3 // tn),
        in_specs=[pl.BlockSpec((8, d), lambda l, j: (0, 0)),
                  pl.BlockSpec((1, d, tn), lambda l, j: (l, 0, j)),
                  pl.BlockSpec((1, 1, tn), lambda l, j: (l, 0, j))],
        out_specs=pl.BlockSpec((1, 8, tn), lambda l, j: (l, 0, j)),
        out_shape=jax.ShapeDtypeStruct((depth, 8, n3), F32),
        compiler_params=_cparams("arbitrary", "arbitrary"),
        name="ada",
    )(cmat, w_ada, b_ada.reshape(depth, 1, n3))


def _proj_kernel(x_ref, sc_ref, sh_ref, w_ref, o_ref, *, res_tile, hi_off, lo_off):
    x = x_ref[0]
    ms = jnp.mean(x * x, axis=-1, keepdims=True)
    h = x * lax.rsqrt(ms + EPS) * (1.0 + sc_ref[0]) + sh_ref[0]
    res = _dot(h.astype(BF16), w_ref[...])
    out = res.astype(o_ref.dtype)
    o_ref[0] = out

    @pl.when(pl.program_id(0) == res_tile)
    def _():
        hi = slice(hi_off, hi_off + LANES)
        o_ref[0, :, lo_off:lo_off + LANES] = (res[:, hi] - out[:, hi].astype(F32)).astype(o_ref.dtype)


def _proj_call(x, scale, shift, w):
    b_, L, d = x.shape
    n = w.shape[1]
    tm = min(1024, L)
    tn = PROJ_TN
    hi0, lo0 = COLS['g_a'][0], COLS['g_a_lo'][0]
    assert hi0 // tn == lo0 // tn
    return pl.pallas_call(
        functools.partial(_proj_kernel, res_tile=hi0 // tn, hi_off=hi0 % tn, lo_off=lo0 % tn),
        grid=(n // tn, b_, L // tm),
        in_specs=[pl.BlockSpec((1, tm, d), lambda j, b, i: (b, i, 0)),
                  pl.BlockSpec((1, 1, d), lambda j, b, i: (b, 0, 0)),
                  pl.BlockSpec((1, 1, d), lambda j, b, i: (b, 0, 0)),
                  pl.BlockSpec((d, tn), lambda j, b, i: (0, j))],
        out_specs=pl.BlockSpec((1, tm, tn), lambda j, b, i: (b, i, j)),
        out_shape=jax.ShapeDtypeStruct((b_, L, n), BF16),
        compiler_params=_cparams("arbitrary", "arbitrary", "arbitrary"),
        name="proj",
    )(x, scale, shift, w)


def _gla_kernel(*refs, reverse, finalize, nchunks):
    if finalize:
        (q_ref, k_ref, v_ref, ga_ref, gal_ref, wa_ref, ba_ref, s0_ref, tri_ref, op_ref, z_ref, nw_ref,
         o_ref, sfin_ref, s_scr) = refs
    else:
        (q_ref, k_ref, v_ref, ga_ref, gal_ref, wa_ref, ba_ref, s0_ref, tri_ref,
         o_ref, sfin_ref, s_scr) = refs
    C = GLA_CHUNK
    T = nchunks * C
    nt_dims = (((1,), (1,)), ((), ()))
    tn_dims = (((0,), (0,)), ((), ()))

    @pl.when(pl.program_id(1) == 0)
    def _():
        s_scr[...] = s0_ref[0]

    ga = ga_ref[0].astype(F32) + gal_ref[0].astype(F32)
    xg = _dot_hp(ga, wa_ref[...]) + ba_ref[...]
    g = (jnp.minimum(xg, 0.0) - jnp.log(1.0 + jnp.exp(-jnp.abs(xg)))) * (1.0 / GLA_GATE_NORM)

    tri = tri_ref[...]
    g1 = g.astype(BF16)
    r1 = g - g1.astype(F32)
    g2 = r1.astype(BF16)
    g3 = (r1 - g2.astype(F32)).astype(BF16)
    b = _dot(tri, g1) + _dot(tri, g2) + _dot(tri, g3)

    def chunk_row(r):
        return jnp.concatenate([jnp.broadcast_to(b[c * C + r:c * C + r + 1], (C, GLA_KW))
                                for c in range(nchunks)], axis=0)

    tot_row = 0 if reverse else C - 1
    bm = chunk_row(C // 2)
    bt = chunk_row(tot_row)
    ri = lax.broadcasted_iota(jnp.int32, (T, T), 0)
    ci = lax.broadcasted_iota(jnp.int32, (T, T), 1)
    same_chunk = (ri // C) == (ci // C)
    mask = same_chunk & ((ci >= ri) if reverse else (ci <= ri))
    lane = lax.broadcasted_iota(jnp.int32, (1, GLA_KW), 1)
    hmask = [(lane >= h * GLA_DK) & (lane < (h + 1) * GLA_DK) for h in range(GLA_HEADS)]

    q = q_ref[0].astype(F32) * (GLA_DK ** -0.5)
    k = k_ref[0].astype(F32)
    v = v_ref[0].astype(BF16)
    qa = q * jnp.exp(b - bm)
    kb = (k * jnp.exp(bm - b)).astype(BF16)
    qe = q * jnp.exp(b)
    kd = (k * jnp.exp(bt - b)).astype(BF16)

    intra = []
    for h in range(GLA_HEADS):
        qa_h = jnp.where(hmask[h], qa, 0.0).astype(BF16)
        att = lax.dot_general(qa_h, kb, nt_dims, preferred_element_type=F32)
        att = jnp.where(mask, att, 0.0).astype(BF16)
        intra.append(_dot(att, v[:, h * GLA_DV:(h + 1) * GLA_DV]))

    inter = [None] * nchunks
    order = range(nchunks - 1, -1, -1) if reverse else range(nchunks)
    for c in order:
        rows = slice(c * C, (c + 1) * C)
        st = s_scr[...]
        st_b = st.astype(BF16)
        dec = jnp.exp(b[c * C + tot_row:c * C + tot_row + 1])
        upd = jnp.zeros_like(st)
        outs = []
        for h in range(GLA_HEADS):
            qe_h = jnp.where(hmask[h], qe[rows], 0.0).astype(BF16)
            outs.append(lax.dot_general(qe_h, st_b, nt_dims, preferred_element_type=F32))
            u_h = lax.dot_general(v[rows, h * GLA_DV:(h + 1) * GLA_DV], kd[rows], tn_dims,
                                  preferred_element_type=F32)
            upd = upd + jnp.where(hmask[h], u_h, 0.0)
        s_scr[...] = dec * st + upd
        inter[c] = jnp.concatenate(outs, axis=1)
    o = jnp.concatenate(intra, axis=1) + jnp.concatenate(inter, axis=0)

    if finalize:
        o = o + op_ref[0]
        parts = []
        for h in range(GLA_HEADS):
            oh = o[:, h * GLA_DV:(h + 1) * GLA_DV]
            ms = jnp.mean(oh * oh, axis=-1, keepdims=True)
            parts.append(oh * lax.rsqrt(ms + EPS) * nw_ref[...])
        y = jnp.concatenate(parts, axis=1) * _silu(z_ref[0].astype(F32))
        o_ref[0] = y.astype(o_ref.dtype)
    else:
        o_ref[0] = o

    @pl.when(pl.program_id(1) == pl.num_programs(1) - 1)
    def _():
        sfin_ref[0] = s_scr[...]


def _gla_call(P, wa_pad, ba, s0, reverse, fin=None):
    b_, L, _ = P.shape
    T = min(GLA_BLOCK, L)
    nt = L // T
    if reverse:
        tmap = lambda t: nt - 1 - t
    else:
        tmap = lambda t: t
    ii = np.arange(T)
    same = (ii[:, None] // GLA_CHUNK) == (ii[None, :] // GLA_CHUNK)
    tri = same & ((ii[None, :] >= ii[:, None]) if reverse else (ii[None, :] <= ii[:, None]))
    tri = jnp.asarray(tri.astype(np.float32)).astype(BF16)

    def col(name, width):
        blk = COLS[name][0] // width
        return pl.BlockSpec((1, T, width), lambda b, t: (b, tmap(t), blk))

    in_specs = [col('g_q', GLA_KW), col('g_k', GLA_KW), col('g_v', GLA_VW),
                col('g_a', LANES), col('g_a_lo', LANES),
                pl.BlockSpec((LANES, GLA_KW), lambda b, t: (0, 0)),
                pl.BlockSpec((1, GLA_KW), lambda b, t: (0, 0)),
                pl.BlockSpec((1, GLA_DV, GLA_KW), lambda b, t: (b, 0, 0)),
                pl.BlockSpec((T, T), lambda b, t: (0, 0))]
    args = [P, P, P, P, P, wa_pad, ba, s0, tri]
    if fin is not None:
        o_prev, nw = fin
        in_specs += [pl.BlockSpec((1, T, GLA_VW), lambda b, t: (b, tmap(t), 0)),
                     col('g_z', GLA_VW),
                     pl.BlockSpec((1, GLA_DV), lambda b, t: (0, 0))]
        args += [o_prev, P, nw]
    out_dtype = BF16 if fin is not None else F32
    return pl.pallas_call(
        functools.partial(_gla_kernel, reverse=reverse, finalize=fin is not None, nchunks=T // GLA_CHUNK),
        grid=(b_, nt),
        in_specs=in_specs,
        out_specs=[pl.BlockSpec((1, T, GLA_VW), lambda b, t: (b, tmap(t), 0)),
                   pl.BlockSpec((1, GLA_DV, GLA_KW), lambda b, t: (b, 0, 0))],
        out_shape=[jax.ShapeDtypeStruct((b_, L, GLA_VW), out_dtype),
                   jax.ShapeDtypeStruct((b_, GLA_DV, GLA_KW), F32)],
        scratch_shapes=[pltpu.VMEM((GLA_DV, GLA_KW), F32)],
        compiler_params=_cparams("arbitrary", "arbitrary"),
        name="gla",
    )(*args)


def _norm_rope(a, gain, bd, cos, sin):
    sq = a * a
    sh, sl = _split(sq)
    ss = _dot(sh, bd) + _dot(sl, bd)
    an = a * lax.rsqrt(ss * (1.0 / ATTN_HD) + EPS) * gain
    w = a.shape[-1]
    lane = lax.broadcasted_iota(jnp.int32, an.shape, 1)
    partner = jnp.where(lane % 2 == 0, pltpu.roll(an, w - 1, 1), pltpu.roll(an, 1, 1))
    return an * cos + partner * sin


Q_SCALE = (ATTN_HD ** -0.5) * math.log2(math.e)
ATTN_QK_DEPTH = 4 * ATTN_HD
ATTN_TK_MAX = 640


def _split8(a):
    hi = a.astype(F8).astype(F32)
    lo = (a - hi).astype(F8)
    return hi.astype(BF16), lo.astype(BF16)


def _qkv_prep_kernel(q_ref, k_ref, v_ref, cq_ref, sq_ref, ck_ref, sk_ref, gq_ref, gk_ref,
                     bdq_ref, bdk_ref, qo_ref, ko_ref, vo_ref):
    hd = ATTN_HD
    q = _norm_rope(q_ref[0].astype(F32), gq_ref[...], bdq_ref[...], cq_ref[...], sq_ref[...]) * Q_SCALE
    qh, ql = _split8(q)
    qt = jnp.concatenate([qh, ql], axis=1).T
    for h in range(ATTN_HEADS):
        hi = qt[hd * h:hd * (h + 1)]
        lo = qt[ATTN_QW + hd * h:ATTN_QW + hd * (h + 1)]
        qo_ref[0, ATTN_QK_DEPTH * h:ATTN_QK_DEPTH * (h + 1), :] = jnp.concatenate([hi, hi, lo, lo], axis=0).astype(F8)
    k = _norm_rope(k_ref[0].astype(F32), gk_ref[...], bdk_ref[...], ck_ref[...], sk_ref[...])
    kh, kl = _split8(k)
    for g in range(ATTN_KV_HEADS):
        cols = slice(g * hd, (g + 1) * hd)
        ko_ref[0, g] = jnp.concatenate([kh[:, cols], kl[:, cols], kh[:, cols], kl[:, cols]], axis=1).astype(F8)
    vo_ref[0] = v_ref[0].astype(BF16).T


def _qkv_prep_call(P, cos_q, sin_q, cos_k, sin_k, gq, gk, bdq, bdk):
    b_, L, _ = P.shape
    T = min(1024, L)

    def col(name, width):
        blk = COLS[name][0] // width
        return pl.BlockSpec((1, T, width), lambda b, t: (b, t, blk))

    tab = lambda w: pl.BlockSpec((T, w), lambda b, t: (t, 0))
    const = lambda r, w: pl.BlockSpec((r, w), lambda b, t: (0, 0))
    return pl.pallas_call(
        _qkv_prep_kernel,
        grid=(b_, L // T),
        in_specs=[col('a_q', ATTN_QW), col('a_k', ATTN_KVW), col('a_v', ATTN_KVW),
                  tab(ATTN_QW), tab(ATTN_QW), tab(ATTN_KVW), tab(ATTN_KVW),
                  const(1, ATTN_QW), const(1, ATTN_KVW),
                  const(ATTN_QW, ATTN_QW), const(ATTN_KVW, ATTN_KVW)],
        out_specs=[pl.BlockSpec((1, ATTN_HEADS * ATTN_QK_DEPTH, T), lambda b, t: (b, 0, t)),
                   pl.BlockSpec((1, ATTN_KV_HEADS, T, ATTN_QK_DEPTH), lambda b, t: (b, 0, t, 0)),
                   pl.BlockSpec((1, ATTN_KVW, T), lambda b, t: (b, 0, t))],
        out_shape=[jax.ShapeDtypeStruct((b_, ATTN_HEADS * ATTN_QK_DEPTH, L), F8),
                   jax.ShapeDtypeStruct((b_, ATTN_KV_HEADS, L, ATTN_QK_DEPTH), F8),
                   jax.ShapeDtypeStruct((b_, ATTN_KVW, L), BF16)],
        compiler_params=_cparams("arbitrary", "arbitrary"),
        name="qkv_prep",
    )(P, P, P, cos_q, sin_q, cos_k, sin_k, gq, gk, bdq, bdk)


ATTN_SUM_ROWS = 16
ATTN_UNROLL = 2
ATTN_TQ = 512
ATTN_COLS = 256


def _attn_kernel(qt_ref, k_ref, vt_ref, z_ref, o_ref, m_scr, t_scr, acc_scr, s_scr, *, tk, nk):
    m_scr[...] = jnp.full(m_scr.shape, -jnp.inf, F32)
    acc_scr[...] = jnp.zeros(acc_scr.shape, F32)
    ones = jnp.ones((ATTN_SUM_ROWS, tk), BF16)

    tq = qt_ref.shape[2]
    width = ATTN_COLS if tq % ATTN_COLS == 0 else tq
    parts = [(h, slice(c, c + width)) for h in range(ATTN_GROUP) for c in range(0, tq, width)]

    def score_part(kj, slot, h, c):
        s = _dot(kj, qt_ref[0, h * ATTN_QK_DEPTH:(h + 1) * ATTN_QK_DEPTH, c])
        s_scr[slot, h, :, c] = s
        t_scr[slot, h, :, c] = jnp.max(s, axis=0, keepdims=True)

    def keys(j):
        start = j * tk if isinstance(j, int) else pl.multiple_of(j * tk, tk)
        return k_ref[0, 0, pl.ds(start, tk), :]

    k0 = keys(0)
    for h, c in parts:
        score_part(k0, 0, h, c)

    def step(j, slot):
        kj = keys(min(j + 1, nk - 1) if isinstance(j, int) else jnp.minimum(j + 1, nk - 1))
        start = j * tk if isinstance(j, int) else pl.multiple_of(j * tk, tk)
        vte = jnp.concatenate([vt_ref[0, :, pl.ds(start, tk)], ones], axis=0)
        for h, c in parts:
            score_part(kj, 1 - slot, h, c)
            m_old = m_scr[h, :, c]
            m_new = jnp.maximum(m_old, t_scr[slot, h, :, c])
            alpha = jnp.exp2(m_old - m_new)
            m_scr[h, :, c] = m_new
            p = jnp.exp2(s_scr[slot, h, :, c] - m_new).astype(BF16)
            acc_scr[h, :, c] = alpha * acc_scr[h, :, c] + _dot(vte, p)

    def trip(i, carry):
        for u in range(ATTN_UNROLL):
            step(ATTN_UNROLL * i + u, u % 2)
        return carry

    ntrips = nk // ATTN_UNROLL
    lax.fori_loop(0, ntrips, trip, 0)
    for j in range(ntrips * ATTN_UNROLL, nk):
        step(j, j % 2)
    o_t = jnp.concatenate([acc_scr[h, :ATTN_HD] / acc_scr[h, ATTN_HD:ATTN_HD + 1] for h in range(ATTN_GROUP)],
                          axis=0)
    out = o_t.T
    o_ref[0] = (out * _silu(z_ref[0].astype(F32))).astype(o_ref.dtype)


def _attn_call(qt, k, vt, P):
    b_, _, L = qt.shape
    Lk = k.shape[2]
    tq = min(ATTN_TQ, L)
    tk = max(t for t in range(LANES, ATTN_TK_MAX + 1, LANES) if Lk % t == 0)
    gw = ATTN_GROUP * ATTN_HD
    zblk = COLS['a_z'][0] // gw
    return pl.pallas_call(
        functools.partial(_attn_kernel, tk=tk, nk=Lk // tk),
        grid=(b_, ATTN_KV_HEADS, L // tq),
        in_specs=[pl.BlockSpec((1, ATTN_GROUP * ATTN_QK_DEPTH, tq), lambda b, g, i: (b, g, i)),
                  pl.BlockSpec((1, 1, Lk, ATTN_QK_DEPTH), lambda b, g, i: (b, g, 0, 0)),
                  pl.BlockSpec((1, ATTN_HD, Lk), lambda b, g, i: (b, g, 0)),
                  pl.BlockSpec((1, tq, gw), lambda b, g, i: (b, i, zblk + g))],
        out_specs=pl.BlockSpec((1, tq, gw), lambda b, g, i: (b, i, g)),
        out_shape=jax.ShapeDtypeStruct((b_, L, ATTN_QW), BF16),
        scratch_shapes=[pltpu.VMEM((ATTN_GROUP, 1, tq), F32),
                        pltpu.VMEM((2, ATTN_GROUP, 1, tq), F32),
                        pltpu.VMEM((ATTN_GROUP, ATTN_HD + ATTN_SUM_ROWS, tq), F32),
                        pltpu.VMEM((2, ATTN_GROUP, tk, tq), F32)],
        compiler_params=_cparams("arbitrary", "arbitrary", "arbitrary"),
        name="attn",
    )(qt, k, vt, P)


def _short_conv_kernel(u_ref, up_ref, un_ref, w_ref, b_ref, v_ref, x1_ref, x2_ref):
    t = pl.program_id(1)
    nt = pl.num_programs(1)
    u = u_ref[0].astype(F32)
    T = u.shape[0]
    hr = up_ref.shape[1]
    prev_row = jnp.where(t > 0, up_ref[0, hr - 1:hr, :].astype(F32), 0.0)
    next_row = jnp.where(t < nt - 1, un_ref[0, 0:1, :].astype(F32), 0.0)
    row = lax.broadcasted_iota(jnp.int32, u.shape, 0)
    u_m1 = jnp.where(row == 0, prev_row, pltpu.roll(u, 1, 0))
    u_p1 = jnp.where(row == T - 1, next_row, pltpu.roll(u, T - 1, 0))
    w = w_ref[...]
    out = b_ref[...] + u_m1 * w[0:1] + u * w[1:2] + u_p1 * w[2:3]
    v_ref[0] = out[:, 0:HY_W]
    x1_ref[0] = out[:, HY_W:2 * HY_W]
    x2_ref[0] = out[:, 2 * HY_W:3 * HY_W]


def _short_conv_call(P, w, bias):
    b_, L, _ = P.shape
    T = min(1024, L)
    cw = (HY_ORDER + 1) * HY_W
    blk = COLS['y_u'][0] // cw
    hr = 16
    hb = T // hr
    nh = L // hr
    out_spec = pl.BlockSpec((1, T, HY_W), lambda b, t: (b, t, 0))
    shp = jax.ShapeDtypeStruct((b_, L, HY_W), F32)
    return pl.pallas_call(
        _short_conv_kernel,
        grid=(b_, L // T),
        in_specs=[pl.BlockSpec((1, T, cw), lambda b, t: (b, t, blk)),
                  pl.BlockSpec((1, hr, cw), lambda b, t: (b, jnp.maximum(t * hb - 1, 0), blk)),
                  pl.BlockSpec((1, hr, cw), lambda b, t: (b, jnp.minimum((t + 1) * hb, nh - 1), blk)),
                  pl.BlockSpec((HY_SHORT, cw), lambda b, t: (0, 0)),
                  pl.BlockSpec((1, cw), lambda b, t: (0, 0))],
        out_specs=[out_spec, out_spec, out_spec],
        out_shape=[shp, shp, shp],
        compiler_params=_cparams("arbitrary", "arbitrary"),
        name="short_conv",
    )(P, P, P, w, bias)


FFT_ROWS = 8


def _filter_kernel(emb_ref, w1_ref, b1_ref, f1_ref, w2_ref, b2_ref, f2_ref, w3_ref, dl_ref,
                   hf_ref, hb_ref, sum_ref):
    t = pl.program_id(0)
    emb = emb_ref[...]
    h = jnp.sin(f1_ref[...] * (_dot_hp(emb, w1_ref[...]) + b1_ref[...]))
    h = jnp.sin(f2_ref[...] * (_dot_hp(h, w2_ref[...]) + b2_ref[...]))
    hh, hl = _split(h)
    tt = emb[:, 0:1]

    @pl.when(t == 0)
    def _():
        sum_ref[...] = jnp.zeros_like(sum_ref)

    lag0 = (lax.broadcasted_iota(jnp.int32, (h.shape[0], HY_W), 0) + t * h.shape[0]) == 0
    for o in range(HY_ORDER):
        for d, out_ref in enumerate((hf_ref, hb_ref)):
            cols = slice((2 * o + d) * HY_W, (2 * o + d + 1) * HY_W)
            wh, wl = _split(w3_ref[:, cols])
            f = _dot(hh, wh) + _dot(hl, wh) + _dot(hh, wl)
            f = f * (jnp.exp(-tt * dl_ref[:, cols]) + HY_MOD_SHIFT)
            sum_ref[:, cols] += jnp.broadcast_to(jnp.sum(jnp.abs(f), axis=0, keepdims=True), (8, HY_W))
            if d == 1:
                f = jnp.where(lag0, 0.0, f)
            out_ref[o] = pltpu.einshape("(an)c->a(nc)", f, a=FFT_ROWS)


def _filter_call(emb, w1, b1, f1, w2, b2, f2, w3, deltas, n1, n2):
    L = emb.shape[0]
    TL = FFT_ROWS * n2
    const = lambda r, w: pl.BlockSpec((r, w), lambda t: (0, 0))
    hspec = pl.BlockSpec((HY_ORDER, FFT_ROWS, n2 * HY_W), lambda t: (0, t, 0))
    hshape = jax.ShapeDtypeStruct((HY_ORDER, n1 // 2, n2 * HY_W), F32)
    return pl.pallas_call(
        _filter_kernel,
        grid=(L // TL,),
        in_specs=[pl.BlockSpec((TL, LANES), lambda t: (t, 0)),
                  const(LANES, HY_FFN), const(1, HY_FFN), const(1, HY_FFN),
                  const(HY_FFN, HY_FFN), const(1, HY_FFN), const(1, HY_FFN),
                  const(HY_FFN, N_FILT), const(1, N_FILT)],
        out_specs=[hspec, hspec, pl.BlockSpec((8, N_FILT), lambda t: (0, 0))],
        out_shape=[hshape, hshape, jax.ShapeDtypeStruct((8, N_FILT), F32)],
        compiler_params=_cparams("arbitrary"),
        name="hy_filter",
    )(emb, w1, b1, f1, w2, b2, f2, w3, deltas)


def _fft_sizes(n):
    lg = int(round(math.log2(n)))
    assert 1 << lg == n
    n1 = 1 << ((lg + 1) // 2)
    return n1, n // n1


def _bf16_pair(a):
    a = np.asarray(a, np.float32)
    hi = jnp.asarray(a, F32).astype(BF16)
    lo = (jnp.asarray(a, F32) - hi.astype(F32)).astype(BF16)
    return hi, lo


def _fft_tables(n1, n2):
    n = n1 * n2
    h1 = n1 // 2
    k1 = np.arange(n1)[:, None]
    a = 2.0 * np.pi * ((k1 * np.arange(h1)[None, :]) % n1) / n1
    c, s = np.cos(a), np.sin(a)
    fa = np.block([[c, s], [-s, c]])
    fa_real = np.concatenate([c, -s], axis=0)
    fd = np.block([[c.T, -s.T], [s.T, c.T]])
    k2 = np.arange(n2)[:, None]
    b = 2.0 * np.pi * ((k2 * np.arange(n2)[None, :]) % n2) / n2
    cb, sb = np.cos(b), np.sin(b)
    fb = np.block([[cb, sb], [-sb, cb]])
    fc = np.block([[cb, -sb], [sb, cb]])
    kn = (jnp.arange(n1, dtype=jnp.int32)[:, None] * jnp.arange(n2, dtype=jnp.int32)[None, :]) % n
    tw = kn.astype(F32) * (2.0 * math.pi / n)
    twr = jnp.broadcast_to(jnp.cos(tw)[:, :, None], (n1, n2, LANES))
    twi = jnp.broadcast_to(-jnp.sin(tw)[:, :, None], (n1, n2, LANES))
    return dict(fa=_bf16_pair(fa), fa_real=_bf16_pair(fa_real), fd=_bf16_pair(fd),
                fb=_bf16_pair(fb), fc=_bf16_pair(fc), twr=twr, twi=twi)


def _fa_kernel(x_ref, mh_ref, ml_ref, o_ref, xh_scr, *, packed, part_axis):
    @pl.when(pl.program_id(part_axis) == 0)
    def _():
        if packed:
            x = x_ref[...]
            x = x.reshape(x.shape[0] * x.shape[1], x.shape[2], x.shape[3])
            x = pltpu.einshape("rnc->r(nc)", x)
        else:
            x = x_ref[0]
        xh_scr[...] = x.astype(BF16)

    xh = xh_scr[...]
    r = _dot(mh_ref[...], xh) + _dot(ml_ref[...], xh)
    o_ref[0] = pltpu.einshape("r(nc)->rnc", r, n=FFT_ROWS)


def _fa_call(x4, mats, n1):
    mh, ml = mats
    _, h1, n2, wd = x4.shape
    wt = FFT_ROWS * wd
    return pl.pallas_call(
        functools.partial(_fa_kernel, packed=True, part_axis=1),
        grid=(n2 // FFT_ROWS, 2),
        in_specs=[pl.BlockSpec((2, h1, FFT_ROWS, wd), lambda j, p: (0, 0, j, 0)),
                  pl.BlockSpec((n1, n1), lambda j, p: (p, 0)),
                  pl.BlockSpec((n1, n1), lambda j, p: (p, 0))],
        out_specs=pl.BlockSpec((1, n1, FFT_ROWS, wd), lambda j, p: (p, 0, j, 0)),
        out_shape=jax.ShapeDtypeStruct((2, n1, n2, wd), F32),
        scratch_shapes=[pltpu.VMEM((n1, wt), BF16)],
        compiler_params=_cparams("arbitrary", "arbitrary"),
        name="fft_a",
    )(x4, mh, ml)


def _fa_real_call(x, mats, n1):
    mh, ml = mats
    ng, h1, lanes = x.shape
    wt = FFT_ROWS * HY_W
    n2 = lanes // HY_W
    out = pl.pallas_call(
        functools.partial(_fa_kernel, packed=False, part_axis=2),
        grid=(ng, lanes // wt, 2),
        in_specs=[pl.BlockSpec((1, h1, wt), lambda g, j, p: (g, 0, j)),
                  pl.BlockSpec((n1, h1), lambda g, j, p: (p, 0)),
                  pl.BlockSpec((n1, h1), lambda g, j, p: (p, 0))],
        out_specs=pl.BlockSpec((1, n1, FFT_ROWS, HY_W), lambda g, j, p: (2 * g + p, 0, j, 0)),
        out_shape=jax.ShapeDtypeStruct((2 * ng, n1, n2, HY_W), F32),
        scratch_shapes=[pltpu.VMEM((h1, wt), BF16)],
        compiler_params=_cparams("arbitrary", "arbitrary", "arbitrary"),
        name="fft_a_real",
    )(x, mh, ml)
    return out.reshape(ng, 2, n1, n2, HY_W)


def _lane_tile(a, width):
    return jnp.concatenate([a] * (width // a.shape[-1]), axis=-1)


def _fb_filter_kernel(tf_ref, tb_ref, twr_ref, twi_ref, fh_ref, fl_ref, sf_ref, sb_ref, g_ref, *, n2, scale):
    w = g_ref.shape[-1]
    inv = jnp.concatenate([scale / sf_ref[0:1, :], scale / sb_ref[0:1, :]], axis=1)

    def body(pair, carry):
        ks = (2 * pair, 2 * pair + 1)
        ps = []
        for k in ks:
            twr = _lane_tile(twr_ref[k], 2 * w)
            twi = _lane_tile(twi_ref[k], 2 * w)
            tr = jnp.concatenate([tf_ref[0, 0, k], tb_ref[0, 0, k]], axis=1)
            ti = jnp.concatenate([tf_ref[0, 1, k], tb_ref[0, 1, k]], axis=1)
            ps.append(jnp.concatenate([tr * twr - ti * twi, tr * twi + ti * twr], axis=0))
        zs = [_dot_mat(fh_ref[...], fl_ref[...], p) * inv for p in ps]
        for k, z in zip(ks, zs):
            g_ref[0, 0, k] = z[:n2, :w] + z[:n2, w:]
            g_ref[0, 1, k] = z[n2:, :w] - z[n2:, w:]
        return carry

    lax.fori_loop(0, FFT_ROWS // 2, body, 0, unroll=2)


def _fb_filter_call(thf, thb, sums, tabs, n1, n2):
    fh, fl = tabs['fb']
    tspec = pl.BlockSpec((1, 2, FFT_ROWS, n2, HY_W), lambda g, i: (g, 0, i, 0, 0))
    twspec = pl.BlockSpec((FFT_ROWS, n2, LANES), lambda g, i: (i, 0, 0))
    sspec = lambda d: pl.BlockSpec((8, HY_W), lambda g, i: (0, 2 * g + d))
    return pl.pallas_call(
        functools.partial(_fb_filter_kernel, n2=n2, scale=1.0 / (n1 * n2)),
        grid=(HY_ORDER, n1 // FFT_ROWS),
        in_specs=[tspec, tspec, twspec, twspec,
                  pl.BlockSpec(fh.shape, lambda g, i: (0, 0)), pl.BlockSpec(fl.shape, lambda g, i: (0, 0)),
                  sspec(0), sspec(1)],
        out_specs=pl.BlockSpec((1, 2, FFT_ROWS, n2, HY_W), lambda g, i: (g, 0, i, 0, 0)),
        out_shape=jax.ShapeDtypeStruct((HY_ORDER, 2, n1, n2, HY_W), F32),
        compiler_params=_cparams("arbitrary", "arbitrary"),
        name="fft_b_filter",
    )(thf, thb, tabs['twr'], tabs['twi'], fh, fl, sums, sums)


def _fb_kernel(t_ref, twr_ref, twi_ref, fh_ref, fl_ref, ch_ref, cl_ref, g_ref, o_ref, t4_scr, *, n2):
    w = g_ref.shape[-1]

    def body(pair, carry):
        ks = (2 * pair, 2 * pair + 1)
        tw = [(_lane_tile(twr_ref[k], w), _lane_tile(twi_ref[k], w)) for k in ks]
        ps = [jnp.concatenate([t_ref[0, k] * a - t_ref[1, k] * b, t_ref[0, k] * b + t_ref[1, k] * a], axis=0)
              for k, (a, b) in zip(ks, tw)]
        zs = [_dot_mat(fh_ref[...], fl_ref[...], p) for p in ps]
        ys = [jnp.concatenate([z[:n2] * g_ref[0, 0, k] - z[n2:] * g_ref[0, 1, k],
                               z[:n2] * g_ref[0, 1, k] + z[n2:] * g_ref[0, 0, k]], axis=0)
              for k, z in zip(ks, zs)]
        vs = [_dot_mat(ch_ref[...], cl_ref[...], y) for y in ys]
        for k, (a, b), v in zip(ks, tw, vs):
            t4_scr[0, k] = v[:n2] * a + v[n2:] * b
            t4_scr[1, k] = v[n2:] * a - v[:n2] * b
        return carry

    lax.fori_loop(0, FFT_ROWS // 2, body, 0, unroll=2)
    o_ref[...] = pltpu.einshape("pknc->pk(nc)", t4_scr[...])


def _fb_call(t, g, order, tabs, n1, n2):
    fh, fl = tabs['fb']
    ch, cl = tabs['fc']
    lanes = n2 * HY_W
    mat = lambda m: pl.BlockSpec(m.shape, lambda i: (0, 0))
    dat = pl.BlockSpec((2, FFT_ROWS, lanes), lambda i: (0, i, 0))
    twspec = pl.BlockSpec((FFT_ROWS, n2, LANES), lambda i: (i, 0, 0))
    return pl.pallas_call(
        functools.partial(_fb_kernel, n2=n2),
        grid=(n1 // FFT_ROWS,),
        in_specs=[pl.BlockSpec((2, FFT_ROWS, n2, HY_W), lambda i: (0, i, 0, 0)),
                  twspec, twspec, mat(fh), mat(fl), mat(ch), mat(cl),
                  pl.BlockSpec((1, 2, FFT_ROWS, n2, HY_W), lambda i: (order, 0, i, 0, 0))],
        out_specs=dat,
        out_shape=jax.ShapeDtypeStruct((2, n1, lanes), F32),
        scratch_shapes=[pltpu.VMEM((2, FFT_ROWS, n2, HY_W), F32)],
        compiler_params=_cparams("arbitrary"),
        name="fft_b",
    )(t, tabs['twr'], tabs['twi'], fh, fl, ch, cl, g)


def _fd_kernel(u_ref, mh_ref, ml_ref, z_ref, x_ref, sk_ref, o_ref):
    u = u_ref[...]
    u = u.reshape(u.shape[0] * u.shape[1], u.shape[2])
    y = _dot_mat(mh_ref[...], ml_ref[...], u)
    y = pltpu.einshape("r(nc)->rnc", y, n=FFT_ROWS)
    o_ref[0] = x_ref[0] * (y + sk_ref[...] * z_ref[0])


def _fd_call(u, mats, z4, x4, skip_row, n1):
    mh, ml = mats
    b_, h1, n2, wd = z4.shape
    wt = FFT_ROWS * wd
    dat = pl.BlockSpec((1, h1, FFT_ROWS, wd), lambda j, p: (p, 0, j, 0))
    return pl.pallas_call(
        _fd_kernel,
        grid=(n2 // FFT_ROWS, b_),
        in_specs=[pl.BlockSpec((2, n1, wt), lambda j, p: (0, 0, j)),
                  pl.BlockSpec((h1, 2 * n1), lambda j, p: (p, 0)),
                  pl.BlockSpec((h1, 2 * n1), lambda j, p: (p, 0)),
                  dat, dat, pl.BlockSpec((1, wd), lambda j, p: (0, 0))],
        out_specs=dat,
        out_shape=jax.ShapeDtypeStruct(z4.shape, F32),
        compiler_params=pltpu.CompilerParams(dimension_semantics=("arbitrary", "arbitrary"),
                                             vmem_limit_bytes=FFT_D_VMEM_LIMIT),
        name="fft_d",
    )(u, mh, ml, z4, x4, skip_row)


def _hyena_filter_spectrum(L, tabs, n1, n2, fp):
    f32 = F32
    t = jnp.linspace(0.0, 1.0, L, dtype=f32)[:, None]
    w = 2.0 * math.pi * jnp.arange(L, dtype=f32)[:, None] / L
    fr = jnp.linspace(1e-4, HY_BANDS - 1, HY_BANDS, dtype=f32)[None]
    emb = jnp.concatenate([t, jnp.cos(fr * w), -jnp.sin(fr * w)], axis=-1)
    emb = jnp.pad(emb, ((0, 0), (0, LANES - HY_EMB)))
    deltas = jnp.abs(jnp.linspace(math.log(HY_DECAY_TARGET) / HY_DECAY_SHORT_PCT,
                                  math.log(HY_DECAY_TARGET) / HY_DECAY_LONG_PCT, N_FILT, dtype=f32))[None]
    f1_w, f1_b, f1_freq, f2_w, f2_b, f2_freq, f3_w = fp
    w1 = jnp.pad(f1_w, ((0, LANES - HY_EMB), (0, 0)))
    hf, hb, sums = _filter_call(emb, w1, f1_b[None], f1_freq[None], f2_w, f2_b[None],
                                f2_freq[None], f3_w, deltas, n1, n2)
    thf = _fa_real_call(hf, tabs['fa_real'], n1)
    thb = _fa_real_call(hb, tabs['fa_real'], n1)
    return _fb_filter_call(thf, thb, sums, tabs, n1, n2)


def _hyena_conv(z, xg, g, order, skip, tabs, n1, n2):
    b_, L, wd = z.shape
    assert b_ == 2, "the two batch rows ride as the real and imaginary parts of one transform"
    z4 = z.reshape(b_, n1 // 2, n2, wd)
    x4 = xg.reshape(b_, n1 // 2, n2, wd)
    t = _fa_call(z4, tabs['fa'], n1)
    u = _fb_call(t, g, order, tabs, n1, n2)
    out = _fd_call(u, tabs['fd'], z4, x4, skip[None, :], n1)
    return out.reshape(b_, L, wd)


def _merge_kernel(yg_ref, ya_ref, yh_ref, yz_ref, mg_ref, ma_ref, mh_ref, x_ref, gate_ref,
                  wg_ref, wa_ref, wh_ref, wo_ref, fn_ref, o_ref, *, final):
    yh = (yh_ref[0] * _silu(yz_ref[0].astype(F32))).astype(BF16)
    y = (_sigmoid(mg_ref[0].astype(F32)) * _dot(yg_ref[0], wg_ref[...])
         + _sigmoid(ma_ref[0].astype(F32)) * _dot(ya_ref[0], wa_ref[...])
         + _sigmoid(mh_ref[0].astype(F32)) * _dot(yh, wh_ref[...]))
    out = _dot(y.astype(BF16), wo_ref[...])
    xn = x_ref[0] + gate_ref[0] * out
    if final:
        ms = jnp.mean(xn * xn, axis=-1, keepdims=True)
        xn = xn * lax.rsqrt(ms + EPS) * fn_ref[...]
    o_ref[0] = xn


def _merge_call(y_gla, y_attn, y_hy, P, x, gate, wg, wa, wh, wo, fnorm, final):
    b_, L, d = x.shape
    T = min(1024, L)
    row = lambda w, blk=0: pl.BlockSpec((1, T, w), lambda b, t: (b, t, blk))
    const = lambda r, w: pl.BlockSpec((r, w), lambda b, t: (0, 0))
    mblk = COLS['m'][0] // d
    return pl.pallas_call(
        functools.partial(_merge_kernel, final=final),
        grid=(b_, L // T),
        in_specs=[row(BRANCH_W), row(BRANCH_W), row(BRANCH_W), row(HY_W, COLS['y_z'][0] // HY_W),
                  row(d, mblk), row(d, mblk + 1), row(d, mblk + 2), row(d),
                  pl.BlockSpec((1, 1, d), lambda b, t: (b, 0, 0)),
                  const(BRANCH_W, d), const(BRANCH_W, d), const(BRANCH_W, d), const(d, d), const(1, d)],
        out_specs=row(d),
        out_shape=jax.ShapeDtypeStruct((b_, L, d), F32),
        compiler_params=_cparams("arbitrary", "arbitrary"),
        name="merge",
    )(y_gla, y_attn, y_hy, P, P, P, P, x, gate, wg, wa, wh, wo, fnorm)


def _pack_w_in(w):
    parts = dict(zip(('g_q', 'g_k', 'g_v', 'g_z', 'g_af', 'g_ab', 'a_q', 'a_k', 'a_v', 'a_z', 'y_u', 'y_z', 'm'),
                     jnp.split(w, [int(i) for i in np.cumsum(SPLITS)[:-1]], axis=-1)))
    parts['g_a'] = jnp.pad(jnp.concatenate([parts['g_af'], parts['g_ab']], axis=-1),
                           ((0, 0), (0, LANES - 2 * GLA_RANK)))
    parts['g_a_lo'] = jnp.zeros_like(parts['g_a'])
    order = sorted(COLS, key=lambda n: COLS[n][0])
    return jnp.concatenate([parts[n] for n in order], axis=-1).astype(BF16)


def _rope_tables(L, heads):
    t = jnp.arange(L)
    row = (t // GRID_W).astype(F32)
    colp = (t % GRID_W).astype(F32)
    half = ATTN_HD // 2
    inv = ROPE_THETA ** (-jnp.arange(0, half, 2, dtype=F32) / half)
    ang = jnp.concatenate([row[:, None] * inv, colp[:, None] * inv], axis=-1)
    cos = jnp.repeat(jnp.cos(ang), 2, axis=-1)
    sin = jnp.stack([-jnp.sin(ang), jnp.sin(ang)], axis=-1).reshape(L, ATTN_HD)
    return jnp.tile(cos, (1, heads)), jnp.tile(sin, (1, heads))


def _identity_rope(L, heads):
    return jnp.ones((L, heads * ATTN_HD), F32), jnp.zeros((L, heads * ATTN_HD), F32)


def _block_diag_ones(width):
    i = np.arange(width) // ATTN_HD
    return jnp.asarray((i[:, None] == i[None, :]).astype(np.float32)).astype(BF16)


def kernel(x, c, ctx, c_ctx, w_ada, b_ada, w_in, gla_wa_f, gla_ba_f, gla_wa_b, gla_ba_b, gla_norm,
           attn_qnorm, attn_knorm, hy_conv_w, hy_conv_b, hy_f1_w, hy_f1_b, hy_f1_freq, hy_f2_w,
           hy_f2_b, hy_f2_freq, hy_f3_w, hy_skip, w_br_gla, w_br_attn, w_br_hy, w_out, final_norm):
    b_, L, d = x.shape
    Lc = ctx.shape[1]
    depth = w_ada.shape[0]

    cos_q, sin_q = _rope_tables(L, ATTN_HEADS)
    cos_k, sin_k = cos_q[:, :ATTN_KVW], sin_q[:, :ATTN_KVW]
    cos_qc, sin_qc = _identity_rope(Lc, ATTN_HEADS)
    cos_kc, sin_kc = cos_qc[:, :ATTN_KVW], sin_qc[:, :ATTN_KVW]
    bdq, bdk = _block_diag_ones(ATTN_QW), _block_diag_ones(ATTN_KVW)
    n1, n2 = _fft_sizes(2 * L)
    n1c, n2c = _fft_sizes(2 * Lc)
    tabs = _fft_tables(n1, n2)
    tabs_c = _fft_tables(n1c, n2c)
    zero_state = jnp.zeros((b_, GLA_DV, GLA_KW), F32)

    cmat = jnp.concatenate([c, c_ctx[None], jnp.zeros((8 - b_ - 1, d), F32)], axis=0)
    mods = _ada_call(cmat, w_ada, b_ada)

    for l in range(depth):
        need_ctx = l < depth - 1
        shift, scale, gate = [m[:b_, None, :] for m in jnp.split(mods[l], 3, axis=-1)]
        shift_c, scale_c, gate_c = [jnp.broadcast_to(m[b_:b_ + 1, None, :], (b_, 1, d))
                                    for m in jnp.split(mods[l], 3, axis=-1)]
        w_packed = _pack_w_in(w_in[l])
        P = _proj_call(x, scale, shift, w_packed)
        Pc = _proj_call(ctx, scale_c, shift_c, w_packed)

        rk = GLA_RANK
        wa_f = jnp.zeros((LANES, GLA_KW), F32).at[0:rk].set(gla_wa_f[l])
        wa_b = jnp.zeros((LANES, GLA_KW), F32).at[rk:2 * rk].set(gla_wa_b[l])
        ba_f, ba_b = gla_ba_f[l][None], gla_ba_b[l][None]
        nw = gla_norm[l][None]
        oc_f, sc_f = _gla_call(Pc, wa_f, ba_f, zero_state, reverse=False)
        yc_gla, sc_b = _gla_call(Pc, wa_b, ba_b, zero_state, reverse=True, fin=(oc_f, nw))
        o_f, _ = _gla_call(P, wa_f, ba_f, sc_f, reverse=False)
        y_gla, _ = _gla_call(P, wa_b, ba_b, sc_b, reverse=True, fin=(o_f, nw))

        gq = jnp.tile(attn_qnorm[l], ATTN_HEADS)[None]
        gk = jnp.tile(attn_knorm[l], ATTN_KV_HEADS)[None]
        q_a, k_a, v_a = _qkv_prep_call(P, cos_q, sin_q, cos_k, sin_k, gq, gk, bdq, bdk)
        qc_a, kc_a, vc_a = _qkv_prep_call(Pc, cos_qc, sin_qc, cos_kc, sin_kc, gq, gk, bdq, bdk)
        y_attn = _attn_call(q_a, jnp.concatenate([k_a, kc_a], axis=2),
                            jnp.concatenate([v_a, vc_a], axis=2), P)

        fp = (hy_f1_w[l], hy_f1_b[l], hy_f1_freq[l], hy_f2_w[l], hy_f2_b[l], hy_f2_freq[l], hy_f3_w[l])
        g_spec = _hyena_filter_spectrum(L, tabs, n1, n2, fp)
        v0, x1, x2 = _short_conv_call(P, hy_conv_w[l], hy_conv_b[l][None])
        z1 = _hyena_conv(v0, x1, g_spec, 0, hy_skip[l, 0], tabs, n1, n2)
        y_hy = _hyena_conv(z1, x2, g_spec, 1, hy_skip[l, 1], tabs, n1, n2)

        wg, wa, wh, wo = (w_br_gla[l].astype(BF16), w_br_attn[l].astype(BF16),
                          w_br_hy[l].astype(BF16), w_out[l].astype(BF16))
        fn = final_norm[None]
        x_new = _merge_call(y_gla, y_attn, y_hy, P, x, gate, wg, wa, wh, wo, fn, final=not need_ctx)

        if need_ctx:
            yc_attn = _attn_call(qc_a, kc_a, vc_a, Pc)
            gc_spec = _hyena_filter_spectrum(Lc, tabs_c, n1c, n2c, fp)
            vc0, xc1, xc2 = _short_conv_call(Pc, hy_conv_w[l], hy_conv_b[l][None])
            zc1 = _hyena_conv(vc0, xc1, gc_spec, 0, hy_skip[l, 0], tabs_c, n1c, n2c)
            yc_hy = _hyena_conv(zc1, xc2, gc_spec, 1, hy_skip[l, 1], tabs_c, n1c, n2c)
            ctx = _merge_call(yc_gla, yc_attn, yc_hy, Pc, ctx, gate_c, wg, wa, wh, wo, fn, final=False)
        x = x_new

    return x
```

```python
import functools
import math

import numpy as np
import jax
import jax.numpy as jnp
from jax import lax
from jax.experimental import pallas as pl
from jax.experimental.pallas import tpu as pltpu

F32 = jnp.float32
BF16 = jnp.bfloat16
F8 = jnp.float8_e4m3fn

D_MODEL = 1024
GRID_W = 64
BRANCH_W = D_MODEL // 2
N_BRANCH = 3
EPS = 1e-6
GLA_HEADS = 4
GLA_DV = BRANCH_W // GLA_HEADS
GLA_DK = GLA_DV // 2
GLA_KW = GLA_HEADS * GLA_DK
GLA_VW = GLA_HEADS * GLA_DV
GLA_RANK = 16
GLA_GATE_NORM = 16.0
GLA_CHUNK = 64
GLA_BLOCK = 256
ATTN_HD = 64
ATTN_HEADS = BRANCH_W // ATTN_HD
ATTN_KV_HEADS = ATTN_HEADS // 4
ATTN_GROUP = ATTN_HEADS // ATTN_KV_HEADS
ATTN_QW = ATTN_HEADS * ATTN_HD
ATTN_KVW = ATTN_KV_HEADS * ATTN_HD
ROPE_THETA = 10000.0
HY_W = BRANCH_W
HY_ORDER = 2
HY_EMB = 33
HY_BANDS = (HY_EMB - 1) // 2
HY_FFN = 64
HY_SHORT = 3
HY_MOD_SHIFT = 0.05
HY_DECAY_SHORT_PCT = 0.3
HY_DECAY_LONG_PCT = 1.5
HY_DECAY_TARGET = 1e-2
N_FILT = 2 * HY_ORDER * HY_W
SPLITS = (GLA_KW, GLA_KW, GLA_VW, GLA_VW, GLA_RANK, GLA_RANK,
          ATTN_QW, ATTN_KVW, ATTN_KVW, ATTN_QW,
          (HY_ORDER + 1) * HY_W, HY_W,
          N_BRANCH * D_MODEL)

LANES = 128
VMEM_LIMIT = 48 * 1024 * 1024
FFT_D_VMEM_LIMIT = 56 * 1024 * 1024

COLS = {
    'm': (0, 3072), 'y_u': (3072, 1536), 'g_v': (4608, 512), 'g_z': (5120, 512),
    'a_q': (5632, 512), 'a_z': (6144, 512), 'y_z': (6656, 512), 'g_q': (7168, 256),
    'g_k': (7424, 256), 'a_k': (7680, 128), 'a_v': (7808, 128), 'g_a': (7936, 128),
    'g_a_lo': (8064, 128),
}
N_PROJ = 8192
PROJ_TN = 2048


def _cparams(*sem):
    return pltpu.CompilerParams(dimension_semantics=sem, vmem_limit_bytes=VMEM_LIMIT)


def _split(a):
    hi = a.astype(BF16)
    lo = (a - hi.astype(F32)).astype(BF16)
    return hi, lo


def _dot(a, b):
    return jnp.dot(a, b, preferred_element_type=F32)


def _dot_hp(a, b):
    ah, al = _split(a)
    bh, bl = _split(b)
    return _dot(ah, bh) + _dot(al, bh) + _dot(ah, bl)


def _dot_mat(mh, ml, x):
    xh = x.astype(BF16)
    return _dot(mh, xh) + _dot(ml, xh)


def _sigmoid(x):
    return 1.0 / (1.0 + jnp.exp(-x))


def _silu(x):
    return x * _sigmoid(x)


def _ada_kernel(c_ref, w_ref, b_ref, o_ref):
    o_ref[0] = _dot_hp(_silu(c_ref[...]), w_ref[0]) + b_ref[0]


def _ada_call(cmat, w_ada, b_ada):
    depth, d, n3 = w_ada.shape
    tn = 1024
    return pl.pallas_call(
        _ada_kernel,
        grid=(depth, n3 // tn),
        in_specs=[pl.BlockSpec((8, d), lambda l, j: (0, 0)),
                  pl.BlockSpec((1, d, tn), lambda l, j: (l, 0, j)),
                  pl.BlockSpec((1, 1, tn), lambda l, j: (l, 0, j))],
        out_specs=pl.BlockSpec((1, 8, tn), lambda l, j: (l, 0, j)),
        out_shape=jax.ShapeDtypeStruct((depth, 8, n3), F32),
        compiler_params=_cparams("arbitrary", "arbitrary"),
        name="ada",
    )(cmat, w_ada, b_ada.reshape(depth, 1, n3))


def _proj_kernel(x_ref, sc_ref, sh_ref, w_ref, o_ref, *, res_tile, hi_off, lo_off):
    x = x_ref[0]
    ms = jnp.mean(x * x, axis=-1, keepdims=True)
    h = x * lax.rsqrt(ms + EPS) * (1.0 + sc_ref[0]) + sh_ref[0]
    res = _dot(h.astype(BF16), w_ref[...])
    out = res.astype(o_ref.dtype)
    o_ref[0] = out

    @pl.when(pl.program_id(0) == res_tile)
    def _():
        hi = slice(hi_off, hi_off + LANES)
        o_ref[0, :, lo_off:lo_off + LANES] = (res[:, hi] - out[:, hi].astype(F32)).astype(o_ref.dtype)


def _proj_call(x, scale, shift, w):
    b_, L, d = x.shape
    n = w.shape[1]
    tm = min(1024, L)
    tn = PROJ_TN
    hi0, lo0 = COLS['g_a'][0], COLS['g_a_lo'][0]
    assert hi0 // tn == lo0 // tn
    return pl.pallas_call(
        functools.partial(_proj_kernel, res_tile=hi0 // tn, hi_off=hi0 % tn, lo_off=lo0 % tn),
        grid=(n // tn, b_, L // tm),
        in_specs=[pl.BlockSpec((1, tm, d), lambda j, b, i: (b, i, 0)),
                  pl.BlockSpec((1, 1, d), lambda j, b, i: (b, 0, 0)),
                  pl.BlockSpec((1, 1, d), lambda j, b, i: (b, 0, 0)),
                  pl.BlockSpec((d, tn), lambda j, b, i: (0, j))],
        out_specs=pl.BlockSpec((1, tm, tn), lambda j, b, i: (b, i, j)),
        out_shape=jax.ShapeDtypeStruct((b_, L, n), BF16),
        compiler_params=_cparams("arbitrary", "arbitrary", "arbitrary"),
        name="proj",
    )(x, scale, shift, w)


def _gla_kernel(*refs, reverse, finalize, nchunks):
    if finalize:
        (q_ref, k_ref, v_ref, ga_ref, gal_ref, wa_ref, ba_ref, s0_ref, tri_ref, op_ref, z_ref, nw_ref,
         o_ref, sfin_ref, s_scr) = refs
    else:
        (q_ref, k_ref, v_ref, ga_ref, gal_ref, wa_ref, ba_ref, s0_ref, tri_ref,
         o_ref, sfin_ref, s_scr) = refs
    C = GLA_CHUNK
    T = nchunks * C
    nt_dims = (((1,), (1,)), ((), ()))
    tn_dims = (((0,), (0,)), ((), ()))

    @pl.when(pl.program_id(1) == 0)
    def _():
        s_scr[...] = s0_ref[0]

    ga = ga_ref[0].astype(F32) + gal_ref[0].astype(F32)
    xg = _dot_hp(ga, wa_ref[...]) + ba_ref[...]
    g = (jnp.minimum(xg, 0.0) - jnp.log(1.0 + jnp.exp(-jnp.abs(xg)))) * (1.0 / GLA_GATE_NORM)

    tri = tri_ref[...]
    g1 = g.astype(BF16)
    r1 = g - g1.astype(F32)
    g2 = r1.astype(BF16)
    g3 = (r1 - g2.astype(F32)).astype(BF16)
    b = _dot(tri, g1) + _dot(tri, g2) + _dot(tri, g3)

    def chunk_row(r):
        return jnp.concatenate([jnp.broadcast_to(b[c * C + r:c * C + r + 1], (C, GLA_KW))
                                for c in range(nchunks)], axis=0)

    tot_row = 0 if reverse else C - 1
    bm = chunk_row(C // 2)
    bt = chunk_row(tot_row)
    ri = lax.broadcasted_iota(jnp.int32, (T, T), 0)
    ci = lax.broadcasted_iota(jnp.int32, (T, T), 1)
    same_chunk = (ri // C) == (ci // C)
    mask = same_chunk & ((ci >= ri) if reverse else (ci <= ri))
    lane = lax.broadcasted_iota(jnp.int32, (1, GLA_KW), 1)
    hmask = [(lane >= h * GLA_DK) & (lane < (h + 1) * GLA_DK) for h in range(GLA_HEADS)]

    q = q_ref[0].astype(F32) * (GLA_DK ** -0.5)
    k = k_ref[0].astype(F32)
    v = v_ref[0].astype(BF16)
    qa = q * jnp.exp(b - bm)
    kb = (k * jnp.exp(bm - b)).astype(BF16)
    qe = q * jnp.exp(b)
    kd = (k * jnp.exp(bt - b)).astype(BF16)

    intra = []
    for h in range(GLA_HEADS):
        qa_h = jnp.where(hmask[h], qa, 0.0).astype(BF16)
        att = lax.dot_general(qa_h, kb, nt_dims, preferred_element_type=F32)
        att = jnp.where(mask, att, 0.0).astype(BF16)
        intra.append(_dot(att, v[:, h * GLA_DV:(h + 1) * GLA_DV]))

    order = range(nchunks - 1, -1, -1) if reverse else range(nchunks)
    upds = {}
    for c in order:
        rows = slice(c * C, (c + 1) * C)
        upd = None
        for h in range(GLA_HEADS):
            u_h = lax.dot_general(v[rows, h * GLA_DV:(h + 1) * GLA_DV], kd[rows], tn_dims,
                                  preferred_element_type=F32)
            u_h = jnp.where(hmask[h], u_h, 0.0)
            upd = u_h if upd is None else upd + u_h
        upds[c] = upd
    inter = [None] * nchunks
    for c in order:
        rows = slice(c * C, (c + 1) * C)
        st = s_scr[...]
        st_b = st.astype(BF16)
        dec = jnp.exp(b[c * C + tot_row:c * C + tot_row + 1])
        outs = []
        for h in range(GLA_HEADS):
            qe_h = jnp.where(hmask[h], qe[rows], 0.0).astype(BF16)
            outs.append(lax.dot_general(qe_h, st_b, nt_dims, preferred_element_type=F32))
        s_scr[...] = dec * st + upds[c]
        inter[c] = jnp.concatenate(outs, axis=1)
    o = jnp.concatenate(intra, axis=1) + jnp.concatenate(inter, axis=0)

    if finalize:
        o = o + op_ref[0]
        parts = []
        for h in range(GLA_HEADS):
            oh = o[:, h * GLA_DV:(h + 1) * GLA_DV]
            ms = jnp.mean(oh * oh, axis=-1, keepdims=True)
            parts.append(oh * lax.rsqrt(ms + EPS) * nw_ref[...])
        y = jnp.concatenate(parts, axis=1) * _silu(z_ref[0].astype(F32))
        o_ref[0] = y.astype(o_ref.dtype)
    else:
        o_ref[0] = o

    @pl.when(pl.program_id(1) == pl.num_programs(1) - 1)
    def _():
        sfin_ref[0] = s_scr[...]


def _gla_call(P, wa_pad, ba, s0, reverse, fin=None):
    b_, L, _ = P.shape
    T = min(GLA_BLOCK, L)
    nt = L // T
    if reverse:
        tmap = lambda t: nt - 1 - t
    else:
        tmap = lambda t: t
    ii = np.arange(T)
    same = (ii[:, None] // GLA_CHUNK) == (ii[None, :] // GLA_CHUNK)
    tri = same & ((ii[None, :] >= ii[:, None]) if reverse else (ii[None, :] <= ii[:, None]))
    tri = jnp.asarray(tri.astype(np.float32)).astype(BF16)

    def col(name, width):
        blk = COLS[name][0] // width
        return pl.BlockSpec((1, T, width), lambda b, t: (b, tmap(t), blk))

    in_specs = [col('g_q', GLA_KW), col('g_k', GLA_KW), col('g_v', GLA_VW),
                col('g_a', LANES), col('g_a_lo', LANES),
                pl.BlockSpec((LANES, GLA_KW), lambda b, t: (0, 0)),
                pl.BlockSpec((1, GLA_KW), lambda b, t: (0, 0)),
                pl.BlockSpec((1, GLA_DV, GLA_KW), lambda b, t: (b, 0, 0)),
                pl.BlockSpec((T, T), lambda b, t: (0, 0))]
    args = [P, P, P, P, P, wa_pad, ba, s0, tri]
    if fin is not None:
        o_prev, nw = fin
        in_specs += [pl.BlockSpec((1, T, GLA_VW), lambda b, t: (b, tmap(t), 0)),
                     col('g_z', GLA_VW),
                     pl.BlockSpec((1, GLA_DV), lambda b, t: (0, 0))]
        args += [o_prev, P, nw]
    out_dtype = BF16 if fin is not None else F32
    return pl.pallas_call(
        functools.partial(_gla_kernel, reverse=reverse, finalize=fin is not None, nchunks=T // GLA_CHUNK),
        grid=(b_, nt),
        in_specs=in_specs,
        out_specs=[pl.BlockSpec((1, T, GLA_VW), lambda b, t: (b, tmap(t), 0)),
                   pl.BlockSpec((1, GLA_DV, GLA_KW), lambda b, t: (b, 0, 0))],
        out_shape=[jax.ShapeDtypeStruct((b_, L, GLA_VW), out_dtype),
                   jax.ShapeDtypeStruct((b_, GLA_DV, GLA_KW), F32)],
        scratch_shapes=[pltpu.VMEM((GLA_DV, GLA_KW), F32)],
        compiler_params=_cparams("arbitrary", "arbitrary"),
        name="gla",
    )(*args)


def _norm_rope(a, gain, bd, cos, sin):
    sq = a * a
    sh, sl = _split(sq)
    ss = _dot(sh, bd) + _dot(sl, bd)
    an = a * lax.rsqrt(ss * (1.0 / ATTN_HD) + EPS) * gain
    w = a.shape[-1]
    lane = lax.broadcasted_iota(jnp.int32, an.shape, 1)
    partner = jnp.where(lane % 2 == 0, pltpu.roll(an, w - 1, 1), pltpu.roll(an, 1, 1))
    return an * cos + partner * sin


Q_SCALE = (ATTN_HD ** -0.5) * math.log2(math.e)
ATTN_QK_DEPTH = 4 * ATTN_HD
ATTN_TK_MAX = 640


def _split8(a):
    hi = a.astype(F8).astype(F32)
    lo = (a - hi).astype(F8)
    return hi.astype(BF16), lo.astype(BF16)


def _qkv_prep_kernel(q_ref, k_ref, v_ref, cq_ref, sq_ref, ck_ref, sk_ref, gq_ref, gk_ref,
                     bdq_ref, bdk_ref, qo_ref, ko_ref, vo_ref):
    hd = ATTN_HD
    q = _norm_rope(q_ref[0].astype(F32), gq_ref[...], bdq_ref[...], cq_ref[...], sq_ref[...]) * Q_SCALE
    qh, ql = _split8(q)
    qt = jnp.concatenate([qh, ql], axis=1).T
    for h in range(ATTN_HEADS):
        hi = qt[hd * h:hd * (h + 1)]
        lo = qt[ATTN_QW + hd * h:ATTN_QW + hd * (h + 1)]
        qo_ref[0, ATTN_QK_DEPTH * h:ATTN_QK_DEPTH * (h + 1), :] = jnp.concatenate([hi, hi, lo, lo], axis=0).astype(F8)
    k = _norm_rope(k_ref[0].astype(F32), gk_ref[...], bdk_ref[...], ck_ref[...], sk_ref[...])
    kh, kl = _split8(k)
    for g in range(ATTN_KV_HEADS):
        cols = slice(g * hd, (g + 1) * hd)
        ko_ref[0, g] = jnp.concatenate([kh[:, cols], kl[:, cols], kh[:, cols], kl[:, cols]], axis=1).astype(F8)
    vo_ref[0] = v_ref[0].astype(BF16).T


def _qkv_prep_call(P, cos_q, sin_q, cos_k, sin_k, gq, gk, bdq, bdk):
    b_, L, _ = P.shape
    T = min(1024, L)

    def col(name, width):
        blk = COLS[name][0] // width
        return pl.BlockSpec((1, T, width), lambda b, t: (b, t, blk))

    tab = lambda w: pl.BlockSpec((T, w), lambda b, t: (t, 0))
    const = lambda r, w: pl.BlockSpec((r, w), lambda b, t: (0, 0))
    return pl.pallas_call(
        _qkv_prep_kernel,
        grid=(b_, L // T),
        in_specs=[col('a_q', ATTN_QW), col('a_k', ATTN_KVW), col('a_v', ATTN_KVW),
                  tab(ATTN_QW), tab(ATTN_QW), tab(ATTN_KVW), tab(ATTN_KVW),
                  const(1, ATTN_QW), const(1, ATTN_KVW),
                  const(ATTN_QW, ATTN_QW), const(ATTN_KVW, ATTN_KVW)],
        out_specs=[pl.BlockSpec((1, ATTN_HEADS * ATTN_QK_DEPTH, T), lambda b, t: (b, 0, t)),
                   pl.BlockSpec((1, ATTN_KV_HEADS, T, ATTN_QK_DEPTH), lambda b, t: (b, 0, t, 0)),
                   pl.BlockSpec((1, ATTN_KVW, T), lambda b, t: (b, 0, t))],
        out_shape=[jax.ShapeDtypeStruct((b_, ATTN_HEADS * ATTN_QK_DEPTH, L), F8),
                   jax.ShapeDtypeStruct((b_, ATTN_KV_HEADS, L, ATTN_QK_DEPTH), F8),
                   jax.ShapeDtypeStruct((b_, ATTN_KVW, L), BF16)],
        compiler_params=_cparams("arbitrary", "arbitrary"),
        name="qkv_prep",
    )(P, P, P, cos_q, sin_q, cos_k, sin_k, gq, gk, bdq, bdk)


ATTN_SUM_ROWS = 16
ATTN_UNROLL = 2
ATTN_TQ = 512
ATTN_COLS = 256


def _attn_kernel(qt_ref, k_ref, vt_ref, z_ref, o_ref, m_scr, t_scr, acc_scr, s_scr, *, tk, nk):
    m_scr[...] = jnp.full(m_scr.shape, -jnp.inf, F32)
    acc_scr[...] = jnp.zeros(acc_scr.shape, F32)
    ones = jnp.ones((ATTN_SUM_ROWS, tk), BF16)

    tq = qt_ref.shape[2]
    width = ATTN_COLS if tq % ATTN_COLS == 0 else tq
    parts = [(h, slice(c, c + width)) for h in range(ATTN_GROUP) for c in range(0, tq, width)]

    def score_part(kj, slot, h, c):
        s = _dot(kj, qt_ref[0, h * ATTN_QK_DEPTH:(h + 1) * ATTN_QK_DEPTH, c])
        s_scr[slot, h, :, c] = s
        t_scr[slot, h, :, c] = jnp.max(s, axis=0, keepdims=True)

    def keys(j):
        start = j * tk if isinstance(j, int) else pl.multiple_of(j * tk, tk)
        return k_ref[0, 0, pl.ds(start, tk), :]

    k0 = keys(0)
    for h, c in parts:
        score_part(k0, 0, h, c)

    def step(j, slot):
        kj = keys(min(j + 1, nk - 1) if isinstance(j, int) else jnp.minimum(j + 1, nk - 1))
        start = j * tk if isinstance(j, int) else pl.multiple_of(j * tk, tk)
        vte = jnp.concatenate([vt_ref[0, :, pl.ds(start, tk)], ones], axis=0)
        for h, c in parts:
            score_part(kj, 1 - slot, h, c)
            m_old = m_scr[h, :, c]
            m_new = jnp.maximum(m_old, t_scr[slot, h, :, c])
            alpha = jnp.exp2(m_old - m_new)
            m_scr[h, :, c] = m_new
            p = jnp.exp2(s_scr[slot, h, :, c] - m_new).astype(BF16)
            acc_scr[h, :, c] = alpha * acc_scr[h, :, c] + _dot(vte, p)

    def trip(i, carry):
        for u in range(ATTN_UNROLL):
            step(ATTN_UNROLL * i + u, u % 2)
        return carry

    ntrips = nk // ATTN_UNROLL
    lax.fori_loop(0, ntrips, trip, 0)
    for j in range(ntrips * ATTN_UNROLL, nk):
        step(j, j % 2)
    o_t = jnp.concatenate([acc_scr[h, :ATTN_HD] / acc_scr[h, ATTN_HD:ATTN_HD + 1] for h in range(ATTN_GROUP)],
                          axis=0)
    out = o_t.T
    o_ref[0] = (out * _silu(z_ref[0].astype(F32))).astype(o_ref.dtype)


def _attn_call(qt, k, vt, P):
    b_, _, L = qt.shape
    Lk = k.shape[2]
    tq = min(ATTN_TQ, L)
    tk = max(t for t in range(LANES, ATTN_TK_MAX + 1, LANES) if Lk % t == 0)
    gw = ATTN_GROUP * ATTN_HD
    zblk = COLS['a_z'][0] // gw
    return pl.pallas_call(
        functools.partial(_attn_kernel, tk=tk, nk=Lk // tk),
        grid=(b_, ATTN_KV_HEADS, L // tq),
        in_specs=[pl.BlockSpec((1, ATTN_GROUP * ATTN_QK_DEPTH, tq), lambda b, g, i: (b, g, i)),
                  pl.BlockSpec((1, 1, Lk, ATTN_QK_DEPTH), lambda b, g, i: (b, g, 0, 0)),
                  pl.BlockSpec((1, ATTN_HD, Lk), lambda b, g, i: (b, g, 0)),
                  pl.BlockSpec((1, tq, gw), lambda b, g, i: (b, i, zblk + g))],
        out_specs=pl.BlockSpec((1, tq, gw), lambda b, g, i: (b, i, g)),
        out_shape=jax.ShapeDtypeStruct((b_, L, ATTN_QW), BF16),
        scratch_shapes=[pltpu.VMEM((ATTN_GROUP, 1, tq), F32),
                        pltpu.VMEM((2, ATTN_GROUP, 1, tq), F32),
                        pltpu.VMEM((ATTN_GROUP, ATTN_HD + ATTN_SUM_ROWS, tq), F32),
                        pltpu.VMEM((2, ATTN_GROUP, tk, tq), F32)],
        compiler_params=_cparams("arbitrary", "arbitrary", "arbitrary"),
        name="attn",
    )(qt, k, vt, P)


def _short_conv_kernel(u_ref, up_ref, un_ref, w_ref, b_ref, v_ref, x1_ref, x2_ref):
    t = pl.program_id(1)
    nt = pl.num_programs(1)
    u = u_ref[0].astype(F32)
    T = u.shape[0]
    hr = up_ref.shape[1]
    prev_row = jnp.where(t > 0, up_ref[0, hr - 1:hr, :].astype(F32), 0.0)
    next_row = jnp.where(t < nt - 1, un_ref[0, 0:1, :].astype(F32), 0.0)
    row = lax.broadcasted_iota(jnp.int32, u.shape, 0)
    u_m1 = jnp.where(row == 0, prev_row, pltpu.roll(u, 1, 0))
    u_p1 = jnp.where(row == T - 1, next_row, pltpu.roll(u, T - 1, 0))
    w = w_ref[...]
    out = b_ref[...] + u_m1 * w[0:1] + u * w[1:2] + u_p1 * w[2:3]
    v_ref[0] = out[:, 0:HY_W]
    x1_ref[0] = out[:, HY_W:2 * HY_W]
    x2_ref[0] = out[:, 2 * HY_W:3 * HY_W]


def _short_conv_call(P, w, bias):
    b_, L, _ = P.shape
    T = min(1024, L)
    cw = (HY_ORDER + 1) * HY_W
    blk = COLS['y_u'][0] // cw
    hr = 16
    hb = T // hr
    nh = L // hr
    out_spec = pl.BlockSpec((1, T, HY_W), lambda b, t: (b, t, 0))
    shp = jax.ShapeDtypeStruct((b_, L, HY_W), F32)
    return pl.pallas_call(
        _short_conv_kernel,
        grid=(b_, L // T),
        in_specs=[pl.BlockSpec((1, T, cw), lambda b, t: (b, t, blk)),
                  pl.BlockSpec((1, hr, cw), lambda b, t: (b, jnp.maximum(t * hb - 1, 0), blk)),
                  pl.BlockSpec((1, hr, cw), lambda b, t: (b, jnp.minimum((t + 1) * hb, nh - 1), blk)),
                  pl.BlockSpec((HY_SHORT, cw), lambda b, t: (0, 0)),
                  pl.BlockSpec((1, cw), lambda b, t: (0, 0))],
        out_specs=[out_spec, out_spec, out_spec],
        out_shape=[shp, shp, shp],
        compiler_params=_cparams("arbitrary", "arbitrary"),
        name="short_conv",
    )(P, P, P, w, bias)


FFT_ROWS = 8


def _filter_kernel(emb_ref, w1_ref, b1_ref, f1_ref, w2_ref, b2_ref, f2_ref, w3_ref, dl_ref,
                   hf_ref, hb_ref, sum_ref):
    t = pl.program_id(0)
    emb = emb_ref[...]
    h = jnp.sin(f1_ref[...] * (_dot_hp(emb, w1_ref[...]) + b1_ref[...]))
    h = jnp.sin(f2_ref[...] * (_dot_hp(h, w2_ref[...]) + b2_ref[...]))
    hh, hl = _split(h)
    tt = emb[:, 0:1]

    @pl.when(t == 0)
    def _():
        sum_ref[...] = jnp.zeros_like(sum_ref)

    lag0 = (lax.broadcasted_iota(jnp.int32, (h.shape[0], HY_W), 0) + t * h.shape[0]) == 0
    for o in range(HY_ORDER):
        for d, out_ref in enumerate((hf_ref, hb_ref)):
            cols = slice((2 * o + d) * HY_W, (2 * o + d + 1) * HY_W)
            wh, wl = _split(w3_ref[:, cols])
            f = _dot(hh, wh) + _dot(hl, wh) + _dot(hh, wl)
            f = f * (jnp.exp(-tt * dl_ref[:, cols]) + HY_MOD_SHIFT)
            sum_ref[:, cols] += jnp.broadcast_to(jnp.sum(jnp.abs(f), axis=0, keepdims=True), (8, HY_W))
            if d == 1:
                f = jnp.where(lag0, 0.0, f)
            out_ref[o] = pltpu.einshape("(an)c->a(nc)", f, a=FFT_ROWS)


def _filter_call(emb, w1, b1, f1, w2, b2, f2, w3, deltas, n1, n2):
    L = emb.shape[0]
    TL = FFT_ROWS * n2
    const = lambda r, w: pl.BlockSpec((r, w), lambda t: (0, 0))
    hspec = pl.BlockSpec((HY_ORDER, FFT_ROWS, n2 * HY_W), lambda t: (0, t, 0))
    hshape = jax.ShapeDtypeStruct((HY_ORDER, n1 // 2, n2 * HY_W), F32)
    return pl.pallas_call(
        _filter_kernel,
        grid=(L // TL,),
        in_specs=[pl.BlockSpec((TL, LANES), lambda t: (t, 0)),
                  const(LANES, HY_FFN), const(1, HY_FFN), const(1, HY_FFN),
                  const(HY_FFN, HY_FFN), const(1, HY_FFN), const(1, HY_FFN),
                  const(HY_FFN, N_FILT), const(1, N_FILT)],
        out_specs=[hspec, hspec, pl.BlockSpec((8, N_FILT), lambda t: (0, 0))],
        out_shape=[hshape, hshape, jax.ShapeDtypeStruct((8, N_FILT), F32)],
        compiler_params=_cparams("arbitrary"),
        name="hy_filter",
    )(emb, w1, b1, f1, w2, b2, f2, w3, deltas)


def _fft_sizes(n):
    lg = int(round(math.log2(n)))
    assert 1 << lg == n
    n1 = 1 << ((lg + 1) // 2)
    return n1, n // n1


def _bf16_pair(a):
    a = np.asarray(a, np.float32)
    hi = jnp.asarray(a, F32).astype(BF16)
    lo = (jnp.asarray(a, F32) - hi.astype(F32)).astype(BF16)
    return hi, lo


def _fft_tables(n1, n2):
    n = n1 * n2
    h1 = n1 // 2
    k1 = np.arange(n1)[:, None]
    a = 2.0 * np.pi * ((k1 * np.arange(h1)[None, :]) % n1) / n1
    c, s = np.cos(a), np.sin(a)
    fa = np.block([[c, s], [-s, c]])
    fa_real = np.concatenate([c, -s], axis=0)
    fd = np.block([[c.T, -s.T], [s.T, c.T]])
    k2 = np.arange(n2)[:, None]
    b = 2.0 * np.pi * ((k2 * np.arange(n2)[None, :]) % n2) / n2
    cb, sb = np.cos(b), np.sin(b)
    fb = np.block([[cb, sb], [-sb, cb]])
    fc = np.block([[cb, -sb], [sb, cb]])
    kn = (jnp.arange(n1, dtype=jnp.int32)[:, None] * jnp.arange(n2, dtype=jnp.int32)[None, :]) % n
    tw = kn.astype(F32) * (2.0 * math.pi / n)
    twr = jnp.broadcast_to(jnp.cos(tw)[:, :, None], (n1, n2, LANES))
    twi = jnp.broadcast_to(-jnp.sin(tw)[:, :, None], (n1, n2, LANES))
    return dict(fa=_bf16_pair(fa), fa_real=_bf16_pair(fa_real), fd=_bf16_pair(fd),
                fb=_bf16_pair(fb), fc=_bf16_pair(fc), twr=twr, twi=twi)


def _fa_kernel(x_ref, mh_ref, ml_ref, o_ref, xh_scr, *, packed, part_axis):
    @pl.when(pl.program_id(part_axis) == 0)
    def _():
        if packed:
            x = x_ref[...]
            x = x.reshape(x.shape[0] * x.shape[1], x.shape[2], x.shape[3])
            x = pltpu.einshape("rnc->r(nc)", x)
        else:
            x = x_ref[0]
        xh_scr[...] = x.astype(BF16)

    xh = xh_scr[...]
    r = _dot(mh_ref[...], xh) + _dot(ml_ref[...], xh)
    o_ref[0] = pltpu.einshape("r(nc)->rnc", r, n=FFT_ROWS)


def _fa_call(x4, mats, n1):
    mh, ml = mats
    _, h1, n2, wd = x4.shape
    wt = FFT_ROWS * wd
    return pl.pallas_call(
        functools.partial(_fa_kernel, packed=True, part_axis=1),
        grid=(n2 // FFT_ROWS, 2),
        in_specs=[pl.BlockSpec((2, h1, FFT_ROWS, wd), lambda j, p: (0, 0, j, 0)),
                  pl.BlockSpec((n1, n1), lambda j, p: (p, 0)),
                  pl.BlockSpec((n1, n1), lambda j, p: (p, 0))],
        out_specs=pl.BlockSpec((1, n1, FFT_ROWS, wd), lambda j, p: (p, 0, j, 0)),
        out_shape=jax.ShapeDtypeStruct((2, n1, n2, wd), F32),
        scratch_shapes=[pltpu.VMEM((n1, wt), BF16)],
        compiler_params=_cparams("arbitrary", "arbitrary"),
        name="fft_a",
    )(x4, mh, ml)


def _fa_real_call(x, mats, n1):
    mh, ml = mats
    ng, h1, lanes = x.shape
    wt = FFT_ROWS * HY_W
    n2 = lanes // HY_W
    out = pl.pallas_call(
        functools.partial(_fa_kernel, packed=False, part_axis=2),
        grid=(ng, lanes // wt, 2),
        in_specs=[pl.BlockSpec((1, h1, wt), lambda g, j, p: (g, 0, j)),
                  pl.BlockSpec((n1, h1), lambda g, j, p: (p, 0)),
                  pl.BlockSpec((n1, h1), lambda g, j, p: (p, 0))],
        out_specs=pl.BlockSpec((1, n1, FFT_ROWS, HY_W), lambda g, j, p: (2 * g + p, 0, j, 0)),
        out_shape=jax.ShapeDtypeStruct((2 * ng, n1, n2, HY_W), F32),
        scratch_shapes=[pltpu.VMEM((h1, wt), BF16)],
        compiler_params=_cparams("arbitrary", "arbitrary", "arbitrary"),
        name="fft_a_real",
    )(x, mh, ml)
    return out.reshape(ng, 2, n1, n2, HY_W)


def _lane_tile(a, width):
    return jnp.concatenate([a] * (width // a.shape[-1]), axis=-1)


def _fb_filter_kernel(tf_ref, tb_ref, twr_ref, twi_ref, fh_ref, fl_ref, sf_ref, sb_ref, g_ref, *, n2, scale):
    w = g_ref.shape[-1]
    inv = jnp.concatenate([scale / sf_ref[0:1, :], scale / sb_ref[0:1, :]], axis=1)

    def body(pair, carry):
        ks = (2 * pair, 2 * pair + 1)
        ps = []
        for k in ks:
            twr = _lane_tile(twr_ref[k], 2 * w)
            twi = _lane_tile(twi_ref[k], 2 * w)
            tr = jnp.concatenate([tf_ref[0, 0, k], tb_ref[0, 0, k]], axis=1)
            ti = jnp.concatenate([tf_ref[0, 1, k], tb_ref[0, 1, k]], axis=1)
            ps.append(jnp.concatenate([tr * twr - ti * twi, tr * twi + ti * twr], axis=0))
        zs = [_dot_mat(fh_ref[...], fl_ref[...], p) * inv for p in ps]
        for k, z in zip(ks, zs):
            g_ref[0, 0, k] = z[:n2, :w] + z[:n2, w:]
            g_ref[0, 1, k] = z[n2:, :w] - z[n2:, w:]
        return carry

    lax.fori_loop(0, FFT_ROWS // 2, body, 0, unroll=2)


def _fb_filter_call(thf, thb, sums, tabs, n1, n2):
    fh, fl = tabs['fb']
    tspec = pl.BlockSpec((1, 2, FFT_ROWS, n2, HY_W), lambda g, i: (g, 0, i, 0, 0))
    twspec = pl.BlockSpec((FFT_ROWS, n2, LANES), lambda g, i: (i, 0, 0))
    sspec = lambda d: pl.BlockSpec((8, HY_W), lambda g, i: (0, 2 * g + d))
    return pl.pallas_call(
        functools.partial(_fb_filter_kernel, n2=n2, scale=1.0 / (n1 * n2)),
        grid=(HY_ORDER, n1 // FFT_ROWS),
        in_specs=[tspec, tspec, twspec, twspec,
                  pl.BlockSpec(fh.shape, lambda g, i: (0, 0)), pl.BlockSpec(fl.shape, lambda g, i: (0, 0)),
                  sspec(0), sspec(1)],
        out_specs=pl.BlockSpec((1, 2, FFT_ROWS, n2, HY_W), lambda g, i: (g, 0, i, 0, 0)),
        out_shape=jax.ShapeDtypeStruct((HY_ORDER, 2, n1, n2, HY_W), F32),
        compiler_params=_cparams("arbitrary", "arbitrary"),
        name="fft_b_filter",
    )(thf, thb, tabs['twr'], tabs['twi'], fh, fl, sums, sums)


def _fb_kernel(t_ref, twr_ref, twi_ref, fh_ref, fl_ref, ch_ref, cl_ref, g_ref, o_ref, t4_scr, *, n2):
    w = g_ref.shape[-1]

    def body(pair, carry):
        ks = (2 * pair, 2 * pair + 1)
        tw = [(_lane_tile(twr_ref[k], w), _lane_tile(twi_ref[k], w)) for k in ks]
        ps = [jnp.concatenate([t_ref[0, k] * a - t_ref[1, k] * b, t_ref[0, k] * b + t_ref[1, k] * a], axis=0)
              for k, (a, b) in zip(ks, tw)]
        zs = [_dot_mat(fh_ref[...], fl_ref[...], p) for p in ps]
        ys = [jnp.concatenate([z[:n2] * g_ref[0, 0, k] - z[n2:] * g_ref[0, 1, k],
                               z[:n2] * g_ref[0, 1, k] + z[n2:] * g_ref[0, 0, k]], axis=0)
              for k, z in zip(ks, zs)]
        vs = [_dot_mat(ch_ref[...], cl_ref[...], y) for y in ys]
        for k, (a, b), v in zip(ks, tw, vs):
            t4_scr[0, k] = v[:n2] * a + v[n2:] * b
            t4_scr[1, k] = v[n2:] * a - v[:n2] * b
        return carry

    lax.fori_loop(0, FFT_ROWS // 2, body, 0, unroll=2)
    o_ref[...] = pltpu.einshape("pknc->pk(nc)", t4_scr[...])


def _fb_call(t, g, order, tabs, n1, n2):
    fh, fl = tabs['fb']
    ch, cl = tabs['fc']
    lanes = n2 * HY_W
    mat = lambda m: pl.BlockSpec(m.shape, lambda i: (0, 0))
    dat = pl.BlockSpec((2, FFT_ROWS, lanes), lambda i: (0, i, 0))
    twspec = pl.BlockSpec((FFT_ROWS, n2, LANES), lambda i: (i, 0, 0))
    return pl.pallas_call(
        functools.partial(_fb_kernel, n2=n2),
        grid=(n1 // FFT_ROWS,),
        in_specs=[pl.BlockSpec((2, FFT_ROWS, n2, HY_W), lambda i: (0, i, 0, 0)),
                  twspec, twspec, mat(fh), mat(fl), mat(ch), mat(cl),
                  pl.BlockSpec((1, 2, FFT_ROWS, n2, HY_W), lambda i: (order, 0, i, 0, 0))],
        out_specs=dat,
        out_shape=jax.ShapeDtypeStruct((2, n1, lanes), F32),
        scratch_shapes=[pltpu.VMEM((2, FFT_ROWS, n2, HY_W), F32)],
        compiler_params=_cparams("arbitrary"),
        name="fft_b",
    )(t, tabs['twr'], tabs['twi'], fh, fl, ch, cl, g)


def _fd_kernel(u_ref, mh_ref, ml_ref, z_ref, x_ref, sk_ref, o_ref):
    u = u_ref[...]
    u = u.reshape(u.shape[0] * u.shape[1], u.shape[2])
    y = _dot_mat(mh_ref[...], ml_ref[...], u)
    y = pltpu.einshape("r(nc)->rnc", y, n=FFT_ROWS)
    o_ref[0] = x_ref[0] * (y + sk_ref[...] * z_ref[0])


def _fd_call(u, mats, z4, x4, skip_row, n1):
    mh, ml = mats
    b_, h1, n2, wd = z4.shape
    wt = FFT_ROWS * wd
    dat = pl.BlockSpec((1, h1, FFT_ROWS, wd), lambda j, p: (p, 0, j, 0))
    return pl.pallas_call(
        _fd_kernel,
        grid=(n2 // FFT_ROWS, b_),
        in_specs=[pl.BlockSpec((2, n1, wt), lambda j, p: (0, 0, j)),
                  pl.BlockSpec((h1, 2 * n1), lambda j, p: (p, 0)),
                  pl.BlockSpec((h1, 2 * n1), lambda j, p: (p, 0)),
                  dat, dat, pl.BlockSpec((1, wd), lambda j, p: (0, 0))],
        out_specs=dat,
        out_shape=jax.ShapeDtypeStruct(z4.shape, F32),
        compiler_params=pltpu.CompilerParams(dimension_semantics=("arbitrary", "arbitrary"),
                                             vmem_limit_bytes=FFT_D_VMEM_LIMIT),
        name="fft_d",
    )(u, mh, ml, z4, x4, skip_row)


def _hyena_filter_spectrum(L, tabs, n1, n2, fp):
    f32 = F32
    t = jnp.linspace(0.0, 1.0, L, dtype=f32)[:, None]
    w = 2.0 * math.pi * jnp.arange(L, dtype=f32)[:, None] / L
    fr = jnp.linspace(1e-4, HY_BANDS - 1, HY_BANDS, dtype=f32)[None]
    emb = jnp.concatenate([t, jnp.cos(fr * w), -jnp.sin(fr * w)], axis=-1)
    emb = jnp.pad(emb, ((0, 0), (0, LANES - HY_EMB)))
    deltas = jnp.abs(jnp.linspace(math.log(HY_DECAY_TARGET) / HY_DECAY_SHORT_PCT,
                                  math.log(HY_DECAY_TARGET) / HY_DECAY_LONG_PCT, N_FILT, dtype=f32))[None]
    f1_w, f1_b, f1_freq, f2_w, f2_b, f2_freq, f3_w = fp
    w1 = jnp.pad(f1_w, ((0, LANES - HY_EMB), (0, 0)))
    hf, hb, sums = _filter_call(emb, w1, f1_b[None], f1_freq[None], f2_w, f2_b[None],
                                f2_freq[None], f3_w, deltas, n1, n2)
    thf = _fa_real_call(hf, tabs['fa_real'], n1)
    thb = _fa_real_call(hb, tabs['fa_real'], n1)
    return _fb_filter_call(thf, thb, sums, tabs, n1, n2)


def _hyena_conv(z, xg, g, order, skip, tabs, n1, n2):
    b_, L, wd = z.shape
    assert b_ == 2, "the two batch rows ride as the real and imaginary parts of one transform"
    z4 = z.reshape(b_, n1 // 2, n2, wd)
    x4 = xg.reshape(b_, n1 // 2, n2, wd)
    t = _fa_call(z4, tabs['fa'], n1)
    u = _fb_call(t, g, order, tabs, n1, n2)
    out = _fd_call(u, tabs['fd'], z4, x4, skip[None, :], n1)
    return out.reshape(b_, L, wd)


def _merge_kernel(yg_ref, ya_ref, yh_ref, yz_ref, mg_ref, ma_ref, mh_ref, x_ref, gate_ref,
                  wg_ref, wa_ref, wh_ref, wo_ref, fn_ref, o_ref, *, final):
    yh = (yh_ref[0] * _silu(yz_ref[0].astype(F32))).astype(BF16)
    y = (_sigmoid(mg_ref[0].astype(F32)) * _dot(yg_ref[0], wg_ref[...])
         + _sigmoid(ma_ref[0].astype(F32)) * _dot(ya_ref[0], wa_ref[...])
         + _sigmoid(mh_ref[0].astype(F32)) * _dot(yh, wh_ref[...]))
    out = _dot(y.astype(BF16), wo_ref[...])
    xn = x_ref[0] + gate_ref[0] * out
    if final:
        ms = jnp.mean(xn * xn, axis=-1, keepdims=True)
        xn = xn * lax.rsqrt(ms + EPS) * fn_ref[...]
    o_ref[0] = xn


def _merge_call(y_gla, y_attn, y_hy, P, x, gate, wg, wa, wh, wo, fnorm, final):
    b_, L, d = x.shape
    T = min(1024, L)
    row = lambda w, blk=0: pl.BlockSpec((1, T, w), lambda b, t: (b, t, blk))
    const = lambda r, w: pl.BlockSpec((r, w), lambda b, t: (0, 0))
    mblk = COLS['m'][0] // d
    return pl.pallas_call(
        functools.partial(_merge_kernel, final=final),
        grid=(b_, L // T),
        in_specs=[row(BRANCH_W), row(BRANCH_W), row(BRANCH_W), row(HY_W, COLS['y_z'][0] // HY_W),
                  row(d, mblk), row(d, mblk + 1), row(d, mblk + 2), row(d),
                  pl.BlockSpec((1, 1, d), lambda b, t: (b, 0, 0)),
                  const(BRANCH_W, d), const(BRANCH_W, d), const(BRANCH_W, d), const(d, d), const(1, d)],
        out_specs=row(d),
        out_shape=jax.ShapeDtypeStruct((b_, L, d), F32),
        compiler_params=_cparams("arbitrary", "arbitrary"),
        name="merge",
    )(y_gla, y_attn, y_hy, P, P, P, P, x, gate, wg, wa, wh, wo, fnorm)


def _pack_w_in(w):
    parts = dict(zip(('g_q', 'g_k', 'g_v', 'g_z', 'g_af', 'g_ab', 'a_q', 'a_k', 'a_v', 'a_z', 'y_u', 'y_z', 'm'),
                     jnp.split(w, [int(i) for i in np.cumsum(SPLITS)[:-1]], axis=-1)))
    parts['g_a'] = jnp.pad(jnp.concatenate([parts['g_af'], parts['g_ab']], axis=-1),
                           ((0, 0), (0, LANES - 2 * GLA_RANK)))
    parts['g_a_lo'] = jnp.zeros_like(parts['g_a'])
    order = sorted(COLS, key=lambda n: COLS[n][0])
    return jnp.concatenate([parts[n] for n in order], axis=-1).astype(BF16)


def _rope_tables(L, heads):
    t = jnp.arange(L)
    row = (t // GRID_W).astype(F32)
    colp = (t % GRID_W).astype(F32)
    half = ATTN_HD // 2
    inv = ROPE_THETA ** (-jnp.arange(0, half, 2, dtype=F32) / half)
    ang = jnp.concatenate([row[:, None] * inv, colp[:, None] * inv], axis=-1)
    cos = jnp.repeat(jnp.cos(ang), 2, axis=-1)
    sin = jnp.stack([-jnp.sin(ang), jnp.sin(ang)], axis=-1).reshape(L, ATTN_HD)
    return jnp.tile(cos, (1, heads)), jnp.tile(sin, (1, heads))


def _identity_rope(L, heads):
    return jnp.ones((L, heads * ATTN_HD), F32), jnp.zeros((L, heads * ATTN_HD), F32)


def _block_diag_ones(width):
    i = np.arange(width) // ATTN_HD
    return jnp.asarray((i[:, None] == i[None, :]).astype(np.float32)).astype(BF16)


def kernel(x, c, ctx, c_ctx, w_ada, b_ada, w_in, gla_wa_f, gla_ba_f, gla_wa_b, gla_ba_b, gla_norm,
           attn_qnorm, attn_knorm, hy_conv_w, hy_conv_b, hy_f1_w, hy_f1_b, hy_f1_freq, hy_f2_w,
           hy_f2_b, hy_f2_freq, hy_f3_w, hy_skip, w_br_gla, w_br_attn, w_br_hy, w_out, final_norm):
    b_, L, d = x.shape
    Lc = ctx.shape[1]
    depth = w_ada.shape[0]

    cos_q, sin_q = _rope_tables(L, ATTN_HEADS)
    cos_k, sin_k = cos_q[:, :ATTN_KVW], sin_q[:, :ATTN_KVW]
    cos_qc, sin_qc = _identity_rope(Lc, ATTN_HEADS)
    cos_kc, sin_kc = cos_qc[:, :ATTN_KVW], sin_qc[:, :ATTN_KVW]
    bdq, bdk = _block_diag_ones(ATTN_QW), _block_diag_ones(ATTN_KVW)
    n1, n2 = _fft_sizes(2 * L)
    n1c, n2c = _fft_sizes(2 * Lc)
    tabs = _fft_tables(n1, n2)
    tabs_c = _fft_tables(n1c, n2c)
    zero_state = jnp.zeros((b_, GLA_DV, GLA_KW), F32)

    cmat = jnp.concatenate([c, c_ctx[None], jnp.zeros((8 - b_ - 1, d), F32)], axis=0)
    mods = _ada_call(cmat, w_ada, b_ada)

    for l in range(depth):
        need_ctx = l < depth - 1
        shift, scale, gate = [m[:b_, None, :] for m in jnp.split(mods[l], 3, axis=-1)]
        shift_c, scale_c, gate_c = [jnp.broadcast_to(m[b_:b_ + 1, None, :], (b_, 1, d))
                                    for m in jnp.split(mods[l], 3, axis=-1)]
        w_packed = _pack_w_in(w_in[l])
        P = _proj_call(x, scale, shift, w_packed)
        Pc = _proj_call(ctx, scale_c, shift_c, w_packed)

        rk = GLA_RANK
        wa_f = jnp.zeros((LANES, GLA_KW), F32).at[0:rk].set(gla_wa_f[l])
        wa_b = jnp.zeros((LANES, GLA_KW), F32).at[rk:2 * rk].set(gla_wa_b[l])
        ba_f, ba_b = gla_ba_f[l][None], gla_ba_b[l][None]
        nw = gla_norm[l][None]
        oc_f, sc_f = _gla_call(Pc, wa_f, ba_f, zero_state, reverse=False)
        yc_gla, sc_b = _gla_call(Pc, wa_b, ba_b, zero_state, reverse=True, fin=(oc_f, nw))
        o_f, _ = _gla_call(P, wa_f, ba_f, sc_f, reverse=False)
        y_gla, _ = _gla_call(P, wa_b, ba_b, sc_b, reverse=True, fin=(o_f, nw))

        gq = jnp.tile(attn_qnorm[l], ATTN_HEADS)[None]
        gk = jnp.tile(attn_knorm[l], ATTN_KV_HEADS)[None]
        q_a, k_a, v_a = _qkv_prep_call(P, cos_q, sin_q, cos_k, sin_k, gq, gk, bdq, bdk)
        qc_a, kc_a, vc_a = _qkv_prep_call(Pc, cos_qc, sin_qc, cos_kc, sin_kc, gq, gk, bdq, bdk)
        y_attn = _attn_call(q_a, jnp.concatenate([k_a, kc_a], axis=2),
                            jnp.concatenate([v_a, vc_a], axis=2), P)

        fp = (hy_f1_w[l], hy_f1_b[l], hy_f1_freq[l], hy_f2_w[l], hy_f2_b[l], hy_f2_freq[l], hy_f3_w[l])
        g_spec = _hyena_filter_spectrum(L, tabs, n1, n2, fp)
        v0, x1, x2 = _short_conv_call(P, hy_conv_w[l], hy_conv_b[l][None])
        z1 = _hyena_conv(v0, x1, g_spec, 0, hy_skip[l, 0], tabs, n1, n2)
        y_hy = _hyena_conv(z1, x2, g_spec, 1, hy_skip[l, 1], tabs, n1, n2)

        wg, wa, wh, wo = (w_br_gla[l].astype(BF16), w_br_attn[l].astype(BF16),
                          w_br_hy[l].astype(BF16), w_out[l].astype(BF16))
        fn = final_norm[None]
        x_new = _merge_call(y_gla, y_attn, y_hy, P, x, gate, wg, wa, wh, wo, fn, final=not need_ctx)

        if need_ctx:
            yc_attn = _attn_call(qc_a, kc_a, vc_a, Pc)
            gc_spec = _hyena_filter_spectrum(Lc, tabs_c, n1c, n2c, fp)
            vc0, xc1, xc2 = _short_conv_call(Pc, hy_conv_w[l], hy_conv_b[l][None])
            zc1 = _hyena_conv(vc0, xc1, gc_spec, 0, hy_skip[l, 0], tabs_c, n1c, n2c)
            yc_hy = _hyena_conv(zc1, xc2, gc_spec, 1, hy_skip[l, 1], tabs_c, n1c, n2c)
            ctx = _merge_call(yc_gla, yc_attn, yc_hy, Pc, ctx, gate_c, wg, wa, wh, wo, fn, final=False)
        x = x_new

    return x
```

```python
import functools
import math

import numpy as np
import jax
import jax.numpy as jnp
from jax import lax
from jax.experimental import pallas as pl
from jax.experimental.pallas import tpu as pltpu

F32 = jnp.float32
BF16 = jnp.bfloat16
F8 = jnp.float8_e4m3fn

D_MODEL = 1024
GRID_W = 64
BRANCH_W = D_MODEL // 2
N_BRANCH = 3
EPS = 1e-6
GLA_HEADS = 4
GLA_DV = BRANCH_W // GLA_HEADS
GLA_DK = GLA_DV // 2
GLA_KW = GLA_HEADS * GLA_DK
GLA_VW = GLA_HEADS * GLA_DV
GLA_RANK = 16
GLA_GATE_NORM = 16.0
GLA_CHUNK = 64
GLA_BLOCK = 256
ATTN_HD = 64
ATTN_HEADS = BRANCH_W // ATTN_HD
ATTN_KV_HEADS = ATTN_HEADS // 4
ATTN_GROUP = ATTN_HEADS // ATTN_KV_HEADS
ATTN_QW = ATTN_HEADS * ATTN_HD
ATTN_KVW = ATTN_KV_HEADS * ATTN_HD
ROPE_THETA = 10000.0
HY_W = BRANCH_W
HY_ORDER = 2
HY_EMB = 33
HY_BANDS = (HY_EMB - 1) // 2
HY_FFN = 64
HY_SHORT = 3
HY_MOD_SHIFT = 0.05
HY_DECAY_SHORT_PCT = 0.3
HY_DECAY_LONG_PCT = 1.5
HY_DECAY_TARGET = 1e-2
N_FILT = 2 * HY_ORDER * HY_W
SPLITS = (GLA_KW, GLA_KW, GLA_VW, GLA_VW, GLA_RANK, GLA_RANK,
          ATTN_QW, ATTN_KVW, ATTN_KVW, ATTN_QW,
          (HY_ORDER + 1) * HY_W, HY_W,
          N_BRANCH * D_MODEL)

LANES = 128
VMEM_LIMIT = 48 * 1024 * 1024
FFT_D_VMEM_LIMIT = 56 * 1024 * 1024

COLS = {
    'm': (0, 3072), 'y_u': (3072, 1536), 'g_v': (4608, 512), 'g_z': (5120, 512),
    'a_q': (5632, 512), 'a_z': (6144, 512), 'y_z': (6656, 512), 'g_q': (7168, 256),
    'g_k': (7424, 256), 'a_k': (7680, 128), 'a_v': (7808, 128), 'g_a': (7936, 128),
    'g_a_lo': (8064, 128),
}
N_PROJ = 8192
PROJ_TN = 2048


def _cparams(*sem):
    return pltpu.CompilerParams(dimension_semantics=sem, vmem_limit_bytes=VMEM_LIMIT)


def _split(a):
    hi = a.astype(BF16)
    lo = (a - hi.astype(F32)).astype(BF16)
    return hi, lo


def _dot(a, b):
    return jnp.dot(a, b, preferred_element_type=F32)


def _dot_hp(a, b):
    ah, al = _split(a)
    bh, bl = _split(b)
    return _dot(ah, bh) + _dot(al, bh) + _dot(ah, bl)


def _dot_mat(mh, ml, x):
    xh = x.astype(BF16)
    return _dot(mh, xh) + _dot(ml, xh)


def _sigmoid(x):
    return 1.0 / (1.0 + jnp.exp(-x))


def _silu(x):
    return x * _sigmoid(x)


def _ada_kernel(c_ref, w_ref, b_ref, o_ref):
    o_ref[0] = _dot_hp(_silu(c_ref[...]), w_ref[0]) + b_ref[0]


def _ada_call(cmat, w_ada, b_ada):
    depth, d, n3 = w_ada.shape
    tn = 1024
    return pl.pallas_call(
        _ada_kernel,
        grid=(depth, n3 // tn),
        in_specs=[pl.BlockSpec((8, d), lambda l, j: (0, 0)),
                  pl.BlockSpec((1, d, tn), lambda l, j: (l, 0, j)),
                  pl.BlockSpec((1, 1, tn), lambda l, j: (l, 0, j))],
        out_specs=pl.BlockSpec((1, 8, tn), lambda l, j: (l, 0, j)),
        out_shape=jax.ShapeDtypeStruct((depth, 8, n3), F32),
        compiler_params=_cparams("arbitrary", "arbitrary"),
        name="ada",
    )(cmat, w_ada, b_ada.reshape(depth, 1, n3))


def _proj_kernel(x_ref, sc_ref, sh_ref, w_ref, o_ref, *, res_tile, hi_off, lo_off):
    x = x_ref[0]
    ms = jnp.mean(x * x, axis=-1, keepdims=True)
    h = x * lax.rsqrt(ms + EPS) * (1.0 + sc_ref[0]) + sh_ref[0]
    res = _dot(h.astype(BF16), w_ref[...])
    out = res.astype(o_ref.dtype)
    o_ref[0] = out

    @pl.when(pl.program_id(0) == res_tile)
    def _():
        hi = slice(hi_off, hi_off + LANES)
        o_ref[0, :, lo_off:lo_off + LANES] = (res[:, hi] - out[:, hi].astype(F32)).astype(o_ref.dtype)


def _proj_call(x, scale, shift, w):
    b_, L, d = x.shape
    n = w.shape[1]
    tm = min(1024, L)
    tn = PROJ_TN
    hi0, lo0 = COLS['g_a'][0], COLS['g_a_lo'][0]
    assert hi0 // tn == lo0 // tn
    return pl.pallas_call(
        functools.partial(_proj_kernel, res_tile=hi0 // tn, hi_off=hi0 % tn, lo_off=lo0 % tn),
        grid=(n // tn, b_, L // tm),
        in_specs=[pl.BlockSpec((1, tm, d), lambda j, b, i: (b, i, 0)),
                  pl.BlockSpec((1, 1, d), lambda j, b, i: (b, 0, 0)),
                  pl.BlockSpec((1, 1, d), lambda j, b, i: (b, 0, 0)),
                  pl.BlockSpec((d, tn), lambda j, b, i: (0, j))],
        out_specs=pl.BlockSpec((1, tm, tn), lambda j, b, i: (b, i, j)),
        out_shape=jax.ShapeDtypeStruct((b_, L, n), BF16),
        compiler_params=_cparams("arbitrary", "arbitrary", "arbitrary"),
        name="proj",
    )(x, scale, shift, w)


def _gla_kernel(*refs, reverse, finalize, nchunks):
    if finalize:
        (q_ref, k_ref, v_ref, ga_ref, gal_ref, wa_ref, ba_ref, s0_ref, tri_ref, op_ref, z_ref, nw_ref,
         o_ref, sfin_ref, s_scr) = refs
    else:
        (q_ref, k_ref, v_ref, ga_ref, gal_ref, wa_ref, ba_ref, s0_ref, tri_ref,
         o_ref, sfin_ref, s_scr) = refs
    C = GLA_CHUNK
    T = nchunks * C
    nt_dims = (((1,), (1,)), ((), ()))
    tn_dims = (((0,), (0,)), ((), ()))

    @pl.when(pl.program_id(1) == 0)
    def _():
        s_scr[...] = s0_ref[0]

    ga = ga_ref[0].astype(F32) + gal_ref[0].astype(F32)
    xg = _dot_hp(ga, wa_ref[...]) + ba_ref[...]
    g = (jnp.minimum(xg, 0.0) - jnp.log(1.0 + jnp.exp(-jnp.abs(xg)))) * (1.0 / GLA_GATE_NORM)

    tri = tri_ref[...]
    g1 = g.astype(BF16)
    r1 = g - g1.astype(F32)
    g2 = r1.astype(BF16)
    g3 = (r1 - g2.astype(F32)).astype(BF16)
    b = _dot(tri, g1) + _dot(tri, g2) + _dot(tri, g3)

    def chunk_row(r):
        return jnp.concatenate([jnp.broadcast_to(b[c * C + r:c * C + r + 1], (C, GLA_KW))
                                for c in range(nchunks)], axis=0)

    tot_row = 0 if reverse else C - 1
    bm = chunk_row(C // 2)
    bt = chunk_row(tot_row)
    ri = lax.broadcasted_iota(jnp.int32, (T, T), 0)
    ci = lax.broadcasted_iota(jnp.int32, (T, T), 1)
    same_chunk = (ri // C) == (ci // C)
    mask = same_chunk & ((ci >= ri) if reverse else (ci <= ri))
    lane = lax.broadcasted_iota(jnp.int32, (1, GLA_KW), 1)
    hmask = [(lane >= h * GLA_DK) & (lane < (h + 1) * GLA_DK) for h in range(GLA_HEADS)]

    q = q_ref[0].astype(F32) * (GLA_DK ** -0.5)
    k = k_ref[0].astype(F32)
    v = v_ref[0].astype(BF16)
    qa = q * jnp.exp(b - bm)
    kb = (k * jnp.exp(bm - b)).astype(BF16)
    qe = q * jnp.exp(b)
    kd = (k * jnp.exp(bt - b)).astype(BF16)

    intra = []
    for h in range(GLA_HEADS):
        qa_h = jnp.where(hmask[h], qa, 0.0).astype(BF16)
        att = lax.dot_general(qa_h, kb, nt_dims, preferred_element_type=F32)
        att = jnp.where(mask, att, 0.0).astype(BF16)
        intra.append(_dot(att, v[:, h * GLA_DV:(h + 1) * GLA_DV]))

    order = range(nchunks - 1, -1, -1) if reverse else range(nchunks)
    upds = {}
    for c in order:
        rows = slice(c * C, (c + 1) * C)
        upd = None
        for h in range(GLA_HEADS):
            u_h = lax.dot_general(v[rows, h * GLA_DV:(h + 1) * GLA_DV], kd[rows], tn_dims,
                                  preferred_element_type=F32)
            u_h = jnp.where(hmask[h], u_h, 0.0)
            upd = u_h if upd is None else upd + u_h
        upds[c] = upd
    inter = [None] * nchunks
    for c in order:
        rows = slice(c * C, (c + 1) * C)
        st = s_scr[...]
        st_b = st.astype(BF16)
        dec = jnp.exp(b[c * C + tot_row:c * C + tot_row + 1])
        outs = []
        for h in range(GLA_HEADS):
            qe_h = jnp.where(hmask[h], qe[rows], 0.0).astype(BF16)
            outs.append(lax.dot_general(qe_h, st_b, nt_dims, preferred_element_type=F32))
        s_scr[...] = dec * st + upds[c]
        inter[c] = jnp.concatenate(outs, axis=1)
    o = jnp.concatenate(intra, axis=1) + jnp.concatenate(inter, axis=0)

    if finalize:
        o = o + op_ref[0]
        parts = []
        for h in range(GLA_HEADS):
            oh = o[:, h * GLA_DV:(h + 1) * GLA_DV]
            ms = jnp.mean(oh * oh, axis=-1, keepdims=True)
            parts.append(oh * lax.rsqrt(ms + EPS) * nw_ref[...])
        y = jnp.concatenate(parts, axis=1) * _silu(z_ref[0].astype(F32))
        o_ref[0] = y.astype(o_ref.dtype)
    else:
        o_ref[0] = o

    @pl.when(pl.program_id(1) == pl.num_programs(1) - 1)
    def _():
        sfin_ref[0] = s_scr[...]


def _gla_call(P, wa_pad, ba, s0, reverse, fin=None):
    b_, L, _ = P.shape
    T = min(GLA_BLOCK, L)
    nt = L // T
    if reverse:
        tmap = lambda t: nt - 1 - t
    else:
        tmap = lambda t: t
    ii = np.arange(T)
    same = (ii[:, None] // GLA_CHUNK) == (ii[None, :] // GLA_CHUNK)
    tri = same & ((ii[None, :] >= ii[:, None]) if reverse else (ii[None, :] <= ii[:, None]))
    tri = jnp.asarray(tri.astype(np.float32)).astype(BF16)

    def col(name, width):
        blk = COLS[name][0] // width
        return pl.BlockSpec((1, T, width), lambda b, t: (b, tmap(t), blk))

    in_specs = [col('g_q', GLA_KW), col('g_k', GLA_KW), col('g_v', GLA_VW),
                col('g_a', LANES), col('g_a_lo', LANES),
                pl.BlockSpec((LANES, GLA_KW), lambda b, t: (0, 0)),
                pl.BlockSpec((1, GLA_KW), lambda b, t: (0, 0)),
                pl.BlockSpec((1, GLA_DV, GLA_KW), lambda b, t: (b, 0, 0)),
                pl.BlockSpec((T, T), lambda b, t: (0, 0))]
    args = [P, P, P, P, P, wa_pad, ba, s0, tri]
    if fin is not None:
        o_prev, nw = fin
        in_specs += [pl.BlockSpec((1, T, GLA_VW), lambda b, t: (b, tmap(t), 0)),
                     col('g_z', GLA_VW),
                     pl.BlockSpec((1, GLA_DV), lambda b, t: (0, 0))]
        args += [o_prev, P, nw]
    out_dtype = BF16 if fin is not None else F32
    return pl.pallas_call(
        functools.partial(_gla_kernel, reverse=reverse, finalize=fin is not None, nchunks=T // GLA_CHUNK),
        grid=(b_, nt),
        in_specs=in_specs,
        out_specs=[pl.BlockSpec((1, T, GLA_VW), lambda b, t: (b, tmap(t), 0)),
                   pl.BlockSpec((1, GLA_DV, GLA_KW), lambda b, t: (b, 0, 0))],
        out_shape=[jax.ShapeDtypeStruct((b_, L, GLA_VW), out_dtype),
                   jax.ShapeDtypeStruct((b_, GLA_DV, GLA_KW), F32)],
        scratch_shapes=[pltpu.VMEM((GLA_DV, GLA_KW), F32)],
        compiler_params=_cparams("arbitrary", "arbitrary"),
        name="gla",
    )(*args)


def _norm_rope(a, gain, bd, cos, sin):
    sq = a * a
    sh, sl = _split(sq)
    ss = _dot(sh, bd) + _dot(sl, bd)
    an = a * lax.rsqrt(ss * (1.0 / ATTN_HD) + EPS) * gain
    w = a.shape[-1]
    lane = lax.broadcasted_iota(jnp.int32, an.shape, 1)
    partner = jnp.where(lane % 2 == 0, pltpu.roll(an, w - 1, 1), pltpu.roll(an, 1, 1))
    return an * cos + partner * sin


Q_SCALE = (ATTN_HD ** -0.5) * math.log2(math.e)
ATTN_QK_DEPTH = 4 * ATTN_HD
ATTN_TK_MAX = 640


def _split8(a):
    hi = a.astype(F8).astype(F32)
    lo = (a - hi).astype(F8)
    return hi.astype(BF16), lo.astype(BF16)


def _qkv_prep_kernel(q_ref, k_ref, v_ref, cq_ref, sq_ref, ck_ref, sk_ref, gq_ref, gk_ref,
                     bdq_ref, bdk_ref, qo_ref, ko_ref, vo_ref):
    hd = ATTN_HD
    q = _norm_rope(q_ref[0].astype(F32), gq_ref[...], bdq_ref[...], cq_ref[...], sq_ref[...]) * Q_SCALE
    qh, ql = _split8(q)
    qt = jnp.concatenate([qh, ql], axis=1).T
    for h in range(ATTN_HEADS):
        hi = qt[hd * h:hd * (h + 1)]
        lo = qt[ATTN_QW + hd * h:ATTN_QW + hd * (h + 1)]
        qo_ref[0, ATTN_QK_DEPTH * h:ATTN_QK_DEPTH * (h + 1), :] = jnp.concatenate([hi, hi, lo, lo], axis=0).astype(F8)
    k = _norm_rope(k_ref[0].astype(F32), gk_ref[...], bdk_ref[...], ck_ref[...], sk_ref[...])
    kh, kl = _split8(k)
    for g in range(ATTN_KV_HEADS):
        cols = slice(g * hd, (g + 1) * hd)
        ko_ref[0, g] = jnp.concatenate([kh[:, cols], kl[:, cols], kh[:, cols], kl[:, cols]], axis=1).astype(F8)
    vo_ref[0] = v_ref[0].astype(BF16).T


def _qkv_prep_call(P, cos_q, sin_q, cos_k, sin_k, gq, gk, bdq, bdk):
    b_, L, _ = P.shape
    T = min(1024, L)

    def col(name, width):
        blk = COLS[name][0] // width
        return pl.BlockSpec((1, T, width), lambda b, t: (b, t, blk))

    tab = lambda w: pl.BlockSpec((T, w), lambda b, t: (t, 0))
    const = lambda r, w: pl.BlockSpec((r, w), lambda b, t: (0, 0))
    return pl.pallas_call(
        _qkv_prep_kernel,
        grid=(b_, L // T),
        in_specs=[col('a_q', ATTN_QW), col('a_k', ATTN_KVW), col('a_v', ATTN_KVW),
                  tab(ATTN_QW), tab(ATTN_QW), tab(ATTN_KVW), tab(ATTN_KVW),
                  const(1, ATTN_QW), const(1, ATTN_KVW),
                  const(ATTN_QW, ATTN_QW), const(ATTN_KVW, ATTN_KVW)],
        out_specs=[pl.BlockSpec((1, ATTN_HEADS * ATTN_QK_DEPTH, T), lambda b, t: (b, 0, t)),
                   pl.BlockSpec((1, ATTN_KV_HEADS, T, ATTN_QK_DEPTH), lambda b, t: (b, 0, t, 0)),
                   pl.BlockSpec((1, ATTN_KVW, T), lambda b, t: (b, 0, t))],
        out_shape=[jax.ShapeDtypeStruct((b_, ATTN_HEADS * ATTN_QK_DEPTH, L), F8),
                   jax.ShapeDtypeStruct((b_, ATTN_KV_HEADS, L, ATTN_QK_DEPTH), F8),
                   jax.ShapeDtypeStruct((b_, ATTN_KVW, L), BF16)],
        compiler_params=_cparams("arbitrary", "arbitrary"),
        name="qkv_prep",
    )(P, P, P, cos_q, sin_q, cos_k, sin_k, gq, gk, bdq, bdk)


ATTN_SUM_ROWS = 16
ATTN_UNROLL = 2
ATTN_TQ = 512
ATTN_COLS = 256


def _attn_kernel(qt_ref, k_ref, vt_ref, z_ref, o_ref, m_scr, t_scr, acc_scr, s_scr, *, tk, nk):
    m_scr[...] = jnp.full(m_scr.shape, -jnp.inf, F32)
    acc_scr[...] = jnp.zeros(acc_scr.shape, F32)
    ones = jnp.ones((ATTN_SUM_ROWS, tk), BF16)

    tq = qt_ref.shape[2]
    width = ATTN_COLS if tq % ATTN_COLS == 0 else tq
    parts = [(h, slice(c, c + width)) for h in range(ATTN_GROUP) for c in range(0, tq, width)]

    def score_part(kj, slot, h, c):
        s = _dot(kj, qt_ref[0, h * ATTN_QK_DEPTH:(h + 1) * ATTN_QK_DEPTH, c])
        s_scr[slot, h, :, c] = s
        t_scr[slot, h, :, c] = jnp.max(s, axis=0, keepdims=True)

    def keys(j):
        start = j * tk if isinstance(j, int) else pl.multiple_of(j * tk, tk)
        return k_ref[0, 0, pl.ds(start, tk), :]

    k0 = keys(0)
    for h, c in parts:
        score_part(k0, 0, h, c)

    def step(j, slot):
        kj = keys(min(j + 1, nk - 1) if isinstance(j, int) else jnp.minimum(j + 1, nk - 1))
        start = j * tk if isinstance(j, int) else pl.multiple_of(j * tk, tk)
        vte = jnp.concatenate([vt_ref[0, :, pl.ds(start, tk)], ones], axis=0)
        for h, c in parts:
            score_part(kj, 1 - slot, h, c)
            m_old = m_scr[h, :, c]
            m_new = jnp.maximum(m_old, t_scr[slot, h, :, c])
            alpha = jnp.exp2(m_old - m_new)
            m_scr[h, :, c] = m_new
            p = jnp.exp2(s_scr[slot, h, :, c] - m_new).astype(BF16)
            acc_scr[h, :, c] = alpha * acc_scr[h, :, c] + _dot(vte, p)

    def trip(i, carry):
        for u in range(ATTN_UNROLL):
            step(ATTN_UNROLL * i + u, u % 2)
        return carry

    ntrips = nk // ATTN_UNROLL
    lax.fori_loop(0, ntrips, trip, 0)
    for j in range(ntrips * ATTN_UNROLL, nk):
        step(j, j % 2)
    o_t = jnp.concatenate([acc_scr[h, :ATTN_HD] / acc_scr[h, ATTN_HD:ATTN_HD + 1] for h in range(ATTN_GROUP)],
                          axis=0)
    out = o_t.T
    o_ref[0] = (out * _silu(z_ref[0].astype(F32))).astype(o_ref.dtype)


def _attn_call(qt, k, vt, P):
    b_, _, L = qt.shape
    Lk = k.shape[2]
    tq = min(ATTN_TQ, L)
    tk = max(t for t in range(LANES, ATTN_TK_MAX + 1, LANES) if Lk % t == 0)
    gw = ATTN_GROUP * ATTN_HD
    zblk = COLS['a_z'][0] // gw
    return pl.pallas_call(
        functools.partial(_attn_kernel, tk=tk, nk=Lk // tk),
        grid=(b_, ATTN_KV_HEADS, L // tq),
        in_specs=[pl.BlockSpec((1, ATTN_GROUP * ATTN_QK_DEPTH, tq), lambda b, g, i: (b, g, i)),
                  pl.BlockSpec((1, 1, Lk, ATTN_QK_DEPTH), lambda b, g, i: (b, g, 0, 0)),
                  pl.BlockSpec((1, ATTN_HD, Lk), lambda b, g, i: (b, g, 0)),
                  pl.BlockSpec((1, tq, gw), lambda b, g, i: (b, i, zblk + g))],
        out_specs=pl.BlockSpec((1, tq, gw), lambda b, g, i: (b, i, g)),
        out_shape=jax.ShapeDtypeStruct((b_, L, ATTN_QW), BF16),
        scratch_shapes=[pltpu.VMEM((ATTN_GROUP, 1, tq), F32),
                        pltpu.VMEM((2, ATTN_GROUP, 1, tq), F32),
                        pltpu.VMEM((ATTN_GROUP, ATTN_HD + ATTN_SUM_ROWS, tq), F32),
                        pltpu.VMEM((2, ATTN_GROUP, tk, tq), F32)],
        compiler_params=_cparams("arbitrary", "arbitrary", "arbitrary"),
        name="attn",
    )(qt, k, vt, P)


def _short_conv_kernel(u_ref, up_ref, un_ref, w_ref, b_ref, v_ref, x1_ref, x2_ref):
    t = pl.program_id(1)
    nt = pl.num_programs(1)
    u = u_ref[0].astype(F32)
    T = u.shape[0]
    hr = up_ref.shape[1]
    prev_row = jnp.where(t > 0, up_ref[0, hr - 1:hr, :].astype(F32), 0.0)
    next_row = jnp.where(t < nt - 1, un_ref[0, 0:1, :].astype(F32), 0.0)
    row = lax.broadcasted_iota(jnp.int32, u.shape, 0)
    u_m1 = jnp.where(row == 0, prev_row, pltpu.roll(u, 1, 0))
    u_p1 = jnp.where(row == T - 1, next_row, pltpu.roll(u, T - 1, 0))
    w = w_ref[...]
    out = b_ref[...] + u_m1 * w[0:1] + u * w[1:2] + u_p1 * w[2:3]
    v_ref[0] = out[:, 0:HY_W]
    x1_ref[0] = out[:, HY_W:2 * HY_W]
    x2_ref[0] = out[:, 2 * HY_W:3 * HY_W]


def _short_conv_call(P, w, bias):
    b_, L, _ = P.shape
    T = min(1024, L)
    cw = (HY_ORDER + 1) * HY_W
    blk = COLS['y_u'][0] // cw
    hr = 16
    hb = T // hr
    nh = L // hr
    out_spec = pl.BlockSpec((1, T, HY_W), lambda b, t: (b, t, 0))
    shp = jax.ShapeDtypeStruct((b_, L, HY_W), F32)
    return pl.pallas_call(
        _short_conv_kernel,
        grid=(b_, L // T),
        in_specs=[pl.BlockSpec((1, T, cw), lambda b, t: (b, t, blk)),
                  pl.BlockSpec((1, hr, cw), lambda b, t: (b, jnp.maximum(t * hb - 1, 0), blk)),
                  pl.BlockSpec((1, hr, cw), lambda b, t: (b, jnp.minimum((t + 1) * hb, nh - 1), blk)),
                  pl.BlockSpec((HY_SHORT, cw), lambda b, t: (0, 0)),
                  pl.BlockSpec((1, cw), lambda b, t: (0, 0))],
        out_specs=[out_spec, out_spec, out_spec],
        out_shape=[shp, shp, shp],
        compiler_params=_cparams("arbitrary", "arbitrary"),
        name="short_conv",
    )(P, P, P, w, bias)


FFT_ROWS = 8


def _filter_kernel(emb_ref, w1_ref, b1_ref, f1_ref, w2_ref, b2_ref, f2_ref, w3_ref, dl_ref,
                   hf_ref, hb_ref, sum_ref):
    t = pl.program_id(0)
    emb = emb_ref[...]
    h = jnp.sin(f1_ref[...] * (_dot_hp(emb, w1_ref[...]) + b1_ref[...]))
    h = jnp.sin(f2_ref[...] * (_dot_hp(h, w2_ref[...]) + b2_ref[...]))
    hh, hl = _split(h)
    tt = emb[:, 0:1]

    @pl.when(t == 0)
    def _():
        sum_ref[...] = jnp.zeros_like(sum_ref)

    lag0 = (lax.broadcasted_iota(jnp.int32, (h.shape[0], HY_W), 0) + t * h.shape[0]) == 0
    for o in range(HY_ORDER):
        for d, out_ref in enumerate((hf_ref, hb_ref)):
            cols = slice((2 * o + d) * HY_W, (2 * o + d + 1) * HY_W)
            wh, wl = _split(w3_ref[:, cols])
            f = _dot(hh, wh) + _dot(hl, wh) + _dot(hh, wl)
            f = f * (jnp.exp(-tt * dl_ref[:, cols]) + HY_MOD_SHIFT)
            sum_ref[:, cols] += jnp.broadcast_to(jnp.sum(jnp.abs(f), axis=0, keepdims=True), (8, HY_W))
            if d == 1:
                f = jnp.where(lag0, 0.0, f)
            out_ref[o] = pltpu.einshape("(an)c->a(nc)", f, a=FFT_ROWS)


def _filter_call(emb, w1, b1, f1, w2, b2, f2, w3, deltas, n1, n2):
    L = emb.shape[0]
    TL = FFT_ROWS * n2
    const = lambda r, w: pl.BlockSpec((r, w), lambda t: (0, 0))
    hspec = pl.BlockSpec((HY_ORDER, FFT_ROWS, n2 * HY_W), lambda t: (0, t, 0))
    hshape = jax.ShapeDtypeStruct((HY_ORDER, n1 // 2, n2 * HY_W), F32)
    return pl.pallas_call(
        _filter_kernel,
        grid=(L // TL,),
        in_specs=[pl.BlockSpec((TL, LANES), lambda t: (t, 0)),
                  const(LANES, HY_FFN), const(1, HY_FFN), const(1, HY_FFN),
                  const(HY_FFN, HY_FFN), const(1, HY_FFN), const(1, HY_FFN),
                  const(HY_FFN, N_FILT), const(1, N_FILT)],
        out_specs=[hspec, hspec, pl.BlockSpec((8, N_FILT), lambda t: (0, 0))],
        out_shape=[hshape, hshape, jax.ShapeDtypeStruct((8, N_FILT), F32)],
        compiler_params=_cparams("arbitrary"),
        name="hy_filter",
    )(emb, w1, b1, f1, w2, b2, f2, w3, deltas)


def _fft_sizes(n):
    lg = int(round(math.log2(n)))
    assert 1 << lg == n
    n1 = 1 << ((lg + 1) // 2)
    return n1, n // n1


def _bf16_pair(a):
    a = np.asarray(a, np.float32)
    hi = jnp.asarray(a, F32).astype(BF16)
    lo = (jnp.asarray(a, F32) - hi.astype(F32)).astype(BF16)
    return hi, lo


def _fft_tables(n1, n2):
    n = n1 * n2
    h1 = n1 // 2
    k1 = np.arange(n1)[:, None]
    a = 2.0 * np.pi * ((k1 * np.arange(h1)[None, :]) % n1) / n1
    c, s = np.cos(a), np.sin(a)
    fa = np.block([[c, s], [-s, c]])
    fa_real = np.concatenate([c, -s], axis=0)
    fd = np.block([[c.T, -s.T], [s.T, c.T]])
    k2 = np.arange(n2)[:, None]
    b = 2.0 * np.pi * ((k2 * np.arange(n2)[None, :]) % n2) / n2
    cb, sb = np.cos(b), np.sin(b)
    fb = np.block([[cb, sb], [-sb, cb]])
    fc = np.block([[cb, -sb], [sb, cb]])
    kn = (jnp.arange(n1, dtype=jnp.int32)[:, None] * jnp.arange(n2, dtype=jnp.int32)[None, :]) % n
    tw = kn.astype(F32) * (2.0 * math.pi / n)
    twr = jnp.broadcast_to(jnp.cos(tw)[:, :, None], (n1, n2, LANES))
    twi = jnp.broadcast_to(-jnp.sin(tw)[:, :, None], (n1, n2, LANES))
    return dict(fa=_bf16_pair(fa), fa_real=_bf16_pair(fa_real), fd=_bf16_pair(fd),
                fb=_bf16_pair(fb), fc=_bf16_pair(fc), twr=twr, twi=twi)


def _fa_kernel(x_ref, mh_ref, ml_ref, o_ref, xh_scr, *, packed, part_axis):
    if packed:
        @pl.when(pl.program_id(part_axis) == 0)
        def _():
            x = x_ref[...]
            x = x.reshape(x.shape[0] * x.shape[1], x.shape[2], x.shape[3])
            xh_scr[...] = pltpu.einshape("rnc->r(nc)", x).astype(BF16)

        xh = xh_scr[...]
    else:
        xh = x_ref[0].astype(BF16)
    r = _dot(mh_ref[...], xh) + _dot(ml_ref[...], xh)
    o_ref[0] = pltpu.einshape("r(nc)->rnc", r, n=FFT_ROWS)


def _fa_call(x4, mats, n1):
    mh, ml = mats
    _, h1, n2, wd = x4.shape
    wt = FFT_ROWS * wd
    return pl.pallas_call(
        functools.partial(_fa_kernel, packed=True, part_axis=1),
        grid=(n2 // FFT_ROWS, 2),
        in_specs=[pl.BlockSpec((2, h1, FFT_ROWS, wd), lambda j, p: (0, 0, j, 0)),
                  pl.BlockSpec((n1, n1), lambda j, p: (p, 0)),
                  pl.BlockSpec((n1, n1), lambda j, p: (p, 0))],
        out_specs=pl.BlockSpec((1, n1, FFT_ROWS, wd), lambda j, p: (p, 0, j, 0)),
        out_shape=jax.ShapeDtypeStruct((2, n1, n2, wd), F32),
        scratch_shapes=[pltpu.VMEM((n1, wt), BF16)],
        compiler_params=_cparams("arbitrary", "arbitrary"),
        name="fft_a",
    )(x4, mh, ml)


def _fa_real_call(x, mats, n1):
    mh, ml = mats
    ng, h1, lanes = x.shape
    wt = FFT_ROWS * HY_W
    n2 = lanes // HY_W
    out = pl.pallas_call(
        functools.partial(_fa_kernel, packed=False, part_axis=2),
        grid=(ng, lanes // wt, 2),
        in_specs=[pl.BlockSpec((1, h1, wt), lambda g, j, p: (g, 0, j)),
                  pl.BlockSpec((n1, h1), lambda g, j, p: (p, 0)),
                  pl.BlockSpec((n1, h1), lambda g, j, p: (p, 0))],
        out_specs=pl.BlockSpec((1, n1, FFT_ROWS, HY_W), lambda g, j, p: (2 * g + p, 0, j, 0)),
        out_shape=jax.ShapeDtypeStruct((2 * ng, n1, n2, HY_W), F32),
        scratch_shapes=[pltpu.VMEM((h1, wt), BF16)],
        compiler_params=_cparams("arbitrary", "arbitrary", "arbitrary"),
        name="fft_a_real",
    )(x, mh, ml)
    return out.reshape(ng, 2, n1, n2, HY_W)


def _lane_tile(a, width):
    return jnp.concatenate([a] * (width // a.shape[-1]), axis=-1)


def _fb_filter_kernel(tf_ref, tb_ref, twr_ref, twi_ref, fh_ref, fl_ref, sf_ref, sb_ref, g_ref, *, n2, scale):
    w = g_ref.shape[-1]
    inv = jnp.concatenate([scale / sf_ref[0:1, :], scale / sb_ref[0:1, :]], axis=1)

    def body(pair, carry):
        ks = (2 * pair, 2 * pair + 1)
        ps = []
        for k in ks:
            twr = _lane_tile(twr_ref[k], 2 * w)
            twi = _lane_tile(twi_ref[k], 2 * w)
            tr = jnp.concatenate([tf_ref[0, 0, k], tb_ref[0, 0, k]], axis=1)
            ti = jnp.concatenate([tf_ref[0, 1, k], tb_ref[0, 1, k]], axis=1)
            ps.append(jnp.concatenate([tr * twr - ti * twi, tr * twi + ti * twr], axis=0))
        zs = [_dot_mat(fh_ref[...], fl_ref[...], p) * inv for p in ps]
        for k, z in zip(ks, zs):
            g_ref[0, 0, k] = z[:n2, :w] + z[:n2, w:]
            g_ref[0, 1, k] = z[n2:, :w] - z[n2:, w:]
        return carry

    lax.fori_loop(0, FFT_ROWS // 2, body, 0, unroll=2)


def _fb_filter_call(thf, thb, sums, tabs, n1, n2):
    fh, fl = tabs['fb']
    tspec = pl.BlockSpec((1, 2, FFT_ROWS, n2, HY_W), lambda g, i: (g, 0, i, 0, 0))
    twspec = pl.BlockSpec((FFT_ROWS, n2, LANES), lambda g, i: (i, 0, 0))
    sspec = lambda d: pl.BlockSpec((8, HY_W), lambda g, i: (0, 2 * g + d))
    return pl.pallas_call(
        functools.partial(_fb_filter_kernel, n2=n2, scale=1.0 / (n1 * n2)),
        grid=(HY_ORDER, n1 // FFT_ROWS),
        in_specs=[tspec, tspec, twspec, twspec,
                  pl.BlockSpec(fh.shape, lambda g, i: (0, 0)), pl.BlockSpec(fl.shape, lambda g, i: (0, 0)),
                  sspec(0), sspec(1)],
        out_specs=pl.BlockSpec((1, 2, FFT_ROWS, n2, HY_W), lambda g, i: (g, 0, i, 0, 0)),
        out_shape=jax.ShapeDtypeStruct((HY_ORDER, 2, n1, n2, HY_W), F32),
        compiler_params=_cparams("arbitrary", "arbitrary"),
        name="fft_b_filter",
    )(thf, thb, tabs['twr'], tabs['twi'], fh, fl, sums, sums)


def _fb_kernel(t_ref, twr_ref, twi_ref, fh_ref, fl_ref, ch_ref, cl_ref, g_ref, o_ref, t4_scr, *, n2):
    w = g_ref.shape[-1]

    def body(pair, carry):
        ks = (2 * pair, 2 * pair + 1)
        tw = [(_lane_tile(twr_ref[k], w), _lane_tile(twi_ref[k], w)) for k in ks]
        ps = [jnp.concatenate([t_ref[0, k] * a - t_ref[1, k] * b, t_ref[0, k] * b + t_ref[1, k] * a], axis=0)
              for k, (a, b) in zip(ks, tw)]
        zs = [_dot_mat(fh_ref[...], fl_ref[...], p) for p in ps]
        ys = [jnp.concatenate([z[:n2] * g_ref[0, 0, k] - z[n2:] * g_ref[0, 1, k],
                               z[:n2] * g_ref[0, 1, k] + z[n2:] * g_ref[0, 0, k]], axis=0)
              for k, z in zip(ks, zs)]
        vs = [_dot_mat(ch_ref[...], cl_ref[...], y) for y in ys]
        for k, (a, b), v in zip(ks, tw, vs):
            t4_scr[0, k] = v[:n2] * a + v[n2:] * b
            t4_scr[1, k] = v[n2:] * a - v[:n2] * b
        return carry

    lax.fori_loop(0, FFT_ROWS // 2, body, 0, unroll=2)
    o_ref[...] = pltpu.einshape("pknc->pk(nc)", t4_scr[...])


def _fb_call(t, g, order, tabs, n1, n2):
    fh, fl = tabs['fb']
    ch, cl = tabs['fc']
    lanes = n2 * HY_W
    mat = lambda m: pl.BlockSpec(m.shape, lambda i: (0, 0))
    dat = pl.BlockSpec((2, FFT_ROWS, lanes), lambda i: (0, i, 0))
    twspec = pl.BlockSpec((FFT_ROWS, n2, LANES), lambda i: (i, 0, 0))
    return pl.pallas_call(
        functools.partial(_fb_kernel, n2=n2),
        grid=(n1 // FFT_ROWS,),
        in_specs=[pl.BlockSpec((2, FFT_ROWS, n2, HY_W), lambda i: (0, i, 0, 0)),
                  twspec, twspec, mat(fh), mat(fl), mat(ch), mat(cl),
                  pl.BlockSpec((1, 2, FFT_ROWS, n2, HY_W), lambda i: (order, 0, i, 0, 0))],
        out_specs=dat,
        out_shape=jax.ShapeDtypeStruct((2, n1, lanes), F32),
        scratch_shapes=[pltpu.VMEM((2, FFT_ROWS, n2, HY_W), F32)],
        compiler_params=_cparams("arbitrary"),
        name="fft_b",
    )(t, tabs['twr'], tabs['twi'], fh, fl, ch, cl, g)


def _fd_kernel(u_ref, mh_ref, ml_ref, z_ref, x_ref, sk_ref, o_ref):
    u = u_ref[...]
    u = u.reshape(u.shape[0] * u.shape[1], u.shape[2])
    y = _dot_mat(mh_ref[...], ml_ref[...], u)
    y = pltpu.einshape("r(nc)->rnc", y, n=FFT_ROWS)
    o_ref[0] = x_ref[0] * (y + sk_ref[...] * z_ref[0])


def _fd_call(u, mats, z4, x4, skip_row, n1):
    mh, ml = mats
    b_, h1, n2, wd = z4.shape
    wt = FFT_ROWS * wd
    dat = pl.BlockSpec((1, h1, FFT_ROWS, wd), lambda j, p: (p, 0, j, 0))
    return pl.pallas_call(
        _fd_kernel,
        grid=(n2 // FFT_ROWS, b_),
        in_specs=[pl.BlockSpec((2, n1, wt), lambda j, p: (0, 0, j)),
                  pl.BlockSpec((h1, 2 * n1), lambda j, p: (p, 0)),
                  pl.BlockSpec((h1, 2 * n1), lambda j, p: (p, 0)),
                  dat, dat, pl.BlockSpec((1, wd), lambda j, p: (0, 0))],
        out_specs=dat,
        out_shape=jax.ShapeDtypeStruct(z4.shape, F32),
        compiler_params=pltpu.CompilerParams(dimension_semantics=("arbitrary", "arbitrary"),
                                             vmem_limit_bytes=FFT_D_VMEM_LIMIT),
        name="fft_d",
    )(u, mh, ml, z4, x4, skip_row)


def _hyena_filter_spectrum(L, tabs, n1, n2, fp):
    f32 = F32
    t = jnp.linspace(0.0, 1.0, L, dtype=f32)[:, None]
    w = 2.0 * math.pi * jnp.arange(L, dtype=f32)[:, None] / L
    fr = jnp.linspace(1e-4, HY_BANDS - 1, HY_BANDS, dtype=f32)[None]
    emb = jnp.concatenate([t, jnp.cos(fr * w), -jnp.sin(fr * w)], axis=-1)
    emb = jnp.pad(emb, ((0, 0), (0, LANES - HY_EMB)))
    deltas = jnp.abs(jnp.linspace(math.log(HY_DECAY_TARGET) / HY_DECAY_SHORT_PCT,
                                  math.log(HY_DECAY_TARGET) / HY_DECAY_LONG_PCT, N_FILT, dtype=f32))[None]
    f1_w, f1_b, f1_freq, f2_w, f2_b, f2_freq, f3_w = fp
    w1 = jnp.pad(f1_w, ((0, LANES - HY_EMB), (0, 0)))
    hf, hb, sums = _filter_call(emb, w1, f1_b[None], f1_freq[None], f2_w, f2_b[None],
                                f2_freq[None], f3_w, deltas, n1, n2)
    thf = _fa_real_call(hf, tabs['fa_real'], n1)
    thb = _fa_real_call(hb, tabs['fa_real'], n1)
    return _fb_filter_call(thf, thb, sums, tabs, n1, n2)


def _hyena_conv(z, xg, g, order, skip, tabs, n1, n2):
    b_, L, wd = z.shape
    assert b_ == 2, "the two batch rows ride as the real and imaginary parts of one transform"
    z4 = z.reshape(b_, n1 // 2, n2, wd)
    x4 = xg.reshape(b_, n1 // 2, n2, wd)
    t = _fa_call(z4, tabs['fa'], n1)
    u = _fb_call(t, g, order, tabs, n1, n2)
    out = _fd_call(u, tabs['fd'], z4, x4, skip[None, :], n1)
    return out.reshape(b_, L, wd)


def _merge_kernel(yg_ref, ya_ref, yh_ref, yz_ref, mg_ref, ma_ref, mh_ref, x_ref, gate_ref,
                  wg_ref, wa_ref, wh_ref, wo_ref, fn_ref, o_ref, *, final):
    yh = (yh_ref[0] * _silu(yz_ref[0].astype(F32))).astype(BF16)
    y = (_sigmoid(mg_ref[0].astype(F32)) * _dot(yg_ref[0], wg_ref[...])
         + _sigmoid(ma_ref[0].astype(F32)) * _dot(ya_ref[0], wa_ref[...])
         + _sigmoid(mh_ref[0].astype(F32)) * _dot(yh, wh_ref[...]))
    out = _dot(y.astype(BF16), wo_ref[...])
    xn = x_ref[0] + gate_ref[0] * out
    if final:
        ms = jnp.mean(xn * xn, axis=-1, keepdims=True)
        xn = xn * lax.rsqrt(ms + EPS) * fn_ref[...]
    o_ref[0] = xn


def _merge_call(y_gla, y_attn, y_hy, P, x, gate, wg, wa, wh, wo, fnorm, final):
    b_, L, d = x.shape
    T = min(1024, L)
    row = lambda w, blk=0: pl.BlockSpec((1, T, w), lambda b, t: (b, t, blk))
    const = lambda r, w: pl.BlockSpec((r, w), lambda b, t: (0, 0))
    mblk = COLS['m'][0] // d
    return pl.pallas_call(
        functools.partial(_merge_kernel, final=final),
        grid=(b_, L // T),
        in_specs=[row(BRANCH_W), row(BRANCH_W), row(BRANCH_W), row(HY_W, COLS['y_z'][0] // HY_W),
                  row(d, mblk), row(d, mblk + 1), row(d, mblk + 2), row(d),
                  pl.BlockSpec((1, 1, d), lambda b, t: (b, 0, 0)),
                  const(BRANCH_W, d), const(BRANCH_W, d), const(BRANCH_W, d), const(d, d), const(1, d)],
        out_specs=row(d),
        out_shape=jax.ShapeDtypeStruct((b_, L, d), F32),
        compiler_params=_cparams("arbitrary", "arbitrary"),
        name="merge",
    )(y_gla, y_attn, y_hy, P, P, P, P, x, gate, wg, wa, wh, wo, fnorm)


def _pack_w_in(w):
    parts = dict(zip(('g_q', 'g_k', 'g_v', 'g_z', 'g_af', 'g_ab', 'a_q', 'a_k', 'a_v', 'a_z', 'y_u', 'y_z', 'm'),
                     jnp.split(w, [int(i) for i in np.cumsum(SPLITS)[:-1]], axis=-1)))
    parts['g_a'] = jnp.pad(jnp.concatenate([parts['g_af'], parts['g_ab']], axis=-1),
                           ((0, 0), (0, LANES - 2 * GLA_RANK)))
    parts['g_a_lo'] = jnp.zeros_like(parts['g_a'])
    order = sorted(COLS, key=lambda n: COLS[n][0])
    return jnp.concatenate([parts[n] for n in order], axis=-1).astype(BF16)


def _rope_tables(L, heads):
    t = jnp.arange(L)
    row = (t // GRID_W).astype(F32)
    colp = (t % GRID_W).astype(F32)
    half = ATTN_HD // 2
    inv = ROPE_THETA ** (-jnp.arange(0, half, 2, dtype=F32) / half)
    ang = jnp.concatenate([row[:, None] * inv, colp[:, None] * inv], axis=-1)
    cos = jnp.repeat(jnp.cos(ang), 2, axis=-1)
    sin = jnp.stack([-jnp.sin(ang), jnp.sin(ang)], axis=-1).reshape(L, ATTN_HD)
    return jnp.tile(cos, (1, heads)), jnp.tile(sin, (1, heads))


def _identity_rope(L, heads):
    return jnp.ones((L, heads * ATTN_HD), F32), jnp.zeros((L, heads * ATTN_HD), F32)


def _block_diag_ones(width):
    i = np.arange(width) // ATTN_HD
    return jnp.asarray((i[:, None] == i[None, :]).astype(np.float32)).astype(BF16)


def kernel(x, c, ctx, c_ctx, w_ada, b_ada, w_in, gla_wa_f, gla_ba_f, gla_wa_b, gla_ba_b, gla_norm,
           attn_qnorm, attn_knorm, hy_conv_w, hy_conv_b, hy_f1_w, hy_f1_b, hy_f1_freq, hy_f2_w,
           hy_f2_b, hy_f2_freq, hy_f3_w, hy_skip, w_br_gla, w_br_attn, w_br_hy, w_out, final_norm):
    b_, L, d = x.shape
    Lc = ctx.shape[1]
    depth = w_ada.shape[0]

    cos_q, sin_q = _rope_tables(L, ATTN_HEADS)
    cos_k, sin_k = cos_q[:, :ATTN_KVW], sin_q[:, :ATTN_KVW]
    cos_qc, sin_qc = _identity_rope(Lc, ATTN_HEADS)
    cos_kc, sin_kc = cos_qc[:, :ATTN_KVW], sin_qc[:, :ATTN_KVW]
    bdq, bdk = _block_diag_ones(ATTN_QW), _block_diag_ones(ATTN_KVW)
    n1, n2 = _fft_sizes(2 * L)
    n1c, n2c = _fft_sizes(2 * Lc)
    tabs = _fft_tables(n1, n2)
    tabs_c = _fft_tables(n1c, n2c)
    zero_state = jnp.zeros((b_, GLA_DV, GLA_KW), F32)

    cmat = jnp.concatenate([c, c_ctx[None], jnp.zeros((8 - b_ - 1, d), F32)], axis=0)
    mods = _ada_call(cmat, w_ada, b_ada)

    for l in range(depth):
        need_ctx = l < depth - 1
        shift, scale, gate = [m[:b_, None, :] for m in jnp.split(mods[l], 3, axis=-1)]
        shift_c, scale_c, gate_c = [jnp.broadcast_to(m[b_:b_ + 1, None, :], (b_, 1, d))
                                    for m in jnp.split(mods[l], 3, axis=-1)]
        w_packed = _pack_w_in(w_in[l])
        P = _proj_call(x, scale, shift, w_packed)
        Pc = _proj_call(ctx, scale_c, shift_c, w_packed)

        rk = GLA_RANK
        wa_f = jnp.zeros((LANES, GLA_KW), F32).at[0:rk].set(gla_wa_f[l])
        wa_b = jnp.zeros((LANES, GLA_KW), F32).at[rk:2 * rk].set(gla_wa_b[l])
        ba_f, ba_b = gla_ba_f[l][None], gla_ba_b[l][None]
        nw = gla_norm[l][None]
        oc_f, sc_f = _gla_call(Pc, wa_f, ba_f, zero_state, reverse=False)
        yc_gla, sc_b = _gla_call(Pc, wa_b, ba_b, zero_state, reverse=True, fin=(oc_f, nw))
        o_f, _ = _gla_call(P, wa_f, ba_f, sc_f, reverse=False)
        y_gla, _ = _gla_call(P, wa_b, ba_b, sc_b, reverse=True, fin=(o_f, nw))

        gq = jnp.tile(attn_qnorm[l], ATTN_HEADS)[None]
        gk = jnp.tile(attn_knorm[l], ATTN_KV_HEADS)[None]
        q_a, k_a, v_a = _qkv_prep_call(P, cos_q, sin_q, cos_k, sin_k, gq, gk, bdq, bdk)
        qc_a, kc_a, vc_a = _qkv_prep_call(Pc, cos_qc, sin_qc, cos_kc, sin_kc, gq, gk, bdq, bdk)
        y_attn = _attn_call(q_a, jnp.concatenate([k_a, kc_a], axis=2),
                            jnp.concatenate([v_a, vc_a], axis=2), P)

        fp = (hy_f1_w[l], hy_f1_b[l], hy_f1_freq[l], hy_f2_w[l], hy_f2_b[l], hy_f2_freq[l], hy_f3_w[l])
        g_spec = _hyena_filter_spectrum(L, tabs, n1, n2, fp)
        v0, x1, x2 = _short_conv_call(P, hy_conv_w[l], hy_conv_b[l][None])
        z1 = _hyena_conv(v0, x1, g_spec, 0, hy_skip[l, 0], tabs, n1, n2)
        y_hy = _hyena_conv(z1, x2, g_spec, 1, hy_skip[l, 1], tabs, n1, n2)

        wg, wa, wh, wo = (w_br_gla[l].astype(BF16), w_br_attn[l].astype(BF16),
                          w_br_hy[l].astype(BF16), w_out[l].astype(BF16))
        fn = final_norm[None]
        x_new = _merge_call(y_gla, y_attn, y_hy, P, x, gate, wg, wa, wh, wo, fn, final=not need_ctx)

        if need_ctx:
            yc_attn = _attn_call(qc_a, kc_a, vc_a, Pc)
            gc_spec = _hyena_filter_spectrum(Lc, tabs_c, n1c, n2c, fp)
            vc0, xc1, xc2 = _short_conv_call(Pc, hy_conv_w[l], hy_conv_b[l][None])
            zc1 = _hyena_conv(vc0, xc1, gc_spec, 0, hy_skip[l, 0], tabs_c, n1c, n2c)
            yc_hy = _hyena_conv(zc1, xc2, gc_spec, 1, hy_skip[l, 1], tabs_c, n1c, n2c)
            ctx = _merge_call(yc_gla, yc_attn, yc_hy, Pc, ctx, gate_c, wg, wa, wh, wo, fn, final=False)
        x = x_new

    return x
```

```python
import functools
import math

import numpy as np
import jax
import jax.numpy as jnp
from jax import lax
from jax.experimental import pallas as pl
from jax.experimental.pallas import tpu as pltpu

F32 = jnp.float32
BF16 = jnp.bfloat16
F8 = jnp.float8_e4m3fn

D_MODEL = 1024
GRID_W = 64
BRANCH_W = D_MODEL // 2
N_BRANCH = 3
EPS = 1e-6
GLA_HEADS = 4
GLA_DV = BRANCH_W // GLA_HEADS
GLA_DK = GLA_DV // 2
GLA_KW = GLA_HEADS * GLA_DK
GLA_VW = GLA_HEADS * GLA_DV
GLA_RANK = 16
GLA_GATE_NORM = 16.0
GLA_CHUNK = 64
GLA_BLOCK = 256
ATTN_HD = 64
ATTN_HEADS = BRANCH_W // ATTN_HD
ATTN_KV_HEADS = ATTN_HEADS // 4
ATTN_GROUP = ATTN_HEADS // ATTN_KV_HEADS
ATTN_QW = ATTN_HEADS * ATTN_HD
ATTN_KVW = ATTN_KV_HEADS * ATTN_HD
ROPE_THETA = 10000.0
HY_W = BRANCH_W
HY_ORDER = 2
HY_EMB = 33
HY_BANDS = (HY_EMB - 1) // 2
HY_FFN = 64
HY_SHORT = 3
HY_MOD_SHIFT = 0.05
HY_DECAY_SHORT_PCT = 0.3
HY_DECAY_LONG_PCT = 1.5
HY_DECAY_TARGET = 1e-2
N_FILT = 2 * HY_ORDER * HY_W
SPLITS = (GLA_KW, GLA_KW, GLA_VW, GLA_VW, GLA_RANK, GLA_RANK,
          ATTN_QW, ATTN_KVW, ATTN_KVW, ATTN_QW,
          (HY_ORDER + 1) * HY_W, HY_W,
          N_BRANCH * D_MODEL)

LANES = 128
VMEM_LIMIT = 48 * 1024 * 1024
FFT_D_VMEM_LIMIT = 56 * 1024 * 1024

COLS = {
    'm': (0, 3072), 'y_u': (3072, 1536), 'g_v': (4608, 512), 'g_z': (5120, 512),
    'a_q': (5632, 512), 'a_z': (6144, 512), 'y_z': (6656, 512), 'g_q': (7168, 256),
    'g_k': (7424, 256), 'a_k': (7680, 128), 'a_v': (7808, 128), 'g_a': (7936, 128),
    'g_a_lo': (8064, 128),
}
N_PROJ = 8192
PROJ_TN = 2048


def _cparams(*sem):
    return pltpu.CompilerParams(dimension_semantics=sem, vmem_limit_bytes=VMEM_LIMIT)


def _split(a):
    hi = a.astype(BF16)
    lo = (a - hi.astype(F32)).astype(BF16)
    return hi, lo


def _dot(a, b):
    return jnp.dot(a, b, preferred_element_type=F32)


def _dot_hp(a, b):
    ah, al = _split(a)
    bh, bl = _split(b)
    return _dot(ah, bh) + _dot(al, bh) + _dot(ah, bl)


def _dot_mat(mh, ml, x):
    xh = x.astype(BF16)
    return _dot(mh, xh) + _dot(ml, xh)


def _sigmoid(x):
    return 1.0 / (1.0 + jnp.exp(-x))


def _silu(x):
    return x * _sigmoid(x)


def _ada_kernel(c_ref, w_ref, b_ref, o_ref):
    o_ref[0] = _dot_hp(_silu(c_ref[...]), w_ref[0]) + b_ref[0]


def _ada_call(cmat, w_ada, b_ada):
    depth, d, n3 = w_ada.shape
    tn = 1024
    return pl.pallas_call(
        _ada_kernel,
        grid=(depth, n3 // tn),
        in_specs=[pl.BlockSpec((8, d), lambda l, j: (0, 0)),
                  pl.BlockSpec((1, d, tn), lambda l, j: (l, 0, j)),
                  pl.BlockSpec((1, 1, tn), lambda l, j: (l, 0, j))],
        out_specs=pl.BlockSpec((1, 8, tn), lambda l, j: (l, 0, j)),
        out_shape=jax.ShapeDtypeStruct((depth, 8, n3), F32),
        compiler_params=_cparams("arbitrary", "arbitrary"),
        name="ada",
    )(cmat, w_ada, b_ada.reshape(depth, 1, n3))


def _proj_kernel(x_ref, sc_ref, sh_ref, w_ref, o_ref, *, res_tile, hi_off, lo_off):
    x = x_ref[0]
    ms = jnp.mean(x * x, axis=-1, keepdims=True)
    h = x * lax.rsqrt(ms + EPS) * (1.0 + sc_ref[0]) + sh_ref[0]
    res = _dot(h.astype(BF16), w_ref[...])
    out = res.astype(o_ref.dtype)
    o_ref[0] = out
    hi = slice(hi_off, hi_off + LANES)
    lo = slice(lo_off, lo_off + LANES)
    resid = (res[:, hi] - out[:, hi].astype(F32)).astype(o_ref.dtype)
    o_ref[0, :, lo] = jnp.where(pl.program_id(0) == res_tile, resid, out[:, lo])


def _proj_call(x, scale, shift, w):
    b_, L, d = x.shape
    n = w.shape[1]
    tm = min(1024, L)
    tn = PROJ_TN
    hi0, lo0 = COLS['g_a'][0], COLS['g_a_lo'][0]
    assert hi0 // tn == lo0 // tn
    return pl.pallas_call(
        functools.partial(_proj_kernel, res_tile=hi0 // tn, hi_off=hi0 % tn, lo_off=lo0 % tn),
        grid=(n // tn, b_, L // tm),
        in_specs=[pl.BlockSpec((1, tm, d), lambda j, b, i: (b, i, 0)),
                  pl.BlockSpec((1, 1, d), lambda j, b, i: (b, 0, 0)),
                  pl.BlockSpec((1, 1, d), lambda j, b, i: (b, 0, 0)),
                  pl.BlockSpec((d, tn), lambda j, b, i: (0, j))],
        out_specs=pl.BlockSpec((1, tm, tn), lambda j, b, i: (b, i, j)),
        out_shape=jax.ShapeDtypeStruct((b_, L, n), BF16),
        compiler_params=_cparams("arbitrary", "arbitrary", "arbitrary"),
        name="proj",
    )(x, scale, shift, w)


def _gla_kernel(*refs, reverse, finalize, nchunks):
    if finalize:
        (q_ref, k_ref, v_ref, ga_ref, gal_ref, wa_ref, ba_ref, s0_ref, tri_ref, op_ref, z_ref, nw_ref,
         o_ref, sfin_ref, s_scr) = refs
    else:
        (q_ref, k_ref, v_ref, ga_ref, gal_ref, wa_ref, ba_ref, s0_ref, tri_ref,
         o_ref, sfin_ref, s_scr) = refs
    C = GLA_CHUNK
    T = nchunks * C
    nt_dims = (((1,), (1,)), ((), ()))
    tn_dims = (((0,), (0,)), ((), ()))

    @pl.when(pl.program_id(1) == 0)
    def _():
        s_scr[...] = s0_ref[0]

    ga = ga_ref[0].astype(F32) + gal_ref[0].astype(F32)
    xg = _dot_hp(ga, wa_ref[...]) + ba_ref[...]
    g = (jnp.minimum(xg, 0.0) - jnp.log(1.0 + jnp.exp(-jnp.abs(xg)))) * (1.0 / GLA_GATE_NORM)

    tri = tri_ref[...]
    g1 = g.astype(BF16)
    r1 = g - g1.astype(F32)
    g2 = r1.astype(BF16)
    g3 = (r1 - g2.astype(F32)).astype(BF16)
    b = _dot(tri, g1) + _dot(tri, g2) + _dot(tri, g3)

    def chunk_row(r):
        return jnp.concatenate([jnp.broadcast_to(b[c * C + r:c * C + r + 1], (C, GLA_KW))
                                for c in range(nchunks)], axis=0)

    tot_row = 0 if reverse else C - 1
    bm = chunk_row(C // 2)
    bt = chunk_row(tot_row)
    ri = lax.broadcasted_iota(jnp.int32, (T, T), 0)
    ci = lax.broadcasted_iota(jnp.int32, (T, T), 1)
    same_chunk = (ri // C) == (ci // C)
    mask = same_chunk & ((ci >= ri) if reverse else (ci <= ri))
    lane = lax.broadcasted_iota(jnp.int32, (1, GLA_KW), 1)
    hmask = [(lane >= h * GLA_DK) & (lane < (h + 1) * GLA_DK) for h in range(GLA_HEADS)]

    q = q_ref[0].astype(F32) * (GLA_DK ** -0.5)
    k = k_ref[0].astype(F32)
    v = v_ref[0].astype(BF16)
    qa = q * jnp.exp(b - bm)
    kb = (k * jnp.exp(bm - b)).astype(BF16)
    qe = q * jnp.exp(b)
    kd = (k * jnp.exp(bt - b)).astype(BF16)

    intra = []
    for h in range(GLA_HEADS):
        qa_h = jnp.where(hmask[h], qa, 0.0).astype(BF16)
        att = lax.dot_general(qa_h, kb, nt_dims, preferred_element_type=F32)
        att = jnp.where(mask, att, 0.0).astype(BF16)
        intra.append(_dot(att, v[:, h * GLA_DV:(h + 1) * GLA_DV]))

    order = range(nchunks - 1, -1, -1) if reverse else range(nchunks)
    upds = {}
    for c in order:
        rows = slice(c * C, (c + 1) * C)
        upd = None
        for h in range(GLA_HEADS):
            u_h = lax.dot_general(v[rows, h * GLA_DV:(h + 1) * GLA_DV], kd[rows], tn_dims,
                                  preferred_element_type=F32)
            u_h = jnp.where(hmask[h], u_h, 0.0)
            upd = u_h if upd is None else upd + u_h
        upds[c] = upd
    inter = [None] * nchunks
    for c in order:
        rows = slice(c * C, (c + 1) * C)
        st = s_scr[...]
        st_b = st.astype(BF16)
        dec = jnp.exp(b[c * C + tot_row:c * C + tot_row + 1])
        outs = []
        for h in range(GLA_HEADS):
            qe_h = jnp.where(hmask[h], qe[rows], 0.0).astype(BF16)
            outs.append(lax.dot_general(qe_h, st_b, nt_dims, preferred_element_type=F32))
        s_scr[...] = dec * st + upds[c]
        inter[c] = jnp.concatenate(outs, axis=1)
    o = jnp.concatenate(intra, axis=1) + jnp.concatenate(inter, axis=0)

    if finalize:
        o = o + op_ref[0]
        parts = []
        for h in range(GLA_HEADS):
            oh = o[:, h * GLA_DV:(h + 1) * GLA_DV]
            ms = jnp.mean(oh * oh, axis=-1, keepdims=True)
            parts.append(oh * lax.rsqrt(ms + EPS) * nw_ref[...])
        y = jnp.concatenate(parts, axis=1) * _silu(z_ref[0].astype(F32))
        o_ref[0] = y.astype(o_ref.dtype)
    else:
        o_ref[0] = o

    @pl.when(pl.program_id(1) == pl.num_programs(1) - 1)
    def _():
        sfin_ref[0] = s_scr[...]


def _gla_call(P, wa_pad, ba, s0, reverse, fin=None):
    b_, L, _ = P.shape
    T = min(GLA_BLOCK, L)
    nt = L // T
    if reverse:
        tmap = lambda t: nt - 1 - t
    else:
        tmap = lambda t: t
    ii = np.arange(T)
    same = (ii[:, None] // GLA_CHUNK) == (ii[None, :] // GLA_CHUNK)
    tri = same & ((ii[None, :] >= ii[:, None]) if reverse else (ii[None, :] <= ii[:, None]))
    tri = jnp.asarray(tri.astype(np.float32)).astype(BF16)

    def col(name, width):
        blk = COLS[name][0] // width
        return pl.BlockSpec((1, T, width), lambda b, t: (b, tmap(t), blk))

    in_specs = [col('g_q', GLA_KW), col('g_k', GLA_KW), col('g_v', GLA_VW),
                col('g_a', LANES), col('g_a_lo', LANES),
                pl.BlockSpec((LANES, GLA_KW), lambda b, t: (0, 0)),
                pl.BlockSpec((1, GLA_KW), lambda b, t: (0, 0)),
                pl.BlockSpec((1, GLA_DV, GLA_KW), lambda b, t: (b, 0, 0)),
                pl.BlockSpec((T, T), lambda b, t: (0, 0))]
    args = [P, P, P, P, P, wa_pad, ba, s0, tri]
    if fin is not None:
        o_prev, nw = fin
        in_specs += [pl.BlockSpec((1, T, GLA_VW), lambda b, t: (b, tmap(t), 0)),
                     col('g_z', GLA_VW),
                     pl.BlockSpec((1, GLA_DV), lambda b, t: (0, 0))]
        args += [o_prev, P, nw]
    out_dtype = BF16 if fin is not None else F32
    return pl.pallas_call(
        functools.partial(_gla_kernel, reverse=reverse, finalize=fin is not None, nchunks=T // GLA_CHUNK),
        grid=(b_, nt),
        in_specs=in_specs,
        out_specs=[pl.BlockSpec((1, T, GLA_VW), lambda b, t: (b, tmap(t), 0)),
                   pl.BlockSpec((1, GLA_DV, GLA_KW), lambda b, t: (b, 0, 0))],
        out_shape=[jax.ShapeDtypeStruct((b_, L, GLA_VW), out_dtype),
                   jax.ShapeDtypeStruct((b_, GLA_DV, GLA_KW), F32)],
        scratch_shapes=[pltpu.VMEM((GLA_DV, GLA_KW), F32)],
        compiler_params=_cparams("arbitrary", "arbitrary"),
        name="gla",
    )(*args)


def _norm_rope(a, gain, bd, cos, sin):
    sq = a * a
    sh, sl = _split(sq)
    ss = _dot(sh, bd) + _dot(sl, bd)
    an = a * lax.rsqrt(ss * (1.0 / ATTN_HD) + EPS) * gain
    w = a.shape[-1]
    lane = lax.broadcasted_iota(jnp.int32, an.shape, 1)
    partner = jnp.where(lane % 2 == 0, pltpu.roll(an, w - 1, 1), pltpu.roll(an, 1, 1))
    return an * cos + partner * sin


Q_SCALE = (ATTN_HD ** -0.5) * math.log2(math.e)
ATTN_QK_DEPTH = 4 * ATTN_HD
ATTN_TK_MAX = 640


def _split8(a):
    hi = a.astype(F8).astype(F32)
    lo = (a - hi).astype(F8)
    return hi.astype(BF16), lo.astype(BF16)


def _qkv_prep_kernel(q_ref, k_ref, v_ref, cq_ref, sq_ref, ck_ref, sk_ref, gq_ref, gk_ref,
                     bdq_ref, bdk_ref, qo_ref, ko_ref, vo_ref):
    hd = ATTN_HD
    q = _norm_rope(q_ref[0].astype(F32), gq_ref[...], bdq_ref[...], cq_ref[...], sq_ref[...]) * Q_SCALE
    qh, ql = _split8(q)
    qt = jnp.concatenate([qh, ql], axis=1).T
    for h in range(ATTN_HEADS):
        hi = qt[hd * h:hd * (h + 1)]
        lo = qt[ATTN_QW + hd * h:ATTN_QW + hd * (h + 1)]
        qo_ref[0, ATTN_QK_DEPTH * h:ATTN_QK_DEPTH * (h + 1), :] = jnp.concatenate([hi, hi, lo, lo], axis=0).astype(F8)
    k = _norm_rope(k_ref[0].astype(F32), gk_ref[...], bdk_ref[...], ck_ref[...], sk_ref[...])
    kh, kl = _split8(k)
    for g in range(ATTN_KV_HEADS):
        cols = slice(g * hd, (g + 1) * hd)
        ko_ref[0, g] = jnp.concatenate([kh[:, cols], kl[:, cols], kh[:, cols], kl[:, cols]], axis=1).astype(F8)
    vo_ref[0] = v_ref[0].astype(BF16).T


def _qkv_prep_call(P, cos_q, sin_q, cos_k, sin_k, gq, gk, bdq, bdk):
    b_, L, _ = P.shape
    T = min(1024, L)

    def col(name, width):
        blk = COLS[name][0] // width
        return pl.BlockSpec((1, T, width), lambda b, t: (b, t, blk))

    tab = lambda w: pl.BlockSpec((T, w), lambda b, t: (t, 0))
    const = lambda r, w: pl.BlockSpec((r, w), lambda b, t: (0, 0))
    return pl.pallas_call(
        _qkv_prep_kernel,
        grid=(b_, L // T),
        in_specs=[col('a_q', ATTN_QW), col('a_k', ATTN_KVW), col('a_v', ATTN_KVW),
                  tab(ATTN_QW), tab(ATTN_QW), tab(ATTN_KVW), tab(ATTN_KVW),
                  const(1, ATTN_QW), const(1, ATTN_KVW),
                  const(ATTN_QW, ATTN_QW), const(ATTN_KVW, ATTN_KVW)],
        out_specs=[pl.BlockSpec((1, ATTN_HEADS * ATTN_QK_DEPTH, T), lambda b, t: (b, 0, t)),
                   pl.BlockSpec((1, ATTN_KV_HEADS, T, ATTN_QK_DEPTH), lambda b, t: (b, 0, t, 0)),
                   pl.BlockSpec((1, ATTN_KVW, T), lambda b, t: (b, 0, t))],
        out_shape=[jax.ShapeDtypeStruct((b_, ATTN_HEADS * ATTN_QK_DEPTH, L), F8),
                   jax.ShapeDtypeStruct((b_, ATTN_KV_HEADS, L, ATTN_QK_DEPTH), F8),
                   jax.ShapeDtypeStruct((b_, ATTN_KVW, L), BF16)],
        compiler_params=_cparams("arbitrary", "arbitrary"),
        name="qkv_prep",
    )(P, P, P, cos_q, sin_q, cos_k, sin_k, gq, gk, bdq, bdk)


ATTN_SUM_ROWS = 16
ATTN_UNROLL = 2
ATTN_TQ = 512
ATTN_COLS = 256


def _attn_kernel(qt_ref, k_ref, vt_ref, z_ref, o_ref, m_scr, t_scr, acc_scr, s_scr, *, tk, nk):
    m_scr[...] = jnp.full(m_scr.shape, -jnp.inf, F32)
    acc_scr[...] = jnp.zeros(acc_scr.shape, F32)
    ones = jnp.ones((ATTN_SUM_ROWS, tk), BF16)

    tq = qt_ref.shape[2]
    width = ATTN_COLS if tq % ATTN_COLS == 0 else tq
    parts = [(h, slice(c, c + width)) for h in range(ATTN_GROUP) for c in range(0, tq, width)]

    def score_part(kj, slot, h, c):
        s = _dot(kj, qt_ref[0, h * ATTN_QK_DEPTH:(h + 1) * ATTN_QK_DEPTH, c])
        s_scr[slot, h, :, c] = s
        t_scr[slot, h, :, c] = jnp.max(s, axis=0, keepdims=True)

    def keys(j):
        start = j * tk if isinstance(j, int) else pl.multiple_of(j * tk, tk)
        return k_ref[0, 0, pl.ds(start, tk), :]

    k0 = keys(0)
    for h, c in parts:
        score_part(k0, 0, h, c)

    def step(j, slot):
        kj = keys(min(j + 1, nk - 1) if isinstance(j, int) else jnp.minimum(j + 1, nk - 1))
        start = j * tk if isinstance(j, int) else pl.multiple_of(j * tk, tk)
        vte = jnp.concatenate([vt_ref[0, :, pl.ds(start, tk)], ones], axis=0)
        for h, c in parts:
            score_part(kj, 1 - slot, h, c)
            m_old = m_scr[h, :, c]
            m_new = jnp.maximum(m_old, t_scr[slot, h, :, c])
            alpha = jnp.exp2(m_old - m_new)
            m_scr[h, :, c] = m_new
            p = jnp.exp2(s_scr[slot, h, :, c] - m_new).astype(BF16)
            acc_scr[h, :, c] = alpha * acc_scr[h, :, c] + _dot(vte, p)

    def trip(i, carry):
        for u in range(ATTN_UNROLL):
            step(ATTN_UNROLL * i + u, u % 2)
        return carry

    ntrips = nk // ATTN_UNROLL
    lax.fori_loop(0, ntrips, trip, 0)
    for j in range(ntrips * ATTN_UNROLL, nk):
        step(j, j % 2)
    o_t = jnp.concatenate([acc_scr[h, :ATTN_HD] / acc_scr[h, ATTN_HD:ATTN_HD + 1] for h in range(ATTN_GROUP)],
                          axis=0)
    out = o_t.T
    o_ref[0] = (out * _silu(z_ref[0].astype(F32))).astype(o_ref.dtype)


def _attn_call(qt, k, vt, P):
    b_, _, L = qt.shape
    Lk = k.shape[2]
    tq = min(ATTN_TQ, L)
    tk = max(t for t in range(LANES, ATTN_TK_MAX + 1, LANES) if Lk % t == 0)
    gw = ATTN_GROUP * ATTN_HD
    zblk = COLS['a_z'][0] // gw
    return pl.pallas_call(
        functools.partial(_attn_kernel, tk=tk, nk=Lk // tk),
        grid=(b_, ATTN_KV_HEADS, L // tq),
        in_specs=[pl.BlockSpec((1, ATTN_GROUP * ATTN_QK_DEPTH, tq), lambda b, g, i: (b, g, i)),
                  pl.BlockSpec((1, 1, Lk, ATTN_QK_DEPTH), lambda b, g, i: (b, g, 0, 0)),
                  pl.BlockSpec((1, ATTN_HD, Lk), lambda b, g, i: (b, g, 0)),
                  pl.BlockSpec((1, tq, gw), lambda b, g, i: (b, i, zblk + g))],
        out_specs=pl.BlockSpec((1, tq, gw), lambda b, g, i: (b, i, g)),
        out_shape=jax.ShapeDtypeStruct((b_, L, ATTN_QW), BF16),
        scratch_shapes=[pltpu.VMEM((ATTN_GROUP, 1, tq), F32),
                        pltpu.VMEM((2, ATTN_GROUP, 1, tq), F32),
                        pltpu.VMEM((ATTN_GROUP, ATTN_HD + ATTN_SUM_ROWS, tq), F32),
                        pltpu.VMEM((2, ATTN_GROUP, tk, tq), F32)],
        compiler_params=_cparams("arbitrary", "arbitrary", "arbitrary"),
        name="attn",
    )(qt, k, vt, P)


def _short_conv_kernel(u_ref, up_ref, un_ref, w_ref, b_ref, v_ref, x1_ref, x2_ref):
    t = pl.program_id(1)
    nt = pl.num_programs(1)
    u = u_ref[0].astype(F32)
    T = u.shape[0]
    hr = up_ref.shape[1]
    prev_row = jnp.where(t > 0, up_ref[0, hr - 1:hr, :].astype(F32), 0.0)
    next_row = jnp.where(t < nt - 1, un_ref[0, 0:1, :].astype(F32), 0.0)
    row = lax.broadcasted_iota(jnp.int32, u.shape, 0)
    u_m1 = jnp.where(row == 0, prev_row, pltpu.roll(u, 1, 0))
    u_p1 = jnp.where(row == T - 1, next_row, pltpu.roll(u, T - 1, 0))
    w = w_ref[...]
    out = b_ref[...] + u_m1 * w[0:1] + u * w[1:2] + u_p1 * w[2:3]
    v_ref[0] = out[:, 0:HY_W]
    x1_ref[0] = out[:, HY_W:2 * HY_W]
    x2_ref[0] = out[:, 2 * HY_W:3 * HY_W]


def _short_conv_call(P, w, bias):
    b_, L, _ = P.shape
    T = min(1024, L)
    cw = (HY_ORDER + 1) * HY_W
    blk = COLS['y_u'][0] // cw
    hr = 16
    hb = T // hr
    nh = L // hr
    out_spec = pl.BlockSpec((1, T, HY_W), lambda b, t: (b, t, 0))
    shp = jax.ShapeDtypeStruct((b_, L, HY_W), F32)
    return pl.pallas_call(
        _short_conv_kernel,
        grid=(b_, L // T),
        in_specs=[pl.BlockSpec((1, T, cw), lambda b, t: (b, t, blk)),
                  pl.BlockSpec((1, hr, cw), lambda b, t: (b, jnp.maximum(t * hb - 1, 0), blk)),
                  pl.BlockSpec((1, hr, cw), lambda b, t: (b, jnp.minimum((t + 1) * hb, nh - 1), blk)),
                  pl.BlockSpec((HY_SHORT, cw), lambda b, t: (0, 0)),
                  pl.BlockSpec((1, cw), lambda b, t: (0, 0))],
        out_specs=[out_spec, out_spec, out_spec],
        out_shape=[shp, shp, shp],
        compiler_params=_cparams("arbitrary", "arbitrary"),
        name="short_conv",
    )(P, P, P, w, bias)


FFT_ROWS = 8


def _filter_kernel(emb_ref, w1_ref, b1_ref, f1_ref, w2_ref, b2_ref, f2_ref, w3_ref, dl_ref,
                   hf_ref, hb_ref, sum_ref):
    t = pl.program_id(0)
    emb = emb_ref[...]
    h = jnp.sin(f1_ref[...] * (_dot_hp(emb, w1_ref[...]) + b1_ref[...]))
    h = jnp.sin(f2_ref[...] * (_dot_hp(h, w2_ref[...]) + b2_ref[...]))
    hh, hl = _split(h)
    tt = emb[:, 0:1]

    @pl.when(t == 0)
    def _():
        sum_ref[...] = jnp.zeros_like(sum_ref)

    lag0 = (lax.broadcasted_iota(jnp.int32, (h.shape[0], HY_W), 0) + t * h.shape[0]) == 0
    for o in range(HY_ORDER):
        for d, out_ref in enumerate((hf_ref, hb_ref)):
            cols = slice((2 * o + d) * HY_W, (2 * o + d + 1) * HY_W)
            wh, wl = _split(w3_ref[:, cols])
            f = _dot(hh, wh) + _dot(hl, wh) + _dot(hh, wl)
            f = f * (jnp.exp(-tt * dl_ref[:, cols]) + HY_MOD_SHIFT)
            sum_ref[:, cols] += jnp.broadcast_to(jnp.sum(jnp.abs(f), axis=0, keepdims=True), (8, HY_W))
            if d == 1:
                f = jnp.where(lag0, 0.0, f)
            out_ref[o] = pltpu.einshape("(an)c->a(nc)", f, a=FFT_ROWS)


def _filter_call(emb, w1, b1, f1, w2, b2, f2, w3, deltas, n1, n2):
    L = emb.shape[0]
    TL = FFT_ROWS * n2
    const = lambda r, w: pl.BlockSpec((r, w), lambda t: (0, 0))
    hspec = pl.BlockSpec((HY_ORDER, FFT_ROWS, n2 * HY_W), lambda t: (0, t, 0))
    hshape = jax.ShapeDtypeStruct((HY_ORDER, n1 // 2, n2 * HY_W), F32)
    return pl.pallas_call(
        _filter_kernel,
        grid=(L // TL,),
        in_specs=[pl.BlockSpec((TL, LANES), lambda t: (t, 0)),
                  const(LANES, HY_FFN), const(1, HY_FFN), const(1, HY_FFN),
                  const(HY_FFN, HY_FFN), const(1, HY_FFN), const(1, HY_FFN),
                  const(HY_FFN, N_FILT), const(1, N_FILT)],
        out_specs=[hspec, hspec, pl.BlockSpec((8, N_FILT), lambda t: (0, 0))],
        out_shape=[hshape, hshape, jax.ShapeDtypeStruct((8, N_FILT), F32)],
        compiler_params=_cparams("arbitrary"),
        name="hy_filter",
    )(emb, w1, b1, f1, w2, b2, f2, w3, deltas)


def _fft_sizes(n):
    lg = int(round(math.log2(n)))
    assert 1 << lg == n
    n1 = 1 << ((lg + 1) // 2)
    return n1, n // n1


def _bf16_pair(a):
    a = np.asarray(a, np.float32)
    hi = jnp.asarray(a, F32).astype(BF16)
    lo = (jnp.asarray(a, F32) - hi.astype(F32)).astype(BF16)
    return hi, lo


def _fft_tables(n1, n2):
    n = n1 * n2
    h1 = n1 // 2
    k1 = np.arange(n1)[:, None]
    a = 2.0 * np.pi * ((k1 * np.arange(h1)[None, :]) % n1) / n1
    c, s = np.cos(a), np.sin(a)
    fa = np.block([[c, s], [-s, c]])
    fa_real = np.concatenate([c, -s], axis=0)
    fd = np.block([[c.T, -s.T], [s.T, c.T]])
    k2 = np.arange(n2)[:, None]
    b = 2.0 * np.pi * ((k2 * np.arange(n2)[None, :]) % n2) / n2
    cb, sb = np.cos(b), np.sin(b)
    fb = np.block([[cb, sb], [-sb, cb]])
    fc = np.block([[cb, -sb], [sb, cb]])
    kn = (jnp.arange(n1, dtype=jnp.int32)[:, None] * jnp.arange(n2, dtype=jnp.int32)[None, :]) % n
    tw = kn.astype(F32) * (2.0 * math.pi / n)
    twr = jnp.broadcast_to(jnp.cos(tw)[:, :, None], (n1, n2, LANES))
    twi = jnp.broadcast_to(-jnp.sin(tw)[:, :, None], (n1, n2, LANES))
    return dict(fa=_bf16_pair(fa), fa_real=_bf16_pair(fa_real), fd=_bf16_pair(fd),
                fb=_bf16_pair(fb), fc=_bf16_pair(fc), twr=twr, twi=twi)


def _fa_kernel(x_ref, mh_ref, ml_ref, o_ref, xh_scr, *, packed, part_axis):
    if packed:
        @pl.when(pl.program_id(part_axis) == 0)
        def _():
            x = x_ref[...]
            x = x.reshape(x.shape[0] * x.shape[1], x.shape[2], x.shape[3])
            xh_scr[...] = pltpu.einshape("rnc->r(nc)", x).astype(BF16)

        xh = xh_scr[...]
    else:
        xh = x_ref[0].astype(BF16)
    r = _dot(mh_ref[...], xh) + _dot(ml_ref[...], xh)
    o_ref[0] = pltpu.einshape("r(nc)->rnc", r, n=FFT_ROWS)


def _fa_call(x4, mats, n1):
    mh, ml = mats
    _, h1, n2, wd = x4.shape
    wt = FFT_ROWS * wd
    return pl.pallas_call(
        functools.partial(_fa_kernel, packed=True, part_axis=1),
        grid=(n2 // FFT_ROWS, 2),
        in_specs=[pl.BlockSpec((2, h1, FFT_ROWS, wd), lambda j, p: (0, 0, j, 0)),
                  pl.BlockSpec((n1, n1), lambda j, p: (p, 0)),
                  pl.BlockSpec((n1, n1), lambda j, p: (p, 0))],
        out_specs=pl.BlockSpec((1, n1, FFT_ROWS, wd), lambda j, p: (p, 0, j, 0)),
        out_shape=jax.ShapeDtypeStruct((2, n1, n2, wd), F32),
        scratch_shapes=[pltpu.VMEM((n1, wt), BF16)],
        compiler_params=_cparams("arbitrary", "arbitrary"),
        name="fft_a",
    )(x4, mh, ml)


def _fa_real_call(x, mats, n1):
    mh, ml = mats
    ng, h1, lanes = x.shape
    wt = FFT_ROWS * HY_W
    n2 = lanes // HY_W
    out = pl.pallas_call(
        functools.partial(_fa_kernel, packed=False, part_axis=2),
        grid=(ng, lanes // wt, 2),
        in_specs=[pl.BlockSpec((1, h1, wt), lambda g, j, p: (g, 0, j)),
                  pl.BlockSpec((n1, h1), lambda g, j, p: (p, 0)),
                  pl.BlockSpec((n1, h1), lambda g, j, p: (p, 0))],
        out_specs=pl.BlockSpec((1, n1, FFT_ROWS, HY_W), lambda g, j, p: (2 * g + p, 0, j, 0)),
        out_shape=jax.ShapeDtypeStruct((2 * ng, n1, n2, HY_W), F32),
        scratch_shapes=[pltpu.VMEM((h1, wt), BF16)],
        compiler_params=_cparams("arbitrary", "arbitrary", "arbitrary"),
        name="fft_a_real",
    )(x, mh, ml)
    return out.reshape(ng, 2, n1, n2, HY_W)


def _lane_tile(a, width):
    return jnp.concatenate([a] * (width // a.shape[-1]), axis=-1)


def _fb_filter_kernel(tf_ref, tb_ref, twr_ref, twi_ref, fh_ref, fl_ref, sf_ref, sb_ref, g_ref, *, n2, scale):
    w = g_ref.shape[-1]
    inv = jnp.concatenate([scale / sf_ref[0:1, :], scale / sb_ref[0:1, :]], axis=1)

    def body(pair, carry):
        ks = (2 * pair, 2 * pair + 1)
        ps = []
        for k in ks:
            twr = _lane_tile(twr_ref[k], 2 * w)
            twi = _lane_tile(twi_ref[k], 2 * w)
            tr = jnp.concatenate([tf_ref[0, 0, k], tb_ref[0, 0, k]], axis=1)
            ti = jnp.concatenate([tf_ref[0, 1, k], tb_ref[0, 1, k]], axis=1)
            ps.append(jnp.concatenate([tr * twr - ti * twi, tr * twi + ti * twr], axis=0))
        zs = [_dot_mat(fh_ref[...], fl_ref[...], p) * inv for p in ps]
        for k, z in zip(ks, zs):
            g_ref[0, 0, k] = z[:n2, :w] + z[:n2, w:]
            g_ref[0, 1, k] = z[n2:, :w] - z[n2:, w:]
        return carry

    lax.fori_loop(0, FFT_ROWS // 2, body, 0, unroll=2)


def _fb_filter_call(thf, thb, sums, tabs, n1, n2):
    fh, fl = tabs['fb']
    tspec = pl.BlockSpec((1, 2, FFT_ROWS, n2, HY_W), lambda g, i: (g, 0, i, 0, 0))
    twspec = pl.BlockSpec((FFT_ROWS, n2, LANES), lambda g, i: (i, 0, 0))
    sspec = lambda d: pl.BlockSpec((8, HY_W), lambda g, i: (0, 2 * g + d))
    return pl.pallas_call(
        functools.partial(_fb_filter_kernel, n2=n2, scale=1.0 / (n1 * n2)),
        grid=(HY_ORDER, n1 // FFT_ROWS),
        in_specs=[tspec, tspec, twspec, twspec,
                  pl.BlockSpec(fh.shape, lambda g, i: (0, 0)), pl.BlockSpec(fl.shape, lambda g, i: (0, 0)),
                  sspec(0), sspec(1)],
        out_specs=pl.BlockSpec((1, 2, FFT_ROWS, n2, HY_W), lambda g, i: (g, 0, i, 0, 0)),
        out_shape=jax.ShapeDtypeStruct((HY_ORDER, 2, n1, n2, HY_W), F32),
        compiler_params=_cparams("arbitrary", "arbitrary"),
        name="fft_b_filter",
    )(thf, thb, tabs['twr'], tabs['twi'], fh, fl, sums, sums)


def _fb_kernel(t_ref, twr_ref, twi_ref, fh_ref, fl_ref, ch_ref, cl_ref, g_ref, o_ref, t4_scr, *, n2):
    w = g_ref.shape[-1]

    def body(pair, carry):
        ks = (2 * pair, 2 * pair + 1)
        tw = [(_lane_tile(twr_ref[k], w), _lane_tile(twi_ref[k], w)) for k in ks]
        ps = [jnp.concatenate([t_ref[0, k] * a - t_ref[1, k] * b, t_ref[0, k] * b + t_ref[1, k] * a], axis=0)
              for k, (a, b) in zip(ks, tw)]
        zs = [_dot_mat(fh_ref[...], fl_ref[...], p) for p in ps]
        ys = [jnp.concatenate([z[:n2] * g_ref[0, 0, k] - z[n2:] * g_ref[0, 1, k],
                               z[:n2] * g_ref[0, 1, k] + z[n2:] * g_ref[0, 0, k]], axis=0)
              for k, z in zip(ks, zs)]
        vs = [_dot_mat(ch_ref[...], cl_ref[...], y) for y in ys]
        for k, (a, b), v in zip(ks, tw, vs):
            t4_scr[0, k] = v[:n2] * a + v[n2:] * b
            t4_scr[1, k] = v[n2:] * a - v[:n2] * b
        return carry

    lax.fori_loop(0, FFT_ROWS // 2, body, 0, unroll=2)
    o_ref[...] = pltpu.einshape("pknc->pk(nc)", t4_scr[...])


def _fb_call(t, g, order, tabs, n1, n2):
    fh, fl = tabs['fb']
    ch, cl = tabs['fc']
    lanes = n2 * HY_W
    mat = lambda m: pl.BlockSpec(m.shape, lambda i: (0, 0))
    dat = pl.BlockSpec((2, FFT_ROWS, lanes), lambda i: (0, i, 0))
    twspec = pl.BlockSpec((FFT_ROWS, n2, LANES), lambda i: (i, 0, 0))
    return pl.pallas_call(
        functools.partial(_fb_kernel, n2=n2),
        grid=(n1 // FFT_ROWS,),
        in_specs=[pl.BlockSpec((2, FFT_ROWS, n2, HY_W), lambda i: (0, i, 0, 0)),
                  twspec, twspec, mat(fh), mat(fl), mat(ch), mat(cl),
                  pl.BlockSpec((1, 2, FFT_ROWS, n2, HY_W), lambda i: (order, 0, i, 0, 0))],
        out_specs=dat,
        out_shape=jax.ShapeDtypeStruct((2, n1, lanes), F32),
        scratch_shapes=[pltpu.VMEM((2, FFT_ROWS, n2, HY_W), F32)],
        compiler_params=_cparams("arbitrary"),
        name="fft_b",
    )(t, tabs['twr'], tabs['twi'], fh, fl, ch, cl, g)


def _fd_kernel(u_ref, mh_ref, ml_ref, z_ref, x_ref, sk_ref, o_ref):
    u = u_ref[...]
    u = u.reshape(u.shape[0] * u.shape[1], u.shape[2])
    y = _dot_mat(mh_ref[...], ml_ref[...], u)
    y = pltpu.einshape("r(nc)->rnc", y, n=FFT_ROWS)
    o_ref[0] = x_ref[0] * (y + sk_ref[...] * z_ref[0])


def _fd_call(u, mats, z4, x4, skip_row, n1):
    mh, ml = mats
    b_, h1, n2, wd = z4.shape
    wt = FFT_ROWS * wd
    dat = pl.BlockSpec((1, h1, FFT_ROWS, wd), lambda j, p: (p, 0, j, 0))
    return pl.pallas_call(
        _fd_kernel,
        grid=(n2 // FFT_ROWS, b_),
        in_specs=[pl.BlockSpec((2, n1, wt), lambda j, p: (0, 0, j)),
                  pl.BlockSpec((h1, 2 * n1), lambda j, p: (p, 0)),
                  pl.BlockSpec((h1, 2 * n1), lambda j, p: (p, 0)),
                  dat, dat, pl.BlockSpec((1, wd), lambda j, p: (0, 0))],
        out_specs=dat,
        out_shape=jax.ShapeDtypeStruct(z4.shape, F32),
        compiler_params=pltpu.CompilerParams(dimension_semantics=("arbitrary", "arbitrary"),
                                             vmem_limit_bytes=FFT_D_VMEM_LIMIT),
        name="fft_d",
    )(u, mh, ml, z4, x4, skip_row)


def _hyena_filter_spectrum(L, tabs, n1, n2, fp):
    f32 = F32
    t = jnp.linspace(0.0, 1.0, L, dtype=f32)[:, None]
    w = 2.0 * math.pi * jnp.arange(L, dtype=f32)[:, None] / L
    fr = jnp.linspace(1e-4, HY_BANDS - 1, HY_BANDS, dtype=f32)[None]
    emb = jnp.concatenate([t, jnp.cos(fr * w), -jnp.sin(fr * w)], axis=-1)
    emb = jnp.pad(emb, ((0, 0), (0, LANES - HY_EMB)))
    deltas = jnp.abs(jnp.linspace(math.log(HY_DECAY_TARGET) / HY_DECAY_SHORT_PCT,
                                  math.log(HY_DECAY_TARGET) / HY_DECAY_LONG_PCT, N_FILT, dtype=f32))[None]
    f1_w, f1_b, f1_freq, f2_w, f2_b, f2_freq, f3_w = fp
    w1 = jnp.pad(f1_w, ((0, LANES - HY_EMB), (0, 0)))
    hf, hb, sums = _filter_call(emb, w1, f1_b[None], f1_freq[None], f2_w, f2_b[None],
                                f2_freq[None], f3_w, deltas, n1, n2)
    thf = _fa_real_call(hf, tabs['fa_real'], n1)
    thb = _fa_real_call(hb, tabs['fa_real'], n1)
    return _fb_filter_call(thf, thb, sums, tabs, n1, n2)


def _hyena_conv(z, xg, g, order, skip, tabs, n1, n2):
    b_, L, wd = z.shape
    assert b_ == 2, "the two batch rows ride as the real and imaginary parts of one transform"
    z4 = z.reshape(b_, n1 // 2, n2, wd)
    x4 = xg.reshape(b_, n1 // 2, n2, wd)
    t = _fa_call(z4, tabs['fa'], n1)
    u = _fb_call(t, g, order, tabs, n1, n2)
    out = _fd_call(u, tabs['fd'], z4, x4, skip[None, :], n1)
    return out.reshape(b_, L, wd)


def _merge_kernel(yg_ref, ya_ref, yh_ref, yz_ref, mg_ref, ma_ref, mh_ref, x_ref, gate_ref,
                  wg_ref, wa_ref, wh_ref, wo_ref, fn_ref, o_ref, *, final):
    yh = (yh_ref[0] * _silu(yz_ref[0].astype(F32))).astype(BF16)
    y = (_sigmoid(mg_ref[0].astype(F32)) * _dot(yg_ref[0], wg_ref[...])
         + _sigmoid(ma_ref[0].astype(F32)) * _dot(ya_ref[0], wa_ref[...])
         + _sigmoid(mh_ref[0].astype(F32)) * _dot(yh, wh_ref[...]))
    out = _dot(y.astype(BF16), wo_ref[...])
    xn = x_ref[0] + gate_ref[0] * out
    if final:
        ms = jnp.mean(xn * xn, axis=-1, keepdims=True)
        xn = xn * lax.rsqrt(ms + EPS) * fn_ref[...]
    o_ref[0] = xn


def _merge_call(y_gla, y_attn, y_hy, P, x, gate, wg, wa, wh, wo, fnorm, final):
    b_, L, d = x.shape
    T = min(1024, L)
    row = lambda w, blk=0: pl.BlockSpec((1, T, w), lambda b, t: (b, t, blk))
    const = lambda r, w: pl.BlockSpec((r, w), lambda b, t: (0, 0))
    mblk = COLS['m'][0] // d
    return pl.pallas_call(
        functools.partial(_merge_kernel, final=final),
        grid=(b_, L // T),
        in_specs=[row(BRANCH_W), row(BRANCH_W), row(BRANCH_W), row(HY_W, COLS['y_z'][0] // HY_W),
                  row(d, mblk), row(d, mblk + 1), row(d, mblk + 2), row(d),
                  pl.BlockSpec((1, 1, d), lambda b, t: (b, 0, 0)),
                  const(BRANCH_W, d), const(BRANCH_W, d), const(BRANCH_W, d), const(d, d), const(1, d)],
        out_specs=row(d),
        out_shape=jax.ShapeDtypeStruct((b_, L, d), F32),
        compiler_params=_cparams("arbitrary", "arbitrary"),
        name="merge",
    )(y_gla, y_attn, y_hy, P, P, P, P, x, gate, wg, wa, wh, wo, fnorm)


def _pack_w_in(w):
    parts = dict(zip(('g_q', 'g_k', 'g_v', 'g_z', 'g_af', 'g_ab', 'a_q', 'a_k', 'a_v', 'a_z', 'y_u', 'y_z', 'm'),
                     jnp.split(w, [int(i) for i in np.cumsum(SPLITS)[:-1]], axis=-1)))
    parts['g_a'] = jnp.pad(jnp.concatenate([parts['g_af'], parts['g_ab']], axis=-1),
                           ((0, 0), (0, LANES - 2 * GLA_RANK)))
    parts['g_a_lo'] = jnp.zeros_like(parts['g_a'])
    order = sorted(COLS, key=lambda n: COLS[n][0])
    return jnp.concatenate([parts[n] for n in order], axis=-1).astype(BF16)


def _rope_tables(L, heads):
    t = jnp.arange(L)
    row = (t // GRID_W).astype(F32)
    colp = (t % GRID_W).astype(F32)
    half = ATTN_HD // 2
    inv = ROPE_THETA ** (-jnp.arange(0, half, 2, dtype=F32) / half)
    ang = jnp.concatenate([row[:, None] * inv, colp[:, None] * inv], axis=-1)
    cos = jnp.repeat(jnp.cos(ang), 2, axis=-1)
    sin = jnp.stack([-jnp.sin(ang), jnp.sin(ang)], axis=-1).reshape(L, ATTN_HD)
    return jnp.tile(cos, (1, heads)), jnp.tile(sin, (1, heads))


def _identity_rope(L, heads):
    return jnp.ones((L, heads * ATTN_HD), F32), jnp.zeros((L, heads * ATTN_HD), F32)


def _block_diag_ones(width):
    i = np.arange(width) // ATTN_HD
    return jnp.asarray((i[:, None] == i[None, :]).astype(np.float32)).astype(BF16)


def kernel(x, c, ctx, c_ctx, w_ada, b_ada, w_in, gla_wa_f, gla_ba_f, gla_wa_b, gla_ba_b, gla_norm,
           attn_qnorm, attn_knorm, hy_conv_w, hy_conv_b, hy_f1_w, hy_f1_b, hy_f1_freq, hy_f2_w,
           hy_f2_b, hy_f2_freq, hy_f3_w, hy_skip, w_br_gla, w_br_attn, w_br_hy, w_out, final_norm):
    b_, L, d = x.shape
    Lc = ctx.shape[1]
    depth = w_ada.shape[0]

    cos_q, sin_q = _rope_tables(L, ATTN_HEADS)
    cos_k, sin_k = cos_q[:, :ATTN_KVW], sin_q[:, :ATTN_KVW]
    cos_qc, sin_qc = _identity_rope(Lc, ATTN_HEADS)
    cos_kc, sin_kc = cos_qc[:, :ATTN_KVW], sin_qc[:, :ATTN_KVW]
    bdq, bdk = _block_diag_ones(ATTN_QW), _block_diag_ones(ATTN_KVW)
    n1, n2 = _fft_sizes(2 * L)
    n1c, n2c = _fft_sizes(2 * Lc)
    tabs = _fft_tables(n1, n2)
    tabs_c = _fft_tables(n1c, n2c)
    zero_state = jnp.zeros((b_, GLA_DV, GLA_KW), F32)

    cmat = jnp.concatenate([c, c_ctx[None], jnp.zeros((8 - b_ - 1, d), F32)], axis=0)
    mods = _ada_call(cmat, w_ada, b_ada)

    for l in range(depth):
        need_ctx = l < depth - 1
        shift, scale, gate = [m[:b_, None, :] for m in jnp.split(mods[l], 3, axis=-1)]
        shift_c, scale_c, gate_c = [jnp.broadcast_to(m[b_:b_ + 1, None, :], (b_, 1, d))
                                    for m in jnp.split(mods[l], 3, axis=-1)]
        w_packed = _pack_w_in(w_in[l])
        P = _proj_call(x, scale, shift, w_packed)
        Pc = _proj_call(ctx, scale_c, shift_c, w_packed)

        rk = GLA_RANK
        wa_f = jnp.zeros((LANES, GLA_KW), F32).at[0:rk].set(gla_wa_f[l])
        wa_b = jnp.zeros((LANES, GLA_KW), F32).at[rk:2 * rk].set(gla_wa_b[l])
        ba_f, ba_b = gla_ba_f[l][None], gla_ba_b[l][None]
        nw = gla_norm[l][None]
        oc_f, sc_f = _gla_call(Pc, wa_f, ba_f, zero_state, reverse=False)
        yc_gla, sc_b = _gla_call(Pc, wa_b, ba_b, zero_state, reverse=True, fin=(oc_f, nw))
        o_f, _ = _gla_call(P, wa_f, ba_f, sc_f, reverse=False)
        y_gla, _ = _gla_call(P, wa_b, ba_b, sc_b, reverse=True, fin=(o_f, nw))

        gq = jnp.tile(attn_qnorm[l], ATTN_HEADS)[None]
        gk = jnp.tile(attn_knorm[l], ATTN_KV_HEADS)[None]
        q_a, k_a, v_a = _qkv_prep_call(P, cos_q, sin_q, cos_k, sin_k, gq, gk, bdq, bdk)
        qc_a, kc_a, vc_a = _qkv_prep_call(Pc, cos_qc, sin_qc, cos_kc, sin_kc, gq, gk, bdq, bdk)
        y_attn = _attn_call(q_a, jnp.concatenate([k_a, kc_a], axis=2),
                            jnp.concatenate([v_a, vc_a], axis=2), P)

        fp = (hy_f1_w[l], hy_f1_b[l], hy_f1_freq[l], hy_f2_w[l], hy_f2_b[l], hy_f2_freq[l], hy_f3_w[l])
        g_spec = _hyena_filter_spectrum(L, tabs, n1, n2, fp)
        v0, x1, x2 = _short_conv_call(P, hy_conv_w[l], hy_conv_b[l][None])
        z1 = _hyena_conv(v0, x1, g_spec, 0, hy_skip[l, 0], tabs, n1, n2)
        y_hy = _hyena_conv(z1, x2, g_spec, 1, hy_skip[l, 1], tabs, n1, n2)

        wg, wa, wh, wo = (w_br_gla[l].astype(BF16), w_br_attn[l].astype(BF16),
                          w_br_hy[l].astype(BF16), w_out[l].astype(BF16))
        fn = final_norm[None]
        x_new = _merge_call(y_gla, y_attn, y_hy, P, x, gate, wg, wa, wh, wo, fn, final=not need_ctx)

        if need_ctx:
            yc_attn = _attn_call(qc_a, kc_a, vc_a, Pc)
            gc_spec = _hyena_filter_spectrum(Lc, tabs_c, n1c, n2c, fp)
            vc0, xc1, xc2 = _short_conv_call(Pc, hy_conv_w[l], hy_conv_b[l][None])
            zc1 = _hyena_conv(vc0, xc1, gc_spec, 0, hy_skip[l, 0], tabs_c, n1c, n2c)
            yc_hy = _hyena_conv(zc1, xc2, gc_spec, 1, hy_skip[l, 1], tabs_c, n1c, n2c)
            ctx = _merge_call(yc_gla, yc_attn, yc_hy, Pc, ctx, gate_c, wg, wa, wh, wo, fn, final=False)
        x = x_new

    return x
```

```python
import functools
import math

import numpy as np
import jax
import jax.numpy as jnp
from jax import lax
from jax.experimental import pallas as pl
from jax.experimental.pallas import tpu as pltpu

F32 = jnp.float32
BF16 = jnp.bfloat16
F8 = jnp.float8_e4m3fn

D_MODEL = 1024
GRID_W = 64
BRANCH_W = D_MODEL // 2
N_BRANCH = 3
EPS = 1e-6
GLA_HEADS = 4
GLA_DV = BRANCH_W // GLA_HEADS
GLA_DK = GLA_DV // 2
GLA_KW = GLA_HEADS * GLA_DK
GLA_VW = GLA_HEADS * GLA_DV
GLA_RANK = 16
GLA_GATE_NORM = 16.0
GLA_CHUNK = 64
GLA_BLOCK = 256
ATTN_HD = 64
ATTN_HEADS = BRANCH_W // ATTN_HD
ATTN_KV_HEADS = ATTN_HEADS // 4
ATTN_GROUP = ATTN_HEADS // ATTN_KV_HEADS
ATTN_QW = ATTN_HEADS * ATTN_HD
ATTN_KVW = ATTN_KV_HEADS * ATTN_HD
ROPE_THETA = 10000.0
HY_W = BRANCH_W
HY_ORDER = 2
HY_EMB = 33
HY_BANDS = (HY_EMB - 1) // 2
HY_FFN = 64
HY_SHORT = 3
HY_MOD_SHIFT = 0.05
HY_DECAY_SHORT_PCT = 0.3
HY_DECAY_LONG_PCT = 1.5
HY_DECAY_TARGET = 1e-2
N_FILT = 2 * HY_ORDER * HY_W
SPLITS = (GLA_KW, GLA_KW, GLA_VW, GLA_VW, GLA_RANK, GLA_RANK,
          ATTN_QW, ATTN_KVW, ATTN_KVW, ATTN_QW,
          (HY_ORDER + 1) * HY_W, HY_W,
          N_BRANCH * D_MODEL)

LANES = 128
VMEM_LIMIT = 48 * 1024 * 1024
FFT_D_VMEM_LIMIT = 56 * 1024 * 1024

COLS = {
    'm': (0, 3072), 'y_u': (3072, 1536), 'g_v': (4608, 512), 'g_z': (5120, 512),
    'a_q': (5632, 512), 'a_z': (6144, 512), 'y_z': (6656, 512), 'g_q': (7168, 256),
    'g_k': (7424, 256), 'a_k': (7680, 128), 'a_v': (7808, 128), 'g_a': (7936, 128),
    'g_a_lo': (8064, 128),
}
N_PROJ = 8192
PROJ_TN = 2048


def _cparams(*sem):
    return pltpu.CompilerParams(dimension_semantics=sem, vmem_limit_bytes=VMEM_LIMIT)


def _split(a):
    hi = a.astype(BF16)
    lo = (a - hi.astype(F32)).astype(BF16)
    return hi, lo


def _dot(a, b):
    return jnp.dot(a, b, preferred_element_type=F32)


def _dot_hp(a, b):
    ah, al = _split(a)
    bh, bl = _split(b)
    return _dot(ah, bh) + _dot(al, bh) + _dot(ah, bl)


def _dot_mat(mh, ml, x):
    xh = x.astype(BF16)
    return _dot(mh, xh) + _dot(ml, xh)


def _sigmoid(x):
    return 1.0 / (1.0 + jnp.exp(-x))


def _silu(x):
    return x * _sigmoid(x)


def _ada_kernel(c_ref, w_ref, b_ref, o_ref):
    o_ref[0] = _dot_hp(_silu(c_ref[...]), w_ref[0]) + b_ref[0]


def _ada_call(cmat, w_ada, b_ada):
    depth, d, n3 = w_ada.shape
    tn = 1024
    return pl.pallas_call(
        _ada_kernel,
        grid=(depth, n3 // tn),
        in_specs=[pl.BlockSpec((8, d), lambda l, j: (0, 0)),
                  pl.BlockSpec((1, d, tn), lambda l, j: (l, 0, j)),
                  pl.BlockSpec((1, 1, tn), lambda l, j: (l, 0, j))],
        out_specs=pl.BlockSpec((1, 8, tn), lambda l, j: (l, 0, j)),
        out_shape=jax.ShapeDtypeStruct((depth, 8, n3), F32),
        compiler_params=_cparams("arbitrary", "arbitrary"),
        name="ada",
    )(cmat, w_ada, b_ada.reshape(depth, 1, n3))


def _proj_kernel(x_ref, sc_ref, sh_ref, w_ref, o_ref, *, res_tile, hi_off, lo_off):
    x = x_ref[0]
    ms = jnp.mean(x * x, axis=-1, keepdims=True)
    h = x * lax.rsqrt(ms + EPS) * (1.0 + sc_ref[0]) + sh_ref[0]
    res = _dot(h.astype(BF16), w_ref[...])
    out = res.astype(o_ref.dtype)
    o_ref[0] = out

    @pl.when(pl.program_id(0) == res_tile)
    def _():
        hi = slice(hi_off, hi_off + LANES)
        o_ref[0, :, lo_off:lo_off + LANES] = (res[:, hi] - out[:, hi].astype(F32)).astype(o_ref.dtype)


def _proj_call(x, scale, shift, w):
    b_, L, d = x.shape
    n = w.shape[1]
    tm = min(1024, L)
    tn = PROJ_TN
    hi0, lo0 = COLS['g_a'][0], COLS['g_a_lo'][0]
    assert hi0 // tn == lo0 // tn
    return pl.pallas_call(
        functools.partial(_proj_kernel, res_tile=hi0 // tn, hi_off=hi0 % tn, lo_off=lo0 % tn),
        grid=(n // tn, b_, L // tm),
        in_specs=[pl.BlockSpec((1, tm, d), lambda j, b, i: (b, i, 0)),
                  pl.BlockSpec((1, 1, d), lambda j, b, i: (b, 0, 0)),
                  pl.BlockSpec((1, 1, d), lambda j, b, i: (b, 0, 0)),
                  pl.BlockSpec((d, tn), lambda j, b, i: (0, j))],
        out_specs=pl.BlockSpec((1, tm, tn), lambda j, b, i: (b, i, j)),
        out_shape=jax.ShapeDtypeStruct((b_, L, n), BF16),
        compiler_params=_cparams("arbitrary", "arbitrary", "arbitrary"),
        name="proj",
    )(x, scale, shift, w)


def _gla_kernel(*refs, reverse, finalize, nchunks):
    if finalize:
        (q_ref, k_ref, v_ref, ga_ref, gal_ref, wa_ref, ba_ref, s0_ref, tri_ref, op_ref, z_ref, nw_ref,
         o_ref, sfin_ref, s_scr) = refs
    else:
        (q_ref, k_ref, v_ref, ga_ref, gal_ref, wa_ref, ba_ref, s0_ref, tri_ref,
         o_ref, sfin_ref, s_scr) = refs
    C = GLA_CHUNK
    T = nchunks * C
    nt_dims = (((1,), (1,)), ((), ()))
    tn_dims = (((0,), (0,)), ((), ()))

    @pl.when(pl.program_id(1) == 0)
    def _():
        s_scr[...] = s0_ref[0]

    ga = ga_ref[0].astype(F32) + gal_ref[0].astype(F32)
    xg = _dot_hp(ga, wa_ref[...]) + ba_ref[...]
    g = (jnp.minimum(xg, 0.0) - jnp.log(1.0 + jnp.exp(-jnp.abs(xg)))) * (1.0 / GLA_GATE_NORM)

    tri = tri_ref[...]
    g1 = g.astype(BF16)
    r1 = g - g1.astype(F32)
    g2 = r1.astype(BF16)
    g3 = (r1 - g2.astype(F32)).astype(BF16)
    b = _dot(tri, g1) + _dot(tri, g2) + _dot(tri, g3)

    def chunk_row(r):
        return jnp.concatenate([jnp.broadcast_to(b[c * C + r:c * C + r + 1], (C, GLA_KW))
                                for c in range(nchunks)], axis=0)

    tot_row = 0 if reverse else C - 1
    bm = chunk_row(C // 2)
    bt = chunk_row(tot_row)
    ri = lax.broadcasted_iota(jnp.int32, (T, T), 0)
    ci = lax.broadcasted_iota(jnp.int32, (T, T), 1)
    same_chunk = (ri // C) == (ci // C)
    mask = same_chunk & ((ci >= ri) if reverse else (ci <= ri))
    lane = lax.broadcasted_iota(jnp.int32, (1, GLA_KW), 1)
    hmask = [(lane >= h * GLA_DK) & (lane < (h + 1) * GLA_DK) for h in range(GLA_HEADS)]

    q = q_ref[0].astype(F32) * (GLA_DK ** -0.5)
    k = k_ref[0].astype(F32)
    v = v_ref[0].astype(BF16)
    qa = q * jnp.exp(b - bm)
    kb = (k * jnp.exp(bm - b)).astype(BF16)
    qe = q * jnp.exp(b)
    kd = (k * jnp.exp(bt - b)).astype(BF16)

    intra = []
    for h in range(GLA_HEADS):
        qa_h = jnp.where(hmask[h], qa, 0.0).astype(BF16)
        att = lax.dot_general(qa_h, kb, nt_dims, preferred_element_type=F32)
        att = jnp.where(mask, att, 0.0).astype(BF16)
        intra.append(_dot(att, v[:, h * GLA_DV:(h + 1) * GLA_DV]))

    order = range(nchunks - 1, -1, -1) if reverse else range(nchunks)
    upds = {}
    for c in order:
        rows = slice(c * C, (c + 1) * C)
        upd = None
        for h in range(GLA_HEADS):
            u_h = lax.dot_general(v[rows, h * GLA_DV:(h + 1) * GLA_DV], kd[rows], tn_dims,
                                  preferred_element_type=F32)
            u_h = jnp.where(hmask[h], u_h, 0.0)
            upd = u_h if upd is None else upd + u_h
        upds[c] = upd
    inter = [None] * nchunks
    for c in order:
        rows = slice(c * C, (c + 1) * C)
        st = s_scr[...]
        st_b = st.astype(BF16)
        dec = jnp.exp(b[c * C + tot_row:c * C + tot_row + 1])
        outs = []
        for h in range(GLA_HEADS):
            qe_h = jnp.where(hmask[h], qe[rows], 0.0).astype(BF16)
            outs.append(lax.dot_general(qe_h, st_b, nt_dims, preferred_element_type=F32))
        s_scr[...] = dec * st + upds[c]
        inter[c] = jnp.concatenate(outs, axis=1)
    o = jnp.concatenate(intra, axis=1) + jnp.concatenate(inter, axis=0)

    if finalize:
        o = o + op_ref[0]
        parts = []
        for h in range(GLA_HEADS):
            oh = o[:, h * GLA_DV:(h + 1) * GLA_DV]
            ms = jnp.mean(oh * oh, axis=-1, keepdims=True)
            parts.append(oh * lax.rsqrt(ms + EPS) * nw_ref[...])
        y = jnp.concatenate(parts, axis=1) * _silu(z_ref[0].astype(F32))
        o_ref[0] = y.astype(o_ref.dtype)
    else:
        o_ref[0] = o

    @pl.when(pl.program_id(1) == pl.num_programs(1) - 1)
    def _():
        sfin_ref[0] = s_scr[...]


def _gla_call(P, wa_pad, ba, s0, reverse, fin=None):
    b_, L, _ = P.shape
    T = min(GLA_BLOCK, L)
    nt = L // T
    if reverse:
        tmap = lambda t: nt - 1 - t
    else:
        tmap = lambda t: t
    ii = np.arange(T)
    same = (ii[:, None] // GLA_CHUNK) == (ii[None, :] // GLA_CHUNK)
    tri = same & ((ii[None, :] >= ii[:, None]) if reverse else (ii[None, :] <= ii[:, None]))
    tri = jnp.asarray(tri.astype(np.float32)).astype(BF16)

    def col(name, width):
        blk = COLS[name][0] // width
        return pl.BlockSpec((1, T, width), lambda b, t: (b, tmap(t), blk))

    in_specs = [col('g_q', GLA_KW), col('g_k', GLA_KW), col('g_v', GLA_VW),
                col('g_a', LANES), col('g_a_lo', LANES),
                pl.BlockSpec((LANES, GLA_KW), lambda b, t: (0, 0)),
                pl.BlockSpec((1, GLA_KW), lambda b, t: (0, 0)),
                pl.BlockSpec((1, GLA_DV, GLA_KW), lambda b, t: (b, 0, 0)),
                pl.BlockSpec((T, T), lambda b, t: (0, 0))]
    args = [P, P, P, P, P, wa_pad, ba, s0, tri]
    if fin is not None:
        o_prev, nw = fin
        in_specs += [pl.BlockSpec((1, T, GLA_VW), lambda b, t: (b, tmap(t), 0)),
                     col('g_z', GLA_VW),
                     pl.BlockSpec((1, GLA_DV), lambda b, t: (0, 0))]
        args += [o_prev, P, nw]
    out_dtype = BF16 if fin is not None else F32
    return pl.pallas_call(
        functools.partial(_gla_kernel, reverse=reverse, finalize=fin is not None, nchunks=T // GLA_CHUNK),
        grid=(b_, nt),
        in_specs=in_specs,
        out_specs=[pl.BlockSpec((1, T, GLA_VW), lambda b, t: (b, tmap(t), 0)),
                   pl.BlockSpec((1, GLA_DV, GLA_KW), lambda b, t: (b, 0, 0))],
        out_shape=[jax.ShapeDtypeStruct((b_, L, GLA_VW), out_dtype),
                   jax.ShapeDtypeStruct((b_, GLA_DV, GLA_KW), F32)],
        scratch_shapes=[pltpu.VMEM((GLA_DV, GLA_KW), F32)],
        compiler_params=_cparams("arbitrary", "arbitrary"),
        name="gla",
    )(*args)


def _norm_rope(a, gain, bd, cos, sin):
    sq = a * a
    sh, sl = _split(sq)
    ss = _dot(sh, bd) + _dot(sl, bd)
    an = a * lax.rsqrt(ss * (1.0 / ATTN_HD) + EPS) * gain
    w = a.shape[-1]
    lane = lax.broadcasted_iota(jnp.int32, an.shape, 1)
    partner = jnp.where(lane % 2 == 0, pltpu.roll(an, w - 1, 1), pltpu.roll(an, 1, 1))
    return an * cos + partner * sin


Q_SCALE = (ATTN_HD ** -0.5) * math.log2(math.e)
ATTN_QK_DEPTH = 4 * ATTN_HD
ATTN_TK_MAX = 1280


def _split8(a):
    hi = a.astype(F8).astype(F32)
    lo = (a - hi).astype(F8)
    return hi.astype(BF16), lo.astype(BF16)


def _qkv_prep_kernel(q_ref, k_ref, v_ref, cq_ref, sq_ref, ck_ref, sk_ref, gq_ref, gk_ref,
                     bdq_ref, bdk_ref, qo_ref, ko_ref, vo_ref):
    hd = ATTN_HD
    q = _norm_rope(q_ref[0].astype(F32), gq_ref[...], bdq_ref[...], cq_ref[...], sq_ref[...]) * Q_SCALE
    qh, ql = _split8(q)
    qt = jnp.concatenate([qh, ql], axis=1).T
    for h in range(ATTN_HEADS):
        hi = qt[hd * h:hd * (h + 1)]
        lo = qt[ATTN_QW + hd * h:ATTN_QW + hd * (h + 1)]
        qo_ref[0, ATTN_QK_DEPTH * h:ATTN_QK_DEPTH * (h + 1), :] = jnp.concatenate([hi, hi, lo, lo], axis=0).astype(F8)
    k = _norm_rope(k_ref[0].astype(F32), gk_ref[...], bdk_ref[...], ck_ref[...], sk_ref[...])
    kh, kl = _split8(k)
    for g in range(ATTN_KV_HEADS):
        cols = slice(g * hd, (g + 1) * hd)
        ko_ref[0, g] = jnp.concatenate([kh[:, cols], kl[:, cols], kh[:, cols], kl[:, cols]], axis=1).astype(F8)
    vo_ref[0] = v_ref[0].astype(BF16).T


def _qkv_prep_call(P, cos_q, sin_q, cos_k, sin_k, gq, gk, bdq, bdk):
    b_, L, _ = P.shape
    T = min(1024, L)

    def col(name, width):
        blk = COLS[name][0] // width
        return pl.BlockSpec((1, T, width), lambda b, t: (b, t, blk))

    tab = lambda w: pl.BlockSpec((T, w), lambda b, t: (t, 0))
    const = lambda r, w: pl.BlockSpec((r, w), lambda b, t: (0, 0))
    return pl.pallas_call(
        _qkv_prep_kernel,
        grid=(b_, L // T),
        in_specs=[col('a_q', ATTN_QW), col('a_k', ATTN_KVW), col('a_v', ATTN_KVW),
                  tab(ATTN_QW), tab(ATTN_QW), tab(ATTN_KVW), tab(ATTN_KVW),
                  const(1, ATTN_QW), const(1, ATTN_KVW),
                  const(ATTN_QW, ATTN_QW), const(ATTN_KVW, ATTN_KVW)],
        out_specs=[pl.BlockSpec((1, ATTN_HEADS * ATTN_QK_DEPTH, T), lambda b, t: (b, 0, t)),
                   pl.BlockSpec((1, ATTN_KV_HEADS, T, ATTN_QK_DEPTH), lambda b, t: (b, 0, t, 0)),
                   pl.BlockSpec((1, ATTN_KVW, T), lambda b, t: (b, 0, t))],
        out_shape=[jax.ShapeDtypeStruct((b_, ATTN_HEADS * ATTN_QK_DEPTH, L), F8),
                   jax.ShapeDtypeStruct((b_, ATTN_KV_HEADS, L, ATTN_QK_DEPTH), F8),
                   jax.ShapeDtypeStruct((b_, ATTN_KVW, L), BF16)],
        compiler_params=_cparams("arbitrary", "arbitrary"),
        name="qkv_prep",
    )(P, P, P, cos_q, sin_q, cos_k, sin_k, gq, gk, bdq, bdk)


ATTN_SUM_ROWS = 16
ATTN_UNROLL = 2
ATTN_TQ = 256
ATTN_COLS = 256


def _attn_kernel(qt_ref, k_ref, vt_ref, z_ref, o_ref, m_scr, t_scr, acc_scr, s_scr, *, tk, nk):
    m_scr[...] = jnp.full(m_scr.shape, -jnp.inf, F32)
    acc_scr[...] = jnp.zeros(acc_scr.shape, F32)
    ones = jnp.ones((ATTN_SUM_ROWS, tk), BF16)

    tq = qt_ref.shape[2]
    width = ATTN_COLS if tq % ATTN_COLS == 0 else tq
    parts = [(h, slice(c, c + width)) for h in range(ATTN_GROUP) for c in range(0, tq, width)]

    def score_part(kj, slot, h, c):
        s = _dot(kj, qt_ref[0, h * ATTN_QK_DEPTH:(h + 1) * ATTN_QK_DEPTH, c])
        s_scr[slot, h, :, c] = s
        t_scr[slot, h, :, c] = jnp.max(s, axis=0, keepdims=True)

    def keys(j):
        start = j * tk if isinstance(j, int) else pl.multiple_of(j * tk, tk)
        return k_ref[0, 0, pl.ds(start, tk), :]

    k0 = keys(0)
    for h, c in parts:
        score_part(k0, 0, h, c)

    def step(j, slot):
        kj = keys(min(j + 1, nk - 1) if isinstance(j, int) else jnp.minimum(j + 1, nk - 1))
        start = j * tk if isinstance(j, int) else pl.multiple_of(j * tk, tk)
        vte = jnp.concatenate([vt_ref[0, :, pl.ds(start, tk)], ones], axis=0)
        for h, c in parts:
            score_part(kj, 1 - slot, h, c)
            m_old = m_scr[h, :, c]
            m_new = jnp.maximum(m_old, t_scr[slot, h, :, c])
            alpha = jnp.exp2(m_old - m_new)
            m_scr[h, :, c] = m_new
            p = jnp.exp2(s_scr[slot, h, :, c] - m_new).astype(BF16)
            acc_scr[h, :, c] = alpha * acc_scr[h, :, c] + _dot(vte, p)

    def trip(i, carry):
        for u in range(ATTN_UNROLL):
            step(ATTN_UNROLL * i + u, u % 2)
        return carry

    ntrips = nk // ATTN_UNROLL
    lax.fori_loop(0, ntrips, trip, 0)
    for j in range(ntrips * ATTN_UNROLL, nk):
        step(j, j % 2)
    o_t = jnp.concatenate([acc_scr[h, :ATTN_HD] / acc_scr[h, ATTN_HD:ATTN_HD + 1] for h in range(ATTN_GROUP)],
                          axis=0)
    out = o_t.T
    o_ref[0] = (out * _silu(z_ref[0].astype(F32))).astype(o_ref.dtype)


def _attn_call(qt, k, vt, P):
    b_, _, L = qt.shape
    Lk = k.shape[2]
    tq = min(ATTN_TQ, L)
    tk = max(t for t in range(LANES, ATTN_TK_MAX + 1, LANES) if Lk % t == 0)
    gw = ATTN_GROUP * ATTN_HD
    zblk = COLS['a_z'][0] // gw
    return pl.pallas_call(
        functools.partial(_attn_kernel, tk=tk, nk=Lk // tk),
        grid=(b_, ATTN_KV_HEADS, L // tq),
        in_specs=[pl.BlockSpec((1, ATTN_GROUP * ATTN_QK_DEPTH, tq), lambda b, g, i: (b, g, i)),
                  pl.BlockSpec((1, 1, Lk, ATTN_QK_DEPTH), lambda b, g, i: (b, g, 0, 0)),
                  pl.BlockSpec((1, ATTN_HD, Lk), lambda b, g, i: (b, g, 0)),
                  pl.BlockSpec((1, tq, gw), lambda b, g, i: (b, i, zblk + g))],
        out_specs=pl.BlockSpec((1, tq, gw), lambda b, g, i: (b, i, g)),
        out_shape=jax.ShapeDtypeStruct((b_, L, ATTN_QW), BF16),
        scratch_shapes=[pltpu.VMEM((ATTN_GROUP, 1, tq), F32),
                        pltpu.VMEM((2, ATTN_GROUP, 1, tq), F32),
                        pltpu.VMEM((ATTN_GROUP, ATTN_HD + ATTN_SUM_ROWS, tq), F32),
                        pltpu.VMEM((2, ATTN_GROUP, tk, tq), F32)],
        compiler_params=_cparams("arbitrary", "arbitrary", "arbitrary"),
        name="attn",
    )(qt, k, vt, P)


def _short_conv_kernel(u_ref, up_ref, un_ref, w_ref, b_ref, v_ref, x1_ref, x2_ref):
    t = pl.program_id(1)
    nt = pl.num_programs(1)
    u = u_ref[0].astype(F32)
    T = u.shape[0]
    hr = up_ref.shape[1]
    prev_row = jnp.where(t > 0, up_ref[0, hr - 1:hr, :].astype(F32), 0.0)
    next_row = jnp.where(t < nt - 1, un_ref[0, 0:1, :].astype(F32), 0.0)
    row = lax.broadcasted_iota(jnp.int32, u.shape, 0)
    u_m1 = jnp.where(row == 0, prev_row, pltpu.roll(u, 1, 0))
    u_p1 = jnp.where(row == T - 1, next_row, pltpu.roll(u, T - 1, 0))
    w = w_ref[...]
    out = b_ref[...] + u_m1 * w[0:1] + u * w[1:2] + u_p1 * w[2:3]
    v_ref[0] = out[:, 0:HY_W]
    x1_ref[0] = out[:, HY_W:2 * HY_W]
    x2_ref[0] = out[:, 2 * HY_W:3 * HY_W]


def _short_conv_call(P, w, bias):
    b_, L, _ = P.shape
    T = min(1024, L)
    cw = (HY_ORDER + 1) * HY_W
    blk = COLS['y_u'][0] // cw
    hr = 16
    hb = T // hr
    nh = L // hr
    out_spec = pl.BlockSpec((1, T, HY_W), lambda b, t: (b, t, 0))
    shp = jax.ShapeDtypeStruct((b_, L, HY_W), F32)
    return pl.pallas_call(
        _short_conv_kernel,
        grid=(b_, L // T),
        in_specs=[pl.BlockSpec((1, T, cw), lambda b, t: (b, t, blk)),
                  pl.BlockSpec((1, hr, cw), lambda b, t: (b, jnp.maximum(t * hb - 1, 0), blk)),
                  pl.BlockSpec((1, hr, cw), lambda b, t: (b, jnp.minimum((t + 1) * hb, nh - 1), blk)),
                  pl.BlockSpec((HY_SHORT, cw), lambda b, t: (0, 0)),
                  pl.BlockSpec((1, cw), lambda b, t: (0, 0))],
        out_specs=[out_spec, out_spec, out_spec],
        out_shape=[shp, shp, shp],
        compiler_params=_cparams("arbitrary", "arbitrary"),
        name="short_conv",
    )(P, P, P, w, bias)


FFT_ROWS = 8


def _filter_kernel(emb_ref, w1_ref, b1_ref, f1_ref, w2_ref, b2_ref, f2_ref, w3_ref, dl_ref,
                   hf_ref, hb_ref, sum_ref):
    t = pl.program_id(0)
    emb = emb_ref[...]
    h = jnp.sin(f1_ref[...] * (_dot_hp(emb, w1_ref[...]) + b1_ref[...]))
    h = jnp.sin(f2_ref[...] * (_dot_hp(h, w2_ref[...]) + b2_ref[...]))
    hh, hl = _split(h)
    tt = emb[:, 0:1]

    @pl.when(t == 0)
    def _():
        sum_ref[...] = jnp.zeros_like(sum_ref)

    lag0 = (lax.broadcasted_iota(jnp.int32, (h.shape[0], HY_W), 0) + t * h.shape[0]) == 0
    for o in range(HY_ORDER):
        for d, out_ref in enumerate((hf_ref, hb_ref)):
            cols = slice((2 * o + d) * HY_W, (2 * o + d + 1) * HY_W)
            wh, wl = _split(w3_ref[:, cols])
            f = _dot(hh, wh) + _dot(hl, wh) + _dot(hh, wl)
            f = f * (jnp.exp(-tt * dl_ref[:, cols]) + HY_MOD_SHIFT)
            sum_ref[:, cols] += jnp.broadcast_to(jnp.sum(jnp.abs(f), axis=0, keepdims=True), (8, HY_W))
            if d == 1:
                f = jnp.where(lag0, 0.0, f)
            out_ref[o] = pltpu.einshape("(an)c->a(nc)", f, a=FFT_ROWS)


def _filter_call(emb, w1, b1, f1, w2, b2, f2, w3, deltas, n1, n2):
    L = emb.shape[0]
    TL = FFT_ROWS * n2
    const = lambda r, w: pl.BlockSpec((r, w), lambda t: (0, 0))
    hspec = pl.BlockSpec((HY_ORDER, FFT_ROWS, n2 * HY_W), lambda t: (0, t, 0))
    hshape = jax.ShapeDtypeStruct((HY_ORDER, n1 // 2, n2 * HY_W), F32)
    return pl.pallas_call(
        _filter_kernel,
        grid=(L // TL,),
        in_specs=[pl.BlockSpec((TL, LANES), lambda t: (t, 0)),
                  const(LANES, HY_FFN), const(1, HY_FFN), const(1, HY_FFN),
                  const(HY_FFN, HY_FFN), const(1, HY_FFN), const(1, HY_FFN),
                  const(HY_FFN, N_FILT), const(1, N_FILT)],
        out_specs=[hspec, hspec, pl.BlockSpec((8, N_FILT), lambda t: (0, 0))],
        out_shape=[hshape, hshape, jax.ShapeDtypeStruct((8, N_FILT), F32)],
        compiler_params=_cparams("arbitrary"),
        name="hy_filter",
    )(emb, w1, b1, f1, w2, b2, f2, w3, deltas)


def _fft_sizes(n):
    lg = int(round(math.log2(n)))
    assert 1 << lg == n
    n1 = 1 << ((lg + 1) // 2)
    return n1, n // n1


def _bf16_pair(a):
    a = np.asarray(a, np.float32)
    hi = jnp.asarray(a, F32).astype(BF16)
    lo = (jnp.asarray(a, F32) - hi.astype(F32)).astype(BF16)
    return hi, lo


def _fft_tables(n1, n2):
    n = n1 * n2
    h1 = n1 // 2
    k1 = np.arange(n1)[:, None]
    a = 2.0 * np.pi * ((k1 * np.arange(h1)[None, :]) % n1) / n1
    c, s = np.cos(a), np.sin(a)
    fa = np.block([[c, s], [-s, c]])
    fa_real = np.concatenate([c, -s], axis=0)
    fd = np.block([[c.T, -s.T], [s.T, c.T]])
    k2 = np.arange(n2)[:, None]
    b = 2.0 * np.pi * ((k2 * np.arange(n2)[None, :]) % n2) / n2
    cb, sb = np.cos(b), np.sin(b)
    fb = np.block([[cb, sb], [-sb, cb]])
    fc = np.block([[cb, -sb], [sb, cb]])
    kn = (jnp.arange(n1, dtype=jnp.int32)[:, None] * jnp.arange(n2, dtype=jnp.int32)[None, :]) % n
    tw = kn.astype(F32) * (2.0 * math.pi / n)
    twr = jnp.broadcast_to(jnp.cos(tw)[:, :, None], (n1, n2, LANES))
    twi = jnp.broadcast_to(-jnp.sin(tw)[:, :, None], (n1, n2, LANES))
    return dict(fa=_bf16_pair(fa), fa_real=_bf16_pair(fa_real), fd=_bf16_pair(fd),
                fb=_bf16_pair(fb), fc=_bf16_pair(fc), twr=twr, twi=twi)


def _fa_kernel(x_ref, mh_ref, ml_ref, o_ref, xh_scr, *, packed, part_axis):
    if packed:
        @pl.when(pl.program_id(part_axis) == 0)
        def _():
            x = x_ref[...]
            x = x.reshape(x.shape[0] * x.shape[1], x.shape[2], x.shape[3])
            xh_scr[...] = pltpu.einshape("rnc->r(nc)", x).astype(BF16)

        xh = xh_scr[...]
    else:
        xh = x_ref[0].astype(BF16)
    r = _dot(mh_ref[...], xh) + _dot(ml_ref[...], xh)
    o_ref[0] = pltpu.einshape("r(nc)->rnc", r, n=FFT_ROWS)


def _fa_call(x4, mats, n1):
    mh, ml = mats
    _, h1, n2, wd = x4.shape
    wt = FFT_ROWS * wd
    return pl.pallas_call(
        functools.partial(_fa_kernel, packed=True, part_axis=1),
        grid=(n2 // FFT_ROWS, 2),
        in_specs=[pl.BlockSpec((2, h1, FFT_ROWS, wd), lambda j, p: (0, 0, j, 0)),
                  pl.BlockSpec((n1, n1), lambda j, p: (p, 0)),
                  pl.BlockSpec((n1, n1), lambda j, p: (p, 0))],
        out_specs=pl.BlockSpec((1, n1, FFT_ROWS, wd), lambda j, p: (p, 0, j, 0)),
        out_shape=jax.ShapeDtypeStruct((2, n1, n2, wd), F32),
        scratch_shapes=[pltpu.VMEM((n1, wt), BF16)],
        compiler_params=_cparams("arbitrary", "arbitrary"),
        name="fft_a",
    )(x4, mh, ml)


def _fa_real_call(x, mats, n1):
    mh, ml = mats
    ng, h1, lanes = x.shape
    wt = FFT_ROWS * HY_W
    n2 = lanes // HY_W
    out = pl.pallas_call(
        functools.partial(_fa_kernel, packed=False, part_axis=2),
        grid=(ng, lanes // wt, 2),
        in_specs=[pl.BlockSpec((1, h1, wt), lambda g, j, p: (g, 0, j)),
                  pl.BlockSpec((n1, h1), lambda g, j, p: (p, 0)),
                  pl.BlockSpec((n1, h1), lambda g, j, p: (p, 0))],
        out_specs=pl.BlockSpec((1, n1, FFT_ROWS, HY_W), lambda g, j, p: (2 * g + p, 0, j, 0)),
        out_shape=jax.ShapeDtypeStruct((2 * ng, n1, n2, HY_W), F32),
        scratch_shapes=[pltpu.VMEM((h1, wt), BF16)],
        compiler_params=_cparams("arbitrary", "arbitrary", "arbitrary"),
        name="fft_a_real",
    )(x, mh, ml)
    return out.reshape(ng, 2, n1, n2, HY_W)


def _lane_tile(a, width):
    return jnp.concatenate([a] * (width // a.shape[-1]), axis=-1)


def _fb_filter_kernel(tf_ref, tb_ref, twr_ref, twi_ref, fh_ref, fl_ref, sf_ref, sb_ref, g_ref, *, n2, scale):
    w = g_ref.shape[-1]
    inv = jnp.concatenate([scale / sf_ref[0:1, :], scale / sb_ref[0:1, :]], axis=1)

    def body(pair, carry):
        ks = (2 * pair, 2 * pair + 1)
        ps = []
        for k in ks:
            twr = _lane_tile(twr_ref[k], 2 * w)
            twi = _lane_tile(twi_ref[k], 2 * w)
            tr = jnp.concatenate([tf_ref[0, 0, k], tb_ref[0, 0, k]], axis=1)
            ti = jnp.concatenate([tf_ref[0, 1, k], tb_ref[0, 1, k]], axis=1)
            ps.append(jnp.concatenate([tr * twr - ti * twi, tr * twi + ti * twr], axis=0))
        zs = [_dot_mat(fh_ref[...], fl_ref[...], p) * inv for p in ps]
        for k, z in zip(ks, zs):
            g_ref[0, 0, k] = z[:n2, :w] + z[:n2, w:]
            g_ref[0, 1, k] = z[n2:, :w] - z[n2:, w:]
        return carry

    lax.fori_loop(0, FFT_ROWS // 2, body, 0, unroll=2)


def _fb_filter_call(thf, thb, sums, tabs, n1, n2):
    fh, fl = tabs['fb']
    tspec = pl.BlockSpec((1, 2, FFT_ROWS, n2, HY_W), lambda g, i: (g, 0, i, 0, 0))
    twspec = pl.BlockSpec((FFT_ROWS, n2, LANES), lambda g, i: (i, 0, 0))
    sspec = lambda d: pl.BlockSpec((8, HY_W), lambda g, i: (0, 2 * g + d))
    return pl.pallas_call(
        functools.partial(_fb_filter_kernel, n2=n2, scale=1.0 / (n1 * n2)),
        grid=(HY_ORDER, n1 // FFT_ROWS),
        in_specs=[tspec, tspec, twspec, twspec,
                  pl.BlockSpec(fh.shape, lambda g, i: (0, 0)), pl.BlockSpec(fl.shape, lambda g, i: (0, 0)),
                  sspec(0), sspec(1)],
        out_specs=pl.BlockSpec((1, 2, FFT_ROWS, n2, HY_W), lambda g, i: (g, 0, i, 0, 0)),
        out_shape=jax.ShapeDtypeStruct((HY_ORDER, 2, n1, n2, HY_W), F32),
        compiler_params=_cparams("arbitrary", "arbitrary"),
        name="fft_b_filter",
    )(thf, thb, tabs['twr'], tabs['twi'], fh, fl, sums, sums)


def _fb_kernel(t_ref, twr_ref, twi_ref, fh_ref, fl_ref, ch_ref, cl_ref, g_ref, o_ref, t4_scr, *, n2):
    w = g_ref.shape[-1]

    def body(pair, carry):
        ks = (2 * pair, 2 * pair + 1)
        tw = [(_lane_tile(twr_ref[k], w), _lane_tile(twi_ref[k], w)) for k in ks]
        ps = [jnp.concatenate([t_ref[0, k] * a - t_ref[1, k] * b, t_ref[0, k] * b + t_ref[1, k] * a], axis=0)
              for k, (a, b) in zip(ks, tw)]
        zs = [_dot_mat(fh_ref[...], fl_ref[...], p) for p in ps]
        ys = [jnp.concatenate([z[:n2] * g_ref[0, 0, k] - z[n2:] * g_ref[0, 1, k],
                               z[:n2] * g_ref[0, 1, k] + z[n2:] * g_ref[0, 0, k]], axis=0)
              for k, z in zip(ks, zs)]
        vs = [_dot_mat(ch_ref[...], cl_ref[...], y) for y in ys]
        for k, (a, b), v in zip(ks, tw, vs):
            t4_scr[0, k] = v[:n2] * a + v[n2:] * b
            t4_scr[1, k] = v[n2:] * a - v[:n2] * b
        return carry

    lax.fori_loop(0, FFT_ROWS // 2, body, 0, unroll=2)
    o_ref[...] = pltpu.einshape("pknc->pk(nc)", t4_scr[...])


def _fb_call(t, g, order, tabs, n1, n2):
    fh, fl = tabs['fb']
    ch, cl = tabs['fc']
    lanes = n2 * HY_W
    mat = lambda m: pl.BlockSpec(m.shape, lambda i: (0, 0))
    dat = pl.BlockSpec((2, FFT_ROWS, lanes), lambda i: (0, i, 0))
    twspec = pl.BlockSpec((FFT_ROWS, n2, LANES), lambda i: (i, 0, 0))
    return pl.pallas_call(
        functools.partial(_fb_kernel, n2=n2),
        grid=(n1 // FFT_ROWS,),
        in_specs=[pl.BlockSpec((2, FFT_ROWS, n2, HY_W), lambda i: (0, i, 0, 0)),
                  twspec, twspec, mat(fh), mat(fl), mat(ch), mat(cl),
                  pl.BlockSpec((1, 2, FFT_ROWS, n2, HY_W), lambda i: (order, 0, i, 0, 0))],
        out_specs=dat,
        out_shape=jax.ShapeDtypeStruct((2, n1, lanes), F32),
        scratch_shapes=[pltpu.VMEM((2, FFT_ROWS, n2, HY_W), F32)],
        compiler_params=_cparams("arbitrary"),
        name="fft_b",
    )(t, tabs['twr'], tabs['twi'], fh, fl, ch, cl, g)


def _fd_kernel(u_ref, mh_ref, ml_ref, z_ref, x_ref, sk_ref, o_ref):
    u = u_ref[...]
    u = u.reshape(u.shape[0] * u.shape[1], u.shape[2])
    y = _dot_mat(mh_ref[...], ml_ref[...], u)
    y = pltpu.einshape("r(nc)->rnc", y, n=FFT_ROWS)
    o_ref[0] = x_ref[0] * (y + sk_ref[...] * z_ref[0])


def _fd_call(u, mats, z4, x4, skip_row, n1):
    mh, ml = mats
    b_, h1, n2, wd = z4.shape
    wt = FFT_ROWS * wd
    dat = pl.BlockSpec((1, h1, FFT_ROWS, wd), lambda j, p: (p, 0, j, 0))
    return pl.pallas_call(
        _fd_kernel,
        grid=(n2 // FFT_ROWS, b_),
        in_specs=[pl.BlockSpec((2, n1, wt), lambda j, p: (0, 0, j)),
                  pl.BlockSpec((h1, 2 * n1), lambda j, p: (p, 0)),
                  pl.BlockSpec((h1, 2 * n1), lambda j, p: (p, 0)),
                  dat, dat, pl.BlockSpec((1, wd), lambda j, p: (0, 0))],
        out_specs=dat,
        out_shape=jax.ShapeDtypeStruct(z4.shape, F32),
        compiler_params=pltpu.CompilerParams(dimension_semantics=("arbitrary", "arbitrary"),
                                             vmem_limit_bytes=FFT_D_VMEM_LIMIT),
        name="fft_d",
    )(u, mh, ml, z4, x4, skip_row)


def _hyena_filter_spectrum(L, tabs, n1, n2, fp):
    f32 = F32
    t = jnp.linspace(0.0, 1.0, L, dtype=f32)[:, None]
    w = 2.0 * math.pi * jnp.arange(L, dtype=f32)[:, None] / L
    fr = jnp.linspace(1e-4, HY_BANDS - 1, HY_BANDS, dtype=f32)[None]
    emb = jnp.concatenate([t, jnp.cos(fr * w), -jnp.sin(fr * w)], axis=-1)
    emb = jnp.pad(emb, ((0, 0), (0, LANES - HY_EMB)))
    deltas = jnp.abs(jnp.linspace(math.log(HY_DECAY_TARGET) / HY_DECAY_SHORT_PCT,
                                  math.log(HY_DECAY_TARGET) / HY_DECAY_LONG_PCT, N_FILT, dtype=f32))[None]
    f1_w, f1_b, f1_freq, f2_w, f2_b, f2_freq, f3_w = fp
    w1 = jnp.pad(f1_w, ((0, LANES - HY_EMB), (0, 0)))
    hf, hb, sums = _filter_call(emb, w1, f1_b[None], f1_freq[None], f2_w, f2_b[None],
                                f2_freq[None], f3_w, deltas, n1, n2)
    thf = _fa_real_call(hf, tabs['fa_real'], n1)
    thb = _fa_real_call(hb, tabs['fa_real'], n1)
    return _fb_filter_call(thf, thb, sums, tabs, n1, n2)


def _hyena_conv(z, xg, g, order, skip, tabs, n1, n2):
    b_, L, wd = z.shape
    assert b_ == 2, "the two batch rows ride as the real and imaginary parts of one transform"
    z4 = z.reshape(b_, n1 // 2, n2, wd)
    x4 = xg.reshape(b_, n1 // 2, n2, wd)
    t = _fa_call(z4, tabs['fa'], n1)
    u = _fb_call(t, g, order, tabs, n1, n2)
    out = _fd_call(u, tabs['fd'], z4, x4, skip[None, :], n1)
    return out.reshape(b_, L, wd)


def _merge_kernel(yg_ref, ya_ref, yh_ref, yz_ref, mg_ref, ma_ref, mh_ref, x_ref, gate_ref,
                  wg_ref, wa_ref, wh_ref, wo_ref, fn_ref, o_ref, *, final):
    yh = (yh_ref[0] * _silu(yz_ref[0].astype(F32))).astype(BF16)
    y = (_sigmoid(mg_ref[0].astype(F32)) * _dot(yg_ref[0], wg_ref[...])
         + _sigmoid(ma_ref[0].astype(F32)) * _dot(ya_ref[0], wa_ref[...])
         + _sigmoid(mh_ref[0].astype(F32)) * _dot(yh, wh_ref[...]))
    out = _dot(y.astype(BF16), wo_ref[...])
    xn = x_ref[0] + gate_ref[0] * out
    if final:
        ms = jnp.mean(xn * xn, axis=-1, keepdims=True)
        xn = xn * lax.rsqrt(ms + EPS) * fn_ref[...]
    o_ref[0] = xn


def _merge_call(y_gla, y_attn, y_hy, P, x, gate, wg, wa, wh, wo, fnorm, final):
    b_, L, d = x.shape
    T = min(1024, L)
    row = lambda w, blk=0: pl.BlockSpec((1, T, w), lambda b, t: (b, t, blk))
    const = lambda r, w: pl.BlockSpec((r, w), lambda b, t: (0, 0))
    mblk = COLS['m'][0] // d
    return pl.pallas_call(
        functools.partial(_merge_kernel, final=final),
        grid=(b_, L // T),
        in_specs=[row(BRANCH_W), row(BRANCH_W), row(BRANCH_W), row(HY_W, COLS['y_z'][0] // HY_W),
                  row(d, mblk), row(d, mblk + 1), row(d, mblk + 2), row(d),
                  pl.BlockSpec((1, 1, d), lambda b, t: (b, 0, 0)),
                  const(BRANCH_W, d), const(BRANCH_W, d), const(BRANCH_W, d), const(d, d), const(1, d)],
        out_specs=row(d),
        out_shape=jax.ShapeDtypeStruct((b_, L, d), F32),
        compiler_params=_cparams("arbitrary", "arbitrary"),
        name="merge",
    )(y_gla, y_attn, y_hy, P, P, P, P, x, gate, wg, wa, wh, wo, fnorm)


def _pack_w_in(w):
    parts = dict(zip(('g_q', 'g_k', 'g_v', 'g_z', 'g_af', 'g_ab', 'a_q', 'a_k', 'a_v', 'a_z', 'y_u', 'y_z', 'm'),
                     jnp.split(w, [int(i) for i in np.cumsum(SPLITS)[:-1]], axis=-1)))
    parts['g_a'] = jnp.pad(jnp.concatenate([parts['g_af'], parts['g_ab']], axis=-1),
                           ((0, 0), (0, LANES - 2 * GLA_RANK)))
    parts['g_a_lo'] = jnp.zeros_like(parts['g_a'])
    order = sorted(COLS, key=lambda n: COLS[n][0])
    return jnp.concatenate([parts[n] for n in order], axis=-1).astype(BF16)


def _rope_tables(L, heads):
    t = jnp.arange(L)
    row = (t // GRID_W).astype(F32)
    colp = (t % GRID_W).astype(F32)
    half = ATTN_HD // 2
    inv = ROPE_THETA ** (-jnp.arange(0, half, 2, dtype=F32) / half)
    ang = jnp.concatenate([row[:, None] * inv, colp[:, None] * inv], axis=-1)
    cos = jnp.repeat(jnp.cos(ang), 2, axis=-1)
    sin = jnp.stack([-jnp.sin(ang), jnp.sin(ang)], axis=-1).reshape(L, ATTN_HD)
    return jnp.tile(cos, (1, heads)), jnp.tile(sin, (1, heads))


def _identity_rope(L, heads):
    return jnp.ones((L, heads * ATTN_HD), F32), jnp.zeros((L, heads * ATTN_HD), F32)


def _block_diag_ones(width):
    i = np.arange(width) // ATTN_HD
    return jnp.asarray((i[:, None] == i[None, :]).astype(np.float32)).astype(BF16)


def kernel(x, c, ctx, c_ctx, w_ada, b_ada, w_in, gla_wa_f, gla_ba_f, gla_wa_b, gla_ba_b, gla_norm,
           attn_qnorm, attn_knorm, hy_conv_w, hy_conv_b, hy_f1_w, hy_f1_b, hy_f1_freq, hy_f2_w,
           hy_f2_b, hy_f2_freq, hy_f3_w, hy_skip, w_br_gla, w_br_attn, w_br_hy, w_out, final_norm):
    b_, L, d = x.shape
    Lc = ctx.shape[1]
    depth = w_ada.shape[0]

    cos_q, sin_q = _rope_tables(L, ATTN_HEADS)
    cos_k, sin_k = cos_q[:, :ATTN_KVW], sin_q[:, :ATTN_KVW]
    cos_qc, sin_qc = _identity_rope(Lc, ATTN_HEADS)
    cos_kc, sin_kc = cos_qc[:, :ATTN_KVW], sin_qc[:, :ATTN_KVW]
    bdq, bdk = _block_diag_ones(ATTN_QW), _block_diag_ones(ATTN_KVW)
    n1, n2 = _fft_sizes(2 * L)
    n1c, n2c = _fft_sizes(2 * Lc)
    tabs = _fft_tables(n1, n2)
    tabs_c = _fft_tables(n1c, n2c)
    zero_state = jnp.zeros((b_, GLA_DV, GLA_KW), F32)

    cmat = jnp.concatenate([c, c_ctx[None], jnp.zeros((8 - b_ - 1, d), F32)], axis=0)
    mods = _ada_call(cmat, w_ada, b_ada)

    for l in range(depth):
        need_ctx = l < depth - 1
        shift, scale, gate = [m[:b_, None, :] for m in jnp.split(mods[l], 3, axis=-1)]
        shift_c, scale_c, gate_c = [jnp.broadcast_to(m[b_:b_ + 1, None, :], (b_, 1, d))
                                    for m in jnp.split(mods[l], 3, axis=-1)]
        w_packed = _pack_w_in(w_in[l])
        P = _proj_call(x, scale, shift, w_packed)
        Pc = _proj_call(ctx, scale_c, shift_c, w_packed)

        rk = GLA_RANK
        wa_f = jnp.zeros((LANES, GLA_KW), F32).at[0:rk].set(gla_wa_f[l])
        wa_b = jnp.zeros((LANES, GLA_KW), F32).at[rk:2 * rk].set(gla_wa_b[l])
        ba_f, ba_b = gla_ba_f[l][None], gla_ba_b[l][None]
        nw = gla_norm[l][None]
        oc_f, sc_f = _gla_call(Pc, wa_f, ba_f, zero_state, reverse=False)
        yc_gla, sc_b = _gla_call(Pc, wa_b, ba_b, zero_state, reverse=True, fin=(oc_f, nw))
        o_f, _ = _gla_call(P, wa_f, ba_f, sc_f, reverse=False)
        y_gla, _ = _gla_call(P, wa_b, ba_b, sc_b, reverse=True, fin=(o_f, nw))

        gq = jnp.tile(attn_qnorm[l], ATTN_HEADS)[None]
        gk = jnp.tile(attn_knorm[l], ATTN_KV_HEADS)[None]
        q_a, k_a, v_a = _qkv_prep_call(P, cos_q, sin_q, cos_k, sin_k, gq, gk, bdq, bdk)
        qc_a, kc_a, vc_a = _qkv_prep_call(Pc, cos_qc, sin_qc, cos_kc, sin_kc, gq, gk, bdq, bdk)
        y_attn = _attn_call(q_a, jnp.concatenate([k_a, kc_a], axis=2),
                            jnp.concatenate([v_a, vc_a], axis=2), P)

        fp = (hy_f1_w[l], hy_f1_b[l], hy_f1_freq[l], hy_f2_w[l], hy_f2_b[l], hy_f2_freq[l], hy_f3_w[l])
        g_spec = _hyena_filter_spectrum(L, tabs, n1, n2, fp)
        v0, x1, x2 = _short_conv_call(P, hy_conv_w[l], hy_conv_b[l][None])
        z1 = _hyena_conv(v0, x1, g_spec, 0, hy_skip[l, 0], tabs, n1, n2)
        y_hy = _hyena_conv(z1, x2, g_spec, 1, hy_skip[l, 1], tabs, n1, n2)

        wg, wa, wh, wo = (w_br_gla[l].astype(BF16), w_br_attn[l].astype(BF16),
                          w_br_hy[l].astype(BF16), w_out[l].astype(BF16))
        fn = final_norm[None]
        x_new = _merge_call(y_gla, y_attn, y_hy, P, x, gate, wg, wa, wh, wo, fn, final=not need_ctx)

        if need_ctx:
            yc_attn = _attn_call(qc_a, kc_a, vc_a, Pc)
            gc_spec = _hyena_filter_spectrum(Lc, tabs_c, n1c, n2c, fp)
            vc0, xc1, xc2 = _short_conv_call(Pc, hy_conv_w[l], hy_conv_b[l][None])
            zc1 = _hyena_conv(vc0, xc1, gc_spec, 0, hy_skip[l, 0], tabs_c, n1c, n2c)
            yc_hy = _hyena_conv(zc1, xc2, gc_spec, 1, hy_skip[l, 1], tabs_c, n1c, n2c)
            ctx = _merge_call(yc_gla, yc_attn, yc_hy, Pc, ctx, gate_c, wg, wa, wh, wo, fn, final=False)
        x = x_new

    return x
```

```python
import functools
import math

import numpy as np
import jax
import jax.numpy as jnp
from jax import lax
from jax.experimental import pallas as pl
from jax.experimental.pallas import tpu as pltpu

F32 = jnp.float32
BF16 = jnp.bfloat16
F8 = jnp.float8_e4m3fn

D_MODEL = 1024
GRID_W = 64
BRANCH_W = D_MODEL // 2
N_BRANCH = 3
EPS = 1e-6
GLA_HEADS = 4
GLA_DV = BRANCH_W // GLA_HEADS
GLA_DK = GLA_DV // 2
GLA_KW = GLA_HEADS * GLA_DK
GLA_VW = GLA_HEADS * GLA_DV
GLA_RANK = 16
GLA_GATE_NORM = 16.0
GLA_CHUNK = 64
GLA_BLOCK = 256
ATTN_HD = 64
ATTN_HEADS = BRANCH_W // ATTN_HD
ATTN_KV_HEADS = ATTN_HEADS // 4
ATTN_GROUP = ATTN_HEADS // ATTN_KV_HEADS
ATTN_QW = ATTN_HEADS * ATTN_HD
ATTN_KVW = ATTN_KV_HEADS * ATTN_HD
ROPE_THETA = 10000.0
HY_W = BRANCH_W
HY_ORDER = 2
HY_EMB = 33
HY_BANDS = (HY_EMB - 1) // 2
HY_FFN = 64
HY_SHORT = 3
HY_MOD_SHIFT = 0.05
HY_DECAY_SHORT_PCT = 0.3
HY_DECAY_LONG_PCT = 1.5
HY_DECAY_TARGET = 1e-2
N_FILT = 2 * HY_ORDER * HY_W
SPLITS = (GLA_KW, GLA_KW, GLA_VW, GLA_VW, GLA_RANK, GLA_RANK,
          ATTN_QW, ATTN_KVW, ATTN_KVW, ATTN_QW,
          (HY_ORDER + 1) * HY_W, HY_W,
          N_BRANCH * D_MODEL)

LANES = 128
VMEM_LIMIT = 48 * 1024 * 1024
FFT_D_VMEM_LIMIT = 56 * 1024 * 1024

COLS = {
    'm': (0, 3072), 'y_u': (3072, 1536), 'g_v': (4608, 512), 'g_z': (5120, 512),
    'a_q': (5632, 512), 'a_z': (6144, 512), 'y_z': (6656, 512), 'g_q': (7168, 256),
    'g_k': (7424, 256), 'a_k': (7680, 128), 'a_v': (7808, 128), 'g_a': (7936, 128),
    'g_a_lo': (8064, 128),
}
N_PROJ = 8192
PROJ_TN = 2048


def _cparams(*sem):
    return pltpu.CompilerParams(dimension_semantics=sem, vmem_limit_bytes=VMEM_LIMIT)


def _split(a):
    hi = a.astype(BF16)
    lo = (a - hi.astype(F32)).astype(BF16)
    return hi, lo


def _dot(a, b):
    return jnp.dot(a, b, preferred_element_type=F32)


def _dot_hp(a, b):
    ah, al = _split(a)
    bh, bl = _split(b)
    return _dot(ah, bh) + _dot(al, bh) + _dot(ah, bl)


def _dot_mat(mh, ml, x):
    xh = x.astype(BF16)
    return _dot(mh, xh) + _dot(ml, xh)


def _sigmoid(x):
    return 1.0 / (1.0 + jnp.exp(-x))


def _silu(x):
    return x * _sigmoid(x)


def _ada_kernel(c_ref, w_ref, b_ref, o_ref):
    o_ref[0] = _dot_hp(_silu(c_ref[...]), w_ref[0]) + b_ref[0]


def _ada_call(cmat, w_ada, b_ada):
    depth, d, n3 = w_ada.shape
    tn = 1024
    return pl.pallas_call(
        _ada_kernel,
        grid=(depth, n3 // tn),
        in_specs=[pl.BlockSpec((8, d), lambda l, j: (0, 0)),
                  pl.BlockSpec((1, d, tn), lambda l, j: (l, 0, j)),
                  pl.BlockSpec((1, 1, tn), lambda l, j: (l, 0, j))],
        out_specs=pl.BlockSpec((1, 8, tn), lambda l, j: (l, 0, j)),
        out_shape=jax.ShapeDtypeStruct((depth, 8, n3), F32),
        compiler_params=_cparams("arbitrary", "arbitrary"),
        name="ada",
    )(cmat, w_ada, b_ada.reshape(depth, 1, n3))


def _proj_kernel(x_ref, sc_ref, sh_ref, w_ref, o_ref, *, res_tile, hi_off, lo_off):
    x = x_ref[0]
    ms = jnp.mean(x * x, axis=-1, keepdims=True)
    h = x * lax.rsqrt(ms + EPS) * (1.0 + sc_ref[0]) + sh_ref[0]
    res = _dot(h.astype(BF16), w_ref[...])
    out = res.astype(o_ref.dtype)
    o_ref[0] = out

    @pl.when(pl.program_id(0) == res_tile)
    def _():
        hi = slice(hi_off, hi_off + LANES)
        o_ref[0, :, lo_off:lo_off + LANES] = (res[:, hi] - out[:, hi].astype(F32)).astype(o_ref.dtype)


def _proj_call(x, scale, shift, w):
    b_, L, d = x.shape
    n = w.shape[1]
    tm = min(1024, L)
    tn = PROJ_TN
    hi0, lo0 = COLS['g_a'][0], COLS['g_a_lo'][0]
    assert hi0 // tn == lo0 // tn
    return pl.pallas_call(
        functools.partial(_proj_kernel, res_tile=hi0 // tn, hi_off=hi0 % tn, lo_off=lo0 % tn),
        grid=(n // tn, b_, L // tm),
        in_specs=[pl.BlockSpec((1, tm, d), lambda j, b, i: (b, i, 0)),
                  pl.BlockSpec((1, 1, d), lambda j, b, i: (b, 0, 0)),
                  pl.BlockSpec((1, 1, d), lambda j, b, i: (b, 0, 0)),
                  pl.BlockSpec((d, tn), lambda j, b, i: (0, j))],
        out_specs=pl.BlockSpec((1, tm, tn), lambda j, b, i: (b, i, j)),
        out_shape=jax.ShapeDtypeStruct((b_, L, n), BF16),
        compiler_params=_cparams("arbitrary", "arbitrary", "arbitrary"),
        name="proj",
    )(x, scale, shift, w)


def _gla_kernel(*refs, reverse, finalize, nchunks):
    if finalize:
        (q_ref, k_ref, v_ref, ga_ref, gal_ref, wa_ref, ba_ref, s0_ref, tri_ref, op_ref, z_ref, nw_ref,
         o_ref, sfin_ref, s_scr) = refs
    else:
        (q_ref, k_ref, v_ref, ga_ref, gal_ref, wa_ref, ba_ref, s0_ref, tri_ref,
         o_ref, sfin_ref, s_scr) = refs
    C = GLA_CHUNK
    T = nchunks * C
    nt_dims = (((1,), (1,)), ((), ()))
    tn_dims = (((0,), (0,)), ((), ()))

    @pl.when(pl.program_id(1) == 0)
    def _():
        s_scr[...] = s0_ref[0]

    ga = ga_ref[0].astype(F32) + gal_ref[0].astype(F32)
    xg = _dot_hp(ga, wa_ref[...]) + ba_ref[...]
    g = (jnp.minimum(xg, 0.0) - jnp.log(1.0 + jnp.exp(-jnp.abs(xg)))) * (1.0 / GLA_GATE_NORM)

    tri = tri_ref[...]
    g1 = g.astype(BF16)
    r1 = g - g1.astype(F32)
    g2 = r1.astype(BF16)
    g3 = (r1 - g2.astype(F32)).astype(BF16)
    b = _dot(tri, g1) + _dot(tri, g2) + _dot(tri, g3)

    def chunk_row(r):
        return jnp.concatenate([jnp.broadcast_to(b[c * C + r:c * C + r + 1], (C, GLA_KW))
                                for c in range(nchunks)], axis=0)

    tot_row = 0 if reverse else C - 1
    bm = chunk_row(C // 2)
    bt = chunk_row(tot_row)
    ri = lax.broadcasted_iota(jnp.int32, (T, T), 0)
    ci = lax.broadcasted_iota(jnp.int32, (T, T), 1)
    same_chunk = (ri // C) == (ci // C)
    mask = same_chunk & ((ci >= ri) if reverse else (ci <= ri))
    lane = lax.broadcasted_iota(jnp.int32, (1, GLA_KW), 1)
    hmask = [(lane >= h * GLA_DK) & (lane < (h + 1) * GLA_DK) for h in range(GLA_HEADS)]

    q = q_ref[0].astype(F32) * (GLA_DK ** -0.5)
    k = k_ref[0].astype(F32)
    v = v_ref[0].astype(BF16)
    qa = q * jnp.exp(b - bm)
    kb = (k * jnp.exp(bm - b)).astype(BF16)
    qe = q * jnp.exp(b)
    kd = (k * jnp.exp(bt - b)).astype(BF16)

    intra = []
    for h in range(GLA_HEADS):
        qa_h = jnp.where(hmask[h], qa, 0.0).astype(BF16)
        att = lax.dot_general(qa_h, kb, nt_dims, preferred_element_type=F32)
        att = jnp.where(mask, att, 0.0).astype(BF16)
        intra.append(_dot(att, v[:, h * GLA_DV:(h + 1) * GLA_DV]))

    order = range(nchunks - 1, -1, -1) if reverse else range(nchunks)
    upds = {}
    for c in order:
        rows = slice(c * C, (c + 1) * C)
        upd = None
        for h in range(GLA_HEADS):
            u_h = lax.dot_general(v[rows, h * GLA_DV:(h + 1) * GLA_DV], kd[rows], tn_dims,
                                  preferred_element_type=F32)
            u_h = jnp.where(hmask[h], u_h, 0.0)
            upd = u_h if upd is None else upd + u_h
        upds[c] = upd
    inter = [None] * nchunks
    for c in order:
        rows = slice(c * C, (c + 1) * C)
        st = s_scr[...]
        st_b = st.astype(BF16)
        dec = jnp.exp(b[c * C + tot_row:c * C + tot_row + 1])
        outs = []
        for h in range(GLA_HEADS):
            qe_h = jnp.where(hmask[h], qe[rows], 0.0).astype(BF16)
            outs.append(lax.dot_general(qe_h, st_b, nt_dims, preferred_element_type=F32))
        s_scr[...] = dec * st + upds[c]
        inter[c] = jnp.concatenate(outs, axis=1)
    o = jnp.concatenate(intra, axis=1) + jnp.concatenate(inter, axis=0)

    if finalize:
        o = o + op_ref[0]
        parts = []
        for h in range(GLA_HEADS):
            oh = o[:, h * GLA_DV:(h + 1) * GLA_DV]
            ms = jnp.mean(oh * oh, axis=-1, keepdims=True)
            parts.append(oh * lax.rsqrt(ms + EPS) * nw_ref[...])
        y = jnp.concatenate(parts, axis=1) * _silu(z_ref[0].astype(F32))
        o_ref[0] = y.astype(o_ref.dtype)
    else:
        o_ref[0] = o

    @pl.when(pl.program_id(1) == pl.num_programs(1) - 1)
    def _():
        sfin_ref[0] = s_scr[...]


def _gla_call(P, wa_pad, ba, s0, reverse, fin=None):
    b_, L, _ = P.shape
    T = min(GLA_BLOCK, L)
    nt = L // T
    if reverse:
        tmap = lambda t: nt - 1 - t
    else:
        tmap = lambda t: t
    ii = np.arange(T)
    same = (ii[:, None] // GLA_CHUNK) == (ii[None, :] // GLA_CHUNK)
    tri = same & ((ii[None, :] >= ii[:, None]) if reverse else (ii[None, :] <= ii[:, None]))
    tri = jnp.asarray(tri.astype(np.float32)).astype(BF16)

    def col(name, width):
        blk = COLS[name][0] // width
        return pl.BlockSpec((1, T, width), lambda b, t: (b, tmap(t), blk))

    in_specs = [col('g_q', GLA_KW), col('g_k', GLA_KW), col('g_v', GLA_VW),
                col('g_a', LANES), col('g_a_lo', LANES),
                pl.BlockSpec((LANES, GLA_KW), lambda b, t: (0, 0)),
                pl.BlockSpec((1, GLA_KW), lambda b, t: (0, 0)),
                pl.BlockSpec((1, GLA_DV, GLA_KW), lambda b, t: (b, 0, 0)),
                pl.BlockSpec((T, T), lambda b, t: (0, 0))]
    args = [P, P, P, P, P, wa_pad, ba, s0, tri]
    if fin is not None:
        o_prev, nw = fin
        in_specs += [pl.BlockSpec((1, T, GLA_VW), lambda b, t: (b, tmap(t), 0)),
                     col('g_z', GLA_VW),
                     pl.BlockSpec((1, GLA_DV), lambda b, t: (0, 0))]
        args += [o_prev, P, nw]
    out_dtype = BF16 if fin is not None else F32
    return pl.pallas_call(
        functools.partial(_gla_kernel, reverse=reverse, finalize=fin is not None, nchunks=T // GLA_CHUNK),
        grid=(b_, nt),
        in_specs=in_specs,
        out_specs=[pl.BlockSpec((1, T, GLA_VW), lambda b, t: (b, tmap(t), 0)),
                   pl.BlockSpec((1, GLA_DV, GLA_KW), lambda b, t: (b, 0, 0))],
        out_shape=[jax.ShapeDtypeStruct((b_, L, GLA_VW), out_dtype),
                   jax.ShapeDtypeStruct((b_, GLA_DV, GLA_KW), F32)],
        scratch_shapes=[pltpu.VMEM((GLA_DV, GLA_KW), F32)],
        compiler_params=_cparams("arbitrary", "arbitrary"),
        name="gla",
    )(*args)


def _norm_rope(a, gain, bd, cos, sin):
    sq = a * a
    sh, sl = _split(sq)
    ss = _dot(sh, bd) + _dot(sl, bd)
    an = a * lax.rsqrt(ss * (1.0 / ATTN_HD) + EPS) * gain
    w = a.shape[-1]
    lane = lax.broadcasted_iota(jnp.int32, an.shape, 1)
    partner = jnp.where(lane % 2 == 0, pltpu.roll(an, w - 1, 1), pltpu.roll(an, 1, 1))
    return an * cos + partner * sin


Q_SCALE = (ATTN_HD ** -0.5) * math.log2(math.e)
ATTN_QK_DEPTH = 4 * ATTN_HD
ATTN_TK_MAX = 1280


def _split8(a):
    hi = a.astype(F8).astype(F32)
    lo = (a - hi).astype(F8)
    return hi.astype(BF16), lo.astype(BF16)


def _qkv_prep_kernel(q_ref, k_ref, v_ref, cq_ref, sq_ref, ck_ref, sk_ref, gq_ref, gk_ref,
                     bdq_ref, bdk_ref, qo_ref, ko_ref, vo_ref):
    hd = ATTN_HD
    q = _norm_rope(q_ref[0].astype(F32), gq_ref[...], bdq_ref[...], cq_ref[...], sq_ref[...]) * Q_SCALE
    qh, ql = _split8(q)
    qt = jnp.concatenate([qh, ql], axis=1).T
    for h in range(ATTN_HEADS):
        hi = qt[hd * h:hd * (h + 1)]
        lo = qt[ATTN_QW + hd * h:ATTN_QW + hd * (h + 1)]
        qo_ref[0, ATTN_QK_DEPTH * h:ATTN_QK_DEPTH * (h + 1), :] = jnp.concatenate([hi, hi, lo, lo], axis=0).astype(F8)
    k = _norm_rope(k_ref[0].astype(F32), gk_ref[...], bdk_ref[...], ck_ref[...], sk_ref[...])
    kh, kl = _split8(k)
    for g in range(ATTN_KV_HEADS):
        cols = slice(g * hd, (g + 1) * hd)
        ko_ref[0, g] = jnp.concatenate([kh[:, cols], kl[:, cols], kh[:, cols], kl[:, cols]], axis=1).astype(F8)
    vo_ref[0] = v_ref[0].astype(BF16).T


def _qkv_prep_call(P, cos_q, sin_q, cos_k, sin_k, gq, gk, bdq, bdk):
    b_, L, _ = P.shape
    T = min(1024, L)

    def col(name, width):
        blk = COLS[name][0] // width
        return pl.BlockSpec((1, T, width), lambda b, t: (b, t, blk))

    tab = lambda w: pl.BlockSpec((T, w), lambda b, t: (t, 0))
    const = lambda r, w: pl.BlockSpec((r, w), lambda b, t: (0, 0))
    return pl.pallas_call(
        _qkv_prep_kernel,
        grid=(b_, L // T),
        in_specs=[col('a_q', ATTN_QW), col('a_k', ATTN_KVW), col('a_v', ATTN_KVW),
                  tab(ATTN_QW), tab(ATTN_QW), tab(ATTN_KVW), tab(ATTN_KVW),
                  const(1, ATTN_QW), const(1, ATTN_KVW),
                  const(ATTN_QW, ATTN_QW), const(ATTN_KVW, ATTN_KVW)],
        out_specs=[pl.BlockSpec((1, ATTN_HEADS * ATTN_QK_DEPTH, T), lambda b, t: (b, 0, t)),
                   pl.BlockSpec((1, ATTN_KV_HEADS, T, ATTN_QK_DEPTH), lambda b, t: (b, 0, t, 0)),
                   pl.BlockSpec((1, ATTN_KVW, T), lambda b, t: (b, 0, t))],
        out_shape=[jax.ShapeDtypeStruct((b_, ATTN_HEADS * ATTN_QK_DEPTH, L), F8),
                   jax.ShapeDtypeStruct((b_, ATTN_KV_HEADS, L, ATTN_QK_DEPTH), F8),
                   jax.ShapeDtypeStruct((b_, ATTN_KVW, L), BF16)],
        compiler_params=_cparams("arbitrary", "arbitrary"),
        name="qkv_prep",
    )(P, P, P, cos_q, sin_q, cos_k, sin_k, gq, gk, bdq, bdk)


ATTN_SUM_ROWS = 16
ATTN_UNROLL = 2
ATTN_TQ = 512
ATTN_COLS = 256


def _attn_kernel(qt_ref, k_ref, vt_ref, z_ref, o_ref, m_scr, t_scr, acc_scr, s_scr, *, tk, nk):
    m_scr[...] = jnp.full(m_scr.shape, -jnp.inf, F32)
    acc_scr[...] = jnp.zeros(acc_scr.shape, F32)
    ones = jnp.ones((ATTN_SUM_ROWS, tk), BF16)

    tq = qt_ref.shape[2]
    width = ATTN_COLS if tq % ATTN_COLS == 0 else tq
    parts = [(h, slice(c, c + width)) for h in range(ATTN_GROUP) for c in range(0, tq, width)]

    def score_part(kj, slot, h, c):
        s = _dot(kj, qt_ref[0, h * ATTN_QK_DEPTH:(h + 1) * ATTN_QK_DEPTH, c])
        s_scr[slot, h, :, c] = s
        t_scr[slot, h, :, c] = jnp.max(s, axis=0, keepdims=True)

    def keys(j):
        start = j * tk if isinstance(j, int) else pl.multiple_of(j * tk, tk)
        return k_ref[0, 0, pl.ds(start, tk), :]

    k0 = keys(0)
    for h, c in parts:
        score_part(k0, 0, h, c)

    def step(j, slot):
        kj = keys(min(j + 1, nk - 1) if isinstance(j, int) else jnp.minimum(j + 1, nk - 1))
        start = j * tk if isinstance(j, int) else pl.multiple_of(j * tk, tk)
        vte = jnp.concatenate([vt_ref[0, :, pl.ds(start, tk)], ones], axis=0)
        for h, c in parts:
            score_part(kj, 1 - slot, h, c)
            m_old = m_scr[h, :, c]
            m_new = jnp.maximum(m_old, t_scr[slot, h, :, c])
            alpha = jnp.exp2(m_old - m_new)
            m_scr[h, :, c] = m_new
            p = jnp.exp2(s_scr[slot, h, :, c] - m_new).astype(BF16)
            acc_scr[h, :, c] = alpha * acc_scr[h, :, c] + _dot(vte, p)

    def trip(i, carry):
        for u in range(ATTN_UNROLL):
            step(ATTN_UNROLL * i + u, u % 2)
        return carry

    ntrips = nk // ATTN_UNROLL
    lax.fori_loop(0, ntrips, trip, 0)
    for j in range(ntrips * ATTN_UNROLL, nk):
        step(j, j % 2)
    o_t = jnp.concatenate([acc_scr[h, :ATTN_HD] / acc_scr[h, ATTN_HD:ATTN_HD + 1] for h in range(ATTN_GROUP)],
                          axis=0)
    out = o_t.T
    o_ref[0] = (out * _silu(z_ref[0].astype(F32))).astype(o_ref.dtype)


def _attn_call(qt, k, vt, P):
    b_, _, L = qt.shape
    Lk = k.shape[2]
    tq = min(ATTN_TQ, L)
    tk = max(t for t in range(LANES, ATTN_TK_MAX + 1, LANES) if Lk % t == 0)
    gw = ATTN_GROUP * ATTN_HD
    zblk = COLS['a_z'][0] // gw
    return pl.pallas_call(
        functools.partial(_attn_kernel, tk=tk, nk=Lk // tk),
        grid=(b_, ATTN_KV_HEADS, L // tq),
        in_specs=[pl.BlockSpec((1, ATTN_GROUP * ATTN_QK_DEPTH, tq), lambda b, g, i: (b, g, i)),
                  pl.BlockSpec((1, 1, Lk, ATTN_QK_DEPTH), lambda b, g, i: (b, g, 0, 0),
                               pipeline_mode=pl.Buffered(1)),
                  pl.BlockSpec((1, ATTN_HD, Lk), lambda b, g, i: (b, g, 0), pipeline_mode=pl.Buffered(1)),
                  pl.BlockSpec((1, tq, gw), lambda b, g, i: (b, i, zblk + g))],
        out_specs=pl.BlockSpec((1, tq, gw), lambda b, g, i: (b, i, g)),
        out_shape=jax.ShapeDtypeStruct((b_, L, ATTN_QW), BF16),
        scratch_shapes=[pltpu.VMEM((ATTN_GROUP, 1, tq), F32),
                        pltpu.VMEM((2, ATTN_GROUP, 1, tq), F32),
                        pltpu.VMEM((ATTN_GROUP, ATTN_HD + ATTN_SUM_ROWS, tq), F32),
                        pltpu.VMEM((2, ATTN_GROUP, tk, tq), F32)],
        compiler_params=_cparams("arbitrary", "arbitrary", "arbitrary"),
        name="attn",
    )(qt, k, vt, P)


def _short_conv_kernel(u_ref, up_ref, un_ref, w_ref, b_ref, v_ref, x1_ref, x2_ref):
    t = pl.program_id(1)
    nt = pl.num_programs(1)
    u = u_ref[0].astype(F32)
    T = u.shape[0]
    hr = up_ref.shape[1]
    prev_row = jnp.where(t > 0, up_ref[0, hr - 1:hr, :].astype(F32), 0.0)
    next_row = jnp.where(t < nt - 1, un_ref[0, 0:1, :].astype(F32), 0.0)
    row = lax.broadcasted_iota(jnp.int32, u.shape, 0)
    u_m1 = jnp.where(row == 0, prev_row, pltpu.roll(u, 1, 0))
    u_p1 = jnp.where(row == T - 1, next_row, pltpu.roll(u, T - 1, 0))
    w = w_ref[...]
    out = b_ref[...] + u_m1 * w[0:1] + u * w[1:2] + u_p1 * w[2:3]
    v_ref[0] = out[:, 0:HY_W]
    x1_ref[0] = out[:, HY_W:2 * HY_W]
    x2_ref[0] = out[:, 2 * HY_W:3 * HY_W]


def _short_conv_call(P, w, bias):
    b_, L, _ = P.shape
    T = min(1024, L)
    cw = (HY_ORDER + 1) * HY_W
    blk = COLS['y_u'][0] // cw
    hr = 16
    hb = T // hr
    nh = L // hr
    out_spec = pl.BlockSpec((1, T, HY_W), lambda b, t: (b, t, 0))
    shp = jax.ShapeDtypeStruct((b_, L, HY_W), F32)
    return pl.pallas_call(
        _short_conv_kernel,
        grid=(b_, L // T),
        in_specs=[pl.BlockSpec((1, T, cw), lambda b, t: (b, t, blk)),
                  pl.BlockSpec((1, hr, cw), lambda b, t: (b, jnp.maximum(t * hb - 1, 0), blk)),
                  pl.BlockSpec((1, hr, cw), lambda b, t: (b, jnp.minimum((t + 1) * hb, nh - 1), blk)),
                  pl.BlockSpec((HY_SHORT, cw), lambda b, t: (0, 0)),
                  pl.BlockSpec((1, cw), lambda b, t: (0, 0))],
        out_specs=[out_spec, out_spec, out_spec],
        out_shape=[shp, shp, shp],
        compiler_params=_cparams("arbitrary", "arbitrary"),
        name="short_conv",
    )(P, P, P, w, bias)


FFT_ROWS = 8


def _filter_kernel(emb_ref, w1_ref, b1_ref, f1_ref, w2_ref, b2_ref, f2_ref, w3_ref, dl_ref,
                   hf_ref, hb_ref, sum_ref):
    t = pl.program_id(0)
    emb = emb_ref[...]
    h = jnp.sin(f1_ref[...] * (_dot_hp(emb, w1_ref[...]) + b1_ref[...]))
    h = jnp.sin(f2_ref[...] * (_dot_hp(h, w2_ref[...]) + b2_ref[...]))
    hh, hl = _split(h)
    tt = emb[:, 0:1]

    @pl.when(t == 0)
    def _():
        sum_ref[...] = jnp.zeros_like(sum_ref)

    lag0 = (lax.broadcasted_iota(jnp.int32, (h.shape[0], HY_W), 0) + t * h.shape[0]) == 0
    for o in range(HY_ORDER):
        for d, out_ref in enumerate((hf_ref, hb_ref)):
            cols = slice((2 * o + d) * HY_W, (2 * o + d + 1) * HY_W)
            wh, wl = _split(w3_ref[:, cols])
            f = _dot(hh, wh) + _dot(hl, wh) + _dot(hh, wl)
            f = f * (jnp.exp(-tt * dl_ref[:, cols]) + HY_MOD_SHIFT)
            sum_ref[:, cols] += jnp.broadcast_to(jnp.sum(jnp.abs(f), axis=0, keepdims=True), (8, HY_W))
            if d == 1:
                f = jnp.where(lag0, 0.0, f)
            out_ref[o] = pltpu.einshape("(an)c->a(nc)", f, a=FFT_ROWS)


def _filter_call(emb, w1, b1, f1, w2, b2, f2, w3, deltas, n1, n2):
    L = emb.shape[0]
    TL = FFT_ROWS * n2
    const = lambda r, w: pl.BlockSpec((r, w), lambda t: (0, 0))
    hspec = pl.BlockSpec((HY_ORDER, FFT_ROWS, n2 * HY_W), lambda t: (0, t, 0))
    hshape = jax.ShapeDtypeStruct((HY_ORDER, n1 // 2, n2 * HY_W), F32)
    return pl.pallas_call(
        _filter_kernel,
        grid=(L // TL,),
        in_specs=[pl.BlockSpec((TL, LANES), lambda t: (t, 0)),
                  const(LANES, HY_FFN), const(1, HY_FFN), const(1, HY_FFN),
                  const(HY_FFN, HY_FFN), const(1, HY_FFN), const(1, HY_FFN),
                  const(HY_FFN, N_FILT), const(1, N_FILT)],
        out_specs=[hspec, hspec, pl.BlockSpec((8, N_FILT), lambda t: (0, 0))],
        out_shape=[hshape, hshape, jax.ShapeDtypeStruct((8, N_FILT), F32)],
        compiler_params=_cparams("arbitrary"),
        name="hy_filter",
    )(emb, w1, b1, f1, w2, b2, f2, w3, deltas)


def _fft_sizes(n):
    lg = int(round(math.log2(n)))
    assert 1 << lg == n
    n1 = 1 << ((lg + 1) // 2)
    return n1, n // n1


def _bf16_pair(a):
    a = np.asarray(a, np.float32)
    hi = jnp.asarray(a, F32).astype(BF16)
    lo = (jnp.asarray(a, F32) - hi.astype(F32)).astype(BF16)
    return hi, lo


def _fft_tables(n1, n2):
    n = n1 * n2
    h1 = n1 // 2
    k1 = np.arange(n1)[:, None]
    a = 2.0 * np.pi * ((k1 * np.arange(h1)[None, :]) % n1) / n1
    c, s = np.cos(a), np.sin(a)
    fa = np.block([[c, s], [-s, c]])
    fa_real = np.concatenate([c, -s], axis=0)
    fd = np.block([[c.T, -s.T], [s.T, c.T]])
    k2 = np.arange(n2)[:, None]
    b = 2.0 * np.pi * ((k2 * np.arange(n2)[None, :]) % n2) / n2
    cb, sb = np.cos(b), np.sin(b)
    fb = np.block([[cb, sb], [-sb, cb]])
    fc = np.block([[cb, -sb], [sb, cb]])
    kn = (jnp.arange(n1, dtype=jnp.int32)[:, None] * jnp.arange(n2, dtype=jnp.int32)[None, :]) % n
    tw = kn.astype(F32) * (2.0 * math.pi / n)
    twr = jnp.broadcast_to(jnp.cos(tw)[:, :, None], (n1, n2, LANES))
    twi = jnp.broadcast_to(-jnp.sin(tw)[:, :, None], (n1, n2, LANES))
    return dict(fa=_bf16_pair(fa), fa_real=_bf16_pair(fa_real), fd=_bf16_pair(fd),
                fb=_bf16_pair(fb), fc=_bf16_pair(fc), twr=twr, twi=twi)


def _fa_kernel(x_ref, mh_ref, ml_ref, o_ref, xh_scr, *, packed, part_axis):
    if packed:
        @pl.when(pl.program_id(part_axis) == 0)
        def _():
            x = x_ref[...]
            x = x.reshape(x.shape[0] * x.shape[1], x.shape[2], x.shape[3])
            xh_scr[...] = pltpu.einshape("rnc->r(nc)", x).astype(BF16)

        xh = xh_scr[...]
    else:
        xh = x_ref[0].astype(BF16)
    r = _dot(mh_ref[...], xh) + _dot(ml_ref[...], xh)
    o_ref[0] = pltpu.einshape("r(nc)->rnc", r, n=FFT_ROWS)


def _fa_call(x4, mats, n1):
    mh, ml = mats
    _, h1, n2, wd = x4.shape
    wt = FFT_ROWS * wd
    return pl.pallas_call(
        functools.partial(_fa_kernel, packed=True, part_axis=1),
        grid=(n2 // FFT_ROWS, 2),
        in_specs=[pl.BlockSpec((2, h1, FFT_ROWS, wd), lambda j, p: (0, 0, j, 0)),
                  pl.BlockSpec((n1, n1), lambda j, p: (p, 0)),
                  pl.BlockSpec((n1, n1), lambda j, p: (p, 0))],
        out_specs=pl.BlockSpec((1, n1, FFT_ROWS, wd), lambda j, p: (p, 0, j, 0)),
        out_shape=jax.ShapeDtypeStruct((2, n1, n2, wd), F32),
        scratch_shapes=[pltpu.VMEM((n1, wt), BF16)],
        compiler_params=_cparams("arbitrary", "arbitrary"),
        name="fft_a",
    )(x4, mh, ml)


def _fa_real_call(x, mats, n1):
    mh, ml = mats
    ng, h1, lanes = x.shape
    wt = FFT_ROWS * HY_W
    n2 = lanes // HY_W
    out = pl.pallas_call(
        functools.partial(_fa_kernel, packed=False, part_axis=2),
        grid=(ng, lanes // wt, 2),
        in_specs=[pl.BlockSpec((1, h1, wt), lambda g, j, p: (g, 0, j)),
                  pl.BlockSpec((n1, h1), lambda g, j, p: (p, 0)),
                  pl.BlockSpec((n1, h1), lambda g, j, p: (p, 0))],
        out_specs=pl.BlockSpec((1, n1, FFT_ROWS, HY_W), lambda g, j, p: (2 * g + p, 0, j, 0)),
        out_shape=jax.ShapeDtypeStruct((2 * ng, n1, n2, HY_W), F32),
        scratch_shapes=[pltpu.VMEM((h1, wt), BF16)],
        compiler_params=_cparams("arbitrary", "arbitrary", "arbitrary"),
        name="fft_a_real",
    )(x, mh, ml)
    return out.reshape(ng, 2, n1, n2, HY_W)


def _lane_tile(a, width):
    return jnp.concatenate([a] * (width // a.shape[-1]), axis=-1)


def _fb_filter_kernel(tf_ref, tb_ref, twr_ref, twi_ref, fh_ref, fl_ref, sf_ref, sb_ref, g_ref, *, n2, scale):
    w = g_ref.shape[-1]
    inv = jnp.concatenate([scale / sf_ref[0:1, :], scale / sb_ref[0:1, :]], axis=1)

    def body(pair, carry):
        ks = (2 * pair, 2 * pair + 1)
        ps = []
        for k in ks:
            twr = _lane_tile(twr_ref[k], 2 * w)
            twi = _lane_tile(twi_ref[k], 2 * w)
            tr = jnp.concatenate([tf_ref[0, 0, k], tb_ref[0, 0, k]], axis=1)
            ti = jnp.concatenate([tf_ref[0, 1, k], tb_ref[0, 1, k]], axis=1)
            ps.append(jnp.concatenate([tr * twr - ti * twi, tr * twi + ti * twr], axis=0))
        zs = [_dot_mat(fh_ref[...], fl_ref[...], p) * inv for p in ps]
        for k, z in zip(ks, zs):
            g_ref[0, 0, k] = z[:n2, :w] + z[:n2, w:]
            g_ref[0, 1, k] = z[n2:, :w] - z[n2:, w:]
        return carry

    lax.fori_loop(0, FFT_ROWS // 2, body, 0, unroll=2)


def _fb_filter_call(thf, thb, sums, tabs, n1, n2):
    fh, fl = tabs['fb']
    tspec = pl.BlockSpec((1, 2, FFT_ROWS, n2, HY_W), lambda g, i: (g, 0, i, 0, 0))
    twspec = pl.BlockSpec((FFT_ROWS, n2, LANES), lambda g, i: (i, 0, 0))
    sspec = lambda d: pl.BlockSpec((8, HY_W), lambda g, i: (0, 2 * g + d))
    return pl.pallas_call(
        functools.partial(_fb_filter_kernel, n2=n2, scale=1.0 / (n1 * n2)),
        grid=(HY_ORDER, n1 // FFT_ROWS),
        in_specs=[tspec, tspec, twspec, twspec,
                  pl.BlockSpec(fh.shape, lambda g, i: (0, 0)), pl.BlockSpec(fl.shape, lambda g, i: (0, 0)),
                  sspec(0), sspec(1)],
        out_specs=pl.BlockSpec((1, 2, FFT_ROWS, n2, HY_W), lambda g, i: (g, 0, i, 0, 0)),
        out_shape=jax.ShapeDtypeStruct((HY_ORDER, 2, n1, n2, HY_W), F32),
        compiler_params=_cparams("arbitrary", "arbitrary"),
        name="fft_b_filter",
    )(thf, thb, tabs['twr'], tabs['twi'], fh, fl, sums, sums)


def _fb_kernel(t_ref, twr_ref, twi_ref, fh_ref, fl_ref, ch_ref, cl_ref, g_ref, o_ref, t4_scr, *, n2):
    w = g_ref.shape[-1]

    def body(pair, carry):
        ks = (2 * pair, 2 * pair + 1)
        tw = [(_lane_tile(twr_ref[k], w), _lane_tile(twi_ref[k], w)) for k in ks]
        ps = [jnp.concatenate([t_ref[0, k] * a - t_ref[1, k] * b, t_ref[0, k] * b + t_ref[1, k] * a], axis=0)
              for k, (a, b) in zip(ks, tw)]
        zs = [_dot_mat(fh_ref[...], fl_ref[...], p) for p in ps]
        ys = [jnp.concatenate([z[:n2] * g_ref[0, 0, k] - z[n2:] * g_ref[0, 1, k],
                               z[:n2] * g_ref[0, 1, k] + z[n2:] * g_ref[0, 0, k]], axis=0)
              for k, z in zip(ks, zs)]
        vs = [_dot_mat(ch_ref[...], cl_ref[...], y) for y in ys]
        for k, (a, b), v in zip(ks, tw, vs):
            t4_scr[0, k] = v[:n2] * a + v[n2:] * b
            t4_scr[1, k] = v[n2:] * a - v[:n2] * b
        return carry

    lax.fori_loop(0, FFT_ROWS // 2, body, 0, unroll=2)
    o_ref[...] = pltpu.einshape("pknc->pk(nc)", t4_scr[...])


def _fb_call(t, g, order, tabs, n1, n2):
    fh, fl = tabs['fb']
    ch, cl = tabs['fc']
    lanes = n2 * HY_W
    mat = lambda m: pl.BlockSpec(m.shape, lambda i: (0, 0))
    dat = pl.BlockSpec((2, FFT_ROWS, lanes), lambda i: (0, i, 0))
    twspec = pl.BlockSpec((FFT_ROWS, n2, LANES), lambda i: (i, 0, 0))
    return pl.pallas_call(
        functools.partial(_fb_kernel, n2=n2),
        grid=(n1 // FFT_ROWS,),
        in_specs=[pl.BlockSpec((2, FFT_ROWS, n2, HY_W), lambda i: (0, i, 0, 0)),
                  twspec, twspec, mat(fh), mat(fl), mat(ch), mat(cl),
                  pl.BlockSpec((1, 2, FFT_ROWS, n2, HY_W), lambda i: (order, 0, i, 0, 0))],
        out_specs=dat,
        out_shape=jax.ShapeDtypeStruct((2, n1, lanes), F32),
        scratch_shapes=[pltpu.VMEM((2, FFT_ROWS, n2, HY_W), F32)],
        compiler_params=_cparams("arbitrary"),
        name="fft_b",
    )(t, tabs['twr'], tabs['twi'], fh, fl, ch, cl, g)


def _fd_kernel(u_ref, mh_ref, ml_ref, z_ref, x_ref, sk_ref, o_ref):
    u = u_ref[...]
    u = u.reshape(u.shape[0] * u.shape[1], u.shape[2])
    y = _dot_mat(mh_ref[...], ml_ref[...], u)
    y = pltpu.einshape("r(nc)->rnc", y, n=FFT_ROWS)
    o_ref[0] = x_ref[0] * (y + sk_ref[...] * z_ref[0])


def _fd_call(u, mats, z4, x4, skip_row, n1):
    mh, ml = mats
    b_, h1, n2, wd = z4.shape
    wt = FFT_ROWS * wd
    dat = pl.BlockSpec((1, h1, FFT_ROWS, wd), lambda j, p: (p, 0, j, 0))
    return pl.pallas_call(
        _fd_kernel,
        grid=(n2 // FFT_ROWS, b_),
        in_specs=[pl.BlockSpec((2, n1, wt), lambda j, p: (0, 0, j)),
                  pl.BlockSpec((h1, 2 * n1), lambda j, p: (p, 0)),
                  pl.BlockSpec((h1, 2 * n1), lambda j, p: (p, 0)),
                  dat, dat, pl.BlockSpec((1, wd), lambda j, p: (0, 0))],
        out_specs=dat,
        out_shape=jax.ShapeDtypeStruct(z4.shape, F32),
        compiler_params=pltpu.CompilerParams(dimension_semantics=("arbitrary", "arbitrary"),
                                             vmem_limit_bytes=FFT_D_VMEM_LIMIT),
        name="fft_d",
    )(u, mh, ml, z4, x4, skip_row)


def _hyena_filter_spectrum(L, tabs, n1, n2, fp):
    f32 = F32
    t = jnp.linspace(0.0, 1.0, L, dtype=f32)[:, None]
    w = 2.0 * math.pi * jnp.arange(L, dtype=f32)[:, None] / L
    fr = jnp.linspace(1e-4, HY_BANDS - 1, HY_BANDS, dtype=f32)[None]
    emb = jnp.concatenate([t, jnp.cos(fr * w), -jnp.sin(fr * w)], axis=-1)
    emb = jnp.pad(emb, ((0, 0), (0, LANES - HY_EMB)))
    deltas = jnp.abs(jnp.linspace(math.log(HY_DECAY_TARGET) / HY_DECAY_SHORT_PCT,
                                  math.log(HY_DECAY_TARGET) / HY_DECAY_LONG_PCT, N_FILT, dtype=f32))[None]
    f1_w, f1_b, f1_freq, f2_w, f2_b, f2_freq, f3_w = fp
    w1 = jnp.pad(f1_w, ((0, LANES - HY_EMB), (0, 0)))
    hf, hb, sums = _filter_call(emb, w1, f1_b[None], f1_freq[None], f2_w, f2_b[None],
                                f2_freq[None], f3_w, deltas, n1, n2)
    thf = _fa_real_call(hf, tabs['fa_real'], n1)
    thb = _fa_real_call(hb, tabs['fa_real'], n1)
    return _fb_filter_call(thf, thb, sums, tabs, n1, n2)


def _hyena_conv(z, xg, g, order, skip, tabs, n1, n2):
    b_, L, wd = z.shape
    assert b_ == 2, "the two batch rows ride as the real and imaginary parts of one transform"
    z4 = z.reshape(b_, n1 // 2, n2, wd)
    x4 = xg.reshape(b_, n1 // 2, n2, wd)
    t = _fa_call(z4, tabs['fa'], n1)
    u = _fb_call(t, g, order, tabs, n1, n2)
    out = _fd_call(u, tabs['fd'], z4, x4, skip[None, :], n1)
    return out.reshape(b_, L, wd)


def _merge_kernel(yg_ref, ya_ref, yh_ref, yz_ref, mg_ref, ma_ref, mh_ref, x_ref, gate_ref,
                  wg_ref, wa_ref, wh_ref, wo_ref, fn_ref, o_ref, *, final):
    yh = (yh_ref[0] * _silu(yz_ref[0].astype(F32))).astype(BF16)
    y = (_sigmoid(mg_ref[0].astype(F32)) * _dot(yg_ref[0], wg_ref[...])
         + _sigmoid(ma_ref[0].astype(F32)) * _dot(ya_ref[0], wa_ref[...])
         + _sigmoid(mh_ref[0].astype(F32)) * _dot(yh, wh_ref[...]))
    out = _dot(y.astype(BF16), wo_ref[...])
    xn = x_ref[0] + gate_ref[0] * out
    if final:
        ms = jnp.mean(xn * xn, axis=-1, keepdims=True)
        xn = xn * lax.rsqrt(ms + EPS) * fn_ref[...]
    o_ref[0] = xn


def _merge_call(y_gla, y_attn, y_hy, P, x, gate, wg, wa, wh, wo, fnorm, final):
    b_, L, d = x.shape
    T = min(1024, L)
    row = lambda w, blk=0: pl.BlockSpec((1, T, w), lambda b, t: (b, t, blk))
    const = lambda r, w: pl.BlockSpec((r, w), lambda b, t: (0, 0))
    mblk = COLS['m'][0] // d
    return pl.pallas_call(
        functools.partial(_merge_kernel, final=final),
        grid=(b_, L // T),
        in_specs=[row(BRANCH_W), row(BRANCH_W), row(BRANCH_W), row(HY_W, COLS['y_z'][0] // HY_W),
                  row(d, mblk), row(d, mblk + 1), row(d, mblk + 2), row(d),
                  pl.BlockSpec((1, 1, d), lambda b, t: (b, 0, 0)),
                  const(BRANCH_W, d), const(BRANCH_W, d), const(BRANCH_W, d), const(d, d), const(1, d)],
        out_specs=row(d),
        out_shape=jax.ShapeDtypeStruct((b_, L, d), F32),
        compiler_params=_cparams("arbitrary", "arbitrary"),
        name="merge",
    )(y_gla, y_attn, y_hy, P, P, P, P, x, gate, wg, wa, wh, wo, fnorm)


def _pack_w_in(w):
    parts = dict(zip(('g_q', 'g_k', 'g_v', 'g_z', 'g_af', 'g_ab', 'a_q', 'a_k', 'a_v', 'a_z', 'y_u', 'y_z', 'm'),
                     jnp.split(w, [int(i) for i in np.cumsum(SPLITS)[:-1]], axis=-1)))
    parts['g_a'] = jnp.pad(jnp.concatenate([parts['g_af'], parts['g_ab']], axis=-1),
                           ((0, 0), (0, LANES - 2 * GLA_RANK)))
    parts['g_a_lo'] = jnp.zeros_like(parts['g_a'])
    order = sorted(COLS, key=lambda n: COLS[n][0])
    return jnp.concatenate([parts[n] for n in order], axis=-1).astype(BF16)


def _rope_tables(L, heads):
    t = jnp.arange(L)
    row = (t // GRID_W).astype(F32)
    colp = (t % GRID_W).astype(F32)
    half = ATTN_HD // 2
    inv = ROPE_THETA ** (-jnp.arange(0, half, 2, dtype=F32) / half)
    ang = jnp.concatenate([row[:, None] * inv, colp[:, None] * inv], axis=-1)
    cos = jnp.repeat(jnp.cos(ang), 2, axis=-1)
    sin = jnp.stack([-jnp.sin(ang), jnp.sin(ang)], axis=-1).reshape(L, ATTN_HD)
    return jnp.tile(cos, (1, heads)), jnp.tile(sin, (1, heads))


def _identity_rope(L, heads):
    return jnp.ones((L, heads * ATTN_HD), F32), jnp.zeros((L, heads * ATTN_HD), F32)


def _block_diag_ones(width):
    i = np.arange(width) // ATTN_HD
    return jnp.asarray((i[:, None] == i[None, :]).astype(np.float32)).astype(BF16)


def kernel(x, c, ctx, c_ctx, w_ada, b_ada, w_in, gla_wa_f, gla_ba_f, gla_wa_b, gla_ba_b, gla_norm,
           attn_qnorm, attn_knorm, hy_conv_w, hy_conv_b, hy_f1_w, hy_f1_b, hy_f1_freq, hy_f2_w,
           hy_f2_b, hy_f2_freq, hy_f3_w, hy_skip, w_br_gla, w_br_attn, w_br_hy, w_out, final_norm):
    b_, L, d = x.shape
    Lc = ctx.shape[1]
    depth = w_ada.shape[0]

    cos_q, sin_q = _rope_tables(L, ATTN_HEADS)
    cos_k, sin_k = cos_q[:, :ATTN_KVW], sin_q[:, :ATTN_KVW]
    cos_qc, sin_qc = _identity_rope(Lc, ATTN_HEADS)
    cos_kc, sin_kc = cos_qc[:, :ATTN_KVW], sin_qc[:, :ATTN_KVW]
    bdq, bdk = _block_diag_ones(ATTN_QW), _block_diag_ones(ATTN_KVW)
    n1, n2 = _fft_sizes(2 * L)
    n1c, n2c = _fft_sizes(2 * Lc)
    tabs = _fft_tables(n1, n2)
    tabs_c = _fft_tables(n1c, n2c)
    zero_state = jnp.zeros((b_, GLA_DV, GLA_KW), F32)

    cmat = jnp.concatenate([c, c_ctx[None], jnp.zeros((8 - b_ - 1, d), F32)], axis=0)
    mods = _ada_call(cmat, w_ada, b_ada)

    for l in range(depth):
        need_ctx = l < depth - 1
        shift, scale, gate = [m[:b_, None, :] for m in jnp.split(mods[l], 3, axis=-1)]
        shift_c, scale_c, gate_c = [jnp.broadcast_to(m[b_:b_ + 1, None, :], (b_, 1, d))
                                    for m in jnp.split(mods[l], 3, axis=-1)]
        w_packed = _pack_w_in(w_in[l])
        P = _proj_call(x, scale, shift, w_packed)
        Pc = _proj_call(ctx, scale_c, shift_c, w_packed)

        rk = GLA_RANK
        wa_f = jnp.zeros((LANES, GLA_KW), F32).at[0:rk].set(gla_wa_f[l])
        wa_b = jnp.zeros((LANES, GLA_KW), F32).at[rk:2 * rk].set(gla_wa_b[l])
        ba_f, ba_b = gla_ba_f[l][None], gla_ba_b[l][None]
        nw = gla_norm[l][None]
        oc_f, sc_f = _gla_call(Pc, wa_f, ba_f, zero_state, reverse=False)
        yc_gla, sc_b = _gla_call(Pc, wa_b, ba_b, zero_state, reverse=True, fin=(oc_f, nw))
        o_f, _ = _gla_call(P, wa_f, ba_f, sc_f, reverse=False)
        y_gla, _ = _gla_call(P, wa_b, ba_b, sc_b, reverse=True, fin=(o_f, nw))

        gq = jnp.tile(attn_qnorm[l], ATTN_HEADS)[None]
        gk = jnp.tile(attn_knorm[l], ATTN_KV_HEADS)[None]
        q_a, k_a, v_a = _qkv_prep_call(P, cos_q, sin_q, cos_k, sin_k, gq, gk, bdq, bdk)
        qc_a, kc_a, vc_a = _qkv_prep_call(Pc, cos_qc, sin_qc, cos_kc, sin_kc, gq, gk, bdq, bdk)
        y_attn = _attn_call(q_a, jnp.concatenate([k_a, kc_a], axis=2),
                            jnp.concatenate([v_a, vc_a], axis=2), P)

        fp = (hy_f1_w[l], hy_f1_b[l], hy_f1_freq[l], hy_f2_w[l], hy_f2_b[l], hy_f2_freq[l], hy_f3_w[l])
        g_spec = _hyena_filter_spectrum(L, tabs, n1, n2, fp)
        v0, x1, x2 = _short_conv_call(P, hy_conv_w[l], hy_conv_b[l][None])
        z1 = _hyena_conv(v0, x1, g_spec, 0, hy_skip[l, 0], tabs, n1, n2)
        y_hy = _hyena_conv(z1, x2, g_spec, 1, hy_skip[l, 1], tabs, n1, n2)

        wg, wa, wh, wo = (w_br_gla[l].astype(BF16), w_br_attn[l].astype(BF16),
                          w_br_hy[l].astype(BF16), w_out[l].astype(BF16))
        fn = final_norm[None]
        x_new = _merge_call(y_gla, y_attn, y_hy, P, x, gate, wg, wa, wh, wo, fn, final=not need_ctx)

        if need_ctx:
            yc_attn = _attn_call(qc_a, kc_a, vc_a, Pc)
            gc_spec = _hyena_filter_spectrum(Lc, tabs_c, n1c, n2c, fp)
            vc0, xc1, xc2 = _short_conv_call(Pc, hy_conv_w[l], hy_conv_b[l][None])
            zc1 = _hyena_conv(vc0, xc1, gc_spec, 0, hy_skip[l, 0], tabs_c, n1c, n2c)
            yc_hy = _hyena_conv(zc1, xc2, gc_spec, 1, hy_skip[l, 1], tabs_c, n1c, n2c)
            ctx = _merge_call(yc_gla, yc_attn, yc_hy, Pc, ctx, gate_c, wg, wa, wh, wo, fn, final=False)
        x = x_new

    return x
```
